```python
import math
import jax, jax.numpy as jnp
from jax import lax
import numpy as np

D_MODEL = 2048
BATCH = 8
SEQ = 4096
DEPTH = 1

CHUNK = 64
D_CONV = 1024
CONV_HEADS = 16
CONV_WIDTH = 3
D_SSM = 1024
SSM_GROUP = 16
SSM_GROUPS = D_SSM // SSM_GROUP
SSM_STATE = 64
D_MIX = D_CONV + D_SSM
N_IN = 4 * D_CONV + 2 * D_SSM
DT_MIN = 1e-3
DT_MAX = 1e-1
EPS = 1e-6

kernel_name = "hybrid_shortconv_s5_block"


def rms_norm(x, g):
    xf = x.astype(jnp.float32)
    y = xf * lax.rsqrt(jnp.mean(xf * xf, axis=-1, keepdims=True) + EPS)
    return (y * g.astype(jnp.float32)).astype(x.dtype)


def causal_dwconv(h, w, b):
    L = h.shape[1]
    hp = jnp.pad(h, ((0, 0), (CONV_WIDTH - 1, 0), (0, 0)))
    out = b[None, None, :]
    for k in range(CONV_WIDTH):
        out = out + w[k][None, None, :] * hp[:, k:k + L, :]
    return out


def _scan_combine(e1, e2):
    a1r, a1i, b1r, b1i = e1
    a2r, a2i, b2r, b2i = e2
    ar = a2r * a1r - a2i * a1i
    ai = a2r * a1i + a2i * a1r
    br = a2r * b1r - a2i * b1i + b2r
    bi = a2r * b1i + a2i * b1r + b2i
    return (ar, ai, br, bi)


def s5_branch(u, a_re, a_im, log_dt, b_re, b_im, c_re, c_im, d_skip, w_glu, b_glu):
    f32 = jnp.float32
    uf = u.astype(f32)
    Bsz, L, _ = uf.shape
    ug = uf.reshape(Bsz, L, SSM_GROUPS, SSM_GROUP)
    lr, li = a_re.astype(f32), a_im.astype(f32)
    dt = jnp.exp(log_dt.astype(f32))[:, None]
    mag = jnp.exp(lr * dt)
    lbr, lbi = mag * jnp.cos(li * dt), mag * jnp.sin(li * dt)
    nr, ni = lbr - 1.0, lbi
    den = lr * lr + li * li
    qr = (nr * lr + ni * li) / den
    qi = (ni * lr - nr * li) / den
    br_, bi_ = b_re.astype(f32), b_im.astype(f32)
    bbr = qr[..., None] * br_ - qi[..., None] * bi_
    bbi = qr[..., None] * bi_ + qi[..., None] * br_
    bu_r = jnp.einsum('blgh,gph->blgp', ug, bbr)
    bu_i = jnp.einsum('blgh,gph->blgp', ug, bbi)
    ar = jnp.broadcast_to(lbr[None, None], bu_r.shape)
    ai = jnp.broadcast_to(lbi[None, None], bu_i.shape)
    _, _, s_r, s_i = lax.associative_scan(_scan_combine, (ar, ai, bu_r, bu_i), axis=1)
    y = (jnp.einsum('ghp,blgp->blgh', c_re.astype(f32), s_r)
         - jnp.einsum('ghp,blgp->blgh', c_im.astype(f32), s_i))
    y = y.reshape(Bsz, L, D_SSM) + d_skip.astype(f32)[None, None, :] * uf
    y = jax.nn.gelu(y)
    y = y * jax.nn.sigmoid(y @ w_glu.astype(f32) + b_glu.astype(f32))
    return y.astype(u.dtype)


def _fwd_setup_inputs(seed: int = 0) -> dict:
    key = jax.random.key(seed)
    ks = jax.random.split(key, 20)
    f32 = jnp.float32
    x = jax.random.normal(ks[0], (BATCH, SEQ, D_MODEL), f32)
    norm_pre_g = 1.0 + 0.05 * jax.random.normal(ks[1], (D_MODEL,), f32)
    w_in = jax.random.normal(ks[2], (D_MODEL, N_IN), f32) * D_MODEL ** -0.5
    conv_w = jax.random.normal(ks[3], (CONV_WIDTH, D_CONV), f32) * CONV_WIDTH ** -0.5
    conv_b = 0.02 * jax.random.normal(ks[4], (D_CONV,), f32)
    n = jnp.arange(SSM_STATE, dtype=f32)[None, :]
    ssm_a_re = -0.5 + 0.01 * jax.random.normal(ks[5], (SSM_GROUPS, SSM_STATE), f32)
    ssm_a_im = math.pi * n + 0.01 * jax.random.normal(ks[6], (SSM_GROUPS, SSM_STATE), f32)
    ssm_log_dt = jax.random.uniform(ks[7], (SSM_GROUPS,), f32, math.log(DT_MIN), math.log(DT_MAX))
    bs = (2.0 * SSM_GROUP) ** -0.5
    ssm_b_re = jax.random.normal(ks[8], (SSM_GROUPS, SSM_STATE, SSM_GROUP), f32) * bs
    ssm_b_im = jax.random.normal(ks[9], (SSM_GROUPS, SSM_STATE, SSM_GROUP), f32) * bs
    cs = (2.0 * SSM_STATE) ** -0.5
    ssm_c_re = jax.random.normal(ks[10], (SSM_GROUPS, SSM_GROUP, SSM_STATE), f32) * cs
    ssm_c_im = jax.random.normal(ks[11], (SSM_GROUPS, SSM_GROUP, SSM_STATE), f32) * cs
    ssm_d = jax.random.normal(ks[12], (D_SSM,), f32)
    w_glu = jax.random.normal(ks[13], (D_SSM, D_SSM), f32) * D_SSM ** -0.5
    b_glu = 0.02 * jax.random.normal(ks[14], (D_SSM,), f32)
    w_out = jax.random.normal(ks[15], (D_MIX, D_MODEL), f32) * D_MIX ** -0.5
    norm_post_g = 1.0 + 0.05 * jax.random.normal(ks[16], (D_MODEL,), f32)
    return {"x": x, "norm_pre_g": norm_pre_g, "w_in": w_in, "conv_w": conv_w, "conv_b": conv_b,
            "ssm_a_re": ssm_a_re, "ssm_a_im": ssm_a_im, "ssm_log_dt": ssm_log_dt,
            "ssm_b_re": ssm_b_re, "ssm_b_im": ssm_b_im, "ssm_c_re": ssm_c_re, "ssm_c_im": ssm_c_im,
            "ssm_d": ssm_d, "w_glu": w_glu, "b_glu": b_glu, "w_out": w_out, "norm_post_g": norm_post_g}


def _fwd_reference(x, norm_pre_g, w_in, conv_w, conv_b, ssm_a_re, ssm_a_im, ssm_log_dt,
              ssm_b_re, ssm_b_im, ssm_c_re, ssm_c_im, ssm_d, w_glu, b_glu, w_out, norm_post_g):
    for _ in range(DEPTH):
        h = rms_norm(x, norm_pre_g)
        proj = h @ w_in
        b_gate, c_gate, v, z_conv = jnp.split(proj[..., :4 * D_CONV], 4, axis=-1)
        u, z_ssm = jnp.split(proj[..., 4 * D_CONV:], 2, axis=-1)
        y_conv = b_gate * causal_dwconv(c_gate * v, conv_w, conv_b)
        y_conv = y_conv * jax.nn.silu(z_conv)
        y_ssm = s5_branch(u, ssm_a_re, ssm_a_im, ssm_log_dt, ssm_b_re, ssm_b_im,
                          ssm_c_re, ssm_c_im, ssm_d, w_glu, b_glu)
        y_ssm = y_ssm * jax.nn.silu(z_ssm)
        mix = jnp.concatenate([y_conv, y_ssm], axis=-1)
        x = x + rms_norm(mix @ w_out, norm_post_g)
    return x


import jax as _jax
import jax.numpy as _jnp

TWIN_FORMAT = 'train_step'
FWD_PARAMS = ['x', 'norm_pre_g', 'w_in', 'conv_w', 'conv_b', 'ssm_a_re', 'ssm_a_im', 'ssm_log_dt', 'ssm_b_re', 'ssm_b_im', 'ssm_c_re', 'ssm_c_im', 'ssm_d', 'w_glu', 'b_glu', 'w_out', 'norm_post_g']
TWIN_WEIGHTS = ['norm_pre_g', 'w_in', 'conv_w', 'conv_b', 'ssm_a_re', 'ssm_a_im', 'ssm_log_dt', 'ssm_b_re', 'ssm_b_im', 'ssm_c_re', 'ssm_c_im', 'ssm_d', 'w_glu', 'b_glu', 'w_out', 'norm_post_g']
TWIN_DIFF_INPUT = 'x'
TWIN_INPUTS = ['x', 'norm_pre_g', 'w_in', 'conv_w', 'conv_b', 'ssm_a_re', 'ssm_a_im', 'ssm_log_dt', 'ssm_b_re', 'ssm_b_im', 'ssm_c_re', 'ssm_c_im', 'ssm_d', 'w_glu', 'b_glu', 'w_out', 'norm_post_g', 'loss_target', 'm_norm_pre_g', 'm_w_in', 'm_conv_w', 'm_conv_b', 'm_ssm_a_re', 'm_ssm_a_im', 'm_ssm_log_dt', 'm_ssm_b_re', 'm_ssm_b_im', 'm_ssm_c_re', 'm_ssm_c_im', 'm_ssm_d', 'm_w_glu', 'm_b_glu', 'm_w_out', 'm_norm_post_g', 'v_norm_pre_g', 'v_w_in', 'v_conv_w', 'v_conv_b', 'v_ssm_a_re', 'v_ssm_a_im', 'v_ssm_log_dt', 'v_ssm_b_re', 'v_ssm_b_im', 'v_ssm_c_re', 'v_ssm_c_im', 'v_ssm_d', 'v_w_glu', 'v_b_glu', 'v_w_out', 'v_norm_post_g']
TWIN_OUTPUTS = ['loss', 'grad_x', 'grad_norm_pre_g', 'grad_w_in', 'grad_conv_w', 'grad_conv_b', 'grad_ssm_a_re', 'grad_ssm_a_im', 'grad_ssm_log_dt', 'grad_ssm_b_re', 'grad_ssm_b_im', 'grad_ssm_c_re', 'grad_ssm_c_im', 'grad_ssm_d', 'grad_w_glu', 'grad_b_glu', 'grad_w_out', 'grad_norm_post_g', 'delta_norm_pre_g', 'delta_w_in', 'delta_conv_w', 'delta_conv_b', 'delta_ssm_a_re', 'delta_ssm_a_im', 'delta_ssm_log_dt', 'delta_ssm_b_re', 'delta_ssm_b_im', 'delta_ssm_c_re', 'delta_ssm_c_im', 'delta_ssm_d', 'delta_w_glu', 'delta_b_glu', 'delta_w_out', 'delta_norm_post_g', 'new_m_norm_pre_g', 'new_m_w_in', 'new_m_conv_w', 'new_m_conv_b', 'new_m_ssm_a_re', 'new_m_ssm_a_im', 'new_m_ssm_log_dt', 'new_m_ssm_b_re', 'new_m_ssm_b_im', 'new_m_ssm_c_re', 'new_m_ssm_c_im', 'new_m_ssm_d', 'new_m_w_glu', 'new_m_b_glu', 'new_m_w_out', 'new_m_norm_post_g', 'new_v_norm_pre_g', 'new_v_w_in', 'new_v_conv_w', 'new_v_conv_b', 'new_v_ssm_a_re', 'new_v_ssm_a_im', 'new_v_ssm_log_dt', 'new_v_ssm_b_re', 'new_v_ssm_b_im', 'new_v_ssm_c_re', 'new_v_ssm_c_im', 'new_v_ssm_d', 'new_v_w_glu', 'new_v_b_glu', 'new_v_w_out', 'new_v_norm_post_g']
TWIN_LEAF_KINDS = {'loss': 'loss', 'grad_x': 'grad_x', 'grad_norm_pre_g': 'grad_w', 'grad_w_in': 'grad_w', 'grad_conv_w': 'grad_w', 'grad_conv_b': 'grad_w', 'grad_ssm_a_re': 'grad_w', 'grad_ssm_a_im': 'grad_w', 'grad_ssm_log_dt': 'grad_w', 'grad_ssm_b_re': 'grad_w', 'grad_ssm_b_im': 'grad_w', 'grad_ssm_c_re': 'grad_w', 'grad_ssm_c_im': 'grad_w', 'grad_ssm_d': 'grad_w', 'grad_w_glu': 'grad_w', 'grad_b_glu': 'grad_w', 'grad_w_out': 'grad_w', 'grad_norm_post_g': 'grad_w', 'delta_norm_pre_g': 'delta_w', 'delta_w_in': 'delta_w', 'delta_conv_w': 'delta_w', 'delta_conv_b': 'delta_w', 'delta_ssm_a_re': 'delta_w', 'delta_ssm_a_im': 'delta_w', 'delta_ssm_log_dt': 'delta_w', 'delta_ssm_b_re': 'delta_w', 'delta_ssm_b_im': 'delta_w', 'delta_ssm_c_re': 'delta_w', 'delta_ssm_c_im': 'delta_w', 'delta_ssm_d': 'delta_w', 'delta_w_glu': 'delta_w', 'delta_b_glu': 'delta_w', 'delta_w_out': 'delta_w', 'delta_norm_post_g': 'delta_w', 'new_m_norm_pre_g': 'new_m', 'new_m_w_in': 'new_m', 'new_m_conv_w': 'new_m', 'new_m_conv_b': 'new_m', 'new_m_ssm_a_re': 'new_m', 'new_m_ssm_a_im': 'new_m', 'new_m_ssm_log_dt': 'new_m', 'new_m_ssm_b_re': 'new_m', 'new_m_ssm_b_im': 'new_m', 'new_m_ssm_c_re': 'new_m', 'new_m_ssm_c_im': 'new_m', 'new_m_ssm_d': 'new_m', 'new_m_w_glu': 'new_m', 'new_m_b_glu': 'new_m', 'new_m_w_out': 'new_m', 'new_m_norm_post_g': 'new_m', 'new_v_norm_pre_g': 'new_v', 'new_v_w_in': 'new_v', 'new_v_conv_w': 'new_v', 'new_v_conv_b': 'new_v', 'new_v_ssm_a_re': 'new_v', 'new_v_ssm_a_im': 'new_v', 'new_v_ssm_log_dt': 'new_v', 'new_v_ssm_b_re': 'new_v', 'new_v_ssm_b_im': 'new_v', 'new_v_ssm_c_re': 'new_v', 'new_v_ssm_c_im': 'new_v', 'new_v_ssm_d': 'new_v', 'new_v_w_glu': 'new_v', 'new_v_b_glu': 'new_v', 'new_v_w_out': 'new_v', 'new_v_norm_post_g': 'new_v'}


def _forward(args):
    return _fwd_reference(*[args[k] for k in FWD_PARAMS])


def _output_shape():
    def fwd():
        inp = _fwd_setup_inputs(0)
        return _fwd_reference(*[inp[k] for k in FWD_PARAMS])
    out = _jax.eval_shape(fwd)
    return out.shape, out.dtype

N_MICROBATCH = 1
ADAM_LR = 0.001
ADAM_B1 = 0.9
ADAM_B2 = 0.999
ADAM_EPS = 1e-08
ADAM_WD = 0.01
ADAM_STEP = 10
PER_EXAMPLE_BATCH_AXIS = {'x': 0, 'loss_target': 0}
SHARED_INPUTS = []
_WEIGHT_DTYPES = {'norm_pre_g': _jnp.float32, 'w_in': _jnp.float32, 'conv_w': _jnp.float32, 'conv_b': _jnp.float32, 'ssm_a_re': _jnp.float32, 'ssm_a_im': _jnp.float32, 'ssm_log_dt': _jnp.float32, 'ssm_b_re': _jnp.float32, 'ssm_b_im': _jnp.float32, 'ssm_c_re': _jnp.float32, 'ssm_c_im': _jnp.float32, 'ssm_d': _jnp.float32, 'w_glu': _jnp.float32, 'b_glu': _jnp.float32, 'w_out': _jnp.float32, 'norm_post_g': _jnp.float32}
MOMENT_SCALE = {'norm_pre_g': 2.513296e-01, 'w_in': 1.455768e-01, 'conv_w': 1.739486e-01, 'conv_b': 1.890664e-01, 'ssm_a_re': 3.337045e-03, 'ssm_a_im': 3.109423e-03, 'ssm_log_dt': 2.639121e+00, 'ssm_b_re': 2.100433e-03, 'ssm_b_im': 2.070481e-03, 'ssm_c_re': 4.165049e-03, 'ssm_c_im': 4.233144e-03, 'ssm_d': 8.037199e-02, 'w_glu': 1.936344e-02, 'b_glu': 3.431312e-02, 'w_out': 1.361078e-01, 'norm_post_g': 1.604467e+01}


def _to_microbatches(a, axis):
    t = _jnp.moveaxis(a, axis, 0)
    t = t.reshape((N_MICROBATCH, t.shape[0] // N_MICROBATCH) + t.shape[1:])
    return _jnp.moveaxis(t, 1, axis + 1)


def setup_inputs(seed: int = 0) -> dict:
    inp = _fwd_setup_inputs(seed)
    key = _jax.random.fold_in(_jax.random.key(seed), 7919)
    shape, _ = _output_shape()
    out = dict(inp)
    out["loss_target"] = _jax.random.normal(_jax.random.fold_in(key, 0), shape, _jnp.float32)
    for i, name in enumerate(TWIN_WEIGHTS):
        w = inp[name].astype(_jnp.float32)
        if MOMENT_SCALE is None:
            s = _jnp.sqrt(_jnp.mean(_jnp.square(w)) + 1e-30)
        else:
            s = MOMENT_SCALE[name]
        km, kv = _jax.random.split(_jax.random.fold_in(key, i + 1))
        out[name] = w
        out["m_" + name] = s * _jax.random.normal(km, w.shape, _jnp.float32)
        out["v_" + name] = (s * s) * _jax.random.uniform(kv, w.shape, _jnp.float32, 0.5, 1.5)
    if N_MICROBATCH > 1:
        for name, axis in PER_EXAMPLE_BATCH_AXIS.items():
            out[name] = _to_microbatches(out[name], axis)
    return {'x': out['x'], 'norm_pre_g': out['norm_pre_g'], 'w_in': out['w_in'], 'conv_w': out['conv_w'], 'conv_b': out['conv_b'], 'ssm_a_re': out['ssm_a_re'], 'ssm_a_im': out['ssm_a_im'], 'ssm_log_dt': out['ssm_log_dt'], 'ssm_b_re': out['ssm_b_re'], 'ssm_b_im': out['ssm_b_im'], 'ssm_c_re': out['ssm_c_re'], 'ssm_c_im': out['ssm_c_im'], 'ssm_d': out['ssm_d'], 'w_glu': out['w_glu'], 'b_glu': out['b_glu'], 'w_out': out['w_out'], 'norm_post_g': out['norm_post_g'], 'loss_target': out['loss_target'], 'm_norm_pre_g': out['m_norm_pre_g'], 'm_w_in': out['m_w_in'], 'm_conv_w': out['m_conv_w'], 'm_conv_b': out['m_conv_b'], 'm_ssm_a_re': out['m_ssm_a_re'], 'm_ssm_a_im': out['m_ssm_a_im'], 'm_ssm_log_dt': out['m_ssm_log_dt'], 'm_ssm_b_re': out['m_ssm_b_re'], 'm_ssm_b_im': out['m_ssm_b_im'], 'm_ssm_c_re': out['m_ssm_c_re'], 'm_ssm_c_im': out['m_ssm_c_im'], 'm_ssm_d': out['m_ssm_d'], 'm_w_glu': out['m_w_glu'], 'm_b_glu': out['m_b_glu'], 'm_w_out': out['m_w_out'], 'm_norm_post_g': out['m_norm_post_g'], 'v_norm_pre_g': out['v_norm_pre_g'], 'v_w_in': out['v_w_in'], 'v_conv_w': out['v_conv_w'], 'v_conv_b': out['v_conv_b'], 'v_ssm_a_re': out['v_ssm_a_re'], 'v_ssm_a_im': out['v_ssm_a_im'], 'v_ssm_log_dt': out['v_ssm_log_dt'], 'v_ssm_b_re': out['v_ssm_b_re'], 'v_ssm_b_im': out['v_ssm_b_im'], 'v_ssm_c_re': out['v_ssm_c_re'], 'v_ssm_c_im': out['v_ssm_c_im'], 'v_ssm_d': out['v_ssm_d'], 'v_w_glu': out['v_w_glu'], 'v_b_glu': out['v_b_glu'], 'v_w_out': out['v_w_out'], 'v_norm_post_g': out['v_norm_post_g']}


def _loss(weights, diff, rest, loss_target):
    with _jax.named_scope("forward"):
        args = {**rest, TWIN_DIFF_INPUT: diff, **{k: w.astype(_WEIGHT_DTYPES[k]) for k, w in weights.items()}}
        y = _forward(args)
    with _jax.named_scope("loss_head"):
        err = _jnp.square(y.astype(_jnp.float32) - loss_target)
        return 0.5 * _jnp.sum(_jnp.mean(err, axis=-1)) if err.ndim else 0.5 * err


def _adamw(w, g, m, v):
    m = ADAM_B1 * m + (1.0 - ADAM_B1) * g
    v = ADAM_B2 * v + (1.0 - ADAM_B2) * _jnp.square(g)
    m_hat = m / (1.0 - ADAM_B1 ** ADAM_STEP)
    v_hat = v / (1.0 - ADAM_B2 ** ADAM_STEP)
    delta = -ADAM_LR * (m_hat / (_jnp.sqrt(v_hat) + ADAM_EPS) + ADAM_WD * w)
    return delta, m, v


def reference(x, norm_pre_g, w_in, conv_w, conv_b, ssm_a_re, ssm_a_im, ssm_log_dt, ssm_b_re, ssm_b_im, ssm_c_re, ssm_c_im, ssm_d, w_glu, b_glu, w_out, norm_post_g, loss_target, m_norm_pre_g, m_w_in, m_conv_w, m_conv_b, m_ssm_a_re, m_ssm_a_im, m_ssm_log_dt, m_ssm_b_re, m_ssm_b_im, m_ssm_c_re, m_ssm_c_im, m_ssm_d, m_w_glu, m_b_glu, m_w_out, m_norm_post_g, v_norm_pre_g, v_w_in, v_conv_w, v_conv_b, v_ssm_a_re, v_ssm_a_im, v_ssm_log_dt, v_ssm_b_re, v_ssm_b_im, v_ssm_c_re, v_ssm_c_im, v_ssm_d, v_w_glu, v_b_glu, v_w_out, v_norm_post_g):
    given = dict(x=x, norm_pre_g=norm_pre_g, w_in=w_in, conv_w=conv_w, conv_b=conv_b, ssm_a_re=ssm_a_re, ssm_a_im=ssm_a_im, ssm_log_dt=ssm_log_dt, ssm_b_re=ssm_b_re, ssm_b_im=ssm_b_im, ssm_c_re=ssm_c_re, ssm_c_im=ssm_c_im, ssm_d=ssm_d, w_glu=w_glu, b_glu=b_glu, w_out=w_out, norm_post_g=norm_post_g, loss_target=loss_target, m_norm_pre_g=m_norm_pre_g, m_w_in=m_w_in, m_conv_w=m_conv_w, m_conv_b=m_conv_b, m_ssm_a_re=m_ssm_a_re, m_ssm_a_im=m_ssm_a_im, m_ssm_log_dt=m_ssm_log_dt, m_ssm_b_re=m_ssm_b_re, m_ssm_b_im=m_ssm_b_im, m_ssm_c_re=m_ssm_c_re, m_ssm_c_im=m_ssm_c_im, m_ssm_d=m_ssm_d, m_w_glu=m_w_glu, m_b_glu=m_b_glu, m_w_out=m_w_out, m_norm_post_g=m_norm_post_g, v_norm_pre_g=v_norm_pre_g, v_w_in=v_w_in, v_conv_w=v_conv_w, v_conv_b=v_conv_b, v_ssm_a_re=v_ssm_a_re, v_ssm_a_im=v_ssm_a_im, v_ssm_log_dt=v_ssm_log_dt, v_ssm_b_re=v_ssm_b_re, v_ssm_b_im=v_ssm_b_im, v_ssm_c_re=v_ssm_c_re, v_ssm_c_im=v_ssm_c_im, v_ssm_d=v_ssm_d, v_w_glu=v_w_glu, v_b_glu=v_b_glu, v_w_out=v_w_out, v_norm_post_g=v_norm_post_g)
    weights = {n: given[n] for n in TWIN_WEIGHTS}
    shared = {n: given[n] for n in SHARED_INPUTS}
    per_example = {n: given[n] for n in ['x']}
    grad_fn = _jax.value_and_grad(_loss, argnums=(0, 1))

    def one_microbatch(ex, loss_target):
        ex = dict(ex)
        diff = ex.pop(TWIN_DIFF_INPUT)
        return grad_fn(weights, diff, {**shared, **ex}, loss_target)

    if N_MICROBATCH == 1:
        loss, (grad_w, grad_x) = one_microbatch(per_example, given["loss_target"])
    else:
        def body(carry, xs):
            loss_sum, grad_sum = carry
            l_k, (gw_k, gx_k) = one_microbatch(xs[0], xs[1])
            with _jax.named_scope("update"):
                return (loss_sum + l_k, _jax.tree.map(_jnp.add, grad_sum, gw_k)), gx_k

        init = (_jnp.zeros((), _jnp.float32), _jax.tree.map(_jnp.zeros_like, weights))
        (loss, grad_w), grad_x = _jax.lax.scan(body, init, (per_example, given["loss_target"]))
    with _jax.named_scope("update"):
        delta_w, new_m, new_v = {}, {}, {}
        for n in TWIN_WEIGHTS:
            delta_w[n], new_m[n], new_v[n] = _adamw(weights[n], grad_w[n], given["m_" + n], given["v_" + n])
    return (loss, grad_x, *[grad_w[n] for n in TWIN_WEIGHTS], *[delta_w[n] for n in TWIN_WEIGHTS],
            *[new_m[n] for n in TWIN_WEIGHTS], *[new_v[n] for n in TWIN_WEIGHTS])
```

```python
import functools
import math

import jax
import jax.numpy as jnp
from jax import lax
from jax.experimental import pallas as pl
from jax.experimental.pallas import tpu as pltpu

F32 = jnp.float32
BF16 = jnp.bfloat16
MESH = pl.DeviceIdType.MESH

EPS = 1e-6
SSM_H = 16
SSM_P = 64
GROUPS_PER_SLAB = 8
SLAB = 128
SLAB_STATES = GROUPS_PER_SLAB * SSM_P
N_CHIPS = 4
N_DEV = 8

ADAM_LR = 0.001
ADAM_B1 = 0.9
ADAM_B2 = 0.999
ADAM_EPS = 1e-08
ADAM_WD = 0.01
ADAM_STEP = 10

MIB = 1024 * 1024
VMEM_CAP = 48 * MIB
SUBLANES = 8

TM_NORM = 512
TM_PROJ = 512
TM_GATE = 256
TM_OUT = 256
TM_DH = 256
T_SCAN = 256
TK_TN = 512
TM_TN = 1024
TR_ELT = 256


def _cp(vmem_bytes, **kw):
    return pltpu.CompilerParams(vmem_limit_bytes=int(min(VMEM_CAP, max(16 * MIB, vmem_bytes))), **kw)


def _my_place():
    return lax.axis_index("x"), lax.axis_index("y"), lax.axis_index("c")


def _other_chips(x, y):
    return [(1 - x, y), (x, 1 - y), (1 - x, 1 - y)]


def _silu(z):
    s = jax.nn.sigmoid(z)
    return z * s, s


def _dsilu(z, s):
    return s * (1.0 + z * (1.0 - s))


_GELU_K = math.sqrt(2.0 / math.pi)
_GELU_C = 0.044715


def _gelu(y):
    th = jnp.tanh(_GELU_K * (y + _GELU_C * y * y * y))
    return 0.5 * y * (1.0 + th), th


def _dgelu(y, th):
    return 0.5 * (1.0 + th) + 0.5 * y * (1.0 - th * th) * _GELU_K * (1.0 + 3.0 * _GELU_C * y * y)


def _cast_bf16(w, name):
    r, c = w.shape
    tr = min(TR_ELT, r)

    def body(w_ref, o_ref):
        o_ref[...] = w_ref[...].astype(BF16)

    return pl.pallas_call(
        body, name=name, grid=(r // tr,),
        in_specs=[pl.BlockSpec((tr, c), lambda i: (i, 0))],
        out_specs=pl.BlockSpec((tr, c), lambda i: (i, 0)),
        out_shape=jax.ShapeDtypeStruct((r, c), BF16),
        compiler_params=_cp(12 * tr * c),
    )(w)


def _prenorm(x, g):
    t, d = x.shape
    tm = min(TM_NORM, t)

    def body(x_ref, g_ref, h_ref):
        xv = x_ref[...]
        r = lax.rsqrt(jnp.mean(xv * xv, axis=-1, keepdims=True) + EPS)
        h_ref[...] = (xv * r * g_ref[...]).astype(BF16)

    return pl.pallas_call(
        body, name="prenorm", grid=(t // tm,),
        in_specs=[pl.BlockSpec((tm, d), lambda i: (i, 0)), pl.BlockSpec((1, d), lambda i: (0, 0))],
        out_specs=pl.BlockSpec((tm, d), lambda i: (i, 0)),
        out_shape=jax.ShapeDtypeStruct((t, d), BF16),
        compiler_params=_cp(20 * tm * d),
    )(x, g)


def _matmul_nn(a, b, name):
    t, k = a.shape
    nb, _, n = b.shape
    tm = min(TM_PROJ, t)

    def body(a_ref, b_ref, o_ref):
        o_ref[...] = jnp.dot(a_ref[...], b_ref[...], preferred_element_type=F32)

    return pl.pallas_call(
        body, name=name, grid=(nb, t // tm),
        in_specs=[pl.BlockSpec((tm, k), lambda j, i: (i, 0)), pl.BlockSpec((None, k, n), lambda j, i: (j, 0, 0))],
        out_specs=pl.BlockSpec((tm, n), lambda j, i: (i, j)),
        out_shape=jax.ShapeDtypeStruct((t, nb * n), F32),
        compiler_params=_cp(2 * (2 * tm * k + 2 * k * n + 4 * tm * n) + 4 * MIB),
    )(a, b)


def _matmul_tn(a, b, nb, name):
    t, m = a.shape
    n = b.shape[1] // nb
    tk = min(TK_TN, t)
    tma = min(TM_TN, m)

    def body(a_ref, b_ref, o_ref):
        @pl.when(pl.program_id(2) == 0)
        def _():
            o_ref[...] = jnp.zeros_like(o_ref)

        o_ref[...] += lax.dot_general(a_ref[...], b_ref[...], (((0,), (0,)), ((), ())), preferred_element_type=F32)

    return pl.pallas_call(
        body, name=name, grid=(nb, m // tma, t // tk),
        in_specs=[pl.BlockSpec((tk, tma), lambda j, i, k: (k, i)), pl.BlockSpec((tk, n), lambda j, i, k: (k, j))],
        out_specs=pl.BlockSpec((None, tma, n), lambda j, i, k: (j, i, 0)),
        out_shape=jax.ShapeDtypeStruct((nb, m, n), F32),
        compiler_params=_cp(2 * (2 * tk * tma + 2 * tk * n + 4 * tma * n) + 8 * MIB),
    )(a, b)


def _outproj(mix, w_out, x, tgt, g_post):
    t, dm = mix.shape
    d = w_out.shape[1]
    tm = min(TM_OUT, t)

    def body(mix_ref, w_ref, x_ref, t_ref, g_ref, loss_ref, dout_ref, do_ref, dmix_ref, gg_ref):
        @pl.when(pl.program_id(0) == 0)
        def _():
            loss_ref[...] = jnp.zeros_like(loss_ref)
            gg_ref[...] = jnp.zeros_like(gg_ref)

        w = w_ref[...]
        o = jnp.dot(mix_ref[...], w, preferred_element_type=F32)
        r = lax.rsqrt(jnp.mean(o * o, axis=-1, keepdims=True) + EPS)
        nh = o * r
        g = g_ref[...]
        e = x_ref[...] + nh * g - t_ref[...]
        loss_ref[...] += jnp.sum(e * e) * (0.5 / d)
        dout = e * (1.0 / d)
        dout_ref[...] = dout
        gg_ref[0:1, :] += jnp.sum(dout * nh, axis=0, keepdims=True)
        dn = dout * g
        do = r * (dn - nh * jnp.mean(dn * nh, axis=-1, keepdims=True))
        dob = do.astype(BF16)
        do_ref[...] = dob
        dmix_ref[...] = lax.dot_general(dob, w, (((1,), (1,)), ((), ())), preferred_element_type=F32)

    row = lambda i: (i, 0)
    fixed = lambda i: (0, 0)
    return pl.pallas_call(
        body, name="outproj", grid=(t // tm,),
        in_specs=[pl.BlockSpec((tm, dm), row), pl.BlockSpec((dm, d), fixed), pl.BlockSpec((tm, d), row),
                  pl.BlockSpec((tm, d), row), pl.BlockSpec((1, d), fixed)],
        out_specs=[pl.BlockSpec((SUBLANES, SLAB), fixed), pl.BlockSpec((tm, d), row), pl.BlockSpec((tm, d), row),
                   pl.BlockSpec((tm, dm), row), pl.BlockSpec((SUBLANES, d), fixed)],
        out_shape=[jax.ShapeDtypeStruct((SUBLANES, SLAB), F32), jax.ShapeDtypeStruct((t, d), F32),
                   jax.ShapeDtypeStruct((t, d), BF16), jax.ShapeDtypeStruct((t, dm), F32),
                   jax.ShapeDtypeStruct((SUBLANES, d), F32)],
        compiler_params=_cp(2 * (2 * dm * d + tm * (2 * dm + 4 * d * 3 + 2 * d + 4 * dm)) + 16 * MIB),
    )(mix, w_out, x, tgt, g_post)


def _dh_prenorm_bwd(dproj, w_in, x, dout, g_pre):
    t, d = x.shape
    nb, _, n = w_in.shape
    tm = min(TM_DH, t)

    def body(dp_ref, w_ref, x_ref, dout_ref, g_ref, dx_ref, gg_ref, acc):
        i, k = pl.program_id(0), pl.program_id(1)

        @pl.when((i == 0) & (k == 0))
        def _():
            gg_ref[...] = jnp.zeros_like(gg_ref)

        part = lax.dot_general(dp_ref[...], w_ref[...], (((1,), (1,)), ((), ())), preferred_element_type=F32)

        @pl.when(k == 0)
        def _():
            acc[...] = part

        @pl.when(k > 0)
        def _():
            acc[...] += part

        @pl.when(k == nb - 1)
        def _():
            dh = acc[...]
            xv = x_ref[...]
            r = lax.rsqrt(jnp.mean(xv * xv, axis=-1, keepdims=True) + EPS)
            xh = xv * r
            gg_ref[0:1, :] += jnp.sum(dh * xh, axis=0, keepdims=True)
            dg = dh * g_ref[...]
            dx_ref[...] = dout_ref[...] + r * (dg - xh * jnp.mean(dg * xh, axis=-1, keepdims=True))

    row = lambda i, k: (i, 0)
    fixed = lambda i, k: (0, 0)
    return pl.pallas_call(
        body, name="dh_prenorm_bwd", grid=(t // tm, nb),
        in_specs=[pl.BlockSpec((tm, n), lambda i, k: (i, k)), pl.BlockSpec((None, d, n), lambda i, k: (k, 0, 0)),
                  pl.BlockSpec((tm, d), row), pl.BlockSpec((tm, d), row), pl.BlockSpec((1, d), fixed)],
        out_specs=[pl.BlockSpec((tm, d), row), pl.BlockSpec((SUBLANES, d), fixed)],
        out_shape=[jax.ShapeDtypeStruct((t, d), F32), jax.ShapeDtypeStruct((SUBLANES, d), F32)],
        scratch_shapes=[pltpu.VMEM((tm, d), F32)],
        compiler_params=_cp(2 * (2 * tm * n + 2 * d * n + 12 * tm * d) + 4 * tm * d + 12 * MIB),
    )(dproj, w_in, x, dout, g_pre)


def _adamw(w, g, m, v, name):
    r, c = w.shape
    tr = min(TR_ELT, r)
    c1 = 1.0 - ADAM_B1 ** ADAM_STEP
    c2 = 1.0 - ADAM_B2 ** ADAM_STEP

    def body(w_ref, g_ref, m_ref, v_ref, d_ref, mo_ref, vo_ref):
        gv = g_ref[...]
        mn = ADAM_B1 * m_ref[...] + (1.0 - ADAM_B1) * gv
        vn = ADAM_B2 * v_ref[...] + (1.0 - ADAM_B2) * (gv * gv)
        d_ref[...] = -ADAM_LR * ((mn / c1) / (jnp.sqrt(vn / c2) + ADAM_EPS) + ADAM_WD * w_ref[...])
        mo_ref[...] = mn
        vo_ref[...] = vn

    spec = pl.BlockSpec((tr, c), lambda i: (i, 0))
    sds = jax.ShapeDtypeStruct((r, c), F32)
    return pl.pallas_call(
        body, name=name, grid=(r // tr,), in_specs=[spec] * 4, out_specs=[spec] * 3, out_shape=[sds] * 3,
        compiler_params=_cp(2 * 7 * 4 * tr * c + 8 * MIB),
    )(w, g, m, v)


def _adamw_small(gathered, w, m, v):
    r, c = w.shape
    c1 = 1.0 - ADAM_B1 ** ADAM_STEP
    c2 = 1.0 - ADAM_B2 ** ADAM_STEP

    def body(gat_ref, w_ref, m_ref, v_ref, g_ref, d_ref, mo_ref, vo_ref):
        gv = gat_ref[0]
        for dev in range(1, N_DEV):
            gv = gv + gat_ref[dev]
        g_ref[...] = gv
        mn = ADAM_B1 * m_ref[...] + (1.0 - ADAM_B1) * gv
        vn = ADAM_B2 * v_ref[...] + (1.0 - ADAM_B2) * (gv * gv)
        d_ref[...] = -ADAM_LR * ((mn / c1) / (jnp.sqrt(vn / c2) + ADAM_EPS) + ADAM_WD * w_ref[...])
        mo_ref[...] = mn
        vo_ref[...] = vn

    sds = jax.ShapeDtypeStruct((r, c), F32)
    return pl.pallas_call(
        body, name="adamw_small", out_shape=[sds] * 4,
        compiler_params=_cp((N_DEV + 12) * 4 * r * c + 8 * MIB),
    )(gathered, w, m, v)


def _zoh(a_re, a_im, log_dt, b_re2, b_im2, expand):
    dt = jnp.exp(log_dt)
    mag = jnp.exp(a_re * dt)
    lbr, lbi = mag * jnp.cos(a_im * dt), mag * jnp.sin(a_im * dt)
    nr, ni = lbr - 1.0, lbi
    den = a_re * a_re + a_im * a_im
    qr = (nr * a_re + ni * a_im) / den
    qi = (ni * a_re - nr * a_im) / den
    qr2 = jnp.dot(qr, expand, precision=lax.Precision.HIGHEST, preferred_element_type=F32)
    qi2 = jnp.dot(qi, expand, precision=lax.Precision.HIGHEST, preferred_element_type=F32)
    return lbr, lbi, qr2 * b_re2 - qi2 * b_im2, qr2 * b_im2 + qi2 * b_re2


def _zoh_fwd(a_re, a_im, log_dt, b_re2, b_im2, expand, power):
    g, p = a_re.shape

    def body(ar_ref, ai_ref, ld_ref, br_ref, bi_ref, e_ref, lbr_ref, lbi_ref, bbr_ref, bbi_ref, pw_ref):
        ar, ai, ld = ar_ref[...], ai_ref[...], ld_ref[...]
        lbr, lbi, bbr, bbi = _zoh(ar, ai, ld, br_ref[...], bi_ref[...], e_ref[...])
        lbr_ref[...], lbi_ref[...], bbr_ref[...], bbi_ref[...] = lbr, lbi, bbr, bbi
        dt = jnp.exp(ld) * float(power)
        mag = jnp.exp(ar * dt)
        pw_ref[0] = mag * jnp.cos(ai * dt)
        pw_ref[1] = mag * jnp.sin(ai * dt)

    gp = jax.ShapeDtypeStruct((g, p), F32)
    gph = jax.ShapeDtypeStruct(b_re2.shape, F32)
    return pl.pallas_call(
        body, name="zoh_fwd", out_shape=[gp, gp, gph, gph, jax.ShapeDtypeStruct((2, g, p), F32)],
        compiler_params=_cp(16 * MIB),
    )(a_re, a_im, log_dt, b_re2, b_im2, expand)


def _zoh_bwd(a_re, a_im, log_dt, b_re2, b_im2, expand, g_lbr, g_lbi, g_bbr, g_bbi):
    def body(ar_ref, ai_ref, ld_ref, br_ref, bi_ref, e_ref, c0, c1, c2, c3, gar, gai, gld, gbr, gbi):
        e = e_ref[...]
        _, vjp = jax.vjp(lambda a, b, c, d, f: _zoh(a, b, c, d, f, e),
                         ar_ref[...], ai_ref[...], ld_ref[...], br_ref[...], bi_ref[...])
        gar[...], gai[...], gld[...], gbr[...], gbi[...] = vjp((c0[...], c1[...], c2[...], c3[...]))

    sds = lambda a: jax.ShapeDtypeStruct(a.shape, F32)
    return pl.pallas_call(
        body, name="zoh_bwd", out_shape=[sds(a_re), sds(a_im), sds(log_dt), sds(b_re2), sds(b_im2)],
        compiler_params=_cp(16 * MIB),
    )(a_re, a_im, log_dt, b_re2, b_im2, expand, g_lbr, g_lbi, g_bbr, g_bbi)


def _blockdiag(blocks):
    nq, _, r, c = blocks.shape
    eye = jnp.eye(GROUPS_PER_SLAB, dtype=blocks.dtype)
    out = blocks[:, :, :, None, :] * eye[None, :, None, :, None]
    return out.reshape(nq, GROUPS_PER_SLAB * r, GROUPS_PER_SLAB * c)


def _blockdiag_take(dense, r, c):
    nq = dense.shape[0]
    eye = jnp.eye(GROUPS_PER_SLAB, dtype=dense.dtype)
    d6 = dense.reshape(nq, GROUPS_PER_SLAB, r, GROUPS_PER_SLAB, c)
    return jnp.sum(d6 * eye[None, :, None, :, None], axis=3)


def _scan_slab(s_ref, row0, q, lam_ref, pw_ref, car_ref, steps, reverse, prev_ref=None, prev_row0=0, glam_ref=None):
    sign = -1.0 if reverse else 1.0
    half = SLAB_STATES // SLAB
    cols = [(q * 2 * half + m, q * 2 * half + half + m, q * SLAB_STATES + m * SLAB) for m in range(half)]
    nm = len(cols)
    full = (SUBLANES, SLAB)
    lam = [(jnp.broadcast_to(lam_ref[0:1, pl.ds(cl, SLAB)], full),
            jnp.broadcast_to(sign * lam_ref[1:2, pl.ds(cl, SLAB)], full)) for (_, _, cl) in cols]

    def step_rows(jj, base):
        j = (steps - 1 - jj) if reverse else jj
        return j, pl.ds(base + j, SUBLANES, stride=steps)

    def pass1(jj, car):
        _, rows = step_rows(jj, row0)
        out = []
        for m, (cr, ci, _) in enumerate(cols):
            sr, si = car[2 * m], car[2 * m + 1]
            lr, li = lam[m]
            nr = lr * sr - li * si + s_ref[cr, rows, :]
            ni = lr * si + li * sr + s_ref[ci, rows, :]
            s_ref[cr, rows, :] = nr
            s_ref[ci, rows, :] = ni
            out += [nr, ni]
        return tuple(out)

    ends = lax.fori_loop(0, steps, pass1, tuple(jnp.zeros(full, F32) for _ in range(2 * nm)))

    entry = []
    for m, (cr, ci, cl) in enumerate(cols):
        ljr = pw_ref[0:1, pl.ds(cl, SLAB)]
        lji = sign * pw_ref[1:2, pl.ds(cl, SLAB)]
        c_r = car_ref[cr, 0:1, :]
        c_i = car_ref[ci, 0:1, :]
        rows_r, rows_i = [None] * SUBLANES, [None] * SUBLANES
        order = range(SUBLANES - 1, -1, -1) if reverse else range(SUBLANES)
        for b in order:
            rows_r[b], rows_i[b] = c_r, c_i
            e_r, e_i = ends[2 * m][b:b + 1], ends[2 * m + 1][b:b + 1]
            c_r, c_i = ljr * c_r - lji * c_i + e_r, ljr * c_i + lji * c_r + e_i
        car_ref[cr, 0:1, :] = c_r
        car_ref[ci, 0:1, :] = c_i
        entry.append((jnp.concatenate(rows_r, axis=0), jnp.concatenate(rows_i, axis=0)))

    def pass2(jj, carry):
        j, rows = step_rows(jj, row0)
        decayed, acc = carry[:2 * nm], carry[2 * nm:]
        out_d, out_a = [], []
        for m, (cr, ci, cl) in enumerate(cols):
            lr, li = lam[m]
            dr, di = decayed[2 * m], decayed[2 * m + 1]
            dr, di = lr * dr - li * di, lr * di + li * dr
            nr = s_ref[cr, rows, :] + dr
            ni = s_ref[ci, rows, :] + di
            s_ref[cr, rows, :] = nr
            s_ref[ci, rows, :] = ni
            out_d += [dr, di]
            if prev_ref is not None:
                prow = pl.ds(prev_row0 - 1 + j, SUBLANES, stride=steps)
                qr = prev_ref[cr, prow, :]
                qi = prev_ref[ci, prow, :]
                out_a += [acc[2 * m] + (nr * qr + ni * qi), acc[2 * m + 1] + (ni * qr - nr * qi)]
        return tuple(out_d) + tuple(out_a)

    n_acc = 2 * nm if prev_ref is not None else 0
    init = tuple(e for pair in entry for e in pair) + tuple(jnp.zeros(full, F32) for _ in range(n_acc))
    accs = lax.fori_loop(0, steps, pass2, init)[2 * nm:]
    if prev_ref is not None:
        for m, (_, _, cl) in enumerate(cols):
            glam_ref[0:1, pl.ds(cl, SLAB)] += jnp.sum(accs[2 * m], axis=0, keepdims=True)
            glam_ref[1:2, pl.ds(cl, SLAB)] += jnp.sum(accs[2 * m + 1], axis=0, keepdims=True)


def _put_slab(s_ref, rows, q, val):
    per = 2 * SLAB_STATES // SLAB
    for i in range(per):
        s_ref[q * per + i, rows, :] = val[:, i * SLAB:(i + 1) * SLAB]


def _get_slab(s_ref, rows, q):
    per = 2 * SLAB_STATES // SLAB
    return jnp.concatenate([s_ref[q * per + i, rows, :] for i in range(per)], axis=1)


def _ssm_fwd(proj, u_block, bq, cq, lam, pw, d_skip):
    t = proj.shape[0]
    nq, ds, w2 = bq.shape
    assert ds == SLAB and w2 == 2 * SLAB_STATES
    dssm = nq * SLAB
    width = nq * w2
    ntile = width // SLAB
    tt = min(T_SCAN, t)
    steps = tt // SUBLANES

    def body(u_ref, bq_ref, cq_ref, lam_ref, pw_ref, d_ref, y_ref, cin_ref, s_ref, car_ref):
        @pl.when(pl.program_id(0) == 0)
        def _():
            car_ref[...] = jnp.zeros_like(car_ref)

        cin_ref[...] = jnp.broadcast_to(car_ref[:, 0:1, :], cin_ref.shape)
        u = u_ref[...]
        ub = u.astype(BF16)
        everything = slice(None)
        for q in range(nq):
            _put_slab(s_ref, everything, q,
                      jnp.dot(ub[:, q * SLAB:(q + 1) * SLAB], bq_ref[q], preferred_element_type=F32))
        for q in range(nq):
            _scan_slab(s_ref, 0, q, lam_ref, pw_ref, car_ref, steps, reverse=False)
        for q in range(nq):
            sl = slice(q * SLAB, (q + 1) * SLAB)
            y_ref[:, sl] = (jnp.dot(_get_slab(s_ref, everything, q).astype(BF16), cq_ref[q],
                                    preferred_element_type=F32) + d_ref[:, sl] * u[:, sl])

    c3 = lambda i: (0, 0, 0)
    c2 = lambda i: (0, 0)
    return pl.pallas_call(
        body, name="ssm_fwd", grid=(t // tt,),
        in_specs=[pl.BlockSpec((tt, dssm), lambda i: (i, u_block)), pl.BlockSpec(bq.shape, c3),
                  pl.BlockSpec(cq.shape, c3), pl.BlockSpec(lam.shape, c2), pl.BlockSpec(pw.shape, c2),
                  pl.BlockSpec((1, dssm), c2)],
        out_specs=[pl.BlockSpec((tt, dssm), lambda i: (i, 0)),
                   pl.BlockSpec((None, ntile, SUBLANES, SLAB), lambda i: (i, 0, 0, 0))],
        out_shape=[jax.ShapeDtypeStruct((t, dssm), F32), jax.ShapeDtypeStruct((t // tt, ntile, SUBLANES, SLAB), F32)],
        scratch_shapes=[pltpu.VMEM((ntile, tt, SLAB), F32), pltpu.VMEM((ntile, SUBLANES, SLAB), F32)],
        compiler_params=_cp(2 * (8 * tt * dssm + 4 * nq * ds * w2 + 8 * steps * width) + 4 * tt * width + 16 * MIB,
                            dimension_semantics=("arbitrary",)),
    )(proj, bq, cq, lam, pw, d_skip)


def _ssm_bwd(proj, u_block, dy, cin, dproj, bq, cq, lam, pw, d_skip):
    t = proj.shape[0]
    nq, ds, w2 = bq.shape
    dssm = nq * SLAB
    width = nq * w2
    ntile = width // SLAB
    tt = min(T_SCAN, t)
    nt = t // tt
    steps = tt // SUBLANES
    halo = SUBLANES

    def body(u_ref, dy_ref, cin_ref, dp_any, bq_ref, cq_ref, lam_ref, pw_ref, d_ref,
             du_ref, gb_ref, gc_ref, glam_ref, gd_ref, s_ref, gs_ref, car_f, car_b):
        del dp_any

        @pl.when(pl.program_id(0) == 0)
        def _():
            car_b[...] = jnp.zeros_like(car_b)
            gb_ref[...] = jnp.zeros_like(gb_ref)
            gc_ref[...] = jnp.zeros_like(gc_ref)
            glam_ref[...] = jnp.zeros_like(glam_ref)
            gd_ref[...] = jnp.zeros_like(gd_ref)

        u = u_ref[...]
        ub = u.astype(BF16)
        dyv = dy_ref[...]
        dyb = dyv.astype(BF16)
        gd_ref[0:1, :] += jnp.sum(dyv * u, axis=0, keepdims=True)
        s_ref[:, 0:halo, :] = cin_ref[...]
        car_f[...] = cin_ref[...]
        data = slice(halo, halo + tt)
        everything = slice(None)
        for q in range(nq):
            _put_slab(s_ref, data, q, jnp.dot(ub[:, q * SLAB:(q + 1) * SLAB], bq_ref[q], preferred_element_type=F32))
        for q in range(nq):
            _scan_slab(s_ref, halo, q, lam_ref, pw_ref, car_f, steps, reverse=False)
        tn = (((0,), (0,)), ((), ()))
        nt_dims = (((1,), (1,)), ((), ()))
        for q in range(nq):
            sl = slice(q * SLAB, (q + 1) * SLAB)
            gc_ref[q] += lax.dot_general(dyb[:, sl], _get_slab(s_ref, data, q).astype(BF16), tn,
                                         preferred_element_type=F32)
            _put_slab(gs_ref, everything, q,
                      lax.dot_general(dyb[:, sl], cq_ref[q], nt_dims, preferred_element_type=F32))
        for q in range(nq):
            _scan_slab(gs_ref, 0, q, lam_ref, pw_ref, car_b, steps, reverse=True,
                       prev_ref=s_ref, prev_row0=halo, glam_ref=glam_ref)
        for q in range(nq):
            sl = slice(q * SLAB, (q + 1) * SLAB)
            gsb = _get_slab(gs_ref, everything, q).astype(BF16)
            du = lax.dot_general(gsb, bq_ref[q], nt_dims, preferred_element_type=F32) + dyv[:, sl] * d_ref[:, sl]
            du_ref[:, sl] = du.astype(BF16)
            gb_ref[q] += lax.dot_general(ub[:, sl], gsb, tn, preferred_element_type=F32)

    c3 = lambda i: (0, 0, 0)
    c2 = lambda i: (0, 0)
    rev = lambda i: (nt - 1 - i, 0)
    dense = jax.ShapeDtypeStruct((nq, SLAB, w2), F32)
    gp = lam.shape[1]
    return pl.pallas_call(
        body, name="ssm_bwd", grid=(nt,),
        in_specs=[pl.BlockSpec((tt, dssm), lambda i: (nt - 1 - i, u_block)), pl.BlockSpec((tt, dssm), rev),
                  pl.BlockSpec((None, ntile, SUBLANES, SLAB), lambda i: (nt - 1 - i, 0, 0, 0)),
                  pl.BlockSpec(memory_space=pl.ANY), pl.BlockSpec(bq.shape, c3), pl.BlockSpec(cq.shape, c3),
                  pl.BlockSpec(lam.shape, c2), pl.BlockSpec(pw.shape, c2), pl.BlockSpec((1, dssm), c2)],
        out_specs=[pl.BlockSpec((tt, dssm), lambda i: (nt - 1 - i, u_block)), pl.BlockSpec(dense.shape, c3),
                   pl.BlockSpec(dense.shape, c3), pl.BlockSpec((SUBLANES, gp), c2), pl.BlockSpec((SUBLANES, dssm), c2)],
        out_shape=[jax.ShapeDtypeStruct(dproj.shape, dproj.dtype), dense, dense,
                   jax.ShapeDtypeStruct((SUBLANES, gp), F32), jax.ShapeDtypeStruct((SUBLANES, dssm), F32)],
        scratch_shapes=[pltpu.VMEM((ntile, tt + halo, SLAB), F32), pltpu.VMEM((ntile, tt, SLAB), F32),
                        pltpu.VMEM((ntile, SUBLANES, SLAB), F32), pltpu.VMEM((ntile, SUBLANES, SLAB), F32)],
        input_output_aliases={3: 0},
        compiler_params=_cp(2 * (10 * tt * dssm + 4 * nq * ds * w2 + 8 * steps * width + 8 * nq * SLAB * w2)
                            + 8 * tt * width + 12 * MIB, dimension_semantics=("arbitrary",)),
    )(proj, dy, cin, dproj, bq, cq, lam, pw, d_skip)


def _gate_fwd(proj, y, conv_w, conv_b, w_glu, b_glu, dc):
    t = proj.shape[0]
    dssm = y.shape[1]
    assert dc == dssm
    tm = min(TM_GATE, t)
    halo = SUBLANES

    def body(b_ref, c_ref, v_ref, zc_ref, zs_ref, y_ref, cw_ref, cb_ref, wg_ref, bg_ref, mix_ref, cv_buf):
        @pl.when(pl.program_id(0) == 0)
        def _():
            cv_buf[0:halo, :] = jnp.zeros((halo, dc), F32)

        cv = c_ref[...] * v_ref[...]
        cv_buf[halo:, :] = cv
        conv = (cb_ref[...] + cw_ref[2:3, :] * cv + cw_ref[1:2, :] * cv_buf[halo - 1:halo - 1 + tm, :]
                + cw_ref[0:1, :] * cv_buf[halo - 2:halo - 2 + tm, :])
        sz, _ = _silu(zc_ref[...])
        mix_ref[:, 0:dc] = (b_ref[...] * conv * sz).astype(BF16)
        cv_buf[0:halo, :] = cv_buf[tm:tm + halo, :]
        ge, _ = _gelu(y_ref[...])
        gl = jnp.dot(ge.astype(BF16), wg_ref[...], preferred_element_type=F32) + bg_ref[...]
        szs, _ = _silu(zs_ref[...])
        mix_ref[:, dc:] = (ge * jax.nn.sigmoid(gl) * szs).astype(BF16)

    col = lambda j: pl.BlockSpec((tm, dc), lambda i, j=j: (i, j))
    fixed = lambda i: (0, 0)
    return pl.pallas_call(
        body, name="gate_fwd", grid=(t // tm,),
        in_specs=[col(0), col(1), col(2), col(3), col(5), pl.BlockSpec((tm, dssm), lambda i: (i, 0)),
                  pl.BlockSpec(conv_w.shape, fixed), pl.BlockSpec((1, dc), fixed),
                  pl.BlockSpec(w_glu.shape, fixed), pl.BlockSpec((1, dssm), fixed)],
        out_specs=pl.BlockSpec((tm, dc + dssm), lambda i: (i, 0)),
        out_shape=jax.ShapeDtypeStruct((t, dc + dssm), BF16),
        scratch_shapes=[pltpu.VMEM((tm + halo, dc), F32)],
        compiler_params=_cp(2 * (6 * 4 * tm * dc + 2 * tm * (dc + dssm) + 2 * dssm * dssm) + 24 * tm * dc + 8 * MIB,
                            dimension_semantics=("arbitrary",)),
    )(proj, proj, proj, proj, proj, y, conv_w, conv_b, w_glu, b_glu)


def _gate_bwd(proj, y, dmix, conv_w, conv_b, w_glu, b_glu, dc):
    t = proj.shape[0]
    dssm = y.shape[1]
    tm = min(TM_GATE, t)
    nt = t // tm
    halo = SUBLANES
    blocks_per_tile = tm // halo

    def body(b_ref, c_ref, v_ref, zc_ref, zs_ref, cp_ref, vp_ref, y_ref, dm_ref, cw_ref, cb_ref, wg_ref, bg_ref,
             dp_ref, dy_ref, gs_ref, gwg_ref, cv_buf, dc_buf):
        i = pl.program_id(0)

        @pl.when(i == 0)
        def _():
            dc_buf[tm:, :] = jnp.zeros((halo, dc), F32)
            gs_ref[...] = jnp.zeros_like(gs_ref)
            gwg_ref[...] = jnp.zeros_like(gwg_ref)

        first_tile = (i == nt - 1)
        bv, cg, vv, zc = b_ref[...], c_ref[...], v_ref[...], zc_ref[...]
        cv = cg * vv
        cv_buf[0:halo, :] = jnp.where(first_tile, 0.0, cp_ref[...] * vp_ref[...])
        cv_buf[halo:, :] = cv
        w0, w1, w2 = cw_ref[0:1, :], cw_ref[1:2, :], cw_ref[2:3, :]
        conv = (cb_ref[...] + w2 * cv + w1 * cv_buf[halo - 1:halo - 1 + tm, :]
                + w0 * cv_buf[halo - 2:halo - 2 + tm, :])
        sz, sgc = _silu(zc)
        dyc = dm_ref[:, 0:dc]
        dp_ref[:, 0:dc] = (dyc * conv * sz).astype(BF16)
        dp_ref[:, 3 * dc:4 * dc] = (dyc * bv * conv * _dsilu(zc, sgc)).astype(BF16)
        dconv = dyc * bv * sz
        dc_buf[0:tm, :] = dconv
        d1 = dc_buf[1:1 + tm, :]
        d2 = dc_buf[2:2 + tm, :]
        dcv = w2 * dconv + w1 * d1 + w0 * d2
        dp_ref[:, dc:2 * dc] = (dcv * vv).astype(BF16)
        dp_ref[:, 2 * dc:3 * dc] = (dcv * cg).astype(BF16)
        gs_ref[0:1, :] += jnp.sum(cv * d2, axis=0, keepdims=True)
        gs_ref[1:2, :] += jnp.sum(cv * d1, axis=0, keepdims=True)
        gs_ref[2:3, :] += jnp.sum(cv * dconv, axis=0, keepdims=True)
        gs_ref[3:4, :] += jnp.sum(dconv, axis=0, keepdims=True)
        dc_buf[tm:, :] = dc_buf[0:halo, :]

        yv, zs = y_ref[...], zs_ref[...]
        ge, th = _gelu(yv)
        geb = ge.astype(BF16)
        wg = wg_ref[...]
        gl = jnp.dot(geb, wg, preferred_element_type=F32) + bg_ref[...]
        sg = jax.nn.sigmoid(gl)
        szs, sgs = _silu(zs)
        dys = dm_ref[:, dc:]
        ys = ge * sg
        dp_ref[:, 4 * dc:5 * dc] = jnp.zeros((tm, dc), BF16)
        dp_ref[:, 5 * dc:] = (dys * ys * _dsilu(zs, sgs)).astype(BF16)
        d_ys = dys * szs
        dgl = d_ys * ge * sg * (1.0 - sg)
        dglb = dgl.astype(BF16)
        gs_ref[4:5, :] += jnp.sum(dgl, axis=0, keepdims=True)
        gwg_ref[...] += lax.dot_general(geb, dglb, (((0,), (0,)), ((), ())), preferred_element_type=F32)
        dge = d_ys * sg + lax.dot_general(dglb, wg, (((1,), (1,)), ((), ())), preferred_element_type=F32)
        dy_ref[...] = dge * _dgelu(yv, th)

    col = lambda j: pl.BlockSpec((tm, dc), lambda i, j=j: (nt - 1 - i, j))
    prev = lambda j: pl.BlockSpec((halo, dc), lambda i, j=j: (jnp.maximum((nt - 1 - i) * blocks_per_tile - 1, 0), j))
    rev = lambda i: (nt - 1 - i, 0)
    fixed = lambda i: (0, 0)
    return pl.pallas_call(
        body, name="gate_bwd", grid=(nt,),
        in_specs=[col(0), col(1), col(2), col(3), col(5), prev(1), prev(2), pl.BlockSpec((tm, dssm), rev),
                  pl.BlockSpec((tm, dc + dssm), rev), pl.BlockSpec(conv_w.shape, fixed), pl.BlockSpec((1, dc), fixed),
                  pl.BlockSpec(w_glu.shape, fixed), pl.BlockSpec((1, dssm), fixed)],
        out_specs=[pl.BlockSpec((tm, 6 * dc), rev), pl.BlockSpec((tm, dssm), rev),
                   pl.BlockSpec((2 * SUBLANES, dc), fixed), pl.BlockSpec((dssm, dssm), fixed)],
        out_shape=[jax.ShapeDtypeStruct((t, 6 * dc), BF16), jax.ShapeDtypeStruct((t, dssm), F32),
                   jax.ShapeDtypeStruct((2 * SUBLANES, dc), F32), jax.ShapeDtypeStruct((dssm, dssm), F32)],
        scratch_shapes=[pltpu.VMEM((tm + halo, dc), F32), pltpu.VMEM((tm + halo, dc), F32)],
        compiler_params=_cp(2 * (6 * 4 * tm * dc + 8 * tm * dc + 12 * tm * dc + 4 * tm * dc + 6 * dssm * dssm)
                            + 40 * tm * dc + 8 * MIB, dimension_semantics=("arbitrary",)),
    )(proj, proj, proj, proj, proj, proj, proj, y, dmix, conv_w, conv_b, w_glu, b_glu)


def _allgather_halves(parts, name):
    n = len(parts)
    per = 2 * (N_CHIPS - 1)

    def body(*refs):
        ins, outs = refs[:n], refs[n:2 * n]
        send, recv, local = refs[2 * n:]
        x, y, c = _my_place()
        k = 2 * x + y
        sib = (x, y, 1 - c)
        chips = _other_chips(x, y)

        def rc(t, s, src, dst, to):
            return pltpu.make_async_remote_copy(src_ref=src, dst_ref=dst, send_sem=send.at[t * per + s],
                                                recv_sem=recv.at[t * per + s], device_id=to, device_id_type=MESH)

        mine = [pltpu.make_async_copy(ins[t], outs[t].at[k], local.at[t]) for t in range(n)]
        for cp in mine:
            cp.start()
        sent = []
        for t in range(n):
            for j, (cx, cy) in enumerate(chips):
                cp = rc(t, j, ins[t].at[c], outs[t].at[k, c], (cx, cy, c))
                cp.start()
                sent.append(cp)
        for t in range(n):
            for j, (cx, cy) in enumerate(chips):
                landed = outs[t].at[2 * cx + cy, c]
                rc(t, j, landed, landed, (cx, cy, c)).wait_recv()
                cp = rc(t, N_CHIPS - 1 + j, landed, landed, sib)
                cp.start()
                sent.append(cp)
        for t in range(n):
            for j, (cx, cy) in enumerate(chips):
                other = outs[t].at[2 * cx + cy, 1 - c]
                rc(t, N_CHIPS - 1 + j, other, other, sib).wait_recv()
        for cp in sent:
            cp.wait_send()
        for cp in mine:
            cp.wait()

    any_spec = pl.BlockSpec(memory_space=pl.ANY)
    return pl.pallas_call(
        body, name=name, in_specs=[any_spec] * n, out_specs=[any_spec] * n,
        out_shape=[jax.ShapeDtypeStruct((N_CHIPS,) + p.shape, p.dtype) for p in parts],
        scratch_shapes=[pltpu.SemaphoreType.DMA((n * per,)), pltpu.SemaphoreType.DMA((n * per,)),
                        pltpu.SemaphoreType.DMA((n,))],
        compiler_params=_cp(16 * MIB),
    )(*parts)


def _allgather_flat(v, name):
    r, c = v.shape
    rels = [(dx, dy, dc) for dx in (0, 1) for dy in (0, 1) for dc in (0, 1)][1:]

    def body(v_ref, out_ref, send, recv):
        x, y, cc = _my_place()
        me = 4 * x + 2 * y + cc

        def peer(rel):
            dx, dy, dc = rel
            return (1 - x if dx else x, 1 - y if dy else y, 1 - cc if dc else cc)

        def rc(s, slot, to):
            return pltpu.make_async_remote_copy(src_ref=v_ref, dst_ref=out_ref.at[slot], send_sem=send.at[s],
                                                recv_sem=recv.at[s], device_id=to, device_id_type=MESH)

        sent = []
        for s, rel in enumerate(rels):
            cp = rc(s, me, peer(rel))
            cp.start()
            sent.append(cp)
        out_ref[me] = v_ref[...]
        for s, rel in enumerate(rels):
            px, py, pc = peer(rel)
            rc(s, 4 * px + 2 * py + pc, (px, py, pc)).wait_recv()
        for cp in sent:
            cp.wait_send()

    return pl.pallas_call(
        body, name=name, out_shape=jax.ShapeDtypeStruct((N_DEV, r, c), F32),
        in_specs=[pl.BlockSpec(memory_space=pltpu.VMEM)], out_specs=pl.BlockSpec(memory_space=pltpu.VMEM),
        scratch_shapes=[pltpu.SemaphoreType.DMA((N_DEV - 1,)), pltpu.SemaphoreType.DMA((N_DEV - 1,))],
        compiler_params=_cp((N_DEV + 2) * 4 * r * c + 8 * MIB),
    )(v)


def _rs_pair_exchange(grads):
    n = len(grads)

    def body(*refs):
        ins, outs = refs[:n], refs[n:2 * n]
        send, recv = refs[2 * n:]
        x, y, c = _my_place()
        cps = []
        for t in range(n):
            cp = pltpu.make_async_remote_copy(src_ref=ins[t].at[:, 1 - c], dst_ref=outs[t], send_sem=send.at[t],
                                              recv_sem=recv.at[t], device_id=(x, y, 1 - c), device_id_type=MESH)
            cp.start()
            cps.append(cp)
        for cp in cps:
            cp.wait()

    any_spec = pl.BlockSpec(memory_space=pl.ANY)
    return pl.pallas_call(
        body, name="rs_pair_exchange", in_specs=[any_spec] * n, out_specs=[any_spec] * n,
        out_shape=[jax.ShapeDtypeStruct((g.shape[0],) + g.shape[2:], F32) for g in grads],
        scratch_shapes=[pltpu.SemaphoreType.DMA((n,)), pltpu.SemaphoreType.DMA((n,))],
        compiler_params=_cp(16 * MIB),
    )(*grads)


def _rs_pair_add(place, grad, got, name):
    nk, _, r2, c = grad.shape
    tr = min(TR_ELT, r2)

    def body(place_ref, g_ref, r_ref, o16_ref, o32_ref):
        del place_ref
        s = g_ref[...] + r_ref[...]
        o32_ref[...] = s
        o16_ref[...] = s.astype(BF16)

    blk = pl.BlockSpec((None, tr, c), lambda k, i, p: (k, i, 0))
    return pl.pallas_call(
        body, name=name,
        grid_spec=pltpu.PrefetchScalarGridSpec(
            num_scalar_prefetch=1, grid=(nk, r2 // tr),
            in_specs=[pl.BlockSpec((None, None, tr, c), lambda k, i, p: (k, p[0], i, 0)), blk],
            out_specs=[blk, blk]),
        out_shape=[jax.ShapeDtypeStruct((nk, r2, c), BF16), jax.ShapeDtypeStruct((nk, r2, c), F32)],
        compiler_params=_cp(2 * 14 * tr * c + 8 * MIB),
    )(place, grad, got)


def _rs_chip_exchange(sums16):
    n = len(sums16)
    per = N_CHIPS - 1

    def body(*refs):
        ins, outs = refs[:n], refs[n:2 * n]
        send, recv = refs[2 * n:]
        x, y, c = _my_place()
        chips = _other_chips(x, y)
        cps = []
        for t in range(n):
            for j, (cx, cy) in enumerate(chips):
                cp = pltpu.make_async_remote_copy(src_ref=ins[t].at[2 * cx + cy], dst_ref=outs[t].at[j],
                                                  send_sem=send.at[t * per + j], recv_sem=recv.at[t * per + j],
                                                  device_id=(cx, cy, c), device_id_type=MESH)
                cp.start()
                cps.append(cp)
        for cp in cps:
            cp.wait()

    any_spec = pl.BlockSpec(memory_space=pl.ANY)
    return pl.pallas_call(
        body, name="rs_chip_exchange", in_specs=[any_spec] * n, out_specs=[any_spec] * n,
        out_shape=[jax.ShapeDtypeStruct((per,) + s.shape[1:], BF16) for s in sums16],
        scratch_shapes=[pltpu.SemaphoreType.DMA((n * per,)), pltpu.SemaphoreType.DMA((n * per,))],
        compiler_params=_cp(16 * MIB),
    )(*sums16)


def _rs_chip_add(place, sums32, got, name):
    _, r2, c = sums32.shape
    tr = min(TR_ELT, r2)

    def body(place_ref, s_ref, q_ref, o_ref):
        del place_ref
        o_ref[...] = ((s_ref[...] + q_ref[0].astype(F32)) + q_ref[1].astype(F32)) + q_ref[2].astype(F32)

    return pl.pallas_call(
        body, name=name,
        grid_spec=pltpu.PrefetchScalarGridSpec(
            num_scalar_prefetch=1, grid=(r2 // tr,),
            in_specs=[pl.BlockSpec((None, tr, c), lambda i, p: (p[1], i, 0)),
                      pl.BlockSpec((N_CHIPS - 1, tr, c), lambda i, p: (0, i, 0))],
            out_specs=pl.BlockSpec((tr, c), lambda i, p: (i, 0))),
        out_shape=jax.ShapeDtypeStruct((r2, c), F32),
        compiler_params=_cp(2 * 14 * tr * c + 8 * MIB),
    )(place, sums32, got)


def _rs_pair_share(halves):
    n = len(halves)

    def body(*refs):
        ins, outs = refs[:n], refs[n:2 * n]
        send, recv, local = refs[2 * n:]
        x, y, c = _my_place()
        cps, mine = [], []
        for t in range(n):
            lc = pltpu.make_async_copy(ins[t], outs[t].at[c], local.at[t])
            lc.start()
            mine.append(lc)
            cp = pltpu.make_async_remote_copy(src_ref=ins[t], dst_ref=outs[t].at[c], send_sem=send.at[t],
                                              recv_sem=recv.at[t], device_id=(x, y, 1 - c), device_id_type=MESH)
            cp.start()
            cps.append(cp)
        for t in range(n):
            other = outs[t].at[1 - c]
            pltpu.make_async_remote_copy(src_ref=other, dst_ref=other, send_sem=send.at[t], recv_sem=recv.at[t],
                                         device_id=(x, y, 1 - c), device_id_type=MESH).wait_recv()
        for cp in cps:
            cp.wait_send()
        for lc in mine:
            lc.wait()

    any_spec = pl.BlockSpec(memory_space=pl.ANY)
    return pl.pallas_call(
        body, name="rs_pair_share", in_specs=[any_spec] * n, out_specs=[any_spec] * n,
        out_shape=[jax.ShapeDtypeStruct((2,) + h.shape, F32) for h in halves],
        scratch_shapes=[pltpu.SemaphoreType.DMA((n,)), pltpu.SemaphoreType.DMA((n,)), pltpu.SemaphoreType.DMA((n,))],
        compiler_params=_cp(16 * MIB),
    )(*halves)


_PACK_TILE = SUBLANES * SLAB


def _pack(arrays):
    rows = []
    for a in arrays:
        flat = a.reshape(-1).astype(F32)
        padded = -(-flat.shape[0] // _PACK_TILE) * _PACK_TILE
        rows.append(jnp.pad(flat, (0, padded - flat.shape[0])).reshape(-1, SLAB))
    return jnp.concatenate(rows, axis=0)


def _unpack(packed, shapes):
    out, row = [], 0
    for shp in shapes:
        size = math.prod(shp)
        nrow = -(-size // _PACK_TILE) * SUBLANES
        out.append(packed[row:row + nrow].reshape(-1)[:size].reshape(shp))
        row += nrow
    return out


def kernel(x, norm_pre_g, w_in, conv_w, conv_b, ssm_a_re, ssm_a_im, ssm_log_dt, ssm_b_re, ssm_b_im, ssm_c_re, ssm_c_im, ssm_d, w_glu, b_glu, w_out, norm_post_g, loss_target, m_norm_pre_g, m_w_in, m_conv_w, m_conv_b, m_ssm_a_re, m_ssm_a_im, m_ssm_log_dt, m_ssm_b_re, m_ssm_b_im, m_ssm_c_re, m_ssm_c_im, m_ssm_d, m_w_glu, m_b_glu, m_w_out, m_norm_post_g, v_norm_pre_g, v_w_in, v_conv_w, v_conv_b, v_ssm_a_re, v_ssm_a_im, v_ssm_log_dt, v_ssm_b_re, v_ssm_b_im, v_ssm_c_re, v_ssm_c_im, v_ssm_d, v_w_glu, v_b_glu, v_w_out, v_norm_post_g):
    xs, tgt = x[0], loss_target[0]
    t, d = xs.shape
    dc = conv_b.shape[0]
    dssm = ssm_d.shape[0]
    g, p = ssm_a_re.shape
    h = SSM_H
    nq = dssm // SLAB
    n_shard = w_in.shape[1]
    steps = min(T_SCAN, t) // SUBLANES
    mx, my, mc = _my_place()
    chip = 2 * mx + my
    place = jnp.stack([mc, chip]).astype(jnp.int32)

    halves = lambda a: a.reshape(2, a.shape[0] // 2, a.shape[1])
    win_g, wout_g, wglu_g = _allgather_halves(
        [halves(_cast_bf16(w_in, "cast_w_in")), halves(_cast_bf16(w_out, "cast_w_out")),
         halves(_cast_bf16(w_glu, "cast_w_glu"))], "allgather_weights")
    win_b = win_g.reshape(N_CHIPS, d, n_shard)
    wout_b = wout_g.reshape(dc + dssm, d)
    wglu_b = wglu_g.reshape(dssm, dssm)

    cw_cols = conv_w.shape[1]
    cw_pad = -(-cw_cols // SLAB) * SLAB
    cw_blk = jnp.zeros((SUBLANES, cw_pad), F32).at[:conv_w.shape[0], :cw_cols].set(conv_w)
    cw_all = _allgather_flat(cw_blk, "allgather_conv_w")
    conv_w_full = jnp.concatenate([cw_all[2 * k, :, :cw_cols] for k in range(N_CHIPS)], axis=1)

    expand = jnp.repeat(jnp.eye(p, dtype=F32), h, axis=1)
    b_re2, b_im2 = ssm_b_re.reshape(g, p * h), ssm_b_im.reshape(g, p * h)
    log_dt2 = ssm_log_dt.reshape(g, 1)
    lbr, lbi, bbr2, bbi2, pw3 = _zoh_fwd(ssm_a_re, ssm_a_im, log_dt2, b_re2, b_im2, expand, steps)
    lam = jnp.stack([lbr.reshape(g * p), lbi.reshape(g * p)])
    pw = pw3.reshape(2, g * p)
    to_slab_b = lambda b2: _blockdiag(b2.reshape(nq, GROUPS_PER_SLAB, p, h).transpose(0, 1, 3, 2))
    bq = jnp.concatenate([to_slab_b(bbr2), to_slab_b(bbi2)], axis=2).astype(BF16)
    to_slab_c = lambda c3: _blockdiag(c3.reshape(nq, GROUPS_PER_SLAB, h, p).transpose(0, 1, 3, 2))
    cq = jnp.concatenate([to_slab_c(ssm_c_re), to_slab_c(-ssm_c_im)], axis=1).astype(BF16)

    g_pre2, g_post2 = norm_pre_g.reshape(1, d), norm_post_g.reshape(1, d)
    conv_b2, b_glu2, d_skip2 = conv_b.reshape(1, dc), b_glu.reshape(1, dssm), ssm_d.reshape(1, dssm)
    u_block = 4 * dc // dssm

    hb = _prenorm(xs, g_pre2)
    proj = _matmul_nn(hb, win_b, "inproj")
    y, cin = _ssm_fwd(proj, u_block, bq, cq, lam, pw, d_skip2)
    mix = _gate_fwd(proj, y, conv_w_full, conv_b2, wglu_b, b_glu2, dc)
    loss_blk, dout, dob, dmix, gg_post = _outproj(mix, wout_b, xs, tgt, g_post2)

    gw_out = _matmul_tn(mix, dob, 1, "grad_w_out")
    dproj, dy, gsmall, gw_glu = _gate_bwd(proj, y, dmix, conv_w_full, conv_b2, wglu_b, b_glu2, dc)
    dproj, gb_dense, gc_dense, glam, gd = _ssm_bwd(proj, u_block, dy, cin, dproj, bq, cq, lam, pw, d_skip2)
    gx, gg_pre = _dh_prenorm_bwd(dproj, win_b, xs, dout, g_pre2)
    gw_in = _matmul_tn(hb, dproj, N_CHIPS, "grad_w_in")

    gb4 = gb_dense.reshape(nq, SLAB, 2, SLAB_STATES)
    g_bbr2 = _blockdiag_take(gb4[:, :, 0, :], h, p).transpose(0, 1, 3, 2).reshape(g, p * h)
    g_bbi2 = _blockdiag_take(gb4[:, :, 1, :], h, p).transpose(0, 1, 3, 2).reshape(g, p * h)
    gc4 = gc_dense.reshape(nq, SLAB, 2, SLAB_STATES)
    g_c_re = _blockdiag_take(gc4[:, :, 0, :], h, p).reshape(g, h, p)
    g_c_im = -_blockdiag_take(gc4[:, :, 1, :], h, p).reshape(g, h, p)
    g_a_re, g_a_im, g_ld, g_b_re2, g_b_im2 = _zoh_bwd(
        ssm_a_re, ssm_a_im, log_dt2, b_re2, b_im2, expand,
        glam[0].reshape(g, p), glam[1].reshape(g, p), g_bbr2, g_bbi2)

    small_names = ["norm_pre_g", "conv_w", "conv_b", "ssm_a_re", "ssm_a_im", "ssm_log_dt", "ssm_b_re", "ssm_b_im",
                   "ssm_c_re", "ssm_c_im", "ssm_d", "b_glu", "norm_post_g", "loss"]
    small_g = {
        "norm_pre_g": gg_pre[0], "conv_w": gsmall[0:3], "conv_b": gsmall[3], "ssm_a_re": g_a_re, "ssm_a_im": g_a_im,
        "ssm_log_dt": g_ld.reshape(g), "ssm_b_re": g_b_re2.reshape(g, p, h), "ssm_b_im": g_b_im2.reshape(g, p, h),
        "ssm_c_re": g_c_re, "ssm_c_im": g_c_im, "ssm_d": gd[0], "b_glu": gsmall[4], "norm_post_g": gg_post[0],
        "loss": loss_blk[0, 0:1],
    }
    zeros_cw = jnp.zeros((conv_w.shape[0], dc), F32)
    one0 = jnp.zeros((1,), F32)
    small_w = dict(norm_pre_g=norm_pre_g, conv_w=zeros_cw, conv_b=conv_b, ssm_a_re=ssm_a_re, ssm_a_im=ssm_a_im,
                   ssm_log_dt=ssm_log_dt, ssm_b_re=ssm_b_re, ssm_b_im=ssm_b_im, ssm_c_re=ssm_c_re, ssm_c_im=ssm_c_im,
                   ssm_d=ssm_d, b_glu=b_glu, norm_post_g=norm_post_g, loss=one0)
    small_m = dict(norm_pre_g=m_norm_pre_g, conv_w=zeros_cw, conv_b=m_conv_b, ssm_a_re=m_ssm_a_re, ssm_a_im=m_ssm_a_im,
                   ssm_log_dt=m_ssm_log_dt, ssm_b_re=m_ssm_b_re, ssm_b_im=m_ssm_b_im, ssm_c_re=m_ssm_c_re,
                   ssm_c_im=m_ssm_c_im, ssm_d=m_ssm_d, b_glu=m_b_glu, norm_post_g=m_norm_post_g, loss=one0)
    small_v = dict(norm_pre_g=v_norm_pre_g, conv_w=zeros_cw, conv_b=v_conv_b, ssm_a_re=v_ssm_a_re, ssm_a_im=v_ssm_a_im,
                   ssm_log_dt=v_ssm_log_dt, ssm_b_re=v_ssm_b_re, ssm_b_im=v_ssm_b_im, ssm_c_re=v_ssm_c_re,
                   ssm_c_im=v_ssm_c_im, ssm_d=v_ssm_d, b_glu=v_b_glu, norm_post_g=v_norm_post_g, loss=one0)
    shapes = [small_w[nm].shape for nm in small_names]
    gathered = _allgather_flat(_pack([small_g[nm] for nm in small_names]), "allgather_small_grads")
    packs = _adamw_small(gathered, _pack([small_w[nm] for nm in small_names]),
                         _pack([small_m[nm] for nm in small_names]), _pack([small_v[nm] for nm in small_names]))
    sg, sd, sm, sv = [dict(zip(small_names, _unpack(pk, shapes))) for pk in packs]
    loss = sg["loss"][0]

    g_cw = lax.dynamic_slice_in_dim(sg["conv_w"], chip * cw_cols, cw_cols, axis=1)
    pad_cw = lambda a: jnp.zeros((SUBLANES, cw_pad), F32).at[:a.shape[0], :cw_cols].set(a)
    cut_cw = lambda a: a[:conv_w.shape[0], :cw_cols]
    d_cw, m_cw, v_cw = [cut_cw(a) for a in _adamw(pad_cw(conv_w), pad_cw(g_cw), pad_cw(m_conv_w), pad_cw(v_conv_w),
                                                  "adamw_conv_w")]

    big =[gw_in.reshape(N_CHIPS, 2, d // 2, n_shard),
           gw_out.reshape(N_CHIPS, 2, (dc + dssm) // (2 * N_CHIPS), d),
           gw_glu.reshape(N_CHIPS, 2, dssm // (2 * N_CHIPS), dssm)]
    tags =["w_in", "w_out", "w_glu"]
    got = _rs_pair_exchange(big)
    sums = [_rs_pair_add(place, gt, rt, "rs_pair_add_" + tg) for gt, rt, tg in zip(big, got, tags)]
    recvd = _rs_chip_exchange([s16 for s16, _ in sums])
    mine = [_rs_chip_add(place, s32, qt, "rs_chip_add_" + tg) for (_, s32), qt, tg in zip(sums, recvd, tags)]
    full = _rs_pair_share(mine)
    g_win, g_wout, g_wglu = [f.reshape(2 * f.shape[1], f.shape[2]) for f in full]
    d_win, m_win, v_win = _adamw(w_in, g_win, m_w_in, v_w_in, "adamw_w_in")
    d_wout, m_wout, v_wout = _adamw(w_out, g_wout, m_w_out, v_w_out, "adamw_w_out")
    d_wglu, m_wglu, v_wglu = _adamw(w_glu, g_wglu, m_w_glu, v_w_glu, "adamw_w_glu")

    order = ["norm_pre_g", "w_in", "conv_w", "conv_b", "ssm_a_re", "ssm_a_im", "ssm_log_dt", "ssm_b_re", "ssm_b_im",
             "ssm_c_re", "ssm_c_im", "ssm_d", "w_glu", "b_glu", "w_out", "norm_post_g"]
    grads, deltas, new_m, new_v = dict(sg), dict(sd), dict(sm), dict(sv)
    grads.update(w_in=g_win, w_out=g_wout, w_glu=g_wglu, conv_w=g_cw)
    deltas.update(w_in=d_win, w_out=d_wout, w_glu=d_wglu, conv_w=d_cw)
    new_m.update(w_in=m_win, w_out=m_wout, w_glu=m_wglu, conv_w=m_cw)
    new_v.update(w_in=v_win, w_out=v_wout, w_glu=v_wglu, conv_w=v_cw)
    return (loss, gx[None], *[grads[nm] for nm in order], *[deltas[nm] for nm in order],
            *[new_m[nm] for nm in order], *[new_v[nm] for nm in order])
```

```python
import functools
import math

import jax
import jax.numpy as jnp
from jax import lax
from jax.experimental import pallas as pl
from jax.experimental.pallas import tpu as pltpu

F32 = jnp.float32
BF16 = jnp.bfloat16
MESH = pl.DeviceIdType.MESH

EPS = 1e-6
SSM_H = 16
SSM_P = 64
GROUPS_PER_SLAB = 8
SLAB = 128
SLAB_STATES = GROUPS_PER_SLAB * SSM_P
N_CHIPS = 4
N_DEV = 8

ADAM_LR = 0.001
ADAM_B1 = 0.9
ADAM_B2 = 0.999
ADAM_EPS = 1e-08
ADAM_WD = 0.01
ADAM_STEP = 10

MIB = 1024 * 1024
VMEM_CAP = 48 * MIB
SUBLANES = 8

TM_NORM = 512
TM_PROJ = 512
TM_GATE = 256
TM_OUT = 256
TM_DH = 256
T_SCAN = 256
TK_TN = 512
TM_TN = 1024
TR_ELT = 256


def _cp(vmem_bytes, **kw):
    return pltpu.CompilerParams(vmem_limit_bytes=int(min(VMEM_CAP, max(16 * MIB, vmem_bytes))), **kw)


def _my_place():
    return lax.axis_index("x"), lax.axis_index("y"), lax.axis_index("c")


def _other_chips(x, y):
    return [(1 - x, y), (x, 1 - y), (1 - x, 1 - y)]


def _silu(z):
    s = jax.nn.sigmoid(z)
    return z * s, s


def _dsilu(z, s):
    return s * (1.0 + z * (1.0 - s))


_GELU_K = math.sqrt(2.0 / math.pi)
_GELU_C = 0.044715


def _gelu(y):
    th = jnp.tanh(_GELU_K * (y + _GELU_C * y * y * y))
    return 0.5 * y * (1.0 + th), th


def _dgelu(y, th):
    return 0.5 * (1.0 + th) + 0.5 * y * (1.0 - th * th) * _GELU_K * (1.0 + 3.0 * _GELU_C * y * y)


def _cast_bf16(w, name):
    r, c = w.shape
    tr = min(TR_ELT, r)

    def body(w_ref, o_ref):
        o_ref[...] = w_ref[...].astype(BF16)

    return pl.pallas_call(
        body, name=name, grid=(r // tr,),
        in_specs=[pl.BlockSpec((tr, c), lambda i: (i, 0))],
        out_specs=pl.BlockSpec((tr, c), lambda i: (i, 0)),
        out_shape=jax.ShapeDtypeStruct((r, c), BF16),
        compiler_params=_cp(12 * tr * c),
    )(w)


def _prenorm(x, g):
    t, d = x.shape
    tm = min(TM_NORM, t)

    def body(x_ref, g_ref, h_ref):
        xv = x_ref[...]
        r = lax.rsqrt(jnp.mean(xv * xv, axis=-1, keepdims=True) + EPS)
        h_ref[...] = (xv * r * g_ref[...]).astype(BF16)

    return pl.pallas_call(
        body, name="prenorm", grid=(t // tm,),
        in_specs=[pl.BlockSpec((tm, d), lambda i: (i, 0)), pl.BlockSpec((1, d), lambda i: (0, 0))],
        out_specs=pl.BlockSpec((tm, d), lambda i: (i, 0)),
        out_shape=jax.ShapeDtypeStruct((t, d), BF16),
        compiler_params=_cp(20 * tm * d),
    )(x, g)


def _matmul_nn(a, b, name):
    t, k = a.shape
    nb, _, n = b.shape
    tm = min(TM_PROJ, t)

    def body(a_ref, b_ref, o_ref):
        o_ref[...] = jnp.dot(a_ref[...], b_ref[...], preferred_element_type=F32)

    return pl.pallas_call(
        body, name=name, grid=(nb, t // tm),
        in_specs=[pl.BlockSpec((tm, k), lambda j, i: (i, 0)), pl.BlockSpec((None, k, n), lambda j, i: (j, 0, 0))],
        out_specs=pl.BlockSpec((tm, n), lambda j, i: (i, j)),
        out_shape=jax.ShapeDtypeStruct((t, nb * n), F32),
        compiler_params=_cp(2 * (2 * tm * k + 2 * k * n + 4 * tm * n) + 4 * MIB),
    )(a, b)


def _matmul_tn(a, b, nb, name):
    t, m = a.shape
    n = b.shape[1] // nb
    tk = min(TK_TN, t)
    tma = min(TM_TN, m)

    def body(a_ref, b_ref, o_ref):
        @pl.when(pl.program_id(2) == 0)
        def _():
            o_ref[...] = jnp.zeros_like(o_ref)

        o_ref[...] += lax.dot_general(a_ref[...], b_ref[...], (((0,), (0,)), ((), ())), preferred_element_type=F32)

    return pl.pallas_call(
        body, name=name, grid=(nb, m // tma, t // tk),
        in_specs=[pl.BlockSpec((tk, tma), lambda j, i, k: (k, i)), pl.BlockSpec((tk, n), lambda j, i, k: (k, j))],
        out_specs=pl.BlockSpec((None, tma, n), lambda j, i, k: (j, i, 0)),
        out_shape=jax.ShapeDtypeStruct((nb, m, n), F32),
        compiler_params=_cp(2 * (2 * tk * tma + 2 * tk * n + 4 * tma * n) + 8 * MIB),
    )(a, b)


def _outproj(mix, w_out, x, tgt, g_post):
    t, dm = mix.shape
    d = w_out.shape[1]
    tm = min(TM_OUT, t)

    def body(mix_ref, w_ref, x_ref, t_ref, g_ref, loss_ref, dout_ref, do_ref, dmix_ref, gg_ref):
        @pl.when(pl.program_id(0) == 0)
        def _():
            loss_ref[...] = jnp.zeros_like(loss_ref)
            gg_ref[...] = jnp.zeros_like(gg_ref)

        w = w_ref[...]
        o = jnp.dot(mix_ref[...], w, preferred_element_type=F32)
        r = lax.rsqrt(jnp.mean(o * o, axis=-1, keepdims=True) + EPS)
        nh = o * r
        g = g_ref[...]
        e = x_ref[...] + nh * g - t_ref[...]
        loss_ref[...] += jnp.sum(e * e) * (0.5 / d)
        dout = e * (1.0 / d)
        dout_ref[...] = dout
        gg_ref[0:1, :] += jnp.sum(dout * nh, axis=0, keepdims=True)
        dn = dout * g
        do = r * (dn - nh * jnp.mean(dn * nh, axis=-1, keepdims=True))
        dob = do.astype(BF16)
        do_ref[...] = dob
        dmix_ref[...] = lax.dot_general(dob, w, (((1,), (1,)), ((), ())), preferred_element_type=F32)

    row = lambda i: (i, 0)
    fixed = lambda i: (0, 0)
    return pl.pallas_call(
        body, name="outproj", grid=(t // tm,),
        in_specs=[pl.BlockSpec((tm, dm), row), pl.BlockSpec((dm, d), fixed), pl.BlockSpec((tm, d), row),
                  pl.BlockSpec((tm, d), row), pl.BlockSpec((1, d), fixed)],
        out_specs=[pl.BlockSpec((SUBLANES, SLAB), fixed), pl.BlockSpec((tm, d), row), pl.BlockSpec((tm, d), row),
                   pl.BlockSpec((tm, dm), row), pl.BlockSpec((SUBLANES, d), fixed)],
        out_shape=[jax.ShapeDtypeStruct((SUBLANES, SLAB), F32), jax.ShapeDtypeStruct((t, d), F32),
                   jax.ShapeDtypeStruct((t, d), BF16), jax.ShapeDtypeStruct((t, dm), F32),
                   jax.ShapeDtypeStruct((SUBLANES, d), F32)],
        compiler_params=_cp(2 * (2 * dm * d + tm * (2 * dm + 4 * d * 3 + 2 * d + 4 * dm)) + 16 * MIB),
    )(mix, w_out, x, tgt, g_post)


def _dh_prenorm_bwd(dproj, w_in, x, dout, g_pre):
    t, d = x.shape
    nb, _, n = w_in.shape
    tm = min(TM_DH, t)

    def body(dp_ref, w_ref, x_ref, dout_ref, g_ref, dx_ref, gg_ref, acc):
        i, k = pl.program_id(0), pl.program_id(1)

        @pl.when((i == 0) & (k == 0))
        def _():
            gg_ref[...] = jnp.zeros_like(gg_ref)

        part = lax.dot_general(dp_ref[...], w_ref[...], (((1,), (1,)), ((), ())), preferred_element_type=F32)

        @pl.when(k == 0)
        def _():
            acc[...] = part

        @pl.when(k > 0)
        def _():
            acc[...] += part

        @pl.when(k == nb - 1)
        def _():
            dh = acc[...]
            xv = x_ref[...]
            r = lax.rsqrt(jnp.mean(xv * xv, axis=-1, keepdims=True) + EPS)
            xh = xv * r
            gg_ref[0:1, :] += jnp.sum(dh * xh, axis=0, keepdims=True)
            dg = dh * g_ref[...]
            dx_ref[...] = dout_ref[...] + r * (dg - xh * jnp.mean(dg * xh, axis=-1, keepdims=True))

    row = lambda i, k: (i, 0)
    fixed = lambda i, k: (0, 0)
    return pl.pallas_call(
        body, name="dh_prenorm_bwd", grid=(t // tm, nb),
        in_specs=[pl.BlockSpec((tm, n), lambda i, k: (i, k)), pl.BlockSpec((None, d, n), lambda i, k: (k, 0, 0)),
                  pl.BlockSpec((tm, d), row), pl.BlockSpec((tm, d), row), pl.BlockSpec((1, d), fixed)],
        out_specs=[pl.BlockSpec((tm, d), row), pl.BlockSpec((SUBLANES, d), fixed)],
        out_shape=[jax.ShapeDtypeStruct((t, d), F32), jax.ShapeDtypeStruct((SUBLANES, d), F32)],
        scratch_shapes=[pltpu.VMEM((tm, d), F32)],
        compiler_params=_cp(2 * (2 * tm * n + 2 * d * n + 12 * tm * d) + 4 * tm * d + 12 * MIB),
    )(dproj, w_in, x, dout, g_pre)


def _adamw(w, g, m, v, name):
    r, c = w.shape
    tr = min(TR_ELT, r)
    c1 = 1.0 - ADAM_B1 ** ADAM_STEP
    c2 = 1.0 - ADAM_B2 ** ADAM_STEP

    def body(w_ref, g_ref, m_ref, v_ref, d_ref, mo_ref, vo_ref):
        gv = g_ref[...]
        mn = ADAM_B1 * m_ref[...] + (1.0 - ADAM_B1) * gv
        vn = ADAM_B2 * v_ref[...] + (1.0 - ADAM_B2) * (gv * gv)
        d_ref[...] = -ADAM_LR * ((mn / c1) / (jnp.sqrt(vn / c2) + ADAM_EPS) + ADAM_WD * w_ref[...])
        mo_ref[...] = mn
        vo_ref[...] = vn

    spec = pl.BlockSpec((tr, c), lambda i: (i, 0))
    sds = jax.ShapeDtypeStruct((r, c), F32)
    return pl.pallas_call(
        body, name=name, grid=(r // tr,), in_specs=[spec] * 4, out_specs=[spec] * 3, out_shape=[sds] * 3,
        compiler_params=_cp(2 * 7 * 4 * tr * c + 8 * MIB),
    )(w, g, m, v)


def _adamw_small(gathered, w, m, v):
    r, c = w.shape
    c1 = 1.0 - ADAM_B1 ** ADAM_STEP
    c2 = 1.0 - ADAM_B2 ** ADAM_STEP

    def body(gat_ref, w_ref, m_ref, v_ref, g_ref, d_ref, mo_ref, vo_ref):
        gv = gat_ref[0]
        for dev in range(1, N_DEV):
            gv = gv + gat_ref[dev]
        g_ref[...] = gv
        mn = ADAM_B1 * m_ref[...] + (1.0 - ADAM_B1) * gv
        vn = ADAM_B2 * v_ref[...] + (1.0 - ADAM_B2) * (gv * gv)
        d_ref[...] = -ADAM_LR * ((mn / c1) / (jnp.sqrt(vn / c2) + ADAM_EPS) + ADAM_WD * w_ref[...])
        mo_ref[...] = mn
        vo_ref[...] = vn

    sds = jax.ShapeDtypeStruct((r, c), F32)
    return pl.pallas_call(
        body, name="adamw_small", out_shape=[sds] * 4,
        compiler_params=_cp((N_DEV + 12) * 4 * r * c + 8 * MIB),
    )(gathered, w, m, v)


def _zoh(a_re, a_im, log_dt, b_re2, b_im2, expand):
    dt = jnp.exp(log_dt)
    mag = jnp.exp(a_re * dt)
    lbr, lbi = mag * jnp.cos(a_im * dt), mag * jnp.sin(a_im * dt)
    nr, ni = lbr - 1.0, lbi
    den = a_re * a_re + a_im * a_im
    qr = (nr * a_re + ni * a_im) / den
    qi = (ni * a_re - nr * a_im) / den
    qr2 = jnp.dot(qr, expand, precision=lax.Precision.HIGHEST, preferred_element_type=F32)
    qi2 = jnp.dot(qi, expand, precision=lax.Precision.HIGHEST, preferred_element_type=F32)
    return lbr, lbi, qr2 * b_re2 - qi2 * b_im2, qr2 * b_im2 + qi2 * b_re2


def _zoh_fwd(a_re, a_im, log_dt, b_re2, b_im2, expand, power):
    g, p = a_re.shape

    def body(ar_ref, ai_ref, ld_ref, br_ref, bi_ref, e_ref, lbr_ref, lbi_ref, bbr_ref, bbi_ref, pw_ref):
        ar, ai, ld = ar_ref[...], ai_ref[...], ld_ref[...]
        lbr, lbi, bbr, bbi = _zoh(ar, ai, ld, br_ref[...], bi_ref[...], e_ref[...])
        lbr_ref[...], lbi_ref[...], bbr_ref[...], bbi_ref[...] = lbr, lbi, bbr, bbi
        dt = jnp.exp(ld) * float(power)
        mag = jnp.exp(ar * dt)
        pw_ref[0] = mag * jnp.cos(ai * dt)
        pw_ref[1] = mag * jnp.sin(ai * dt)

    gp = jax.ShapeDtypeStruct((g, p), F32)
    gph = jax.ShapeDtypeStruct(b_re2.shape, F32)
    return pl.pallas_call(
        body, name="zoh_fwd", out_shape=[gp, gp, gph, gph, jax.ShapeDtypeStruct((2, g, p), F32)],
        compiler_params=_cp(16 * MIB),
    )(a_re, a_im, log_dt, b_re2, b_im2, expand)


def _zoh_bwd(a_re, a_im, log_dt, b_re2, b_im2, expand, g_lbr, g_lbi, g_bbr, g_bbi):
    def body(ar_ref, ai_ref, ld_ref, br_ref, bi_ref, e_ref, c0, c1, c2, c3, gar, gai, gld, gbr, gbi):
        e = e_ref[...]
        _, vjp = jax.vjp(lambda a, b, c, d, f: _zoh(a, b, c, d, f, e),
                         ar_ref[...], ai_ref[...], ld_ref[...], br_ref[...], bi_ref[...])
        gar[...], gai[...], gld[...], gbr[...], gbi[...] = vjp((c0[...], c1[...], c2[...], c3[...]))

    sds = lambda a: jax.ShapeDtypeStruct(a.shape, F32)
    return pl.pallas_call(
        body, name="zoh_bwd", out_shape=[sds(a_re), sds(a_im), sds(log_dt), sds(b_re2), sds(b_im2)],
        compiler_params=_cp(16 * MIB),
    )(a_re, a_im, log_dt, b_re2, b_im2, expand, g_lbr, g_lbi, g_bbr, g_bbi)


def _blockdiag(blocks):
    nq, _, r, c = blocks.shape
    eye = jnp.eye(GROUPS_PER_SLAB, dtype=blocks.dtype)
    out = blocks[:, :, :, None, :] * eye[None, :, None, :, None]
    return out.reshape(nq, GROUPS_PER_SLAB * r, GROUPS_PER_SLAB * c)


def _blockdiag_take(dense, r, c):
    nq = dense.shape[0]
    eye = jnp.eye(GROUPS_PER_SLAB, dtype=dense.dtype)
    d6 = dense.reshape(nq, GROUPS_PER_SLAB, r, GROUPS_PER_SLAB, c)
    return jnp.sum(d6 * eye[None, :, None, :, None], axis=3)


def _scan_slab(s_ref, row0, q, lam_ref, pw_ref, car_ref, steps, reverse, prev_ref=None, prev_row0=0, glam_ref=None):
    sign = -1.0 if reverse else 1.0
    half = SLAB_STATES // SLAB
    cols = [(q * 2 * half + m, q * 2 * half + half + m, q * SLAB_STATES + m * SLAB) for m in range(half)]
    nm = len(cols)
    full = (SUBLANES, SLAB)
    lam = [(jnp.broadcast_to(lam_ref[0:1, pl.ds(cl, SLAB)], full),
            jnp.broadcast_to(sign * lam_ref[1:2, pl.ds(cl, SLAB)], full)) for (_, _, cl) in cols]

    def step_rows(jj, base):
        j = (steps - 1 - jj) if reverse else jj
        return j, pl.ds(pl.multiple_of(base + j * SUBLANES, SUBLANES), SUBLANES)

    def pass1(jj, car):
        _, rows = step_rows(jj, row0)
        out = []
        for m, (cr, ci, _) in enumerate(cols):
            sr, si = car[2 * m], car[2 * m + 1]
            lr, li = lam[m]
            nr = lr * sr - li * si + s_ref[cr, rows, :]
            ni = lr * si + li * sr + s_ref[ci, rows, :]
            s_ref[cr, rows, :] = nr
            s_ref[ci, rows, :] = ni
            out += [nr, ni]
        return tuple(out)

    ends = lax.fori_loop(0, steps, pass1, tuple(jnp.zeros(full, F32) for _ in range(2 * nm)))

    entry = []
    for m, (cr, ci, cl) in enumerate(cols):
        ljr = pw_ref[0:1, pl.ds(cl, SLAB)]
        lji = sign * pw_ref[1:2, pl.ds(cl, SLAB)]
        c_r = car_ref[cr, 0:1, :]
        c_i = car_ref[ci, 0:1, :]
        rows_r, rows_i = [None] * SUBLANES, [None] * SUBLANES
        order = range(SUBLANES - 1, -1, -1) if reverse else range(SUBLANES)
        for b in order:
            rows_r[b], rows_i[b] = c_r, c_i
            e_r, e_i = ends[2 * m][b:b + 1], ends[2 * m + 1][b:b + 1]
            c_r, c_i = ljr * c_r - lji * c_i + e_r, ljr * c_i + lji * c_r + e_i
        car_ref[cr, 0:1, :] = c_r
        car_ref[ci, 0:1, :] = c_i
        entry.append((jnp.concatenate(rows_r, axis=0), jnp.concatenate(rows_i, axis=0)))

    def pass2(jj, carry):
        j, rows = step_rows(jj, row0)
        decayed, acc = carry[:2 * nm], carry[2 * nm:]
        out_d, out_a = [], []
        for m, (cr, ci, cl) in enumerate(cols):
            lr, li = lam[m]
            dr, di = decayed[2 * m], decayed[2 * m + 1]
            dr, di = lr * dr - li * di, lr * di + li * dr
            nr = s_ref[cr, rows, :] + dr
            ni = s_ref[ci, rows, :] + di
            s_ref[cr, rows, :] = nr
            s_ref[ci, rows, :] = ni
            out_d += [dr, di]
            if prev_ref is not None:
                prow = pl.ds(pl.multiple_of(prev_row0 + (j - 1) * SUBLANES, SUBLANES), SUBLANES)
                qr = prev_ref[cr, prow, :]
                qi = prev_ref[ci, prow, :]
                out_a += [acc[2 * m] + (nr * qr + ni * qi), acc[2 * m + 1] + (ni * qr - nr * qi)]
        return tuple(out_d) + tuple(out_a)

    n_acc = 2 * nm if prev_ref is not None else 0
    init = tuple(e for pair in entry for e in pair) + tuple(jnp.zeros(full, F32) for _ in range(n_acc))
    accs = lax.fori_loop(0, steps, pass2, init)[2 * nm:]
    if prev_ref is not None:
        for m, (_, _, cl) in enumerate(cols):
            glam_ref[0:1, pl.ds(cl, SLAB)] += jnp.sum(accs[2 * m], axis=0, keepdims=True)
            glam_ref[1:2, pl.ds(cl, SLAB)] += jnp.sum(accs[2 * m + 1], axis=0, keepdims=True)


def _put_slab(s_ref, rows, q, val):
    per = 2 * SLAB_STATES // SLAB
    for i in range(per):
        s_ref[q * per + i, rows, :] = val[:, i * SLAB:(i + 1) * SLAB]


def _get_slab(s_ref, rows, q):
    per = 2 * SLAB_STATES // SLAB
    return jnp.concatenate([s_ref[q * per + i, rows, :] for i in range(per)], axis=1)


def _step_major_perm(tt):
    r = jnp.arange(tt)
    held = (r % SUBLANES) * (tt // SUBLANES) + r // SUBLANES
    return held[:, None] == r[None, :]


def _ssm_fwd(proj, u_block, bq, cq, lam, pw, d_skip):
    t = proj.shape[0]
    nq, ds, w2 = bq.shape
    assert ds == SLAB and w2 == 2 * SLAB_STATES
    dssm = nq * SLAB
    width = nq * w2
    ntile = width // SLAB
    tt = min(T_SCAN, t)
    steps = tt // SUBLANES
    perm = _step_major_perm(tt)
    pm, pmt = perm.astype(BF16), perm.T.astype(F32)

    def body(u_ref, pm_ref, pmt_ref, bq_ref, cq_ref, lam_ref, pw_ref, d_ref, y_ref, cin_ref, s_ref, car_ref):
        @pl.when(pl.program_id(0) == 0)
        def _():
            car_ref[...] = jnp.zeros_like(car_ref)

        cin_ref[...] = jnp.broadcast_to(car_ref[:, 0:1, :], cin_ref.shape)
        u = u_ref[...]
        ub = jnp.dot(pm_ref[...], u.astype(BF16), preferred_element_type=F32).astype(BF16)
        everything = slice(None)
        for q in range(nq):
            _put_slab(s_ref, everything, q,
                      jnp.dot(ub[:, q * SLAB:(q + 1) * SLAB], bq_ref[q], preferred_element_type=F32))
        for q in range(nq):
            _scan_slab(s_ref, 0, q, lam_ref, pw_ref, car_ref, steps, reverse=False)
        y_sm = jnp.concatenate(
            [jnp.dot(_get_slab(s_ref, everything, q).astype(BF16), cq_ref[q], preferred_element_type=F32)
             for q in range(nq)], axis=1)
        y_ref[...] = (jnp.dot(pmt_ref[...], y_sm, precision=lax.Precision.HIGHEST, preferred_element_type=F32)
                      + d_ref[...] * u)

    c3 = lambda i: (0, 0, 0)
    c2 = lambda i: (0, 0)
    return pl.pallas_call(
        body, name="ssm_fwd", grid=(t // tt,),
        in_specs=[pl.BlockSpec((tt, dssm), lambda i: (i, u_block)), pl.BlockSpec((tt, tt), c2),
                  pl.BlockSpec((tt, tt), c2), pl.BlockSpec(bq.shape, c3),
                  pl.BlockSpec(cq.shape, c3), pl.BlockSpec(lam.shape, c2), pl.BlockSpec(pw.shape, c2),
                  pl.BlockSpec((1, dssm), c2)],
        out_specs=[pl.BlockSpec((tt, dssm), lambda i: (i, 0)),
                   pl.BlockSpec((None, ntile, SUBLANES, SLAB), lambda i: (i, 0, 0, 0))],
        out_shape=[jax.ShapeDtypeStruct((t, dssm), F32), jax.ShapeDtypeStruct((t // tt, ntile, SUBLANES, SLAB), F32)],
        scratch_shapes=[pltpu.VMEM((ntile, tt, SLAB), F32), pltpu.VMEM((ntile, SUBLANES, SLAB), F32)],
        compiler_params=_cp(2 * (8 * tt * dssm + 4 * nq * ds * w2) + 4 * tt * width + 16 * MIB,
                            dimension_semantics=("arbitrary",)),
    )(proj, pm, pmt, bq, cq, lam, pw, d_skip)


def _ssm_bwd(proj, u_block, dy, cin, dproj, bq, cq, lam, pw, d_skip):
    t = proj.shape[0]
    nq, ds, w2 = bq.shape
    dssm = nq * SLAB
    width = nq * w2
    ntile = width // SLAB
    tt = min(T_SCAN, t)
    nt = t // tt
    steps = tt // SUBLANES
    halo = SUBLANES
    perm = _step_major_perm(tt)
    pm, pmt = perm.astype(BF16), perm.T.astype(F32)

    def body(u_ref, dy_ref, cin_ref, dp_any, pm_ref, pmt_ref, bq_ref, cq_ref, lam_ref, pw_ref, d_ref,
             du_ref, gb_ref, gc_ref, glam_ref, gd_ref, s_ref, gs_ref, car_f, car_b):
        del dp_any

        @pl.when(pl.program_id(0) == 0)
        def _():
            car_b[...] = jnp.zeros_like(car_b)
            gb_ref[...] = jnp.zeros_like(gb_ref)
            gc_ref[...] = jnp.zeros_like(gc_ref)
            glam_ref[...] = jnp.zeros_like(glam_ref)
            gd_ref[...] = jnp.zeros_like(gd_ref)

        u = u_ref[...]
        dyv = dy_ref[...]
        gd_ref[0:1, :] += jnp.sum(dyv * u, axis=0, keepdims=True)
        pmv = pm_ref[...]
        ub = jnp.dot(pmv, u.astype(BF16), preferred_element_type=F32).astype(BF16)
        dyb = jnp.dot(pmv, dyv.astype(BF16), preferred_element_type=F32).astype(BF16)
        car_f[...] = cin_ref[...]
        data = slice(halo, halo + tt)
        everything = slice(None)
        for q in range(nq):
            _put_slab(s_ref, data, q, jnp.dot(ub[:, q * SLAB:(q + 1) * SLAB], bq_ref[q], preferred_element_type=F32))
        for q in range(nq):
            _scan_slab(s_ref, halo, q, lam_ref, pw_ref, car_f, steps, reverse=False)
        last_step = s_ref[:, halo + tt - SUBLANES:halo + tt, :]
        s_ref[:, 0:halo, :] = jnp.concatenate([cin_ref[:, 0:1, :], last_step[:, 0:SUBLANES - 1, :]], axis=1)
        tn = (((0,), (0,)), ((), ()))
        nt_dims = (((1,), (1,)), ((), ()))
        for q in range(nq):
            sl = slice(q * SLAB, (q + 1) * SLAB)
            gc_ref[q] += lax.dot_general(dyb[:, sl], _get_slab(s_ref, data, q).astype(BF16), tn,
                                         preferred_element_type=F32)
            _put_slab(gs_ref, everything, q,
                      lax.dot_general(dyb[:, sl], cq_ref[q], nt_dims, preferred_element_type=F32))
        for q in range(nq):
            _scan_slab(gs_ref, 0, q, lam_ref, pw_ref, car_b, steps, reverse=True,
                       prev_ref=s_ref, prev_row0=halo, glam_ref=glam_ref)
        du_parts = []
        for q in range(nq):
            sl = slice(q * SLAB, (q + 1) * SLAB)
            gsb = _get_slab(gs_ref, everything, q).astype(BF16)
            du_parts.append(lax.dot_general(gsb, bq_ref[q], nt_dims, preferred_element_type=F32))
            gb_ref[q] += lax.dot_general(ub[:, sl], gsb, tn, preferred_element_type=F32)
        du_sm = jnp.concatenate(du_parts, axis=1)
        du = jnp.dot(pmt_ref[...], du_sm, precision=lax.Precision.HIGHEST, preferred_element_type=F32)
        du_ref[...] = (du + dyv * d_ref[...]).astype(BF16)

    c3 = lambda i: (0, 0, 0)
    c2 = lambda i: (0, 0)
    rev = lambda i: (nt - 1 - i, 0)
    dense = jax.ShapeDtypeStruct((nq, SLAB, w2), F32)
    gp = lam.shape[1]
    return pl.pallas_call(
        body, name="ssm_bwd", grid=(nt,),
        in_specs=[pl.BlockSpec((tt, dssm), lambda i: (nt - 1 - i, u_block)), pl.BlockSpec((tt, dssm), rev),
                  pl.BlockSpec((None, ntile, SUBLANES, SLAB), lambda i: (nt - 1 - i, 0, 0, 0)),
                  pl.BlockSpec(memory_space=pl.ANY), pl.BlockSpec((tt, tt), c2), pl.BlockSpec((tt, tt), c2),
                  pl.BlockSpec(bq.shape, c3), pl.BlockSpec(cq.shape, c3),
                  pl.BlockSpec(lam.shape, c2), pl.BlockSpec(pw.shape, c2), pl.BlockSpec((1, dssm), c2)],
        out_specs=[pl.BlockSpec((tt, dssm), lambda i: (nt - 1 - i, u_block)), pl.BlockSpec(dense.shape, c3),
                   pl.BlockSpec(dense.shape, c3), pl.BlockSpec((SUBLANES, gp), c2), pl.BlockSpec((SUBLANES, dssm), c2)],
        out_shape=[jax.ShapeDtypeStruct(dproj.shape, dproj.dtype), dense, dense,
                   jax.ShapeDtypeStruct((SUBLANES, gp), F32), jax.ShapeDtypeStruct((SUBLANES, dssm), F32)],
        scratch_shapes=[pltpu.VMEM((ntile, tt + halo, SLAB), F32), pltpu.VMEM((ntile, tt, SLAB), F32),
                        pltpu.VMEM((ntile, SUBLANES, SLAB), F32), pltpu.VMEM((ntile, SUBLANES, SLAB), F32)],
        input_output_aliases={3: 0},
        compiler_params=_cp(2 * (10 * tt * dssm + 4 * nq * ds * w2 + 8 * nq * SLAB * w2)
                            + 8 * tt * width + 12 * MIB, dimension_semantics=("arbitrary",)),
    )(proj, dy, cin, dproj, pm, pmt, bq, cq, lam, pw, d_skip)


def _gate_fwd(proj, y, conv_w, conv_b, w_glu, b_glu, dc):
    t = proj.shape[0]
    dssm = y.shape[1]
    assert dc == dssm
    tm = min(TM_GATE, t)
    halo = SUBLANES

    def body(b_ref, c_ref, v_ref, zc_ref, zs_ref, y_ref, cw_ref, cb_ref, wg_ref, bg_ref, mix_ref, cv_buf):
        @pl.when(pl.program_id(0) == 0)
        def _():
            cv_buf[0:halo, :] = jnp.zeros((halo, dc), F32)

        cv = c_ref[...] * v_ref[...]
        cv_buf[halo:, :] = cv
        conv = (cb_ref[...] + cw_ref[2:3, :] * cv + cw_ref[1:2, :] * cv_buf[halo - 1:halo - 1 + tm, :]
                + cw_ref[0:1, :] * cv_buf[halo - 2:halo - 2 + tm, :])
        sz, _ = _silu(zc_ref[...])
        mix_ref[:, 0:dc] = (b_ref[...] * conv * sz).astype(BF16)
        cv_buf[0:halo, :] = cv_buf[tm:tm + halo, :]
        ge, _ = _gelu(y_ref[...])
        gl = jnp.dot(ge.astype(BF16), wg_ref[...], preferred_element_type=F32) + bg_ref[...]
        szs, _ = _silu(zs_ref[...])
        mix_ref[:, dc:] = (ge * jax.nn.sigmoid(gl) * szs).astype(BF16)

    col = lambda j: pl.BlockSpec((tm, dc), lambda i, j=j: (i, j))
    fixed = lambda i: (0, 0)
    return pl.pallas_call(
        body, name="gate_fwd", grid=(t // tm,),
        in_specs=[col(0), col(1), col(2), col(3), col(5), pl.BlockSpec((tm, dssm), lambda i: (i, 0)),
                  pl.BlockSpec(conv_w.shape, fixed), pl.BlockSpec((1, dc), fixed),
                  pl.BlockSpec(w_glu.shape, fixed), pl.BlockSpec((1, dssm), fixed)],
        out_specs=pl.BlockSpec((tm, dc + dssm), lambda i: (i, 0)),
        out_shape=jax.ShapeDtypeStruct((t, dc + dssm), BF16),
        scratch_shapes=[pltpu.VMEM((tm + halo, dc), F32)],
        compiler_params=_cp(2 * (6 * 4 * tm * dc + 2 * tm * (dc + dssm) + 2 * dssm * dssm) + 24 * tm * dc + 8 * MIB,
                            dimension_semantics=("arbitrary",)),
    )(proj, proj, proj, proj, proj, y, conv_w, conv_b, w_glu, b_glu)


def _gate_bwd(proj, y, dmix, conv_w, conv_b, w_glu, b_glu, dc):
    t = proj.shape[0]
    dssm = y.shape[1]
    tm = min(TM_GATE, t)
    nt = t // tm
    halo = SUBLANES
    blocks_per_tile = tm // halo

    def body(b_ref, c_ref, v_ref, zc_ref, zs_ref, cp_ref, vp_ref, y_ref, dm_ref, cw_ref, cb_ref, wg_ref, bg_ref,
             dp_ref, dy_ref, gs_ref, gwg_ref, cv_buf, dc_buf):
        i = pl.program_id(0)

        @pl.when(i == 0)
        def _():
            dc_buf[tm:, :] = jnp.zeros((halo, dc), F32)
            gs_ref[...] = jnp.zeros_like(gs_ref)
            gwg_ref[...] = jnp.zeros_like(gwg_ref)

        first_tile = (i == nt - 1)
        bv, cg, vv, zc = b_ref[...], c_ref[...], v_ref[...], zc_ref[...]
        cv = cg * vv
        cv_buf[0:halo, :] = jnp.where(first_tile, 0.0, cp_ref[...] * vp_ref[...])
        cv_buf[halo:, :] = cv
        w0, w1, w2 = cw_ref[0:1, :], cw_ref[1:2, :], cw_ref[2:3, :]
        conv = (cb_ref[...] + w2 * cv + w1 * cv_buf[halo - 1:halo - 1 + tm, :]
                + w0 * cv_buf[halo - 2:halo - 2 + tm, :])
        sz, sgc = _silu(zc)
        dyc = dm_ref[:, 0:dc]
        dp_ref[:, 0:dc] = (dyc * conv * sz).astype(BF16)
        dp_ref[:, 3 * dc:4 * dc] = (dyc * bv * conv * _dsilu(zc, sgc)).astype(BF16)
        dconv = dyc * bv * sz
        dc_buf[0:tm, :] = dconv
        d1 = dc_buf[1:1 + tm, :]
        d2 = dc_buf[2:2 + tm, :]
        dcv = w2 * dconv + w1 * d1 + w0 * d2
        dp_ref[:, dc:2 * dc] = (dcv * vv).astype(BF16)
        dp_ref[:, 2 * dc:3 * dc] = (dcv * cg).astype(BF16)
        gs_ref[0:1, :] += jnp.sum(cv * d2, axis=0, keepdims=True)
        gs_ref[1:2, :] += jnp.sum(cv * d1, axis=0, keepdims=True)
        gs_ref[2:3, :] += jnp.sum(cv * dconv, axis=0, keepdims=True)
        gs_ref[3:4, :] += jnp.sum(dconv, axis=0, keepdims=True)
        dc_buf[tm:, :] = dc_buf[0:halo, :]

        yv, zs = y_ref[...], zs_ref[...]
        ge, th = _gelu(yv)
        geb = ge.astype(BF16)
        wg = wg_ref[...]
        gl = jnp.dot(geb, wg, preferred_element_type=F32) + bg_ref[...]
        sg = jax.nn.sigmoid(gl)
        szs, sgs = _silu(zs)
        dys = dm_ref[:, dc:]
        ys = ge * sg
        dp_ref[:, 4 * dc:5 * dc] = jnp.zeros((tm, dc), BF16)
        dp_ref[:, 5 * dc:] = (dys * ys * _dsilu(zs, sgs)).astype(BF16)
        d_ys = dys * szs
        dgl = d_ys * ge * sg * (1.0 - sg)
        dglb = dgl.astype(BF16)
        gs_ref[4:5, :] += jnp.sum(dgl, axis=0, keepdims=True)
        gwg_ref[...] += lax.dot_general(geb, dglb, (((0,), (0,)), ((), ())), preferred_element_type=F32)
        dge = d_ys * sg + lax.dot_general(dglb, wg, (((1,), (1,)), ((), ())), preferred_element_type=F32)
        dy_ref[...] = dge * _dgelu(yv, th)

    col = lambda j: pl.BlockSpec((tm, dc), lambda i, j=j: (nt - 1 - i, j))
    prev = lambda j: pl.BlockSpec((halo, dc), lambda i, j=j: (jnp.maximum((nt - 1 - i) * blocks_per_tile - 1, 0), j))
    rev = lambda i: (nt - 1 - i, 0)
    fixed = lambda i: (0, 0)
    return pl.pallas_call(
        body, name="gate_bwd", grid=(nt,),
        in_specs=[col(0), col(1), col(2), col(3), col(5), prev(1), prev(2), pl.BlockSpec((tm, dssm), rev),
                  pl.BlockSpec((tm, dc + dssm), rev), pl.BlockSpec(conv_w.shape, fixed), pl.BlockSpec((1, dc), fixed),
                  pl.BlockSpec(w_glu.shape, fixed), pl.BlockSpec((1, dssm), fixed)],
        out_specs=[pl.BlockSpec((tm, 6 * dc), rev), pl.BlockSpec((tm, dssm), rev),
                   pl.BlockSpec((2 * SUBLANES, dc), fixed), pl.BlockSpec((dssm, dssm), fixed)],
        out_shape=[jax.ShapeDtypeStruct((t, 6 * dc), BF16), jax.ShapeDtypeStruct((t, dssm), F32),
                   jax.ShapeDtypeStruct((2 * SUBLANES, dc), F32), jax.ShapeDtypeStruct((dssm, dssm), F32)],
        scratch_shapes=[pltpu.VMEM((tm + halo, dc), F32), pltpu.VMEM((tm + halo, dc), F32)],
        compiler_params=_cp(2 * (6 * 4 * tm * dc + 8 * tm * dc + 12 * tm * dc + 4 * tm * dc + 6 * dssm * dssm)
                            + 40 * tm * dc + 8 * MIB, dimension_semantics=("arbitrary",)),
    )(proj, proj, proj, proj, proj, proj, proj, y, dmix, conv_w, conv_b, w_glu, b_glu)


def _allgather_halves(parts, name):
    n = len(parts)
    per = 2 * (N_CHIPS - 1)

    def body(*refs):
        ins, outs = refs[:n], refs[n:2 * n]
        send, recv, local = refs[2 * n:]
        x, y, c = _my_place()
        k = 2 * x + y
        sib = (x, y, 1 - c)
        chips = _other_chips(x, y)

        def rc(t, s, src, dst, to):
            return pltpu.make_async_remote_copy(src_ref=src, dst_ref=dst, send_sem=send.at[t * per + s],
                                                recv_sem=recv.at[t * per + s], device_id=to, device_id_type=MESH)

        mine = [pltpu.make_async_copy(ins[t], outs[t].at[k], local.at[t]) for t in range(n)]
        for cp in mine:
            cp.start()
        sent = []
        for t in range(n):
            for j, (cx, cy) in enumerate(chips):
                cp = rc(t, j, ins[t].at[c], outs[t].at[k, c], (cx, cy, c))
                cp.start()
                sent.append(cp)
        for t in range(n):
            for j, (cx, cy) in enumerate(chips):
                landed = outs[t].at[2 * cx + cy, c]
                rc(t, j, landed, landed, (cx, cy, c)).wait_recv()
                cp = rc(t, N_CHIPS - 1 + j, landed, landed, sib)
                cp.start()
                sent.append(cp)
        for t in range(n):
            for j, (cx, cy) in enumerate(chips):
                other = outs[t].at[2 * cx + cy, 1 - c]
                rc(t, N_CHIPS - 1 + j, other, other, sib).wait_recv()
        for cp in sent:
            cp.wait_send()
        for cp in mine:
            cp.wait()

    any_spec = pl.BlockSpec(memory_space=pl.ANY)
    return pl.pallas_call(
        body, name=name, in_specs=[any_spec] * n, out_specs=[any_spec] * n,
        out_shape=[jax.ShapeDtypeStruct((N_CHIPS,) + p.shape, p.dtype) for p in parts],
        scratch_shapes=[pltpu.SemaphoreType.DMA((n * per,)), pltpu.SemaphoreType.DMA((n * per,)),
                        pltpu.SemaphoreType.DMA((n,))],
        compiler_params=_cp(16 * MIB),
    )(*parts)


def _allgather_flat(v, name):
    r, c = v.shape
    rels = [(dx, dy, dc) for dx in (0, 1) for dy in (0, 1) for dc in (0, 1)][1:]

    def body(v_ref, out_ref, send, recv):
        x, y, cc = _my_place()
        me = 4 * x + 2 * y + cc

        def peer(rel):
            dx, dy, dc = rel
            return (1 - x if dx else x, 1 - y if dy else y, 1 - cc if dc else cc)

        def rc(s, slot, to):
            return pltpu.make_async_remote_copy(src_ref=v_ref, dst_ref=out_ref.at[slot], send_sem=send.at[s],
                                                recv_sem=recv.at[s], device_id=to, device_id_type=MESH)

        sent = []
        for s, rel in enumerate(rels):
            cp = rc(s, me, peer(rel))
            cp.start()
            sent.append(cp)
        out_ref[me] = v_ref[...]
        for s, rel in enumerate(rels):
            px, py, pc = peer(rel)
            rc(s, 4 * px + 2 * py + pc, (px, py, pc)).wait_recv()
        for cp in sent:
            cp.wait_send()

    return pl.pallas_call(
        body, name=name, out_shape=jax.ShapeDtypeStruct((N_DEV, r, c), F32),
        in_specs=[pl.BlockSpec(memory_space=pltpu.VMEM)], out_specs=pl.BlockSpec(memory_space=pltpu.VMEM),
        scratch_shapes=[pltpu.SemaphoreType.DMA((N_DEV - 1,)), pltpu.SemaphoreType.DMA((N_DEV - 1,))],
        compiler_params=_cp((N_DEV + 2) * 4 * r * c + 8 * MIB),
    )(v)


def _rs_pair_exchange(grads):
    n = len(grads)

    def body(*refs):
        ins, outs = refs[:n], refs[n:2 * n]
        send, recv = refs[2 * n:]
        x, y, c = _my_place()
        cps = []
        for t in range(n):
            cp = pltpu.make_async_remote_copy(src_ref=ins[t].at[:, 1 - c], dst_ref=outs[t], send_sem=send.at[t],
                                              recv_sem=recv.at[t], device_id=(x, y, 1 - c), device_id_type=MESH)
            cp.start()
            cps.append(cp)
        for cp in cps:
            cp.wait()

    any_spec = pl.BlockSpec(memory_space=pl.ANY)
    return pl.pallas_call(
        body, name="rs_pair_exchange", in_specs=[any_spec] * n, out_specs=[any_spec] * n,
        out_shape=[jax.ShapeDtypeStruct((g.shape[0],) + g.shape[2:], F32) for g in grads],
        scratch_shapes=[pltpu.SemaphoreType.DMA((n,)), pltpu.SemaphoreType.DMA((n,))],
        compiler_params=_cp(16 * MIB),
    )(*grads)


def _rs_pair_add(place, grad, got, name):
    nk, _, r2, c = grad.shape
    tr = min(TR_ELT, r2)

    def body(place_ref, g_ref, r_ref, o16_ref, o32_ref):
        del place_ref
        s = g_ref[...] + r_ref[...]
        o32_ref[...] = s
        o16_ref[...] = s.astype(BF16)

    blk = pl.BlockSpec((None, tr, c), lambda k, i, p: (k, i, 0))
    return pl.pallas_call(
        body, name=name,
        grid_spec=pltpu.PrefetchScalarGridSpec(
            num_scalar_prefetch=1, grid=(nk, r2 // tr),
            in_specs=[pl.BlockSpec((None, None, tr, c), lambda k, i, p: (k, p[0], i, 0)), blk],
            out_specs=[blk, blk]),
        out_shape=[jax.ShapeDtypeStruct((nk, r2, c), BF16), jax.ShapeDtypeStruct((nk, r2, c), F32)],
        compiler_params=_cp(2 * 14 * tr * c + 8 * MIB),
    )(place, grad, got)


def _rs_chip_exchange(sums16):
    n = len(sums16)
    per = N_CHIPS - 1

    def body(*refs):
        ins, outs = refs[:n], refs[n:2 * n]
        send, recv = refs[2 * n:]
        x, y, c = _my_place()
        chips = _other_chips(x, y)
        cps = []
        for t in range(n):
            for j, (cx, cy) in enumerate(chips):
                cp = pltpu.make_async_remote_copy(src_ref=ins[t].at[2 * cx + cy], dst_ref=outs[t].at[j],
                                                  send_sem=send.at[t * per + j], recv_sem=recv.at[t * per + j],
                                                  device_id=(cx, cy, c), device_id_type=MESH)
                cp.start()
                cps.append(cp)
        for cp in cps:
            cp.wait()

    any_spec = pl.BlockSpec(memory_space=pl.ANY)
    return pl.pallas_call(
        body, name="rs_chip_exchange", in_specs=[any_spec] * n, out_specs=[any_spec] * n,
        out_shape=[jax.ShapeDtypeStruct((per,) + s.shape[1:], BF16) for s in sums16],
        scratch_shapes=[pltpu.SemaphoreType.DMA((n * per,)), pltpu.SemaphoreType.DMA((n * per,))],
        compiler_params=_cp(16 * MIB),
    )(*sums16)


def _rs_chip_add(place, sums32, got, name):
    _, r2, c = sums32.shape
    tr = min(TR_ELT, r2)

    def body(place_ref, s_ref, q_ref, o_ref):
        del place_ref
        o_ref[...] = ((s_ref[...] + q_ref[0].astype(F32)) + q_ref[1].astype(F32)) + q_ref[2].astype(F32)

    return pl.pallas_call(
        body, name=name,
        grid_spec=pltpu.PrefetchScalarGridSpec(
            num_scalar_prefetch=1, grid=(r2 // tr,),
            in_specs=[pl.BlockSpec((None, tr, c), lambda i, p: (p[1], i, 0)),
                      pl.BlockSpec((N_CHIPS - 1, tr, c), lambda i, p: (0, i, 0))],
            out_specs=pl.BlockSpec((tr, c), lambda i, p: (i, 0))),
        out_shape=jax.ShapeDtypeStruct((r2, c), F32),
        compiler_params=_cp(2 * 14 * tr * c + 8 * MIB),
    )(place, sums32, got)


def _rs_pair_share(halves):
    n = len(halves)

    def body(*refs):
        ins, outs = refs[:n], refs[n:2 * n]
        send, recv, local = refs[2 * n:]
        x, y, c = _my_place()
        cps, mine = [], []
        for t in range(n):
            lc = pltpu.make_async_copy(ins[t], outs[t].at[c], local.at[t])
            lc.start()
            mine.append(lc)
            cp = pltpu.make_async_remote_copy(src_ref=ins[t], dst_ref=outs[t].at[c], send_sem=send.at[t],
                                              recv_sem=recv.at[t], device_id=(x, y, 1 - c), device_id_type=MESH)
            cp.start()
            cps.append(cp)
        for t in range(n):
            other = outs[t].at[1 - c]
            pltpu.make_async_remote_copy(src_ref=other, dst_ref=other, send_sem=send.at[t], recv_sem=recv.at[t],
                                         device_id=(x, y, 1 - c), device_id_type=MESH).wait_recv()
        for cp in cps:
            cp.wait_send()
        for lc in mine:
            lc.wait()

    any_spec = pl.BlockSpec(memory_space=pl.ANY)
    return pl.pallas_call(
        body, name="rs_pair_share", in_specs=[any_spec] * n, out_specs=[any_spec] * n,
        out_shape=[jax.ShapeDtypeStruct((2,) + h.shape, F32) for h in halves],
        scratch_shapes=[pltpu.SemaphoreType.DMA((n,)), pltpu.SemaphoreType.DMA((n,)), pltpu.SemaphoreType.DMA((n,))],
        compiler_params=_cp(16 * MIB),
    )(*halves)


_PACK_TILE = SUBLANES * SLAB


def _pack(arrays):
    rows = []
    for a in arrays:
        flat = a.reshape(-1).astype(F32)
        padded = -(-flat.shape[0] // _PACK_TILE) * _PACK_TILE
        rows.append(jnp.pad(flat, (0, padded - flat.shape[0])).reshape(-1, SLAB))
    return jnp.concatenate(rows, axis=0)


def _unpack(packed, shapes):
    out, row = [], 0
    for shp in shapes:
        size = math.prod(shp)
        nrow = -(-size // _PACK_TILE) * SUBLANES
        out.append(packed[row:row + nrow].reshape(-1)[:size].reshape(shp))
        row += nrow
    return out


def kernel(x, norm_pre_g, w_in, conv_w, conv_b, ssm_a_re, ssm_a_im, ssm_log_dt, ssm_b_re, ssm_b_im, ssm_c_re, ssm_c_im, ssm_d, w_glu, b_glu, w_out, norm_post_g, loss_target, m_norm_pre_g, m_w_in, m_conv_w, m_conv_b, m_ssm_a_re, m_ssm_a_im, m_ssm_log_dt, m_ssm_b_re, m_ssm_b_im, m_ssm_c_re, m_ssm_c_im, m_ssm_d, m_w_glu, m_b_glu, m_w_out, m_norm_post_g, v_norm_pre_g, v_w_in, v_conv_w, v_conv_b, v_ssm_a_re, v_ssm_a_im, v_ssm_log_dt, v_ssm_b_re, v_ssm_b_im, v_ssm_c_re, v_ssm_c_im, v_ssm_d, v_w_glu, v_b_glu, v_w_out, v_norm_post_g):
    xs, tgt = x[0], loss_target[0]
    t, d = xs.shape
    dc = conv_b.shape[0]
    dssm = ssm_d.shape[0]
    g, p = ssm_a_re.shape
    h = SSM_H
    nq = dssm // SLAB
    n_shard = w_in.shape[1]
    steps = min(T_SCAN, t) // SUBLANES
    mx, my, mc = _my_place()
    chip = 2 * mx + my
    place = jnp.stack([mc, chip]).astype(jnp.int32)

    halves = lambda a: a.reshape(2, a.shape[0] // 2, a.shape[1])
    win_g, wout_g, wglu_g = _allgather_halves(
        [halves(_cast_bf16(w_in, "cast_w_in")), halves(_cast_bf16(w_out, "cast_w_out")),
         halves(_cast_bf16(w_glu, "cast_w_glu"))], "allgather_weights")
    win_b = win_g.reshape(N_CHIPS, d, n_shard)
    wout_b = wout_g.reshape(dc + dssm, d)
    wglu_b = wglu_g.reshape(dssm, dssm)

    cw_cols = conv_w.shape[1]
    cw_pad = -(-cw_cols // SLAB) * SLAB
    cw_blk = jnp.zeros((SUBLANES, cw_pad), F32).at[:conv_w.shape[0], :cw_cols].set(conv_w)
    cw_all = _allgather_flat(cw_blk, "allgather_conv_w")
    conv_w_full = jnp.concatenate([cw_all[2 * k, :, :cw_cols] for k in range(N_CHIPS)], axis=1)

    expand = jnp.repeat(jnp.eye(p, dtype=F32), h, axis=1)
    b_re2, b_im2 = ssm_b_re.reshape(g, p * h), ssm_b_im.reshape(g, p * h)
    log_dt2 = ssm_log_dt.reshape(g, 1)
    lbr, lbi, bbr2, bbi2, pw3 = _zoh_fwd(ssm_a_re, ssm_a_im, log_dt2, b_re2, b_im2, expand, steps)
    lam = jnp.stack([lbr.reshape(g * p), lbi.reshape(g * p)])
    pw = pw3.reshape(2, g * p)
    to_slab_b = lambda b2: _blockdiag(b2.reshape(nq, GROUPS_PER_SLAB, p, h).transpose(0, 1, 3, 2))
    bq = jnp.concatenate([to_slab_b(bbr2), to_slab_b(bbi2)], axis=2).astype(BF16)
    to_slab_c = lambda c3: _blockdiag(c3.reshape(nq, GROUPS_PER_SLAB, h, p).transpose(0, 1, 3, 2))
    cq = jnp.concatenate([to_slab_c(ssm_c_re), to_slab_c(-ssm_c_im)], axis=1).astype(BF16)

    g_pre2, g_post2 = norm_pre_g.reshape(1, d), norm_post_g.reshape(1, d)
    conv_b2, b_glu2, d_skip2 = conv_b.reshape(1, dc), b_glu.reshape(1, dssm), ssm_d.reshape(1, dssm)
    u_block = 4 * dc // dssm

    hb = _prenorm(xs, g_pre2)
    proj = _matmul_nn(hb, win_b, "inproj")
    y, cin = _ssm_fwd(proj, u_block, bq, cq, lam, pw, d_skip2)
    mix = _gate_fwd(proj, y, conv_w_full, conv_b2, wglu_b, b_glu2, dc)
    loss_blk, dout, dob, dmix, gg_post = _outproj(mix, wout_b, xs, tgt, g_post2)

    gw_out = _matmul_tn(mix, dob, 1, "grad_w_out")
    dproj, dy, gsmall, gw_glu = _gate_bwd(proj, y, dmix, conv_w_full, conv_b2, wglu_b, b_glu2, dc)
    dproj, gb_dense, gc_dense, glam, gd = _ssm_bwd(proj, u_block, dy, cin, dproj, bq, cq, lam, pw, d_skip2)
    gx, gg_pre = _dh_prenorm_bwd(dproj, win_b, xs, dout, g_pre2)
    gw_in = _matmul_tn(hb, dproj, N_CHIPS, "grad_w_in")

    gb4 = gb_dense.reshape(nq, SLAB, 2, SLAB_STATES)
    g_bbr2 = _blockdiag_take(gb4[:, :, 0, :], h, p).transpose(0, 1, 3, 2).reshape(g, p * h)
    g_bbi2 = _blockdiag_take(gb4[:, :, 1, :], h, p).transpose(0, 1, 3, 2).reshape(g, p * h)
    gc4 = gc_dense.reshape(nq, SLAB, 2, SLAB_STATES)
    g_c_re = _blockdiag_take(gc4[:, :, 0, :], h, p).reshape(g, h, p)
    g_c_im = -_blockdiag_take(gc4[:, :, 1, :], h, p).reshape(g, h, p)
    g_a_re, g_a_im, g_ld, g_b_re2, g_b_im2 = _zoh_bwd(
        ssm_a_re, ssm_a_im, log_dt2, b_re2, b_im2, expand,
        glam[0].reshape(g, p), glam[1].reshape(g, p), g_bbr2, g_bbi2)

    small_names = ["norm_pre_g", "conv_w", "conv_b", "ssm_a_re", "ssm_a_im", "ssm_log_dt", "ssm_b_re", "ssm_b_im",
                   "ssm_c_re", "ssm_c_im", "ssm_d", "b_glu", "norm_post_g", "loss"]
    small_g = {
        "norm_pre_g": gg_pre[0], "conv_w": gsmall[0:3], "conv_b": gsmall[3], "ssm_a_re": g_a_re, "ssm_a_im": g_a_im,
        "ssm_log_dt": g_ld.reshape(g), "ssm_b_re": g_b_re2.reshape(g, p, h), "ssm_b_im": g_b_im2.reshape(g, p, h),
        "ssm_c_re": g_c_re, "ssm_c_im": g_c_im, "ssm_d": gd[0], "b_glu": gsmall[4], "norm_post_g": gg_post[0],
        "loss": loss_blk[0, 0:1],
    }
    zeros_cw = jnp.zeros((conv_w.shape[0], dc), F32)
    one0 = jnp.zeros((1,), F32)
    small_w = dict(norm_pre_g=norm_pre_g, conv_w=zeros_cw, conv_b=conv_b, ssm_a_re=ssm_a_re, ssm_a_im=ssm_a_im,
                   ssm_log_dt=ssm_log_dt, ssm_b_re=ssm_b_re, ssm_b_im=ssm_b_im, ssm_c_re=ssm_c_re, ssm_c_im=ssm_c_im,
                   ssm_d=ssm_d, b_glu=b_glu, norm_post_g=norm_post_g, loss=one0)
    small_m = dict(norm_pre_g=m_norm_pre_g, conv_w=zeros_cw, conv_b=m_conv_b, ssm_a_re=m_ssm_a_re, ssm_a_im=m_ssm_a_im,
                   ssm_log_dt=m_ssm_log_dt, ssm_b_re=m_ssm_b_re, ssm_b_im=m_ssm_b_im, ssm_c_re=m_ssm_c_re,
                   ssm_c_im=m_ssm_c_im, ssm_d=m_ssm_d, b_glu=m_b_glu, norm_post_g=m_norm_post_g, loss=one0)
    small_v = dict(norm_pre_g=v_norm_pre_g, conv_w=zeros_cw, conv_b=v_conv_b, ssm_a_re=v_ssm_a_re, ssm_a_im=v_ssm_a_im,
                   ssm_log_dt=v_ssm_log_dt, ssm_b_re=v_ssm_b_re, ssm_b_im=v_ssm_b_im, ssm_c_re=v_ssm_c_re,
                   ssm_c_im=v_ssm_c_im, ssm_d=v_ssm_d, b_glu=v_b_glu, norm_post_g=v_norm_post_g, loss=one0)
    shapes = [small_w[nm].shape for nm in small_names]
    gathered = _allgather_flat(_pack([small_g[nm] for nm in small_names]), "allgather_small_grads")
    packs = _adamw_small(gathered, _pack([small_w[nm] for nm in small_names]),
                         _pack([small_m[nm] for nm in small_names]), _pack([small_v[nm] for nm in small_names]))
    sg, sd, sm, sv = [dict(zip(small_names, _unpack(pk, shapes))) for pk in packs]
    loss = sg["loss"][0]

    g_cw = lax.dynamic_slice_in_dim(sg["conv_w"], chip * cw_cols, cw_cols, axis=1)
    pad_cw = lambda a: jnp.zeros((SUBLANES, cw_pad), F32).at[:a.shape[0], :cw_cols].set(a)
    cut_cw = lambda a: a[:conv_w.shape[0], :cw_cols]
    d_cw, m_cw, v_cw = [cut_cw(a) for a in _adamw(pad_cw(conv_w), pad_cw(g_cw), pad_cw(m_conv_w), pad_cw(v_conv_w),
                                                  "adamw_conv_w")]

    big =[gw_in.reshape(N_CHIPS, 2, d // 2, n_shard),
           gw_out.reshape(N_CHIPS, 2, (dc + dssm) // (2 * N_CHIPS), d),
           gw_glu.reshape(N_CHIPS, 2, dssm // (2 * N_CHIPS), dssm)]
    tags =["w_in", "w_out", "w_glu"]
    got = _rs_pair_exchange(big)
    sums = [_rs_pair_add(place, gt, rt, "rs_pair_add_" + tg) for gt, rt, tg in zip(big, got, tags)]
    recvd = _rs_chip_exchange([s16 for s16, _ in sums])
    mine = [_rs_chip_add(place, s32, qt, "rs_chip_add_" + tg) for (_, s32), qt, tg in zip(sums, recvd, tags)]
    full = _rs_pair_share(mine)
    g_win, g_wout, g_wglu = [f.reshape(2 * f.shape[1], f.shape[2]) for f in full]
    d_win, m_win, v_win = _adamw(w_in, g_win, m_w_in, v_w_in, "adamw_w_in")
    d_wout, m_wout, v_wout = _adamw(w_out, g_wout, m_w_out, v_w_out, "adamw_w_out")
    d_wglu, m_wglu, v_wglu = _adamw(w_glu, g_wglu, m_w_glu, v_w_glu, "adamw_w_glu")

    order = ["norm_pre_g", "w_in", "conv_w", "conv_b", "ssm_a_re", "ssm_a_im", "ssm_log_dt", "ssm_b_re", "ssm_b_im",
             "ssm_c_re", "ssm_c_im", "ssm_d", "w_glu", "b_glu", "w_out", "norm_post_g"]
    grads, deltas, new_m, new_v = dict(sg), dict(sd), dict(sm), dict(sv)
    grads.update(w_in=g_win, w_out=g_wout, w_glu=g_wglu, conv_w=g_cw)
    deltas.update(w_in=d_win, w_out=d_wout, w_glu=d_wglu, conv_w=d_cw)
    new_m.update(w_in=m_win, w_out=m_wout, w_glu=m_wglu, conv_w=m_cw)
    new_v.update(w_in=v_win, w_out=v_wout, w_glu=v_wglu, conv_w=v_cw)
    return (loss, gx[None], *[grads[nm] for nm in order], *[deltas[nm] for nm in order],
            *[new_m[nm] for nm in order], *[new_v[nm] for nm in order])
```

```python
import functools
import math

import jax
import jax.numpy as jnp
from jax import lax
from jax.experimental import pallas as pl
from jax.experimental.pallas import tpu as pltpu

F32 = jnp.float32
BF16 = jnp.bfloat16
MESH = pl.DeviceIdType.MESH

EPS = 1e-6
SSM_H = 16
SSM_P = 64
GROUPS_PER_SLAB = 8
SLAB = 128
SLAB_STATES = GROUPS_PER_SLAB * SSM_P
N_CHIPS = 4
N_DEV = 8

ADAM_LR = 0.001
ADAM_B1 = 0.9
ADAM_B2 = 0.999
ADAM_EPS = 1e-08
ADAM_WD = 0.01
ADAM_STEP = 10

MIB = 1024 * 1024
VMEM_CAP = 48 * MIB
SUBLANES = 8

TM_NORM = 512
TM_PROJ = 512
TM_GATE = 256
TM_OUT = 256
TM_DH = 256
T_SCAN = 256
TK_TN = 512
SCAN_UNROLL = 4
TM_TN = 1024
TR_ELT = 256


def _cp(vmem_bytes, **kw):
    return pltpu.CompilerParams(vmem_limit_bytes=int(min(VMEM_CAP, max(16 * MIB, vmem_bytes))), **kw)


def _my_place():
    return lax.axis_index("x"), lax.axis_index("y"), lax.axis_index("c")


def _other_chips(x, y):
    return [(1 - x, y), (x, 1 - y), (1 - x, 1 - y)]


def _silu(z):
    s = jax.nn.sigmoid(z)
    return z * s, s


def _dsilu(z, s):
    return s * (1.0 + z * (1.0 - s))


_GELU_K = math.sqrt(2.0 / math.pi)
_GELU_C = 0.044715


def _gelu(y):
    th = jnp.tanh(_GELU_K * (y + _GELU_C * y * y * y))
    return 0.5 * y * (1.0 + th), th


def _dgelu(y, th):
    return 0.5 * (1.0 + th) + 0.5 * y * (1.0 - th * th) * _GELU_K * (1.0 + 3.0 * _GELU_C * y * y)


def _cast_bf16(w, name):
    r, c = w.shape
    tr = min(TR_ELT, r)

    def body(w_ref, o_ref):
        o_ref[...] = w_ref[...].astype(BF16)

    return pl.pallas_call(
        body, name=name, grid=(r // tr,),
        in_specs=[pl.BlockSpec((tr, c), lambda i: (i, 0))],
        out_specs=pl.BlockSpec((tr, c), lambda i: (i, 0)),
        out_shape=jax.ShapeDtypeStruct((r, c), BF16),
        compiler_params=_cp(12 * tr * c),
    )(w)


def _prenorm(x, g):
    t, d = x.shape
    tm = min(TM_NORM, t)

    def body(x_ref, g_ref, h_ref):
        xv = x_ref[...]
        r = lax.rsqrt(jnp.mean(xv * xv, axis=-1, keepdims=True) + EPS)
        h_ref[...] = (xv * r * g_ref[...]).astype(BF16)

    return pl.pallas_call(
        body, name="prenorm", grid=(t // tm,),
        in_specs=[pl.BlockSpec((tm, d), lambda i: (i, 0)), pl.BlockSpec((1, d), lambda i: (0, 0))],
        out_specs=pl.BlockSpec((tm, d), lambda i: (i, 0)),
        out_shape=jax.ShapeDtypeStruct((t, d), BF16),
        compiler_params=_cp(20 * tm * d),
    )(x, g)


def _matmul_nn(a, b, name):
    t, k = a.shape
    nb, _, n = b.shape
    tm = min(TM_PROJ, t)

    def body(a_ref, b_ref, o_ref):
        o_ref[...] = jnp.dot(a_ref[...], b_ref[...], preferred_element_type=F32)

    return pl.pallas_call(
        body, name=name, grid=(nb, t // tm),
        in_specs=[pl.BlockSpec((tm, k), lambda j, i: (i, 0)), pl.BlockSpec((None, k, n), lambda j, i: (j, 0, 0))],
        out_specs=pl.BlockSpec((tm, n), lambda j, i: (i, j)),
        out_shape=jax.ShapeDtypeStruct((t, nb * n), F32),
        compiler_params=_cp(2 * (2 * tm * k + 2 * k * n + 4 * tm * n) + 4 * MIB),
    )(a, b)


def _matmul_tn(a, b, nb, name):
    t, m = a.shape
    n = b.shape[1] // nb
    tk = min(TK_TN, t)
    tma = min(TM_TN, m)

    def body(a_ref, b_ref, o_ref):
        @pl.when(pl.program_id(2) == 0)
        def _():
            o_ref[...] = jnp.zeros_like(o_ref)

        o_ref[...] += lax.dot_general(a_ref[...], b_ref[...], (((0,), (0,)), ((), ())), preferred_element_type=F32)

    return pl.pallas_call(
        body, name=name, grid=(nb, m // tma, t // tk),
        in_specs=[pl.BlockSpec((tk, tma), lambda j, i, k: (k, i)), pl.BlockSpec((tk, n), lambda j, i, k: (k, j))],
        out_specs=pl.BlockSpec((None, tma, n), lambda j, i, k: (j, i, 0)),
        out_shape=jax.ShapeDtypeStruct((nb, m, n), F32),
        compiler_params=_cp(2 * (2 * tk * tma + 2 * tk * n + 4 * tma * n) + 8 * MIB),
    )(a, b)


def _outproj(mix, w_out, x, tgt, g_post):
    t, dm = mix.shape
    d = w_out.shape[1]
    tm = min(TM_OUT, t)

    def body(mix_ref, w_ref, x_ref, t_ref, g_ref, loss_ref, dout_ref, do_ref, dmix_ref, gg_ref):
        @pl.when(pl.program_id(0) == 0)
        def _():
            loss_ref[...] = jnp.zeros_like(loss_ref)
            gg_ref[...] = jnp.zeros_like(gg_ref)

        w = w_ref[...]
        o = jnp.dot(mix_ref[...], w, preferred_element_type=F32)
        r = lax.rsqrt(jnp.mean(o * o, axis=-1, keepdims=True) + EPS)
        nh = o * r
        g = g_ref[...]
        e = x_ref[...] + nh * g - t_ref[...]
        loss_ref[...] += jnp.sum(e * e) * (0.5 / d)
        dout = e * (1.0 / d)
        dout_ref[...] = dout
        gg_ref[0:1, :] += jnp.sum(dout * nh, axis=0, keepdims=True)
        dn = dout * g
        do = r * (dn - nh * jnp.mean(dn * nh, axis=-1, keepdims=True))
        dob = do.astype(BF16)
        do_ref[...] = dob
        dmix_ref[...] = lax.dot_general(dob, w, (((1,), (1,)), ((), ())), preferred_element_type=F32)

    row = lambda i: (i, 0)
    fixed = lambda i: (0, 0)
    return pl.pallas_call(
        body, name="outproj", grid=(t // tm,),
        in_specs=[pl.BlockSpec((tm, dm), row), pl.BlockSpec((dm, d), fixed), pl.BlockSpec((tm, d), row),
                  pl.BlockSpec((tm, d), row), pl.BlockSpec((1, d), fixed)],
        out_specs=[pl.BlockSpec((SUBLANES, SLAB), fixed), pl.BlockSpec((tm, d), row), pl.BlockSpec((tm, d), row),
                   pl.BlockSpec((tm, dm), row), pl.BlockSpec((SUBLANES, d), fixed)],
        out_shape=[jax.ShapeDtypeStruct((SUBLANES, SLAB), F32), jax.ShapeDtypeStruct((t, d), F32),
                   jax.ShapeDtypeStruct((t, d), BF16), jax.ShapeDtypeStruct((t, dm), F32),
                   jax.ShapeDtypeStruct((SUBLANES, d), F32)],
        compiler_params=_cp(2 * (2 * dm * d + tm * (2 * dm + 4 * d * 3 + 2 * d + 4 * dm)) + 16 * MIB),
    )(mix, w_out, x, tgt, g_post)


def _dh_prenorm_bwd(dproj, w_in, x, dout, g_pre):
    t, d = x.shape
    nb, _, n = w_in.shape
    tm = min(TM_DH, t)

    def body(dp_ref, w_ref, x_ref, dout_ref, g_ref, dx_ref, gg_ref):
        @pl.when(pl.program_id(0) == 0)
        def _():
            gg_ref[...] = jnp.zeros_like(gg_ref)

        dh = None
        for k in range(nb):
            part = lax.dot_general(dp_ref[:, k * n:(k + 1) * n], w_ref[k], (((1,), (1,)), ((), ())),
                                   preferred_element_type=F32)
            dh = part if dh is None else dh + part
        xv = x_ref[...]
        r = lax.rsqrt(jnp.mean(xv * xv, axis=-1, keepdims=True) + EPS)
        xh = xv * r
        gg_ref[0:1, :] += jnp.sum(dh * xh, axis=0, keepdims=True)
        dg = dh * g_ref[...]
        dx_ref[...] = dout_ref[...] + r * (dg - xh * jnp.mean(dg * xh, axis=-1, keepdims=True))

    row = lambda i: (i, 0)
    fixed = lambda i: (0, 0)
    w_spec = pl.BlockSpec(w_in.shape, lambda i: (0, 0, 0), pipeline_mode=pl.Buffered(1))
    return pl.pallas_call(
        body, name="dh_prenorm_bwd", grid=(t // tm,),
        in_specs=[pl.BlockSpec((tm, nb * n), row), w_spec,
                  pl.BlockSpec((tm, d), row), pl.BlockSpec((tm, d), row), pl.BlockSpec((1, d), fixed)],
        out_specs=[pl.BlockSpec((tm, d), row), pl.BlockSpec((SUBLANES, d), fixed)],
        out_shape=[jax.ShapeDtypeStruct((t, d), F32), jax.ShapeDtypeStruct((SUBLANES, d), F32)],
        compiler_params=_cp(2 * nb * d * n + 2 * (2 * tm * nb * n + 12 * tm * d) + 16 * tm * d + 4 * MIB),
    )(dproj, w_in, x, dout, g_pre)


def _adamw(w, g, m, v, name):
    r, c = w.shape
    tr = min(TR_ELT, r)
    c1 = 1.0 - ADAM_B1 ** ADAM_STEP
    c2 = 1.0 - ADAM_B2 ** ADAM_STEP

    def body(w_ref, g_ref, m_ref, v_ref, d_ref, mo_ref, vo_ref):
        gv = g_ref[...]
        mn = ADAM_B1 * m_ref[...] + (1.0 - ADAM_B1) * gv
        vn = ADAM_B2 * v_ref[...] + (1.0 - ADAM_B2) * (gv * gv)
        d_ref[...] = -ADAM_LR * ((mn / c1) / (jnp.sqrt(vn / c2) + ADAM_EPS) + ADAM_WD * w_ref[...])
        mo_ref[...] = mn
        vo_ref[...] = vn

    spec = pl.BlockSpec((tr, c), lambda i: (i, 0))
    sds = jax.ShapeDtypeStruct((r, c), F32)
    return pl.pallas_call(
        body, name=name, grid=(r // tr,), in_specs=[spec] * 4, out_specs=[spec] * 3, out_shape=[sds] * 3,
        compiler_params=_cp(2 * 7 * 4 * tr * c + 8 * MIB),
    )(w, g, m, v)


def _adamw_small(gathered, w, m, v):
    r, c = w.shape
    c1 = 1.0 - ADAM_B1 ** ADAM_STEP
    c2 = 1.0 - ADAM_B2 ** ADAM_STEP

    def body(gat_ref, w_ref, m_ref, v_ref, g_ref, d_ref, mo_ref, vo_ref):
        gv = gat_ref[0]
        for dev in range(1, N_DEV):
            gv = gv + gat_ref[dev]
        g_ref[...] = gv
        mn = ADAM_B1 * m_ref[...] + (1.0 - ADAM_B1) * gv
        vn = ADAM_B2 * v_ref[...] + (1.0 - ADAM_B2) * (gv * gv)
        d_ref[...] = -ADAM_LR * ((mn / c1) / (jnp.sqrt(vn / c2) + ADAM_EPS) + ADAM_WD * w_ref[...])
        mo_ref[...] = mn
        vo_ref[...] = vn

    sds = jax.ShapeDtypeStruct((r, c), F32)
    return pl.pallas_call(
        body, name="adamw_small", out_shape=[sds] * 4,
        compiler_params=_cp((N_DEV + 12) * 4 * r * c + 8 * MIB),
    )(gathered, w, m, v)


def _zoh(a_re, a_im, log_dt, b_re2, b_im2, expand):
    dt = jnp.exp(log_dt)
    mag = jnp.exp(a_re * dt)
    lbr, lbi = mag * jnp.cos(a_im * dt), mag * jnp.sin(a_im * dt)
    nr, ni = lbr - 1.0, lbi
    den = a_re * a_re + a_im * a_im
    qr = (nr * a_re + ni * a_im) / den
    qi = (ni * a_re - nr * a_im) / den
    qr2 = jnp.dot(qr, expand, precision=lax.Precision.HIGHEST, preferred_element_type=F32)
    qi2 = jnp.dot(qi, expand, precision=lax.Precision.HIGHEST, preferred_element_type=F32)
    return lbr, lbi, qr2 * b_re2 - qi2 * b_im2, qr2 * b_im2 + qi2 * b_re2


def _zoh_fwd(a_re, a_im, log_dt, b_re2, b_im2, expand, power):
    g, p = a_re.shape

    def body(ar_ref, ai_ref, ld_ref, br_ref, bi_ref, e_ref, lbr_ref, lbi_ref, bbr_ref, bbi_ref, pw_ref):
        ar, ai, ld = ar_ref[...], ai_ref[...], ld_ref[...]
        lbr, lbi, bbr, bbi = _zoh(ar, ai, ld, br_ref[...], bi_ref[...], e_ref[...])
        lbr_ref[...], lbi_ref[...], bbr_ref[...], bbi_ref[...] = lbr, lbi, bbr, bbi
        dt = jnp.exp(ld) * float(power)
        mag = jnp.exp(ar * dt)
        pw_ref[0] = mag * jnp.cos(ai * dt)
        pw_ref[1] = mag * jnp.sin(ai * dt)

    gp = jax.ShapeDtypeStruct((g, p), F32)
    gph = jax.ShapeDtypeStruct(b_re2.shape, F32)
    return pl.pallas_call(
        body, name="zoh_fwd", out_shape=[gp, gp, gph, gph, jax.ShapeDtypeStruct((2, g, p), F32)],
        compiler_params=_cp(16 * MIB),
    )(a_re, a_im, log_dt, b_re2, b_im2, expand)


def _zoh_bwd(a_re, a_im, log_dt, b_re2, b_im2, expand, g_lbr, g_lbi, g_bbr, g_bbi):
    def body(ar_ref, ai_ref, ld_ref, br_ref, bi_ref, e_ref, c0, c1, c2, c3, gar, gai, gld, gbr, gbi):
        e = e_ref[...]
        _, vjp = jax.vjp(lambda a, b, c, d, f: _zoh(a, b, c, d, f, e),
                         ar_ref[...], ai_ref[...], ld_ref[...], br_ref[...], bi_ref[...])
        gar[...], gai[...], gld[...], gbr[...], gbi[...] = vjp((c0[...], c1[...], c2[...], c3[...]))

    sds = lambda a: jax.ShapeDtypeStruct(a.shape, F32)
    return pl.pallas_call(
        body, name="zoh_bwd", out_shape=[sds(a_re), sds(a_im), sds(log_dt), sds(b_re2), sds(b_im2)],
        compiler_params=_cp(16 * MIB),
    )(a_re, a_im, log_dt, b_re2, b_im2, expand, g_lbr, g_lbi, g_bbr, g_bbi)


def _blockdiag(blocks):
    nq, _, r, c = blocks.shape
    eye = jnp.eye(GROUPS_PER_SLAB, dtype=blocks.dtype)
    out = blocks[:, :, :, None, :] * eye[None, :, None, :, None]
    return out.reshape(nq, GROUPS_PER_SLAB * r, GROUPS_PER_SLAB * c)


def _blockdiag_take(dense, r, c):
    nq = dense.shape[0]
    eye = jnp.eye(GROUPS_PER_SLAB, dtype=dense.dtype)
    d6 = dense.reshape(nq, GROUPS_PER_SLAB, r, GROUPS_PER_SLAB, c)
    return jnp.sum(d6 * eye[None, :, None, :, None], axis=3)


def _scan_slab(s_ref, row0, q, lam_ref, pw_ref, car_ref, steps, reverse, prev_ref=None, prev_row0=0, glam_ref=None):
    sign = -1.0 if reverse else 1.0
    half = SLAB_STATES // SLAB
    cols = [(q * 2 * half + m, q * 2 * half + half + m, q * SLAB_STATES + m * SLAB) for m in range(half)]
    nm = len(cols)
    full = (SUBLANES, SLAB)
    lam = [(jnp.broadcast_to(lam_ref[0:1, pl.ds(cl, SLAB)], full),
            jnp.broadcast_to(sign * lam_ref[1:2, pl.ds(cl, SLAB)], full)) for (_, _, cl) in cols]

    def step_rows(jj, base):
        j = (steps - 1 - jj) if reverse else jj
        return j, pl.ds(pl.multiple_of(base + j * SUBLANES, SUBLANES), SUBLANES)

    def pass1(jj, car):
        _, rows = step_rows(jj, row0)
        out = []
        for m, (cr, ci, _) in enumerate(cols):
            sr, si = car[2 * m], car[2 * m + 1]
            lr, li = lam[m]
            nr = lr * sr - li * si + s_ref[cr, rows, :]
            ni = lr * si + li * sr + s_ref[ci, rows, :]
            s_ref[cr, rows, :] = nr
            s_ref[ci, rows, :] = ni
            out += [nr, ni]
        return tuple(out)

    ends = lax.fori_loop(0, steps, pass1, tuple(jnp.zeros(full, F32) for _ in range(2 * nm)), unroll=SCAN_UNROLL)

    entry = []
    for m, (cr, ci, cl) in enumerate(cols):
        ljr = pw_ref[0:1, pl.ds(cl, SLAB)]
        lji = sign * pw_ref[1:2, pl.ds(cl, SLAB)]
        c_r = car_ref[cr, 0:1, :]
        c_i = car_ref[ci, 0:1, :]
        rows_r, rows_i = [None] * SUBLANES, [None] * SUBLANES
        order = range(SUBLANES - 1, -1, -1) if reverse else range(SUBLANES)
        for b in order:
            rows_r[b], rows_i[b] = c_r, c_i
            e_r, e_i = ends[2 * m][b:b + 1], ends[2 * m + 1][b:b + 1]
            c_r, c_i = ljr * c_r - lji * c_i + e_r, ljr * c_i + lji * c_r + e_i
        car_ref[cr, 0:1, :] = c_r
        car_ref[ci, 0:1, :] = c_i
        entry.append((jnp.concatenate(rows_r, axis=0), jnp.concatenate(rows_i, axis=0)))

    def pass2(jj, carry):
        j, rows = step_rows(jj, row0)
        decayed, acc = carry[:2 * nm], carry[2 * nm:]
        out_d, out_a = [], []
        for m, (cr, ci, cl) in enumerate(cols):
            lr, li = lam[m]
            dr, di = decayed[2 * m], decayed[2 * m + 1]
            dr, di = lr * dr - li * di, lr * di + li * dr
            nr = s_ref[cr, rows, :] + dr
            ni = s_ref[ci, rows, :] + di
            s_ref[cr, rows, :] = nr
            s_ref[ci, rows, :] = ni
            out_d += [dr, di]
            if prev_ref is not None:
                prow = pl.ds(pl.multiple_of(prev_row0 + (j - 1) * SUBLANES, SUBLANES), SUBLANES)
                qr = prev_ref[cr, prow, :]
                qi = prev_ref[ci, prow, :]
                out_a += [acc[2 * m] + (nr * qr + ni * qi), acc[2 * m + 1] + (ni * qr - nr * qi)]
        return tuple(out_d) + tuple(out_a)

    n_acc = 2 * nm if prev_ref is not None else 0
    init = tuple(e for pair in entry for e in pair) + tuple(jnp.zeros(full, F32) for _ in range(n_acc))
    accs = lax.fori_loop(0, steps, pass2, init, unroll=SCAN_UNROLL)[2 * nm:]
    if prev_ref is not None:
        for m, (_, _, cl) in enumerate(cols):
            glam_ref[0:1, pl.ds(cl, SLAB)] += jnp.sum(accs[2 * m], axis=0, keepdims=True)
            glam_ref[1:2, pl.ds(cl, SLAB)] += jnp.sum(accs[2 * m + 1], axis=0, keepdims=True)


def _permute_rows_f32(perm_bf16, v):
    hi = v.astype(BF16)
    lo = (v - hi.astype(F32)).astype(BF16)
    return (jnp.dot(perm_bf16, hi, preferred_element_type=F32) + jnp.dot(perm_bf16, lo, preferred_element_type=F32))


def _put_slab(s_ref, rows, q, val):
    per = 2 * SLAB_STATES // SLAB
    for i in range(per):
        s_ref[q * per + i, rows, :] = val[:, i * SLAB:(i + 1) * SLAB]


def _get_slab(s_ref, rows, q):
    per = 2 * SLAB_STATES // SLAB
    return jnp.concatenate([s_ref[q * per + i, rows, :] for i in range(per)], axis=1)


def _step_major_perm(tt):
    r = jnp.arange(tt)
    held = (r % SUBLANES) * (tt // SUBLANES) + r // SUBLANES
    return held[:, None] == r[None, :]


def _ssm_fwd(proj, u_block, bq, cq, lam, pw, d_skip):
    t = proj.shape[0]
    nq, ds, w2 = bq.shape
    assert ds == SLAB and w2 == 2 * SLAB_STATES
    dssm = nq * SLAB
    width = nq * w2
    ntile = width // SLAB
    tt = min(T_SCAN, t)
    steps = tt // SUBLANES
    perm = _step_major_perm(tt)
    pm, pmt = perm.astype(BF16), perm.T.astype(BF16)

    def body(u_ref, pm_ref, pmt_ref, bq_ref, cq_ref, lam_ref, pw_ref, d_ref, y_ref, cin_ref, s_ref, car_ref):
        @pl.when(pl.program_id(0) == 0)
        def _():
            car_ref[...] = jnp.zeros_like(car_ref)

        cin_ref[...] = jnp.broadcast_to(car_ref[:, 0:1, :], cin_ref.shape)
        u = u_ref[...]
        ub = jnp.dot(pm_ref[...], u.astype(BF16), preferred_element_type=F32).astype(BF16)
        everything = slice(None)
        for q in range(nq):
            _put_slab(s_ref, everything, q,
                      jnp.dot(ub[:, q * SLAB:(q + 1) * SLAB], bq_ref[q], preferred_element_type=F32))
        for q in range(nq):
            _scan_slab(s_ref, 0, q, lam_ref, pw_ref, car_ref, steps, reverse=False)
        y_sm = jnp.concatenate(
            [jnp.dot(_get_slab(s_ref, everything, q).astype(BF16), cq_ref[q], preferred_element_type=F32)
             for q in range(nq)], axis=1)
        y_ref[...] = _permute_rows_f32(pmt_ref[...], y_sm) + d_ref[...] * u

    c3 = lambda i: (0, 0, 0)
    c2 = lambda i: (0, 0)
    return pl.pallas_call(
        body, name="ssm_fwd", grid=(t // tt,),
        in_specs=[pl.BlockSpec((tt, dssm), lambda i: (i, u_block)), pl.BlockSpec((tt, tt), c2),
                  pl.BlockSpec((tt, tt), c2), pl.BlockSpec(bq.shape, c3),
                  pl.BlockSpec(cq.shape, c3), pl.BlockSpec(lam.shape, c2), pl.BlockSpec(pw.shape, c2),
                  pl.BlockSpec((1, dssm), c2)],
        out_specs=[pl.BlockSpec((tt, dssm), lambda i: (i, 0)),
                   pl.BlockSpec((None, ntile, SUBLANES, SLAB), lambda i: (i, 0, 0, 0))],
        out_shape=[jax.ShapeDtypeStruct((t, dssm), F32), jax.ShapeDtypeStruct((t // tt, ntile, SUBLANES, SLAB), F32)],
        scratch_shapes=[pltpu.VMEM((ntile, tt, SLAB), F32), pltpu.VMEM((ntile, SUBLANES, SLAB), F32)],
        compiler_params=_cp(2 * (8 * tt * dssm + 4 * nq * ds * w2) + 4 * tt * width + 16 * MIB,
                            dimension_semantics=("arbitrary",)),
    )(proj, pm, pmt, bq, cq, lam, pw, d_skip)


def _ssm_bwd(proj, u_block, dy, cin, dproj, bq, cq, lam, pw, d_skip):
    t = proj.shape[0]
    nq, ds, w2 = bq.shape
    dssm = nq * SLAB
    width = nq * w2
    ntile = width // SLAB
    tt = min(T_SCAN, t)
    nt = t // tt
    steps = tt // SUBLANES
    halo = SUBLANES
    perm = _step_major_perm(tt)
    pm, pmt = perm.astype(BF16), perm.T.astype(BF16)

    def body(u_ref, dy_ref, cin_ref, dp_any, pm_ref, pmt_ref, bq_ref, cq_ref, lam_ref, pw_ref, d_ref,
             du_ref, gb_ref, gc_ref, glam_ref, gd_ref, s_ref, gs_ref, car_f, car_b):
        del dp_any

        @pl.when(pl.program_id(0) == 0)
        def _():
            car_b[...] = jnp.zeros_like(car_b)
            gb_ref[...] = jnp.zeros_like(gb_ref)
            gc_ref[...] = jnp.zeros_like(gc_ref)
            glam_ref[...] = jnp.zeros_like(glam_ref)
            gd_ref[...] = jnp.zeros_like(gd_ref)

        u = u_ref[...]
        dyv = dy_ref[...]
        gd_ref[0:1, :] += jnp.sum(dyv * u, axis=0, keepdims=True)
        pmv = pm_ref[...]
        ub = jnp.dot(pmv, u.astype(BF16), preferred_element_type=F32).astype(BF16)
        dyb = jnp.dot(pmv, dyv.astype(BF16), preferred_element_type=F32).astype(BF16)
        car_f[...] = cin_ref[...]
        data = slice(halo, halo + tt)
        everything = slice(None)
        for q in range(nq):
            _put_slab(s_ref, data, q, jnp.dot(ub[:, q * SLAB:(q + 1) * SLAB], bq_ref[q], preferred_element_type=F32))
        for q in range(nq):
            _scan_slab(s_ref, halo, q, lam_ref, pw_ref, car_f, steps, reverse=False)
        last_step = s_ref[:, halo + tt - SUBLANES:halo + tt, :]
        s_ref[:, 0:halo, :] = jnp.concatenate([cin_ref[:, 0:1, :], last_step[:, 0:SUBLANES - 1, :]], axis=1)
        tn = (((0,), (0,)), ((), ()))
        nt_dims = (((1,), (1,)), ((), ()))
        for q in range(nq):
            sl = slice(q * SLAB, (q + 1) * SLAB)
            gc_ref[q] += lax.dot_general(dyb[:, sl], _get_slab(s_ref, data, q).astype(BF16), tn,
                                         preferred_element_type=F32)
            _put_slab(gs_ref, everything, q,
                      lax.dot_general(dyb[:, sl], cq_ref[q], nt_dims, preferred_element_type=F32))
        for q in range(nq):
            _scan_slab(gs_ref, 0, q, lam_ref, pw_ref, car_b, steps, reverse=True,
                       prev_ref=s_ref, prev_row0=halo, glam_ref=glam_ref)
        du_parts = []
        for q in range(nq):
            sl = slice(q * SLAB, (q + 1) * SLAB)
            gsb = _get_slab(gs_ref, everything, q).astype(BF16)
            du_parts.append(lax.dot_general(gsb, bq_ref[q], nt_dims, preferred_element_type=F32))
            gb_ref[q] += lax.dot_general(ub[:, sl], gsb, tn, preferred_element_type=F32)
        du_sm = jnp.concatenate(du_parts, axis=1)
        du_ref[...] = (_permute_rows_f32(pmt_ref[...], du_sm) + dyv * d_ref[...]).astype(BF16)

    c3 = lambda i: (0, 0, 0)
    c2 = lambda i: (0, 0)
    rev = lambda i: (nt - 1 - i, 0)
    dense = jax.ShapeDtypeStruct((nq, SLAB, w2), F32)
    gp = lam.shape[1]
    return pl.pallas_call(
        body, name="ssm_bwd", grid=(nt,),
        in_specs=[pl.BlockSpec((tt, dssm), lambda i: (nt - 1 - i, u_block)), pl.BlockSpec((tt, dssm), rev),
                  pl.BlockSpec((None, ntile, SUBLANES, SLAB), lambda i: (nt - 1 - i, 0, 0, 0)),
                  pl.BlockSpec(memory_space=pl.ANY), pl.BlockSpec((tt, tt), c2), pl.BlockSpec((tt, tt), c2),
                  pl.BlockSpec(bq.shape, c3), pl.BlockSpec(cq.shape, c3),
                  pl.BlockSpec(lam.shape, c2), pl.BlockSpec(pw.shape, c2), pl.BlockSpec((1, dssm), c2)],
        out_specs=[pl.BlockSpec((tt, dssm), lambda i: (nt - 1 - i, u_block)), pl.BlockSpec(dense.shape, c3),
                   pl.BlockSpec(dense.shape, c3), pl.BlockSpec((SUBLANES, gp), c2), pl.BlockSpec((SUBLANES, dssm), c2)],
        out_shape=[jax.ShapeDtypeStruct(dproj.shape, dproj.dtype), dense, dense,
                   jax.ShapeDtypeStruct((SUBLANES, gp), F32), jax.ShapeDtypeStruct((SUBLANES, dssm), F32)],
        scratch_shapes=[pltpu.VMEM((ntile, tt + halo, SLAB), F32), pltpu.VMEM((ntile, tt, SLAB), F32),
                        pltpu.VMEM((ntile, SUBLANES, SLAB), F32), pltpu.VMEM((ntile, SUBLANES, SLAB), F32)],
        input_output_aliases={3: 0},
        compiler_params=_cp(2 * (10 * tt * dssm + 4 * nq * ds * w2 + 8 * nq * SLAB * w2)
                            + 8 * tt * width + 12 * MIB, dimension_semantics=("arbitrary",)),
    )(proj, dy, cin, dproj, pm, pmt, bq, cq, lam, pw, d_skip)


def _gate_fwd(proj, y, conv_w, conv_b, w_glu, b_glu, dc):
    t = proj.shape[0]
    dssm = y.shape[1]
    assert dc == dssm
    tm = min(TM_GATE, t)
    halo = SUBLANES

    def body(b_ref, c_ref, v_ref, zc_ref, zs_ref, y_ref, cw_ref, cb_ref, wg_ref, bg_ref, mix_ref, cv_buf):
        @pl.when(pl.program_id(0) == 0)
        def _():
            cv_buf[0:halo, :] = jnp.zeros((halo, dc), F32)

        cv = c_ref[...] * v_ref[...]
        cv_buf[halo:, :] = cv
        conv = (cb_ref[...] + cw_ref[2:3, :] * cv + cw_ref[1:2, :] * cv_buf[halo - 1:halo - 1 + tm, :]
                + cw_ref[0:1, :] * cv_buf[halo - 2:halo - 2 + tm, :])
        sz, _ = _silu(zc_ref[...])
        mix_ref[:, 0:dc] = (b_ref[...] * conv * sz).astype(BF16)
        cv_buf[0:halo, :] = cv_buf[tm:tm + halo, :]
        ge, _ = _gelu(y_ref[...])
        gl = jnp.dot(ge.astype(BF16), wg_ref[...], preferred_element_type=F32) + bg_ref[...]
        szs, _ = _silu(zs_ref[...])
        mix_ref[:, dc:] = (ge * jax.nn.sigmoid(gl) * szs).astype(BF16)

    col = lambda j: pl.BlockSpec((tm, dc), lambda i, j=j: (i, j))
    fixed = lambda i: (0, 0)
    return pl.pallas_call(
        body, name="gate_fwd", grid=(t // tm,),
        in_specs=[col(0), col(1), col(2), col(3), col(5), pl.BlockSpec((tm, dssm), lambda i: (i, 0)),
                  pl.BlockSpec(conv_w.shape, fixed), pl.BlockSpec((1, dc), fixed),
                  pl.BlockSpec(w_glu.shape, fixed), pl.BlockSpec((1, dssm), fixed)],
        out_specs=pl.BlockSpec((tm, dc + dssm), lambda i: (i, 0)),
        out_shape=jax.ShapeDtypeStruct((t, dc + dssm), BF16),
        scratch_shapes=[pltpu.VMEM((tm + halo, dc), F32)],
        compiler_params=_cp(2 * (6 * 4 * tm * dc + 2 * tm * (dc + dssm) + 2 * dssm * dssm) + 24 * tm * dc + 8 * MIB,
                            dimension_semantics=("arbitrary",)),
    )(proj, proj, proj, proj, proj, y, conv_w, conv_b, w_glu, b_glu)


def _gate_bwd(proj, y, dmix, conv_w, conv_b, w_glu, b_glu, dc):
    t = proj.shape[0]
    dssm = y.shape[1]
    tm = min(TM_GATE, t)
    nt = t // tm
    halo = SUBLANES
    blocks_per_tile = tm // halo

    def body(b_ref, c_ref, v_ref, zc_ref, zs_ref, cp_ref, vp_ref, y_ref, dm_ref, cw_ref, cb_ref, wg_ref, bg_ref,
             dp_ref, dy_ref, gs_ref, gwg_ref, cv_buf, dc_buf):
        i = pl.program_id(0)

        @pl.when(i == 0)
        def _():
            dc_buf[tm:, :] = jnp.zeros((halo, dc), F32)
            gs_ref[...] = jnp.zeros_like(gs_ref)
            gwg_ref[...] = jnp.zeros_like(gwg_ref)

        first_tile = (i == nt - 1)
        bv, cg, vv, zc = b_ref[...], c_ref[...], v_ref[...], zc_ref[...]
        cv = cg * vv
        cv_buf[0:halo, :] = jnp.where(first_tile, 0.0, cp_ref[...] * vp_ref[...])
        cv_buf[halo:, :] = cv
        w0, w1, w2 = cw_ref[0:1, :], cw_ref[1:2, :], cw_ref[2:3, :]
        conv = (cb_ref[...] + w2 * cv + w1 * cv_buf[halo - 1:halo - 1 + tm, :]
                + w0 * cv_buf[halo - 2:halo - 2 + tm, :])
        sz, sgc = _silu(zc)
        dyc = dm_ref[:, 0:dc]
        dp_ref[:, 0:dc] = (dyc * conv * sz).astype(BF16)
        dp_ref[:, 3 * dc:4 * dc] = (dyc * bv * conv * _dsilu(zc, sgc)).astype(BF16)
        dconv = dyc * bv * sz
        dc_buf[0:tm, :] = dconv
        d1 = dc_buf[1:1 + tm, :]
        d2 = dc_buf[2:2 + tm, :]
        dcv = w2 * dconv + w1 * d1 + w0 * d2
        dp_ref[:, dc:2 * dc] = (dcv * vv).astype(BF16)
        dp_ref[:, 2 * dc:3 * dc] = (dcv * cg).astype(BF16)
        gs_ref[0:1, :] += jnp.sum(cv * d2, axis=0, keepdims=True)
        gs_ref[1:2, :] += jnp.sum(cv * d1, axis=0, keepdims=True)
        gs_ref[2:3, :] += jnp.sum(cv * dconv, axis=0, keepdims=True)
        gs_ref[3:4, :] += jnp.sum(dconv, axis=0, keepdims=True)
        dc_buf[tm:, :] = dc_buf[0:halo, :]

        yv, zs = y_ref[...], zs_ref[...]
        ge, th = _gelu(yv)
        geb = ge.astype(BF16)
        wg = wg_ref[...]
        gl = jnp.dot(geb, wg, preferred_element_type=F32) + bg_ref[...]
        sg = jax.nn.sigmoid(gl)
        szs, sgs = _silu(zs)
        dys = dm_ref[:, dc:]
        ys = ge * sg
        dp_ref[:, 4 * dc:5 * dc] = jnp.zeros((tm, dc), BF16)
        dp_ref[:, 5 * dc:] = (dys * ys * _dsilu(zs, sgs)).astype(BF16)
        d_ys = dys * szs
        dgl = d_ys * ge * sg * (1.0 - sg)
        dglb = dgl.astype(BF16)
        gs_ref[4:5, :] += jnp.sum(dgl, axis=0, keepdims=True)
        gwg_ref[...] += lax.dot_general(geb, dglb, (((0,), (0,)), ((), ())), preferred_element_type=F32)
        dge = d_ys * sg + lax.dot_general(dglb, wg, (((1,), (1,)), ((), ())), preferred_element_type=F32)
        dy_ref[...] = dge * _dgelu(yv, th)

    col = lambda j: pl.BlockSpec((tm, dc), lambda i, j=j: (nt - 1 - i, j))
    prev = lambda j: pl.BlockSpec((halo, dc), lambda i, j=j: (jnp.maximum((nt - 1 - i) * blocks_per_tile - 1, 0), j))
    rev = lambda i: (nt - 1 - i, 0)
    fixed = lambda i: (0, 0)
    return pl.pallas_call(
        body, name="gate_bwd", grid=(nt,),
        in_specs=[col(0), col(1), col(2), col(3), col(5), prev(1), prev(2), pl.BlockSpec((tm, dssm), rev),
                  pl.BlockSpec((tm, dc + dssm), rev), pl.BlockSpec(conv_w.shape, fixed), pl.BlockSpec((1, dc), fixed),
                  pl.BlockSpec(w_glu.shape, fixed), pl.BlockSpec((1, dssm), fixed)],
        out_specs=[pl.BlockSpec((tm, 6 * dc), rev), pl.BlockSpec((tm, dssm), rev),
                   pl.BlockSpec((2 * SUBLANES, dc), fixed), pl.BlockSpec((dssm, dssm), fixed)],
        out_shape=[jax.ShapeDtypeStruct((t, 6 * dc), BF16), jax.ShapeDtypeStruct((t, dssm), F32),
                   jax.ShapeDtypeStruct((2 * SUBLANES, dc), F32), jax.ShapeDtypeStruct((dssm, dssm), F32)],
        scratch_shapes=[pltpu.VMEM((tm + halo, dc), F32), pltpu.VMEM((tm + halo, dc), F32)],
        compiler_params=_cp(2 * (6 * 4 * tm * dc + 8 * tm * dc + 12 * tm * dc + 4 * tm * dc + 6 * dssm * dssm)
                            + 40 * tm * dc + 8 * MIB, dimension_semantics=("arbitrary",)),
    )(proj, proj, proj, proj, proj, proj, proj, y, dmix, conv_w, conv_b, w_glu, b_glu)


def _allgather_halves(parts, name):
    n = len(parts)
    per = 2 * (N_CHIPS - 1)

    def body(*refs):
        ins, outs = refs[:n], refs[n:2 * n]
        send, recv, local = refs[2 * n:]
        x, y, c = _my_place()
        k = 2 * x + y
        sib = (x, y, 1 - c)
        chips = _other_chips(x, y)

        def rc(t, s, src, dst, to):
            return pltpu.make_async_remote_copy(src_ref=src, dst_ref=dst, send_sem=send.at[t * per + s],
                                                recv_sem=recv.at[t * per + s], device_id=to, device_id_type=MESH)

        mine = [pltpu.make_async_copy(ins[t], outs[t].at[k], local.at[t]) for t in range(n)]
        for cp in mine:
            cp.start()
        sent = []
        for t in range(n):
            for j, (cx, cy) in enumerate(chips):
                cp = rc(t, j, ins[t].at[c], outs[t].at[k, c], (cx, cy, c))
                cp.start()
                sent.append(cp)
        for t in range(n):
            for j, (cx, cy) in enumerate(chips):
                landed = outs[t].at[2 * cx + cy, c]
                rc(t, j, landed, landed, (cx, cy, c)).wait_recv()
                cp = rc(t, N_CHIPS - 1 + j, landed, landed, sib)
                cp.start()
                sent.append(cp)
        for t in range(n):
            for j, (cx, cy) in enumerate(chips):
                other = outs[t].at[2 * cx + cy, 1 - c]
                rc(t, N_CHIPS - 1 + j, other, other, sib).wait_recv()
        for cp in sent:
            cp.wait_send()
        for cp in mine:
            cp.wait()

    any_spec = pl.BlockSpec(memory_space=pl.ANY)
    return pl.pallas_call(
        body, name=name, in_specs=[any_spec] * n, out_specs=[any_spec] * n,
        out_shape=[jax.ShapeDtypeStruct((N_CHIPS,) + p.shape, p.dtype) for p in parts],
        scratch_shapes=[pltpu.SemaphoreType.DMA((n * per,)), pltpu.SemaphoreType.DMA((n * per,)),
                        pltpu.SemaphoreType.DMA((n,))],
        compiler_params=_cp(16 * MIB),
    )(*parts)


def _allgather_flat(v, name):
    r, c = v.shape
    rels = [(dx, dy, dc) for dx in (0, 1) for dy in (0, 1) for dc in (0, 1)][1:]

    def body(v_ref, out_ref, send, recv):
        x, y, cc = _my_place()
        me = 4 * x + 2 * y + cc

        def peer(rel):
            dx, dy, dc = rel
            return (1 - x if dx else x, 1 - y if dy else y, 1 - cc if dc else cc)

        def rc(s, slot, to):
            return pltpu.make_async_remote_copy(src_ref=v_ref, dst_ref=out_ref.at[slot], send_sem=send.at[s],
                                                recv_sem=recv.at[s], device_id=to, device_id_type=MESH)

        sent = []
        for s, rel in enumerate(rels):
            cp = rc(s, me, peer(rel))
            cp.start()
            sent.append(cp)
        out_ref[me] = v_ref[...]
        for s, rel in enumerate(rels):
            px, py, pc = peer(rel)
            rc(s, 4 * px + 2 * py + pc, (px, py, pc)).wait_recv()
        for cp in sent:
            cp.wait_send()

    return pl.pallas_call(
        body, name=name, out_shape=jax.ShapeDtypeStruct((N_DEV, r, c), F32),
        in_specs=[pl.BlockSpec(memory_space=pltpu.VMEM)], out_specs=pl.BlockSpec(memory_space=pltpu.VMEM),
        scratch_shapes=[pltpu.SemaphoreType.DMA((N_DEV - 1,)), pltpu.SemaphoreType.DMA((N_DEV - 1,))],
        compiler_params=_cp((N_DEV + 2) * 4 * r * c + 8 * MIB),
    )(v)


def _rs_pair_exchange(grads):
    n = len(grads)

    def body(*refs):
        ins, outs = refs[:n], refs[n:2 * n]
        send, recv = refs[2 * n:]
        x, y, c = _my_place()
        cps = []
        for t in range(n):
            cp = pltpu.make_async_remote_copy(src_ref=ins[t].at[:, 1 - c], dst_ref=outs[t], send_sem=send.at[t],
                                              recv_sem=recv.at[t], device_id=(x, y, 1 - c), device_id_type=MESH)
            cp.start()
            cps.append(cp)
        for cp in cps:
            cp.wait()

    any_spec = pl.BlockSpec(memory_space=pl.ANY)
    return pl.pallas_call(
        body, name="rs_pair_exchange", in_specs=[any_spec] * n, out_specs=[any_spec] * n,
        out_shape=[jax.ShapeDtypeStruct((g.shape[0],) + g.shape[2:], F32) for g in grads],
        scratch_shapes=[pltpu.SemaphoreType.DMA((n,)), pltpu.SemaphoreType.DMA((n,))],
        compiler_params=_cp(16 * MIB),
    )(*grads)


def _rs_pair_add(place, grad, got, name):
    nk, _, r2, c = grad.shape
    tr = min(TR_ELT, r2)

    def body(place_ref, g_ref, r_ref, o16_ref, o32_ref):
        del place_ref
        s = g_ref[...] + r_ref[...]
        o32_ref[...] = s
        o16_ref[...] = s.astype(BF16)

    blk = pl.BlockSpec((None, tr, c), lambda k, i, p: (k, i, 0))
    return pl.pallas_call(
        body, name=name,
        grid_spec=pltpu.PrefetchScalarGridSpec(
            num_scalar_prefetch=1, grid=(nk, r2 // tr),
            in_specs=[pl.BlockSpec((None, None, tr, c), lambda k, i, p: (k, p[0], i, 0)), blk],
            out_specs=[blk, blk]),
        out_shape=[jax.ShapeDtypeStruct((nk, r2, c), BF16), jax.ShapeDtypeStruct((nk, r2, c), F32)],
        compiler_params=_cp(2 * 14 * tr * c + 8 * MIB),
    )(place, grad, got)


def _rs_chip_exchange(sums16):
    n = len(sums16)
    per = N_CHIPS - 1

    def body(*refs):
        ins, outs = refs[:n], refs[n:2 * n]
        send, recv = refs[2 * n:]
        x, y, c = _my_place()
        chips = _other_chips(x, y)
        cps = []
        for t in range(n):
            for j, (cx, cy) in enumerate(chips):
                cp = pltpu.make_async_remote_copy(src_ref=ins[t].at[2 * cx + cy], dst_ref=outs[t].at[j],
                                                  send_sem=send.at[t * per + j], recv_sem=recv.at[t * per + j],
                                                  device_id=(cx, cy, c), device_id_type=MESH)
                cp.start()
                cps.append(cp)
        for cp in cps:
            cp.wait()

    any_spec = pl.BlockSpec(memory_space=pl.ANY)
    return pl.pallas_call(
        body, name="rs_chip_exchange", in_specs=[any_spec] * n, out_specs=[any_spec] * n,
        out_shape=[jax.ShapeDtypeStruct((per,) + s.shape[1:], BF16) for s in sums16],
        scratch_shapes=[pltpu.SemaphoreType.DMA((n * per,)), pltpu.SemaphoreType.DMA((n * per,))],
        compiler_params=_cp(16 * MIB),
    )(*sums16)


def _rs_chip_add(place, sums32, got, name):
    _, r2, c = sums32.shape
    tr = min(TR_ELT, r2)

    def body(place_ref, s_ref, q_ref, o_ref):
        del place_ref
        o_ref[...] = ((s_ref[...] + q_ref[0].astype(F32)) + q_ref[1].astype(F32)) + q_ref[2].astype(F32)

    return pl.pallas_call(
        body, name=name,
        grid_spec=pltpu.PrefetchScalarGridSpec(
            num_scalar_prefetch=1, grid=(r2 // tr,),
            in_specs=[pl.BlockSpec((None, tr, c), lambda i, p: (p[1], i, 0)),
                      pl.BlockSpec((N_CHIPS - 1, tr, c), lambda i, p: (0, i, 0))],
            out_specs=pl.BlockSpec((tr, c), lambda i, p: (i, 0))),
        out_shape=jax.ShapeDtypeStruct((r2, c), F32),
        compiler_params=_cp(2 * 14 * tr * c + 8 * MIB),
    )(place, sums32, got)


def _rs_pair_share(halves):
    n = len(halves)

    def body(*refs):
        ins, outs = refs[:n], refs[n:2 * n]
        send, recv, local = refs[2 * n:]
        x, y, c = _my_place()
        cps, mine = [], []
        for t in range(n):
            lc = pltpu.make_async_copy(ins[t], outs[t].at[c], local.at[t])
            lc.start()
            mine.append(lc)
            cp = pltpu.make_async_remote_copy(src_ref=ins[t], dst_ref=outs[t].at[c], send_sem=send.at[t],
                                              recv_sem=recv.at[t], device_id=(x, y, 1 - c), device_id_type=MESH)
            cp.start()
            cps.append(cp)
        for t in range(n):
            other = outs[t].at[1 - c]
            pltpu.make_async_remote_copy(src_ref=other, dst_ref=other, send_sem=send.at[t], recv_sem=recv.at[t],
                                         device_id=(x, y, 1 - c), device_id_type=MESH).wait_recv()
        for cp in cps:
            cp.wait_send()
        for lc in mine:
            lc.wait()

    any_spec = pl.BlockSpec(memory_space=pl.ANY)
    return pl.pallas_call(
        body, name="rs_pair_share", in_specs=[any_spec] * n, out_specs=[any_spec] * n,
        out_shape=[jax.ShapeDtypeStruct((2,) + h.shape, F32) for h in halves],
        scratch_shapes=[pltpu.SemaphoreType.DMA((n,)), pltpu.SemaphoreType.DMA((n,)), pltpu.SemaphoreType.DMA((n,))],
        compiler_params=_cp(16 * MIB),
    )(*halves)


_PACK_TILE = SUBLANES * SLAB


def _pack(arrays):
    rows = []
    for a in arrays:
        flat = a.reshape(-1).astype(F32)
        padded = -(-flat.shape[0] // _PACK_TILE) * _PACK_TILE
        rows.append(jnp.pad(flat, (0, padded - flat.shape[0])).reshape(-1, SLAB))
    return jnp.concatenate(rows, axis=0)


def _unpack(packed, shapes):
    out, row = [], 0
    for shp in shapes:
        size = math.prod(shp)
        nrow = -(-size // _PACK_TILE) * SUBLANES
        out.append(packed[row:row + nrow].reshape(-1)[:size].reshape(shp))
        row += nrow
    return out


def kernel(x, norm_pre_g, w_in, conv_w, conv_b, ssm_a_re, ssm_a_im, ssm_log_dt, ssm_b_re, ssm_b_im, ssm_c_re, ssm_c_im, ssm_d, w_glu, b_glu, w_out, norm_post_g, loss_target, m_norm_pre_g, m_w_in, m_conv_w, m_conv_b, m_ssm_a_re, m_ssm_a_im, m_ssm_log_dt, m_ssm_b_re, m_ssm_b_im, m_ssm_c_re, m_ssm_c_im, m_ssm_d, m_w_glu, m_b_glu, m_w_out, m_norm_post_g, v_norm_pre_g, v_w_in, v_conv_w, v_conv_b, v_ssm_a_re, v_ssm_a_im, v_ssm_log_dt, v_ssm_b_re, v_ssm_b_im, v_ssm_c_re, v_ssm_c_im, v_ssm_d, v_w_glu, v_b_glu, v_w_out, v_norm_post_g):
    xs, tgt = x[0], loss_target[0]
    t, d = xs.shape
    dc = conv_b.shape[0]
    dssm = ssm_d.shape[0]
    g, p = ssm_a_re.shape
    h = SSM_H
    nq = dssm // SLAB
    n_shard = w_in.shape[1]
    steps = min(T_SCAN, t) // SUBLANES
    mx, my, mc = _my_place()
    chip = 2 * mx + my
    place = jnp.stack([mc, chip]).astype(jnp.int32)

    halves = lambda a: a.reshape(2, a.shape[0] // 2, a.shape[1])
    win_g, wout_g, wglu_g = _allgather_halves(
        [halves(_cast_bf16(w_in, "cast_w_in")), halves(_cast_bf16(w_out, "cast_w_out")),
         halves(_cast_bf16(w_glu, "cast_w_glu"))], "allgather_weights")
    win_b = win_g.reshape(N_CHIPS, d, n_shard)
    wout_b = wout_g.reshape(dc + dssm, d)
    wglu_b = wglu_g.reshape(dssm, dssm)

    cw_cols = conv_w.shape[1]
    cw_pad = -(-cw_cols // SLAB) * SLAB
    cw_blk = jnp.zeros((SUBLANES, cw_pad), F32).at[:conv_w.shape[0], :cw_cols].set(conv_w)
    cw_all = _allgather_flat(cw_blk, "allgather_conv_w")
    conv_w_full = jnp.concatenate([cw_all[2 * k, :, :cw_cols] for k in range(N_CHIPS)], axis=1)

    expand = jnp.repeat(jnp.eye(p, dtype=F32), h, axis=1)
    b_re2, b_im2 = ssm_b_re.reshape(g, p * h), ssm_b_im.reshape(g, p * h)
    log_dt2 = ssm_log_dt.reshape(g, 1)
    lbr, lbi, bbr2, bbi2, pw3 = _zoh_fwd(ssm_a_re, ssm_a_im, log_dt2, b_re2, b_im2, expand, steps)
    lam = jnp.stack([lbr.reshape(g * p), lbi.reshape(g * p)])
    pw = pw3.reshape(2, g * p)
    to_slab_b = lambda b2: _blockdiag(b2.reshape(nq, GROUPS_PER_SLAB, p, h).transpose(0, 1, 3, 2))
    bq = jnp.concatenate([to_slab_b(bbr2), to_slab_b(bbi2)], axis=2).astype(BF16)
    to_slab_c = lambda c3: _blockdiag(c3.reshape(nq, GROUPS_PER_SLAB, h, p).transpose(0, 1, 3, 2))
    cq = jnp.concatenate([to_slab_c(ssm_c_re), to_slab_c(-ssm_c_im)], axis=1).astype(BF16)

    g_pre2, g_post2 = norm_pre_g.reshape(1, d), norm_post_g.reshape(1, d)
    conv_b2, b_glu2, d_skip2 = conv_b.reshape(1, dc), b_glu.reshape(1, dssm), ssm_d.reshape(1, dssm)
    u_block = 4 * dc // dssm

    hb = _prenorm(xs, g_pre2)
    proj = _matmul_nn(hb, win_b, "inproj")
    y, cin = _ssm_fwd(proj, u_block, bq, cq, lam, pw, d_skip2)
    mix = _gate_fwd(proj, y, conv_w_full, conv_b2, wglu_b, b_glu2, dc)
    loss_blk, dout, dob, dmix, gg_post = _outproj(mix, wout_b, xs, tgt, g_post2)

    gw_out = _matmul_tn(mix, dob, 1, "grad_w_out")
    dproj, dy, gsmall, gw_glu = _gate_bwd(proj, y, dmix, conv_w_full, conv_b2, wglu_b, b_glu2, dc)
    dproj, gb_dense, gc_dense, glam, gd = _ssm_bwd(proj, u_block, dy, cin, dproj, bq, cq, lam, pw, d_skip2)
    gx, gg_pre = _dh_prenorm_bwd(dproj, win_b, xs, dout, g_pre2)
    gw_in = _matmul_tn(hb, dproj, N_CHIPS, "grad_w_in")

    gb4 = gb_dense.reshape(nq, SLAB, 2, SLAB_STATES)
    g_bbr2 = _blockdiag_take(gb4[:, :, 0, :], h, p).transpose(0, 1, 3, 2).reshape(g, p * h)
    g_bbi2 = _blockdiag_take(gb4[:, :, 1, :], h, p).transpose(0, 1, 3, 2).reshape(g, p * h)
    gc4 = gc_dense.reshape(nq, SLAB, 2, SLAB_STATES)
    g_c_re = _blockdiag_take(gc4[:, :, 0, :], h, p).reshape(g, h, p)
    g_c_im = -_blockdiag_take(gc4[:, :, 1, :], h, p).reshape(g, h, p)
    g_a_re, g_a_im, g_ld, g_b_re2, g_b_im2 = _zoh_bwd(
        ssm_a_re, ssm_a_im, log_dt2, b_re2, b_im2, expand,
        glam[0].reshape(g, p), glam[1].reshape(g, p), g_bbr2, g_bbi2)

    small_names = ["norm_pre_g", "conv_w", "conv_b", "ssm_a_re", "ssm_a_im", "ssm_log_dt", "ssm_b_re", "ssm_b_im",
                   "ssm_c_re", "ssm_c_im", "ssm_d", "b_glu", "norm_post_g", "loss"]
    small_g = {
        "norm_pre_g": gg_pre[0], "conv_w": gsmall[0:3], "conv_b": gsmall[3], "ssm_a_re": g_a_re, "ssm_a_im": g_a_im,
        "ssm_log_dt": g_ld.reshape(g), "ssm_b_re": g_b_re2.reshape(g, p, h), "ssm_b_im": g_b_im2.reshape(g, p, h),
        "ssm_c_re": g_c_re, "ssm_c_im": g_c_im, "ssm_d": gd[0], "b_glu": gsmall[4], "norm_post_g": gg_post[0],
        "loss": loss_blk[0, 0:1],
    }
    zeros_cw = jnp.zeros((conv_w.shape[0], dc), F32)
    one0 = jnp.zeros((1,), F32)
    small_w = dict(norm_pre_g=norm_pre_g, conv_w=zeros_cw, conv_b=conv_b, ssm_a_re=ssm_a_re, ssm_a_im=ssm_a_im,
                   ssm_log_dt=ssm_log_dt, ssm_b_re=ssm_b_re, ssm_b_im=ssm_b_im, ssm_c_re=ssm_c_re, ssm_c_im=ssm_c_im,
                   ssm_d=ssm_d, b_glu=b_glu, norm_post_g=norm_post_g, loss=one0)
    small_m = dict(norm_pre_g=m_norm_pre_g, conv_w=zeros_cw, conv_b=m_conv_b, ssm_a_re=m_ssm_a_re, ssm_a_im=m_ssm_a_im,
                   ssm_log_dt=m_ssm_log_dt, ssm_b_re=m_ssm_b_re, ssm_b_im=m_ssm_b_im, ssm_c_re=m_ssm_c_re,
                   ssm_c_im=m_ssm_c_im, ssm_d=m_ssm_d, b_glu=m_b_glu, norm_post_g=m_norm_post_g, loss=one0)
    small_v = dict(norm_pre_g=v_norm_pre_g, conv_w=zeros_cw, conv_b=v_conv_b, ssm_a_re=v_ssm_a_re, ssm_a_im=v_ssm_a_im,
                   ssm_log_dt=v_ssm_log_dt, ssm_b_re=v_ssm_b_re, ssm_b_im=v_ssm_b_im, ssm_c_re=v_ssm_c_re,
                   ssm_c_im=v_ssm_c_im, ssm_d=v_ssm_d, b_glu=v_b_glu, norm_post_g=v_norm_post_g, loss=one0)
    shapes = [small_w[nm].shape for nm in small_names]
    gathered = _allgather_flat(_pack([small_g[nm] for nm in small_names]), "allgather_small_grads")
    packs = _adamw_small(gathered, _pack([small_w[nm] for nm in small_names]),
                         _pack([small_m[nm] for nm in small_names]), _pack([small_v[nm] for nm in small_names]))
    sg, sd, sm, sv = [dict(zip(small_names, _unpack(pk, shapes))) for pk in packs]
    loss = sg["loss"][0]

    g_cw = lax.dynamic_slice_in_dim(sg["conv_w"], chip * cw_cols, cw_cols, axis=1)
    pad_cw = lambda a: jnp.zeros((SUBLANES, cw_pad), F32).at[:a.shape[0], :cw_cols].set(a)
    cut_cw = lambda a: a[:conv_w.shape[0], :cw_cols]
    d_cw, m_cw, v_cw = [cut_cw(a) for a in _adamw(pad_cw(conv_w), pad_cw(g_cw), pad_cw(m_conv_w), pad_cw(v_conv_w),
                                                  "adamw_conv_w")]

    big =[gw_in.reshape(N_CHIPS, 2, d // 2, n_shard),
           gw_out.reshape(N_CHIPS, 2, (dc + dssm) // (2 * N_CHIPS), d),
           gw_glu.reshape(N_CHIPS, 2, dssm // (2 * N_CHIPS), dssm)]
    tags =["w_in", "w_out", "w_glu"]
    got = _rs_pair_exchange(big)
    sums = [_rs_pair_add(place, gt, rt, "rs_pair_add_" + tg) for gt, rt, tg in zip(big, got, tags)]
    recvd = _rs_chip_exchange([s16 for s16, _ in sums])
    mine = [_rs_chip_add(place, s32, qt, "rs_chip_add_" + tg) for (_, s32), qt, tg in zip(sums, recvd, tags)]
    full = _rs_pair_share(mine)
    g_win, g_wout, g_wglu = [f.reshape(2 * f.shape[1], f.shape[2]) for f in full]
    d_win, m_win, v_win = _adamw(w_in, g_win, m_w_in, v_w_in, "adamw_w_in")
    d_wout, m_wout, v_wout = _adamw(w_out, g_wout, m_w_out, v_w_out, "adamw_w_out")
    d_wglu, m_wglu, v_wglu = _adamw(w_glu, g_wglu, m_w_glu, v_w_glu, "adamw_w_glu")

    order = ["norm_pre_g", "w_in", "conv_w", "conv_b", "ssm_a_re", "ssm_a_im", "ssm_log_dt", "ssm_b_re", "ssm_b_im",
             "ssm_c_re", "ssm_c_im", "ssm_d", "w_glu", "b_glu", "w_out", "norm_post_g"]
    grads, deltas, new_m, new_v = dict(sg), dict(sd), dict(sm), dict(sv)
    grads.update(w_in=g_win, w_out=g_wout, w_glu=g_wglu, conv_w=g_cw)
    deltas.update(w_in=d_win, w_out=d_wout, w_glu=d_wglu, conv_w=d_cw)
    new_m.update(w_in=m_win, w_out=m_wout, w_glu=m_wglu, conv_w=m_cw)
    new_v.update(w_in=v_win, w_out=v_wout, w_glu=v_wglu, conv_w=v_cw)
    return (loss, gx[None], *[grads[nm] for nm in order], *[deltas[nm] for nm in order],
            *[new_m[nm] for nm in order], *[new_v[nm] for nm in order])
```

```python
import functools
import math

import jax
import jax.numpy as jnp
from jax import lax
from jax.experimental import pallas as pl
from jax.experimental.pallas import tpu as pltpu

F32 = jnp.float32
BF16 = jnp.bfloat16
MESH = pl.DeviceIdType.MESH

EPS = 1e-6
SSM_H = 16
SSM_P = 64
GROUPS_PER_SLAB = 8
SLAB = 128
SLAB_STATES = GROUPS_PER_SLAB * SSM_P
N_CHIPS = 4
N_DEV = 8

ADAM_LR = 0.001
ADAM_B1 = 0.9
ADAM_B2 = 0.999
ADAM_EPS = 1e-08
ADAM_WD = 0.01
ADAM_STEP = 10

MIB = 1024 * 1024
VMEM_CAP = 48 * MIB
SUBLANES = 8

TM_NORM = 512
TM_PROJ = 512
TM_GATE = 256
TM_OUT = 256
TM_DH = 256
T_SCAN = 256
TK_TN = 512
SCAN_UNROLL = 4
TM_TN = 1024
TR_ELT = 256


def _cp(vmem_bytes, **kw):
    return pltpu.CompilerParams(vmem_limit_bytes=int(min(VMEM_CAP, max(16 * MIB, vmem_bytes))), **kw)


def _my_place():
    return lax.axis_index("x"), lax.axis_index("y"), lax.axis_index("c")


def _other_chips(x, y):
    return [(1 - x, y), (x, 1 - y), (1 - x, 1 - y)]


def _silu(z):
    s = jax.nn.sigmoid(z)
    return z * s, s


def _dsilu(z, s):
    return s * (1.0 + z * (1.0 - s))


_GELU_K = math.sqrt(2.0 / math.pi)
_GELU_C = 0.044715


def _gelu(y):
    th = jnp.tanh(_GELU_K * (y + _GELU_C * y * y * y))
    return 0.5 * y * (1.0 + th), th


def _dgelu(y, th):
    return 0.5 * (1.0 + th) + 0.5 * y * (1.0 - th * th) * _GELU_K * (1.0 + 3.0 * _GELU_C * y * y)


def _cast_bf16(w, name):
    r, c = w.shape
    tr = min(TR_ELT, r)

    def body(w_ref, o_ref):
        o_ref[...] = w_ref[...].astype(BF16)

    return pl.pallas_call(
        body, name=name, grid=(r // tr,),
        in_specs=[pl.BlockSpec((tr, c), lambda i: (i, 0))],
        out_specs=pl.BlockSpec((tr, c), lambda i: (i, 0)),
        out_shape=jax.ShapeDtypeStruct((r, c), BF16),
        compiler_params=_cp(12 * tr * c),
    )(w)


def _prenorm(x, g):
    t, d = x.shape
    tm = min(TM_NORM, t)

    def body(x_ref, g_ref, h_ref):
        xv = x_ref[...]
        r = lax.rsqrt(jnp.mean(xv * xv, axis=-1, keepdims=True) + EPS)
        h_ref[...] = (xv * r * g_ref[...]).astype(BF16)

    return pl.pallas_call(
        body, name="prenorm", grid=(t // tm,),
        in_specs=[pl.BlockSpec((tm, d), lambda i: (i, 0)), pl.BlockSpec((1, d), lambda i: (0, 0))],
        out_specs=pl.BlockSpec((tm, d), lambda i: (i, 0)),
        out_shape=jax.ShapeDtypeStruct((t, d), BF16),
        compiler_params=_cp(20 * tm * d),
    )(x, g)


def _matmul_nn(a, b, name):
    t, k = a.shape
    nb, _, n = b.shape
    tm = min(TM_PROJ, t)

    def body(a_ref, b_ref, o_ref):
        o_ref[...] = jnp.dot(a_ref[...], b_ref[...], preferred_element_type=F32)

    return pl.pallas_call(
        body, name=name, grid=(nb, t // tm),
        in_specs=[pl.BlockSpec((tm, k), lambda j, i: (i, 0)), pl.BlockSpec((None, k, n), lambda j, i: (j, 0, 0))],
        out_specs=pl.BlockSpec((tm, n), lambda j, i: (i, j)),
        out_shape=jax.ShapeDtypeStruct((t, nb * n), F32),
        compiler_params=_cp(2 * (2 * tm * k + 2 * k * n + 4 * tm * n) + 4 * MIB),
    )(a, b)


def _matmul_tn(a, b, nb, name):
    t, m = a.shape
    n = b.shape[1] // nb
    tk = min(TK_TN, t)
    tma = min(TM_TN, m)

    def body(a_ref, b_ref, o_ref):
        @pl.when(pl.program_id(2) == 0)
        def _():
            o_ref[...] = jnp.zeros_like(o_ref)

        o_ref[...] += lax.dot_general(a_ref[...], b_ref[...], (((0,), (0,)), ((), ())), preferred_element_type=F32)

    return pl.pallas_call(
        body, name=name, grid=(nb, m // tma, t // tk),
        in_specs=[pl.BlockSpec((tk, tma), lambda j, i, k: (k, i)), pl.BlockSpec((tk, n), lambda j, i, k: (k, j))],
        out_specs=pl.BlockSpec((None, tma, n), lambda j, i, k: (j, i, 0)),
        out_shape=jax.ShapeDtypeStruct((nb, m, n), F32),
        compiler_params=_cp(2 * (2 * tk * tma + 2 * tk * n + 4 * tma * n) + 8 * MIB),
    )(a, b)


def _outproj(mix, w_out, x, tgt, g_post):
    t, dm = mix.shape
    d = w_out.shape[1]
    tm = min(TM_OUT, t)

    def body(mix_ref, w_ref, x_ref, t_ref, g_ref, loss_ref, dout_ref, do_ref, dmix_ref, gg_ref):
        @pl.when(pl.program_id(0) == 0)
        def _():
            loss_ref[...] = jnp.zeros_like(loss_ref)
            gg_ref[...] = jnp.zeros_like(gg_ref)

        w = w_ref[...]
        o = jnp.dot(mix_ref[...], w, preferred_element_type=F32)
        r = lax.rsqrt(jnp.mean(o * o, axis=-1, keepdims=True) + EPS)
        nh = o * r
        g = g_ref[...]
        e = x_ref[...] + nh * g - t_ref[...]
        loss_ref[...] += jnp.sum(e * e) * (0.5 / d)
        dout = e * (1.0 / d)
        dout_ref[...] = dout
        gg_ref[0:1, :] += jnp.sum(dout * nh, axis=0, keepdims=True)
        dn = dout * g
        do = r * (dn - nh * jnp.mean(dn * nh, axis=-1, keepdims=True))
        dob = do.astype(BF16)
        do_ref[...] = dob
        dmix_ref[...] = lax.dot_general(dob, w, (((1,), (1,)), ((), ())), preferred_element_type=F32)

    row = lambda i: (i, 0)
    fixed = lambda i: (0, 0)
    return pl.pallas_call(
        body, name="outproj", grid=(t // tm,),
        in_specs=[pl.BlockSpec((tm, dm), row), pl.BlockSpec((dm, d), fixed), pl.BlockSpec((tm, d), row),
                  pl.BlockSpec((tm, d), row), pl.BlockSpec((1, d), fixed)],
        out_specs=[pl.BlockSpec((SUBLANES, SLAB), fixed), pl.BlockSpec((tm, d), row), pl.BlockSpec((tm, d), row),
                   pl.BlockSpec((tm, dm), row), pl.BlockSpec((SUBLANES, d), fixed)],
        out_shape=[jax.ShapeDtypeStruct((SUBLANES, SLAB), F32), jax.ShapeDtypeStruct((t, d), F32),
                   jax.ShapeDtypeStruct((t, d), BF16), jax.ShapeDtypeStruct((t, dm), F32),
                   jax.ShapeDtypeStruct((SUBLANES, d), F32)],
        compiler_params=_cp(2 * (2 * dm * d + tm * (2 * dm + 4 * d * 3 + 2 * d + 4 * dm)) + 16 * MIB),
    )(mix, w_out, x, tgt, g_post)


def _dh_prenorm_bwd(dproj, w_in, x, dout, g_pre):
    t, d = x.shape
    nb, _, n = w_in.shape
    tm = min(TM_DH, t)

    def body(dp_ref, w_ref, x_ref, dout_ref, g_ref, dx_ref, gg_ref):
        @pl.when(pl.program_id(0) == 0)
        def _():
            gg_ref[...] = jnp.zeros_like(gg_ref)

        dh = None
        for k in range(nb):
            part = lax.dot_general(dp_ref[:, k * n:(k + 1) * n], w_ref[k], (((1,), (1,)), ((), ())),
                                   preferred_element_type=F32)
            dh = part if dh is None else dh + part
        xv = x_ref[...]
        r = lax.rsqrt(jnp.mean(xv * xv, axis=-1, keepdims=True) + EPS)
        xh = xv * r
        gg_ref[0:1, :] += jnp.sum(dh * xh, axis=0, keepdims=True)
        dg = dh * g_ref[...]
        dx_ref[...] = dout_ref[...] + r * (dg - xh * jnp.mean(dg * xh, axis=-1, keepdims=True))

    row = lambda i: (i, 0)
    fixed = lambda i: (0, 0)
    w_spec = pl.BlockSpec(w_in.shape, lambda i: (0, 0, 0), pipeline_mode=pl.Buffered(1))
    return pl.pallas_call(
        body, name="dh_prenorm_bwd", grid=(t // tm,),
        in_specs=[pl.BlockSpec((tm, nb * n), row), w_spec,
                  pl.BlockSpec((tm, d), row), pl.BlockSpec((tm, d), row), pl.BlockSpec((1, d), fixed)],
        out_specs=[pl.BlockSpec((tm, d), row), pl.BlockSpec((SUBLANES, d), fixed)],
        out_shape=[jax.ShapeDtypeStruct((t, d), F32), jax.ShapeDtypeStruct((SUBLANES, d), F32)],
        compiler_params=_cp(2 * nb * d * n + 2 * (2 * tm * nb * n + 12 * tm * d) + 16 * tm * d + 4 * MIB),
    )(dproj, w_in, x, dout, g_pre)


def _adamw(w, g, m, v, name):
    r, c = w.shape
    tr = min(TR_ELT, r)
    c1 = 1.0 - ADAM_B1 ** ADAM_STEP
    c2 = 1.0 - ADAM_B2 ** ADAM_STEP

    def body(w_ref, g_ref, m_ref, v_ref, d_ref, mo_ref, vo_ref):
        gv = g_ref[...]
        mn = ADAM_B1 * m_ref[...] + (1.0 - ADAM_B1) * gv
        vn = ADAM_B2 * v_ref[...] + (1.0 - ADAM_B2) * (gv * gv)
        d_ref[...] = -ADAM_LR * ((mn / c1) / (jnp.sqrt(vn / c2) + ADAM_EPS) + ADAM_WD * w_ref[...])
        mo_ref[...] = mn
        vo_ref[...] = vn

    spec = pl.BlockSpec((tr, c), lambda i: (i, 0))
    sds = jax.ShapeDtypeStruct((r, c), F32)
    return pl.pallas_call(
        body, name=name, grid=(r // tr,), in_specs=[spec] * 4, out_specs=[spec] * 3, out_shape=[sds] * 3,
        compiler_params=_cp(2 * 7 * 4 * tr * c + 8 * MIB),
    )(w, g, m, v)


def _adamw_small(g, w, m, v):
    r, c = w.shape
    c1 = 1.0 - ADAM_B1 ** ADAM_STEP
    c2 = 1.0 - ADAM_B2 ** ADAM_STEP

    def body(g_ref, w_ref, m_ref, v_ref, d_ref, mo_ref, vo_ref):
        gv = g_ref[...]
        mn = ADAM_B1 * m_ref[...] + (1.0 - ADAM_B1) * gv
        vn = ADAM_B2 * v_ref[...] + (1.0 - ADAM_B2) * (gv * gv)
        d_ref[...] = -ADAM_LR * ((mn / c1) / (jnp.sqrt(vn / c2) + ADAM_EPS) + ADAM_WD * w_ref[...])
        mo_ref[...] = mn
        vo_ref[...] = vn

    sds = jax.ShapeDtypeStruct((r, c), F32)
    return pl.pallas_call(
        body, name="adamw_small", out_shape=[sds] * 3,
        compiler_params=_cp(12 * 4 * r * c + 8 * MIB),
    )(g, w, m, v)


def _allreduce_small(v):
    r, c = v.shape
    assert r % (2 * SUBLANES) == 0
    h = r // 2

    def body(v_ref, out_ref, got_ref, chip_ref, slots_ref, send, recv):
        x, y, cc = _my_place()
        k = 2 * x + y
        sib = (x, y, 1 - cc)

        def rc(s, src, dst, to):
            return pltpu.make_async_remote_copy(src_ref=src, dst_ref=dst, send_sem=send.at[s], recv_sem=recv.at[s],
                                                device_id=to, device_id_type=MESH)

        pair = rc(0, v_ref, got_ref, sib)
        pair.start()
        pair.wait()
        chip_ref[...] = v_ref[...] + got_ref[...]
        mine = pl.ds(pl.multiple_of(cc * h, SUBLANES), h)
        theirs = pl.ds(pl.multiple_of((1 - cc) * h, SUBLANES), h)
        chips = _other_chips(x, y)
        sent = []
        for j, (cx, cy) in enumerate(chips):
            cp = rc(1 + j, chip_ref.at[mine], slots_ref.at[k], (cx, cy, cc))
            cp.start()
            sent.append(cp)
        slots_ref[k] = chip_ref[mine, :]
        for j, (cx, cy) in enumerate(chips):
            rc(1 + j, chip_ref.at[mine], slots_ref.at[2 * cx + cy], (cx, cy, cc)).wait_recv()
        total = slots_ref[0]
        for kk in range(1, N_CHIPS):
            total = total + slots_ref[kk]
        out_ref[mine, :] = total
        for cp in sent:
            cp.wait_send()
        share = rc(N_CHIPS, out_ref.at[mine], out_ref.at[mine], sib)
        share.start()
        rc(N_CHIPS, out_ref.at[theirs], out_ref.at[theirs], sib).wait_recv()
        share.wait_send()

    vm = pl.BlockSpec(memory_space=pltpu.VMEM)
    return pl.pallas_call(
        body, name="allreduce_small", out_shape=jax.ShapeDtypeStruct((r, c), F32), in_specs=[vm], out_specs=vm,
        scratch_shapes=[pltpu.VMEM((r, c), F32), pltpu.VMEM((r, c), F32), pltpu.VMEM((N_CHIPS, h, c), F32),
                        pltpu.SemaphoreType.DMA((N_CHIPS + 1,)), pltpu.SemaphoreType.DMA((N_CHIPS + 1,))],
        compiler_params=_cp(6 * 4 * r * c + 8 * MIB),
    )(v)


def _zoh(a_re, a_im, log_dt, b_re2, b_im2, expand):
    dt = jnp.exp(log_dt)
    mag = jnp.exp(a_re * dt)
    lbr, lbi = mag * jnp.cos(a_im * dt), mag * jnp.sin(a_im * dt)
    nr, ni = lbr - 1.0, lbi
    den = a_re * a_re + a_im * a_im
    qr = (nr * a_re + ni * a_im) / den
    qi = (ni * a_re - nr * a_im) / den
    qr2 = jnp.dot(qr, expand, precision=lax.Precision.HIGHEST, preferred_element_type=F32)
    qi2 = jnp.dot(qi, expand, precision=lax.Precision.HIGHEST, preferred_element_type=F32)
    return lbr, lbi, qr2 * b_re2 - qi2 * b_im2, qr2 * b_im2 + qi2 * b_re2


def _zoh_fwd(a_re, a_im, log_dt, b_re2, b_im2, expand, power):
    g, p = a_re.shape

    def body(ar_ref, ai_ref, ld_ref, br_ref, bi_ref, e_ref, lbr_ref, lbi_ref, bbr_ref, bbi_ref, pw_ref):
        ar, ai, ld = ar_ref[...], ai_ref[...], ld_ref[...]
        lbr, lbi, bbr, bbi = _zoh(ar, ai, ld, br_ref[...], bi_ref[...], e_ref[...])
        lbr_ref[...], lbi_ref[...], bbr_ref[...], bbi_ref[...] = lbr, lbi, bbr, bbi
        dt = jnp.exp(ld) * float(power)
        mag = jnp.exp(ar * dt)
        pw_ref[0] = mag * jnp.cos(ai * dt)
        pw_ref[1] = mag * jnp.sin(ai * dt)

    gp = jax.ShapeDtypeStruct((g, p), F32)
    gph = jax.ShapeDtypeStruct(b_re2.shape, F32)
    return pl.pallas_call(
        body, name="zoh_fwd", out_shape=[gp, gp, gph, gph, jax.ShapeDtypeStruct((2, g, p), F32)],
        compiler_params=_cp(16 * MIB),
    )(a_re, a_im, log_dt, b_re2, b_im2, expand)


def _zoh_bwd(a_re, a_im, log_dt, b_re2, b_im2, expand, g_lbr, g_lbi, g_bbr, g_bbi):
    def body(ar_ref, ai_ref, ld_ref, br_ref, bi_ref, e_ref, c0, c1, c2, c3, gar, gai, gld, gbr, gbi):
        e = e_ref[...]
        _, vjp = jax.vjp(lambda a, b, c, d, f: _zoh(a, b, c, d, f, e),
                         ar_ref[...], ai_ref[...], ld_ref[...], br_ref[...], bi_ref[...])
        gar[...], gai[...], gld[...], gbr[...], gbi[...] = vjp((c0[...], c1[...], c2[...], c3[...]))

    sds = lambda a: jax.ShapeDtypeStruct(a.shape, F32)
    return pl.pallas_call(
        body, name="zoh_bwd", out_shape=[sds(a_re), sds(a_im), sds(log_dt), sds(b_re2), sds(b_im2)],
        compiler_params=_cp(16 * MIB),
    )(a_re, a_im, log_dt, b_re2, b_im2, expand, g_lbr, g_lbi, g_bbr, g_bbi)


def _blockdiag(blocks):
    nq, _, r, c = blocks.shape
    eye = jnp.eye(GROUPS_PER_SLAB, dtype=blocks.dtype)
    out = blocks[:, :, :, None, :] * eye[None, :, None, :, None]
    return out.reshape(nq, GROUPS_PER_SLAB * r, GROUPS_PER_SLAB * c)


def _blockdiag_take(dense, r, c):
    nq = dense.shape[0]
    eye = jnp.eye(GROUPS_PER_SLAB, dtype=dense.dtype)
    d6 = dense.reshape(nq, GROUPS_PER_SLAB, r, GROUPS_PER_SLAB, c)
    return jnp.sum(d6 * eye[None, :, None, :, None], axis=3)


def _scan_slab(s_ref, row0, q, lam_ref, pw_ref, car_ref, steps, reverse, prev_ref=None, prev_row0=0, glam_ref=None):
    sign = -1.0 if reverse else 1.0
    half = SLAB_STATES // SLAB
    cols = [(q * 2 * half + m, q * 2 * half + half + m, q * SLAB_STATES + m * SLAB) for m in range(half)]
    nm = len(cols)
    full = (SUBLANES, SLAB)
    lam = [(jnp.broadcast_to(lam_ref[0:1, pl.ds(cl, SLAB)], full),
            jnp.broadcast_to(sign * lam_ref[1:2, pl.ds(cl, SLAB)], full)) for (_, _, cl) in cols]

    def step_rows(jj, base):
        j = (steps - 1 - jj) if reverse else jj
        return j, pl.ds(pl.multiple_of(base + j * SUBLANES, SUBLANES), SUBLANES)

    def pass1(jj, car):
        _, rows = step_rows(jj, row0)
        out = []
        for m, (cr, ci, _) in enumerate(cols):
            sr, si = car[2 * m], car[2 * m + 1]
            lr, li = lam[m]
            nr = lr * sr - li * si + s_ref[cr, rows, :]
            ni = lr * si + li * sr + s_ref[ci, rows, :]
            s_ref[cr, rows, :] = nr
            s_ref[ci, rows, :] = ni
            out += [nr, ni]
        return tuple(out)

    def unrolled(step_fn):
        def outer(jo, carry):
            for k in range(SCAN_UNROLL):
                carry = step_fn(jo * SCAN_UNROLL + k, carry)
            return carry
        return outer

    assert steps % SCAN_UNROLL == 0
    ends = lax.fori_loop(0, steps // SCAN_UNROLL, unrolled(pass1), tuple(jnp.zeros(full, F32) for _ in range(2 * nm)))

    entry = []
    for m, (cr, ci, cl) in enumerate(cols):
        ljr = pw_ref[0:1, pl.ds(cl, SLAB)]
        lji = sign * pw_ref[1:2, pl.ds(cl, SLAB)]
        c_r = car_ref[cr, 0:1, :]
        c_i = car_ref[ci, 0:1, :]
        rows_r, rows_i = [None] * SUBLANES, [None] * SUBLANES
        order = range(SUBLANES - 1, -1, -1) if reverse else range(SUBLANES)
        for b in order:
            rows_r[b], rows_i[b] = c_r, c_i
            e_r, e_i = ends[2 * m][b:b + 1], ends[2 * m + 1][b:b + 1]
            c_r, c_i = ljr * c_r - lji * c_i + e_r, ljr * c_i + lji * c_r + e_i
        car_ref[cr, 0:1, :] = c_r
        car_ref[ci, 0:1, :] = c_i
        entry.append((jnp.concatenate(rows_r, axis=0), jnp.concatenate(rows_i, axis=0)))

    def pass2(jj, carry):
        j, rows = step_rows(jj, row0)
        decayed, acc = carry[:2 * nm], carry[2 * nm:]
        out_d, out_a = [], []
        for m, (cr, ci, cl) in enumerate(cols):
            lr, li = lam[m]
            dr, di = decayed[2 * m], decayed[2 * m + 1]
            dr, di = lr * dr - li * di, lr * di + li * dr
            nr = s_ref[cr, rows, :] + dr
            ni = s_ref[ci, rows, :] + di
            s_ref[cr, rows, :] = nr
            s_ref[ci, rows, :] = ni
            out_d += [dr, di]
            if prev_ref is not None:
                prow = pl.ds(pl.multiple_of(prev_row0 + (j - 1) * SUBLANES, SUBLANES), SUBLANES)
                qr = prev_ref[cr, prow, :]
                qi = prev_ref[ci, prow, :]
                out_a += [acc[2 * m] + (nr * qr + ni * qi), acc[2 * m + 1] + (ni * qr - nr * qi)]
        return tuple(out_d) + tuple(out_a)

    n_acc = 2 * nm if prev_ref is not None else 0
    init = tuple(e for pair in entry for e in pair) + tuple(jnp.zeros(full, F32) for _ in range(n_acc))
    accs = lax.fori_loop(0, steps // SCAN_UNROLL, unrolled(pass2), init)[2 * nm:]
    if prev_ref is not None:
        for m, (_, _, cl) in enumerate(cols):
            glam_ref[0:1, pl.ds(cl, SLAB)] += jnp.sum(accs[2 * m], axis=0, keepdims=True)
            glam_ref[1:2, pl.ds(cl, SLAB)] += jnp.sum(accs[2 * m + 1], axis=0, keepdims=True)


def _permute_rows_f32(perm_bf16, v):
    hi = v.astype(BF16)
    lo = (v - hi.astype(F32)).astype(BF16)
    return (jnp.dot(perm_bf16, hi, preferred_element_type=F32) + jnp.dot(perm_bf16, lo, preferred_element_type=F32))


def _put_slab(s_ref, rows, q, val):
    per = 2 * SLAB_STATES // SLAB
    for i in range(per):
        s_ref[q * per + i, rows, :] = val[:, i * SLAB:(i + 1) * SLAB]


def _get_slab(s_ref, rows, q):
    per = 2 * SLAB_STATES // SLAB
    return jnp.concatenate([s_ref[q * per + i, rows, :] for i in range(per)], axis=1)


def _step_major_perm(tt):
    r = jnp.arange(tt)
    held = (r % SUBLANES) * (tt // SUBLANES) + r // SUBLANES
    return held[:, None] == r[None, :]


def _ssm_fwd(proj, u_block, bq, cq, lam, pw, d_skip):
    t = proj.shape[0]
    nq, ds, w2 = bq.shape
    assert ds == SLAB and w2 == 2 * SLAB_STATES
    dssm = nq * SLAB
    width = nq * w2
    ntile = width // SLAB
    tt = min(T_SCAN, t)
    steps = tt // SUBLANES
    perm = _step_major_perm(tt)
    pm, pmt = perm.astype(BF16), perm.T.astype(BF16)

    def body(u_ref, pm_ref, pmt_ref, bq_ref, cq_ref, lam_ref, pw_ref, d_ref, y_ref, cin_ref, s_ref, car_ref):
        @pl.when(pl.program_id(0) == 0)
        def _():
            car_ref[...] = jnp.zeros_like(car_ref)

        cin_ref[...] = jnp.broadcast_to(car_ref[:, 0:1, :], cin_ref.shape)
        u = u_ref[...]
        ub = jnp.dot(pm_ref[...], u.astype(BF16), preferred_element_type=F32).astype(BF16)
        everything = slice(None)
        for q in range(nq):
            _put_slab(s_ref, everything, q,
                      jnp.dot(ub[:, q * SLAB:(q + 1) * SLAB], bq_ref[q], preferred_element_type=F32))
        for q in range(nq):
            _scan_slab(s_ref, 0, q, lam_ref, pw_ref, car_ref, steps, reverse=False)
        y_sm = jnp.concatenate(
            [jnp.dot(_get_slab(s_ref, everything, q).astype(BF16), cq_ref[q], preferred_element_type=F32)
             for q in range(nq)], axis=1)
        y_ref[...] = _permute_rows_f32(pmt_ref[...], y_sm) + d_ref[...] * u

    c3 = lambda i: (0, 0, 0)
    c2 = lambda i: (0, 0)
    return pl.pallas_call(
        body, name="ssm_fwd", grid=(t // tt,),
        in_specs=[pl.BlockSpec((tt, dssm), lambda i: (i, u_block)), pl.BlockSpec((tt, tt), c2),
                  pl.BlockSpec((tt, tt), c2), pl.BlockSpec(bq.shape, c3),
                  pl.BlockSpec(cq.shape, c3), pl.BlockSpec(lam.shape, c2), pl.BlockSpec(pw.shape, c2),
                  pl.BlockSpec((1, dssm), c2)],
        out_specs=[pl.BlockSpec((tt, dssm), lambda i: (i, 0)),
                   pl.BlockSpec((None, ntile, SUBLANES, SLAB), lambda i: (i, 0, 0, 0))],
        out_shape=[jax.ShapeDtypeStruct((t, dssm), F32), jax.ShapeDtypeStruct((t // tt, ntile, SUBLANES, SLAB), F32)],
        scratch_shapes=[pltpu.VMEM((ntile, tt, SLAB), F32), pltpu.VMEM((ntile, SUBLANES, SLAB), F32)],
        compiler_params=_cp(2 * (8 * tt * dssm + 4 * nq * ds * w2) + 4 * tt * width + 16 * MIB,
                            dimension_semantics=("arbitrary",)),
    )(proj, pm, pmt, bq, cq, lam, pw, d_skip)


def _ssm_bwd(proj, u_block, dy, cin, dproj, bq, cq, lam, pw, d_skip):
    t = proj.shape[0]
    nq, ds, w2 = bq.shape
    dssm = nq * SLAB
    width = nq * w2
    ntile = width // SLAB
    tt = min(T_SCAN, t)
    nt = t // tt
    steps = tt // SUBLANES
    halo = SUBLANES
    perm = _step_major_perm(tt)
    pm, pmt = perm.astype(BF16), perm.T.astype(BF16)

    def body(u_ref, dy_ref, cin_ref, dp_any, pm_ref, pmt_ref, bq_ref, cq_ref, lam_ref, pw_ref, d_ref,
             du_ref, gb_ref, gc_ref, glam_ref, gd_ref, s_ref, gs_ref, car_f, car_b):
        del dp_any

        @pl.when(pl.program_id(0) == 0)
        def _():
            car_b[...] = jnp.zeros_like(car_b)
            gb_ref[...] = jnp.zeros_like(gb_ref)
            gc_ref[...] = jnp.zeros_like(gc_ref)
            glam_ref[...] = jnp.zeros_like(glam_ref)
            gd_ref[...] = jnp.zeros_like(gd_ref)

        u = u_ref[...]
        dyv = dy_ref[...]
        gd_ref[0:1, :] += jnp.sum(dyv * u, axis=0, keepdims=True)
        pmv = pm_ref[...]
        ub = jnp.dot(pmv, u.astype(BF16), preferred_element_type=F32).astype(BF16)
        dyb = jnp.dot(pmv, dyv.astype(BF16), preferred_element_type=F32).astype(BF16)
        car_f[...] = cin_ref[...]
        data = slice(halo, halo + tt)
        everything = slice(None)
        for q in range(nq):
            _put_slab(s_ref, data, q, jnp.dot(ub[:, q * SLAB:(q + 1) * SLAB], bq_ref[q], preferred_element_type=F32))
        for q in range(nq):
            _scan_slab(s_ref, halo, q, lam_ref, pw_ref, car_f, steps, reverse=False)
        last_step = s_ref[:, halo + tt - SUBLANES:halo + tt, :]
        s_ref[:, 0:halo, :] = jnp.concatenate([cin_ref[:, 0:1, :], last_step[:, 0:SUBLANES - 1, :]], axis=1)
        tn = (((0,), (0,)), ((), ()))
        nt_dims = (((1,), (1,)), ((), ()))
        for q in range(nq):
            sl = slice(q * SLAB, (q + 1) * SLAB)
            gc_ref[q] += lax.dot_general(dyb[:, sl], _get_slab(s_ref, data, q).astype(BF16), tn,
                                         preferred_element_type=F32)
            _put_slab(gs_ref, everything, q,
                      lax.dot_general(dyb[:, sl], cq_ref[q], nt_dims, preferred_element_type=F32))
        for q in range(nq):
            _scan_slab(gs_ref, 0, q, lam_ref, pw_ref, car_b, steps, reverse=True,
                       prev_ref=s_ref, prev_row0=halo, glam_ref=glam_ref)
        du_parts = []
        for q in range(nq):
            sl = slice(q * SLAB, (q + 1) * SLAB)
            gsb = _get_slab(gs_ref, everything, q).astype(BF16)
            du_parts.append(lax.dot_general(gsb, bq_ref[q], nt_dims, preferred_element_type=F32))
            gb_ref[q] += lax.dot_general(ub[:, sl], gsb, tn, preferred_element_type=F32)
        du_sm = jnp.concatenate(du_parts, axis=1)
        du_ref[...] = (_permute_rows_f32(pmt_ref[...], du_sm) + dyv * d_ref[...]).astype(BF16)

    c3 = lambda i: (0, 0, 0)
    c2 = lambda i: (0, 0)
    rev = lambda i: (nt - 1 - i, 0)
    dense = jax.ShapeDtypeStruct((nq, SLAB, w2), F32)
    gp = lam.shape[1]
    return pl.pallas_call(
        body, name="ssm_bwd", grid=(nt,),
        in_specs=[pl.BlockSpec((tt, dssm), lambda i: (nt - 1 - i, u_block)), pl.BlockSpec((tt, dssm), rev),
                  pl.BlockSpec((None, ntile, SUBLANES, SLAB), lambda i: (nt - 1 - i, 0, 0, 0)),
                  pl.BlockSpec(memory_space=pl.ANY), pl.BlockSpec((tt, tt), c2), pl.BlockSpec((tt, tt), c2),
                  pl.BlockSpec(bq.shape, c3), pl.BlockSpec(cq.shape, c3),
                  pl.BlockSpec(lam.shape, c2), pl.BlockSpec(pw.shape, c2), pl.BlockSpec((1, dssm), c2)],
        out_specs=[pl.BlockSpec((tt, dssm), lambda i: (nt - 1 - i, u_block)), pl.BlockSpec(dense.shape, c3),
                   pl.BlockSpec(dense.shape, c3), pl.BlockSpec((SUBLANES, gp), c2), pl.BlockSpec((SUBLANES, dssm), c2)],
        out_shape=[jax.ShapeDtypeStruct(dproj.shape, dproj.dtype), dense, dense,
                   jax.ShapeDtypeStruct((SUBLANES, gp), F32), jax.ShapeDtypeStruct((SUBLANES, dssm), F32)],
        scratch_shapes=[pltpu.VMEM((ntile, tt + halo, SLAB), F32), pltpu.VMEM((ntile, tt, SLAB), F32),
                        pltpu.VMEM((ntile, SUBLANES, SLAB), F32), pltpu.VMEM((ntile, SUBLANES, SLAB), F32)],
        input_output_aliases={3: 0},
        compiler_params=_cp(2 * (10 * tt * dssm + 4 * nq * ds * w2 + 8 * nq * SLAB * w2)
                            + 8 * tt * width + 12 * MIB, dimension_semantics=("arbitrary",)),
    )(proj, dy, cin, dproj, pm, pmt, bq, cq, lam, pw, d_skip)


def _gate_fwd(proj, y, conv_w, conv_b, w_glu, b_glu, dc):
    t = proj.shape[0]
    dssm = y.shape[1]
    assert dc == dssm
    tm = min(TM_GATE, t)
    halo = SUBLANES

    def body(b_ref, c_ref, v_ref, zc_ref, zs_ref, y_ref, cw_ref, cb_ref, wg_ref, bg_ref, mix_ref, cv_buf):
        @pl.when(pl.program_id(0) == 0)
        def _():
            cv_buf[0:halo, :] = jnp.zeros((halo, dc), F32)

        cv = c_ref[...] * v_ref[...]
        cv_buf[halo:, :] = cv
        conv = (cb_ref[...] + cw_ref[2:3, :] * cv + cw_ref[1:2, :] * cv_buf[halo - 1:halo - 1 + tm, :]
                + cw_ref[0:1, :] * cv_buf[halo - 2:halo - 2 + tm, :])
        sz, _ = _silu(zc_ref[...])
        mix_ref[:, 0:dc] = (b_ref[...] * conv * sz).astype(BF16)
        cv_buf[0:halo, :] = cv_buf[tm:tm + halo, :]
        ge, _ = _gelu(y_ref[...])
        gl = jnp.dot(ge.astype(BF16), wg_ref[...], preferred_element_type=F32) + bg_ref[...]
        szs, _ = _silu(zs_ref[...])
        mix_ref[:, dc:] = (ge * jax.nn.sigmoid(gl) * szs).astype(BF16)

    col = lambda j: pl.BlockSpec((tm, dc), lambda i, j=j: (i, j))
    fixed = lambda i: (0, 0)
    return pl.pallas_call(
        body, name="gate_fwd", grid=(t // tm,),
        in_specs=[col(0), col(1), col(2), col(3), col(5), pl.BlockSpec((tm, dssm), lambda i: (i, 0)),
                  pl.BlockSpec(conv_w.shape, fixed), pl.BlockSpec((1, dc), fixed),
                  pl.BlockSpec(w_glu.shape, fixed), pl.BlockSpec((1, dssm), fixed)],
        out_specs=pl.BlockSpec((tm, dc + dssm), lambda i: (i, 0)),
        out_shape=jax.ShapeDtypeStruct((t, dc + dssm), BF16),
        scratch_shapes=[pltpu.VMEM((tm + halo, dc), F32)],
        compiler_params=_cp(2 * (6 * 4 * tm * dc + 2 * tm * (dc + dssm) + 2 * dssm * dssm) + 24 * tm * dc + 8 * MIB,
                            dimension_semantics=("arbitrary",)),
    )(proj, proj, proj, proj, proj, y, conv_w, conv_b, w_glu, b_glu)


def _gate_bwd(proj, y, dmix, conv_w, conv_b, w_glu, b_glu, dc):
    t = proj.shape[0]
    dssm = y.shape[1]
    tm = min(TM_GATE, t)
    nt = t // tm
    halo = SUBLANES
    blocks_per_tile = tm // halo

    def body(b_ref, c_ref, v_ref, zc_ref, zs_ref, cp_ref, vp_ref, y_ref, dm_ref, cw_ref, cb_ref, wg_ref, bg_ref,
             dp_ref, dy_ref, gs_ref, gwg_ref, cv_buf, dc_buf):
        i = pl.program_id(0)

        @pl.when(i == 0)
        def _():
            dc_buf[tm:, :] = jnp.zeros((halo, dc), F32)
            gs_ref[...] = jnp.zeros_like(gs_ref)
            gwg_ref[...] = jnp.zeros_like(gwg_ref)

        first_tile = (i == nt - 1)
        bv, cg, vv, zc = b_ref[...], c_ref[...], v_ref[...], zc_ref[...]
        cv = cg * vv
        cv_buf[0:halo, :] = jnp.where(first_tile, 0.0, cp_ref[...] * vp_ref[...])
        cv_buf[halo:, :] = cv
        w0, w1, w2 = cw_ref[0:1, :], cw_ref[1:2, :], cw_ref[2:3, :]
        conv = (cb_ref[...] + w2 * cv + w1 * cv_buf[halo - 1:halo - 1 + tm, :]
                + w0 * cv_buf[halo - 2:halo - 2 + tm, :])
        sz, sgc = _silu(zc)
        dyc = dm_ref[:, 0:dc]
        dp_ref[:, 0:dc] = (dyc * conv * sz).astype(BF16)
        dp_ref[:, 3 * dc:4 * dc] = (dyc * bv * conv * _dsilu(zc, sgc)).astype(BF16)
        dconv = dyc * bv * sz
        dc_buf[0:tm, :] = dconv
        d1 = dc_buf[1:1 + tm, :]
        d2 = dc_buf[2:2 + tm, :]
        dcv = w2 * dconv + w1 * d1 + w0 * d2
        dp_ref[:, dc:2 * dc] = (dcv * vv).astype(BF16)
        dp_ref[:, 2 * dc:3 * dc] = (dcv * cg).astype(BF16)
        gs_ref[0:1, :] += jnp.sum(cv * d2, axis=0, keepdims=True)
        gs_ref[1:2, :] += jnp.sum(cv * d1, axis=0, keepdims=True)
        gs_ref[2:3, :] += jnp.sum(cv * dconv, axis=0, keepdims=True)
        gs_ref[3:4, :] += jnp.sum(dconv, axis=0, keepdims=True)
        dc_buf[tm:, :] = dc_buf[0:halo, :]

        yv, zs = y_ref[...], zs_ref[...]
        ge, th = _gelu(yv)
        geb = ge.astype(BF16)
        wg = wg_ref[...]
        gl = jnp.dot(geb, wg, preferred_element_type=F32) + bg_ref[...]
        sg = jax.nn.sigmoid(gl)
        szs, sgs = _silu(zs)
        dys = dm_ref[:, dc:]
        ys = ge * sg
        dp_ref[:, 4 * dc:5 * dc] = jnp.zeros((tm, dc), BF16)
        dp_ref[:, 5 * dc:] = (dys * ys * _dsilu(zs, sgs)).astype(BF16)
        d_ys = dys * szs
        dgl = d_ys * ge * sg * (1.0 - sg)
        dglb = dgl.astype(BF16)
        gs_ref[4:5, :] += jnp.sum(dgl, axis=0, keepdims=True)
        gwg_ref[...] += lax.dot_general(geb, dglb, (((0,), (0,)), ((), ())), preferred_element_type=F32)
        dge = d_ys * sg + lax.dot_general(dglb, wg, (((1,), (1,)), ((), ())), preferred_element_type=F32)
        dy_ref[...] = dge * _dgelu(yv, th)

    col = lambda j: pl.BlockSpec((tm, dc), lambda i, j=j: (nt - 1 - i, j))
    prev = lambda j: pl.BlockSpec((halo, dc), lambda i, j=j: (jnp.maximum((nt - 1 - i) * blocks_per_tile - 1, 0), j))
    rev = lambda i: (nt - 1 - i, 0)
    fixed = lambda i: (0, 0)
    return pl.pallas_call(
        body, name="gate_bwd", grid=(nt,),
        in_specs=[col(0), col(1), col(2), col(3), col(5), prev(1), prev(2), pl.BlockSpec((tm, dssm), rev),
                  pl.BlockSpec((tm, dc + dssm), rev), pl.BlockSpec(conv_w.shape, fixed), pl.BlockSpec((1, dc), fixed),
                  pl.BlockSpec(w_glu.shape, fixed), pl.BlockSpec((1, dssm), fixed)],
        out_specs=[pl.BlockSpec((tm, 6 * dc), rev), pl.BlockSpec((tm, dssm), rev),
                   pl.BlockSpec((2 * SUBLANES, dc), fixed), pl.BlockSpec((dssm, dssm), fixed)],
        out_shape=[jax.ShapeDtypeStruct((t, 6 * dc), BF16), jax.ShapeDtypeStruct((t, dssm), F32),
                   jax.ShapeDtypeStruct((2 * SUBLANES, dc), F32), jax.ShapeDtypeStruct((dssm, dssm), F32)],
        scratch_shapes=[pltpu.VMEM((tm + halo, dc), F32), pltpu.VMEM((tm + halo, dc), F32)],
        compiler_params=_cp(2 * (6 * 4 * tm * dc + 8 * tm * dc + 12 * tm * dc + 4 * tm * dc + 6 * dssm * dssm)
                            + 40 * tm * dc + 8 * MIB, dimension_semantics=("arbitrary",)),
    )(proj, proj, proj, proj, proj, proj, proj, y, dmix, conv_w, conv_b, w_glu, b_glu)


def _allgather_halves(parts, name):
    n = len(parts)
    per = 2 * (N_CHIPS - 1)

    def body(*refs):
        ins, outs = refs[:n], refs[n:2 * n]
        send, recv, local = refs[2 * n:]
        x, y, c = _my_place()
        k = 2 * x + y
        sib = (x, y, 1 - c)
        chips = _other_chips(x, y)

        def rc(t, s, src, dst, to):
            return pltpu.make_async_remote_copy(src_ref=src, dst_ref=dst, send_sem=send.at[t * per + s],
                                                recv_sem=recv.at[t * per + s], device_id=to, device_id_type=MESH)

        mine = [pltpu.make_async_copy(ins[t], outs[t].at[k], local.at[t]) for t in range(n)]
        for cp in mine:
            cp.start()
        sent = []
        for t in range(n):
            for j, (cx, cy) in enumerate(chips):
                cp = rc(t, j, ins[t].at[c], outs[t].at[k, c], (cx, cy, c))
                cp.start()
                sent.append(cp)
        for t in range(n):
            for j, (cx, cy) in enumerate(chips):
                landed = outs[t].at[2 * cx + cy, c]
                rc(t, j, landed, landed, (cx, cy, c)).wait_recv()
                cp = rc(t, N_CHIPS - 1 + j, landed, landed, sib)
                cp.start()
                sent.append(cp)
        for t in range(n):
            for j, (cx, cy) in enumerate(chips):
                other = outs[t].at[2 * cx + cy, 1 - c]
                rc(t, N_CHIPS - 1 + j, other, other, sib).wait_recv()
        for cp in sent:
            cp.wait_send()
        for cp in mine:
            cp.wait()

    any_spec = pl.BlockSpec(memory_space=pl.ANY)
    return pl.pallas_call(
        body, name=name, in_specs=[any_spec] * n, out_specs=[any_spec] * n,
        out_shape=[jax.ShapeDtypeStruct((N_CHIPS,) + p.shape, p.dtype) for p in parts],
        scratch_shapes=[pltpu.SemaphoreType.DMA((n * per,)), pltpu.SemaphoreType.DMA((n * per,)),
                        pltpu.SemaphoreType.DMA((n,))],
        compiler_params=_cp(16 * MIB),
    )(*parts)


def _allgather_flat(v, name):
    r, c = v.shape
    rels = [(dx, dy, dc) for dx in (0, 1) for dy in (0, 1) for dc in (0, 1)][1:]

    def body(v_ref, out_ref, send, recv):
        x, y, cc = _my_place()
        me = 4 * x + 2 * y + cc

        def peer(rel):
            dx, dy, dc = rel
            return (1 - x if dx else x, 1 - y if dy else y, 1 - cc if dc else cc)

        def rc(s, slot, to):
            return pltpu.make_async_remote_copy(src_ref=v_ref, dst_ref=out_ref.at[slot], send_sem=send.at[s],
                                                recv_sem=recv.at[s], device_id=to, device_id_type=MESH)

        sent = []
        for s, rel in enumerate(rels):
            cp = rc(s, me, peer(rel))
            cp.start()
            sent.append(cp)
        out_ref[me] = v_ref[...]
        for s, rel in enumerate(rels):
            px, py, pc = peer(rel)
            rc(s, 4 * px + 2 * py + pc, (px, py, pc)).wait_recv()
        for cp in sent:
            cp.wait_send()

    return pl.pallas_call(
        body, name=name, out_shape=jax.ShapeDtypeStruct((N_DEV, r, c), F32),
        in_specs=[pl.BlockSpec(memory_space=pltpu.VMEM)], out_specs=pl.BlockSpec(memory_space=pltpu.VMEM),
        scratch_shapes=[pltpu.SemaphoreType.DMA((N_DEV - 1,)), pltpu.SemaphoreType.DMA((N_DEV - 1,))],
        compiler_params=_cp((N_DEV + 2) * 4 * r * c + 8 * MIB),
    )(v)


def _rs_pair_exchange(grads):
    n = len(grads)

    def body(*refs):
        ins, outs = refs[:n], refs[n:2 * n]
        send, recv = refs[2 * n:]
        x, y, c = _my_place()
        cps = []
        for t in range(n):
            cp = pltpu.make_async_remote_copy(src_ref=ins[t].at[:, 1 - c], dst_ref=outs[t], send_sem=send.at[t],
                                              recv_sem=recv.at[t], device_id=(x, y, 1 - c), device_id_type=MESH)
            cp.start()
            cps.append(cp)
        for cp in cps:
            cp.wait()

    any_spec = pl.BlockSpec(memory_space=pl.ANY)
    return pl.pallas_call(
        body, name="rs_pair_exchange", in_specs=[any_spec] * n, out_specs=[any_spec] * n,
        out_shape=[jax.ShapeDtypeStruct((g.shape[0],) + g.shape[2:], F32) for g in grads],
        scratch_shapes=[pltpu.SemaphoreType.DMA((n,)), pltpu.SemaphoreType.DMA((n,))],
        compiler_params=_cp(16 * MIB),
    )(*grads)


def _rs_pair_add(place, grad, got, name):
    nk, _, r2, c = grad.shape
    tr = min(TR_ELT, r2)

    def body(place_ref, g_ref, r_ref, o16_ref, o32_ref):
        del place_ref
        s = g_ref[...] + r_ref[...]
        o32_ref[...] = s
        o16_ref[...] = s.astype(BF16)

    blk = pl.BlockSpec((None, tr, c), lambda k, i, p: (k, i, 0))
    return pl.pallas_call(
        body, name=name,
        grid_spec=pltpu.PrefetchScalarGridSpec(
            num_scalar_prefetch=1, grid=(nk, r2 // tr),
            in_specs=[pl.BlockSpec((None, None, tr, c), lambda k, i, p: (k, p[0], i, 0)), blk],
            out_specs=[blk, blk]),
        out_shape=[jax.ShapeDtypeStruct((nk, r2, c), BF16), jax.ShapeDtypeStruct((nk, r2, c), F32)],
        compiler_params=_cp(2 * 14 * tr * c + 8 * MIB),
    )(place, grad, got)


def _rs_chip_exchange(sums16):
    n = len(sums16)
    per = N_CHIPS - 1

    def body(*refs):
        ins, outs = refs[:n], refs[n:2 * n]
        send, recv = refs[2 * n:]
        x, y, c = _my_place()
        chips = _other_chips(x, y)
        cps = []
        for t in range(n):
            for j, (cx, cy) in enumerate(chips):
                cp = pltpu.make_async_remote_copy(src_ref=ins[t].at[2 * cx + cy], dst_ref=outs[t].at[j],
                                                  send_sem=send.at[t * per + j], recv_sem=recv.at[t * per + j],
                                                  device_id=(cx, cy, c), device_id_type=MESH)
                cp.start()
                cps.append(cp)
        for cp in cps:
            cp.wait()

    any_spec = pl.BlockSpec(memory_space=pl.ANY)
    return pl.pallas_call(
        body, name="rs_chip_exchange", in_specs=[any_spec] * n, out_specs=[any_spec] * n,
        out_shape=[jax.ShapeDtypeStruct((per,) + s.shape[1:], BF16) for s in sums16],
        scratch_shapes=[pltpu.SemaphoreType.DMA((n * per,)), pltpu.SemaphoreType.DMA((n * per,))],
        compiler_params=_cp(16 * MIB),
    )(*sums16)


def _rs_chip_add(place, sums32, got, name):
    _, r2, c = sums32.shape
    tr = min(TR_ELT, r2)

    def body(place_ref, s_ref, q_ref, o_ref):
        del place_ref
        o_ref[...] = ((s_ref[...] + q_ref[0].astype(F32)) + q_ref[1].astype(F32)) + q_ref[2].astype(F32)

    return pl.pallas_call(
        body, name=name,
        grid_spec=pltpu.PrefetchScalarGridSpec(
            num_scalar_prefetch=1, grid=(r2 // tr,),
            in_specs=[pl.BlockSpec((None, tr, c), lambda i, p: (p[1], i, 0)),
                      pl.BlockSpec((N_CHIPS - 1, tr, c), lambda i, p: (0, i, 0))],
            out_specs=pl.BlockSpec((tr, c), lambda i, p: (i, 0))),
        out_shape=jax.ShapeDtypeStruct((r2, c), F32),
        compiler_params=_cp(2 * 14 * tr * c + 8 * MIB),
    )(place, sums32, got)


def _rs_pair_share(halves):
    n = len(halves)

    def body(*refs):
        ins, outs = refs[:n], refs[n:2 * n]
        send, recv, local = refs[2 * n:]
        x, y, c = _my_place()
        cps, mine = [], []
        for t in range(n):
            lc = pltpu.make_async_copy(ins[t], outs[t].at[c], local.at[t])
            lc.start()
            mine.append(lc)
            cp = pltpu.make_async_remote_copy(src_ref=ins[t], dst_ref=outs[t].at[c], send_sem=send.at[t],
                                              recv_sem=recv.at[t], device_id=(x, y, 1 - c), device_id_type=MESH)
            cp.start()
            cps.append(cp)
        for t in range(n):
            other = outs[t].at[1 - c]
            pltpu.make_async_remote_copy(src_ref=other, dst_ref=other, send_sem=send.at[t], recv_sem=recv.at[t],
                                         device_id=(x, y, 1 - c), device_id_type=MESH).wait_recv()
        for cp in cps:
            cp.wait_send()
        for lc in mine:
            lc.wait()

    any_spec = pl.BlockSpec(memory_space=pl.ANY)
    return pl.pallas_call(
        body, name="rs_pair_share", in_specs=[any_spec] * n, out_specs=[any_spec] * n,
        out_shape=[jax.ShapeDtypeStruct((2,) + h.shape, F32) for h in halves],
        scratch_shapes=[pltpu.SemaphoreType.DMA((n,)), pltpu.SemaphoreType.DMA((n,)), pltpu.SemaphoreType.DMA((n,))],
        compiler_params=_cp(16 * MIB),
    )(*halves)


_PACK_TILE = SUBLANES * SLAB


def _pack(arrays):
    rows = []
    for a in arrays:
        flat = a.reshape(-1).astype(F32)
        padded = -(-flat.shape[0] // _PACK_TILE) * _PACK_TILE
        rows.append(jnp.pad(flat, (0, padded - flat.shape[0])).reshape(-1, SLAB))
    n_rows = sum(r.shape[0] for r in rows)
    if n_rows % (2 * SUBLANES):
        rows.append(jnp.zeros((SUBLANES, SLAB), F32))
    return jnp.concatenate(rows, axis=0)


def _unpack(packed, shapes):
    out, row = [], 0
    for shp in shapes:
        size = math.prod(shp)
        nrow = -(-size // _PACK_TILE) * SUBLANES
        out.append(packed[row:row + nrow].reshape(-1)[:size].reshape(shp))
        row += nrow
    return out


def kernel(x, norm_pre_g, w_in, conv_w, conv_b, ssm_a_re, ssm_a_im, ssm_log_dt, ssm_b_re, ssm_b_im, ssm_c_re, ssm_c_im, ssm_d, w_glu, b_glu, w_out, norm_post_g, loss_target, m_norm_pre_g, m_w_in, m_conv_w, m_conv_b, m_ssm_a_re, m_ssm_a_im, m_ssm_log_dt, m_ssm_b_re, m_ssm_b_im, m_ssm_c_re, m_ssm_c_im, m_ssm_d, m_w_glu, m_b_glu, m_w_out, m_norm_post_g, v_norm_pre_g, v_w_in, v_conv_w, v_conv_b, v_ssm_a_re, v_ssm_a_im, v_ssm_log_dt, v_ssm_b_re, v_ssm_b_im, v_ssm_c_re, v_ssm_c_im, v_ssm_d, v_w_glu, v_b_glu, v_w_out, v_norm_post_g):
    xs, tgt = x[0], loss_target[0]
    t, d = xs.shape
    dc = conv_b.shape[0]
    dssm = ssm_d.shape[0]
    g, p = ssm_a_re.shape
    h = SSM_H
    nq = dssm // SLAB
    n_shard = w_in.shape[1]
    steps = min(T_SCAN, t) // SUBLANES
    mx, my, mc = _my_place()
    chip = 2 * mx + my
    place = jnp.stack([mc, chip]).astype(jnp.int32)

    halves = lambda a: a.reshape(2, a.shape[0] // 2, a.shape[1])
    win_g, wout_g, wglu_g = _allgather_halves(
        [halves(_cast_bf16(w_in, "cast_w_in")), halves(_cast_bf16(w_out, "cast_w_out")),
         halves(_cast_bf16(w_glu, "cast_w_glu"))], "allgather_weights")
    win_b = win_g.reshape(N_CHIPS, d, n_shard)
    wout_b = wout_g.reshape(dc + dssm, d)
    wglu_b = wglu_g.reshape(dssm, dssm)

    cw_cols = conv_w.shape[1]
    cw_pad = -(-cw_cols // SLAB) * SLAB
    cw_blk = jnp.zeros((SUBLANES, cw_pad), F32).at[:conv_w.shape[0], :cw_cols].set(conv_w)
    cw_all = _allgather_flat(cw_blk, "allgather_conv_w")
    conv_w_full = jnp.concatenate([cw_all[2 * k, :, :cw_cols] for k in range(N_CHIPS)], axis=1)

    expand = jnp.repeat(jnp.eye(p, dtype=F32), h, axis=1)
    b_re2, b_im2 = ssm_b_re.reshape(g, p * h), ssm_b_im.reshape(g, p * h)
    log_dt2 = ssm_log_dt.reshape(g, 1)
    lbr, lbi, bbr2, bbi2, pw3 = _zoh_fwd(ssm_a_re, ssm_a_im, log_dt2, b_re2, b_im2, expand, steps)
    lam = jnp.stack([lbr.reshape(g * p), lbi.reshape(g * p)])
    pw = pw3.reshape(2, g * p)
    to_slab_b = lambda b2: _blockdiag(b2.reshape(nq, GROUPS_PER_SLAB, p, h).transpose(0, 1, 3, 2))
    bq = jnp.concatenate([to_slab_b(bbr2), to_slab_b(bbi2)], axis=2).astype(BF16)
    to_slab_c = lambda c3: _blockdiag(c3.reshape(nq, GROUPS_PER_SLAB, h, p).transpose(0, 1, 3, 2))
    cq = jnp.concatenate([to_slab_c(ssm_c_re), to_slab_c(-ssm_c_im)], axis=1).astype(BF16)

    g_pre2, g_post2 = norm_pre_g.reshape(1, d), norm_post_g.reshape(1, d)
    conv_b2, b_glu2, d_skip2 = conv_b.reshape(1, dc), b_glu.reshape(1, dssm), ssm_d.reshape(1, dssm)
    u_block = 4 * dc // dssm

    hb = _prenorm(xs, g_pre2)
    proj = _matmul_nn(hb, win_b, "inproj")
    y, cin = _ssm_fwd(proj, u_block, bq, cq, lam, pw, d_skip2)
    mix = _gate_fwd(proj, y, conv_w_full, conv_b2, wglu_b, b_glu2, dc)
    loss_blk, dout, dob, dmix, gg_post = _outproj(mix, wout_b, xs, tgt, g_post2)

    gw_out = _matmul_tn(mix, dob, 1, "grad_w_out")
    dproj, dy, gsmall, gw_glu = _gate_bwd(proj, y, dmix, conv_w_full, conv_b2, wglu_b, b_glu2, dc)
    dproj, gb_dense, gc_dense, glam, gd = _ssm_bwd(proj, u_block, dy, cin, dproj, bq, cq, lam, pw, d_skip2)
    gx, gg_pre = _dh_prenorm_bwd(dproj, win_b, xs, dout, g_pre2)
    gw_in = _matmul_tn(hb, dproj, N_CHIPS, "grad_w_in")

    gb4 = gb_dense.reshape(nq, SLAB, 2, SLAB_STATES)
    g_bbr2 = _blockdiag_take(gb4[:, :, 0, :], h, p).transpose(0, 1, 3, 2).reshape(g, p * h)
    g_bbi2 = _blockdiag_take(gb4[:, :, 1, :], h, p).transpose(0, 1, 3, 2).reshape(g, p * h)
    gc4 = gc_dense.reshape(nq, SLAB, 2, SLAB_STATES)
    g_c_re = _blockdiag_take(gc4[:, :, 0, :], h, p).reshape(g, h, p)
    g_c_im = -_blockdiag_take(gc4[:, :, 1, :], h, p).reshape(g, h, p)
    g_a_re, g_a_im, g_ld, g_b_re2, g_b_im2 = _zoh_bwd(
        ssm_a_re, ssm_a_im, log_dt2, b_re2, b_im2, expand,
        glam[0].reshape(g, p), glam[1].reshape(g, p), g_bbr2, g_bbi2)

    small_names = ["norm_pre_g", "conv_w", "conv_b", "ssm_a_re", "ssm_a_im", "ssm_log_dt", "ssm_b_re", "ssm_b_im",
                   "ssm_c_re", "ssm_c_im", "ssm_d", "b_glu", "norm_post_g", "loss"]
    small_g = {
        "norm_pre_g": gg_pre[0], "conv_w": gsmall[0:3], "conv_b": gsmall[3], "ssm_a_re": g_a_re, "ssm_a_im": g_a_im,
        "ssm_log_dt": g_ld.reshape(g), "ssm_b_re": g_b_re2.reshape(g, p, h), "ssm_b_im": g_b_im2.reshape(g, p, h),
        "ssm_c_re": g_c_re, "ssm_c_im": g_c_im, "ssm_d": gd[0], "b_glu": gsmall[4], "norm_post_g": gg_post[0],
        "loss": loss_blk[0, 0:1],
    }
    zeros_cw = jnp.zeros((conv_w.shape[0], dc), F32)
    one0 = jnp.zeros((1,), F32)
    small_w = dict(norm_pre_g=norm_pre_g, conv_w=zeros_cw, conv_b=conv_b, ssm_a_re=ssm_a_re, ssm_a_im=ssm_a_im,
                   ssm_log_dt=ssm_log_dt, ssm_b_re=ssm_b_re, ssm_b_im=ssm_b_im, ssm_c_re=ssm_c_re, ssm_c_im=ssm_c_im,
                   ssm_d=ssm_d, b_glu=b_glu, norm_post_g=norm_post_g, loss=one0)
    small_m = dict(norm_pre_g=m_norm_pre_g, conv_w=zeros_cw, conv_b=m_conv_b, ssm_a_re=m_ssm_a_re, ssm_a_im=m_ssm_a_im,
                   ssm_log_dt=m_ssm_log_dt, ssm_b_re=m_ssm_b_re, ssm_b_im=m_ssm_b_im, ssm_c_re=m_ssm_c_re,
                   ssm_c_im=m_ssm_c_im, ssm_d=m_ssm_d, b_glu=m_b_glu, norm_post_g=m_norm_post_g, loss=one0)
    small_v = dict(norm_pre_g=v_norm_pre_g, conv_w=zeros_cw, conv_b=v_conv_b, ssm_a_re=v_ssm_a_re, ssm_a_im=v_ssm_a_im,
                   ssm_log_dt=v_ssm_log_dt, ssm_b_re=v_ssm_b_re, ssm_b_im=v_ssm_b_im, ssm_c_re=v_ssm_c_re,
                   ssm_c_im=v_ssm_c_im, ssm_d=v_ssm_d, b_glu=v_b_glu, norm_post_g=v_norm_post_g, loss=one0)
    shapes = [small_w[nm].shape for nm in small_names]
    g_pack = _allreduce_small(_pack([small_g[nm] for nm in small_names]))
    packs = _adamw_small(g_pack, _pack([small_w[nm] for nm in small_names]),
                         _pack([small_m[nm] for nm in small_names]), _pack([small_v[nm] for nm in small_names]))
    sg, sd, sm, sv = [dict(zip(small_names, _unpack(pk, shapes))) for pk in (g_pack, *packs)]
    loss = sg["loss"][0]

    g_cw = lax.dynamic_slice_in_dim(sg["conv_w"], chip * cw_cols, cw_cols, axis=1)
    pad_cw = lambda a: jnp.zeros((SUBLANES, cw_pad), F32).at[:a.shape[0], :cw_cols].set(a)
    cut_cw = lambda a: a[:conv_w.shape[0], :cw_cols]
    d_cw, m_cw, v_cw = [cut_cw(a) for a in _adamw(pad_cw(conv_w), pad_cw(g_cw), pad_cw(m_conv_w), pad_cw(v_conv_w),
                                                  "adamw_conv_w")]

    big =[gw_in.reshape(N_CHIPS, 2, d // 2, n_shard),
           gw_out.reshape(N_CHIPS, 2, (dc + dssm) // (2 * N_CHIPS), d),
           gw_glu.reshape(N_CHIPS, 2, dssm // (2 * N_CHIPS), dssm)]
    tags =["w_in", "w_out", "w_glu"]
    got = _rs_pair_exchange(big)
    sums = [_rs_pair_add(place, gt, rt, "rs_pair_add_" + tg) for gt, rt, tg in zip(big, got, tags)]
    recvd = _rs_chip_exchange([s16 for s16, _ in sums])
    mine = [_rs_chip_add(place, s32, qt, "rs_chip_add_" + tg) for (_, s32), qt, tg in zip(sums, recvd, tags)]
    full = _rs_pair_share(mine)
    g_win, g_wout, g_wglu = [f.reshape(2 * f.shape[1], f.shape[2]) for f in full]
    d_win, m_win, v_win = _adamw(w_in, g_win, m_w_in, v_w_in, "adamw_w_in")
    d_wout, m_wout, v_wout = _adamw(w_out, g_wout, m_w_out, v_w_out, "adamw_w_out")
    d_wglu, m_wglu, v_wglu = _adamw(w_glu, g_wglu, m_w_glu, v_w_glu, "adamw_w_glu")

    order = ["norm_pre_g", "w_in", "conv_w", "conv_b", "ssm_a_re", "ssm_a_im", "ssm_log_dt", "ssm_b_re", "ssm_b_im",
             "ssm_c_re", "ssm_c_im", "ssm_d", "w_glu", "b_glu", "w_out", "norm_post_g"]
    grads, deltas, new_m, new_v = dict(sg), dict(sd), dict(sm), dict(sv)
    grads.update(w_in=g_win, w_out=g_wout, w_glu=g_wglu, conv_w=g_cw)
    deltas.update(w_in=d_win, w_out=d_wout, w_glu=d_wglu, conv_w=d_cw)
    new_m.update(w_in=m_win, w_out=m_wout, w_glu=m_wglu, conv_w=m_cw)
    new_v.update(w_in=v_win, w_out=v_wout, w_glu=v_wglu, conv_w=v_cw)
    return (loss, gx[None], *[grads[nm] for nm in order], *[deltas[nm] for nm in order],
            *[new_m[nm] for nm in order], *[new_v[nm] for nm in order])
```

```python
import functools
import math

import jax
import jax.numpy as jnp
from jax import lax
from jax.experimental import pallas as pl
from jax.experimental.pallas import tpu as pltpu

F32 = jnp.float32
BF16 = jnp.bfloat16
MESH = pl.DeviceIdType.MESH

EPS = 1e-6
SSM_H = 16
SSM_P = 64
GROUPS_PER_SLAB = 8
SLAB = 128
SLAB_STATES = GROUPS_PER_SLAB * SSM_P
N_CHIPS = 4
N_DEV = 8

ADAM_LR = 0.001
ADAM_B1 = 0.9
ADAM_B2 = 0.999
ADAM_EPS = 1e-08
ADAM_WD = 0.01
ADAM_STEP = 10

MIB = 1024 * 1024
VMEM_CAP = 48 * MIB
SUBLANES = 8

TM_NORM = 512
TM_PROJ = 512
TM_GATE = 256
TM_OUT = 256
TM_DH = 256
T_SCAN = 256
TK_TN = 512
SCAN_UNROLL = 4
TM_TN = 1024
TR_ELT = 256


def _cp(vmem_bytes, **kw):
    return pltpu.CompilerParams(vmem_limit_bytes=int(min(VMEM_CAP, max(16 * MIB, vmem_bytes))), **kw)


def _my_place():
    return lax.axis_index("x"), lax.axis_index("y"), lax.axis_index("c")


def _other_chips(x, y):
    return [(1 - x, y), (x, 1 - y), (1 - x, 1 - y)]


def _silu(z):
    s = jax.nn.sigmoid(z)
    return z * s, s


def _dsilu(z, s):
    return s * (1.0 + z * (1.0 - s))


_GELU_K = math.sqrt(2.0 / math.pi)
_GELU_C = 0.044715


def _gelu(y):
    th = jnp.tanh(_GELU_K * (y + _GELU_C * y * y * y))
    return 0.5 * y * (1.0 + th), th


def _dgelu(y, th):
    return 0.5 * (1.0 + th) + 0.5 * y * (1.0 - th * th) * _GELU_K * (1.0 + 3.0 * _GELU_C * y * y)


def _cast_bf16(w, name):
    r, c = w.shape
    tr = min(TR_ELT, r)

    def body(w_ref, o_ref):
        o_ref[...] = w_ref[...].astype(BF16)

    return pl.pallas_call(
        body, name=name, grid=(r // tr,),
        in_specs=[pl.BlockSpec((tr, c), lambda i: (i, 0))],
        out_specs=pl.BlockSpec((tr, c), lambda i: (i, 0)),
        out_shape=jax.ShapeDtypeStruct((r, c), BF16),
        compiler_params=_cp(12 * tr * c),
    )(w)


def _prenorm(x, g):
    t, d = x.shape
    tm = min(TM_NORM, t)

    def body(x_ref, g_ref, h_ref):
        xv = x_ref[...]
        r = lax.rsqrt(jnp.mean(xv * xv, axis=-1, keepdims=True) + EPS)
        h_ref[...] = (xv * r * g_ref[...]).astype(BF16)

    return pl.pallas_call(
        body, name="prenorm", grid=(t // tm,),
        in_specs=[pl.BlockSpec((tm, d), lambda i: (i, 0)), pl.BlockSpec((1, d), lambda i: (0, 0))],
        out_specs=pl.BlockSpec((tm, d), lambda i: (i, 0)),
        out_shape=jax.ShapeDtypeStruct((t, d), BF16),
        compiler_params=_cp(20 * tm * d),
    )(x, g)


def _matmul_nn(a, b, name):
    t, k = a.shape
    nb, _, n = b.shape
    tm = min(TM_PROJ, t)

    def body(a_ref, b_ref, o_ref):
        o_ref[...] = jnp.dot(a_ref[...], b_ref[...], preferred_element_type=F32)

    return pl.pallas_call(
        body, name=name, grid=(nb, t // tm),
        in_specs=[pl.BlockSpec((tm, k), lambda j, i: (i, 0)), pl.BlockSpec((None, k, n), lambda j, i: (j, 0, 0))],
        out_specs=pl.BlockSpec((tm, n), lambda j, i: (i, j)),
        out_shape=jax.ShapeDtypeStruct((t, nb * n), F32),
        compiler_params=_cp(2 * (2 * tm * k + 2 * k * n + 4 * tm * n) + 4 * MIB),
    )(a, b)


def _matmul_tn(a, b, nb, name):
    t, m = a.shape
    n = b.shape[1] // nb
    tk = min(TK_TN, t)
    tma = min(TM_TN, m)

    def body(a_ref, b_ref, o_ref):
        @pl.when(pl.program_id(2) == 0)
        def _():
            o_ref[...] = jnp.zeros_like(o_ref)

        o_ref[...] += lax.dot_general(a_ref[...], b_ref[...], (((0,), (0,)), ((), ())), preferred_element_type=F32)

    return pl.pallas_call(
        body, name=name, grid=(nb, m // tma, t // tk),
        in_specs=[pl.BlockSpec((tk, tma), lambda j, i, k: (k, i)), pl.BlockSpec((tk, n), lambda j, i, k: (k, j))],
        out_specs=pl.BlockSpec((None, tma, n), lambda j, i, k: (j, i, 0)),
        out_shape=jax.ShapeDtypeStruct((nb, m, n), F32),
        compiler_params=_cp(2 * (2 * tk * tma + 2 * tk * n + 4 * tma * n) + 8 * MIB),
    )(a, b)


def _outproj(mix, w_out, x, tgt, g_post):
    t, dm = mix.shape
    d = w_out.shape[1]
    tm = min(TM_OUT, t)

    def body(mix_ref, w_ref, x_ref, t_ref, g_ref, loss_ref, dout_ref, do_ref, dmix_ref, gg_ref):
        @pl.when(pl.program_id(0) == 0)
        def _():
            loss_ref[...] = jnp.zeros_like(loss_ref)
            gg_ref[...] = jnp.zeros_like(gg_ref)

        w = w_ref[...]
        o = jnp.dot(mix_ref[...], w, preferred_element_type=F32)
        r = lax.rsqrt(jnp.mean(o * o, axis=-1, keepdims=True) + EPS)
        nh = o * r
        g = g_ref[...]
        e = x_ref[...] + nh * g - t_ref[...]
        loss_ref[...] += jnp.sum(e * e) * (0.5 / d)
        dout = e * (1.0 / d)
        dout_ref[...] = dout
        gg_ref[0:1, :] += jnp.sum(dout * nh, axis=0, keepdims=True)
        dn = dout * g
        do = r * (dn - nh * jnp.mean(dn * nh, axis=-1, keepdims=True))
        dob = do.astype(BF16)
        do_ref[...] = dob
        dmix_ref[...] = lax.dot_general(dob, w, (((1,), (1,)), ((), ())), preferred_element_type=F32)

    row = lambda i: (i, 0)
    fixed = lambda i: (0, 0)
    return pl.pallas_call(
        body, name="outproj", grid=(t // tm,),
        in_specs=[pl.BlockSpec((tm, dm), row), pl.BlockSpec((dm, d), fixed), pl.BlockSpec((tm, d), row),
                  pl.BlockSpec((tm, d), row), pl.BlockSpec((1, d), fixed)],
        out_specs=[pl.BlockSpec((SUBLANES, SLAB), fixed), pl.BlockSpec((tm, d), row), pl.BlockSpec((tm, d), row),
                   pl.BlockSpec((tm, dm), row), pl.BlockSpec((SUBLANES, d), fixed)],
        out_shape=[jax.ShapeDtypeStruct((SUBLANES, SLAB), F32), jax.ShapeDtypeStruct((t, d), F32),
                   jax.ShapeDtypeStruct((t, d), BF16), jax.ShapeDtypeStruct((t, dm), F32),
                   jax.ShapeDtypeStruct((SUBLANES, d), F32)],
        compiler_params=_cp(2 * (2 * dm * d + tm * (2 * dm + 4 * d * 3 + 2 * d + 4 * dm)) + 16 * MIB),
    )(mix, w_out, x, tgt, g_post)


def _dh_prenorm_bwd(dproj, w_in, x, dout, g_pre):
    t, d = x.shape
    nb, _, n = w_in.shape
    tm = min(TM_DH, t)

    def body(dp_ref, w_ref, x_ref, dout_ref, g_ref, dx_ref, gg_ref):
        @pl.when(pl.program_id(0) == 0)
        def _():
            gg_ref[...] = jnp.zeros_like(gg_ref)

        dh = None
        for k in range(nb):
            part = lax.dot_general(dp_ref[:, k * n:(k + 1) * n], w_ref[k], (((1,), (1,)), ((), ())),
                                   preferred_element_type=F32)
            dh = part if dh is None else dh + part
        xv = x_ref[...]
        r = lax.rsqrt(jnp.mean(xv * xv, axis=-1, keepdims=True) + EPS)
        xh = xv * r
        gg_ref[0:1, :] += jnp.sum(dh * xh, axis=0, keepdims=True)
        dg = dh * g_ref[...]
        dx_ref[...] = dout_ref[...] + r * (dg - xh * jnp.mean(dg * xh, axis=-1, keepdims=True))

    row = lambda i: (i, 0)
    fixed = lambda i: (0, 0)
    w_spec = pl.BlockSpec(w_in.shape, lambda i: (0, 0, 0), pipeline_mode=pl.Buffered(1))
    return pl.pallas_call(
        body, name="dh_prenorm_bwd", grid=(t // tm,),
        in_specs=[pl.BlockSpec((tm, nb * n), row), w_spec,
                  pl.BlockSpec((tm, d), row), pl.BlockSpec((tm, d), row), pl.BlockSpec((1, d), fixed)],
        out_specs=[pl.BlockSpec((tm, d), row), pl.BlockSpec((SUBLANES, d), fixed)],
        out_shape=[jax.ShapeDtypeStruct((t, d), F32), jax.ShapeDtypeStruct((SUBLANES, d), F32)],
        compiler_params=_cp(2 * nb * d * n + 2 * (2 * tm * nb * n + 12 * tm * d) + 16 * tm * d + 4 * MIB),
    )(dproj, w_in, x, dout, g_pre)


def _adamw(w, g, m, v, name):
    r, c = w.shape
    tr = min(TR_ELT, r)
    c1 = 1.0 - ADAM_B1 ** ADAM_STEP
    c2 = 1.0 - ADAM_B2 ** ADAM_STEP

    def body(w_ref, g_ref, m_ref, v_ref, d_ref, mo_ref, vo_ref):
        gv = g_ref[...]
        mn = ADAM_B1 * m_ref[...] + (1.0 - ADAM_B1) * gv
        vn = ADAM_B2 * v_ref[...] + (1.0 - ADAM_B2) * (gv * gv)
        d_ref[...] = -ADAM_LR * ((mn / c1) / (jnp.sqrt(vn / c2) + ADAM_EPS) + ADAM_WD * w_ref[...])
        mo_ref[...] = mn
        vo_ref[...] = vn

    spec = pl.BlockSpec((tr, c), lambda i: (i, 0))
    sds = jax.ShapeDtypeStruct((r, c), F32)
    return pl.pallas_call(
        body, name=name, grid=(r // tr,), in_specs=[spec] * 4, out_specs=[spec] * 3, out_shape=[sds] * 3,
        compiler_params=_cp(2 * 7 * 4 * tr * c + 8 * MIB),
    )(w, g, m, v)


def _adamw_small(g, w, m, v):
    r, c = w.shape
    c1 = 1.0 - ADAM_B1 ** ADAM_STEP
    c2 = 1.0 - ADAM_B2 ** ADAM_STEP

    def body(g_ref, w_ref, m_ref, v_ref, d_ref, mo_ref, vo_ref):
        gv = g_ref[...]
        mn = ADAM_B1 * m_ref[...] + (1.0 - ADAM_B1) * gv
        vn = ADAM_B2 * v_ref[...] + (1.0 - ADAM_B2) * (gv * gv)
        d_ref[...] = -ADAM_LR * ((mn / c1) / (jnp.sqrt(vn / c2) + ADAM_EPS) + ADAM_WD * w_ref[...])
        mo_ref[...] = mn
        vo_ref[...] = vn

    sds = jax.ShapeDtypeStruct((r, c), F32)
    return pl.pallas_call(
        body, name="adamw_small", out_shape=[sds] * 3,
        compiler_params=_cp(12 * 4 * r * c + 8 * MIB),
    )(g, w, m, v)


def _allreduce_small(v):
    r, c = v.shape
    assert r % (2 * SUBLANES) == 0
    h = r // 2

    def body(v_ref, out_ref, got_ref, chip_ref, slots_ref, send, recv):
        x, y, cc = _my_place()
        k = 2 * x + y
        sib = (x, y, 1 - cc)

        def rc(s, src, dst, to):
            return pltpu.make_async_remote_copy(src_ref=src, dst_ref=dst, send_sem=send.at[s], recv_sem=recv.at[s],
                                                device_id=to, device_id_type=MESH)

        pair = rc(0, v_ref, got_ref, sib)
        pair.start()
        pair.wait()
        chip_ref[...] = v_ref[...] + got_ref[...]
        mine = pl.ds(pl.multiple_of(cc * h, SUBLANES), h)
        theirs = pl.ds(pl.multiple_of((1 - cc) * h, SUBLANES), h)
        chips = _other_chips(x, y)
        sent = []
        for j, (cx, cy) in enumerate(chips):
            cp = rc(1 + j, chip_ref.at[mine], slots_ref.at[k], (cx, cy, cc))
            cp.start()
            sent.append(cp)
        slots_ref[k] = chip_ref[mine, :]
        for j, (cx, cy) in enumerate(chips):
            rc(1 + j, chip_ref.at[mine], slots_ref.at[2 * cx + cy], (cx, cy, cc)).wait_recv()
        total = slots_ref[0]
        for kk in range(1, N_CHIPS):
            total = total + slots_ref[kk]
        out_ref[mine, :] = total
        for cp in sent:
            cp.wait_send()
        share = rc(N_CHIPS, out_ref.at[mine], out_ref.at[mine], sib)
        share.start()
        rc(N_CHIPS, out_ref.at[theirs], out_ref.at[theirs], sib).wait_recv()
        share.wait_send()

    vm = pl.BlockSpec(memory_space=pltpu.VMEM)
    return pl.pallas_call(
        body, name="allreduce_small", out_shape=jax.ShapeDtypeStruct((r, c), F32), in_specs=[vm], out_specs=vm,
        scratch_shapes=[pltpu.VMEM((r, c), F32), pltpu.VMEM((r, c), F32), pltpu.VMEM((N_CHIPS, h, c), F32),
                        pltpu.SemaphoreType.DMA((N_CHIPS + 1,)), pltpu.SemaphoreType.DMA((N_CHIPS + 1,))],
        compiler_params=_cp(6 * 4 * r * c + 8 * MIB),
    )(v)


def _zoh(a_re, a_im, log_dt, b_re2, b_im2, expand):
    dt = jnp.exp(log_dt)
    mag = jnp.exp(a_re * dt)
    lbr, lbi = mag * jnp.cos(a_im * dt), mag * jnp.sin(a_im * dt)
    nr, ni = lbr - 1.0, lbi
    den = a_re * a_re + a_im * a_im
    qr = (nr * a_re + ni * a_im) / den
    qi = (ni * a_re - nr * a_im) / den
    qr2 = jnp.dot(qr, expand, precision=lax.Precision.HIGHEST, preferred_element_type=F32)
    qi2 = jnp.dot(qi, expand, precision=lax.Precision.HIGHEST, preferred_element_type=F32)
    return lbr, lbi, qr2 * b_re2 - qi2 * b_im2, qr2 * b_im2 + qi2 * b_re2


def _zoh_fwd(a_re, a_im, log_dt, b_re2, b_im2, expand, power):
    g, p = a_re.shape

    def body(ar_ref, ai_ref, ld_ref, br_ref, bi_ref, e_ref, lbr_ref, lbi_ref, bbr_ref, bbi_ref, pw_ref):
        ar, ai, ld = ar_ref[...], ai_ref[...], ld_ref[...]
        lbr, lbi, bbr, bbi = _zoh(ar, ai, ld, br_ref[...], bi_ref[...], e_ref[...])
        lbr_ref[...], lbi_ref[...], bbr_ref[...], bbi_ref[...] = lbr, lbi, bbr, bbi
        dt = jnp.exp(ld) * float(power)
        mag = jnp.exp(ar * dt)
        pw_ref[0] = mag * jnp.cos(ai * dt)
        pw_ref[1] = mag * jnp.sin(ai * dt)

    gp = jax.ShapeDtypeStruct((g, p), F32)
    gph = jax.ShapeDtypeStruct(b_re2.shape, F32)
    return pl.pallas_call(
        body, name="zoh_fwd", out_shape=[gp, gp, gph, gph, jax.ShapeDtypeStruct((2, g, p), F32)],
        compiler_params=_cp(16 * MIB),
    )(a_re, a_im, log_dt, b_re2, b_im2, expand)


def _zoh_bwd(a_re, a_im, log_dt, b_re2, b_im2, expand, g_lbr, g_lbi, g_bbr, g_bbi):
    def body(ar_ref, ai_ref, ld_ref, br_ref, bi_ref, e_ref, c0, c1, c2, c3, gar, gai, gld, gbr, gbi):
        e = e_ref[...]
        _, vjp = jax.vjp(lambda a, b, c, d, f: _zoh(a, b, c, d, f, e),
                         ar_ref[...], ai_ref[...], ld_ref[...], br_ref[...], bi_ref[...])
        gar[...], gai[...], gld[...], gbr[...], gbi[...] = vjp((c0[...], c1[...], c2[...], c3[...]))

    sds = lambda a: jax.ShapeDtypeStruct(a.shape, F32)
    return pl.pallas_call(
        body, name="zoh_bwd", out_shape=[sds(a_re), sds(a_im), sds(log_dt), sds(b_re2), sds(b_im2)],
        compiler_params=_cp(16 * MIB),
    )(a_re, a_im, log_dt, b_re2, b_im2, expand, g_lbr, g_lbi, g_bbr, g_bbi)


def _blockdiag(blocks):
    nq, _, r, c = blocks.shape
    eye = jnp.eye(GROUPS_PER_SLAB, dtype=blocks.dtype)
    out = blocks[:, :, :, None, :] * eye[None, :, None, :, None]
    return out.reshape(nq, GROUPS_PER_SLAB * r, GROUPS_PER_SLAB * c)


def _blockdiag_take(dense, r, c):
    nq = dense.shape[0]
    eye = jnp.eye(GROUPS_PER_SLAB, dtype=dense.dtype)
    d6 = dense.reshape(nq, GROUPS_PER_SLAB, r, GROUPS_PER_SLAB, c)
    return jnp.sum(d6 * eye[None, :, None, :, None], axis=3)


def _scan_slab(s_ref, row0, q, lam_ref, pw_ref, car_ref, steps, reverse, prev_ref=None, prev_row0=0, glam_ref=None):
    sign = -1.0 if reverse else 1.0
    half = SLAB_STATES // SLAB
    cols = [(q * 2 * half + m, q * 2 * half + half + m, q * SLAB_STATES + m * SLAB) for m in range(half)]
    nm = len(cols)
    full = (SUBLANES, SLAB)
    lam = [(jnp.broadcast_to(lam_ref[0:1, pl.ds(cl, SLAB)], full),
            jnp.broadcast_to(sign * lam_ref[1:2, pl.ds(cl, SLAB)], full)) for (_, _, cl) in cols]

    def step_rows(jj, base):
        j = (steps - 1 - jj) if reverse else jj
        return j, pl.ds(pl.multiple_of(base + j * SUBLANES, SUBLANES), SUBLANES)

    def pass1(jj, car):
        _, rows = step_rows(jj, row0)
        out = []
        for m, (cr, ci, _) in enumerate(cols):
            sr, si = car[2 * m], car[2 * m + 1]
            lr, li = lam[m]
            nr = lr * sr - li * si + s_ref[cr, rows, :]
            ni = lr * si + li * sr + s_ref[ci, rows, :]
            s_ref[cr, rows, :] = nr
            s_ref[ci, rows, :] = ni
            out += [nr, ni]
        return tuple(out)

    def unrolled(step_fn):
        def outer(jo, carry):
            for k in range(SCAN_UNROLL):
                carry = step_fn(jo * SCAN_UNROLL + k, carry)
            return carry
        return outer

    assert steps % SCAN_UNROLL == 0
    ends = lax.fori_loop(0, steps // SCAN_UNROLL, unrolled(pass1), tuple(jnp.zeros(full, F32) for _ in range(2 * nm)))

    entry = []
    for m, (cr, ci, cl) in enumerate(cols):
        ljr = pw_ref[0:1, pl.ds(cl, SLAB)]
        lji = sign * pw_ref[1:2, pl.ds(cl, SLAB)]
        c_r = car_ref[cr, 0:1, :]
        c_i = car_ref[ci, 0:1, :]
        rows_r, rows_i = [None] * SUBLANES, [None] * SUBLANES
        order = range(SUBLANES - 1, -1, -1) if reverse else range(SUBLANES)
        for b in order:
            rows_r[b], rows_i[b] = c_r, c_i
            e_r, e_i = ends[2 * m][b:b + 1], ends[2 * m + 1][b:b + 1]
            c_r, c_i = ljr * c_r - lji * c_i + e_r, ljr * c_i + lji * c_r + e_i
        car_ref[cr, 0:1, :] = c_r
        car_ref[ci, 0:1, :] = c_i
        entry.append((jnp.concatenate(rows_r, axis=0), jnp.concatenate(rows_i, axis=0)))

    def pass2(jj, carry):
        j, rows = step_rows(jj, row0)
        decayed, acc = carry[:2 * nm], carry[2 * nm:]
        out_d, out_a = [], []
        for m, (cr, ci, cl) in enumerate(cols):
            lr, li = lam[m]
            dr, di = decayed[2 * m], decayed[2 * m + 1]
            dr, di = lr * dr - li * di, lr * di + li * dr
            nr = s_ref[cr, rows, :] + dr
            ni = s_ref[ci, rows, :] + di
            s_ref[cr, rows, :] = nr
            s_ref[ci, rows, :] = ni
            out_d += [dr, di]
            if prev_ref is not None:
                prow = pl.ds(pl.multiple_of(prev_row0 + (j - 1) * SUBLANES, SUBLANES), SUBLANES)
                qr = prev_ref[cr, prow, :]
                qi = prev_ref[ci, prow, :]
                out_a += [acc[2 * m] + (nr * qr + ni * qi), acc[2 * m + 1] + (ni * qr - nr * qi)]
        return tuple(out_d) + tuple(out_a)

    n_acc = 2 * nm if prev_ref is not None else 0
    init = tuple(e for pair in entry for e in pair) + tuple(jnp.zeros(full, F32) for _ in range(n_acc))
    accs = lax.fori_loop(0, steps // SCAN_UNROLL, unrolled(pass2), init)[2 * nm:]
    if prev_ref is not None:
        for m, (_, _, cl) in enumerate(cols):
            glam_ref[0:1, pl.ds(cl, SLAB)] += jnp.sum(accs[2 * m], axis=0, keepdims=True)
            glam_ref[1:2, pl.ds(cl, SLAB)] += jnp.sum(accs[2 * m + 1], axis=0, keepdims=True)


def _permute_rows_f32(perm_bf16, v):
    hi = v.astype(BF16)
    lo = (v - hi.astype(F32)).astype(BF16)
    return (jnp.dot(perm_bf16, hi, preferred_element_type=F32) + jnp.dot(perm_bf16, lo, preferred_element_type=F32))


def _put_slab(s_ref, rows, q, val):
    per = 2 * SLAB_STATES // SLAB
    for i in range(per):
        s_ref[q * per + i, rows, :] = val[:, i * SLAB:(i + 1) * SLAB]


def _get_slab(s_ref, rows, q):
    per = 2 * SLAB_STATES // SLAB
    return jnp.concatenate([s_ref[q * per + i, rows, :] for i in range(per)], axis=1)


def _step_major_perm(tt):
    r = jnp.arange(tt)
    held = (r % SUBLANES) * (tt // SUBLANES) + r // SUBLANES
    return held[:, None] == r[None, :]


def _ssm_fwd(proj, u_block, bq, cq, lam, pw, d_skip):
    t = proj.shape[0]
    nq, ds, w2 = bq.shape
    assert ds == SLAB and w2 == 2 * SLAB_STATES
    dssm = nq * SLAB
    width = nq * w2
    ntile = width // SLAB
    tt = min(T_SCAN, t)
    steps = tt // SUBLANES
    perm = _step_major_perm(tt)
    pm, pmt = perm.astype(BF16), perm.T.astype(BF16)

    def body(u_ref, pm_ref, pmt_ref, bq_ref, cq_ref, lam_ref, pw_ref, d_ref, y_ref, cin_ref, s_ref, car_ref):
        @pl.when(pl.program_id(0) == 0)
        def _():
            car_ref[...] = jnp.zeros_like(car_ref)

        cin_ref[...] = jnp.broadcast_to(car_ref[:, 0:1, :], cin_ref.shape)
        u = u_ref[...]
        ub = jnp.dot(pm_ref[...], u.astype(BF16), preferred_element_type=F32).astype(BF16)
        everything = slice(None)
        for q in range(nq):
            _put_slab(s_ref, everything, q,
                      jnp.dot(ub[:, q * SLAB:(q + 1) * SLAB], bq_ref[q], preferred_element_type=F32))
        for q in range(nq):
            _scan_slab(s_ref, 0, q, lam_ref, pw_ref, car_ref, steps, reverse=False)
        y_sm = jnp.concatenate(
            [jnp.dot(_get_slab(s_ref, everything, q).astype(BF16), cq_ref[q], preferred_element_type=F32)
             for q in range(nq)], axis=1)
        y_ref[...] = _permute_rows_f32(pmt_ref[...], y_sm) + d_ref[...] * u

    c3 = lambda i: (0, 0, 0)
    c2 = lambda i: (0, 0)
    return pl.pallas_call(
        body, name="ssm_fwd", grid=(t // tt,),
        in_specs=[pl.BlockSpec((tt, dssm), lambda i: (i, u_block)), pl.BlockSpec((tt, tt), c2),
                  pl.BlockSpec((tt, tt), c2), pl.BlockSpec(bq.shape, c3),
                  pl.BlockSpec(cq.shape, c3), pl.BlockSpec(lam.shape, c2), pl.BlockSpec(pw.shape, c2),
                  pl.BlockSpec((1, dssm), c2)],
        out_specs=[pl.BlockSpec((tt, dssm), lambda i: (i, 0)),
                   pl.BlockSpec((None, ntile, SUBLANES, SLAB), lambda i: (i, 0, 0, 0))],
        out_shape=[jax.ShapeDtypeStruct((t, dssm), F32), jax.ShapeDtypeStruct((t // tt, ntile, SUBLANES, SLAB), F32)],
        scratch_shapes=[pltpu.VMEM((ntile, tt, SLAB), F32), pltpu.VMEM((ntile, SUBLANES, SLAB), F32)],
        compiler_params=_cp(2 * (8 * tt * dssm + 4 * nq * ds * w2) + 4 * tt * width + 16 * MIB,
                            dimension_semantics=("arbitrary",)),
    )(proj, pm, pmt, bq, cq, lam, pw, d_skip)


def _ssm_bwd(proj, u_block, dy, cin, dproj, bq, cq, lam, pw, d_skip):
    t = proj.shape[0]
    nq, ds, w2 = bq.shape
    dssm = nq * SLAB
    width = nq * w2
    ntile = width // SLAB
    tt = min(T_SCAN, t)
    nt = t // tt
    steps = tt // SUBLANES
    halo = SUBLANES
    perm = _step_major_perm(tt)
    pm, pmt = perm.astype(BF16), perm.T.astype(BF16)

    def body(u_ref, dy_ref, cin_ref, dp_any, pm_ref, pmt_ref, bq_ref, cq_ref, lam_ref, pw_ref, d_ref,
             du_ref, gb_ref, gc_ref, glam_ref, gd_ref, s_ref, gs_ref, car_f, car_b):
        del dp_any

        @pl.when(pl.program_id(0) == 0)
        def _():
            car_b[...] = jnp.zeros_like(car_b)
            gb_ref[...] = jnp.zeros_like(gb_ref)
            gc_ref[...] = jnp.zeros_like(gc_ref)
            glam_ref[...] = jnp.zeros_like(glam_ref)
            gd_ref[...] = jnp.zeros_like(gd_ref)

        u = u_ref[...]
        dyv = dy_ref[...]
        gd_ref[0:1, :] += jnp.sum(dyv * u, axis=0, keepdims=True)
        pmv = pm_ref[...]
        ub = jnp.dot(pmv, u.astype(BF16), preferred_element_type=F32).astype(BF16)
        dyb = jnp.dot(pmv, dyv.astype(BF16), preferred_element_type=F32).astype(BF16)
        car_f[...] = cin_ref[...]
        data = slice(halo, halo + tt)
        everything = slice(None)
        for q in range(nq):
            _put_slab(s_ref, data, q, jnp.dot(ub[:, q * SLAB:(q + 1) * SLAB], bq_ref[q], preferred_element_type=F32))
        for q in range(nq):
            _scan_slab(s_ref, halo, q, lam_ref, pw_ref, car_f, steps, reverse=False)
        last_step = s_ref[:, halo + tt - SUBLANES:halo + tt, :]
        s_ref[:, 0:halo, :] = jnp.concatenate([cin_ref[:, 0:1, :], last_step[:, 0:SUBLANES - 1, :]], axis=1)
        tn = (((0,), (0,)), ((), ()))
        nt_dims = (((1,), (1,)), ((), ()))
        for q in range(nq):
            sl = slice(q * SLAB, (q + 1) * SLAB)
            gc_ref[q] += lax.dot_general(dyb[:, sl], _get_slab(s_ref, data, q).astype(BF16), tn,
                                         preferred_element_type=F32)
            _put_slab(gs_ref, everything, q,
                      lax.dot_general(dyb[:, sl], cq_ref[q], nt_dims, preferred_element_type=F32))
        for q in range(nq):
            _scan_slab(gs_ref, 0, q, lam_ref, pw_ref, car_b, steps, reverse=True,
                       prev_ref=s_ref, prev_row0=halo, glam_ref=glam_ref)
        du_parts = []
        for q in range(nq):
            sl = slice(q * SLAB, (q + 1) * SLAB)
            gsb = _get_slab(gs_ref, everything, q).astype(BF16)
            du_parts.append(lax.dot_general(gsb, bq_ref[q], nt_dims, preferred_element_type=F32))
            gb_ref[q] += lax.dot_general(ub[:, sl], gsb, tn, preferred_element_type=F32)
        du_sm = jnp.concatenate(du_parts, axis=1)
        du_ref[...] = (_permute_rows_f32(pmt_ref[...], du_sm) + dyv * d_ref[...]).astype(BF16)

    c3 = lambda i: (0, 0, 0)
    c2 = lambda i: (0, 0)
    rev = lambda i: (nt - 1 - i, 0)
    dense = jax.ShapeDtypeStruct((nq, SLAB, w2), F32)
    gp = lam.shape[1]
    return pl.pallas_call(
        body, name="ssm_bwd", grid=(nt,),
        in_specs=[pl.BlockSpec((tt, dssm), lambda i: (nt - 1 - i, u_block)), pl.BlockSpec((tt, dssm), rev),
                  pl.BlockSpec((None, ntile, SUBLANES, SLAB), lambda i: (nt - 1 - i, 0, 0, 0)),
                  pl.BlockSpec(memory_space=pl.ANY), pl.BlockSpec((tt, tt), c2), pl.BlockSpec((tt, tt), c2),
                  pl.BlockSpec(bq.shape, c3), pl.BlockSpec(cq.shape, c3),
                  pl.BlockSpec(lam.shape, c2), pl.BlockSpec(pw.shape, c2), pl.BlockSpec((1, dssm), c2)],
        out_specs=[pl.BlockSpec((tt, dssm), lambda i: (nt - 1 - i, u_block)), pl.BlockSpec(dense.shape, c3),
                   pl.BlockSpec(dense.shape, c3), pl.BlockSpec((SUBLANES, gp), c2), pl.BlockSpec((SUBLANES, dssm), c2)],
        out_shape=[jax.ShapeDtypeStruct(dproj.shape, dproj.dtype), dense, dense,
                   jax.ShapeDtypeStruct((SUBLANES, gp), F32), jax.ShapeDtypeStruct((SUBLANES, dssm), F32)],
        scratch_shapes=[pltpu.VMEM((ntile, tt + halo, SLAB), F32), pltpu.VMEM((ntile, tt, SLAB), F32),
                        pltpu.VMEM((ntile, SUBLANES, SLAB), F32), pltpu.VMEM((ntile, SUBLANES, SLAB), F32)],
        input_output_aliases={3: 0},
        compiler_params=_cp(2 * (10 * tt * dssm + 4 * nq * ds * w2 + 8 * nq * SLAB * w2)
                            + 8 * tt * width + 12 * MIB, dimension_semantics=("arbitrary",)),
    )(proj, dy, cin, dproj, pm, pmt, bq, cq, lam, pw, d_skip)


def _gate_fwd(proj, y, conv_w, conv_b, w_glu, b_glu, dc):
    t = proj.shape[0]
    dssm = y.shape[1]
    assert dc == dssm
    tm = min(TM_GATE, t)
    halo = SUBLANES

    def body(b_ref, c_ref, v_ref, zc_ref, zs_ref, y_ref, cw_ref, cb_ref, wg_ref, bg_ref, mix_ref, cv_buf):
        @pl.when(pl.program_id(0) == 0)
        def _():
            cv_buf[0:halo, :] = jnp.zeros((halo, dc), F32)

        cv = c_ref[...] * v_ref[...]
        cv_buf[halo:, :] = cv
        conv = (cb_ref[...] + cw_ref[2:3, :] * cv + cw_ref[1:2, :] * cv_buf[halo - 1:halo - 1 + tm, :]
                + cw_ref[0:1, :] * cv_buf[halo - 2:halo - 2 + tm, :])
        sz, _ = _silu(zc_ref[...])
        mix_ref[:, 0:dc] = (b_ref[...] * conv * sz).astype(BF16)
        cv_buf[0:halo, :] = cv_buf[tm:tm + halo, :]
        ge, _ = _gelu(y_ref[...])
        gl = jnp.dot(ge.astype(BF16), wg_ref[...], preferred_element_type=F32) + bg_ref[...]
        szs, _ = _silu(zs_ref[...])
        mix_ref[:, dc:] = (ge * jax.nn.sigmoid(gl) * szs).astype(BF16)

    col = lambda j: pl.BlockSpec((tm, dc), lambda i, j=j: (i, j))
    fixed = lambda i: (0, 0)
    return pl.pallas_call(
        body, name="gate_fwd", grid=(t // tm,),
        in_specs=[col(0), col(1), col(2), col(3), col(5), pl.BlockSpec((tm, dssm), lambda i: (i, 0)),
                  pl.BlockSpec(conv_w.shape, fixed), pl.BlockSpec((1, dc), fixed),
                  pl.BlockSpec(w_glu.shape, fixed), pl.BlockSpec((1, dssm), fixed)],
        out_specs=pl.BlockSpec((tm, dc + dssm), lambda i: (i, 0)),
        out_shape=jax.ShapeDtypeStruct((t, dc + dssm), BF16),
        scratch_shapes=[pltpu.VMEM((tm + halo, dc), F32)],
        compiler_params=_cp(2 * (6 * 4 * tm * dc + 2 * tm * (dc + dssm) + 2 * dssm * dssm) + 24 * tm * dc + 8 * MIB,
                            dimension_semantics=("arbitrary",)),
    )(proj, proj, proj, proj, proj, y, conv_w, conv_b, w_glu, b_glu)


def _gate_bwd(proj, y, dmix, conv_w, conv_b, w_glu, b_glu, dc):
    t = proj.shape[0]
    dssm = y.shape[1]
    tm = min(TM_GATE, t)
    nt = t // tm
    halo = SUBLANES
    blocks_per_tile = tm // halo

    def body(b_ref, c_ref, v_ref, zc_ref, zs_ref, cp_ref, vp_ref, y_ref, dm_ref, cw_ref, cb_ref, wg_ref, bg_ref,
             dp_ref, dy_ref, gs_ref, gwg_ref, cv_buf, dc_buf):
        i = pl.program_id(0)

        @pl.when(i == 0)
        def _():
            dc_buf[tm:, :] = jnp.zeros((halo, dc), F32)
            gs_ref[...] = jnp.zeros_like(gs_ref)
            gwg_ref[...] = jnp.zeros_like(gwg_ref)

        first_tile = (i == nt - 1)
        bv, cg, vv, zc = b_ref[...], c_ref[...], v_ref[...], zc_ref[...]
        cv = cg * vv
        cv_buf[0:halo, :] = jnp.where(first_tile, 0.0, cp_ref[...] * vp_ref[...])
        cv_buf[halo:, :] = cv
        w0, w1, w2 = cw_ref[0:1, :], cw_ref[1:2, :], cw_ref[2:3, :]
        conv = (cb_ref[...] + w2 * cv + w1 * cv_buf[halo - 1:halo - 1 + tm, :]
                + w0 * cv_buf[halo - 2:halo - 2 + tm, :])
        sz, sgc = _silu(zc)
        dyc = dm_ref[:, 0:dc]
        dp_ref[:, 0:dc] = (dyc * conv * sz).astype(BF16)
        dp_ref[:, 3 * dc:4 * dc] = (dyc * bv * conv * _dsilu(zc, sgc)).astype(BF16)
        dconv = dyc * bv * sz
        dc_buf[0:tm, :] = dconv
        d1 = dc_buf[1:1 + tm, :]
        d2 = dc_buf[2:2 + tm, :]
        dcv = w2 * dconv + w1 * d1 + w0 * d2
        dp_ref[:, dc:2 * dc] = (dcv * vv).astype(BF16)
        dp_ref[:, 2 * dc:3 * dc] = (dcv * cg).astype(BF16)
        gs_ref[0:1, :] += jnp.sum(cv * d2, axis=0, keepdims=True)
        gs_ref[1:2, :] += jnp.sum(cv * d1, axis=0, keepdims=True)
        gs_ref[2:3, :] += jnp.sum(cv * dconv, axis=0, keepdims=True)
        gs_ref[3:4, :] += jnp.sum(dconv, axis=0, keepdims=True)
        dc_buf[tm:, :] = dc_buf[0:halo, :]

        yv, zs = y_ref[...], zs_ref[...]
        ge, th = _gelu(yv)
        geb = ge.astype(BF16)
        wg = wg_ref[...]
        gl = jnp.dot(geb, wg, preferred_element_type=F32) + bg_ref[...]
        sg = jax.nn.sigmoid(gl)
        szs, sgs = _silu(zs)
        dys = dm_ref[:, dc:]
        ys = ge * sg
        dp_ref[:, 4 * dc:5 * dc] = jnp.zeros((tm, dc), BF16)
        dp_ref[:, 5 * dc:] = (dys * ys * _dsilu(zs, sgs)).astype(BF16)
        d_ys = dys * szs
        dgl = d_ys * ge * sg * (1.0 - sg)
        dglb = dgl.astype(BF16)
        gs_ref[4:5, :] += jnp.sum(dgl, axis=0, keepdims=True)
        gwg_ref[...] += lax.dot_general(geb, dglb, (((0,), (0,)), ((), ())), preferred_element_type=F32)
        dge = d_ys * sg + lax.dot_general(dglb, wg, (((1,), (1,)), ((), ())), preferred_element_type=F32)
        dy_ref[...] = dge * _dgelu(yv, th)

    col = lambda j: pl.BlockSpec((tm, dc), lambda i, j=j: (nt - 1 - i, j))
    prev = lambda j: pl.BlockSpec((halo, dc), lambda i, j=j: (jnp.maximum((nt - 1 - i) * blocks_per_tile - 1, 0), j))
    rev = lambda i: (nt - 1 - i, 0)
    fixed = lambda i: (0, 0)
    return pl.pallas_call(
        body, name="gate_bwd", grid=(nt,),
        in_specs=[col(0), col(1), col(2), col(3), col(5), prev(1), prev(2), pl.BlockSpec((tm, dssm), rev),
                  pl.BlockSpec((tm, dc + dssm), rev), pl.BlockSpec(conv_w.shape, fixed), pl.BlockSpec((1, dc), fixed),
                  pl.BlockSpec(w_glu.shape, fixed), pl.BlockSpec((1, dssm), fixed)],
        out_specs=[pl.BlockSpec((tm, 6 * dc), rev), pl.BlockSpec((tm, dssm), rev),
                   pl.BlockSpec((2 * SUBLANES, dc), fixed), pl.BlockSpec((dssm, dssm), fixed)],
        out_shape=[jax.ShapeDtypeStruct((t, 6 * dc), BF16), jax.ShapeDtypeStruct((t, dssm), F32),
                   jax.ShapeDtypeStruct((2 * SUBLANES, dc), F32), jax.ShapeDtypeStruct((dssm, dssm), F32)],
        scratch_shapes=[pltpu.VMEM((tm + halo, dc), F32), pltpu.VMEM((tm + halo, dc), F32)],
        compiler_params=_cp(2 * (6 * 4 * tm * dc + 8 * tm * dc + 12 * tm * dc + 4 * tm * dc + 6 * dssm * dssm)
                            + 40 * tm * dc + 8 * MIB, dimension_semantics=("arbitrary",)),
    )(proj, proj, proj, proj, proj, proj, proj, y, dmix, conv_w, conv_b, w_glu, b_glu)


def _allgather_halves(parts, name):
    n = len(parts)
    per = 2 * (N_CHIPS - 1)

    def body(*refs):
        ins, outs = refs[:n], refs[n:2 * n]
        send, recv, local = refs[2 * n:]
        x, y, c = _my_place()
        k = 2 * x + y
        sib = (x, y, 1 - c)
        chips = _other_chips(x, y)

        def rc(t, s, src, dst, to):
            return pltpu.make_async_remote_copy(src_ref=src, dst_ref=dst, send_sem=send.at[t * per + s],
                                                recv_sem=recv.at[t * per + s], device_id=to, device_id_type=MESH)

        mine = [pltpu.make_async_copy(ins[t], outs[t].at[k], local.at[t]) for t in range(n)]
        for cp in mine:
            cp.start()
        sent = []
        for t in range(n):
            for j, (cx, cy) in enumerate(chips):
                cp = rc(t, j, ins[t].at[c], outs[t].at[k, c], (cx, cy, c))
                cp.start()
                sent.append(cp)
        for t in range(n):
            for j, (cx, cy) in enumerate(chips):
                landed = outs[t].at[2 * cx + cy, c]
                rc(t, j, landed, landed, (cx, cy, c)).wait_recv()
                cp = rc(t, N_CHIPS - 1 + j, landed, landed, sib)
                cp.start()
                sent.append(cp)
        for t in range(n):
            for j, (cx, cy) in enumerate(chips):
                other = outs[t].at[2 * cx + cy, 1 - c]
                rc(t, N_CHIPS - 1 + j, other, other, sib).wait_recv()
        for cp in sent:
            cp.wait_send()
        for cp in mine:
            cp.wait()

    any_spec = pl.BlockSpec(memory_space=pl.ANY)
    return pl.pallas_call(
        body, name=name, in_specs=[any_spec] * n, out_specs=[any_spec] * n,
        out_shape=[jax.ShapeDtypeStruct((N_CHIPS,) + p.shape, p.dtype) for p in parts],
        scratch_shapes=[pltpu.SemaphoreType.DMA((n * per,)), pltpu.SemaphoreType.DMA((n * per,)),
                        pltpu.SemaphoreType.DMA((n,))],
        compiler_params=_cp(16 * MIB),
    )(*parts)


def _allgather_flat(v, name):
    r, c = v.shape
    rels = [(dx, dy, dc) for dx in (0, 1) for dy in (0, 1) for dc in (0, 1)][1:]

    def body(v_ref, out_ref, send, recv):
        x, y, cc = _my_place()
        me = 4 * x + 2 * y + cc

        def peer(rel):
            dx, dy, dc = rel
            return (1 - x if dx else x, 1 - y if dy else y, 1 - cc if dc else cc)

        def rc(s, slot, to):
            return pltpu.make_async_remote_copy(src_ref=v_ref, dst_ref=out_ref.at[slot], send_sem=send.at[s],
                                                recv_sem=recv.at[s], device_id=to, device_id_type=MESH)

        sent = []
        for s, rel in enumerate(rels):
            cp = rc(s, me, peer(rel))
            cp.start()
            sent.append(cp)
        out_ref[me] = v_ref[...]
        for s, rel in enumerate(rels):
            px, py, pc = peer(rel)
            rc(s, 4 * px + 2 * py + pc, (px, py, pc)).wait_recv()
        for cp in sent:
            cp.wait_send()

    return pl.pallas_call(
        body, name=name, out_shape=jax.ShapeDtypeStruct((N_DEV, r, c), F32),
        in_specs=[pl.BlockSpec(memory_space=pltpu.VMEM)], out_specs=pl.BlockSpec(memory_space=pltpu.VMEM),
        scratch_shapes=[pltpu.SemaphoreType.DMA((N_DEV - 1,)), pltpu.SemaphoreType.DMA((N_DEV - 1,))],
        compiler_params=_cp((N_DEV + 2) * 4 * r * c + 8 * MIB),
    )(v)


def _rs_pair_exchange(grads, name):
    n = len(grads)

    def body(*refs):
        ins, outs = refs[:n], refs[n:2 * n]
        send, recv = refs[2 * n:]
        x, y, c = _my_place()
        cps = []
        for t in range(n):
            cp = pltpu.make_async_remote_copy(src_ref=ins[t].at[:, 1 - c], dst_ref=outs[t], send_sem=send.at[t],
                                              recv_sem=recv.at[t], device_id=(x, y, 1 - c), device_id_type=MESH)
            cp.start()
            cps.append(cp)
        for cp in cps:
            cp.wait()

    any_spec = pl.BlockSpec(memory_space=pl.ANY)
    return pl.pallas_call(
        body, name=name, in_specs=[any_spec] * n, out_specs=[any_spec] * n,
        out_shape=[jax.ShapeDtypeStruct((g.shape[0],) + g.shape[2:], F32) for g in grads],
        scratch_shapes=[pltpu.SemaphoreType.DMA((n,)), pltpu.SemaphoreType.DMA((n,))],
        compiler_params=_cp(16 * MIB),
    )(*grads)


def _rs_pair_add(place, grad, got, name):
    nk, _, r2, c = grad.shape
    tr = min(TR_ELT, r2)

    def body(place_ref, g_ref, r_ref, o16_ref, o32_ref):
        del place_ref
        s = g_ref[...] + r_ref[...]
        o32_ref[...] = s
        o16_ref[...] = s.astype(BF16)

    blk = pl.BlockSpec((None, tr, c), lambda k, i, p: (k, i, 0))
    return pl.pallas_call(
        body, name=name,
        grid_spec=pltpu.PrefetchScalarGridSpec(
            num_scalar_prefetch=1, grid=(nk, r2 // tr),
            in_specs=[pl.BlockSpec((None, None, tr, c), lambda k, i, p: (k, p[0], i, 0)), blk],
            out_specs=[blk, blk]),
        out_shape=[jax.ShapeDtypeStruct((nk, r2, c), BF16), jax.ShapeDtypeStruct((nk, r2, c), F32)],
        compiler_params=_cp(2 * 14 * tr * c + 8 * MIB),
    )(place, grad, got)


_HBM_SPEC = pl.BlockSpec(memory_space=pltpu.HBM)
_SEM_SPEC = pl.BlockSpec(memory_space=pltpu.SEMAPHORE)
_DATAFLOW = pltpu.SideEffectType.DATAFLOW_SIDE_EFFECTING


def _split_copy_start(srcs, land_shapes, plan, n_sems, name):
    ns, nl = len(srcs), len(land_shapes)

    def body(*refs):
        src_refs, land_refs = refs[:ns], refs[ns:ns + nl]
        send, recv = refs[ns + nl], refs[ns + nl + 1]
        token = refs[-1]
        sends, _ = plan(src_refs, land_refs)
        for src, dst, to, si, ri in sends:
            pltpu.make_async_remote_copy(src_ref=src, dst_ref=dst, send_sem=send.at[si], recv_sem=recv.at[ri],
                                         device_id=to, device_id_type=MESH).start()
        token[...] = jnp.zeros_like(token)

    lands = [lax.empty(shp, dt) for shp, dt in land_shapes]
    through = [pltpu.HBM(a.shape, a.dtype) for a in srcs] + [pltpu.HBM(shp, dt) for shp, dt in land_shapes]
    out = pl.pallas_call(
        body, name=name,
        out_shape=(pltpu.SemaphoreType.DMA((n_sems,)), pltpu.SemaphoreType.DMA((n_sems,)), *through,
                   jax.ShapeDtypeStruct((SUBLANES, SLAB), F32)),
        in_specs=[_HBM_SPEC] * (ns + nl),
        out_specs=(_SEM_SPEC, _SEM_SPEC, *([_HBM_SPEC] * (ns + nl)), pl.BlockSpec(memory_space=pltpu.VMEM)),
        input_output_aliases={i: 2 + i for i in range(ns + nl)},
        compiler_params=pltpu.CompilerParams(has_side_effects=_DATAFLOW),
    )(*[pltpu.with_memory_space_constraint(a, pltpu.HBM) for a in (*srcs, *lands)])
    return out[0], out[1], list(out[2:2 + ns]), list(out[2 + ns:2 + ns + nl]), out[-1]


def _split_copy_wait(send, recv, srcs, lands, plan, after, name):
    ns, nl, na = len(srcs), len(lands), len(after)

    def body(*refs):
        src_refs, land_refs = refs[:ns], refs[ns:ns + nl]
        send_ref, recv_ref = refs[ns + nl], refs[ns + nl + 1]
        sends, arrivals = plan(src_refs, land_refs)
        for src, dst, to, si, ri in sends:
            pltpu.make_async_remote_copy(src_ref=src, dst_ref=dst, send_sem=send_ref.at[si], recv_sem=recv_ref.at[ri],
                                         device_id=to, device_id_type=MESH).wait_send()
        for (src, _, to, si, _), (view, ri) in zip(sends, arrivals):
            pltpu.make_async_remote_copy(src_ref=view, dst_ref=view, send_sem=send_ref.at[si], recv_sem=recv_ref.at[ri],
                                         device_id=to, device_id_type=MESH).wait_recv()

    out = pl.pallas_call(
        body, name=name,
        out_shape=[pltpu.HBM(a.shape, a.dtype) for a in (*srcs, *lands)],
        in_specs=[_HBM_SPEC] * (ns + nl) + [_SEM_SPEC, _SEM_SPEC] + [pl.BlockSpec(memory_space=pl.ANY)] * na,
        out_specs=[_HBM_SPEC] * (ns + nl),
        input_output_aliases={i: i for i in range(ns + nl)},
        compiler_params=pltpu.CompilerParams(has_side_effects=_DATAFLOW),
    )(*srcs, *lands, send, recv, *after)
    return list(out[:ns]), list(out[ns:])


def _chip_exchange_plan(n):
    per = N_CHIPS - 1

    def plan(srcs, lands):
        x, y, c = _my_place()
        sends, arrivals = [], []
        for t in range(n):
            for j, (cx, cy) in enumerate(_other_chips(x, y)):
                sends.append((srcs[t].at[2 * cx + cy], lands[t].at[j], (cx, cy, c), t * per + j, t * per + j))
                arrivals.append((lands[t].at[j], t * per + j))
        return sends, arrivals

    return plan


def _gather_direct_plan(n):
    per = 2 * (N_CHIPS - 1)

    def plan(srcs, lands):
        x, y, c = _my_place()
        k = 2 * x + y
        sends, arrivals = [], []
        for t in range(n):
            for j, (cx, cy) in enumerate(_other_chips(x, y)):
                for core in (0, 1):
                    sends.append((srcs[t].at[c], lands[t].at[k, c], (cx, cy, core),
                                  t * per + 2 * j + core, t * per + 2 * j + c))
                    arrivals.append((lands[t].at[2 * cx + cy, core], t * per + 2 * j + core))
        return sends, arrivals

    return plan


def _rs_chip_add(place, sums32, got, name):
    _, r2, c = sums32.shape
    tr = min(TR_ELT, r2)

    def body(place_ref, s_ref, q_ref, o_ref):
        del place_ref
        o_ref[...] = ((s_ref[...] + q_ref[0].astype(F32)) + q_ref[1].astype(F32)) + q_ref[2].astype(F32)

    return pl.pallas_call(
        body, name=name,
        grid_spec=pltpu.PrefetchScalarGridSpec(
            num_scalar_prefetch=1, grid=(r2 // tr,),
            in_specs=[pl.BlockSpec((None, tr, c), lambda i, p: (p[1], i, 0)),
                      pl.BlockSpec((N_CHIPS - 1, tr, c), lambda i, p: (0, i, 0))],
            out_specs=pl.BlockSpec((tr, c), lambda i, p: (i, 0))),
        out_shape=jax.ShapeDtypeStruct((r2, c), F32),
        compiler_params=_cp(2 * 14 * tr * c + 8 * MIB),
    )(place, sums32, got)


def _rs_pair_share(halves, name):
    n = len(halves)

    def body(*refs):
        ins, outs = refs[:n], refs[n:2 * n]
        send, recv, local = refs[2 * n:]
        x, y, c = _my_place()
        cps, mine = [], []
        for t in range(n):
            lc = pltpu.make_async_copy(ins[t], outs[t].at[c], local.at[t])
            lc.start()
            mine.append(lc)
            cp = pltpu.make_async_remote_copy(src_ref=ins[t], dst_ref=outs[t].at[c], send_sem=send.at[t],
                                              recv_sem=recv.at[t], device_id=(x, y, 1 - c), device_id_type=MESH)
            cp.start()
            cps.append(cp)
        for t in range(n):
            other = outs[t].at[1 - c]
            pltpu.make_async_remote_copy(src_ref=other, dst_ref=other, send_sem=send.at[t], recv_sem=recv.at[t],
                                         device_id=(x, y, 1 - c), device_id_type=MESH).wait_recv()
        for cp in cps:
            cp.wait_send()
        for lc in mine:
            lc.wait()

    any_spec = pl.BlockSpec(memory_space=pl.ANY)
    return pl.pallas_call(
        body, name=name, in_specs=[any_spec] * n, out_specs=[any_spec] * n,
        out_shape=[jax.ShapeDtypeStruct((2,) + h.shape, F32) for h in halves],
        scratch_shapes=[pltpu.SemaphoreType.DMA((n,)), pltpu.SemaphoreType.DMA((n,)), pltpu.SemaphoreType.DMA((n,))],
        compiler_params=_cp(16 * MIB),
    )(*halves)


_PACK_TILE = SUBLANES * SLAB


def _pack(arrays):
    rows = []
    for a in arrays:
        flat = a.reshape(-1).astype(F32)
        padded = -(-flat.shape[0] // _PACK_TILE) * _PACK_TILE
        rows.append(jnp.pad(flat, (0, padded - flat.shape[0])).reshape(-1, SLAB))
    n_rows = sum(r.shape[0] for r in rows)
    if n_rows % (2 * SUBLANES):
        rows.append(jnp.zeros((SUBLANES, SLAB), F32))
    return jnp.concatenate(rows, axis=0)


def _unpack(packed, shapes):
    out, row = [], 0
    for shp in shapes:
        size = math.prod(shp)
        nrow = -(-size // _PACK_TILE) * SUBLANES
        out.append(packed[row:row + nrow].reshape(-1)[:size].reshape(shp))
        row += nrow
    return out


def kernel(x, norm_pre_g, w_in, conv_w, conv_b, ssm_a_re, ssm_a_im, ssm_log_dt, ssm_b_re, ssm_b_im, ssm_c_re, ssm_c_im, ssm_d, w_glu, b_glu, w_out, norm_post_g, loss_target, m_norm_pre_g, m_w_in, m_conv_w, m_conv_b, m_ssm_a_re, m_ssm_a_im, m_ssm_log_dt, m_ssm_b_re, m_ssm_b_im, m_ssm_c_re, m_ssm_c_im, m_ssm_d, m_w_glu, m_b_glu, m_w_out, m_norm_post_g, v_norm_pre_g, v_w_in, v_conv_w, v_conv_b, v_ssm_a_re, v_ssm_a_im, v_ssm_log_dt, v_ssm_b_re, v_ssm_b_im, v_ssm_c_re, v_ssm_c_im, v_ssm_d, v_w_glu, v_b_glu, v_w_out, v_norm_post_g):
    xs, tgt = x[0], loss_target[0]
    t, d = xs.shape
    dc = conv_b.shape[0]
    dssm = ssm_d.shape[0]
    g, p = ssm_a_re.shape
    h = SSM_H
    nq = dssm // SLAB
    n_shard = w_in.shape[1]
    steps = min(T_SCAN, t) // SUBLANES
    mx, my, mc = _my_place()
    chip = 2 * mx + my
    place = jnp.stack([mc, chip]).astype(jnp.int32)

    halves = lambda a: a.reshape(2, a.shape[0] // 2, a.shape[1])
    (win_g,) = _allgather_halves([halves(_cast_bf16(w_in, "cast_w_in"))], "allgather_w_in")
    win_b = win_g.reshape(N_CHIPS, d, n_shard)
    side = [halves(_cast_bf16(w_out, "cast_w_out")), halves(_cast_bf16(w_glu, "cast_w_glu"))]
    side_plan = _gather_direct_plan(len(side))
    side_sems = 2 * (N_CHIPS - 1) * len(side)
    ag_send, ag_recv, ag_srcs, ag_lands, ag_token = _split_copy_start(
        side, [((N_CHIPS,) + a.shape, BF16) for a in side], side_plan, side_sems, "gather_side_weights_start")

    cw_cols = conv_w.shape[1]
    cw_pad = -(-cw_cols // SLAB) * SLAB
    cw_blk = jnp.zeros((SUBLANES, cw_pad), F32).at[:conv_w.shape[0], :cw_cols].set(conv_w)
    cw_all = _allgather_flat(cw_blk, "allgather_conv_w")
    conv_w_full = jnp.concatenate([cw_all[2 * k, :, :cw_cols] for k in range(N_CHIPS)], axis=1)

    expand = jnp.repeat(jnp.eye(p, dtype=F32), h, axis=1)
    b_re2, b_im2 = ssm_b_re.reshape(g, p * h), ssm_b_im.reshape(g, p * h)
    log_dt2 = ssm_log_dt.reshape(g, 1)
    lbr, lbi, bbr2, bbi2, pw3 = _zoh_fwd(ssm_a_re, ssm_a_im, log_dt2, b_re2, b_im2, expand, steps)
    lam = jnp.stack([lbr.reshape(g * p), lbi.reshape(g * p)])
    pw = pw3.reshape(2, g * p)
    to_slab_b = lambda b2: _blockdiag(b2.reshape(nq, GROUPS_PER_SLAB, p, h).transpose(0, 1, 3, 2))
    bq = jnp.concatenate([to_slab_b(bbr2), to_slab_b(bbi2)], axis=2).astype(BF16)
    to_slab_c = lambda c3: _blockdiag(c3.reshape(nq, GROUPS_PER_SLAB, h, p).transpose(0, 1, 3, 2))
    cq = jnp.concatenate([to_slab_c(ssm_c_re), to_slab_c(-ssm_c_im)], axis=1).astype(BF16)

    g_pre2, g_post2 = norm_pre_g.reshape(1, d), norm_post_g.reshape(1, d)
    conv_b2, b_glu2, d_skip2 = conv_b.reshape(1, dc), b_glu.reshape(1, dssm), ssm_d.reshape(1, dssm)
    u_block = 4 * dc // dssm

    hb = _prenorm(xs, g_pre2 + ag_token[0:1, 0:1])
    proj = _matmul_nn(hb, win_b, "inproj")
    y, cin = _ssm_fwd(proj, u_block, bq, cq, lam, pw, d_skip2)
    side_own, side_all = _split_copy_wait(ag_send, ag_recv, ag_srcs, ag_lands, side_plan, [cin],
                                          "gather_side_weights_wait")
    wout_g, wglu_g = [lax.dynamic_update_index_in_dim(all_, own, chip, 0) for own, all_ in zip(side_own, side_all)]
    wout_b = wout_g.reshape(dc + dssm, d)
    wglu_b = wglu_g.reshape(dssm, dssm)
    mix = _gate_fwd(proj, y, conv_w_full, conv_b2, wglu_b, b_glu2, dc)
    loss_blk, dout, dob, dmix, gg_post = _outproj(mix, wout_b, xs, tgt, g_post2)

    def reduce_start(grads, tags, group):
        got = _rs_pair_exchange(grads, "rs_pair_exchange_" + group)
        sums = [_rs_pair_add(place, gt, rt, "rs_pair_add_" + tg) for gt, rt, tg in zip(grads, got, tags)]
        plan = _chip_exchange_plan(len(grads))
        started = _split_copy_start([s16 for s16, _ in sums],
                                    [((N_CHIPS - 1,) + s16.shape[1:], BF16) for s16, _ in sums], plan,
                                    (N_CHIPS - 1) * len(grads), "rs_chip_exchange_" + group + "_start")
        return plan, started, [s32 for _, s32 in sums]

    def reduce_finish(plan, started, sums32, tags, group, after):
        send, recv, srcs, lands, _ = started
        _, landed = _split_copy_wait(send, recv, srcs, lands, plan, after, "rs_chip_exchange_" + group + "_wait")
        mine = [_rs_chip_add(place, s32, qt, "rs_chip_add_" + tg) for s32, qt, tg in zip(sums32, landed, tags)]
        full = _rs_pair_share(mine, "rs_pair_share_" + group)
        return [f.reshape(2 * f.shape[1], f.shape[2]) for f in full]

    gw_out = _matmul_tn(mix, dob, 1, "grad_w_out")
    dproj, dy, gsmall, gw_glu = _gate_bwd(proj, y, dmix, conv_w_full, conv_b2, wglu_b, b_glu2, dc)
    rs_a = reduce_start([gw_out.reshape(N_CHIPS, 2, (dc + dssm) // (2 * N_CHIPS), d),
                         gw_glu.reshape(N_CHIPS, 2, dssm // (2 * N_CHIPS), dssm)], ["w_out", "w_glu"], "a")
    dproj, gb_dense, gc_dense, glam, gd = _ssm_bwd(proj, u_block, dy, cin, dproj, bq, cq, lam, pw,
                                                   d_skip2 + rs_a[1][4][0:1, 0:1])
    gw_in = _matmul_tn(hb, dproj, N_CHIPS, "grad_w_in")
    rs_b = reduce_start([gw_in.reshape(N_CHIPS, 2, d // 2, n_shard)], ["w_in"], "b")
    gx, gg_pre = _dh_prenorm_bwd(dproj, win_b, xs, dout, g_pre2 + rs_b[1][4][0:1, 0:1])

    gb4 = gb_dense.reshape(nq, SLAB, 2, SLAB_STATES)
    g_bbr2 = _blockdiag_take(gb4[:, :, 0, :], h, p).transpose(0, 1, 3, 2).reshape(g, p * h)
    g_bbi2 = _blockdiag_take(gb4[:, :, 1, :], h, p).transpose(0, 1, 3, 2).reshape(g, p * h)
    gc4 = gc_dense.reshape(nq, SLAB, 2, SLAB_STATES)
    g_c_re = _blockdiag_take(gc4[:, :, 0, :], h, p).reshape(g, h, p)
    g_c_im = -_blockdiag_take(gc4[:, :, 1, :], h, p).reshape(g, h, p)
    g_a_re, g_a_im, g_ld, g_b_re2, g_b_im2 = _zoh_bwd(
        ssm_a_re, ssm_a_im, log_dt2, b_re2, b_im2, expand,
        glam[0].reshape(g, p), glam[1].reshape(g, p), g_bbr2, g_bbi2)

    small_names = ["norm_pre_g", "conv_w", "conv_b", "ssm_a_re", "ssm_a_im", "ssm_log_dt", "ssm_b_re", "ssm_b_im",
                   "ssm_c_re", "ssm_c_im", "ssm_d", "b_glu", "norm_post_g", "loss"]
    small_g = {
        "norm_pre_g": gg_pre[0], "conv_w": gsmall[0:3], "conv_b": gsmall[3], "ssm_a_re": g_a_re, "ssm_a_im": g_a_im,
        "ssm_log_dt": g_ld.reshape(g), "ssm_b_re": g_b_re2.reshape(g, p, h), "ssm_b_im": g_b_im2.reshape(g, p, h),
        "ssm_c_re": g_c_re, "ssm_c_im": g_c_im, "ssm_d": gd[0], "b_glu": gsmall[4], "norm_post_g": gg_post[0],
        "loss": loss_blk[0, 0:1],
    }
    zeros_cw = jnp.zeros((conv_w.shape[0], dc), F32)
    one0 = jnp.zeros((1,), F32)
    small_w = dict(norm_pre_g=norm_pre_g, conv_w=zeros_cw, conv_b=conv_b, ssm_a_re=ssm_a_re, ssm_a_im=ssm_a_im,
                   ssm_log_dt=ssm_log_dt, ssm_b_re=ssm_b_re, ssm_b_im=ssm_b_im, ssm_c_re=ssm_c_re, ssm_c_im=ssm_c_im,
                   ssm_d=ssm_d, b_glu=b_glu, norm_post_g=norm_post_g, loss=one0)
    small_m = dict(norm_pre_g=m_norm_pre_g, conv_w=zeros_cw, conv_b=m_conv_b, ssm_a_re=m_ssm_a_re, ssm_a_im=m_ssm_a_im,
                   ssm_log_dt=m_ssm_log_dt, ssm_b_re=m_ssm_b_re, ssm_b_im=m_ssm_b_im, ssm_c_re=m_ssm_c_re,
                   ssm_c_im=m_ssm_c_im, ssm_d=m_ssm_d, b_glu=m_b_glu, norm_post_g=m_norm_post_g, loss=one0)
    small_v = dict(norm_pre_g=v_norm_pre_g, conv_w=zeros_cw, conv_b=v_conv_b, ssm_a_re=v_ssm_a_re, ssm_a_im=v_ssm_a_im,
                   ssm_log_dt=v_ssm_log_dt, ssm_b_re=v_ssm_b_re, ssm_b_im=v_ssm_b_im, ssm_c_re=v_ssm_c_re,
                   ssm_c_im=v_ssm_c_im, ssm_d=v_ssm_d, b_glu=v_b_glu, norm_post_g=v_norm_post_g, loss=one0)
    shapes = [small_w[nm].shape for nm in small_names]
    g_pack = _allreduce_small(_pack([small_g[nm] for nm in small_names]))
    packs = _adamw_small(g_pack, _pack([small_w[nm] for nm in small_names]),
                         _pack([small_m[nm] for nm in small_names]), _pack([small_v[nm] for nm in small_names]))
    sg, sd, sm, sv = [dict(zip(small_names, _unpack(pk, shapes))) for pk in (g_pack, *packs)]
    loss = sg["loss"][0]

    g_cw = lax.dynamic_slice_in_dim(sg["conv_w"], chip * cw_cols, cw_cols, axis=1)
    pad_cw = lambda a: jnp.zeros((SUBLANES, cw_pad), F32).at[:a.shape[0], :cw_cols].set(a)
    cut_cw = lambda a: a[:conv_w.shape[0], :cw_cols]
    d_cw, m_cw, v_cw = [cut_cw(a) for a in _adamw(pad_cw(conv_w), pad_cw(g_cw), pad_cw(m_conv_w), pad_cw(v_conv_w),
                                                  "adamw_conv_w")]

    g_wout, g_wglu = reduce_finish(*rs_a, ["w_out", "w_glu"], "a", [d_cw, packs[0]])
    (g_win,) = reduce_finish(*rs_b, ["w_in"], "b", [g_wout])
    d_win, m_win, v_win = _adamw(w_in, g_win, m_w_in, v_w_in, "adamw_w_in")
    d_wout, m_wout, v_wout = _adamw(w_out, g_wout, m_w_out, v_w_out, "adamw_w_out")
    d_wglu, m_wglu, v_wglu = _adamw(w_glu, g_wglu, m_w_glu, v_w_glu, "adamw_w_glu")

    order = ["norm_pre_g", "w_in", "conv_w", "conv_b", "ssm_a_re", "ssm_a_im", "ssm_log_dt", "ssm_b_re", "ssm_b_im",
             "ssm_c_re", "ssm_c_im", "ssm_d", "w_glu", "b_glu", "w_out", "norm_post_g"]
    grads, deltas, new_m, new_v = dict(sg), dict(sd), dict(sm), dict(sv)
    grads.update(w_in=g_win, w_out=g_wout, w_glu=g_wglu, conv_w=g_cw)
    deltas.update(w_in=d_win, w_out=d_wout, w_glu=d_wglu, conv_w=d_cw)
    new_m.update(w_in=m_win, w_out=m_wout, w_glu=m_wglu, conv_w=m_cw)
    new_v.update(w_in=v_win, w_out=v_wout, w_glu=v_wglu, conv_w=v_cw)
    return (loss, gx[None], *[grads[nm] for nm in order], *[deltas[nm] for nm in order],
            *[new_m[nm] for nm in order], *[new_v[nm] for nm in order])
```

```python
import functools
import math

import jax
import jax.numpy as jnp
from jax import lax
from jax.experimental import pallas as pl
from jax.experimental.pallas import tpu as pltpu

F32 = jnp.float32
BF16 = jnp.bfloat16
MESH = pl.DeviceIdType.MESH

EPS = 1e-6
SSM_H = 16
SSM_P = 64
GROUPS_PER_SLAB = 8
SLAB = 128
SLAB_STATES = GROUPS_PER_SLAB * SSM_P
N_CHIPS = 4
N_DEV = 8

ADAM_LR = 0.001
ADAM_B1 = 0.9
ADAM_B2 = 0.999
ADAM_EPS = 1e-08
ADAM_WD = 0.01
ADAM_STEP = 10

MIB = 1024 * 1024
VMEM_CAP = 48 * MIB
SUBLANES = 8

TM_NORM = 512
TM_PROJ = 512
TM_GATE = 256
TM_OUT = 256
TM_DH = 256
T_SCAN = 256
TK_TN = 512
SCAN_UNROLL = 4
TM_TN = 1024
TR_ELT = 256


def _cp(vmem_bytes, **kw):
    return pltpu.CompilerParams(vmem_limit_bytes=int(min(VMEM_CAP, max(16 * MIB, vmem_bytes))), **kw)


def _my_place():
    return lax.axis_index("x"), lax.axis_index("y"), lax.axis_index("c")


def _other_chips(x, y):
    return [(1 - x, y), (x, 1 - y), (1 - x, 1 - y)]


def _silu(z):
    s = jax.nn.sigmoid(z)
    return z * s, s


def _dsilu(z, s):
    return s * (1.0 + z * (1.0 - s))


_GELU_K = math.sqrt(2.0 / math.pi)
_GELU_C = 0.044715


def _gelu(y):
    th = jnp.tanh(_GELU_K * (y + _GELU_C * y * y * y))
    return 0.5 * y * (1.0 + th), th


def _dgelu(y, th):
    return 0.5 * (1.0 + th) + 0.5 * y * (1.0 - th * th) * _GELU_K * (1.0 + 3.0 * _GELU_C * y * y)


def _cast_bf16(w, name, after=None):
    r, c = w.shape
    tr = min(TR_ELT, r)
    extra = [] if after is None else [after]

    def body(w_ref, *rest):
        rest[-1][...] = w_ref[...].astype(BF16)

    return pl.pallas_call(
        body, name=name, grid=(r // tr,),
        in_specs=[pl.BlockSpec((tr, c), lambda i: (i, 0))] + [pl.BlockSpec((SUBLANES, SLAB), lambda i: (0, 0))] * len(extra),
        out_specs=pl.BlockSpec((tr, c), lambda i: (i, 0)),
        out_shape=jax.ShapeDtypeStruct((r, c), BF16),
        compiler_params=_cp(12 * tr * c),
    )(w, *extra)


def _prenorm(x, g):
    t, d = x.shape
    tm = min(TM_NORM, t)

    def body(x_ref, g_ref, h_ref):
        xv = x_ref[...]
        r = lax.rsqrt(jnp.mean(xv * xv, axis=-1, keepdims=True) + EPS)
        h_ref[...] = (xv * r * g_ref[...]).astype(BF16)

    return pl.pallas_call(
        body, name="prenorm", grid=(t // tm,),
        in_specs=[pl.BlockSpec((tm, d), lambda i: (i, 0)), pl.BlockSpec((1, d), lambda i: (0, 0))],
        out_specs=pl.BlockSpec((tm, d), lambda i: (i, 0)),
        out_shape=jax.ShapeDtypeStruct((t, d), BF16),
        compiler_params=_cp(20 * tm * d),
    )(x, g)


def _matmul_nn(a, b, name):
    t, k = a.shape
    nb, _, n = b.shape
    tm = min(TM_PROJ, t)

    def body(a_ref, b_ref, o_ref):
        o_ref[...] = jnp.dot(a_ref[...], b_ref[...], preferred_element_type=F32)

    return pl.pallas_call(
        body, name=name, grid=(nb, t // tm),
        in_specs=[pl.BlockSpec((tm, k), lambda j, i: (i, 0)), pl.BlockSpec((None, k, n), lambda j, i: (j, 0, 0))],
        out_specs=pl.BlockSpec((tm, n), lambda j, i: (i, j)),
        out_shape=jax.ShapeDtypeStruct((t, nb * n), F32),
        compiler_params=_cp(2 * (2 * tm * k + 2 * k * n + 4 * tm * n) + 4 * MIB),
    )(a, b)


def _matmul_tn(a, b, nb, name):
    t, m = a.shape
    n = b.shape[1] // nb
    tk = min(TK_TN, t)
    tma = min(TM_TN, m)

    def body(a_ref, b_ref, o_ref):
        @pl.when(pl.program_id(2) == 0)
        def _():
            o_ref[...] = jnp.zeros_like(o_ref)

        o_ref[...] += lax.dot_general(a_ref[...], b_ref[...], (((0,), (0,)), ((), ())), preferred_element_type=F32)

    return pl.pallas_call(
        body, name=name, grid=(nb, m // tma, t // tk),
        in_specs=[pl.BlockSpec((tk, tma), lambda j, i, k: (k, i)), pl.BlockSpec((tk, n), lambda j, i, k: (k, j))],
        out_specs=pl.BlockSpec((None, tma, n), lambda j, i, k: (j, i, 0)),
        out_shape=jax.ShapeDtypeStruct((nb, m, n), F32),
        compiler_params=_cp(2 * (2 * tk * tma + 2 * tk * n + 4 * tma * n) + 8 * MIB),
    )(a, b)


def _outproj(mix, w_out, x, tgt, g_post):
    t, dm = mix.shape
    d = w_out.shape[1]
    tm = min(TM_OUT, t)

    def body(mix_ref, w_ref, x_ref, t_ref, g_ref, loss_ref, dout_ref, do_ref, dmix_ref, gg_ref):
        @pl.when(pl.program_id(0) == 0)
        def _():
            loss_ref[...] = jnp.zeros_like(loss_ref)
            gg_ref[...] = jnp.zeros_like(gg_ref)

        w = w_ref[...]
        o = jnp.dot(mix_ref[...], w, preferred_element_type=F32)
        r = lax.rsqrt(jnp.mean(o * o, axis=-1, keepdims=True) + EPS)
        nh = o * r
        g = g_ref[...]
        e = x_ref[...] + nh * g - t_ref[...]
        loss_ref[...] += jnp.sum(e * e) * (0.5 / d)
        dout = e * (1.0 / d)
        dout_ref[...] = dout
        gg_ref[0:1, :] += jnp.sum(dout * nh, axis=0, keepdims=True)
        dn = dout * g
        do = r * (dn - nh * jnp.mean(dn * nh, axis=-1, keepdims=True))
        dob = do.astype(BF16)
        do_ref[...] = dob
        dmix_ref[...] = lax.dot_general(dob, w, (((1,), (1,)), ((), ())), preferred_element_type=F32)

    row = lambda i: (i, 0)
    fixed = lambda i: (0, 0)
    return pl.pallas_call(
        body, name="outproj", grid=(t // tm,),
        in_specs=[pl.BlockSpec((tm, dm), row), pl.BlockSpec((dm, d), fixed), pl.BlockSpec((tm, d), row),
                  pl.BlockSpec((tm, d), row), pl.BlockSpec((1, d), fixed)],
        out_specs=[pl.BlockSpec((SUBLANES, SLAB), fixed), pl.BlockSpec((tm, d), row), pl.BlockSpec((tm, d), row),
                   pl.BlockSpec((tm, dm), row), pl.BlockSpec((SUBLANES, d), fixed)],
        out_shape=[jax.ShapeDtypeStruct((SUBLANES, SLAB), F32), jax.ShapeDtypeStruct((t, d), F32),
                   jax.ShapeDtypeStruct((t, d), BF16), jax.ShapeDtypeStruct((t, dm), F32),
                   jax.ShapeDtypeStruct((SUBLANES, d), F32)],
        compiler_params=_cp(2 * (2 * dm * d + tm * (2 * dm + 4 * d * 3 + 2 * d + 4 * dm)) + 16 * MIB),
    )(mix, w_out, x, tgt, g_post)


def _dh_prenorm_bwd(dproj, w_in, x, dout, g_pre):
    t, d = x.shape
    nb, _, n = w_in.shape
    tm = min(TM_DH, t)

    def body(dp_ref, w_ref, x_ref, dout_ref, g_ref, dx_ref, gg_ref):
        @pl.when(pl.program_id(0) == 0)
        def _():
            gg_ref[...] = jnp.zeros_like(gg_ref)

        dh = None
        for k in range(nb):
            part = lax.dot_general(dp_ref[:, k * n:(k + 1) * n], w_ref[k], (((1,), (1,)), ((), ())),
                                   preferred_element_type=F32)
            dh = part if dh is None else dh + part
        xv = x_ref[...]
        r = lax.rsqrt(jnp.mean(xv * xv, axis=-1, keepdims=True) + EPS)
        xh = xv * r
        gg_ref[0:1, :] += jnp.sum(dh * xh, axis=0, keepdims=True)
        dg = dh * g_ref[...]
        dx_ref[...] = dout_ref[...] + r * (dg - xh * jnp.mean(dg * xh, axis=-1, keepdims=True))

    row = lambda i: (i, 0)
    fixed = lambda i: (0, 0)
    w_spec = pl.BlockSpec(w_in.shape, lambda i: (0, 0, 0), pipeline_mode=pl.Buffered(1))
    return pl.pallas_call(
        body, name="dh_prenorm_bwd", grid=(t // tm,),
        in_specs=[pl.BlockSpec((tm, nb * n), row), w_spec,
                  pl.BlockSpec((tm, d), row), pl.BlockSpec((tm, d), row), pl.BlockSpec((1, d), fixed)],
        out_specs=[pl.BlockSpec((tm, d), row), pl.BlockSpec((SUBLANES, d), fixed)],
        out_shape=[jax.ShapeDtypeStruct((t, d), F32), jax.ShapeDtypeStruct((SUBLANES, d), F32)],
        compiler_params=_cp(2 * nb * d * n + 2 * (2 * tm * nb * n + 12 * tm * d) + 16 * tm * d + 4 * MIB),
    )(dproj, w_in, x, dout, g_pre)


def _adamw(w, g, m, v, name):
    r, c = w.shape
    tr = min(TR_ELT, r)
    c1 = 1.0 - ADAM_B1 ** ADAM_STEP
    c2 = 1.0 - ADAM_B2 ** ADAM_STEP

    def body(w_ref, g_ref, m_ref, v_ref, d_ref, mo_ref, vo_ref):
        gv = g_ref[...]
        mn = ADAM_B1 * m_ref[...] + (1.0 - ADAM_B1) * gv
        vn = ADAM_B2 * v_ref[...] + (1.0 - ADAM_B2) * (gv * gv)
        d_ref[...] = -ADAM_LR * ((mn / c1) / (jnp.sqrt(vn / c2) + ADAM_EPS) + ADAM_WD * w_ref[...])
        mo_ref[...] = mn
        vo_ref[...] = vn

    spec = pl.BlockSpec((tr, c), lambda i: (i, 0))
    sds = jax.ShapeDtypeStruct((r, c), F32)
    return pl.pallas_call(
        body, name=name, grid=(r // tr,), in_specs=[spec] * 4, out_specs=[spec] * 3, out_shape=[sds] * 3,
        compiler_params=_cp(2 * 7 * 4 * tr * c + 8 * MIB),
    )(w, g, m, v)


def _adamw_small(g, w, m, v):
    r, c = w.shape
    c1 = 1.0 - ADAM_B1 ** ADAM_STEP
    c2 = 1.0 - ADAM_B2 ** ADAM_STEP

    def body(g_ref, w_ref, m_ref, v_ref, d_ref, mo_ref, vo_ref):
        gv = g_ref[...]
        mn = ADAM_B1 * m_ref[...] + (1.0 - ADAM_B1) * gv
        vn = ADAM_B2 * v_ref[...] + (1.0 - ADAM_B2) * (gv * gv)
        d_ref[...] = -ADAM_LR * ((mn / c1) / (jnp.sqrt(vn / c2) + ADAM_EPS) + ADAM_WD * w_ref[...])
        mo_ref[...] = mn
        vo_ref[...] = vn

    sds = jax.ShapeDtypeStruct((r, c), F32)
    return pl.pallas_call(
        body, name="adamw_small", out_shape=[sds] * 3,
        compiler_params=_cp(12 * 4 * r * c + 8 * MIB),
    )(g, w, m, v)


def _allreduce_small(v):
    r, c = v.shape
    assert r % (2 * SUBLANES) == 0
    h = r // 2

    def body(v_ref, out_ref, got_ref, chip_ref, slots_ref, send, recv):
        x, y, cc = _my_place()
        k = 2 * x + y
        sib = (x, y, 1 - cc)

        def rc(s, src, dst, to):
            return pltpu.make_async_remote_copy(src_ref=src, dst_ref=dst, send_sem=send.at[s], recv_sem=recv.at[s],
                                                device_id=to, device_id_type=MESH)

        pair = rc(0, v_ref, got_ref, sib)
        pair.start()
        pair.wait()
        chip_ref[...] = v_ref[...] + got_ref[...]
        mine = pl.ds(pl.multiple_of(cc * h, SUBLANES), h)
        theirs = pl.ds(pl.multiple_of((1 - cc) * h, SUBLANES), h)
        chips = _other_chips(x, y)
        sent = []
        for j, (cx, cy) in enumerate(chips):
            cp = rc(1 + j, chip_ref.at[mine], slots_ref.at[k], (cx, cy, cc))
            cp.start()
            sent.append(cp)
        slots_ref[k] = chip_ref[mine, :]
        for j, (cx, cy) in enumerate(chips):
            rc(1 + j, chip_ref.at[mine], slots_ref.at[2 * cx + cy], (cx, cy, cc)).wait_recv()
        total = slots_ref[0]
        for kk in range(1, N_CHIPS):
            total = total + slots_ref[kk]
        out_ref[mine, :] = total
        for cp in sent:
            cp.wait_send()
        share = rc(N_CHIPS, out_ref.at[mine], out_ref.at[mine], sib)
        share.start()
        rc(N_CHIPS, out_ref.at[theirs], out_ref.at[theirs], sib).wait_recv()
        share.wait_send()

    vm = pl.BlockSpec(memory_space=pltpu.VMEM)
    return pl.pallas_call(
        body, name="allreduce_small", out_shape=jax.ShapeDtypeStruct((r, c), F32), in_specs=[vm], out_specs=vm,
        scratch_shapes=[pltpu.VMEM((r, c), F32), pltpu.VMEM((r, c), F32), pltpu.VMEM((N_CHIPS, h, c), F32),
                        pltpu.SemaphoreType.DMA((N_CHIPS + 1,)), pltpu.SemaphoreType.DMA((N_CHIPS + 1,))],
        compiler_params=_cp(6 * 4 * r * c + 8 * MIB),
    )(v)


def _zoh(a_re, a_im, log_dt, b_re2, b_im2, expand):
    dt = jnp.exp(log_dt)
    mag = jnp.exp(a_re * dt)
    lbr, lbi = mag * jnp.cos(a_im * dt), mag * jnp.sin(a_im * dt)
    nr, ni = lbr - 1.0, lbi
    den = a_re * a_re + a_im * a_im
    qr = (nr * a_re + ni * a_im) / den
    qi = (ni * a_re - nr * a_im) / den
    qr2 = jnp.dot(qr, expand, precision=lax.Precision.HIGHEST, preferred_element_type=F32)
    qi2 = jnp.dot(qi, expand, precision=lax.Precision.HIGHEST, preferred_element_type=F32)
    return lbr, lbi, qr2 * b_re2 - qi2 * b_im2, qr2 * b_im2 + qi2 * b_re2


def _zoh_fwd(a_re, a_im, log_dt, b_re2, b_im2, expand, power):
    g, p = a_re.shape

    def body(ar_ref, ai_ref, ld_ref, br_ref, bi_ref, e_ref, lbr_ref, lbi_ref, bbr_ref, bbi_ref, pw_ref):
        ar, ai, ld = ar_ref[...], ai_ref[...], ld_ref[...]
        lbr, lbi, bbr, bbi = _zoh(ar, ai, ld, br_ref[...], bi_ref[...], e_ref[...])
        lbr_ref[...], lbi_ref[...], bbr_ref[...], bbi_ref[...] = lbr, lbi, bbr, bbi
        dt = jnp.exp(ld) * float(power)
        mag = jnp.exp(ar * dt)
        pw_ref[0] = mag * jnp.cos(ai * dt)
        pw_ref[1] = mag * jnp.sin(ai * dt)

    gp = jax.ShapeDtypeStruct((g, p), F32)
    gph = jax.ShapeDtypeStruct(b_re2.shape, F32)
    return pl.pallas_call(
        body, name="zoh_fwd", out_shape=[gp, gp, gph, gph, jax.ShapeDtypeStruct((2, g, p), F32)],
        compiler_params=_cp(16 * MIB),
    )(a_re, a_im, log_dt, b_re2, b_im2, expand)


def _zoh_bwd(a_re, a_im, log_dt, b_re2, b_im2, expand, g_lbr, g_lbi, g_bbr, g_bbi):
    def body(ar_ref, ai_ref, ld_ref, br_ref, bi_ref, e_ref, c0, c1, c2, c3, gar, gai, gld, gbr, gbi):
        e = e_ref[...]
        _, vjp = jax.vjp(lambda a, b, c, d, f: _zoh(a, b, c, d, f, e),
                         ar_ref[...], ai_ref[...], ld_ref[...], br_ref[...], bi_ref[...])
        gar[...], gai[...], gld[...], gbr[...], gbi[...] = vjp((c0[...], c1[...], c2[...], c3[...]))

    sds = lambda a: jax.ShapeDtypeStruct(a.shape, F32)
    return pl.pallas_call(
        body, name="zoh_bwd", out_shape=[sds(a_re), sds(a_im), sds(log_dt), sds(b_re2), sds(b_im2)],
        compiler_params=_cp(16 * MIB),
    )(a_re, a_im, log_dt, b_re2, b_im2, expand, g_lbr, g_lbi, g_bbr, g_bbi)


def _blockdiag(blocks):
    nq, _, r, c = blocks.shape
    eye = jnp.eye(GROUPS_PER_SLAB, dtype=blocks.dtype)
    out = blocks[:, :, :, None, :] * eye[None, :, None, :, None]
    return out.reshape(nq, GROUPS_PER_SLAB * r, GROUPS_PER_SLAB * c)


def _blockdiag_take(dense, r, c):
    nq = dense.shape[0]
    d5 = dense.reshape(nq, GROUPS_PER_SLAB, r, GROUPS_PER_SLAB, c)
    return jnp.stack([d5[:, i, :, i, :] for i in range(GROUPS_PER_SLAB)], axis=1)


def _scan_slab(s_ref, row0, q, lam_ref, pw_ref, car_ref, steps, reverse, prev_ref=None, prev_row0=0, glam_ref=None):
    sign = -1.0 if reverse else 1.0
    half = SLAB_STATES // SLAB
    cols = [(q * 2 * half + m, q * 2 * half + half + m, q * SLAB_STATES + m * SLAB) for m in range(half)]
    nm = len(cols)
    full = (SUBLANES, SLAB)
    lam = [(jnp.broadcast_to(lam_ref[0:1, pl.ds(cl, SLAB)], full),
            jnp.broadcast_to(sign * lam_ref[1:2, pl.ds(cl, SLAB)], full)) for (_, _, cl) in cols]

    def step_rows(jj, base):
        j = (steps - 1 - jj) if reverse else jj
        return j, pl.ds(pl.multiple_of(base + j * SUBLANES, SUBLANES), SUBLANES)

    def pass1(jj, car):
        _, rows = step_rows(jj, row0)
        out = []
        for m, (cr, ci, _) in enumerate(cols):
            sr, si = car[2 * m], car[2 * m + 1]
            lr, li = lam[m]
            nr = lr * sr - li * si + s_ref[cr, rows, :]
            ni = lr * si + li * sr + s_ref[ci, rows, :]
            s_ref[cr, rows, :] = nr
            s_ref[ci, rows, :] = ni
            out += [nr, ni]
        return tuple(out)

    def unrolled(step_fn):
        def outer(jo, carry):
            for k in range(SCAN_UNROLL):
                carry = step_fn(jo * SCAN_UNROLL + k, carry)
            return carry
        return outer

    assert steps % SCAN_UNROLL == 0
    ends = lax.fori_loop(0, steps // SCAN_UNROLL, unrolled(pass1), tuple(jnp.zeros(full, F32) for _ in range(2 * nm)))

    entry = []
    for m, (cr, ci, cl) in enumerate(cols):
        ljr = pw_ref[0:1, pl.ds(cl, SLAB)]
        lji = sign * pw_ref[1:2, pl.ds(cl, SLAB)]
        c_r = car_ref[cr, 0:1, :]
        c_i = car_ref[ci, 0:1, :]
        rows_r, rows_i = [None] * SUBLANES, [None] * SUBLANES
        order = range(SUBLANES - 1, -1, -1) if reverse else range(SUBLANES)
        for b in order:
            rows_r[b], rows_i[b] = c_r, c_i
            e_r, e_i = ends[2 * m][b:b + 1], ends[2 * m + 1][b:b + 1]
            c_r, c_i = ljr * c_r - lji * c_i + e_r, ljr * c_i + lji * c_r + e_i
        car_ref[cr, 0:1, :] = c_r
        car_ref[ci, 0:1, :] = c_i
        entry.append((jnp.concatenate(rows_r, axis=0), jnp.concatenate(rows_i, axis=0)))

    def pass2(jj, carry):
        j, rows = step_rows(jj, row0)
        decayed, acc = carry[:2 * nm], carry[2 * nm:]
        out_d, out_a = [], []
        for m, (cr, ci, cl) in enumerate(cols):
            lr, li = lam[m]
            dr, di = decayed[2 * m], decayed[2 * m + 1]
            dr, di = lr * dr - li * di, lr * di + li * dr
            nr = s_ref[cr, rows, :] + dr
            ni = s_ref[ci, rows, :] + di
            s_ref[cr, rows, :] = nr
            s_ref[ci, rows, :] = ni
            out_d += [dr, di]
            if prev_ref is not None:
                prow = pl.ds(pl.multiple_of(prev_row0 + (j - 1) * SUBLANES, SUBLANES), SUBLANES)
                qr = prev_ref[cr, prow, :]
                qi = prev_ref[ci, prow, :]
                out_a += [acc[2 * m] + (nr * qr + ni * qi), acc[2 * m + 1] + (ni * qr - nr * qi)]
        return tuple(out_d) + tuple(out_a)

    n_acc = 2 * nm if prev_ref is not None else 0
    init = tuple(e for pair in entry for e in pair) + tuple(jnp.zeros(full, F32) for _ in range(n_acc))
    accs = lax.fori_loop(0, steps // SCAN_UNROLL, unrolled(pass2), init)[2 * nm:]
    if prev_ref is not None:
        for m, (_, _, cl) in enumerate(cols):
            glam_ref[0:1, pl.ds(cl, SLAB)] += jnp.sum(accs[2 * m], axis=0, keepdims=True)
            glam_ref[1:2, pl.ds(cl, SLAB)] += jnp.sum(accs[2 * m + 1], axis=0, keepdims=True)


def _permute_rows_f32(perm_bf16, v):
    hi = v.astype(BF16)
    lo = (v - hi.astype(F32)).astype(BF16)
    return (jnp.dot(perm_bf16, hi, preferred_element_type=F32) + jnp.dot(perm_bf16, lo, preferred_element_type=F32))


def _put_slab(s_ref, rows, q, val):
    per = 2 * SLAB_STATES // SLAB
    for i in range(per):
        s_ref[q * per + i, rows, :] = val[:, i * SLAB:(i + 1) * SLAB]


def _get_slab(s_ref, rows, q):
    per = 2 * SLAB_STATES // SLAB
    return jnp.concatenate([s_ref[q * per + i, rows, :] for i in range(per)], axis=1)


def _step_major_perm(tt):
    r = jnp.arange(tt)
    held = (r % SUBLANES) * (tt // SUBLANES) + r // SUBLANES
    return held[:, None] == r[None, :]


def _ssm_fwd(proj, u_block, bq, cq, lam, pw, d_skip):
    t = proj.shape[0]
    nq, ds, w2 = bq.shape
    assert ds == SLAB and w2 == 2 * SLAB_STATES
    dssm = nq * SLAB
    width = nq * w2
    ntile = width // SLAB
    tt = min(T_SCAN, t)
    steps = tt // SUBLANES
    perm = _step_major_perm(tt)
    pm, pmt = perm.astype(BF16), perm.T.astype(BF16)

    def body(u_ref, pm_ref, pmt_ref, bq_ref, cq_ref, lam_ref, pw_ref, d_ref, y_ref, cin_ref, s_ref, car_ref):
        @pl.when(pl.program_id(0) == 0)
        def _():
            car_ref[...] = jnp.zeros_like(car_ref)

        cin_ref[...] = jnp.broadcast_to(car_ref[:, 0:1, :], cin_ref.shape)
        u = u_ref[...]
        ub = jnp.dot(pm_ref[...], u.astype(BF16), preferred_element_type=F32).astype(BF16)
        everything = slice(None)
        for q in range(nq):
            _put_slab(s_ref, everything, q,
                      jnp.dot(ub[:, q * SLAB:(q + 1) * SLAB], bq_ref[q], preferred_element_type=F32))
        for q in range(nq):
            _scan_slab(s_ref, 0, q, lam_ref, pw_ref, car_ref, steps, reverse=False)
        y_sm = jnp.concatenate(
            [jnp.dot(_get_slab(s_ref, everything, q).astype(BF16), cq_ref[q], preferred_element_type=F32)
             for q in range(nq)], axis=1)
        y_ref[...] = _permute_rows_f32(pmt_ref[...], y_sm) + d_ref[...] * u

    c3 = lambda i: (0, 0, 0)
    c2 = lambda i: (0, 0)
    return pl.pallas_call(
        body, name="ssm_fwd", grid=(t // tt,),
        in_specs=[pl.BlockSpec((tt, dssm), lambda i: (i, u_block)), pl.BlockSpec((tt, tt), c2),
                  pl.BlockSpec((tt, tt), c2), pl.BlockSpec(bq.shape, c3),
                  pl.BlockSpec(cq.shape, c3), pl.BlockSpec(lam.shape, c2), pl.BlockSpec(pw.shape, c2),
                  pl.BlockSpec((1, dssm), c2)],
        out_specs=[pl.BlockSpec((tt, dssm), lambda i: (i, 0)),
                   pl.BlockSpec((None, ntile, SUBLANES, SLAB), lambda i: (i, 0, 0, 0))],
        out_shape=[jax.ShapeDtypeStruct((t, dssm), F32), jax.ShapeDtypeStruct((t // tt, ntile, SUBLANES, SLAB), F32)],
        scratch_shapes=[pltpu.VMEM((ntile, tt, SLAB), F32), pltpu.VMEM((ntile, SUBLANES, SLAB), F32)],
        compiler_params=_cp(2 * (8 * tt * dssm + 4 * nq * ds * w2) + 4 * tt * width + 16 * MIB,
                            dimension_semantics=("arbitrary",)),
    )(proj, pm, pmt, bq, cq, lam, pw, d_skip)


def _ssm_bwd(proj, u_block, dy, cin, dproj, bq, cq, lam, pw, d_skip):
    t = proj.shape[0]
    nq, ds, w2 = bq.shape
    dssm = nq * SLAB
    width = nq * w2
    ntile = width // SLAB
    tt = min(T_SCAN, t)
    nt = t // tt
    steps = tt // SUBLANES
    halo = SUBLANES
    perm = _step_major_perm(tt)
    pm, pmt = perm.astype(BF16), perm.T.astype(BF16)

    def body(u_ref, dy_ref, cin_ref, dp_any, pm_ref, pmt_ref, bq_ref, cq_ref, lam_ref, pw_ref, d_ref,
             du_ref, gb_ref, gc_ref, glam_ref, gd_ref, s_ref, gs_ref, car_f, car_b):
        del dp_any

        @pl.when(pl.program_id(0) == 0)
        def _():
            car_b[...] = jnp.zeros_like(car_b)
            gb_ref[...] = jnp.zeros_like(gb_ref)
            gc_ref[...] = jnp.zeros_like(gc_ref)
            glam_ref[...] = jnp.zeros_like(glam_ref)
            gd_ref[...] = jnp.zeros_like(gd_ref)

        u = u_ref[...]
        dyv = dy_ref[...]
        gd_ref[0:1, :] += jnp.sum(dyv * u, axis=0, keepdims=True)
        pmv = pm_ref[...]
        ub = jnp.dot(pmv, u.astype(BF16), preferred_element_type=F32).astype(BF16)
        dyb = jnp.dot(pmv, dyv.astype(BF16), preferred_element_type=F32).astype(BF16)
        car_f[...] = cin_ref[...]
        data = slice(halo, halo + tt)
        everything = slice(None)
        for q in range(nq):
            _put_slab(s_ref, data, q, jnp.dot(ub[:, q * SLAB:(q + 1) * SLAB], bq_ref[q], preferred_element_type=F32))
        for q in range(nq):
            _scan_slab(s_ref, halo, q, lam_ref, pw_ref, car_f, steps, reverse=False)
        last_step = s_ref[:, halo + tt - SUBLANES:halo + tt, :]
        s_ref[:, 0:halo, :] = jnp.concatenate([cin_ref[:, 0:1, :], last_step[:, 0:SUBLANES - 1, :]], axis=1)
        tn = (((0,), (0,)), ((), ()))
        nt_dims = (((1,), (1,)), ((), ()))
        for q in range(nq):
            sl = slice(q * SLAB, (q + 1) * SLAB)
            gc_ref[q] += lax.dot_general(dyb[:, sl], _get_slab(s_ref, data, q).astype(BF16), tn,
                                         preferred_element_type=F32)
            _put_slab(gs_ref, everything, q,
                      lax.dot_general(dyb[:, sl], cq_ref[q], nt_dims, preferred_element_type=F32))
        for q in range(nq):
            _scan_slab(gs_ref, 0, q, lam_ref, pw_ref, car_b, steps, reverse=True,
                       prev_ref=s_ref, prev_row0=halo, glam_ref=glam_ref)
        du_parts = []
        for q in range(nq):
            sl = slice(q * SLAB, (q + 1) * SLAB)
            gsb = _get_slab(gs_ref, everything, q).astype(BF16)
            du_parts.append(lax.dot_general(gsb, bq_ref[q], nt_dims, preferred_element_type=F32))
            gb_ref[q] += lax.dot_general(ub[:, sl], gsb, tn, preferred_element_type=F32)
        du_sm = jnp.concatenate(du_parts, axis=1)
        du_ref[...] = (_permute_rows_f32(pmt_ref[...], du_sm) + dyv * d_ref[...]).astype(BF16)

    c3 = lambda i: (0, 0, 0)
    c2 = lambda i: (0, 0)
    rev = lambda i: (nt - 1 - i, 0)
    dense = jax.ShapeDtypeStruct((nq, SLAB, w2), F32)
    gp = lam.shape[1]
    return pl.pallas_call(
        body, name="ssm_bwd", grid=(nt,),
        in_specs=[pl.BlockSpec((tt, dssm), lambda i: (nt - 1 - i, u_block)), pl.BlockSpec((tt, dssm), rev),
                  pl.BlockSpec((None, ntile, SUBLANES, SLAB), lambda i: (nt - 1 - i, 0, 0, 0)),
                  pl.BlockSpec(memory_space=pl.ANY), pl.BlockSpec((tt, tt), c2), pl.BlockSpec((tt, tt), c2),
                  pl.BlockSpec(bq.shape, c3), pl.BlockSpec(cq.shape, c3),
                  pl.BlockSpec(lam.shape, c2), pl.BlockSpec(pw.shape, c2), pl.BlockSpec((1, dssm), c2)],
        out_specs=[pl.BlockSpec((tt, dssm), lambda i: (nt - 1 - i, u_block)), pl.BlockSpec(dense.shape, c3),
                   pl.BlockSpec(dense.shape, c3), pl.BlockSpec((SUBLANES, gp), c2), pl.BlockSpec((SUBLANES, dssm), c2)],
        out_shape=[jax.ShapeDtypeStruct(dproj.shape, dproj.dtype), dense, dense,
                   jax.ShapeDtypeStruct((SUBLANES, gp), F32), jax.ShapeDtypeStruct((SUBLANES, dssm), F32)],
        scratch_shapes=[pltpu.VMEM((ntile, tt + halo, SLAB), F32), pltpu.VMEM((ntile, tt, SLAB), F32),
                        pltpu.VMEM((ntile, SUBLANES, SLAB), F32), pltpu.VMEM((ntile, SUBLANES, SLAB), F32)],
        input_output_aliases={3: 0},
        compiler_params=_cp(2 * (10 * tt * dssm + 4 * nq * ds * w2 + 8 * nq * SLAB * w2)
                            + 8 * tt * width + 12 * MIB, dimension_semantics=("arbitrary",)),
    )(proj, dy, cin, dproj, pm, pmt, bq, cq, lam, pw, d_skip)


def _gate_fwd(proj, y, conv_w, conv_b, w_glu, b_glu, dc):
    t = proj.shape[0]
    dssm = y.shape[1]
    assert dc == dssm
    tm = min(TM_GATE, t)
    halo = SUBLANES

    def body(b_ref, c_ref, v_ref, zc_ref, zs_ref, y_ref, cw_ref, cb_ref, wg_ref, bg_ref, mix_ref, cv_buf):
        @pl.when(pl.program_id(0) == 0)
        def _():
            cv_buf[0:halo, :] = jnp.zeros((halo, dc), F32)

        cv = c_ref[...] * v_ref[...]
        cv_buf[halo:, :] = cv
        conv = (cb_ref[...] + cw_ref[2:3, :] * cv + cw_ref[1:2, :] * cv_buf[halo - 1:halo - 1 + tm, :]
                + cw_ref[0:1, :] * cv_buf[halo - 2:halo - 2 + tm, :])
        sz, _ = _silu(zc_ref[...])
        mix_ref[:, 0:dc] = (b_ref[...] * conv * sz).astype(BF16)
        cv_buf[0:halo, :] = cv_buf[tm:tm + halo, :]
        ge, _ = _gelu(y_ref[...])
        gl = jnp.dot(ge.astype(BF16), wg_ref[...], preferred_element_type=F32) + bg_ref[...]
        szs, _ = _silu(zs_ref[...])
        mix_ref[:, dc:] = (ge * jax.nn.sigmoid(gl) * szs).astype(BF16)

    col = lambda j: pl.BlockSpec((tm, dc), lambda i, j=j: (i, j))
    fixed = lambda i: (0, 0)
    return pl.pallas_call(
        body, name="gate_fwd", grid=(t // tm,),
        in_specs=[col(0), col(1), col(2), col(3), col(5), pl.BlockSpec((tm, dssm), lambda i: (i, 0)),
                  pl.BlockSpec(conv_w.shape, fixed), pl.BlockSpec((1, dc), fixed),
                  pl.BlockSpec(w_glu.shape, fixed), pl.BlockSpec((1, dssm), fixed)],
        out_specs=pl.BlockSpec((tm, dc + dssm), lambda i: (i, 0)),
        out_shape=jax.ShapeDtypeStruct((t, dc + dssm), BF16),
        scratch_shapes=[pltpu.VMEM((tm + halo, dc), F32)],
        compiler_params=_cp(2 * (6 * 4 * tm * dc + 2 * tm * (dc + dssm) + 2 * dssm * dssm) + 24 * tm * dc + 8 * MIB,
                            dimension_semantics=("arbitrary",)),
    )(proj, proj, proj, proj, proj, y, conv_w, conv_b, w_glu, b_glu)


def _gate_bwd(proj, y, dmix, conv_w, conv_b, w_glu, b_glu, dc):
    t = proj.shape[0]
    dssm = y.shape[1]
    tm = min(TM_GATE, t)
    nt = t // tm
    halo = SUBLANES
    blocks_per_tile = tm // halo

    def body(b_ref, c_ref, v_ref, zc_ref, zs_ref, cp_ref, vp_ref, y_ref, dm_ref, cw_ref, cb_ref, wg_ref, bg_ref,
             dp_ref, dy_ref, gs_ref, gwg_ref, cv_buf, dc_buf):
        i = pl.program_id(0)

        @pl.when(i == 0)
        def _():
            dc_buf[tm:, :] = jnp.zeros((halo, dc), F32)
            gs_ref[...] = jnp.zeros_like(gs_ref)
            gwg_ref[...] = jnp.zeros_like(gwg_ref)

        first_tile = (i == nt - 1)
        bv, cg, vv, zc = b_ref[...], c_ref[...], v_ref[...], zc_ref[...]
        cv = cg * vv
        cv_buf[0:halo, :] = jnp.where(first_tile, 0.0, cp_ref[...] * vp_ref[...])
        cv_buf[halo:, :] = cv
        w0, w1, w2 = cw_ref[0:1, :], cw_ref[1:2, :], cw_ref[2:3, :]
        conv = (cb_ref[...] + w2 * cv + w1 * cv_buf[halo - 1:halo - 1 + tm, :]
                + w0 * cv_buf[halo - 2:halo - 2 + tm, :])
        sz, sgc = _silu(zc)
        dyc = dm_ref[:, 0:dc]
        dp_ref[:, 0:dc] = (dyc * conv * sz).astype(BF16)
        dp_ref[:, 3 * dc:4 * dc] = (dyc * bv * conv * _dsilu(zc, sgc)).astype(BF16)
        dconv = dyc * bv * sz
        dc_buf[0:tm, :] = dconv
        d1 = dc_buf[1:1 + tm, :]
        d2 = dc_buf[2:2 + tm, :]
        dcv = w2 * dconv + w1 * d1 + w0 * d2
        dp_ref[:, dc:2 * dc] = (dcv * vv).astype(BF16)
        dp_ref[:, 2 * dc:3 * dc] = (dcv * cg).astype(BF16)
        gs_ref[0:1, :] += jnp.sum(cv * d2, axis=0, keepdims=True)
        gs_ref[1:2, :] += jnp.sum(cv * d1, axis=0, keepdims=True)
        gs_ref[2:3, :] += jnp.sum(cv * dconv, axis=0, keepdims=True)
        gs_ref[3:4, :] += jnp.sum(dconv, axis=0, keepdims=True)
        dc_buf[tm:, :] = dc_buf[0:halo, :]

        yv, zs = y_ref[...], zs_ref[...]
        ge, th = _gelu(yv)
        geb = ge.astype(BF16)
        wg = wg_ref[...]
        gl = jnp.dot(geb, wg, preferred_element_type=F32) + bg_ref[...]
        sg = jax.nn.sigmoid(gl)
        szs, sgs = _silu(zs)
        dys = dm_ref[:, dc:]
        ys = ge * sg
        dp_ref[:, 4 * dc:5 * dc] = jnp.zeros((tm, dc), BF16)
        dp_ref[:, 5 * dc:] = (dys * ys * _dsilu(zs, sgs)).astype(BF16)
        d_ys = dys * szs
        dgl = d_ys * ge * sg * (1.0 - sg)
        dglb = dgl.astype(BF16)
        gs_ref[4:5, :] += jnp.sum(dgl, axis=0, keepdims=True)
        gwg_ref[...] += lax.dot_general(geb, dglb, (((0,), (0,)), ((), ())), preferred_element_type=F32)
        dge = d_ys * sg + lax.dot_general(dglb, wg, (((1,), (1,)), ((), ())), preferred_element_type=F32)
        dy_ref[...] = dge * _dgelu(yv, th)

    col = lambda j: pl.BlockSpec((tm, dc), lambda i, j=j: (nt - 1 - i, j))
    prev = lambda j: pl.BlockSpec((halo, dc), lambda i, j=j: (jnp.maximum((nt - 1 - i) * blocks_per_tile - 1, 0), j))
    rev = lambda i: (nt - 1 - i, 0)
    fixed = lambda i: (0, 0)
    return pl.pallas_call(
        body, name="gate_bwd", grid=(nt,),
        in_specs=[col(0), col(1), col(2), col(3), col(5), prev(1), prev(2), pl.BlockSpec((tm, dssm), rev),
                  pl.BlockSpec((tm, dc + dssm), rev), pl.BlockSpec(conv_w.shape, fixed), pl.BlockSpec((1, dc), fixed),
                  pl.BlockSpec(w_glu.shape, fixed), pl.BlockSpec((1, dssm), fixed)],
        out_specs=[pl.BlockSpec((tm, 6 * dc), rev), pl.BlockSpec((tm, dssm), rev),
                   pl.BlockSpec((2 * SUBLANES, dc), fixed), pl.BlockSpec((dssm, dssm), fixed)],
        out_shape=[jax.ShapeDtypeStruct((t, 6 * dc), BF16), jax.ShapeDtypeStruct((t, dssm), F32),
                   jax.ShapeDtypeStruct((2 * SUBLANES, dc), F32), jax.ShapeDtypeStruct((dssm, dssm), F32)],
        scratch_shapes=[pltpu.VMEM((tm + halo, dc), F32), pltpu.VMEM((tm + halo, dc), F32)],
        compiler_params=_cp(2 * (6 * 4 * tm * dc + 8 * tm * dc + 12 * tm * dc + 4 * tm * dc + 6 * dssm * dssm)
                            + 40 * tm * dc + 8 * MIB, dimension_semantics=("arbitrary",)),
    )(proj, proj, proj, proj, proj, proj, proj, y, dmix, conv_w, conv_b, w_glu, b_glu)


def _allgather_halves(parts, name):
    n = len(parts)
    per = 2 * (N_CHIPS - 1)

    def body(*refs):
        ins, outs = refs[:n], refs[n:2 * n]
        send, recv, local = refs[2 * n:]
        x, y, c = _my_place()
        k = 2 * x + y
        sib = (x, y, 1 - c)
        chips = _other_chips(x, y)

        def rc(t, s, src, dst, to):
            return pltpu.make_async_remote_copy(src_ref=src, dst_ref=dst, send_sem=send.at[t * per + s],
                                                recv_sem=recv.at[t * per + s], device_id=to, device_id_type=MESH)

        mine = [pltpu.make_async_copy(ins[t], outs[t].at[k], local.at[t]) for t in range(n)]
        for cp in mine:
            cp.start()
        sent = []
        for t in range(n):
            for j, (cx, cy) in enumerate(chips):
                cp = rc(t, j, ins[t].at[c], outs[t].at[k, c], (cx, cy, c))
                cp.start()
                sent.append(cp)
        for t in range(n):
            for j, (cx, cy) in enumerate(chips):
                landed = outs[t].at[2 * cx + cy, c]
                rc(t, j, landed, landed, (cx, cy, c)).wait_recv()
                cp = rc(t, N_CHIPS - 1 + j, landed, landed, sib)
                cp.start()
                sent.append(cp)
        for t in range(n):
            for j, (cx, cy) in enumerate(chips):
                other = outs[t].at[2 * cx + cy, 1 - c]
                rc(t, N_CHIPS - 1 + j, other, other, sib).wait_recv()
        for cp in sent:
            cp.wait_send()
        for cp in mine:
            cp.wait()

    any_spec = pl.BlockSpec(memory_space=pl.ANY)
    return pl.pallas_call(
        body, name=name, in_specs=[any_spec] * n, out_specs=[any_spec] * n,
        out_shape=[jax.ShapeDtypeStruct((N_CHIPS,) + p.shape, p.dtype) for p in parts],
        scratch_shapes=[pltpu.SemaphoreType.DMA((n * per,)), pltpu.SemaphoreType.DMA((n * per,)),
                        pltpu.SemaphoreType.DMA((n,))],
        compiler_params=_cp(16 * MIB),
    )(*parts)


def _allgather_flat(v, name):
    r, c = v.shape
    rels = [(dx, dy, dc) for dx in (0, 1) for dy in (0, 1) for dc in (0, 1)][1:]

    def body(v_ref, out_ref, send, recv):
        x, y, cc = _my_place()
        me = 4 * x + 2 * y + cc

        def peer(rel):
            dx, dy, dc = rel
            return (1 - x if dx else x, 1 - y if dy else y, 1 - cc if dc else cc)

        def rc(s, slot, to):
            return pltpu.make_async_remote_copy(src_ref=v_ref, dst_ref=out_ref.at[slot], send_sem=send.at[s],
                                                recv_sem=recv.at[s], device_id=to, device_id_type=MESH)

        sent = []
        for s, rel in enumerate(rels):
            cp = rc(s, me, peer(rel))
            cp.start()
            sent.append(cp)
        out_ref[me] = v_ref[...]
        for s, rel in enumerate(rels):
            px, py, pc = peer(rel)
            rc(s, 4 * px + 2 * py + pc, (px, py, pc)).wait_recv()
        for cp in sent:
            cp.wait_send()

    return pl.pallas_call(
        body, name=name, out_shape=jax.ShapeDtypeStruct((N_DEV, r, c), F32),
        in_specs=[pl.BlockSpec(memory_space=pltpu.VMEM)], out_specs=pl.BlockSpec(memory_space=pltpu.VMEM),
        scratch_shapes=[pltpu.SemaphoreType.DMA((N_DEV - 1,)), pltpu.SemaphoreType.DMA((N_DEV - 1,))],
        compiler_params=_cp((N_DEV + 2) * 4 * r * c + 8 * MIB),
    )(v)


def _rs_pair_exchange(grads, name):
    n = len(grads)

    def body(*refs):
        ins, outs = refs[:n], refs[n:2 * n]
        send, recv = refs[2 * n:]
        x, y, c = _my_place()
        cps = []
        for t in range(n):
            cp = pltpu.make_async_remote_copy(src_ref=ins[t].at[:, 1 - c], dst_ref=outs[t], send_sem=send.at[t],
                                              recv_sem=recv.at[t], device_id=(x, y, 1 - c), device_id_type=MESH)
            cp.start()
            cps.append(cp)
        for cp in cps:
            cp.wait()

    any_spec = pl.BlockSpec(memory_space=pl.ANY)
    return pl.pallas_call(
        body, name=name, in_specs=[any_spec] * n, out_specs=[any_spec] * n,
        out_shape=[jax.ShapeDtypeStruct((g.shape[0],) + g.shape[2:], F32) for g in grads],
        scratch_shapes=[pltpu.SemaphoreType.DMA((n,)), pltpu.SemaphoreType.DMA((n,))],
        compiler_params=_cp(16 * MIB),
    )(*grads)


def _rs_pair_add(place, grad, got, name):
    nk, _, r2, c = grad.shape
    tr = min(TR_ELT, r2)

    def body(place_ref, g_ref, r_ref, o16_ref, o32_ref):
        del place_ref
        s = g_ref[...] + r_ref[...]
        o32_ref[...] = s
        o16_ref[...] = s.astype(BF16)

    blk = pl.BlockSpec((None, tr, c), lambda k, i, p: (k, i, 0))
    return pl.pallas_call(
        body, name=name,
        grid_spec=pltpu.PrefetchScalarGridSpec(
            num_scalar_prefetch=1, grid=(nk, r2 // tr),
            in_specs=[pl.BlockSpec((None, None, tr, c), lambda k, i, p: (k, p[0], i, 0)), blk],
            out_specs=[blk, blk]),
        out_shape=[jax.ShapeDtypeStruct((nk, r2, c), BF16), jax.ShapeDtypeStruct((nk, r2, c), F32)],
        compiler_params=_cp(2 * 14 * tr * c + 8 * MIB),
    )(place, grad, got)


_HBM_SPEC = pl.BlockSpec(memory_space=pltpu.HBM)
_SEM_SPEC = pl.BlockSpec(memory_space=pltpu.SEMAPHORE)
_DATAFLOW = pltpu.SideEffectType.DATAFLOW_SIDE_EFFECTING


def _split_copy_start(srcs, land_shapes, plan, n_sems, name):
    ns, nl = len(srcs), len(land_shapes)

    def body(*refs):
        src_refs, land_refs = refs[:ns], refs[ns:ns + nl]
        send, recv = refs[ns + nl], refs[ns + nl + 1]
        token = refs[-1]
        sends, _ = plan(src_refs, land_refs)
        for src, dst, to, si, ri in sends:
            pltpu.make_async_remote_copy(src_ref=src, dst_ref=dst, send_sem=send.at[si], recv_sem=recv.at[ri],
                                         device_id=to, device_id_type=MESH).start()
        token[...] = jnp.zeros_like(token)

    lands = [lax.empty(shp, dt) for shp, dt in land_shapes]
    through = [pltpu.HBM(a.shape, a.dtype) for a in srcs] + [pltpu.HBM(shp, dt) for shp, dt in land_shapes]
    out = pl.pallas_call(
        body, name=name,
        out_shape=(pltpu.SemaphoreType.DMA((n_sems,)), pltpu.SemaphoreType.DMA((n_sems,)), *through,
                   jax.ShapeDtypeStruct((SUBLANES, SLAB), F32)),
        in_specs=[_HBM_SPEC] * (ns + nl),
        out_specs=(_SEM_SPEC, _SEM_SPEC, *([_HBM_SPEC] * (ns + nl)), pl.BlockSpec(memory_space=pltpu.VMEM)),
        input_output_aliases={i: 2 + i for i in range(ns + nl)},
        compiler_params=pltpu.CompilerParams(has_side_effects=_DATAFLOW),
    )(*[pltpu.with_memory_space_constraint(a, pltpu.HBM) for a in (*srcs, *lands)])
    return out[0], out[1], list(out[2:2 + ns]), list(out[2 + ns:2 + ns + nl]), out[-1]


def _split_copy_wait(send, recv, srcs, lands, plan, after, name):
    ns, nl, na = len(srcs), len(lands), len(after)

    def body(*refs):
        src_refs, land_refs = refs[:ns], refs[ns:ns + nl]
        send_ref, recv_ref = refs[ns + nl], refs[ns + nl + 1]
        sends, arrivals = plan(src_refs, land_refs)
        for src, dst, to, si, ri in sends:
            pltpu.make_async_remote_copy(src_ref=src, dst_ref=dst, send_sem=send_ref.at[si], recv_sem=recv_ref.at[ri],
                                         device_id=to, device_id_type=MESH).wait_send()
        for (src, _, to, si, _), (view, ri) in zip(sends, arrivals):
            pltpu.make_async_remote_copy(src_ref=view, dst_ref=view, send_sem=send_ref.at[si], recv_sem=recv_ref.at[ri],
                                         device_id=to, device_id_type=MESH).wait_recv()

    out = pl.pallas_call(
        body, name=name,
        out_shape=[pltpu.HBM(a.shape, a.dtype) for a in (*srcs, *lands)],
        in_specs=[_HBM_SPEC] * (ns + nl) + [_SEM_SPEC, _SEM_SPEC] + [pl.BlockSpec(memory_space=pl.ANY)] * na,
        out_specs=[_HBM_SPEC] * (ns + nl),
        input_output_aliases={i: i for i in range(ns + nl)},
        compiler_params=pltpu.CompilerParams(has_side_effects=_DATAFLOW),
    )(*srcs, *lands, send, recv, *after)
    return list(out[:ns]), list(out[ns:])


def _chip_exchange_plan(n):
    per = N_CHIPS - 1

    def plan(srcs, lands):
        x, y, c = _my_place()
        sends, arrivals = [], []
        for t in range(n):
            for j, (cx, cy) in enumerate(_other_chips(x, y)):
                sends.append((srcs[t].at[2 * cx + cy], lands[t].at[j], (cx, cy, c), t * per + j, t * per + j))
                arrivals.append((lands[t].at[j], t * per + j))
        return sends, arrivals

    return plan


def _gather_direct_plan(n):
    per = 2 * (N_CHIPS - 1)

    def plan(srcs, lands):
        x, y, c = _my_place()
        k = 2 * x + y
        sends, arrivals = [], []
        for t in range(n):
            for j, (cx, cy) in enumerate(_other_chips(x, y)):
                for core in (0, 1):
                    sends.append((srcs[t].at[c], lands[t].at[k, c], (cx, cy, core),
                                  t * per + 2 * j + core, t * per + 2 * j + c))
                    arrivals.append((lands[t].at[2 * cx + cy, core], t * per + 2 * j + core))
        return sends, arrivals

    return plan


def _rs_chip_add(place, sums32, got, name):
    _, r2, c = sums32.shape
    tr = min(TR_ELT, r2)

    def body(place_ref, s_ref, q_ref, o_ref):
        del place_ref
        o_ref[...] = ((s_ref[...] + q_ref[0].astype(F32)) + q_ref[1].astype(F32)) + q_ref[2].astype(F32)

    return pl.pallas_call(
        body, name=name,
        grid_spec=pltpu.PrefetchScalarGridSpec(
            num_scalar_prefetch=1, grid=(r2 // tr,),
            in_specs=[pl.BlockSpec((None, tr, c), lambda i, p: (p[1], i, 0)),
                      pl.BlockSpec((N_CHIPS - 1, tr, c), lambda i, p: (0, i, 0))],
            out_specs=pl.BlockSpec((tr, c), lambda i, p: (i, 0))),
        out_shape=jax.ShapeDtypeStruct((r2, c), F32),
        compiler_params=_cp(2 * 14 * tr * c + 8 * MIB),
    )(place, sums32, got)


def _rs_pair_share(halves, name):
    n = len(halves)

    def body(*refs):
        ins, outs = refs[:n], refs[n:2 * n]
        send, recv, local = refs[2 * n:]
        x, y, c = _my_place()
        cps, mine = [], []
        for t in range(n):
            lc = pltpu.make_async_copy(ins[t], outs[t].at[c], local.at[t])
            lc.start()
            mine.append(lc)
            cp = pltpu.make_async_remote_copy(src_ref=ins[t], dst_ref=outs[t].at[c], send_sem=send.at[t],
                                              recv_sem=recv.at[t], device_id=(x, y, 1 - c), device_id_type=MESH)
            cp.start()
            cps.append(cp)
        for t in range(n):
            other = outs[t].at[1 - c]
            pltpu.make_async_remote_copy(src_ref=other, dst_ref=other, send_sem=send.at[t], recv_sem=recv.at[t],
                                         device_id=(x, y, 1 - c), device_id_type=MESH).wait_recv()
        for cp in cps:
            cp.wait_send()
        for lc in mine:
            lc.wait()

    any_spec = pl.BlockSpec(memory_space=pl.ANY)
    return pl.pallas_call(
        body, name=name, in_specs=[any_spec] * n, out_specs=[any_spec] * n,
        out_shape=[jax.ShapeDtypeStruct((2,) + h.shape, F32) for h in halves],
        scratch_shapes=[pltpu.SemaphoreType.DMA((n,)), pltpu.SemaphoreType.DMA((n,)), pltpu.SemaphoreType.DMA((n,))],
        compiler_params=_cp(16 * MIB),
    )(*halves)


_PACK_TILE = SUBLANES * SLAB


def _pack(arrays):
    rows = []
    for a in arrays:
        flat = a.reshape(-1).astype(F32)
        padded = -(-flat.shape[0] // _PACK_TILE) * _PACK_TILE
        rows.append(jnp.pad(flat, (0, padded - flat.shape[0])).reshape(-1, SLAB))
    n_rows = sum(r.shape[0] for r in rows)
    if n_rows % (2 * SUBLANES):
        rows.append(jnp.zeros((SUBLANES, SLAB), F32))
    return jnp.concatenate(rows, axis=0)


def _unpack(packed, shapes):
    out, row = [], 0
    for shp in shapes:
        size = math.prod(shp)
        nrow = -(-size // _PACK_TILE) * SUBLANES
        out.append(packed[row:row + nrow].reshape(-1)[:size].reshape(shp))
        row += nrow
    return out


def kernel(x, norm_pre_g, w_in, conv_w, conv_b, ssm_a_re, ssm_a_im, ssm_log_dt, ssm_b_re, ssm_b_im, ssm_c_re, ssm_c_im, ssm_d, w_glu, b_glu, w_out, norm_post_g, loss_target, m_norm_pre_g, m_w_in, m_conv_w, m_conv_b, m_ssm_a_re, m_ssm_a_im, m_ssm_log_dt, m_ssm_b_re, m_ssm_b_im, m_ssm_c_re, m_ssm_c_im, m_ssm_d, m_w_glu, m_b_glu, m_w_out, m_norm_post_g, v_norm_pre_g, v_w_in, v_conv_w, v_conv_b, v_ssm_a_re, v_ssm_a_im, v_ssm_log_dt, v_ssm_b_re, v_ssm_b_im, v_ssm_c_re, v_ssm_c_im, v_ssm_d, v_w_glu, v_b_glu, v_w_out, v_norm_post_g):
    xs, tgt = x[0], loss_target[0]
    t, d = xs.shape
    dc = conv_b.shape[0]
    dssm = ssm_d.shape[0]
    g, p = ssm_a_re.shape
    h = SSM_H
    nq = dssm // SLAB
    n_shard = w_in.shape[1]
    steps = min(T_SCAN, t) // SUBLANES
    mx, my, mc = _my_place()
    chip = 2 * mx + my
    place = jnp.stack([mc, chip]).astype(jnp.int32)

    halves = lambda a: a.reshape(2, a.shape[0] // 2, a.shape[1])
    (win_g,) = _allgather_halves([halves(_cast_bf16(w_in, "cast_w_in"))], "allgather_w_in")
    win_b = win_g.reshape(N_CHIPS, d, n_shard)
    gathered = win_g[0, 0, :SUBLANES, :SLAB].astype(F32)
    side = [halves(_cast_bf16(w_out, "cast_w_out", gathered)), halves(_cast_bf16(w_glu, "cast_w_glu", gathered))]
    side_plan = _gather_direct_plan(len(side))
    side_sems = 2 * (N_CHIPS - 1) * len(side)
    ag_send, ag_recv, ag_srcs, ag_lands, ag_token = _split_copy_start(
        side, [((N_CHIPS,) + a.shape, BF16) for a in side], side_plan, side_sems, "gather_side_weights_start")

    cw_cols = conv_w.shape[1]
    cw_pad = -(-cw_cols // SLAB) * SLAB
    cw_blk = jnp.zeros((SUBLANES, cw_pad), F32).at[:conv_w.shape[0], :cw_cols].set(conv_w)
    cw_all = _allgather_flat(cw_blk, "allgather_conv_w")
    conv_w_full = jnp.concatenate([cw_all[2 * k, :, :cw_cols] for k in range(N_CHIPS)], axis=1)

    expand = jnp.repeat(jnp.eye(p, dtype=F32), h, axis=1)
    b_re2, b_im2 = ssm_b_re.reshape(g, p * h), ssm_b_im.reshape(g, p * h)
    log_dt2 = ssm_log_dt.reshape(g, 1)
    lbr, lbi, bbr2, bbi2, pw3 = _zoh_fwd(ssm_a_re, ssm_a_im, log_dt2, b_re2, b_im2, expand, steps)
    lam = jnp.stack([lbr.reshape(g * p), lbi.reshape(g * p)])
    pw = pw3.reshape(2, g * p)
    to_slab_b = lambda b2: _blockdiag(b2.reshape(nq, GROUPS_PER_SLAB, p, h).transpose(0, 1, 3, 2))
    bq = jnp.concatenate([to_slab_b(bbr2), to_slab_b(bbi2)], axis=2).astype(BF16)
    to_slab_c = lambda c3: _blockdiag(c3.reshape(nq, GROUPS_PER_SLAB, h, p).transpose(0, 1, 3, 2))
    cq = jnp.concatenate([to_slab_c(ssm_c_re), to_slab_c(-ssm_c_im)], axis=1).astype(BF16)

    g_pre2, g_post2 = norm_pre_g.reshape(1, d), norm_post_g.reshape(1, d)
    conv_b2, b_glu2, d_skip2 = conv_b.reshape(1, dc), b_glu.reshape(1, dssm), ssm_d.reshape(1, dssm)
    u_block = 4 * dc // dssm

    hb = _prenorm(xs, g_pre2 + ag_token[0:1, 0:1])
    proj = _matmul_nn(hb, win_b, "inproj")
    y, cin = _ssm_fwd(proj, u_block, bq, cq, lam, pw, d_skip2)
    side_own, side_all = _split_copy_wait(ag_send, ag_recv, ag_srcs, ag_lands, side_plan, [cin],
                                          "gather_side_weights_wait")
    wout_g, wglu_g = [lax.dynamic_update_index_in_dim(all_, own, chip, 0) for own, all_ in zip(side_own, side_all)]
    wout_b = wout_g.reshape(dc + dssm, d)
    wglu_b = wglu_g.reshape(dssm, dssm)
    mix = _gate_fwd(proj, y, conv_w_full, conv_b2, wglu_b, b_glu2, dc)
    loss_blk, dout, dob, dmix, gg_post = _outproj(mix, wout_b, xs, tgt, g_post2)

    def reduce_start(grads, tags, group):
        got = _rs_pair_exchange(grads, "rs_pair_exchange_" + group)
        sums = [_rs_pair_add(place, gt, rt, "rs_pair_add_" + tg) for gt, rt, tg in zip(grads, got, tags)]
        plan = _chip_exchange_plan(len(grads))
        started = _split_copy_start([s16 for s16, _ in sums],
                                    [((N_CHIPS - 1,) + s16.shape[1:], BF16) for s16, _ in sums], plan,
                                    (N_CHIPS - 1) * len(grads), "rs_chip_exchange_" + group + "_start")
        return plan, started, [s32 for _, s32 in sums]

    def reduce_finish(plan, started, sums32, tags, group, after):
        send, recv, srcs, lands, _ = started
        _, landed = _split_copy_wait(send, recv, srcs, lands, plan, after, "rs_chip_exchange_" + group + "_wait")
        mine = [_rs_chip_add(place, s32, qt, "rs_chip_add_" + tg) for s32, qt, tg in zip(sums32, landed, tags)]
        full = _rs_pair_share(mine, "rs_pair_share_" + group)
        return [f.reshape(2 * f.shape[1], f.shape[2]) for f in full]

    gw_out = _matmul_tn(mix, dob, 1, "grad_w_out")
    dproj, dy, gsmall, gw_glu = _gate_bwd(proj, y, dmix, conv_w_full, conv_b2, wglu_b, b_glu2, dc)
    rs_a = reduce_start([gw_out.reshape(N_CHIPS, 2, (dc + dssm) // (2 * N_CHIPS), d),
                         gw_glu.reshape(N_CHIPS, 2, dssm // (2 * N_CHIPS), dssm)], ["w_out", "w_glu"], "a")
    dproj, gb_dense, gc_dense, glam, gd = _ssm_bwd(proj, u_block, dy, cin, dproj, bq, cq, lam, pw,
                                                   d_skip2 + rs_a[1][4][0:1, 0:1])
    gw_in = _matmul_tn(hb, dproj, N_CHIPS, "grad_w_in")
    rs_b = reduce_start([gw_in.reshape(N_CHIPS, 2, d // 2, n_shard)], ["w_in"], "b")
    gx, gg_pre = _dh_prenorm_bwd(dproj, win_b, xs, dout, g_pre2 + rs_b[1][4][0:1, 0:1])

    gb4 = gb_dense.reshape(nq, SLAB, 2, SLAB_STATES)
    g_bbr2 = _blockdiag_take(gb4[:, :, 0, :], h, p).transpose(0, 1, 3, 2).reshape(g, p * h)
    g_bbi2 = _blockdiag_take(gb4[:, :, 1, :], h, p).transpose(0, 1, 3, 2).reshape(g, p * h)
    gc4 = gc_dense.reshape(nq, SLAB, 2, SLAB_STATES)
    g_c_re = _blockdiag_take(gc4[:, :, 0, :], h, p).reshape(g, h, p)
    g_c_im = -_blockdiag_take(gc4[:, :, 1, :], h, p).reshape(g, h, p)
    g_a_re, g_a_im, g_ld, g_b_re2, g_b_im2 = _zoh_bwd(
        ssm_a_re, ssm_a_im, log_dt2, b_re2, b_im2, expand,
        glam[0].reshape(g, p), glam[1].reshape(g, p), g_bbr2, g_bbi2)

    small_names = ["norm_pre_g", "conv_w", "conv_b", "ssm_a_re", "ssm_a_im", "ssm_log_dt", "ssm_b_re", "ssm_b_im",
                   "ssm_c_re", "ssm_c_im", "ssm_d", "b_glu", "norm_post_g", "loss"]
    small_g = {
        "norm_pre_g": gg_pre[0], "conv_w": gsmall[0:3], "conv_b": gsmall[3], "ssm_a_re": g_a_re, "ssm_a_im": g_a_im,
        "ssm_log_dt": g_ld.reshape(g), "ssm_b_re": g_b_re2.reshape(g, p, h), "ssm_b_im": g_b_im2.reshape(g, p, h),
        "ssm_c_re": g_c_re, "ssm_c_im": g_c_im, "ssm_d": gd[0], "b_glu": gsmall[4], "norm_post_g": gg_post[0],
        "loss": loss_blk[0, 0:1],
    }
    zeros_cw = jnp.zeros((conv_w.shape[0], dc), F32)
    one0 = jnp.zeros((1,), F32)
    small_w = dict(norm_pre_g=norm_pre_g, conv_w=zeros_cw, conv_b=conv_b, ssm_a_re=ssm_a_re, ssm_a_im=ssm_a_im,
                   ssm_log_dt=ssm_log_dt, ssm_b_re=ssm_b_re, ssm_b_im=ssm_b_im, ssm_c_re=ssm_c_re, ssm_c_im=ssm_c_im,
                   ssm_d=ssm_d, b_glu=b_glu, norm_post_g=norm_post_g, loss=one0)
    small_m = dict(norm_pre_g=m_norm_pre_g, conv_w=zeros_cw, conv_b=m_conv_b, ssm_a_re=m_ssm_a_re, ssm_a_im=m_ssm_a_im,
                   ssm_log_dt=m_ssm_log_dt, ssm_b_re=m_ssm_b_re, ssm_b_im=m_ssm_b_im, ssm_c_re=m_ssm_c_re,
                   ssm_c_im=m_ssm_c_im, ssm_d=m_ssm_d, b_glu=m_b_glu, norm_post_g=m_norm_post_g, loss=one0)
    small_v = dict(norm_pre_g=v_norm_pre_g, conv_w=zeros_cw, conv_b=v_conv_b, ssm_a_re=v_ssm_a_re, ssm_a_im=v_ssm_a_im,
                   ssm_log_dt=v_ssm_log_dt, ssm_b_re=v_ssm_b_re, ssm_b_im=v_ssm_b_im, ssm_c_re=v_ssm_c_re,
                   ssm_c_im=v_ssm_c_im, ssm_d=v_ssm_d, b_glu=v_b_glu, norm_post_g=v_norm_post_g, loss=one0)
    shapes = [small_w[nm].shape for nm in small_names]
    g_pack = _allreduce_small(_pack([small_g[nm] for nm in small_names]))
    packs = _adamw_small(g_pack, _pack([small_w[nm] for nm in small_names]),
                         _pack([small_m[nm] for nm in small_names]), _pack([small_v[nm] for nm in small_names]))
    sg, sd, sm, sv = [dict(zip(small_names, _unpack(pk, shapes))) for pk in (g_pack, *packs)]
    loss = sg["loss"][0]

    g_cw = lax.dynamic_slice_in_dim(sg["conv_w"], chip * cw_cols, cw_cols, axis=1)
    pad_cw = lambda a: jnp.zeros((SUBLANES, cw_pad), F32).at[:a.shape[0], :cw_cols].set(a)
    cut_cw = lambda a: a[:conv_w.shape[0], :cw_cols]
    d_cw, m_cw, v_cw = [cut_cw(a) for a in _adamw(pad_cw(conv_w), pad_cw(g_cw), pad_cw(m_conv_w), pad_cw(v_conv_w),
                                                  "adamw_conv_w")]

    g_wout, g_wglu = reduce_finish(*rs_a, ["w_out", "w_glu"], "a", [d_cw, packs[0]])
    (g_win,) = reduce_finish(*rs_b, ["w_in"], "b", [g_wout])
    d_win, m_win, v_win = _adamw(w_in, g_win, m_w_in, v_w_in, "adamw_w_in")
    d_wout, m_wout, v_wout = _adamw(w_out, g_wout, m_w_out, v_w_out, "adamw_w_out")
    d_wglu, m_wglu, v_wglu = _adamw(w_glu, g_wglu, m_w_glu, v_w_glu, "adamw_w_glu")

    order = ["norm_pre_g", "w_in", "conv_w", "conv_b", "ssm_a_re", "ssm_a_im", "ssm_log_dt", "ssm_b_re", "ssm_b_im",
             "ssm_c_re", "ssm_c_im", "ssm_d", "w_glu", "b_glu", "w_out", "norm_post_g"]
    grads, deltas, new_m, new_v = dict(sg), dict(sd), dict(sm), dict(sv)
    grads.update(w_in=g_win, w_out=g_wout, w_glu=g_wglu, conv_w=g_cw)
    deltas.update(w_in=d_win, w_out=d_wout, w_glu=d_wglu, conv_w=d_cw)
    new_m.update(w_in=m_win, w_out=m_wout, w_glu=m_wglu, conv_w=m_cw)
    new_v.update(w_in=v_win, w_out=v_wout, w_glu=v_wglu, conv_w=v_cw)
    return (loss, gx[None], *[grads[nm] for nm in order], *[deltas[nm] for nm in order],
            *[new_m[nm] for nm in order], *[new_v[nm] for nm in order])
```

```python
import functools
import math

import jax
import jax.numpy as jnp
from jax import lax
from jax.experimental import pallas as pl
from jax.experimental.pallas import tpu as pltpu

F32 = jnp.float32
BF16 = jnp.bfloat16
MESH = pl.DeviceIdType.MESH

EPS = 1e-6
SSM_H = 16
SSM_P = 64
GROUPS_PER_SLAB = 8
SLAB = 128
SLAB_STATES = GROUPS_PER_SLAB * SSM_P
N_CHIPS = 4
N_DEV = 8

ADAM_LR = 0.001
ADAM_B1 = 0.9
ADAM_B2 = 0.999
ADAM_EPS = 1e-08
ADAM_WD = 0.01
ADAM_STEP = 10

MIB = 1024 * 1024
VMEM_CAP = 48 * MIB
SUBLANES = 8

TM_NORM = 512
TM_PROJ = 512
TM_GATE = 256
TM_OUT = 256
TM_DH = 256
T_SCAN = 256
TK_TN = 512
SCAN_UNROLL = 4
TM_TN = 1024
TR_ELT = 256


def _cp(vmem_bytes, **kw):
    return pltpu.CompilerParams(vmem_limit_bytes=int(min(VMEM_CAP, max(16 * MIB, vmem_bytes))), **kw)


def _pallas_hbm(body, **kw):
    pin = lambda s: pltpu.HBM(s.shape, s.dtype) if isinstance(s, jax.ShapeDtypeStruct) else s
    out_shape = kw.pop("out_shape")
    out_shape = [pin(s) for s in out_shape] if isinstance(out_shape, (list, tuple)) else pin(out_shape)
    call = pl.pallas_call(body, out_shape=out_shape, **kw)

    def run(*args):
        return call(*[pltpu.with_memory_space_constraint(a, pltpu.HBM) if jnp.issubdtype(a.dtype, jnp.floating) else a
                      for a in args])

    return run


def _my_place():
    return lax.axis_index("x"), lax.axis_index("y"), lax.axis_index("c")


def _other_chips(x, y):
    return [(1 - x, y), (x, 1 - y), (1 - x, 1 - y)]


def _silu(z):
    s = jax.nn.sigmoid(z)
    return z * s, s


def _dsilu(z, s):
    return s * (1.0 + z * (1.0 - s))


_GELU_K = math.sqrt(2.0 / math.pi)
_GELU_C = 0.044715


def _gelu(y):
    th = jnp.tanh(_GELU_K * (y + _GELU_C * y * y * y))
    return 0.5 * y * (1.0 + th), th


def _dgelu(y, th):
    return 0.5 * (1.0 + th) + 0.5 * y * (1.0 - th * th) * _GELU_K * (1.0 + 3.0 * _GELU_C * y * y)


def _cast_bf16(w, name, after=None):
    r, c = w.shape
    tr = min(TR_ELT, r)
    extra = [] if after is None else [after]

    def body(w_ref, *rest):
        rest[-1][...] = w_ref[...].astype(BF16)

    return _pallas_hbm(
        body, name=name, grid=(r // tr,),
        in_specs=[pl.BlockSpec((tr, c), lambda i: (i, 0))] + [pl.BlockSpec((SUBLANES, SLAB), lambda i: (0, 0))] * len(extra),
        out_specs=pl.BlockSpec((tr, c), lambda i: (i, 0)),
        out_shape=jax.ShapeDtypeStruct((r, c), BF16),
        compiler_params=_cp(12 * tr * c),
    )(w, *extra)


def _prenorm(x, g):
    t, d = x.shape
    tm = min(TM_NORM, t)

    def body(x_ref, g_ref, h_ref):
        xv = x_ref[...]
        r = lax.rsqrt(jnp.mean(xv * xv, axis=-1, keepdims=True) + EPS)
        h_ref[...] = (xv * r * g_ref[...]).astype(BF16)

    return _pallas_hbm(
        body, name="prenorm", grid=(t // tm,),
        in_specs=[pl.BlockSpec((tm, d), lambda i: (i, 0)), pl.BlockSpec((1, d), lambda i: (0, 0))],
        out_specs=pl.BlockSpec((tm, d), lambda i: (i, 0)),
        out_shape=jax.ShapeDtypeStruct((t, d), BF16),
        compiler_params=_cp(20 * tm * d),
    )(x, g)


def _matmul_nn(a, b, name):
    t, k = a.shape
    nb, _, n = b.shape
    tm = min(TM_PROJ, t)

    def body(a_ref, b_ref, o_ref):
        o_ref[...] = jnp.dot(a_ref[...], b_ref[...], preferred_element_type=F32)

    return _pallas_hbm(
        body, name=name, grid=(nb, t // tm),
        in_specs=[pl.BlockSpec((tm, k), lambda j, i: (i, 0)), pl.BlockSpec((None, k, n), lambda j, i: (j, 0, 0))],
        out_specs=pl.BlockSpec((tm, n), lambda j, i: (i, j)),
        out_shape=jax.ShapeDtypeStruct((t, nb * n), F32),
        compiler_params=_cp(2 * (2 * tm * k + 2 * k * n + 4 * tm * n) + 4 * MIB),
    )(a, b)


def _matmul_tn(a, b, nb, name):
    t, m = a.shape
    n = b.shape[1] // nb
    tk = min(TK_TN, t)
    tma = min(TM_TN, m)

    def body(a_ref, b_ref, o_ref):
        @pl.when(pl.program_id(2) == 0)
        def _():
            o_ref[...] = jnp.zeros_like(o_ref)

        o_ref[...] += lax.dot_general(a_ref[...], b_ref[...], (((0,), (0,)), ((), ())), preferred_element_type=F32)

    return _pallas_hbm(
        body, name=name, grid=(nb, m // tma, t // tk),
        in_specs=[pl.BlockSpec((tk, tma), lambda j, i, k: (k, i)), pl.BlockSpec((tk, n), lambda j, i, k: (k, j))],
        out_specs=pl.BlockSpec((None, tma, n), lambda j, i, k: (j, i, 0)),
        out_shape=jax.ShapeDtypeStruct((nb, m, n), F32),
        compiler_params=_cp(2 * (2 * tk * tma + 2 * tk * n + 4 * tma * n) + 8 * MIB),
    )(a, b)


def _outproj(mix, w_out, x, tgt, g_post):
    t, dm = mix.shape
    d = w_out.shape[1]
    tm = min(TM_OUT, t)

    def body(mix_ref, w_ref, x_ref, t_ref, g_ref, loss_ref, dout_ref, do_ref, dmix_ref, gg_ref):
        @pl.when(pl.program_id(0) == 0)
        def _():
            loss_ref[...] = jnp.zeros_like(loss_ref)
            gg_ref[...] = jnp.zeros_like(gg_ref)

        w = w_ref[...]
        o = jnp.dot(mix_ref[...], w, preferred_element_type=F32)
        r = lax.rsqrt(jnp.mean(o * o, axis=-1, keepdims=True) + EPS)
        nh = o * r
        g = g_ref[...]
        e = x_ref[...] + nh * g - t_ref[...]
        loss_ref[...] += jnp.sum(e * e) * (0.5 / d)
        dout = e * (1.0 / d)
        dout_ref[...] = dout
        gg_ref[0:1, :] += jnp.sum(dout * nh, axis=0, keepdims=True)
        dn = dout * g
        do = r * (dn - nh * jnp.mean(dn * nh, axis=-1, keepdims=True))
        dob = do.astype(BF16)
        do_ref[...] = dob
        dmix_ref[...] = lax.dot_general(dob, w, (((1,), (1,)), ((), ())), preferred_element_type=F32)

    row = lambda i: (i, 0)
    fixed = lambda i: (0, 0)
    return _pallas_hbm(
        body, name="outproj", grid=(t // tm,),
        in_specs=[pl.BlockSpec((tm, dm), row), pl.BlockSpec((dm, d), fixed), pl.BlockSpec((tm, d), row),
                  pl.BlockSpec((tm, d), row), pl.BlockSpec((1, d), fixed)],
        out_specs=[pl.BlockSpec((SUBLANES, SLAB), fixed), pl.BlockSpec((tm, d), row), pl.BlockSpec((tm, d), row),
                   pl.BlockSpec((tm, dm), row), pl.BlockSpec((SUBLANES, d), fixed)],
        out_shape=[jax.ShapeDtypeStruct((SUBLANES, SLAB), F32), jax.ShapeDtypeStruct((t, d), F32),
                   jax.ShapeDtypeStruct((t, d), BF16), jax.ShapeDtypeStruct((t, dm), F32),
                   jax.ShapeDtypeStruct((SUBLANES, d), F32)],
        compiler_params=_cp(2 * (2 * dm * d + tm * (2 * dm + 4 * d * 3 + 2 * d + 4 * dm)) + 16 * MIB),
    )(mix, w_out, x, tgt, g_post)


def _dh_prenorm_bwd(dproj, w_in, x, dout, g_pre):
    t, d = x.shape
    nb, _, n = w_in.shape
    tm = min(TM_DH, t)

    def body(dp_ref, w_ref, x_ref, dout_ref, g_ref, dx_ref, gg_ref):
        @pl.when(pl.program_id(0) == 0)
        def _():
            gg_ref[...] = jnp.zeros_like(gg_ref)

        dh = None
        for k in range(nb):
            part = lax.dot_general(dp_ref[:, k * n:(k + 1) * n], w_ref[k], (((1,), (1,)), ((), ())),
                                   preferred_element_type=F32)
            dh = part if dh is None else dh + part
        xv = x_ref[...]
        r = lax.rsqrt(jnp.mean(xv * xv, axis=-1, keepdims=True) + EPS)
        xh = xv * r
        gg_ref[0:1, :] += jnp.sum(dh * xh, axis=0, keepdims=True)
        dg = dh * g_ref[...]
        dx_ref[...] = dout_ref[...] + r * (dg - xh * jnp.mean(dg * xh, axis=-1, keepdims=True))

    row = lambda i: (i, 0)
    fixed = lambda i: (0, 0)
    w_spec = pl.BlockSpec(w_in.shape, lambda i: (0, 0, 0), pipeline_mode=pl.Buffered(1))
    return _pallas_hbm(
        body, name="dh_prenorm_bwd", grid=(t // tm,),
        in_specs=[pl.BlockSpec((tm, nb * n), row), w_spec,
                  pl.BlockSpec((tm, d), row), pl.BlockSpec((tm, d), row), pl.BlockSpec((1, d), fixed)],
        out_specs=[pl.BlockSpec((tm, d), row), pl.BlockSpec((SUBLANES, d), fixed)],
        out_shape=[jax.ShapeDtypeStruct((t, d), F32), jax.ShapeDtypeStruct((SUBLANES, d), F32)],
        compiler_params=_cp(2 * nb * d * n + 2 * (2 * tm * nb * n + 12 * tm * d) + 16 * tm * d + 4 * MIB),
    )(dproj, w_in, x, dout, g_pre)


def _adamw(w, g, m, v, name):
    r, c = w.shape
    tr = min(TR_ELT, r)
    c1 = 1.0 - ADAM_B1 ** ADAM_STEP
    c2 = 1.0 - ADAM_B2 ** ADAM_STEP

    def body(w_ref, g_ref, m_ref, v_ref, d_ref, mo_ref, vo_ref):
        gv = g_ref[...]
        mn = ADAM_B1 * m_ref[...] + (1.0 - ADAM_B1) * gv
        vn = ADAM_B2 * v_ref[...] + (1.0 - ADAM_B2) * (gv * gv)
        d_ref[...] = -ADAM_LR * ((mn / c1) / (jnp.sqrt(vn / c2) + ADAM_EPS) + ADAM_WD * w_ref[...])
        mo_ref[...] = mn
        vo_ref[...] = vn

    spec = pl.BlockSpec((tr, c), lambda i: (i, 0))
    sds = jax.ShapeDtypeStruct((r, c), F32)
    return _pallas_hbm(
        body, name=name, grid=(r // tr,), in_specs=[spec] * 4, out_specs=[spec] * 3, out_shape=[sds] * 3,
        compiler_params=_cp(2 * 7 * 4 * tr * c + 8 * MIB),
    )(w, g, m, v)


def _adamw_small(g, w, m, v):
    r, c = w.shape
    c1 = 1.0 - ADAM_B1 ** ADAM_STEP
    c2 = 1.0 - ADAM_B2 ** ADAM_STEP

    def body(g_ref, w_ref, m_ref, v_ref, d_ref, mo_ref, vo_ref):
        gv = g_ref[...]
        mn = ADAM_B1 * m_ref[...] + (1.0 - ADAM_B1) * gv
        vn = ADAM_B2 * v_ref[...] + (1.0 - ADAM_B2) * (gv * gv)
        d_ref[...] = -ADAM_LR * ((mn / c1) / (jnp.sqrt(vn / c2) + ADAM_EPS) + ADAM_WD * w_ref[...])
        mo_ref[...] = mn
        vo_ref[...] = vn

    sds = jax.ShapeDtypeStruct((r, c), F32)
    return pl.pallas_call(
        body, name="adamw_small", out_shape=[sds] * 3,
        compiler_params=_cp(12 * 4 * r * c + 8 * MIB),
    )(g, w, m, v)


def _allreduce_small(v):
    r, c = v.shape
    assert r % (2 * SUBLANES) == 0
    h = r // 2

    def body(v_ref, out_ref, got_ref, chip_ref, slots_ref, send, recv):
        x, y, cc = _my_place()
        k = 2 * x + y
        sib = (x, y, 1 - cc)

        def rc(s, src, dst, to):
            return pltpu.make_async_remote_copy(src_ref=src, dst_ref=dst, send_sem=send.at[s], recv_sem=recv.at[s],
                                                device_id=to, device_id_type=MESH)

        pair = rc(0, v_ref, got_ref, sib)
        pair.start()
        pair.wait()
        chip_ref[...] = v_ref[...] + got_ref[...]
        mine = pl.ds(pl.multiple_of(cc * h, SUBLANES), h)
        theirs = pl.ds(pl.multiple_of((1 - cc) * h, SUBLANES), h)
        chips = _other_chips(x, y)
        sent = []
        for j, (cx, cy) in enumerate(chips):
            cp = rc(1 + j, chip_ref.at[mine], slots_ref.at[k], (cx, cy, cc))
            cp.start()
            sent.append(cp)
        slots_ref[k] = chip_ref[mine, :]
        for j, (cx, cy) in enumerate(chips):
            rc(1 + j, chip_ref.at[mine], slots_ref.at[2 * cx + cy], (cx, cy, cc)).wait_recv()
        total = slots_ref[0]
        for kk in range(1, N_CHIPS):
            total = total + slots_ref[kk]
        out_ref[mine, :] = total
        for cp in sent:
            cp.wait_send()
        share = rc(N_CHIPS, out_ref.at[mine], out_ref.at[mine], sib)
        share.start()
        rc(N_CHIPS, out_ref.at[theirs], out_ref.at[theirs], sib).wait_recv()
        share.wait_send()

    vm = pl.BlockSpec(memory_space=pltpu.VMEM)
    return pl.pallas_call(
        body, name="allreduce_small", out_shape=jax.ShapeDtypeStruct((r, c), F32), in_specs=[vm], out_specs=vm,
        scratch_shapes=[pltpu.VMEM((r, c), F32), pltpu.VMEM((r, c), F32), pltpu.VMEM((N_CHIPS, h, c), F32),
                        pltpu.SemaphoreType.DMA((N_CHIPS + 1,)), pltpu.SemaphoreType.DMA((N_CHIPS + 1,))],
        compiler_params=_cp(6 * 4 * r * c + 8 * MIB),
    )(v)


def _zoh(a_re, a_im, log_dt, b_re2, b_im2, expand):
    dt = jnp.exp(log_dt)
    mag = jnp.exp(a_re * dt)
    lbr, lbi = mag * jnp.cos(a_im * dt), mag * jnp.sin(a_im * dt)
    nr, ni = lbr - 1.0, lbi
    den = a_re * a_re + a_im * a_im
    qr = (nr * a_re + ni * a_im) / den
    qi = (ni * a_re - nr * a_im) / den
    qr2 = jnp.dot(qr, expand, precision=lax.Precision.HIGHEST, preferred_element_type=F32)
    qi2 = jnp.dot(qi, expand, precision=lax.Precision.HIGHEST, preferred_element_type=F32)
    return lbr, lbi, qr2 * b_re2 - qi2 * b_im2, qr2 * b_im2 + qi2 * b_re2


def _zoh_fwd(a_re, a_im, log_dt, b_re2, b_im2, expand, power):
    g, p = a_re.shape

    def body(ar_ref, ai_ref, ld_ref, br_ref, bi_ref, e_ref, lbr_ref, lbi_ref, bbr_ref, bbi_ref, pw_ref):
        ar, ai, ld = ar_ref[...], ai_ref[...], ld_ref[...]
        lbr, lbi, bbr, bbi = _zoh(ar, ai, ld, br_ref[...], bi_ref[...], e_ref[...])
        lbr_ref[...], lbi_ref[...], bbr_ref[...], bbi_ref[...] = lbr, lbi, bbr, bbi
        dt = jnp.exp(ld) * float(power)
        mag = jnp.exp(ar * dt)
        pw_ref[0] = mag * jnp.cos(ai * dt)
        pw_ref[1] = mag * jnp.sin(ai * dt)

    gp = jax.ShapeDtypeStruct((g, p), F32)
    gph = jax.ShapeDtypeStruct(b_re2.shape, F32)
    return pl.pallas_call(
        body, name="zoh_fwd", out_shape=[gp, gp, gph, gph, jax.ShapeDtypeStruct((2, g, p), F32)],
        compiler_params=_cp(16 * MIB),
    )(a_re, a_im, log_dt, b_re2, b_im2, expand)


def _zoh_bwd(a_re, a_im, log_dt, b_re2, b_im2, expand, g_lbr, g_lbi, g_bbr, g_bbi):
    def body(ar_ref, ai_ref, ld_ref, br_ref, bi_ref, e_ref, c0, c1, c2, c3, gar, gai, gld, gbr, gbi):
        e = e_ref[...]
        _, vjp = jax.vjp(lambda a, b, c, d, f: _zoh(a, b, c, d, f, e),
                         ar_ref[...], ai_ref[...], ld_ref[...], br_ref[...], bi_ref[...])
        gar[...], gai[...], gld[...], gbr[...], gbi[...] = vjp((c0[...], c1[...], c2[...], c3[...]))

    sds = lambda a: jax.ShapeDtypeStruct(a.shape, F32)
    return pl.pallas_call(
        body, name="zoh_bwd", out_shape=[sds(a_re), sds(a_im), sds(log_dt), sds(b_re2), sds(b_im2)],
        compiler_params=_cp(16 * MIB),
    )(a_re, a_im, log_dt, b_re2, b_im2, expand, g_lbr, g_lbi, g_bbr, g_bbi)


def _blockdiag(blocks):
    nq, _, r, c = blocks.shape
    eye = jnp.eye(GROUPS_PER_SLAB, dtype=blocks.dtype)
    out = blocks[:, :, :, None, :] * eye[None, :, None, :, None]
    return out.reshape(nq, GROUPS_PER_SLAB * r, GROUPS_PER_SLAB * c)


def _blockdiag_take(dense, r, c):
    nq = dense.shape[0]
    d5 = dense.reshape(nq, GROUPS_PER_SLAB, r, GROUPS_PER_SLAB, c)
    return jnp.stack([d5[:, i, :, i, :] for i in range(GROUPS_PER_SLAB)], axis=1)


def _scan_slab(s_ref, row0, q, lam_ref, pw_ref, car_ref, steps, reverse, prev_ref=None, prev_row0=0, glam_ref=None):
    sign = -1.0 if reverse else 1.0
    half = SLAB_STATES // SLAB
    cols = [(q * 2 * half + m, q * 2 * half + half + m, q * SLAB_STATES + m * SLAB) for m in range(half)]
    nm = len(cols)
    full = (SUBLANES, SLAB)
    lam = [(jnp.broadcast_to(lam_ref[0:1, pl.ds(cl, SLAB)], full),
            jnp.broadcast_to(sign * lam_ref[1:2, pl.ds(cl, SLAB)], full)) for (_, _, cl) in cols]

    def step_rows(jj, base):
        j = (steps - 1 - jj) if reverse else jj
        return j, pl.ds(pl.multiple_of(base + j * SUBLANES, SUBLANES), SUBLANES)

    def pass1(jj, car):
        _, rows = step_rows(jj, row0)
        out = []
        for m, (cr, ci, _) in enumerate(cols):
            sr, si = car[2 * m], car[2 * m + 1]
            lr, li = lam[m]
            nr = lr * sr - li * si + s_ref[cr, rows, :]
            ni = lr * si + li * sr + s_ref[ci, rows, :]
            s_ref[cr, rows, :] = nr
            s_ref[ci, rows, :] = ni
            out += [nr, ni]
        return tuple(out)

    def unrolled(step_fn):
        def outer(jo, carry):
            for k in range(SCAN_UNROLL):
                carry = step_fn(jo * SCAN_UNROLL + k, carry)
            return carry
        return outer

    assert steps % SCAN_UNROLL == 0
    ends = lax.fori_loop(0, steps // SCAN_UNROLL, unrolled(pass1), tuple(jnp.zeros(full, F32) for _ in range(2 * nm)))

    entry = []
    for m, (cr, ci, cl) in enumerate(cols):
        ljr = pw_ref[0:1, pl.ds(cl, SLAB)]
        lji = sign * pw_ref[1:2, pl.ds(cl, SLAB)]
        c_r = car_ref[cr, 0:1, :]
        c_i = car_ref[ci, 0:1, :]
        rows_r, rows_i = [None] * SUBLANES, [None] * SUBLANES
        order = range(SUBLANES - 1, -1, -1) if reverse else range(SUBLANES)
        for b in order:
            rows_r[b], rows_i[b] = c_r, c_i
            e_r, e_i = ends[2 * m][b:b + 1], ends[2 * m + 1][b:b + 1]
            c_r, c_i = ljr * c_r - lji * c_i + e_r, ljr * c_i + lji * c_r + e_i
        car_ref[cr, 0:1, :] = c_r
        car_ref[ci, 0:1, :] = c_i
        entry.append((jnp.concatenate(rows_r, axis=0), jnp.concatenate(rows_i, axis=0)))

    def pass2(jj, carry):
        j, rows = step_rows(jj, row0)
        decayed, acc = carry[:2 * nm], carry[2 * nm:]
        out_d, out_a = [], []
        for m, (cr, ci, cl) in enumerate(cols):
            lr, li = lam[m]
            dr, di = decayed[2 * m], decayed[2 * m + 1]
            dr, di = lr * dr - li * di, lr * di + li * dr
            nr = s_ref[cr, rows, :] + dr
            ni = s_ref[ci, rows, :] + di
            s_ref[cr, rows, :] = nr
            s_ref[ci, rows, :] = ni
            out_d += [dr, di]
            if prev_ref is not None:
                prow = pl.ds(pl.multiple_of(prev_row0 + (j - 1) * SUBLANES, SUBLANES), SUBLANES)
                qr = prev_ref[cr, prow, :]
                qi = prev_ref[ci, prow, :]
                out_a += [acc[2 * m] + (nr * qr + ni * qi), acc[2 * m + 1] + (ni * qr - nr * qi)]
        return tuple(out_d) + tuple(out_a)

    n_acc = 2 * nm if prev_ref is not None else 0
    init = tuple(e for pair in entry for e in pair) + tuple(jnp.zeros(full, F32) for _ in range(n_acc))
    accs = lax.fori_loop(0, steps // SCAN_UNROLL, unrolled(pass2), init)[2 * nm:]
    if prev_ref is not None:
        for m, (_, _, cl) in enumerate(cols):
            glam_ref[0:1, pl.ds(cl, SLAB)] += jnp.sum(accs[2 * m], axis=0, keepdims=True)
            glam_ref[1:2, pl.ds(cl, SLAB)] += jnp.sum(accs[2 * m + 1], axis=0, keepdims=True)


def _permute_rows_f32(perm_bf16, v):
    hi = v.astype(BF16)
    lo = (v - hi.astype(F32)).astype(BF16)
    return (jnp.dot(perm_bf16, hi, preferred_element_type=F32) + jnp.dot(perm_bf16, lo, preferred_element_type=F32))


def _put_slab(s_ref, rows, q, val):
    per = 2 * SLAB_STATES // SLAB
    for i in range(per):
        s_ref[q * per + i, rows, :] = val[:, i * SLAB:(i + 1) * SLAB]


def _get_slab(s_ref, rows, q):
    per = 2 * SLAB_STATES // SLAB
    return jnp.concatenate([s_ref[q * per + i, rows, :] for i in range(per)], axis=1)


def _step_major_perm(tt):
    r = jnp.arange(tt)
    held = (r % SUBLANES) * (tt // SUBLANES) + r // SUBLANES
    return held[:, None] == r[None, :]


def _ssm_fwd(proj, u_block, bq, cq, lam, pw, d_skip):
    t = proj.shape[0]
    nq, ds, w2 = bq.shape
    assert ds == SLAB and w2 == 2 * SLAB_STATES
    dssm = nq * SLAB
    width = nq * w2
    ntile = width // SLAB
    tt = min(T_SCAN, t)
    steps = tt // SUBLANES
    perm = _step_major_perm(tt)
    pm, pmt = perm.astype(BF16), perm.T.astype(BF16)

    def body(u_ref, pm_ref, pmt_ref, bq_ref, cq_ref, lam_ref, pw_ref, d_ref, y_ref, cin_ref, s_ref, car_ref):
        @pl.when(pl.program_id(0) == 0)
        def _():
            car_ref[...] = jnp.zeros_like(car_ref)

        cin_ref[...] = jnp.broadcast_to(car_ref[:, 0:1, :], cin_ref.shape)
        u = u_ref[...]
        ub = jnp.dot(pm_ref[...], u.astype(BF16), preferred_element_type=F32).astype(BF16)
        everything = slice(None)
        for q in range(nq):
            _put_slab(s_ref, everything, q,
                      jnp.dot(ub[:, q * SLAB:(q + 1) * SLAB], bq_ref[q], preferred_element_type=F32))
        for q in range(nq):
            _scan_slab(s_ref, 0, q, lam_ref, pw_ref, car_ref, steps, reverse=False)
        y_sm = jnp.concatenate(
            [jnp.dot(_get_slab(s_ref, everything, q).astype(BF16), cq_ref[q], preferred_element_type=F32)
             for q in range(nq)], axis=1)
        y_ref[...] = _permute_rows_f32(pmt_ref[...], y_sm) + d_ref[...] * u

    c3 = lambda i: (0, 0, 0)
    c2 = lambda i: (0, 0)
    return _pallas_hbm(
        body, name="ssm_fwd", grid=(t // tt,),
        in_specs=[pl.BlockSpec((tt, dssm), lambda i: (i, u_block)), pl.BlockSpec((tt, tt), c2),
                  pl.BlockSpec((tt, tt), c2), pl.BlockSpec(bq.shape, c3),
                  pl.BlockSpec(cq.shape, c3), pl.BlockSpec(lam.shape, c2), pl.BlockSpec(pw.shape, c2),
                  pl.BlockSpec((1, dssm), c2)],
        out_specs=[pl.BlockSpec((tt, dssm), lambda i: (i, 0)),
                   pl.BlockSpec((None, ntile, SUBLANES, SLAB), lambda i: (i, 0, 0, 0))],
        out_shape=[jax.ShapeDtypeStruct((t, dssm), F32), jax.ShapeDtypeStruct((t // tt, ntile, SUBLANES, SLAB), F32)],
        scratch_shapes=[pltpu.VMEM((ntile, tt, SLAB), F32), pltpu.VMEM((ntile, SUBLANES, SLAB), F32)],
        compiler_params=_cp(2 * (8 * tt * dssm + 4 * nq * ds * w2) + 4 * tt * width + 16 * MIB,
                            dimension_semantics=("arbitrary",)),
    )(proj, pm, pmt, bq, cq, lam, pw, d_skip)


def _ssm_bwd(proj, u_block, dy, cin, dproj, bq, cq, lam, pw, d_skip):
    t = proj.shape[0]
    nq, ds, w2 = bq.shape
    dssm = nq * SLAB
    width = nq * w2
    ntile = width // SLAB
    tt = min(T_SCAN, t)
    nt = t // tt
    steps = tt // SUBLANES
    halo = SUBLANES
    perm = _step_major_perm(tt)
    pm, pmt = perm.astype(BF16), perm.T.astype(BF16)

    def body(u_ref, dy_ref, cin_ref, dp_any, pm_ref, pmt_ref, bq_ref, cq_ref, lam_ref, pw_ref, d_ref,
             du_ref, gb_ref, gc_ref, glam_ref, gd_ref, s_ref, gs_ref, car_f, car_b):
        del dp_any

        @pl.when(pl.program_id(0) == 0)
        def _():
            car_b[...] = jnp.zeros_like(car_b)
            gb_ref[...] = jnp.zeros_like(gb_ref)
            gc_ref[...] = jnp.zeros_like(gc_ref)
            glam_ref[...] = jnp.zeros_like(glam_ref)
            gd_ref[...] = jnp.zeros_like(gd_ref)

        u = u_ref[...]
        dyv = dy_ref[...]
        gd_ref[0:1, :] += jnp.sum(dyv * u, axis=0, keepdims=True)
        pmv = pm_ref[...]
        ub = jnp.dot(pmv, u.astype(BF16), preferred_element_type=F32).astype(BF16)
        dyb = jnp.dot(pmv, dyv.astype(BF16), preferred_element_type=F32).astype(BF16)
        car_f[...] = cin_ref[...]
        data = slice(halo, halo + tt)
        everything = slice(None)
        for q in range(nq):
            _put_slab(s_ref, data, q, jnp.dot(ub[:, q * SLAB:(q + 1) * SLAB], bq_ref[q], preferred_element_type=F32))
        for q in range(nq):
            _scan_slab(s_ref, halo, q, lam_ref, pw_ref, car_f, steps, reverse=False)
        last_step = s_ref[:, halo + tt - SUBLANES:halo + tt, :]
        s_ref[:, 0:halo, :] = jnp.concatenate([cin_ref[:, 0:1, :], last_step[:, 0:SUBLANES - 1, :]], axis=1)
        tn = (((0,), (0,)), ((), ()))
        nt_dims = (((1,), (1,)), ((), ()))
        for q in range(nq):
            sl = slice(q * SLAB, (q + 1) * SLAB)
            gc_ref[q] += lax.dot_general(dyb[:, sl], _get_slab(s_ref, data, q).astype(BF16), tn,
                                         preferred_element_type=F32)
            _put_slab(gs_ref, everything, q,
                      lax.dot_general(dyb[:, sl], cq_ref[q], nt_dims, preferred_element_type=F32))
        for q in range(nq):
            _scan_slab(gs_ref, 0, q, lam_ref, pw_ref, car_b, steps, reverse=True,
                       prev_ref=s_ref, prev_row0=halo, glam_ref=glam_ref)
        du_parts = []
        for q in range(nq):
            sl = slice(q * SLAB, (q + 1) * SLAB)
            gsb = _get_slab(gs_ref, everything, q).astype(BF16)
            du_parts.append(lax.dot_general(gsb, bq_ref[q], nt_dims, preferred_element_type=F32))
            gb_ref[q] += lax.dot_general(ub[:, sl], gsb, tn, preferred_element_type=F32)
        du_sm = jnp.concatenate(du_parts, axis=1)
        du_ref[...] = (_permute_rows_f32(pmt_ref[...], du_sm) + dyv * d_ref[...]).astype(BF16)

    c3 = lambda i: (0, 0, 0)
    c2 = lambda i: (0, 0)
    rev = lambda i: (nt - 1 - i, 0)
    dense = jax.ShapeDtypeStruct((nq, SLAB, w2), F32)
    gp = lam.shape[1]
    return _pallas_hbm(
        body, name="ssm_bwd", grid=(nt,),
        in_specs=[pl.BlockSpec((tt, dssm), lambda i: (nt - 1 - i, u_block)), pl.BlockSpec((tt, dssm), rev),
                  pl.BlockSpec((None, ntile, SUBLANES, SLAB), lambda i: (nt - 1 - i, 0, 0, 0)),
                  pl.BlockSpec(memory_space=pl.ANY), pl.BlockSpec((tt, tt), c2), pl.BlockSpec((tt, tt), c2),
                  pl.BlockSpec(bq.shape, c3), pl.BlockSpec(cq.shape, c3),
                  pl.BlockSpec(lam.shape, c2), pl.BlockSpec(pw.shape, c2), pl.BlockSpec((1, dssm), c2)],
        out_specs=[pl.BlockSpec((tt, dssm), lambda i: (nt - 1 - i, u_block)), pl.BlockSpec(dense.shape, c3),
                   pl.BlockSpec(dense.shape, c3), pl.BlockSpec((SUBLANES, gp), c2), pl.BlockSpec((SUBLANES, dssm), c2)],
        out_shape=[jax.ShapeDtypeStruct(dproj.shape, dproj.dtype), dense, dense,
                   jax.ShapeDtypeStruct((SUBLANES, gp), F32), jax.ShapeDtypeStruct((SUBLANES, dssm), F32)],
        scratch_shapes=[pltpu.VMEM((ntile, tt + halo, SLAB), F32), pltpu.VMEM((ntile, tt, SLAB), F32),
                        pltpu.VMEM((ntile, SUBLANES, SLAB), F32), pltpu.VMEM((ntile, SUBLANES, SLAB), F32)],
        input_output_aliases={3: 0},
        compiler_params=_cp(2 * (10 * tt * dssm + 4 * nq * ds * w2 + 8 * nq * SLAB * w2)
                            + 8 * tt * width + 12 * MIB, dimension_semantics=("arbitrary",)),
    )(proj, dy, cin, dproj, pm, pmt, bq, cq, lam, pw, d_skip)


def _gate_fwd(proj, y, conv_w, conv_b, w_glu, b_glu, dc):
    t = proj.shape[0]
    dssm = y.shape[1]
    assert dc == dssm
    tm = min(TM_GATE, t)
    halo = SUBLANES

    def body(b_ref, c_ref, v_ref, zc_ref, zs_ref, y_ref, cw_ref, cb_ref, wg_ref, bg_ref, mix_ref, cv_buf):
        @pl.when(pl.program_id(0) == 0)
        def _():
            cv_buf[0:halo, :] = jnp.zeros((halo, dc), F32)

        cv = c_ref[...] * v_ref[...]
        cv_buf[halo:, :] = cv
        conv = (cb_ref[...] + cw_ref[2:3, :] * cv + cw_ref[1:2, :] * cv_buf[halo - 1:halo - 1 + tm, :]
                + cw_ref[0:1, :] * cv_buf[halo - 2:halo - 2 + tm, :])
        sz, _ = _silu(zc_ref[...])
        mix_ref[:, 0:dc] = (b_ref[...] * conv * sz).astype(BF16)
        cv_buf[0:halo, :] = cv_buf[tm:tm + halo, :]
        ge, _ = _gelu(y_ref[...])
        gl = jnp.dot(ge.astype(BF16), wg_ref[...], preferred_element_type=F32) + bg_ref[...]
        szs, _ = _silu(zs_ref[...])
        mix_ref[:, dc:] = (ge * jax.nn.sigmoid(gl) * szs).astype(BF16)

    col = lambda j: pl.BlockSpec((tm, dc), lambda i, j=j: (i, j))
    fixed = lambda i: (0, 0)
    return _pallas_hbm(
        body, name="gate_fwd", grid=(t // tm,),
        in_specs=[col(0), col(1), col(2), col(3), col(5), pl.BlockSpec((tm, dssm), lambda i: (i, 0)),
                  pl.BlockSpec(conv_w.shape, fixed), pl.BlockSpec((1, dc), fixed),
                  pl.BlockSpec(w_glu.shape, fixed), pl.BlockSpec((1, dssm), fixed)],
        out_specs=pl.BlockSpec((tm, dc + dssm), lambda i: (i, 0)),
        out_shape=jax.ShapeDtypeStruct((t, dc + dssm), BF16),
        scratch_shapes=[pltpu.VMEM((tm + halo, dc), F32)],
        compiler_params=_cp(2 * (6 * 4 * tm * dc + 2 * tm * (dc + dssm) + 2 * dssm * dssm) + 24 * tm * dc + 8 * MIB,
                            dimension_semantics=("arbitrary",)),
    )(proj, proj, proj, proj, proj, y, conv_w, conv_b, w_glu, b_glu)


def _gate_bwd(proj, y, dmix, conv_w, conv_b, w_glu, b_glu, dc):
    t = proj.shape[0]
    dssm = y.shape[1]
    tm = min(TM_GATE, t)
    nt = t // tm
    halo = SUBLANES
    blocks_per_tile = tm // halo

    def body(b_ref, c_ref, v_ref, zc_ref, zs_ref, cp_ref, vp_ref, y_ref, dm_ref, cw_ref, cb_ref, wg_ref, bg_ref,
             dp_ref, dy_ref, gs_ref, gwg_ref, cv_buf, dc_buf):
        i = pl.program_id(0)

        @pl.when(i == 0)
        def _():
            dc_buf[tm:, :] = jnp.zeros((halo, dc), F32)
            gs_ref[...] = jnp.zeros_like(gs_ref)
            gwg_ref[...] = jnp.zeros_like(gwg_ref)

        first_tile = (i == nt - 1)
        bv, cg, vv, zc = b_ref[...], c_ref[...], v_ref[...], zc_ref[...]
        cv = cg * vv
        cv_buf[0:halo, :] = jnp.where(first_tile, 0.0, cp_ref[...] * vp_ref[...])
        cv_buf[halo:, :] = cv
        w0, w1, w2 = cw_ref[0:1, :], cw_ref[1:2, :], cw_ref[2:3, :]
        conv = (cb_ref[...] + w2 * cv + w1 * cv_buf[halo - 1:halo - 1 + tm, :]
                + w0 * cv_buf[halo - 2:halo - 2 + tm, :])
        sz, sgc = _silu(zc)
        dyc = dm_ref[:, 0:dc]
        dp_ref[:, 0:dc] = (dyc * conv * sz).astype(BF16)
        dp_ref[:, 3 * dc:4 * dc] = (dyc * bv * conv * _dsilu(zc, sgc)).astype(BF16)
        dconv = dyc * bv * sz
        dc_buf[0:tm, :] = dconv
        d1 = dc_buf[1:1 + tm, :]
        d2 = dc_buf[2:2 + tm, :]
        dcv = w2 * dconv + w1 * d1 + w0 * d2
        dp_ref[:, dc:2 * dc] = (dcv * vv).astype(BF16)
        dp_ref[:, 2 * dc:3 * dc] = (dcv * cg).astype(BF16)
        gs_ref[0:1, :] += jnp.sum(cv * d2, axis=0, keepdims=True)
        gs_ref[1:2, :] += jnp.sum(cv * d1, axis=0, keepdims=True)
        gs_ref[2:3, :] += jnp.sum(cv * dconv, axis=0, keepdims=True)
        gs_ref[3:4, :] += jnp.sum(dconv, axis=0, keepdims=True)
        dc_buf[tm:, :] = dc_buf[0:halo, :]

        yv, zs = y_ref[...], zs_ref[...]
        ge, th = _gelu(yv)
        geb = ge.astype(BF16)
        wg = wg_ref[...]
        gl = jnp.dot(geb, wg, preferred_element_type=F32) + bg_ref[...]
        sg = jax.nn.sigmoid(gl)
        szs, sgs = _silu(zs)
        dys = dm_ref[:, dc:]
        ys = ge * sg
        dp_ref[:, 4 * dc:5 * dc] = jnp.zeros((tm, dc), BF16)
        dp_ref[:, 5 * dc:] = (dys * ys * _dsilu(zs, sgs)).astype(BF16)
        d_ys = dys * szs
        dgl = d_ys * ge * sg * (1.0 - sg)
        dglb = dgl.astype(BF16)
        gs_ref[4:5, :] += jnp.sum(dgl, axis=0, keepdims=True)
        gwg_ref[...] += lax.dot_general(geb, dglb, (((0,), (0,)), ((), ())), preferred_element_type=F32)
        dge = d_ys * sg + lax.dot_general(dglb, wg, (((1,), (1,)), ((), ())), preferred_element_type=F32)
        dy_ref[...] = dge * _dgelu(yv, th)

    col = lambda j: pl.BlockSpec((tm, dc), lambda i, j=j: (nt - 1 - i, j))
    prev = lambda j: pl.BlockSpec((halo, dc), lambda i, j=j: (jnp.maximum((nt - 1 - i) * blocks_per_tile - 1, 0), j))
    rev = lambda i: (nt - 1 - i, 0)
    fixed = lambda i: (0, 0)
    return _pallas_hbm(
        body, name="gate_bwd", grid=(nt,),
        in_specs=[col(0), col(1), col(2), col(3), col(5), prev(1), prev(2), pl.BlockSpec((tm, dssm), rev),
                  pl.BlockSpec((tm, dc + dssm), rev), pl.BlockSpec(conv_w.shape, fixed), pl.BlockSpec((1, dc), fixed),
                  pl.BlockSpec(w_glu.shape, fixed), pl.BlockSpec((1, dssm), fixed)],
        out_specs=[pl.BlockSpec((tm, 6 * dc), rev), pl.BlockSpec((tm, dssm), rev),
                   pl.BlockSpec((2 * SUBLANES, dc), fixed), pl.BlockSpec((dssm, dssm), fixed)],
        out_shape=[jax.ShapeDtypeStruct((t, 6 * dc), BF16), jax.ShapeDtypeStruct((t, dssm), F32),
                   jax.ShapeDtypeStruct((2 * SUBLANES, dc), F32), jax.ShapeDtypeStruct((dssm, dssm), F32)],
        scratch_shapes=[pltpu.VMEM((tm + halo, dc), F32), pltpu.VMEM((tm + halo, dc), F32)],
        compiler_params=_cp(2 * (6 * 4 * tm * dc + 8 * tm * dc + 12 * tm * dc + 4 * tm * dc + 6 * dssm * dssm)
                            + 40 * tm * dc + 8 * MIB, dimension_semantics=("arbitrary",)),
    )(proj, proj, proj, proj, proj, proj, proj, y, dmix, conv_w, conv_b, w_glu, b_glu)


def _allgather_halves(parts, name):
    n = len(parts)
    per = 2 * (N_CHIPS - 1)

    def body(*refs):
        ins, outs = refs[:n], refs[n:2 * n]
        send, recv, local = refs[2 * n:]
        x, y, c = _my_place()
        k = 2 * x + y
        sib = (x, y, 1 - c)
        chips = _other_chips(x, y)

        def rc(t, s, src, dst, to):
            return pltpu.make_async_remote_copy(src_ref=src, dst_ref=dst, send_sem=send.at[t * per + s],
                                                recv_sem=recv.at[t * per + s], device_id=to, device_id_type=MESH)

        mine = [pltpu.make_async_copy(ins[t], outs[t].at[k], local.at[t]) for t in range(n)]
        for cp in mine:
            cp.start()
        sent = []
        for t in range(n):
            for j, (cx, cy) in enumerate(chips):
                cp = rc(t, j, ins[t].at[c], outs[t].at[k, c], (cx, cy, c))
                cp.start()
                sent.append(cp)
        for t in range(n):
            for j, (cx, cy) in enumerate(chips):
                landed = outs[t].at[2 * cx + cy, c]
                rc(t, j, landed, landed, (cx, cy, c)).wait_recv()
                cp = rc(t, N_CHIPS - 1 + j, landed, landed, sib)
                cp.start()
                sent.append(cp)
        for t in range(n):
            for j, (cx, cy) in enumerate(chips):
                other = outs[t].at[2 * cx + cy, 1 - c]
                rc(t, N_CHIPS - 1 + j, other, other, sib).wait_recv()
        for cp in sent:
            cp.wait_send()
        for cp in mine:
            cp.wait()

    any_spec = pl.BlockSpec(memory_space=pl.ANY)
    return _pallas_hbm(
        body, name=name, in_specs=[any_spec] * n, out_specs=[any_spec] * n,
        out_shape=[jax.ShapeDtypeStruct((N_CHIPS,) + p.shape, p.dtype) for p in parts],
        scratch_shapes=[pltpu.SemaphoreType.DMA((n * per,)), pltpu.SemaphoreType.DMA((n * per,)),
                        pltpu.SemaphoreType.DMA((n,))],
        compiler_params=_cp(16 * MIB),
    )(*parts)


def _allgather_flat(v, name):
    r, c = v.shape
    rels = [(dx, dy, dc) for dx in (0, 1) for dy in (0, 1) for dc in (0, 1)][1:]

    def body(v_ref, out_ref, send, recv):
        x, y, cc = _my_place()
        me = 4 * x + 2 * y + cc

        def peer(rel):
            dx, dy, dc = rel
            return (1 - x if dx else x, 1 - y if dy else y, 1 - cc if dc else cc)

        def rc(s, slot, to):
            return pltpu.make_async_remote_copy(src_ref=v_ref, dst_ref=out_ref.at[slot], send_sem=send.at[s],
                                                recv_sem=recv.at[s], device_id=to, device_id_type=MESH)

        sent = []
        for s, rel in enumerate(rels):
            cp = rc(s, me, peer(rel))
            cp.start()
            sent.append(cp)
        out_ref[me] = v_ref[...]
        for s, rel in enumerate(rels):
            px, py, pc = peer(rel)
            rc(s, 4 * px + 2 * py + pc, (px, py, pc)).wait_recv()
        for cp in sent:
            cp.wait_send()

    return pl.pallas_call(
        body, name=name, out_shape=jax.ShapeDtypeStruct((N_DEV, r, c), F32),
        in_specs=[pl.BlockSpec(memory_space=pltpu.VMEM)], out_specs=pl.BlockSpec(memory_space=pltpu.VMEM),
        scratch_shapes=[pltpu.SemaphoreType.DMA((N_DEV - 1,)), pltpu.SemaphoreType.DMA((N_DEV - 1,))],
        compiler_params=_cp((N_DEV + 2) * 4 * r * c + 8 * MIB),
    )(v)


def _rs_pair_exchange(grads, name):
    n = len(grads)

    def body(*refs):
        ins, outs = refs[:n], refs[n:2 * n]
        send, recv = refs[2 * n:]
        x, y, c = _my_place()
        cps = []
        for t in range(n):
            cp = pltpu.make_async_remote_copy(src_ref=ins[t].at[:, 1 - c], dst_ref=outs[t], send_sem=send.at[t],
                                              recv_sem=recv.at[t], device_id=(x, y, 1 - c), device_id_type=MESH)
            cp.start()
            cps.append(cp)
        for cp in cps:
            cp.wait()

    any_spec = pl.BlockSpec(memory_space=pl.ANY)
    return _pallas_hbm(
        body, name=name, in_specs=[any_spec] * n, out_specs=[any_spec] * n,
        out_shape=[jax.ShapeDtypeStruct((g.shape[0],) + g.shape[2:], F32) for g in grads],
        scratch_shapes=[pltpu.SemaphoreType.DMA((n,)), pltpu.SemaphoreType.DMA((n,))],
        compiler_params=_cp(16 * MIB),
    )(*grads)


def _rs_pair_add(place, grad, got, name):
    nk, _, r2, c = grad.shape
    tr = min(TR_ELT, r2)

    def body(place_ref, g_ref, r_ref, o16_ref, o32_ref):
        del place_ref
        s = g_ref[...] + r_ref[...]
        o32_ref[...] = s
        o16_ref[...] = s.astype(BF16)

    blk = pl.BlockSpec((None, tr, c), lambda k, i, p: (k, i, 0))
    return _pallas_hbm(
        body, name=name,
        grid_spec=pltpu.PrefetchScalarGridSpec(
            num_scalar_prefetch=1, grid=(nk, r2 // tr),
            in_specs=[pl.BlockSpec((None, None, tr, c), lambda k, i, p: (k, p[0], i, 0)), blk],
            out_specs=[blk, blk]),
        out_shape=[jax.ShapeDtypeStruct((nk, r2, c), BF16), jax.ShapeDtypeStruct((nk, r2, c), F32)],
        compiler_params=_cp(2 * 14 * tr * c + 8 * MIB),
    )(place, grad, got)


_HBM_SPEC = pl.BlockSpec(memory_space=pltpu.HBM)
_SEM_SPEC = pl.BlockSpec(memory_space=pltpu.SEMAPHORE)
_DATAFLOW = pltpu.SideEffectType.DATAFLOW_SIDE_EFFECTING


def _split_copy_start(srcs, land_shapes, plan, n_sems, name):
    ns, nl = len(srcs), len(land_shapes)

    def body(*refs):
        src_refs, land_refs = refs[:ns], refs[ns:ns + nl]
        send, recv = refs[ns + nl], refs[ns + nl + 1]
        token = refs[-1]
        sends, _ = plan(src_refs, land_refs)
        for src, dst, to, si, ri in sends:
            pltpu.make_async_remote_copy(src_ref=src, dst_ref=dst, send_sem=send.at[si], recv_sem=recv.at[ri],
                                         device_id=to, device_id_type=MESH).start()
        token[...] = jnp.zeros_like(token)

    lands = [lax.empty(shp, dt) for shp, dt in land_shapes]
    through = [pltpu.HBM(a.shape, a.dtype) for a in srcs] + [pltpu.HBM(shp, dt) for shp, dt in land_shapes]
    out = pl.pallas_call(
        body, name=name,
        out_shape=(pltpu.SemaphoreType.DMA((n_sems,)), pltpu.SemaphoreType.DMA((n_sems,)), *through,
                   jax.ShapeDtypeStruct((SUBLANES, SLAB), F32)),
        in_specs=[_HBM_SPEC] * (ns + nl),
        out_specs=(_SEM_SPEC, _SEM_SPEC, *([_HBM_SPEC] * (ns + nl)), pl.BlockSpec(memory_space=pltpu.VMEM)),
        input_output_aliases={i: 2 + i for i in range(ns + nl)},
        compiler_params=pltpu.CompilerParams(has_side_effects=_DATAFLOW),
    )(*[pltpu.with_memory_space_constraint(a, pltpu.HBM) for a in (*srcs, *lands)])
    return out[0], out[1], list(out[2:2 + ns]), list(out[2 + ns:2 + ns + nl]), out[-1]


def _split_copy_wait(send, recv, srcs, lands, plan, after, name):
    ns, nl, na = len(srcs), len(lands), len(after)

    def body(*refs):
        src_refs, land_refs = refs[:ns], refs[ns:ns + nl]
        send_ref, recv_ref = refs[ns + nl], refs[ns + nl + 1]
        sends, arrivals = plan(src_refs, land_refs)
        for src, dst, to, si, ri in sends:
            pltpu.make_async_remote_copy(src_ref=src, dst_ref=dst, send_sem=send_ref.at[si], recv_sem=recv_ref.at[ri],
                                         device_id=to, device_id_type=MESH).wait_send()
        for (src, _, to, si, _), (view, ri) in zip(sends, arrivals):
            pltpu.make_async_remote_copy(src_ref=view, dst_ref=view, send_sem=send_ref.at[si], recv_sem=recv_ref.at[ri],
                                         device_id=to, device_id_type=MESH).wait_recv()

    out = pl.pallas_call(
        body, name=name,
        out_shape=[pltpu.HBM(a.shape, a.dtype) for a in (*srcs, *lands)],
        in_specs=[_HBM_SPEC] * (ns + nl) + [_SEM_SPEC, _SEM_SPEC] + [pl.BlockSpec(memory_space=pl.ANY)] * na,
        out_specs=[_HBM_SPEC] * (ns + nl),
        input_output_aliases={i: i for i in range(ns + nl)},
        compiler_params=pltpu.CompilerParams(has_side_effects=_DATAFLOW),
    )(*srcs, *lands, send, recv, *after)
    return list(out[:ns]), list(out[ns:])


def _chip_exchange_plan(n):
    per = N_CHIPS - 1

    def plan(srcs, lands):
        x, y, c = _my_place()
        sends, arrivals = [], []
        for t in range(n):
            for j, (cx, cy) in enumerate(_other_chips(x, y)):
                sends.append((srcs[t].at[2 * cx + cy], lands[t].at[j], (cx, cy, c), t * per + j, t * per + j))
                arrivals.append((lands[t].at[j], t * per + j))
        return sends, arrivals

    return plan


def _gather_direct_plan(n):
    per = 2 * (N_CHIPS - 1)

    def plan(srcs, lands):
        x, y, c = _my_place()
        k = 2 * x + y
        sends, arrivals = [], []
        for t in range(n):
            for j, (cx, cy) in enumerate(_other_chips(x, y)):
                for core in (0, 1):
                    sends.append((srcs[t].at[c], lands[t].at[k, c], (cx, cy, core),
                                  t * per + 2 * j + core, t * per + 2 * j + c))
                    arrivals.append((lands[t].at[2 * cx + cy, core], t * per + 2 * j + core))
        return sends, arrivals

    return plan


def _rs_chip_add(place, sums32, got, name):
    _, r2, c = sums32.shape
    tr = min(TR_ELT, r2)

    def body(place_ref, s_ref, q_ref, o_ref):
        del place_ref
        o_ref[...] = ((s_ref[...] + q_ref[0].astype(F32)) + q_ref[1].astype(F32)) + q_ref[2].astype(F32)

    return _pallas_hbm(
        body, name=name,
        grid_spec=pltpu.PrefetchScalarGridSpec(
            num_scalar_prefetch=1, grid=(r2 // tr,),
            in_specs=[pl.BlockSpec((None, tr, c), lambda i, p: (p[1], i, 0)),
                      pl.BlockSpec((N_CHIPS - 1, tr, c), lambda i, p: (0, i, 0))],
            out_specs=pl.BlockSpec((tr, c), lambda i, p: (i, 0))),
        out_shape=jax.ShapeDtypeStruct((r2, c), F32),
        compiler_params=_cp(2 * 14 * tr * c + 8 * MIB),
    )(place, sums32, got)


def _rs_pair_share(halves, name):
    n = len(halves)

    def body(*refs):
        ins, outs = refs[:n], refs[n:2 * n]
        send, recv, local = refs[2 * n:]
        x, y, c = _my_place()
        cps, mine = [], []
        for t in range(n):
            lc = pltpu.make_async_copy(ins[t], outs[t].at[c], local.at[t])
            lc.start()
            mine.append(lc)
            cp = pltpu.make_async_remote_copy(src_ref=ins[t], dst_ref=outs[t].at[c], send_sem=send.at[t],
                                              recv_sem=recv.at[t], device_id=(x, y, 1 - c), device_id_type=MESH)
            cp.start()
            cps.append(cp)
        for t in range(n):
            other = outs[t].at[1 - c]
            pltpu.make_async_remote_copy(src_ref=other, dst_ref=other, send_sem=send.at[t], recv_sem=recv.at[t],
                                         device_id=(x, y, 1 - c), device_id_type=MESH).wait_recv()
        for cp in cps:
            cp.wait_send()
        for lc in mine:
            lc.wait()

    any_spec = pl.BlockSpec(memory_space=pl.ANY)
    return _pallas_hbm(
        body, name=name, in_specs=[any_spec] * n, out_specs=[any_spec] * n,
        out_shape=[jax.ShapeDtypeStruct((2,) + h.shape, F32) for h in halves],
        scratch_shapes=[pltpu.SemaphoreType.DMA((n,)), pltpu.SemaphoreType.DMA((n,)), pltpu.SemaphoreType.DMA((n,))],
        compiler_params=_cp(16 * MIB),
    )(*halves)


_PACK_TILE = SUBLANES * SLAB


def _pack(arrays):
    rows = []
    for a in arrays:
        flat = a.reshape(-1).astype(F32)
        padded = -(-flat.shape[0] // _PACK_TILE) * _PACK_TILE
        rows.append(jnp.pad(flat, (0, padded - flat.shape[0])).reshape(-1, SLAB))
    n_rows = sum(r.shape[0] for r in rows)
    if n_rows % (2 * SUBLANES):
        rows.append(jnp.zeros((SUBLANES, SLAB), F32))
    return jnp.concatenate(rows, axis=0)


def _unpack(packed, shapes):
    out, row = [], 0
    for shp in shapes:
        size = math.prod(shp)
        nrow = -(-size // _PACK_TILE) * SUBLANES
        out.append(packed[row:row + nrow].reshape(-1)[:size].reshape(shp))
        row += nrow
    return out


def kernel(x, norm_pre_g, w_in, conv_w, conv_b, ssm_a_re, ssm_a_im, ssm_log_dt, ssm_b_re, ssm_b_im, ssm_c_re, ssm_c_im, ssm_d, w_glu, b_glu, w_out, norm_post_g, loss_target, m_norm_pre_g, m_w_in, m_conv_w, m_conv_b, m_ssm_a_re, m_ssm_a_im, m_ssm_log_dt, m_ssm_b_re, m_ssm_b_im, m_ssm_c_re, m_ssm_c_im, m_ssm_d, m_w_glu, m_b_glu, m_w_out, m_norm_post_g, v_norm_pre_g, v_w_in, v_conv_w, v_conv_b, v_ssm_a_re, v_ssm_a_im, v_ssm_log_dt, v_ssm_b_re, v_ssm_b_im, v_ssm_c_re, v_ssm_c_im, v_ssm_d, v_w_glu, v_b_glu, v_w_out, v_norm_post_g):
    xs, tgt = x[0], loss_target[0]
    t, d = xs.shape
    dc = conv_b.shape[0]
    dssm = ssm_d.shape[0]
    g, p = ssm_a_re.shape
    h = SSM_H
    nq = dssm // SLAB
    n_shard = w_in.shape[1]
    steps = min(T_SCAN, t) // SUBLANES
    mx, my, mc = _my_place()
    chip = 2 * mx + my
    place = jnp.stack([mc, chip]).astype(jnp.int32)

    halves = lambda a: a.reshape(2, a.shape[0] // 2, a.shape[1])
    (win_g,) = _allgather_halves([halves(_cast_bf16(w_in, "cast_w_in"))], "allgather_w_in")
    win_b = win_g.reshape(N_CHIPS, d, n_shard)
    gathered = win_g[0, 0, :SUBLANES, :SLAB].astype(F32)
    side = [halves(_cast_bf16(w_out, "cast_w_out", gathered)), halves(_cast_bf16(w_glu, "cast_w_glu", gathered))]
    side_plan = _gather_direct_plan(len(side))
    side_sems = 2 * (N_CHIPS - 1) * len(side)
    ag_send, ag_recv, ag_srcs, ag_lands, ag_token = _split_copy_start(
        side, [((N_CHIPS,) + a.shape, BF16) for a in side], side_plan, side_sems, "gather_side_weights_start")

    cw_cols = conv_w.shape[1]
    cw_pad = -(-cw_cols // SLAB) * SLAB
    cw_blk = jnp.zeros((SUBLANES, cw_pad), F32).at[:conv_w.shape[0], :cw_cols].set(conv_w)
    cw_all = _allgather_flat(cw_blk, "allgather_conv_w")
    conv_w_full = jnp.concatenate([cw_all[2 * k, :, :cw_cols] for k in range(N_CHIPS)], axis=1)

    expand = jnp.repeat(jnp.eye(p, dtype=F32), h, axis=1)
    b_re2, b_im2 = ssm_b_re.reshape(g, p * h), ssm_b_im.reshape(g, p * h)
    log_dt2 = ssm_log_dt.reshape(g, 1)
    lbr, lbi, bbr2, bbi2, pw3 = _zoh_fwd(ssm_a_re, ssm_a_im, log_dt2, b_re2, b_im2, expand, steps)
    lam = jnp.stack([lbr.reshape(g * p), lbi.reshape(g * p)])
    pw = pw3.reshape(2, g * p)
    to_slab_b = lambda b2: _blockdiag(b2.reshape(nq, GROUPS_PER_SLAB, p, h).transpose(0, 1, 3, 2))
    bq = jnp.concatenate([to_slab_b(bbr2), to_slab_b(bbi2)], axis=2).astype(BF16)
    to_slab_c = lambda c3: _blockdiag(c3.reshape(nq, GROUPS_PER_SLAB, h, p).transpose(0, 1, 3, 2))
    cq = jnp.concatenate([to_slab_c(ssm_c_re), to_slab_c(-ssm_c_im)], axis=1).astype(BF16)

    g_pre2, g_post2 = norm_pre_g.reshape(1, d), norm_post_g.reshape(1, d)
    conv_b2, b_glu2, d_skip2 = conv_b.reshape(1, dc), b_glu.reshape(1, dssm), ssm_d.reshape(1, dssm)
    u_block = 4 * dc // dssm

    hb = _prenorm(xs, g_pre2 + ag_token[0:1, 0:1])
    proj = _matmul_nn(hb, win_b, "inproj")
    y, cin = _ssm_fwd(proj, u_block, bq, cq, lam, pw, d_skip2)
    side_own, side_all = _split_copy_wait(ag_send, ag_recv, ag_srcs, ag_lands, side_plan, [cin],
                                          "gather_side_weights_wait")
    wout_g, wglu_g = [lax.dynamic_update_index_in_dim(all_, own, chip, 0) for own, all_ in zip(side_own, side_all)]
    wout_b = wout_g.reshape(dc + dssm, d)
    wglu_b = wglu_g.reshape(dssm, dssm)
    mix = _gate_fwd(proj, y, conv_w_full, conv_b2, wglu_b, b_glu2, dc)
    loss_blk, dout, dob, dmix, gg_post = _outproj(mix, wout_b, xs, tgt, g_post2)

    def reduce_start(grads, tags, group):
        got = _rs_pair_exchange(grads, "rs_pair_exchange_" + group)
        sums = [_rs_pair_add(place, gt, rt, "rs_pair_add_" + tg) for gt, rt, tg in zip(grads, got, tags)]
        plan = _chip_exchange_plan(len(grads))
        started = _split_copy_start([s16 for s16, _ in sums],
                                    [((N_CHIPS - 1,) + s16.shape[1:], BF16) for s16, _ in sums], plan,
                                    (N_CHIPS - 1) * len(grads), "rs_chip_exchange_" + group + "_start")
        return plan, started, [s32 for _, s32 in sums]

    def reduce_finish(plan, started, sums32, tags, group, after):
        send, recv, srcs, lands, _ = started
        _, landed = _split_copy_wait(send, recv, srcs, lands, plan, after, "rs_chip_exchange_" + group + "_wait")
        mine = [_rs_chip_add(place, s32, qt, "rs_chip_add_" + tg) for s32, qt, tg in zip(sums32, landed, tags)]
        full = _rs_pair_share(mine, "rs_pair_share_" + group)
        return [f.reshape(2 * f.shape[1], f.shape[2]) for f in full]

    gw_out = _matmul_tn(mix, dob, 1, "grad_w_out")
    dproj, dy, gsmall, gw_glu = _gate_bwd(proj, y, dmix, conv_w_full, conv_b2, wglu_b, b_glu2, dc)
    rs_a = reduce_start([gw_out.reshape(N_CHIPS, 2, (dc + dssm) // (2 * N_CHIPS), d),
                         gw_glu.reshape(N_CHIPS, 2, dssm // (2 * N_CHIPS), dssm)], ["w_out", "w_glu"], "a")
    dproj, gb_dense, gc_dense, glam, gd = _ssm_bwd(proj, u_block, dy, cin, dproj, bq, cq, lam, pw,
                                                   d_skip2 + rs_a[1][4][0:1, 0:1])
    gw_in = _matmul_tn(hb, dproj, N_CHIPS, "grad_w_in")
    rs_b = reduce_start([gw_in.reshape(N_CHIPS, 2, d // 2, n_shard)], ["w_in"], "b")
    gx, gg_pre = _dh_prenorm_bwd(dproj, win_b, xs, dout, g_pre2 + rs_b[1][4][0:1, 0:1])

    gb4 = gb_dense.reshape(nq, SLAB, 2, SLAB_STATES)
    g_bbr2 = _blockdiag_take(gb4[:, :, 0, :], h, p).transpose(0, 1, 3, 2).reshape(g, p * h)
    g_bbi2 = _blockdiag_take(gb4[:, :, 1, :], h, p).transpose(0, 1, 3, 2).reshape(g, p * h)
    gc4 = gc_dense.reshape(nq, SLAB, 2, SLAB_STATES)
    g_c_re = _blockdiag_take(gc4[:, :, 0, :], h, p).reshape(g, h, p)
    g_c_im = -_blockdiag_take(gc4[:, :, 1, :], h, p).reshape(g, h, p)
    g_a_re, g_a_im, g_ld, g_b_re2, g_b_im2 = _zoh_bwd(
        ssm_a_re, ssm_a_im, log_dt2, b_re2, b_im2, expand,
        glam[0].reshape(g, p), glam[1].reshape(g, p), g_bbr2, g_bbi2)

    small_names = ["norm_pre_g", "conv_w", "conv_b", "ssm_a_re", "ssm_a_im", "ssm_log_dt", "ssm_b_re", "ssm_b_im",
                   "ssm_c_re", "ssm_c_im", "ssm_d", "b_glu", "norm_post_g", "loss"]
    small_g = {
        "norm_pre_g": gg_pre[0], "conv_w": gsmall[0:3], "conv_b": gsmall[3], "ssm_a_re": g_a_re, "ssm_a_im": g_a_im,
        "ssm_log_dt": g_ld.reshape(g), "ssm_b_re": g_b_re2.reshape(g, p, h), "ssm_b_im": g_b_im2.reshape(g, p, h),
        "ssm_c_re": g_c_re, "ssm_c_im": g_c_im, "ssm_d": gd[0], "b_glu": gsmall[4], "norm_post_g": gg_post[0],
        "loss": loss_blk[0, 0:1],
    }
    zeros_cw = jnp.zeros((conv_w.shape[0], dc), F32)
    one0 = jnp.zeros((1,), F32)
    small_w = dict(norm_pre_g=norm_pre_g, conv_w=zeros_cw, conv_b=conv_b, ssm_a_re=ssm_a_re, ssm_a_im=ssm_a_im,
                   ssm_log_dt=ssm_log_dt, ssm_b_re=ssm_b_re, ssm_b_im=ssm_b_im, ssm_c_re=ssm_c_re, ssm_c_im=ssm_c_im,
                   ssm_d=ssm_d, b_glu=b_glu, norm_post_g=norm_post_g, loss=one0)
    small_m = dict(norm_pre_g=m_norm_pre_g, conv_w=zeros_cw, conv_b=m_conv_b, ssm_a_re=m_ssm_a_re, ssm_a_im=m_ssm_a_im,
                   ssm_log_dt=m_ssm_log_dt, ssm_b_re=m_ssm_b_re, ssm_b_im=m_ssm_b_im, ssm_c_re=m_ssm_c_re,
                   ssm_c_im=m_ssm_c_im, ssm_d=m_ssm_d, b_glu=m_b_glu, norm_post_g=m_norm_post_g, loss=one0)
    small_v = dict(norm_pre_g=v_norm_pre_g, conv_w=zeros_cw, conv_b=v_conv_b, ssm_a_re=v_ssm_a_re, ssm_a_im=v_ssm_a_im,
                   ssm_log_dt=v_ssm_log_dt, ssm_b_re=v_ssm_b_re, ssm_b_im=v_ssm_b_im, ssm_c_re=v_ssm_c_re,
                   ssm_c_im=v_ssm_c_im, ssm_d=v_ssm_d, b_glu=v_b_glu, norm_post_g=v_norm_post_g, loss=one0)
    shapes = [small_w[nm].shape for nm in small_names]
    g_pack = _allreduce_small(_pack([small_g[nm] for nm in small_names]))
    packs = _adamw_small(g_pack, _pack([small_w[nm] for nm in small_names]),
                         _pack([small_m[nm] for nm in small_names]), _pack([small_v[nm] for nm in small_names]))
    sg, sd, sm, sv = [dict(zip(small_names, _unpack(pk, shapes))) for pk in (g_pack, *packs)]
    loss = sg["loss"][0]

    g_cw = lax.dynamic_slice_in_dim(sg["conv_w"], chip * cw_cols, cw_cols, axis=1)
    pad_cw = lambda a: jnp.zeros((SUBLANES, cw_pad), F32).at[:a.shape[0], :cw_cols].set(a)
    cut_cw = lambda a: a[:conv_w.shape[0], :cw_cols]
    d_cw, m_cw, v_cw = [cut_cw(a) for a in _adamw(pad_cw(conv_w), pad_cw(g_cw), pad_cw(m_conv_w), pad_cw(v_conv_w),
                                                  "adamw_conv_w")]

    g_wout, g_wglu = reduce_finish(*rs_a, ["w_out", "w_glu"], "a", [d_cw, packs[0]])
    (g_win,) = reduce_finish(*rs_b, ["w_in"], "b", [g_wout])
    d_win, m_win, v_win = _adamw(w_in, g_win, m_w_in, v_w_in, "adamw_w_in")
    d_wout, m_wout, v_wout = _adamw(w_out, g_wout, m_w_out, v_w_out, "adamw_w_out")
    d_wglu, m_wglu, v_wglu = _adamw(w_glu, g_wglu, m_w_glu, v_w_glu, "adamw_w_glu")

    order = ["norm_pre_g", "w_in", "conv_w", "conv_b", "ssm_a_re", "ssm_a_im", "ssm_log_dt", "ssm_b_re", "ssm_b_im",
             "ssm_c_re", "ssm_c_im", "ssm_d", "w_glu", "b_glu", "w_out", "norm_post_g"]
    grads, deltas, new_m, new_v = dict(sg), dict(sd), dict(sm), dict(sv)
    grads.update(w_in=g_win, w_out=g_wout, w_glu=g_wglu, conv_w=g_cw)
    deltas.update(w_in=d_win, w_out=d_wout, w_glu=d_wglu, conv_w=d_cw)
    new_m.update(w_in=m_win, w_out=m_wout, w_glu=m_wglu, conv_w=m_cw)
    new_v.update(w_in=v_win, w_out=v_wout, w_glu=v_wglu, conv_w=v_cw)
    return (loss, gx[None], *[grads[nm] for nm in order], *[deltas[nm] for nm in order],
            *[new_m[nm] for nm in order], *[new_v[nm] for nm in order])
```

```python
import functools
import math

import jax
import jax.numpy as jnp
from jax import lax
from jax.experimental import pallas as pl
from jax.experimental.pallas import tpu as pltpu

F32 = jnp.float32
BF16 = jnp.bfloat16
MESH = pl.DeviceIdType.MESH

EPS = 1e-6
SSM_H = 16
SSM_P = 64
GROUPS_PER_SLAB = 8
SLAB = 128
SLAB_STATES = GROUPS_PER_SLAB * SSM_P
N_CHIPS = 4
N_DEV = 8

ADAM_LR = 0.001
ADAM_B1 = 0.9
ADAM_B2 = 0.999
ADAM_EPS = 1e-08
ADAM_WD = 0.01
ADAM_STEP = 10

MIB = 1024 * 1024
VMEM_CAP = 48 * MIB
SUBLANES = 8

TM_NORM = 512
TM_PROJ = 512
TM_GATE = 256
TM_OUT = 256
TM_DH = 256
T_SCAN = 256
TK_TN = 512
SCAN_UNROLL = 4
TM_TN = 1024
TR_ELT = 256


def _cp(vmem_bytes, **kw):
    return pltpu.CompilerParams(vmem_limit_bytes=int(min(VMEM_CAP, max(16 * MIB, vmem_bytes))), **kw)


def _my_place():
    return lax.axis_index("x"), lax.axis_index("y"), lax.axis_index("c")


def _other_chips(x, y):
    return [(1 - x, y), (x, 1 - y), (1 - x, 1 - y)]


def _silu(z):
    s = jax.nn.sigmoid(z)
    return z * s, s


def _dsilu(z, s):
    return s * (1.0 + z * (1.0 - s))


_GELU_K = math.sqrt(2.0 / math.pi)
_GELU_C = 0.044715


def _gelu(y):
    th = jnp.tanh(_GELU_K * (y + _GELU_C * y * y * y))
    return 0.5 * y * (1.0 + th), th


def _dgelu(y, th):
    return 0.5 * (1.0 + th) + 0.5 * y * (1.0 - th * th) * _GELU_K * (1.0 + 3.0 * _GELU_C * y * y)


def _cast_bf16(w, name, after=None):
    r, c = w.shape
    tr = min(TR_ELT, r)
    extra = [] if after is None else [after]

    def body(w_ref, *rest):
        rest[-1][...] = w_ref[...].astype(BF16)

    return pl.pallas_call(
        body, name=name, grid=(r // tr,),
        in_specs=[pl.BlockSpec((tr, c), lambda i: (i, 0))] + [pl.BlockSpec((SUBLANES, SLAB), lambda i: (0, 0))] * len(extra),
        out_specs=pl.BlockSpec((tr, c), lambda i: (i, 0)),
        out_shape=jax.ShapeDtypeStruct((r, c), BF16),
        compiler_params=_cp(12 * tr * c),
    )(w, *extra)


def _prenorm(x, g):
    t, d = x.shape
    tm = min(TM_NORM, t)

    def body(x_ref, g_ref, h_ref):
        xv = x_ref[...]
        r = lax.rsqrt(jnp.mean(xv * xv, axis=-1, keepdims=True) + EPS)
        h_ref[...] = (xv * r * g_ref[...]).astype(BF16)

    return pl.pallas_call(
        body, name="prenorm", grid=(t // tm,),
        in_specs=[pl.BlockSpec((tm, d), lambda i: (i, 0)), pl.BlockSpec((1, d), lambda i: (0, 0))],
        out_specs=pl.BlockSpec((tm, d), lambda i: (i, 0)),
        out_shape=jax.ShapeDtypeStruct((t, d), BF16),
        compiler_params=_cp(20 * tm * d),
    )(x, g)


def _matmul_nn(a, b, name, after):
    t, k = a.shape
    nb, _, n = b.shape
    tm = min(TM_PROJ, t)

    def body(a_ref, b_ref, after_ref, o_ref):
        del after_ref
        o_ref[...] = jnp.dot(a_ref[...], b_ref[...], preferred_element_type=F32)

    return pl.pallas_call(
        body, name=name, grid=(nb, t // tm),
        in_specs=[pl.BlockSpec((tm, k), lambda j, i: (i, 0)), pl.BlockSpec((None, k, n), lambda j, i: (j, 0, 0)),
                  pl.BlockSpec((SUBLANES, SLAB), lambda j, i: (0, 0))],
        out_specs=pl.BlockSpec((tm, n), lambda j, i: (i, j)),
        out_shape=jax.ShapeDtypeStruct((t, nb * n), F32),
        compiler_params=_cp(2 * (2 * tm * k + 2 * k * n + 4 * tm * n) + 4 * MIB),
    )(a, b, after)


def _matmul_tn(a, b, nb, name):
    t, m = a.shape
    n = b.shape[1] // nb
    tk = min(TK_TN, t)
    tma = min(TM_TN, m)

    def body(a_ref, b_ref, o_ref):
        @pl.when(pl.program_id(2) == 0)
        def _():
            o_ref[...] = jnp.zeros_like(o_ref)

        o_ref[...] += lax.dot_general(a_ref[...], b_ref[...], (((0,), (0,)), ((), ())), preferred_element_type=F32)

    return pl.pallas_call(
        body, name=name, grid=(nb, m // tma, t // tk),
        in_specs=[pl.BlockSpec((tk, tma), lambda j, i, k: (k, i)), pl.BlockSpec((tk, n), lambda j, i, k: (k, j))],
        out_specs=pl.BlockSpec((None, tma, n), lambda j, i, k: (j, i, 0)),
        out_shape=jax.ShapeDtypeStruct((nb, m, n), F32),
        compiler_params=_cp(2 * (2 * tk * tma + 2 * tk * n + 4 * tma * n) + 8 * MIB),
    )(a, b)


def _outproj(mix, w_out, x, tgt, g_post):
    t, dm = mix.shape
    d = w_out.shape[1]
    tm = min(TM_OUT, t)

    def body(mix_ref, w_ref, x_ref, t_ref, g_ref, loss_ref, dout_ref, do_ref, dmix_ref, gg_ref):
        @pl.when(pl.program_id(0) == 0)
        def _():
            loss_ref[...] = jnp.zeros_like(loss_ref)
            gg_ref[...] = jnp.zeros_like(gg_ref)

        w = w_ref[...]
        o = jnp.dot(mix_ref[...], w, preferred_element_type=F32)
        r = lax.rsqrt(jnp.mean(o * o, axis=-1, keepdims=True) + EPS)
        nh = o * r
        g = g_ref[...]
        e = x_ref[...] + nh * g - t_ref[...]
        loss_ref[...] += jnp.sum(e * e) * (0.5 / d)
        dout = e * (1.0 / d)
        dout_ref[...] = dout
        gg_ref[0:1, :] += jnp.sum(dout * nh, axis=0, keepdims=True)
        dn = dout * g
        do = r * (dn - nh * jnp.mean(dn * nh, axis=-1, keepdims=True))
        dob = do.astype(BF16)
        do_ref[...] = dob
        dmix_ref[...] = lax.dot_general(dob, w, (((1,), (1,)), ((), ())), preferred_element_type=F32)

    row = lambda i: (i, 0)
    fixed = lambda i: (0, 0)
    return pl.pallas_call(
        body, name="outproj", grid=(t // tm,),
        in_specs=[pl.BlockSpec((tm, dm), row), pl.BlockSpec((dm, d), fixed), pl.BlockSpec((tm, d), row),
                  pl.BlockSpec((tm, d), row), pl.BlockSpec((1, d), fixed)],
        out_specs=[pl.BlockSpec((SUBLANES, SLAB), fixed), pl.BlockSpec((tm, d), row), pl.BlockSpec((tm, d), row),
                   pl.BlockSpec((tm, dm), row), pl.BlockSpec((SUBLANES, d), fixed)],
        out_shape=[jax.ShapeDtypeStruct((SUBLANES, SLAB), F32), jax.ShapeDtypeStruct((t, d), F32),
                   jax.ShapeDtypeStruct((t, d), BF16), jax.ShapeDtypeStruct((t, dm), F32),
                   jax.ShapeDtypeStruct((SUBLANES, d), F32)],
        compiler_params=_cp(2 * (2 * dm * d + tm * (2 * dm + 4 * d * 3 + 2 * d + 4 * dm)) + 16 * MIB),
    )(mix, w_out, x, tgt, g_post)


def _dh_prenorm_bwd(dproj, w_in, x, dout, g_pre):
    t, d = x.shape
    nb, _, n = w_in.shape
    tm = min(TM_DH, t)

    def body(dp_ref, w_ref, x_ref, dout_ref, g_ref, dx_ref, gg_ref):
        @pl.when(pl.program_id(0) == 0)
        def _():
            gg_ref[...] = jnp.zeros_like(gg_ref)

        dh = None
        for k in range(nb):
            part = lax.dot_general(dp_ref[:, k * n:(k + 1) * n], w_ref[k], (((1,), (1,)), ((), ())),
                                   preferred_element_type=F32)
            dh = part if dh is None else dh + part
        xv = x_ref[...]
        r = lax.rsqrt(jnp.mean(xv * xv, axis=-1, keepdims=True) + EPS)
        xh = xv * r
        gg_ref[0:1, :] += jnp.sum(dh * xh, axis=0, keepdims=True)
        dg = dh * g_ref[...]
        dx_ref[...] = dout_ref[...] + r * (dg - xh * jnp.mean(dg * xh, axis=-1, keepdims=True))

    row = lambda i: (i, 0)
    fixed = lambda i: (0, 0)
    w_spec = pl.BlockSpec(w_in.shape, lambda i: (0, 0, 0), pipeline_mode=pl.Buffered(1))
    return pl.pallas_call(
        body, name="dh_prenorm_bwd", grid=(t // tm,),
        in_specs=[pl.BlockSpec((tm, nb * n), row), w_spec,
                  pl.BlockSpec((tm, d), row), pl.BlockSpec((tm, d), row), pl.BlockSpec((1, d), fixed)],
        out_specs=[pl.BlockSpec((tm, d), row), pl.BlockSpec((SUBLANES, d), fixed)],
        out_shape=[jax.ShapeDtypeStruct((t, d), F32), jax.ShapeDtypeStruct((SUBLANES, d), F32)],
        compiler_params=_cp(2 * nb * d * n + 2 * (2 * tm * nb * n + 12 * tm * d) + 16 * tm * d + 4 * MIB),
    )(dproj, w_in, x, dout, g_pre)


def _adamw(w, g, m, v, name):
    r, c = w.shape
    tr = min(TR_ELT, r)
    c1 = 1.0 - ADAM_B1 ** ADAM_STEP
    c2 = 1.0 - ADAM_B2 ** ADAM_STEP

    def body(w_ref, g_ref, m_ref, v_ref, d_ref, mo_ref, vo_ref):
        gv = g_ref[...]
        mn = ADAM_B1 * m_ref[...] + (1.0 - ADAM_B1) * gv
        vn = ADAM_B2 * v_ref[...] + (1.0 - ADAM_B2) * (gv * gv)
        d_ref[...] = -ADAM_LR * ((mn / c1) / (jnp.sqrt(vn / c2) + ADAM_EPS) + ADAM_WD * w_ref[...])
        mo_ref[...] = mn
        vo_ref[...] = vn

    spec = pl.BlockSpec((tr, c), lambda i: (i, 0))
    sds = jax.ShapeDtypeStruct((r, c), F32)
    return pl.pallas_call(
        body, name=name, grid=(r // tr,), in_specs=[spec] * 4, out_specs=[spec] * 3, out_shape=[sds] * 3,
        compiler_params=_cp(2 * 7 * 4 * tr * c + 8 * MIB),
    )(w, g, m, v)


def _adamw_small(g, w, m, v):
    r, c = w.shape
    c1 = 1.0 - ADAM_B1 ** ADAM_STEP
    c2 = 1.0 - ADAM_B2 ** ADAM_STEP

    def body(g_ref, w_ref, m_ref, v_ref, d_ref, mo_ref, vo_ref):
        gv = g_ref[...]
        mn = ADAM_B1 * m_ref[...] + (1.0 - ADAM_B1) * gv
        vn = ADAM_B2 * v_ref[...] + (1.0 - ADAM_B2) * (gv * gv)
        d_ref[...] = -ADAM_LR * ((mn / c1) / (jnp.sqrt(vn / c2) + ADAM_EPS) + ADAM_WD * w_ref[...])
        mo_ref[...] = mn
        vo_ref[...] = vn

    sds = jax.ShapeDtypeStruct((r, c), F32)
    return pl.pallas_call(
        body, name="adamw_small", out_shape=[sds] * 3,
        compiler_params=_cp(12 * 4 * r * c + 8 * MIB),
    )(g, w, m, v)


def _allreduce_small(v):
    r, c = v.shape
    assert r % (2 * SUBLANES) == 0
    h = r // 2

    def body(v_ref, out_ref, got_ref, chip_ref, slots_ref, send, recv):
        x, y, cc = _my_place()
        k = 2 * x + y
        sib = (x, y, 1 - cc)

        def rc(s, src, dst, to):
            return pltpu.make_async_remote_copy(src_ref=src, dst_ref=dst, send_sem=send.at[s], recv_sem=recv.at[s],
                                                device_id=to, device_id_type=MESH)

        pair = rc(0, v_ref, got_ref, sib)
        pair.start()
        pair.wait()
        chip_ref[...] = v_ref[...] + got_ref[...]
        mine = pl.ds(pl.multiple_of(cc * h, SUBLANES), h)
        theirs = pl.ds(pl.multiple_of((1 - cc) * h, SUBLANES), h)
        chips = _other_chips(x, y)
        sent = []
        for j, (cx, cy) in enumerate(chips):
            cp = rc(1 + j, chip_ref.at[mine], slots_ref.at[k], (cx, cy, cc))
            cp.start()
            sent.append(cp)
        slots_ref[k] = chip_ref[mine, :]
        for j, (cx, cy) in enumerate(chips):
            rc(1 + j, chip_ref.at[mine], slots_ref.at[2 * cx + cy], (cx, cy, cc)).wait_recv()
        total = slots_ref[0]
        for kk in range(1, N_CHIPS):
            total = total + slots_ref[kk]
        out_ref[mine, :] = total
        for cp in sent:
            cp.wait_send()
        share = rc(N_CHIPS, out_ref.at[mine], out_ref.at[mine], sib)
        share.start()
        rc(N_CHIPS, out_ref.at[theirs], out_ref.at[theirs], sib).wait_recv()
        share.wait_send()

    vm = pl.BlockSpec(memory_space=pltpu.VMEM)
    return pl.pallas_call(
        body, name="allreduce_small", out_shape=jax.ShapeDtypeStruct((r, c), F32), in_specs=[vm], out_specs=vm,
        scratch_shapes=[pltpu.VMEM((r, c), F32), pltpu.VMEM((r, c), F32), pltpu.VMEM((N_CHIPS, h, c), F32),
                        pltpu.SemaphoreType.DMA((N_CHIPS + 1,)), pltpu.SemaphoreType.DMA((N_CHIPS + 1,))],
        compiler_params=_cp(6 * 4 * r * c + 8 * MIB),
    )(v)


def _zoh(a_re, a_im, log_dt, b_re2, b_im2, expand):
    dt = jnp.exp(log_dt)
    mag = jnp.exp(a_re * dt)
    lbr, lbi = mag * jnp.cos(a_im * dt), mag * jnp.sin(a_im * dt)
    nr, ni = lbr - 1.0, lbi
    den = a_re * a_re + a_im * a_im
    qr = (nr * a_re + ni * a_im) / den
    qi = (ni * a_re - nr * a_im) / den
    qr2 = jnp.dot(qr, expand, precision=lax.Precision.HIGHEST, preferred_element_type=F32)
    qi2 = jnp.dot(qi, expand, precision=lax.Precision.HIGHEST, preferred_element_type=F32)
    return lbr, lbi, qr2 * b_re2 - qi2 * b_im2, qr2 * b_im2 + qi2 * b_re2


def _zoh_fwd(a_re, a_im, log_dt, b_re2, b_im2, expand, power):
    g, p = a_re.shape

    def body(ar_ref, ai_ref, ld_ref, br_ref, bi_ref, e_ref, lbr_ref, lbi_ref, bbr_ref, bbi_ref, pw_ref):
        ar, ai, ld = ar_ref[...], ai_ref[...], ld_ref[...]
        lbr, lbi, bbr, bbi = _zoh(ar, ai, ld, br_ref[...], bi_ref[...], e_ref[...])
        lbr_ref[...], lbi_ref[...], bbr_ref[...], bbi_ref[...] = lbr, lbi, bbr, bbi
        dt = jnp.exp(ld) * float(power)
        mag = jnp.exp(ar * dt)
        pw_ref[0] = mag * jnp.cos(ai * dt)
        pw_ref[1] = mag * jnp.sin(ai * dt)

    gp = jax.ShapeDtypeStruct((g, p), F32)
    gph = jax.ShapeDtypeStruct(b_re2.shape, F32)
    return pl.pallas_call(
        body, name="zoh_fwd", out_shape=[gp, gp, gph, gph, jax.ShapeDtypeStruct((2, g, p), F32)],
        compiler_params=_cp(16 * MIB),
    )(a_re, a_im, log_dt, b_re2, b_im2, expand)


def _zoh_bwd(a_re, a_im, log_dt, b_re2, b_im2, expand, g_lbr, g_lbi, g_bbr, g_bbi):
    def body(ar_ref, ai_ref, ld_ref, br_ref, bi_ref, e_ref, c0, c1, c2, c3, gar, gai, gld, gbr, gbi):
        e = e_ref[...]
        _, vjp = jax.vjp(lambda a, b, c, d, f: _zoh(a, b, c, d, f, e),
                         ar_ref[...], ai_ref[...], ld_ref[...], br_ref[...], bi_ref[...])
        gar[...], gai[...], gld[...], gbr[...], gbi[...] = vjp((c0[...], c1[...], c2[...], c3[...]))

    sds = lambda a: jax.ShapeDtypeStruct(a.shape, F32)
    return pl.pallas_call(
        body, name="zoh_bwd", out_shape=[sds(a_re), sds(a_im), sds(log_dt), sds(b_re2), sds(b_im2)],
        compiler_params=_cp(16 * MIB),
    )(a_re, a_im, log_dt, b_re2, b_im2, expand, g_lbr, g_lbi, g_bbr, g_bbi)


def _blockdiag(blocks):
    nq, _, r, c = blocks.shape
    eye = jnp.eye(GROUPS_PER_SLAB, dtype=blocks.dtype)
    out = blocks[:, :, :, None, :] * eye[None, :, None, :, None]
    return out.reshape(nq, GROUPS_PER_SLAB * r, GROUPS_PER_SLAB * c)


def _blockdiag_take(dense, r, c):
    nq = dense.shape[0]
    d5 = dense.reshape(nq, GROUPS_PER_SLAB, r, GROUPS_PER_SLAB, c)
    return jnp.stack([d5[:, i, :, i, :] for i in range(GROUPS_PER_SLAB)], axis=1)


def _scan_slab(s_ref, row0, q, lam_ref, pw_ref, car_ref, steps, reverse, prev_ref=None, prev_row0=0, glam_ref=None):
    sign = -1.0 if reverse else 1.0
    half = SLAB_STATES // SLAB
    cols = [(q * 2 * half + m, q * 2 * half + half + m, q * SLAB_STATES + m * SLAB) for m in range(half)]
    nm = len(cols)
    full = (SUBLANES, SLAB)
    lam = [(jnp.broadcast_to(lam_ref[0:1, pl.ds(cl, SLAB)], full),
            jnp.broadcast_to(sign * lam_ref[1:2, pl.ds(cl, SLAB)], full)) for (_, _, cl) in cols]

    def step_rows(jj, base):
        j = (steps - 1 - jj) if reverse else jj
        return j, pl.ds(pl.multiple_of(base + j * SUBLANES, SUBLANES), SUBLANES)

    def pass1(jj, car):
        _, rows = step_rows(jj, row0)
        out = []
        for m, (cr, ci, _) in enumerate(cols):
            sr, si = car[2 * m], car[2 * m + 1]
            lr, li = lam[m]
            nr = lr * sr - li * si + s_ref[cr, rows, :]
            ni = lr * si + li * sr + s_ref[ci, rows, :]
            s_ref[cr, rows, :] = nr
            s_ref[ci, rows, :] = ni
            out += [nr, ni]
        return tuple(out)

    def unrolled(step_fn):
        def outer(jo, carry):
            for k in range(SCAN_UNROLL):
                carry = step_fn(jo * SCAN_UNROLL + k, carry)
            return carry
        return outer

    assert steps % SCAN_UNROLL == 0
    ends = lax.fori_loop(0, steps // SCAN_UNROLL, unrolled(pass1), tuple(jnp.zeros(full, F32) for _ in range(2 * nm)))

    entry = []
    for m, (cr, ci, cl) in enumerate(cols):
        ljr = pw_ref[0:1, pl.ds(cl, SLAB)]
        lji = sign * pw_ref[1:2, pl.ds(cl, SLAB)]
        c_r = car_ref[cr, 0:1, :]
        c_i = car_ref[ci, 0:1, :]
        rows_r, rows_i = [None] * SUBLANES, [None] * SUBLANES
        order = range(SUBLANES - 1, -1, -1) if reverse else range(SUBLANES)
        for b in order:
            rows_r[b], rows_i[b] = c_r, c_i
            e_r, e_i = ends[2 * m][b:b + 1], ends[2 * m + 1][b:b + 1]
            c_r, c_i = ljr * c_r - lji * c_i + e_r, ljr * c_i + lji * c_r + e_i
        car_ref[cr, 0:1, :] = c_r
        car_ref[ci, 0:1, :] = c_i
        entry.append((jnp.concatenate(rows_r, axis=0), jnp.concatenate(rows_i, axis=0)))

    def pass2(jj, carry):
        j, rows = step_rows(jj, row0)
        decayed, acc = carry[:2 * nm], carry[2 * nm:]
        out_d, out_a = [], []
        for m, (cr, ci, cl) in enumerate(cols):
            lr, li = lam[m]
            dr, di = decayed[2 * m], decayed[2 * m + 1]
            dr, di = lr * dr - li * di, lr * di + li * dr
            nr = s_ref[cr, rows, :] + dr
            ni = s_ref[ci, rows, :] + di
            s_ref[cr, rows, :] = nr
            s_ref[ci, rows, :] = ni
            out_d += [dr, di]
            if prev_ref is not None:
                prow = pl.ds(pl.multiple_of(prev_row0 + (j - 1) * SUBLANES, SUBLANES), SUBLANES)
                qr = prev_ref[cr, prow, :]
                qi = prev_ref[ci, prow, :]
                out_a += [acc[2 * m] + (nr * qr + ni * qi), acc[2 * m + 1] + (ni * qr - nr * qi)]
        return tuple(out_d) + tuple(out_a)

    n_acc = 2 * nm if prev_ref is not None else 0
    init = tuple(e for pair in entry for e in pair) + tuple(jnp.zeros(full, F32) for _ in range(n_acc))
    accs = lax.fori_loop(0, steps // SCAN_UNROLL, unrolled(pass2), init)[2 * nm:]
    if prev_ref is not None:
        for m, (_, _, cl) in enumerate(cols):
            glam_ref[0:1, pl.ds(cl, SLAB)] += jnp.sum(accs[2 * m], axis=0, keepdims=True)
            glam_ref[1:2, pl.ds(cl, SLAB)] += jnp.sum(accs[2 * m + 1], axis=0, keepdims=True)


def _permute_rows_f32(perm_bf16, v):
    hi = v.astype(BF16)
    lo = (v - hi.astype(F32)).astype(BF16)
    return (jnp.dot(perm_bf16, hi, preferred_element_type=F32) + jnp.dot(perm_bf16, lo, preferred_element_type=F32))


def _put_slab(s_ref, rows, q, val):
    per = 2 * SLAB_STATES // SLAB
    for i in range(per):
        s_ref[q * per + i, rows, :] = val[:, i * SLAB:(i + 1) * SLAB]


def _get_slab(s_ref, rows, q):
    per = 2 * SLAB_STATES // SLAB
    return jnp.concatenate([s_ref[q * per + i, rows, :] for i in range(per)], axis=1)


def _step_major_perm(tt):
    r = jnp.arange(tt)
    held = (r % SUBLANES) * (tt // SUBLANES) + r // SUBLANES
    return held[:, None] == r[None, :]


def _ssm_fwd(proj, u_block, bq, cq, lam, pw, d_skip):
    t = proj.shape[0]
    nq, ds, w2 = bq.shape
    assert ds == SLAB and w2 == 2 * SLAB_STATES
    dssm = nq * SLAB
    width = nq * w2
    ntile = width // SLAB
    tt = min(T_SCAN, t)
    steps = tt // SUBLANES
    perm = _step_major_perm(tt)
    pm, pmt = perm.astype(BF16), perm.T.astype(BF16)

    def body(u_ref, pm_ref, pmt_ref, bq_ref, cq_ref, lam_ref, pw_ref, d_ref, y_ref, cin_ref, s_ref, car_ref):
        @pl.when(pl.program_id(0) == 0)
        def _():
            car_ref[...] = jnp.zeros_like(car_ref)

        cin_ref[...] = jnp.broadcast_to(car_ref[:, 0:1, :], cin_ref.shape)
        u = u_ref[...]
        ub = jnp.dot(pm_ref[...], u.astype(BF16), preferred_element_type=F32).astype(BF16)
        everything = slice(None)
        for q in range(nq):
            _put_slab(s_ref, everything, q,
                      jnp.dot(ub[:, q * SLAB:(q + 1) * SLAB], bq_ref[q], preferred_element_type=F32))
        for q in range(nq):
            _scan_slab(s_ref, 0, q, lam_ref, pw_ref, car_ref, steps, reverse=False)
        y_sm = jnp.concatenate(
            [jnp.dot(_get_slab(s_ref, everything, q).astype(BF16), cq_ref[q], preferred_element_type=F32)
             for q in range(nq)], axis=1)
        y_ref[...] = _permute_rows_f32(pmt_ref[...], y_sm) + d_ref[...] * u

    c3 = lambda i: (0, 0, 0)
    c2 = lambda i: (0, 0)
    return pl.pallas_call(
        body, name="ssm_fwd", grid=(t // tt,),
        in_specs=[pl.BlockSpec((tt, dssm), lambda i: (i, u_block)), pl.BlockSpec((tt, tt), c2),
                  pl.BlockSpec((tt, tt), c2), pl.BlockSpec(bq.shape, c3),
                  pl.BlockSpec(cq.shape, c3), pl.BlockSpec(lam.shape, c2), pl.BlockSpec(pw.shape, c2),
                  pl.BlockSpec((1, dssm), c2)],
        out_specs=[pl.BlockSpec((tt, dssm), lambda i: (i, 0)),
                   pl.BlockSpec((None, ntile, SUBLANES, SLAB), lambda i: (i, 0, 0, 0))],
        out_shape=[jax.ShapeDtypeStruct((t, dssm), F32), jax.ShapeDtypeStruct((t // tt, ntile, SUBLANES, SLAB), F32)],
        scratch_shapes=[pltpu.VMEM((ntile, tt, SLAB), F32), pltpu.VMEM((ntile, SUBLANES, SLAB), F32)],
        compiler_params=_cp(2 * (8 * tt * dssm + 4 * nq * ds * w2) + 4 * tt * width + 16 * MIB,
                            dimension_semantics=("arbitrary",)),
    )(proj, pm, pmt, bq, cq, lam, pw, d_skip)


def _ssm_bwd(proj, u_block, dy, cin, dproj, bq, cq, lam, pw, d_skip):
    t = proj.shape[0]
    nq, ds, w2 = bq.shape
    dssm = nq * SLAB
    width = nq * w2
    ntile = width // SLAB
    tt = min(T_SCAN, t)
    nt = t // tt
    steps = tt // SUBLANES
    halo = SUBLANES
    perm = _step_major_perm(tt)
    pm, pmt = perm.astype(BF16), perm.T.astype(BF16)

    def body(u_ref, dy_ref, cin_ref, dp_any, pm_ref, pmt_ref, bq_ref, cq_ref, lam_ref, pw_ref, d_ref,
             du_ref, gb_ref, gc_ref, glam_ref, gd_ref, s_ref, gs_ref, car_f, car_b):
        del dp_any

        @pl.when(pl.program_id(0) == 0)
        def _():
            car_b[...] = jnp.zeros_like(car_b)
            gb_ref[...] = jnp.zeros_like(gb_ref)
            gc_ref[...] = jnp.zeros_like(gc_ref)
            glam_ref[...] = jnp.zeros_like(glam_ref)
            gd_ref[...] = jnp.zeros_like(gd_ref)

        u = u_ref[...]
        dyv = dy_ref[...]
        gd_ref[0:1, :] += jnp.sum(dyv * u, axis=0, keepdims=True)
        pmv = pm_ref[...]
        ub = jnp.dot(pmv, u.astype(BF16), preferred_element_type=F32).astype(BF16)
        dyb = jnp.dot(pmv, dyv.astype(BF16), preferred_element_type=F32).astype(BF16)
        car_f[...] = cin_ref[...]
        data = slice(halo, halo + tt)
        everything = slice(None)
        for q in range(nq):
            _put_slab(s_ref, data, q, jnp.dot(ub[:, q * SLAB:(q + 1) * SLAB], bq_ref[q], preferred_element_type=F32))
        for q in range(nq):
            _scan_slab(s_ref, halo, q, lam_ref, pw_ref, car_f, steps, reverse=False)
        last_step = s_ref[:, halo + tt - SUBLANES:halo + tt, :]
        s_ref[:, 0:halo, :] = jnp.concatenate([cin_ref[:, 0:1, :], last_step[:, 0:SUBLANES - 1, :]], axis=1)
        tn = (((0,), (0,)), ((), ()))
        nt_dims = (((1,), (1,)), ((), ()))
        for q in range(nq):
            sl = slice(q * SLAB, (q + 1) * SLAB)
            gc_ref[q] += lax.dot_general(dyb[:, sl], _get_slab(s_ref, data, q).astype(BF16), tn,
                                         preferred_element_type=F32)
            _put_slab(gs_ref, everything, q,
                      lax.dot_general(dyb[:, sl], cq_ref[q], nt_dims, preferred_element_type=F32))
        for q in range(nq):
            _scan_slab(gs_ref, 0, q, lam_ref, pw_ref, car_b, steps, reverse=True,
                       prev_ref=s_ref, prev_row0=halo, glam_ref=glam_ref)
        du_parts = []
        for q in range(nq):
            sl = slice(q * SLAB, (q + 1) * SLAB)
            gsb = _get_slab(gs_ref, everything, q).astype(BF16)
            du_parts.append(lax.dot_general(gsb, bq_ref[q], nt_dims, preferred_element_type=F32))
            gb_ref[q] += lax.dot_general(ub[:, sl], gsb, tn, preferred_element_type=F32)
        du_sm = jnp.concatenate(du_parts, axis=1)
        du_ref[...] = (_permute_rows_f32(pmt_ref[...], du_sm) + dyv * d_ref[...]).astype(BF16)

    c3 = lambda i: (0, 0, 0)
    c2 = lambda i: (0, 0)
    rev = lambda i: (nt - 1 - i, 0)
    dense = jax.ShapeDtypeStruct((nq, SLAB, w2), F32)
    gp = lam.shape[1]
    return pl.pallas_call(
        body, name="ssm_bwd", grid=(nt,),
        in_specs=[pl.BlockSpec((tt, dssm), lambda i: (nt - 1 - i, u_block)), pl.BlockSpec((tt, dssm), rev),
                  pl.BlockSpec((None, ntile, SUBLANES, SLAB), lambda i: (nt - 1 - i, 0, 0, 0)),
                  pl.BlockSpec(memory_space=pl.ANY), pl.BlockSpec((tt, tt), c2), pl.BlockSpec((tt, tt), c2),
                  pl.BlockSpec(bq.shape, c3), pl.BlockSpec(cq.shape, c3),
                  pl.BlockSpec(lam.shape, c2), pl.BlockSpec(pw.shape, c2), pl.BlockSpec((1, dssm), c2)],
        out_specs=[pl.BlockSpec((tt, dssm), lambda i: (nt - 1 - i, u_block)), pl.BlockSpec(dense.shape, c3),
                   pl.BlockSpec(dense.shape, c3), pl.BlockSpec((SUBLANES, gp), c2), pl.BlockSpec((SUBLANES, dssm), c2)],
        out_shape=[jax.ShapeDtypeStruct(dproj.shape, dproj.dtype), dense, dense,
                   jax.ShapeDtypeStruct((SUBLANES, gp), F32), jax.ShapeDtypeStruct((SUBLANES, dssm), F32)],
        scratch_shapes=[pltpu.VMEM((ntile, tt + halo, SLAB), F32), pltpu.VMEM((ntile, tt, SLAB), F32),
                        pltpu.VMEM((ntile, SUBLANES, SLAB), F32), pltpu.VMEM((ntile, SUBLANES, SLAB), F32)],
        input_output_aliases={3: 0},
        compiler_params=_cp(2 * (10 * tt * dssm + 4 * nq * ds * w2 + 8 * nq * SLAB * w2)
                            + 8 * tt * width + 12 * MIB, dimension_semantics=("arbitrary",)),
    )(proj, dy, cin, dproj, pm, pmt, bq, cq, lam, pw, d_skip)


def _gate_fwd(proj, y, conv_w, conv_b, w_glu, b_glu, dc):
    t = proj.shape[0]
    dssm = y.shape[1]
    assert dc == dssm
    tm = min(TM_GATE, t)
    halo = SUBLANES

    def body(b_ref, c_ref, v_ref, zc_ref, zs_ref, y_ref, cw_ref, cb_ref, wg_ref, bg_ref, mix_ref, cv_buf):
        @pl.when(pl.program_id(0) == 0)
        def _():
            cv_buf[0:halo, :] = jnp.zeros((halo, dc), F32)

        cv = c_ref[...] * v_ref[...]
        cv_buf[halo:, :] = cv
        conv = (cb_ref[...] + cw_ref[2:3, :] * cv + cw_ref[1:2, :] * cv_buf[halo - 1:halo - 1 + tm, :]
                + cw_ref[0:1, :] * cv_buf[halo - 2:halo - 2 + tm, :])
        sz, _ = _silu(zc_ref[...])
        mix_ref[:, 0:dc] = (b_ref[...] * conv * sz).astype(BF16)
        cv_buf[0:halo, :] = cv_buf[tm:tm + halo, :]
        ge, _ = _gelu(y_ref[...])
        gl = jnp.dot(ge.astype(BF16), wg_ref[...], preferred_element_type=F32) + bg_ref[...]
        szs, _ = _silu(zs_ref[...])
        mix_ref[:, dc:] = (ge * jax.nn.sigmoid(gl) * szs).astype(BF16)

    col = lambda j: pl.BlockSpec((tm, dc), lambda i, j=j: (i, j))
    fixed = lambda i: (0, 0)
    return pl.pallas_call(
        body, name="gate_fwd", grid=(t // tm,),
        in_specs=[col(0), col(1), col(2), col(3), col(5), pl.BlockSpec((tm, dssm), lambda i: (i, 0)),
                  pl.BlockSpec(conv_w.shape, fixed), pl.BlockSpec((1, dc), fixed),
                  pl.BlockSpec(w_glu.shape, fixed), pl.BlockSpec((1, dssm), fixed)],
        out_specs=pl.BlockSpec((tm, dc + dssm), lambda i: (i, 0)),
        out_shape=jax.ShapeDtypeStruct((t, dc + dssm), BF16),
        scratch_shapes=[pltpu.VMEM((tm + halo, dc), F32)],
        compiler_params=_cp(2 * (6 * 4 * tm * dc + 2 * tm * (dc + dssm) + 2 * dssm * dssm) + 24 * tm * dc + 8 * MIB,
                            dimension_semantics=("arbitrary",)),
    )(proj, proj, proj, proj, proj, y, conv_w, conv_b, w_glu, b_glu)


def _gate_bwd(proj, y, dmix, conv_w, conv_b, w_glu, b_glu, dc):
    t = proj.shape[0]
    dssm = y.shape[1]
    tm = min(TM_GATE, t)
    nt = t // tm
    halo = SUBLANES
    blocks_per_tile = tm // halo

    def body(b_ref, c_ref, v_ref, zc_ref, zs_ref, cp_ref, vp_ref, y_ref, dm_ref, cw_ref, cb_ref, wg_ref, bg_ref,
             dp_ref, dy_ref, gs_ref, gwg_ref, cv_buf, dc_buf):
        i = pl.program_id(0)

        @pl.when(i == 0)
        def _():
            dc_buf[tm:, :] = jnp.zeros((halo, dc), F32)
            gs_ref[...] = jnp.zeros_like(gs_ref)
            gwg_ref[...] = jnp.zeros_like(gwg_ref)

        first_tile = (i == nt - 1)
        bv, cg, vv, zc = b_ref[...], c_ref[...], v_ref[...], zc_ref[...]
        cv = cg * vv
        cv_buf[0:halo, :] = jnp.where(first_tile, 0.0, cp_ref[...] * vp_ref[...])
        cv_buf[halo:, :] = cv
        w0, w1, w2 = cw_ref[0:1, :], cw_ref[1:2, :], cw_ref[2:3, :]
        conv = (cb_ref[...] + w2 * cv + w1 * cv_buf[halo - 1:halo - 1 + tm, :]
                + w0 * cv_buf[halo - 2:halo - 2 + tm, :])
        sz, sgc = _silu(zc)
        dyc = dm_ref[:, 0:dc]
        dp_ref[:, 0:dc] = (dyc * conv * sz).astype(BF16)
        dp_ref[:, 3 * dc:4 * dc] = (dyc * bv * conv * _dsilu(zc, sgc)).astype(BF16)
        dconv = dyc * bv * sz
        dc_buf[0:tm, :] = dconv
        d1 = dc_buf[1:1 + tm, :]
        d2 = dc_buf[2:2 + tm, :]
        dcv = w2 * dconv + w1 * d1 + w0 * d2
        dp_ref[:, dc:2 * dc] = (dcv * vv).astype(BF16)
        dp_ref[:, 2 * dc:3 * dc] = (dcv * cg).astype(BF16)
        gs_ref[0:1, :] += jnp.sum(cv * d2, axis=0, keepdims=True)
        gs_ref[1:2, :] += jnp.sum(cv * d1, axis=0, keepdims=True)
        gs_ref[2:3, :] += jnp.sum(cv * dconv, axis=0, keepdims=True)
        gs_ref[3:4, :] += jnp.sum(dconv, axis=0, keepdims=True)
        dc_buf[tm:, :] = dc_buf[0:halo, :]

        yv, zs = y_ref[...], zs_ref[...]
        ge, th = _gelu(yv)
        geb = ge.astype(BF16)
        wg = wg_ref[...]
        gl = jnp.dot(geb, wg, preferred_element_type=F32) + bg_ref[...]
        sg = jax.nn.sigmoid(gl)
        szs, sgs = _silu(zs)
        dys = dm_ref[:, dc:]
        ys = ge * sg
        dp_ref[:, 4 * dc:5 * dc] = jnp.zeros((tm, dc), BF16)
        dp_ref[:, 5 * dc:] = (dys * ys * _dsilu(zs, sgs)).astype(BF16)
        d_ys = dys * szs
        dgl = d_ys * ge * sg * (1.0 - sg)
        dglb = dgl.astype(BF16)
        gs_ref[4:5, :] += jnp.sum(dgl, axis=0, keepdims=True)
        gwg_ref[...] += lax.dot_general(geb, dglb, (((0,), (0,)), ((), ())), preferred_element_type=F32)
        dge = d_ys * sg + lax.dot_general(dglb, wg, (((1,), (1,)), ((), ())), preferred_element_type=F32)
        dy_ref[...] = dge * _dgelu(yv, th)

    col = lambda j: pl.BlockSpec((tm, dc), lambda i, j=j: (nt - 1 - i, j))
    prev = lambda j: pl.BlockSpec((halo, dc), lambda i, j=j: (jnp.maximum((nt - 1 - i) * blocks_per_tile - 1, 0), j))
    rev = lambda i: (nt - 1 - i, 0)
    fixed = lambda i: (0, 0)
    return pl.pallas_call(
        body, name="gate_bwd", grid=(nt,),
        in_specs=[col(0), col(1), col(2), col(3), col(5), prev(1), prev(2), pl.BlockSpec((tm, dssm), rev),
                  pl.BlockSpec((tm, dc + dssm), rev), pl.BlockSpec(conv_w.shape, fixed), pl.BlockSpec((1, dc), fixed),
                  pl.BlockSpec(w_glu.shape, fixed), pl.BlockSpec((1, dssm), fixed)],
        out_specs=[pl.BlockSpec((tm, 6 * dc), rev), pl.BlockSpec((tm, dssm), rev),
                   pl.BlockSpec((2 * SUBLANES, dc), fixed), pl.BlockSpec((dssm, dssm), fixed)],
        out_shape=[jax.ShapeDtypeStruct((t, 6 * dc), BF16), jax.ShapeDtypeStruct((t, dssm), F32),
                   jax.ShapeDtypeStruct((2 * SUBLANES, dc), F32), jax.ShapeDtypeStruct((dssm, dssm), F32)],
        scratch_shapes=[pltpu.VMEM((tm + halo, dc), F32), pltpu.VMEM((tm + halo, dc), F32)],
        compiler_params=_cp(2 * (6 * 4 * tm * dc + 8 * tm * dc + 12 * tm * dc + 4 * tm * dc + 6 * dssm * dssm)
                            + 40 * tm * dc + 8 * MIB, dimension_semantics=("arbitrary",)),
    )(proj, proj, proj, proj, proj, proj, proj, y, dmix, conv_w, conv_b, w_glu, b_glu)


def _allgather_halves(parts, name):
    n = len(parts)
    per = 2 * (N_CHIPS - 1)

    def body(*refs):
        ins, outs = refs[:n], refs[n:2 * n]
        send, recv, local = refs[2 * n:]
        x, y, c = _my_place()
        k = 2 * x + y
        sib = (x, y, 1 - c)
        chips = _other_chips(x, y)

        def rc(t, s, src, dst, to):
            return pltpu.make_async_remote_copy(src_ref=src, dst_ref=dst, send_sem=send.at[t * per + s],
                                                recv_sem=recv.at[t * per + s], device_id=to, device_id_type=MESH)

        mine = [pltpu.make_async_copy(ins[t], outs[t].at[k], local.at[t]) for t in range(n)]
        for cp in mine:
            cp.start()
        sent = []
        for t in range(n):
            for j, (cx, cy) in enumerate(chips):
                cp = rc(t, j, ins[t].at[c], outs[t].at[k, c], (cx, cy, c))
                cp.start()
                sent.append(cp)
        for t in range(n):
            for j, (cx, cy) in enumerate(chips):
                landed = outs[t].at[2 * cx + cy, c]
                rc(t, j, landed, landed, (cx, cy, c)).wait_recv()
                cp = rc(t, N_CHIPS - 1 + j, landed, landed, sib)
                cp.start()
                sent.append(cp)
        for t in range(n):
            for j, (cx, cy) in enumerate(chips):
                other = outs[t].at[2 * cx + cy, 1 - c]
                rc(t, N_CHIPS - 1 + j, other, other, sib).wait_recv()
        for cp in sent:
            cp.wait_send()
        for cp in mine:
            cp.wait()

    any_spec = pl.BlockSpec(memory_space=pl.ANY)
    return pl.pallas_call(
        body, name=name, in_specs=[any_spec] * n, out_specs=[any_spec] * n,
        out_shape=[jax.ShapeDtypeStruct((N_CHIPS,) + p.shape, p.dtype) for p in parts],
        scratch_shapes=[pltpu.SemaphoreType.DMA((n * per,)), pltpu.SemaphoreType.DMA((n * per,)),
                        pltpu.SemaphoreType.DMA((n,))],
        compiler_params=_cp(16 * MIB),
    )(*parts)


def _allgather_flat(v, name):
    r, c = v.shape
    rels = [(dx, dy, dc) for dx in (0, 1) for dy in (0, 1) for dc in (0, 1)][1:]

    def body(v_ref, out_ref, send, recv):
        x, y, cc = _my_place()
        me = 4 * x + 2 * y + cc

        def peer(rel):
            dx, dy, dc = rel
            return (1 - x if dx else x, 1 - y if dy else y, 1 - cc if dc else cc)

        def rc(s, slot, to):
            return pltpu.make_async_remote_copy(src_ref=v_ref, dst_ref=out_ref.at[slot], send_sem=send.at[s],
                                                recv_sem=recv.at[s], device_id=to, device_id_type=MESH)

        sent = []
        for s, rel in enumerate(rels):
            cp = rc(s, me, peer(rel))
            cp.start()
            sent.append(cp)
        out_ref[me] = v_ref[...]
        for s, rel in enumerate(rels):
            px, py, pc = peer(rel)
            rc(s, 4 * px + 2 * py + pc, (px, py, pc)).wait_recv()
        for cp in sent:
            cp.wait_send()

    return pl.pallas_call(
        body, name=name, out_shape=jax.ShapeDtypeStruct((N_DEV, r, c), F32),
        in_specs=[pl.BlockSpec(memory_space=pltpu.VMEM)], out_specs=pl.BlockSpec(memory_space=pltpu.VMEM),
        scratch_shapes=[pltpu.SemaphoreType.DMA((N_DEV - 1,)), pltpu.SemaphoreType.DMA((N_DEV - 1,))],
        compiler_params=_cp((N_DEV + 2) * 4 * r * c + 8 * MIB),
    )(v)


def _rs_pair_exchange(grads, name):
    n = len(grads)

    def body(*refs):
        ins, outs = refs[:n], refs[n:2 * n]
        send, recv = refs[2 * n:]
        x, y, c = _my_place()
        cps = []
        for t in range(n):
            cp = pltpu.make_async_remote_copy(src_ref=ins[t].at[:, 1 - c], dst_ref=outs[t], send_sem=send.at[t],
                                              recv_sem=recv.at[t], device_id=(x, y, 1 - c), device_id_type=MESH)
            cp.start()
            cps.append(cp)
        for cp in cps:
            cp.wait()

    any_spec = pl.BlockSpec(memory_space=pl.ANY)
    return pl.pallas_call(
        body, name=name, in_specs=[any_spec] * n, out_specs=[any_spec] * n,
        out_shape=[jax.ShapeDtypeStruct((g.shape[0],) + g.shape[2:], F32) for g in grads],
        scratch_shapes=[pltpu.SemaphoreType.DMA((n,)), pltpu.SemaphoreType.DMA((n,))],
        compiler_params=_cp(16 * MIB),
    )(*grads)


def _rs_pair_add(place, grad, got, name):
    nk, _, r2, c = grad.shape
    tr = min(TR_ELT, r2)

    def body(place_ref, g_ref, r_ref, o16_ref, o32_ref):
        del place_ref
        s = g_ref[...] + r_ref[...]
        o32_ref[...] = s
        o16_ref[...] = s.astype(BF16)

    blk = pl.BlockSpec((None, tr, c), lambda k, i, p: (k, i, 0))
    return pl.pallas_call(
        body, name=name,
        grid_spec=pltpu.PrefetchScalarGridSpec(
            num_scalar_prefetch=1, grid=(nk, r2 // tr),
            in_specs=[pl.BlockSpec((None, None, tr, c), lambda k, i, p: (k, p[0], i, 0)), blk],
            out_specs=[blk, blk]),
        out_shape=[jax.ShapeDtypeStruct((nk, r2, c), BF16), jax.ShapeDtypeStruct((nk, r2, c), F32)],
        compiler_params=_cp(2 * 14 * tr * c + 8 * MIB),
    )(place, grad, got)


_HBM_SPEC = pl.BlockSpec(memory_space=pltpu.HBM)
_SEM_SPEC = pl.BlockSpec(memory_space=pltpu.SEMAPHORE)
_DATAFLOW = pltpu.SideEffectType.DATAFLOW_SIDE_EFFECTING


def _split_copy_start(srcs, land_shapes, plan, n_sems, name):
    ns, nl = len(srcs), len(land_shapes)

    def body(*refs):
        src_refs, land_refs = refs[:ns], refs[ns:ns + nl]
        send, recv = refs[ns + nl], refs[ns + nl + 1]
        token = refs[-1]
        sends, _ = plan(src_refs, land_refs)
        for src, dst, to, si, ri in sends:
            pltpu.make_async_remote_copy(src_ref=src, dst_ref=dst, send_sem=send.at[si], recv_sem=recv.at[ri],
                                         device_id=to, device_id_type=MESH).start()
        token[...] = jnp.zeros_like(token)

    lands = [lax.empty(shp, dt) for shp, dt in land_shapes]
    through = [pltpu.HBM(a.shape, a.dtype) for a in srcs] + [pltpu.HBM(shp, dt) for shp, dt in land_shapes]
    out = pl.pallas_call(
        body, name=name,
        out_shape=(pltpu.SemaphoreType.DMA((n_sems,)), pltpu.SemaphoreType.DMA((n_sems,)), *through,
                   jax.ShapeDtypeStruct((SUBLANES, SLAB), F32)),
        in_specs=[_HBM_SPEC] * (ns + nl),
        out_specs=(_SEM_SPEC, _SEM_SPEC, *([_HBM_SPEC] * (ns + nl)), pl.BlockSpec(memory_space=pltpu.VMEM)),
        input_output_aliases={i: 2 + i for i in range(ns + nl)},
        compiler_params=pltpu.CompilerParams(has_side_effects=_DATAFLOW),
    )(*[pltpu.with_memory_space_constraint(a, pltpu.HBM) for a in (*srcs, *lands)])
    return out[0], out[1], list(out[2:2 + ns]), list(out[2 + ns:2 + ns + nl]), out[-1]


def _split_copy_wait(send, recv, srcs, lands, plan, after, name):
    ns, nl, na = len(srcs), len(lands), len(after)

    def body(*refs):
        src_refs, land_refs = refs[:ns], refs[ns:ns + nl]
        send_ref, recv_ref = refs[ns + nl], refs[ns + nl + 1]
        sends, arrivals = plan(src_refs, land_refs)
        for src, dst, to, si, ri in sends:
            pltpu.make_async_remote_copy(src_ref=src, dst_ref=dst, send_sem=send_ref.at[si], recv_sem=recv_ref.at[ri],
                                         device_id=to, device_id_type=MESH).wait_send()
        for (src, _, to, si, _), (view, ri) in zip(sends, arrivals):
            pltpu.make_async_remote_copy(src_ref=view, dst_ref=view, send_sem=send_ref.at[si], recv_sem=recv_ref.at[ri],
                                         device_id=to, device_id_type=MESH).wait_recv()

    out = pl.pallas_call(
        body, name=name,
        out_shape=[pltpu.HBM(a.shape, a.dtype) for a in (*srcs, *lands)],
        in_specs=[_HBM_SPEC] * (ns + nl) + [_SEM_SPEC, _SEM_SPEC] + [pl.BlockSpec(memory_space=pl.ANY)] * na,
        out_specs=[_HBM_SPEC] * (ns + nl),
        input_output_aliases={i: i for i in range(ns + nl)},
        compiler_params=pltpu.CompilerParams(has_side_effects=_DATAFLOW),
    )(*srcs, *lands, send, recv, *after)
    return list(out[:ns]), list(out[ns:])


def _chip_exchange_plan(n):
    per = N_CHIPS - 1

    def plan(srcs, lands):
        x, y, c = _my_place()
        sends, arrivals = [], []
        for t in range(n):
            for j, (cx, cy) in enumerate(_other_chips(x, y)):
                sends.append((srcs[t].at[2 * cx + cy], lands[t].at[j], (cx, cy, c), t * per + j, t * per + j))
                arrivals.append((lands[t].at[j], t * per + j))
        return sends, arrivals

    return plan


def _gather_half_plan(n):
    per = N_CHIPS - 1

    def plan(srcs, lands):
        x, y, c = _my_place()
        k = 2 * x + y
        sends, arrivals = [], []
        for t in range(n):
            for j, (cx, cy) in enumerate(_other_chips(x, y)):
                sends.append((srcs[t].at[c], lands[t].at[k, c], (cx, cy, c), t * per + j, t * per + j))
                arrivals.append((lands[t].at[2 * cx + cy, c], t * per + j))
        return sends, arrivals

    return plan


def _gather_complete(own, landed, name):
    def body(own_ref, in_ref, out_ref, send, recv, local):
        x, y, c = _my_place()
        sib = (x, y, 1 - c)
        mine = pltpu.make_async_copy(own_ref, out_ref.at[2 * x + y], local)
        mine.start()
        sent = []
        for j, (cx, cy) in enumerate(_other_chips(x, y)):
            half = (2 * cx + cy, c)
            cp = pltpu.make_async_remote_copy(src_ref=in_ref.at[half], dst_ref=out_ref.at[half], send_sem=send.at[j],
                                              recv_sem=recv.at[j], device_id=sib, device_id_type=MESH)
            cp.start()
            sent.append(cp)
        for j, (cx, cy) in enumerate(_other_chips(x, y)):
            other = out_ref.at[2 * cx + cy, 1 - c]
            pltpu.make_async_remote_copy(src_ref=other, dst_ref=other, send_sem=send.at[j], recv_sem=recv.at[j],
                                         device_id=sib, device_id_type=MESH).wait_recv()
        for cp in sent:
            cp.wait_send()
        mine.wait()

    any_spec = pl.BlockSpec(memory_space=pl.ANY)
    per = N_CHIPS - 1
    return pl.pallas_call(
        body, name=name, in_specs=[any_spec, any_spec], out_specs=any_spec,
        out_shape=jax.ShapeDtypeStruct(landed.shape, landed.dtype), input_output_aliases={1: 0},
        scratch_shapes=[pltpu.SemaphoreType.DMA((per,)), pltpu.SemaphoreType.DMA((per,)), pltpu.SemaphoreType.DMA],
        compiler_params=_cp(16 * MIB),
    )(own, landed)


def _gather_direct_plan(n):
    per = 2 * (N_CHIPS - 1)

    def plan(srcs, lands):
        x, y, c = _my_place()
        k = 2 * x + y
        sends, arrivals = [], []
        for t in range(n):
            for j, (cx, cy) in enumerate(_other_chips(x, y)):
                for core in (0, 1):
                    sends.append((srcs[t].at[c], lands[t].at[k, c], (cx, cy, core),
                                  t * per + 2 * j + core, t * per + 2 * j + c))
                    arrivals.append((lands[t].at[2 * cx + cy, core], t * per + 2 * j + core))
        return sends, arrivals

    return plan


def _rs_chip_add(place, sums32, got, name):
    _, r2, c = sums32.shape
    tr = min(TR_ELT, r2)

    def body(place_ref, s_ref, q_ref, o_ref):
        del place_ref
        o_ref[...] = ((s_ref[...] + q_ref[0].astype(F32)) + q_ref[1].astype(F32)) + q_ref[2].astype(F32)

    return pl.pallas_call(
        body, name=name,
        grid_spec=pltpu.PrefetchScalarGridSpec(
            num_scalar_prefetch=1, grid=(r2 // tr,),
            in_specs=[pl.BlockSpec((None, tr, c), lambda i, p: (p[1], i, 0)),
                      pl.BlockSpec((N_CHIPS - 1, tr, c), lambda i, p: (0, i, 0))],
            out_specs=pl.BlockSpec((tr, c), lambda i, p: (i, 0))),
        out_shape=jax.ShapeDtypeStruct((r2, c), F32),
        compiler_params=_cp(2 * 14 * tr * c + 8 * MIB),
    )(place, sums32, got)


def _rs_pair_share(halves, name):
    n = len(halves)

    def body(*refs):
        ins, outs = refs[:n], refs[n:2 * n]
        send, recv, local = refs[2 * n:]
        x, y, c = _my_place()
        cps, mine = [], []
        for t in range(n):
            lc = pltpu.make_async_copy(ins[t], outs[t].at[c], local.at[t])
            lc.start()
            mine.append(lc)
            cp = pltpu.make_async_remote_copy(src_ref=ins[t], dst_ref=outs[t].at[c], send_sem=send.at[t],
                                              recv_sem=recv.at[t], device_id=(x, y, 1 - c), device_id_type=MESH)
            cp.start()
            cps.append(cp)
        for t in range(n):
            other = outs[t].at[1 - c]
            pltpu.make_async_remote_copy(src_ref=other, dst_ref=other, send_sem=send.at[t], recv_sem=recv.at[t],
                                         device_id=(x, y, 1 - c), device_id_type=MESH).wait_recv()
        for cp in cps:
            cp.wait_send()
        for lc in mine:
            lc.wait()

    any_spec = pl.BlockSpec(memory_space=pl.ANY)
    return pl.pallas_call(
        body, name=name, in_specs=[any_spec] * n, out_specs=[any_spec] * n,
        out_shape=[jax.ShapeDtypeStruct((2,) + h.shape, F32) for h in halves],
        scratch_shapes=[pltpu.SemaphoreType.DMA((n,)), pltpu.SemaphoreType.DMA((n,)), pltpu.SemaphoreType.DMA((n,))],
        compiler_params=_cp(16 * MIB),
    )(*halves)


_PACK_TILE = SUBLANES * SLAB


def _pack(arrays):
    rows = []
    for a in arrays:
        flat = a.reshape(-1).astype(F32)
        padded = -(-flat.shape[0] // _PACK_TILE) * _PACK_TILE
        rows.append(jnp.pad(flat, (0, padded - flat.shape[0])).reshape(-1, SLAB))
    n_rows = sum(r.shape[0] for r in rows)
    if n_rows % (2 * SUBLANES):
        rows.append(jnp.zeros((SUBLANES, SLAB), F32))
    return jnp.concatenate(rows, axis=0)


def _unpack(packed, shapes):
    out, row = [], 0
    for shp in shapes:
        size = math.prod(shp)
        nrow = -(-size // _PACK_TILE) * SUBLANES
        out.append(packed[row:row + nrow].reshape(-1)[:size].reshape(shp))
        row += nrow
    return out


def kernel(x, norm_pre_g, w_in, conv_w, conv_b, ssm_a_re, ssm_a_im, ssm_log_dt, ssm_b_re, ssm_b_im, ssm_c_re, ssm_c_im, ssm_d, w_glu, b_glu, w_out, norm_post_g, loss_target, m_norm_pre_g, m_w_in, m_conv_w, m_conv_b, m_ssm_a_re, m_ssm_a_im, m_ssm_log_dt, m_ssm_b_re, m_ssm_b_im, m_ssm_c_re, m_ssm_c_im, m_ssm_d, m_w_glu, m_b_glu, m_w_out, m_norm_post_g, v_norm_pre_g, v_w_in, v_conv_w, v_conv_b, v_ssm_a_re, v_ssm_a_im, v_ssm_log_dt, v_ssm_b_re, v_ssm_b_im, v_ssm_c_re, v_ssm_c_im, v_ssm_d, v_w_glu, v_b_glu, v_w_out, v_norm_post_g):
    xs, tgt = x[0], loss_target[0]
    t, d = xs.shape
    dc = conv_b.shape[0]
    dssm = ssm_d.shape[0]
    g, p = ssm_a_re.shape
    h = SSM_H
    nq = dssm // SLAB
    n_shard = w_in.shape[1]
    steps = min(T_SCAN, t) // SUBLANES
    mx, my, mc = _my_place()
    chip = 2 * mx + my
    place = jnp.stack([mc, chip]).astype(jnp.int32)

    halves = lambda a: a.reshape(2, a.shape[0] // 2, a.shape[1])
    win_half = halves(_cast_bf16(w_in, "cast_w_in"))
    win_plan = _gather_half_plan(1)
    win_send, win_recv, win_srcs, win_lands, win_token = _split_copy_start(
        [win_half], [((N_CHIPS,) + win_half.shape, BF16)], win_plan, N_CHIPS - 1, "gather_w_in_start")
    behind_w_in = win_token[0:1, 0:1]

    cw_cols = conv_w.shape[1]
    cw_pad = -(-cw_cols // SLAB) * SLAB
    cw_blk = jnp.zeros((SUBLANES, cw_pad), F32).at[:conv_w.shape[0], :cw_cols].set(conv_w) + behind_w_in
    cw_all = _allgather_flat(cw_blk, "allgather_conv_w")
    conv_w_full = jnp.concatenate([cw_all[2 * k, :, :cw_cols] for k in range(N_CHIPS)], axis=1)

    expand = jnp.repeat(jnp.eye(p, dtype=F32), h, axis=1)
    b_re2, b_im2 = ssm_b_re.reshape(g, p * h), ssm_b_im.reshape(g, p * h)
    log_dt2 = ssm_log_dt.reshape(g, 1) + behind_w_in
    lbr, lbi, bbr2, bbi2, pw3 = _zoh_fwd(ssm_a_re, ssm_a_im, log_dt2, b_re2, b_im2, expand, steps)
    lam = jnp.stack([lbr.reshape(g * p), lbi.reshape(g * p)])
    pw = pw3.reshape(2, g * p)
    to_slab_b = lambda b2: _blockdiag(b2.reshape(nq, GROUPS_PER_SLAB, p, h).transpose(0, 1, 3, 2))
    bq = jnp.concatenate([to_slab_b(bbr2), to_slab_b(bbi2)], axis=2).astype(BF16)
    to_slab_c = lambda c3: _blockdiag(c3.reshape(nq, GROUPS_PER_SLAB, h, p).transpose(0, 1, 3, 2))
    cq = jnp.concatenate([to_slab_c(ssm_c_re), to_slab_c(-ssm_c_im)], axis=1).astype(BF16)

    g_pre2, g_post2 = norm_pre_g.reshape(1, d), norm_post_g.reshape(1, d)
    conv_b2, b_glu2, d_skip2 = conv_b.reshape(1, dc), b_glu.reshape(1, dssm), ssm_d.reshape(1, dssm)
    u_block = 4 * dc // dssm

    hb = _prenorm(xs, g_pre2 + behind_w_in)
    win_own, win_landed = _split_copy_wait(win_send, win_recv, win_srcs, win_lands, win_plan,
                                           [hb, bq, cq, conv_w_full], "gather_w_in_wait")
    win_g = _gather_complete(win_own[0], win_landed[0], "gather_w_in_complete")
    win_b = win_g.reshape(N_CHIPS, d, n_shard)
    gathered = win_g[0, 0, :SUBLANES, :SLAB].astype(F32)
    side = [halves(_cast_bf16(w_out, "cast_w_out", gathered)), halves(_cast_bf16(w_glu, "cast_w_glu", gathered))]
    side_plan = _gather_direct_plan(len(side))
    side_sems = 2 * (N_CHIPS - 1) * len(side)
    ag_send, ag_recv, ag_srcs, ag_lands, ag_token = _split_copy_start(
        side, [((N_CHIPS,) + a.shape, BF16) for a in side], side_plan, side_sems, "gather_side_weights_start")
    proj = _matmul_nn(hb, win_b, "inproj", ag_token)
    y, cin = _ssm_fwd(proj, u_block, bq, cq, lam, pw, d_skip2)
    side_own, side_all = _split_copy_wait(ag_send, ag_recv, ag_srcs, ag_lands, side_plan, [cin],
                                          "gather_side_weights_wait")
    wout_g, wglu_g = [lax.dynamic_update_index_in_dim(all_, own, chip, 0) for own, all_ in zip(side_own, side_all)]
    wout_b = wout_g.reshape(dc + dssm, d)
    wglu_b = wglu_g.reshape(dssm, dssm)
    mix = _gate_fwd(proj, y, conv_w_full, conv_b2, wglu_b, b_glu2, dc)
    loss_blk, dout, dob, dmix, gg_post = _outproj(mix, wout_b, xs, tgt, g_post2)

    def reduce_start(grads, tags, group):
        got = _rs_pair_exchange(grads, "rs_pair_exchange_" + group)
        sums = [_rs_pair_add(place, gt, rt, "rs_pair_add_" + tg) for gt, rt, tg in zip(grads, got, tags)]
        plan = _chip_exchange_plan(len(grads))
        started = _split_copy_start([s16 for s16, _ in sums],
                                    [((N_CHIPS - 1,) + s16.shape[1:], BF16) for s16, _ in sums], plan,
                                    (N_CHIPS - 1) * len(grads), "rs_chip_exchange_" + group + "_start")
        return plan, started, [s32 for _, s32 in sums]

    def reduce_finish(plan, started, sums32, tags, group, after):
        send, recv, srcs, lands, _ = started
        _, landed = _split_copy_wait(send, recv, srcs, lands, plan, after, "rs_chip_exchange_" + group + "_wait")
        mine = [_rs_chip_add(place, s32, qt, "rs_chip_add_" + tg) for s32, qt, tg in zip(sums32, landed, tags)]
        full = _rs_pair_share(mine, "rs_pair_share_" + group)
        return [f.reshape(2 * f.shape[1], f.shape[2]) for f in full]

    gw_out = _matmul_tn(mix, dob, 1, "grad_w_out")
    dproj, dy, gsmall, gw_glu = _gate_bwd(proj, y, dmix, conv_w_full, conv_b2, wglu_b, b_glu2, dc)
    rs_a = reduce_start([gw_out.reshape(N_CHIPS, 2, (dc + dssm) // (2 * N_CHIPS), d),
                         gw_glu.reshape(N_CHIPS, 2, dssm // (2 * N_CHIPS), dssm)], ["w_out", "w_glu"], "a")
    dproj, gb_dense, gc_dense, glam, gd = _ssm_bwd(proj, u_block, dy, cin, dproj, bq, cq, lam, pw,
                                                   d_skip2 + rs_a[1][4][0:1, 0:1])
    gw_in = _matmul_tn(hb, dproj, N_CHIPS, "grad_w_in")
    rs_b = reduce_start([gw_in.reshape(N_CHIPS, 2, d // 2, n_shard)], ["w_in"], "b")
    gx, gg_pre = _dh_prenorm_bwd(dproj, win_b, xs, dout, g_pre2 + rs_b[1][4][0:1, 0:1])

    gb4 = gb_dense.reshape(nq, SLAB, 2, SLAB_STATES)
    g_bbr2 = _blockdiag_take(gb4[:, :, 0, :], h, p).transpose(0, 1, 3, 2).reshape(g, p * h)
    g_bbi2 = _blockdiag_take(gb4[:, :, 1, :], h, p).transpose(0, 1, 3, 2).reshape(g, p * h)
    gc4 = gc_dense.reshape(nq, SLAB, 2, SLAB_STATES)
    g_c_re = _blockdiag_take(gc4[:, :, 0, :], h, p).reshape(g, h, p)
    g_c_im = -_blockdiag_take(gc4[:, :, 1, :], h, p).reshape(g, h, p)
    g_a_re, g_a_im, g_ld, g_b_re2, g_b_im2 = _zoh_bwd(
        ssm_a_re, ssm_a_im, log_dt2, b_re2, b_im2, expand,
        glam[0].reshape(g, p), glam[1].reshape(g, p), g_bbr2, g_bbi2)

    small_names = ["norm_pre_g", "conv_w", "conv_b", "ssm_a_re", "ssm_a_im", "ssm_log_dt", "ssm_b_re", "ssm_b_im",
                   "ssm_c_re", "ssm_c_im", "ssm_d", "b_glu", "norm_post_g", "loss"]
    small_g = {
        "norm_pre_g": gg_pre[0], "conv_w": gsmall[0:3], "conv_b": gsmall[3], "ssm_a_re": g_a_re, "ssm_a_im": g_a_im,
        "ssm_log_dt": g_ld.reshape(g), "ssm_b_re": g_b_re2.reshape(g, p, h), "ssm_b_im": g_b_im2.reshape(g, p, h),
        "ssm_c_re": g_c_re, "ssm_c_im": g_c_im, "ssm_d": gd[0], "b_glu": gsmall[4], "norm_post_g": gg_post[0],
        "loss": loss_blk[0, 0:1],
    }
    zeros_cw = jnp.zeros((conv_w.shape[0], dc), F32)
    one0 = jnp.zeros((1,), F32)
    small_w = dict(norm_pre_g=norm_pre_g, conv_w=zeros_cw, conv_b=conv_b, ssm_a_re=ssm_a_re, ssm_a_im=ssm_a_im,
                   ssm_log_dt=ssm_log_dt, ssm_b_re=ssm_b_re, ssm_b_im=ssm_b_im, ssm_c_re=ssm_c_re, ssm_c_im=ssm_c_im,
                   ssm_d=ssm_d, b_glu=b_glu, norm_post_g=norm_post_g, loss=one0)
    small_m = dict(norm_pre_g=m_norm_pre_g, conv_w=zeros_cw, conv_b=m_conv_b, ssm_a_re=m_ssm_a_re, ssm_a_im=m_ssm_a_im,
                   ssm_log_dt=m_ssm_log_dt, ssm_b_re=m_ssm_b_re, ssm_b_im=m_ssm_b_im, ssm_c_re=m_ssm_c_re,
                   ssm_c_im=m_ssm_c_im, ssm_d=m_ssm_d, b_glu=m_b_glu, norm_post_g=m_norm_post_g, loss=one0)
    small_v = dict(norm_pre_g=v_norm_pre_g, conv_w=zeros_cw, conv_b=v_conv_b, ssm_a_re=v_ssm_a_re, ssm_a_im=v_ssm_a_im,
                   ssm_log_dt=v_ssm_log_dt, ssm_b_re=v_ssm_b_re, ssm_b_im=v_ssm_b_im, ssm_c_re=v_ssm_c_re,
                   ssm_c_im=v_ssm_c_im, ssm_d=v_ssm_d, b_glu=v_b_glu, norm_post_g=v_norm_post_g, loss=one0)
    shapes = [small_w[nm].shape for nm in small_names]
    g_pack = _allreduce_small(_pack([small_g[nm] for nm in small_names]))
    packs = _adamw_small(g_pack, _pack([small_w[nm] for nm in small_names]),
                         _pack([small_m[nm] for nm in small_names]), _pack([small_v[nm] for nm in small_names]))
    sg, sd, sm, sv = [dict(zip(small_names, _unpack(pk, shapes))) for pk in (g_pack, *packs)]
    loss = sg["loss"][0]

    g_cw = lax.dynamic_slice_in_dim(sg["conv_w"], chip * cw_cols, cw_cols, axis=1)
    pad_cw = lambda a: jnp.zeros((SUBLANES, cw_pad), F32).at[:a.shape[0], :cw_cols].set(a)
    cut_cw = lambda a: a[:conv_w.shape[0], :cw_cols]
    d_cw, m_cw, v_cw = [cut_cw(a) for a in _adamw(pad_cw(conv_w), pad_cw(g_cw), pad_cw(m_conv_w), pad_cw(v_conv_w),
                                                  "adamw_conv_w")]

    g_wout, g_wglu = reduce_finish(*rs_a, ["w_out", "w_glu"], "a", [d_cw, packs[0]])
    (g_win,) = reduce_finish(*rs_b, ["w_in"], "b", [g_wout])
    d_win, m_win, v_win = _adamw(w_in, g_win, m_w_in, v_w_in, "adamw_w_in")
    d_wout, m_wout, v_wout = _adamw(w_out, g_wout, m_w_out, v_w_out, "adamw_w_out")
    d_wglu, m_wglu, v_wglu = _adamw(w_glu, g_wglu, m_w_glu, v_w_glu, "adamw_w_glu")

    order = ["norm_pre_g", "w_in", "conv_w", "conv_b", "ssm_a_re", "ssm_a_im", "ssm_log_dt", "ssm_b_re", "ssm_b_im",
             "ssm_c_re", "ssm_c_im", "ssm_d", "w_glu", "b_glu", "w_out", "norm_post_g"]
    grads, deltas, new_m, new_v = dict(sg), dict(sd), dict(sm), dict(sv)
    grads.update(w_in=g_win, w_out=g_wout, w_glu=g_wglu, conv_w=g_cw)
    deltas.update(w_in=d_win, w_out=d_wout, w_glu=d_wglu, conv_w=d_cw)
    new_m.update(w_in=m_win, w_out=m_wout, w_glu=m_wglu, conv_w=m_cw)
    new_v.update(w_in=v_win, w_out=v_wout, w_glu=v_wglu, conv_w=v_cw)
    return (loss, gx[None], *[grads[nm] for nm in order], *[deltas[nm] for nm in order],
            *[new_m[nm] for nm in order], *[new_v[nm] for nm in order])
```

```python
import functools
import math

import jax
import jax.numpy as jnp
from jax import lax
from jax.experimental import pallas as pl
from jax.experimental.pallas import tpu as pltpu

F32 = jnp.float32
BF16 = jnp.bfloat16
MESH = pl.DeviceIdType.MESH

EPS = 1e-6
SSM_H = 16
SSM_P = 64
GROUPS_PER_SLAB = 8
SLAB = 128
SLAB_STATES = GROUPS_PER_SLAB * SSM_P
N_CHIPS = 4
N_DEV = 8

ADAM_LR = 0.001
ADAM_B1 = 0.9
ADAM_B2 = 0.999
ADAM_EPS = 1e-08
ADAM_WD = 0.01
ADAM_STEP = 10

MIB = 1024 * 1024
VMEM_CAP = 48 * MIB
SUBLANES = 8

TM_NORM = 512
TM_PROJ = 512
TM_GATE = 256
TM_OUT = 256
TM_DH = 256
T_SCAN = 256
TK_TN = 512
SCAN_UNROLL = 4
TM_TN = 1024
TR_ELT = 256


def _cp(vmem_bytes, **kw):
    return pltpu.CompilerParams(vmem_limit_bytes=int(min(VMEM_CAP, max(16 * MIB, vmem_bytes))), **kw)


def _my_place():
    return lax.axis_index("x"), lax.axis_index("y"), lax.axis_index("c")


def _other_chips(x, y):
    return [(1 - x, y), (x, 1 - y), (1 - x, 1 - y)]


def _silu(z):
    s = jax.nn.sigmoid(z)
    return z * s, s


def _dsilu(z, s):
    return s * (1.0 + z * (1.0 - s))


_GELU_K = math.sqrt(2.0 / math.pi)
_GELU_C = 0.044715


def _gelu(y):
    th = jnp.tanh(_GELU_K * (y + _GELU_C * y * y * y))
    return 0.5 * y * (1.0 + th), th


def _dgelu(y, th):
    return 0.5 * (1.0 + th) + 0.5 * y * (1.0 - th * th) * _GELU_K * (1.0 + 3.0 * _GELU_C * y * y)


def _cast_bf16(w, name, after=None):
    r, c = w.shape
    tr = min(TR_ELT, r)
    extra = [] if after is None else [after]

    def body(w_ref, *rest):
        rest[-1][...] = w_ref[...].astype(BF16)

    return pl.pallas_call(
        body, name=name, grid=(r // tr,),
        in_specs=[pl.BlockSpec((tr, c), lambda i: (i, 0))] + [pl.BlockSpec((SUBLANES, SLAB), lambda i: (0, 0))] * len(extra),
        out_specs=pl.BlockSpec((tr, c), lambda i: (i, 0)),
        out_shape=jax.ShapeDtypeStruct((r, c), BF16),
        compiler_params=_cp(12 * tr * c),
    )(w, *extra)


def _prenorm(x, g):
    t, d = x.shape
    tm = min(TM_NORM, t)

    def body(x_ref, g_ref, h_ref):
        xv = x_ref[...]
        r = lax.rsqrt(jnp.mean(xv * xv, axis=-1, keepdims=True) + EPS)
        h_ref[...] = (xv * r * g_ref[...]).astype(BF16)

    return pl.pallas_call(
        body, name="prenorm", grid=(t // tm,),
        in_specs=[pl.BlockSpec((tm, d), lambda i: (i, 0)), pl.BlockSpec((1, d), lambda i: (0, 0))],
        out_specs=pl.BlockSpec((tm, d), lambda i: (i, 0)),
        out_shape=jax.ShapeDtypeStruct((t, d), BF16),
        compiler_params=_cp(20 * tm * d),
    )(x, g)


def _inproj_own(place, a, w_own):
    t, k = a.shape
    n = w_own.shape[1]
    tm = min(TM_PROJ, t)

    def body(place_ref, a_ref, b_ref, o_ref):
        del place_ref
        o_ref[...] = jnp.dot(a_ref[...], b_ref[...], preferred_element_type=F32)

    return pl.pallas_call(
        body, name="inproj_own",
        grid_spec=pltpu.PrefetchScalarGridSpec(
            num_scalar_prefetch=1, grid=(t // tm,),
            in_specs=[pl.BlockSpec((tm, k), lambda i, p: (i, 0)), pl.BlockSpec((k, n), lambda i, p: (0, 0))],
            out_specs=pl.BlockSpec((tm, n), lambda i, p: (i, p[1]))),
        out_shape=jax.ShapeDtypeStruct((t, N_CHIPS * n), F32),
        compiler_params=_cp(2 * (2 * tm * k + 2 * k * n + 4 * tm * n) + 4 * MIB),
    )(place, a, w_own)


def _inproj_rest(place, a, b, partial, after):
    t, k = a.shape
    nb, _, n = b.shape
    tm = min(TM_PROJ, t)
    shard = lambda s, p: (p[1] + 1 + s) % nb

    def body(place_ref, a_ref, b_ref, after_ref, partial_ref, o_ref):
        del place_ref, after_ref, partial_ref
        o_ref[...] = jnp.dot(a_ref[...], b_ref[...], preferred_element_type=F32)

    return pl.pallas_call(
        body, name="inproj_rest",
        grid_spec=pltpu.PrefetchScalarGridSpec(
            num_scalar_prefetch=1, grid=(nb - 1, t // tm),
            in_specs=[pl.BlockSpec((tm, k), lambda s, i, p: (i, 0)),
                      pl.BlockSpec((None, k, n), lambda s, i, p: (shard(s, p), 0, 0)),
                      pl.BlockSpec((SUBLANES, SLAB), lambda s, i, p: (0, 0)),
                      pl.BlockSpec(memory_space=pl.ANY)],
            out_specs=pl.BlockSpec((tm, n), lambda s, i, p: (i, shard(s, p)))),
        out_shape=jax.ShapeDtypeStruct((t, nb * n), F32),
        input_output_aliases={4: 0},
        compiler_params=_cp(2 * (2 * tm * k + 2 * k * n + 4 * tm * n) + 4 * MIB),
    )(place, a, b, after, partial)


def _matmul_tn(a, b, nb, name):
    t, m = a.shape
    n = b.shape[1] // nb
    tk = min(TK_TN, t)
    tma = min(TM_TN, m)

    def body(a_ref, b_ref, o_ref):
        @pl.when(pl.program_id(2) == 0)
        def _():
            o_ref[...] = jnp.zeros_like(o_ref)

        o_ref[...] += lax.dot_general(a_ref[...], b_ref[...], (((0,), (0,)), ((), ())), preferred_element_type=F32)

    return pl.pallas_call(
        body, name=name, grid=(nb, m // tma, t // tk),
        in_specs=[pl.BlockSpec((tk, tma), lambda j, i, k: (k, i)), pl.BlockSpec((tk, n), lambda j, i, k: (k, j))],
        out_specs=pl.BlockSpec((None, tma, n), lambda j, i, k: (j, i, 0)),
        out_shape=jax.ShapeDtypeStruct((nb, m, n), F32),
        compiler_params=_cp(2 * (2 * tk * tma + 2 * tk * n + 4 * tma * n) + 8 * MIB),
    )(a, b)


def _outproj(mix, w_out, x, tgt, g_post):
    t, dm = mix.shape
    d = w_out.shape[1]
    tm = min(TM_OUT, t)

    def body(mix_ref, w_ref, x_ref, t_ref, g_ref, loss_ref, dout_ref, do_ref, dmix_ref, gg_ref):
        @pl.when(pl.program_id(0) == 0)
        def _():
            loss_ref[...] = jnp.zeros_like(loss_ref)
            gg_ref[...] = jnp.zeros_like(gg_ref)

        w = w_ref[...]
        o = jnp.dot(mix_ref[...], w, preferred_element_type=F32)
        r = lax.rsqrt(jnp.mean(o * o, axis=-1, keepdims=True) + EPS)
        nh = o * r
        g = g_ref[...]
        e = x_ref[...] + nh * g - t_ref[...]
        loss_ref[...] += jnp.sum(e * e) * (0.5 / d)
        dout = e * (1.0 / d)
        dout_ref[...] = dout
        gg_ref[0:1, :] += jnp.sum(dout * nh, axis=0, keepdims=True)
        dn = dout * g
        do = r * (dn - nh * jnp.mean(dn * nh, axis=-1, keepdims=True))
        dob = do.astype(BF16)
        do_ref[...] = dob
        dmix_ref[...] = lax.dot_general(dob, w, (((1,), (1,)), ((), ())), preferred_element_type=F32)

    row = lambda i: (i, 0)
    fixed = lambda i: (0, 0)
    return pl.pallas_call(
        body, name="outproj", grid=(t // tm,),
        in_specs=[pl.BlockSpec((tm, dm), row), pl.BlockSpec((dm, d), fixed), pl.BlockSpec((tm, d), row),
                  pl.BlockSpec((tm, d), row), pl.BlockSpec((1, d), fixed)],
        out_specs=[pl.BlockSpec((SUBLANES, SLAB), fixed), pl.BlockSpec((tm, d), row), pl.BlockSpec((tm, d), row),
                   pl.BlockSpec((tm, dm), row), pl.BlockSpec((SUBLANES, d), fixed)],
        out_shape=[jax.ShapeDtypeStruct((SUBLANES, SLAB), F32), jax.ShapeDtypeStruct((t, d), F32),
                   jax.ShapeDtypeStruct((t, d), BF16), jax.ShapeDtypeStruct((t, dm), F32),
                   jax.ShapeDtypeStruct((SUBLANES, d), F32)],
        compiler_params=_cp(2 * (2 * dm * d + tm * (2 * dm + 4 * d * 3 + 2 * d + 4 * dm)) + 16 * MIB),
    )(mix, w_out, x, tgt, g_post)


def _dh_prenorm_bwd(dproj, w_in, x, dout, g_pre):
    t, d = x.shape
    nb, _, n = w_in.shape
    tm = min(TM_DH, t)

    def body(dp_ref, w_ref, x_ref, dout_ref, g_ref, dx_ref, gg_ref):
        @pl.when(pl.program_id(0) == 0)
        def _():
            gg_ref[...] = jnp.zeros_like(gg_ref)

        dh = None
        for k in range(nb):
            part = lax.dot_general(dp_ref[:, k * n:(k + 1) * n], w_ref[k], (((1,), (1,)), ((), ())),
                                   preferred_element_type=F32)
            dh = part if dh is None else dh + part
        xv = x_ref[...]
        r = lax.rsqrt(jnp.mean(xv * xv, axis=-1, keepdims=True) + EPS)
        xh = xv * r
        gg_ref[0:1, :] += jnp.sum(dh * xh, axis=0, keepdims=True)
        dg = dh * g_ref[...]
        dx_ref[...] = dout_ref[...] + r * (dg - xh * jnp.mean(dg * xh, axis=-1, keepdims=True))

    row = lambda i: (i, 0)
    fixed = lambda i: (0, 0)
    w_spec = pl.BlockSpec(w_in.shape, lambda i: (0, 0, 0), pipeline_mode=pl.Buffered(1))
    return pl.pallas_call(
        body, name="dh_prenorm_bwd", grid=(t // tm,),
        in_specs=[pl.BlockSpec((tm, nb * n), row), w_spec,
                  pl.BlockSpec((tm, d), row), pl.BlockSpec((tm, d), row), pl.BlockSpec((1, d), fixed)],
        out_specs=[pl.BlockSpec((tm, d), row), pl.BlockSpec((SUBLANES, d), fixed)],
        out_shape=[jax.ShapeDtypeStruct((t, d), F32), jax.ShapeDtypeStruct((SUBLANES, d), F32)],
        compiler_params=_cp(2 * nb * d * n + 2 * (2 * tm * nb * n + 12 * tm * d) + 16 * tm * d + 4 * MIB),
    )(dproj, w_in, x, dout, g_pre)


def _adamw(w, g, m, v, name):
    r, c = w.shape
    tr = min(TR_ELT, r)
    c1 = 1.0 - ADAM_B1 ** ADAM_STEP
    c2 = 1.0 - ADAM_B2 ** ADAM_STEP

    def body(w_ref, g_ref, m_ref, v_ref, d_ref, mo_ref, vo_ref):
        gv = g_ref[...]
        mn = ADAM_B1 * m_ref[...] + (1.0 - ADAM_B1) * gv
        vn = ADAM_B2 * v_ref[...] + (1.0 - ADAM_B2) * (gv * gv)
        d_ref[...] = -ADAM_LR * ((mn / c1) / (jnp.sqrt(vn / c2) + ADAM_EPS) + ADAM_WD * w_ref[...])
        mo_ref[...] = mn
        vo_ref[...] = vn

    spec = pl.BlockSpec((tr, c), lambda i: (i, 0))
    sds = jax.ShapeDtypeStruct((r, c), F32)
    return pl.pallas_call(
        body, name=name, grid=(r // tr,), in_specs=[spec] * 4, out_specs=[spec] * 3, out_shape=[sds] * 3,
        compiler_params=_cp(2 * 7 * 4 * tr * c + 8 * MIB),
    )(w, g, m, v)


def _adamw_small(g, w, m, v):
    r, c = w.shape
    c1 = 1.0 - ADAM_B1 ** ADAM_STEP
    c2 = 1.0 - ADAM_B2 ** ADAM_STEP

    def body(g_ref, w_ref, m_ref, v_ref, d_ref, mo_ref, vo_ref):
        gv = g_ref[...]
        mn = ADAM_B1 * m_ref[...] + (1.0 - ADAM_B1) * gv
        vn = ADAM_B2 * v_ref[...] + (1.0 - ADAM_B2) * (gv * gv)
        d_ref[...] = -ADAM_LR * ((mn / c1) / (jnp.sqrt(vn / c2) + ADAM_EPS) + ADAM_WD * w_ref[...])
        mo_ref[...] = mn
        vo_ref[...] = vn

    sds = jax.ShapeDtypeStruct((r, c), F32)
    return pl.pallas_call(
        body, name="adamw_small", out_shape=[sds] * 3,
        compiler_params=_cp(12 * 4 * r * c + 8 * MIB),
    )(g, w, m, v)


def _allreduce_small(v):
    r, c = v.shape
    assert r % (2 * SUBLANES) == 0
    h = r // 2

    def body(v_ref, out_ref, got_ref, chip_ref, slots_ref, send, recv):
        x, y, cc = _my_place()
        k = 2 * x + y
        sib = (x, y, 1 - cc)

        def rc(s, src, dst, to):
            return pltpu.make_async_remote_copy(src_ref=src, dst_ref=dst, send_sem=send.at[s], recv_sem=recv.at[s],
                                                device_id=to, device_id_type=MESH)

        pair = rc(0, v_ref, got_ref, sib)
        pair.start()
        pair.wait()
        chip_ref[...] = v_ref[...] + got_ref[...]
        mine = pl.ds(pl.multiple_of(cc * h, SUBLANES), h)
        theirs = pl.ds(pl.multiple_of((1 - cc) * h, SUBLANES), h)
        chips = _other_chips(x, y)
        sent = []
        for j, (cx, cy) in enumerate(chips):
            cp = rc(1 + j, chip_ref.at[mine], slots_ref.at[k], (cx, cy, cc))
            cp.start()
            sent.append(cp)
        slots_ref[k] = chip_ref[mine, :]
        for j, (cx, cy) in enumerate(chips):
            rc(1 + j, chip_ref.at[mine], slots_ref.at[2 * cx + cy], (cx, cy, cc)).wait_recv()
        total = slots_ref[0]
        for kk in range(1, N_CHIPS):
            total = total + slots_ref[kk]
        out_ref[mine, :] = total
        for cp in sent:
            cp.wait_send()
        share = rc(N_CHIPS, out_ref.at[mine], out_ref.at[mine], sib)
        share.start()
        rc(N_CHIPS, out_ref.at[theirs], out_ref.at[theirs], sib).wait_recv()
        share.wait_send()

    vm = pl.BlockSpec(memory_space=pltpu.VMEM)
    return pl.pallas_call(
        body, name="allreduce_small", out_shape=jax.ShapeDtypeStruct((r, c), F32), in_specs=[vm], out_specs=vm,
        scratch_shapes=[pltpu.VMEM((r, c), F32), pltpu.VMEM((r, c), F32), pltpu.VMEM((N_CHIPS, h, c), F32),
                        pltpu.SemaphoreType.DMA((N_CHIPS + 1,)), pltpu.SemaphoreType.DMA((N_CHIPS + 1,))],
        compiler_params=_cp(6 * 4 * r * c + 8 * MIB),
    )(v)


def _zoh(a_re, a_im, log_dt, b_re2, b_im2, expand):
    dt = jnp.exp(log_dt)
    mag = jnp.exp(a_re * dt)
    lbr, lbi = mag * jnp.cos(a_im * dt), mag * jnp.sin(a_im * dt)
    nr, ni = lbr - 1.0, lbi
    den = a_re * a_re + a_im * a_im
    qr = (nr * a_re + ni * a_im) / den
    qi = (ni * a_re - nr * a_im) / den
    qr2 = jnp.dot(qr, expand, precision=lax.Precision.HIGHEST, preferred_element_type=F32)
    qi2 = jnp.dot(qi, expand, precision=lax.Precision.HIGHEST, preferred_element_type=F32)
    return lbr, lbi, qr2 * b_re2 - qi2 * b_im2, qr2 * b_im2 + qi2 * b_re2


def _zoh_fwd(a_re, a_im, log_dt, b_re2, b_im2, expand, power):
    g, p = a_re.shape

    def body(ar_ref, ai_ref, ld_ref, br_ref, bi_ref, e_ref, lbr_ref, lbi_ref, bbr_ref, bbi_ref, pw_ref):
        ar, ai, ld = ar_ref[...], ai_ref[...], ld_ref[...]
        lbr, lbi, bbr, bbi = _zoh(ar, ai, ld, br_ref[...], bi_ref[...], e_ref[...])
        lbr_ref[...], lbi_ref[...], bbr_ref[...], bbi_ref[...] = lbr, lbi, bbr, bbi
        dt = jnp.exp(ld) * float(power)
        mag = jnp.exp(ar * dt)
        pw_ref[0] = mag * jnp.cos(ai * dt)
        pw_ref[1] = mag * jnp.sin(ai * dt)

    gp = jax.ShapeDtypeStruct((g, p), F32)
    gph = jax.ShapeDtypeStruct(b_re2.shape, F32)
    return pl.pallas_call(
        body, name="zoh_fwd", out_shape=[gp, gp, gph, gph, jax.ShapeDtypeStruct((2, g, p), F32)],
        compiler_params=_cp(16 * MIB),
    )(a_re, a_im, log_dt, b_re2, b_im2, expand)


def _zoh_bwd(a_re, a_im, log_dt, b_re2, b_im2, expand, g_lbr, g_lbi, g_bbr, g_bbi):
    def body(ar_ref, ai_ref, ld_ref, br_ref, bi_ref, e_ref, c0, c1, c2, c3, gar, gai, gld, gbr, gbi):
        e = e_ref[...]
        _, vjp = jax.vjp(lambda a, b, c, d, f: _zoh(a, b, c, d, f, e),
                         ar_ref[...], ai_ref[...], ld_ref[...], br_ref[...], bi_ref[...])
        gar[...], gai[...], gld[...], gbr[...], gbi[...] = vjp((c0[...], c1[...], c2[...], c3[...]))

    sds = lambda a: jax.ShapeDtypeStruct(a.shape, F32)
    return pl.pallas_call(
        body, name="zoh_bwd", out_shape=[sds(a_re), sds(a_im), sds(log_dt), sds(b_re2), sds(b_im2)],
        compiler_params=_cp(16 * MIB),
    )(a_re, a_im, log_dt, b_re2, b_im2, expand, g_lbr, g_lbi, g_bbr, g_bbi)


def _blockdiag(blocks):
    nq, _, r, c = blocks.shape
    eye = jnp.eye(GROUPS_PER_SLAB, dtype=blocks.dtype)
    out = blocks[:, :, :, None, :] * eye[None, :, None, :, None]
    return out.reshape(nq, GROUPS_PER_SLAB * r, GROUPS_PER_SLAB * c)


def _blockdiag_take(dense, r, c):
    nq = dense.shape[0]
    d5 = dense.reshape(nq, GROUPS_PER_SLAB, r, GROUPS_PER_SLAB, c)
    return jnp.stack([d5[:, i, :, i, :] for i in range(GROUPS_PER_SLAB)], axis=1)


def _scan_slab(s_ref, row0, q, lam_ref, pw_ref, car_ref, steps, reverse, prev_ref=None, prev_row0=0, glam_ref=None):
    sign = -1.0 if reverse else 1.0
    half = SLAB_STATES // SLAB
    cols = [(q * 2 * half + m, q * 2 * half + half + m, q * SLAB_STATES + m * SLAB) for m in range(half)]
    nm = len(cols)
    full = (SUBLANES, SLAB)
    lam = [(jnp.broadcast_to(lam_ref[0:1, pl.ds(cl, SLAB)], full),
            jnp.broadcast_to(sign * lam_ref[1:2, pl.ds(cl, SLAB)], full)) for (_, _, cl) in cols]

    def step_rows(jj, base):
        j = (steps - 1 - jj) if reverse else jj
        return j, pl.ds(pl.multiple_of(base + j * SUBLANES, SUBLANES), SUBLANES)

    def pass1(jj, car):
        _, rows = step_rows(jj, row0)
        out = []
        for m, (cr, ci, _) in enumerate(cols):
            sr, si = car[2 * m], car[2 * m + 1]
            lr, li = lam[m]
            nr = lr * sr - li * si + s_ref[cr, rows, :]
            ni = lr * si + li * sr + s_ref[ci, rows, :]
            s_ref[cr, rows, :] = nr
            s_ref[ci, rows, :] = ni
            out += [nr, ni]
        return tuple(out)

    def unrolled(step_fn):
        def outer(jo, carry):
            for k in range(SCAN_UNROLL):
                carry = step_fn(jo * SCAN_UNROLL + k, carry)
            return carry
        return outer

    assert steps % SCAN_UNROLL == 0
    ends = lax.fori_loop(0, steps // SCAN_UNROLL, unrolled(pass1), tuple(jnp.zeros(full, F32) for _ in range(2 * nm)))

    entry = []
    for m, (cr, ci, cl) in enumerate(cols):
        ljr = pw_ref[0:1, pl.ds(cl, SLAB)]
        lji = sign * pw_ref[1:2, pl.ds(cl, SLAB)]
        c_r = car_ref[cr, 0:1, :]
        c_i = car_ref[ci, 0:1, :]
        rows_r, rows_i = [None] * SUBLANES, [None] * SUBLANES
        order = range(SUBLANES - 1, -1, -1) if reverse else range(SUBLANES)
        for b in order:
            rows_r[b], rows_i[b] = c_r, c_i
            e_r, e_i = ends[2 * m][b:b + 1], ends[2 * m + 1][b:b + 1]
            c_r, c_i = ljr * c_r - lji * c_i + e_r, ljr * c_i + lji * c_r + e_i
        car_ref[cr, 0:1, :] = c_r
        car_ref[ci, 0:1, :] = c_i
        entry.append((jnp.concatenate(rows_r, axis=0), jnp.concatenate(rows_i, axis=0)))

    def pass2(jj, carry):
        j, rows = step_rows(jj, row0)
        decayed, acc = carry[:2 * nm], carry[2 * nm:]
        out_d, out_a = [], []
        for m, (cr, ci, cl) in enumerate(cols):
            lr, li = lam[m]
            dr, di = decayed[2 * m], decayed[2 * m + 1]
            dr, di = lr * dr - li * di, lr * di + li * dr
            nr = s_ref[cr, rows, :] + dr
            ni = s_ref[ci, rows, :] + di
            s_ref[cr, rows, :] = nr
            s_ref[ci, rows, :] = ni
            out_d += [dr, di]
            if prev_ref is not None:
                prow = pl.ds(pl.multiple_of(prev_row0 + (j - 1) * SUBLANES, SUBLANES), SUBLANES)
                qr = prev_ref[cr, prow, :]
                qi = prev_ref[ci, prow, :]
                out_a += [acc[2 * m] + (nr * qr + ni * qi), acc[2 * m + 1] + (ni * qr - nr * qi)]
        return tuple(out_d) + tuple(out_a)

    n_acc = 2 * nm if prev_ref is not None else 0
    init = tuple(e for pair in entry for e in pair) + tuple(jnp.zeros(full, F32) for _ in range(n_acc))
    accs = lax.fori_loop(0, steps // SCAN_UNROLL, unrolled(pass2), init)[2 * nm:]
    if prev_ref is not None:
        for m, (_, _, cl) in enumerate(cols):
            glam_ref[0:1, pl.ds(cl, SLAB)] += jnp.sum(accs[2 * m], axis=0, keepdims=True)
            glam_ref[1:2, pl.ds(cl, SLAB)] += jnp.sum(accs[2 * m + 1], axis=0, keepdims=True)


def _permute_rows_f32(perm_bf16, v):
    hi = v.astype(BF16)
    lo = (v - hi.astype(F32)).astype(BF16)
    return (jnp.dot(perm_bf16, hi, preferred_element_type=F32) + jnp.dot(perm_bf16, lo, preferred_element_type=F32))


def _put_slab(s_ref, rows, q, val):
    per = 2 * SLAB_STATES // SLAB
    for i in range(per):
        s_ref[q * per + i, rows, :] = val[:, i * SLAB:(i + 1) * SLAB]


def _get_slab(s_ref, rows, q):
    per = 2 * SLAB_STATES // SLAB
    return jnp.concatenate([s_ref[q * per + i, rows, :] for i in range(per)], axis=1)


def _step_major_perm(tt):
    r = jnp.arange(tt)
    held = (r % SUBLANES) * (tt // SUBLANES) + r // SUBLANES
    return held[:, None] == r[None, :]


def _ssm_fwd(proj, u_block, bq, cq, lam, pw, d_skip):
    t = proj.shape[0]
    nq, ds, w2 = bq.shape
    assert ds == SLAB and w2 == 2 * SLAB_STATES
    dssm = nq * SLAB
    width = nq * w2
    ntile = width // SLAB
    tt = min(T_SCAN, t)
    steps = tt // SUBLANES
    perm = _step_major_perm(tt)
    pm, pmt = perm.astype(BF16), perm.T.astype(BF16)

    def body(u_ref, pm_ref, pmt_ref, bq_ref, cq_ref, lam_ref, pw_ref, d_ref, y_ref, cin_ref, s_ref, car_ref):
        @pl.when(pl.program_id(0) == 0)
        def _():
            car_ref[...] = jnp.zeros_like(car_ref)

        cin_ref[...] = jnp.broadcast_to(car_ref[:, 0:1, :], cin_ref.shape)
        u = u_ref[...]
        ub = jnp.dot(pm_ref[...], u.astype(BF16), preferred_element_type=F32).astype(BF16)
        everything = slice(None)
        for q in range(nq):
            _put_slab(s_ref, everything, q,
                      jnp.dot(ub[:, q * SLAB:(q + 1) * SLAB], bq_ref[q], preferred_element_type=F32))
        for q in range(nq):
            _scan_slab(s_ref, 0, q, lam_ref, pw_ref, car_ref, steps, reverse=False)
        y_sm = jnp.concatenate(
            [jnp.dot(_get_slab(s_ref, everything, q).astype(BF16), cq_ref[q], preferred_element_type=F32)
             for q in range(nq)], axis=1)
        y_ref[...] = _permute_rows_f32(pmt_ref[...], y_sm) + d_ref[...] * u

    c3 = lambda i: (0, 0, 0)
    c2 = lambda i: (0, 0)
    return pl.pallas_call(
        body, name="ssm_fwd", grid=(t // tt,),
        in_specs=[pl.BlockSpec((tt, dssm), lambda i: (i, u_block)), pl.BlockSpec((tt, tt), c2),
                  pl.BlockSpec((tt, tt), c2), pl.BlockSpec(bq.shape, c3),
                  pl.BlockSpec(cq.shape, c3), pl.BlockSpec(lam.shape, c2), pl.BlockSpec(pw.shape, c2),
                  pl.BlockSpec((1, dssm), c2)],
        out_specs=[pl.BlockSpec((tt, dssm), lambda i: (i, 0)),
                   pl.BlockSpec((None, ntile, SUBLANES, SLAB), lambda i: (i, 0, 0, 0))],
        out_shape=[jax.ShapeDtypeStruct((t, dssm), F32), jax.ShapeDtypeStruct((t // tt, ntile, SUBLANES, SLAB), F32)],
        scratch_shapes=[pltpu.VMEM((ntile, tt, SLAB), F32), pltpu.VMEM((ntile, SUBLANES, SLAB), F32)],
        compiler_params=_cp(2 * (8 * tt * dssm + 4 * nq * ds * w2) + 4 * tt * width + 16 * MIB,
                            dimension_semantics=("arbitrary",)),
    )(proj, pm, pmt, bq, cq, lam, pw, d_skip)


def _ssm_bwd(proj, u_block, dy, cin, dproj, bq, cq, lam, pw, d_skip):
    t = proj.shape[0]
    nq, ds, w2 = bq.shape
    dssm = nq * SLAB
    width = nq * w2
    ntile = width // SLAB
    tt = min(T_SCAN, t)
    nt = t // tt
    steps = tt // SUBLANES
    halo = SUBLANES
    perm = _step_major_perm(tt)
    pm, pmt = perm.astype(BF16), perm.T.astype(BF16)

    def body(u_ref, dy_ref, cin_ref, dp_any, pm_ref, pmt_ref, bq_ref, cq_ref, lam_ref, pw_ref, d_ref,
             du_ref, gb_ref, gc_ref, glam_ref, gd_ref, s_ref, gs_ref, car_f, car_b):
        del dp_any

        @pl.when(pl.program_id(0) == 0)
        def _():
            car_b[...] = jnp.zeros_like(car_b)
            gb_ref[...] = jnp.zeros_like(gb_ref)
            gc_ref[...] = jnp.zeros_like(gc_ref)
            glam_ref[...] = jnp.zeros_like(glam_ref)
            gd_ref[...] = jnp.zeros_like(gd_ref)

        u = u_ref[...]
        dyv = dy_ref[...]
        gd_ref[0:1, :] += jnp.sum(dyv * u, axis=0, keepdims=True)
        pmv = pm_ref[...]
        ub = jnp.dot(pmv, u.astype(BF16), preferred_element_type=F32).astype(BF16)
        dyb = jnp.dot(pmv, dyv.astype(BF16), preferred_element_type=F32).astype(BF16)
        car_f[...] = cin_ref[...]
        data = slice(halo, halo + tt)
        everything = slice(None)
        for q in range(nq):
            _put_slab(s_ref, data, q, jnp.dot(ub[:, q * SLAB:(q + 1) * SLAB], bq_ref[q], preferred_element_type=F32))
        for q in range(nq):
            _scan_slab(s_ref, halo, q, lam_ref, pw_ref, car_f, steps, reverse=False)
        last_step = s_ref[:, halo + tt - SUBLANES:halo + tt, :]
        s_ref[:, 0:halo, :] = jnp.concatenate([cin_ref[:, 0:1, :], last_step[:, 0:SUBLANES - 1, :]], axis=1)
        tn = (((0,), (0,)), ((), ()))
        nt_dims = (((1,), (1,)), ((), ()))
        for q in range(nq):
            sl = slice(q * SLAB, (q + 1) * SLAB)
            gc_ref[q] += lax.dot_general(dyb[:, sl], _get_slab(s_ref, data, q).astype(BF16), tn,
                                         preferred_element_type=F32)
            _put_slab(gs_ref, everything, q,
                      lax.dot_general(dyb[:, sl], cq_ref[q], nt_dims, preferred_element_type=F32))
        for q in range(nq):
            _scan_slab(gs_ref, 0, q, lam_ref, pw_ref, car_b, steps, reverse=True,
                       prev_ref=s_ref, prev_row0=halo, glam_ref=glam_ref)
        du_parts = []
        for q in range(nq):
            sl = slice(q * SLAB, (q + 1) * SLAB)
            gsb = _get_slab(gs_ref, everything, q).astype(BF16)
            du_parts.append(lax.dot_general(gsb, bq_ref[q], nt_dims, preferred_element_type=F32))
            gb_ref[q] += lax.dot_general(ub[:, sl], gsb, tn, preferred_element_type=F32)
        du_sm = jnp.concatenate(du_parts, axis=1)
        du_ref[...] = (_permute_rows_f32(pmt_ref[...], du_sm) + dyv * d_ref[...]).astype(BF16)

    c3 = lambda i: (0, 0, 0)
    c2 = lambda i: (0, 0)
    rev = lambda i: (nt - 1 - i, 0)
    dense = jax.ShapeDtypeStruct((nq, SLAB, w2), F32)
    gp = lam.shape[1]
    return pl.pallas_call(
        body, name="ssm_bwd", grid=(nt,),
        in_specs=[pl.BlockSpec((tt, dssm), lambda i: (nt - 1 - i, u_block)), pl.BlockSpec((tt, dssm), rev),
                  pl.BlockSpec((None, ntile, SUBLANES, SLAB), lambda i: (nt - 1 - i, 0, 0, 0)),
                  pl.BlockSpec(memory_space=pl.ANY), pl.BlockSpec((tt, tt), c2), pl.BlockSpec((tt, tt), c2),
                  pl.BlockSpec(bq.shape, c3), pl.BlockSpec(cq.shape, c3),
                  pl.BlockSpec(lam.shape, c2), pl.BlockSpec(pw.shape, c2), pl.BlockSpec((1, dssm), c2)],
        out_specs=[pl.BlockSpec((tt, dssm), lambda i: (nt - 1 - i, u_block)), pl.BlockSpec(dense.shape, c3),
                   pl.BlockSpec(dense.shape, c3), pl.BlockSpec((SUBLANES, gp), c2), pl.BlockSpec((SUBLANES, dssm), c2)],
        out_shape=[jax.ShapeDtypeStruct(dproj.shape, dproj.dtype), dense, dense,
                   jax.ShapeDtypeStruct((SUBLANES, gp), F32), jax.ShapeDtypeStruct((SUBLANES, dssm), F32)],
        scratch_shapes=[pltpu.VMEM((ntile, tt + halo, SLAB), F32), pltpu.VMEM((ntile, tt, SLAB), F32),
                        pltpu.VMEM((ntile, SUBLANES, SLAB), F32), pltpu.VMEM((ntile, SUBLANES, SLAB), F32)],
        input_output_aliases={3: 0},
        compiler_params=_cp(2 * (10 * tt * dssm + 4 * nq * ds * w2 + 8 * nq * SLAB * w2)
                            + 8 * tt * width + 12 * MIB, dimension_semantics=("arbitrary",)),
    )(proj, dy, cin, dproj, pm, pmt, bq, cq, lam, pw, d_skip)


def _gate_fwd(proj, y, conv_w, conv_b, w_glu, b_glu, dc):
    t = proj.shape[0]
    dssm = y.shape[1]
    assert dc == dssm
    tm = min(TM_GATE, t)
    halo = SUBLANES

    def body(b_ref, c_ref, v_ref, zc_ref, zs_ref, y_ref, cw_ref, cb_ref, wg_ref, bg_ref, mix_ref, cv_buf):
        @pl.when(pl.program_id(0) == 0)
        def _():
            cv_buf[0:halo, :] = jnp.zeros((halo, dc), F32)

        cv = c_ref[...] * v_ref[...]
        cv_buf[halo:, :] = cv
        conv = (cb_ref[...] + cw_ref[2:3, :] * cv + cw_ref[1:2, :] * cv_buf[halo - 1:halo - 1 + tm, :]
                + cw_ref[0:1, :] * cv_buf[halo - 2:halo - 2 + tm, :])
        sz, _ = _silu(zc_ref[...])
        mix_ref[:, 0:dc] = (b_ref[...] * conv * sz).astype(BF16)
        cv_buf[0:halo, :] = cv_buf[tm:tm + halo, :]
        ge, _ = _gelu(y_ref[...])
        gl = jnp.dot(ge.astype(BF16), wg_ref[...], preferred_element_type=F32) + bg_ref[...]
        szs, _ = _silu(zs_ref[...])
        mix_ref[:, dc:] = (ge * jax.nn.sigmoid(gl) * szs).astype(BF16)

    col = lambda j: pl.BlockSpec((tm, dc), lambda i, j=j: (i, j))
    fixed = lambda i: (0, 0)
    return pl.pallas_call(
        body, name="gate_fwd", grid=(t // tm,),
        in_specs=[col(0), col(1), col(2), col(3), col(5), pl.BlockSpec((tm, dssm), lambda i: (i, 0)),
                  pl.BlockSpec(conv_w.shape, fixed), pl.BlockSpec((1, dc), fixed),
                  pl.BlockSpec(w_glu.shape, fixed), pl.BlockSpec((1, dssm), fixed)],
        out_specs=pl.BlockSpec((tm, dc + dssm), lambda i: (i, 0)),
        out_shape=jax.ShapeDtypeStruct((t, dc + dssm), BF16),
        scratch_shapes=[pltpu.VMEM((tm + halo, dc), F32)],
        compiler_params=_cp(2 * (6 * 4 * tm * dc + 2 * tm * (dc + dssm) + 2 * dssm * dssm) + 24 * tm * dc + 8 * MIB,
                            dimension_semantics=("arbitrary",)),
    )(proj, proj, proj, proj, proj, y, conv_w, conv_b, w_glu, b_glu)


def _gate_bwd(proj, y, dmix, conv_w, conv_b, w_glu, b_glu, dc):
    t = proj.shape[0]
    dssm = y.shape[1]
    tm = min(TM_GATE, t)
    nt = t // tm
    halo = SUBLANES
    blocks_per_tile = tm // halo

    def body(b_ref, c_ref, v_ref, zc_ref, zs_ref, cp_ref, vp_ref, y_ref, dm_ref, cw_ref, cb_ref, wg_ref, bg_ref,
             dp_ref, dy_ref, gs_ref, gwg_ref, cv_buf, dc_buf):
        i = pl.program_id(0)

        @pl.when(i == 0)
        def _():
            dc_buf[tm:, :] = jnp.zeros((halo, dc), F32)
            gs_ref[...] = jnp.zeros_like(gs_ref)
            gwg_ref[...] = jnp.zeros_like(gwg_ref)

        first_tile = (i == nt - 1)
        bv, cg, vv, zc = b_ref[...], c_ref[...], v_ref[...], zc_ref[...]
        cv = cg * vv
        cv_buf[0:halo, :] = jnp.where(first_tile, 0.0, cp_ref[...] * vp_ref[...])
        cv_buf[halo:, :] = cv
        w0, w1, w2 = cw_ref[0:1, :], cw_ref[1:2, :], cw_ref[2:3, :]
        conv = (cb_ref[...] + w2 * cv + w1 * cv_buf[halo - 1:halo - 1 + tm, :]
                + w0 * cv_buf[halo - 2:halo - 2 + tm, :])
        sz, sgc = _silu(zc)
        dyc = dm_ref[:, 0:dc]
        dp_ref[:, 0:dc] = (dyc * conv * sz).astype(BF16)
        dp_ref[:, 3 * dc:4 * dc] = (dyc * bv * conv * _dsilu(zc, sgc)).astype(BF16)
        dconv = dyc * bv * sz
        dc_buf[0:tm, :] = dconv
        d1 = dc_buf[1:1 + tm, :]
        d2 = dc_buf[2:2 + tm, :]
        dcv = w2 * dconv + w1 * d1 + w0 * d2
        dp_ref[:, dc:2 * dc] = (dcv * vv).astype(BF16)
        dp_ref[:, 2 * dc:3 * dc] = (dcv * cg).astype(BF16)
        gs_ref[0:1, :] += jnp.sum(cv * d2, axis=0, keepdims=True)
        gs_ref[1:2, :] += jnp.sum(cv * d1, axis=0, keepdims=True)
        gs_ref[2:3, :] += jnp.sum(cv * dconv, axis=0, keepdims=True)
        gs_ref[3:4, :] += jnp.sum(dconv, axis=0, keepdims=True)
        dc_buf[tm:, :] = dc_buf[0:halo, :]

        yv, zs = y_ref[...], zs_ref[...]
        ge, th = _gelu(yv)
        geb = ge.astype(BF16)
        wg = wg_ref[...]
        gl = jnp.dot(geb, wg, preferred_element_type=F32) + bg_ref[...]
        sg = jax.nn.sigmoid(gl)
        szs, sgs = _silu(zs)
        dys = dm_ref[:, dc:]
        ys = ge * sg
        dp_ref[:, 4 * dc:5 * dc] = jnp.zeros((tm, dc), BF16)
        dp_ref[:, 5 * dc:] = (dys * ys * _dsilu(zs, sgs)).astype(BF16)
        d_ys = dys * szs
        dgl = d_ys * ge * sg * (1.0 - sg)
        dglb = dgl.astype(BF16)
        gs_ref[4:5, :] += jnp.sum(dgl, axis=0, keepdims=True)
        gwg_ref[...] += lax.dot_general(geb, dglb, (((0,), (0,)), ((), ())), preferred_element_type=F32)
        dge = d_ys * sg + lax.dot_general(dglb, wg, (((1,), (1,)), ((), ())), preferred_element_type=F32)
        dy_ref[...] = dge * _dgelu(yv, th)

    col = lambda j: pl.BlockSpec((tm, dc), lambda i, j=j: (nt - 1 - i, j))
    prev = lambda j: pl.BlockSpec((halo, dc), lambda i, j=j: (jnp.maximum((nt - 1 - i) * blocks_per_tile - 1, 0), j))
    rev = lambda i: (nt - 1 - i, 0)
    fixed = lambda i: (0, 0)
    return pl.pallas_call(
        body, name="gate_bwd", grid=(nt,),
        in_specs=[col(0), col(1), col(2), col(3), col(5), prev(1), prev(2), pl.BlockSpec((tm, dssm), rev),
                  pl.BlockSpec((tm, dc + dssm), rev), pl.BlockSpec(conv_w.shape, fixed), pl.BlockSpec((1, dc), fixed),
                  pl.BlockSpec(w_glu.shape, fixed), pl.BlockSpec((1, dssm), fixed)],
        out_specs=[pl.BlockSpec((tm, 6 * dc), rev), pl.BlockSpec((tm, dssm), rev),
                   pl.BlockSpec((2 * SUBLANES, dc), fixed), pl.BlockSpec((dssm, dssm), fixed)],
        out_shape=[jax.ShapeDtypeStruct((t, 6 * dc), BF16), jax.ShapeDtypeStruct((t, dssm), F32),
                   jax.ShapeDtypeStruct((2 * SUBLANES, dc), F32), jax.ShapeDtypeStruct((dssm, dssm), F32)],
        scratch_shapes=[pltpu.VMEM((tm + halo, dc), F32), pltpu.VMEM((tm + halo, dc), F32)],
        compiler_params=_cp(2 * (6 * 4 * tm * dc + 8 * tm * dc + 12 * tm * dc + 4 * tm * dc + 6 * dssm * dssm)
                            + 40 * tm * dc + 8 * MIB, dimension_semantics=("arbitrary",)),
    )(proj, proj, proj, proj, proj, proj, proj, y, dmix, conv_w, conv_b, w_glu, b_glu)


def _allgather_halves(parts, name):
    n = len(parts)
    per = 2 * (N_CHIPS - 1)

    def body(*refs):
        ins, outs = refs[:n], refs[n:2 * n]
        send, recv, local = refs[2 * n:]
        x, y, c = _my_place()
        k = 2 * x + y
        sib = (x, y, 1 - c)
        chips = _other_chips(x, y)

        def rc(t, s, src, dst, to):
            return pltpu.make_async_remote_copy(src_ref=src, dst_ref=dst, send_sem=send.at[t * per + s],
                                                recv_sem=recv.at[t * per + s], device_id=to, device_id_type=MESH)

        mine = [pltpu.make_async_copy(ins[t], outs[t].at[k], local.at[t]) for t in range(n)]
        for cp in mine:
            cp.start()
        sent = []
        for t in range(n):
            for j, (cx, cy) in enumerate(chips):
                cp = rc(t, j, ins[t].at[c], outs[t].at[k, c], (cx, cy, c))
                cp.start()
                sent.append(cp)
        for t in range(n):
            for j, (cx, cy) in enumerate(chips):
                landed = outs[t].at[2 * cx + cy, c]
                rc(t, j, landed, landed, (cx, cy, c)).wait_recv()
                cp = rc(t, N_CHIPS - 1 + j, landed, landed, sib)
                cp.start()
                sent.append(cp)
        for t in range(n):
            for j, (cx, cy) in enumerate(chips):
                other = outs[t].at[2 * cx + cy, 1 - c]
                rc(t, N_CHIPS - 1 + j, other, other, sib).wait_recv()
        for cp in sent:
            cp.wait_send()
        for cp in mine:
            cp.wait()

    any_spec = pl.BlockSpec(memory_space=pl.ANY)
    return pl.pallas_call(
        body, name=name, in_specs=[any_spec] * n, out_specs=[any_spec] * n,
        out_shape=[jax.ShapeDtypeStruct((N_CHIPS,) + p.shape, p.dtype) for p in parts],
        scratch_shapes=[pltpu.SemaphoreType.DMA((n * per,)), pltpu.SemaphoreType.DMA((n * per,)),
                        pltpu.SemaphoreType.DMA((n,))],
        compiler_params=_cp(16 * MIB),
    )(*parts)


def _allgather_flat(v, name):
    r, c = v.shape
    rels = [(dx, dy, dc) for dx in (0, 1) for dy in (0, 1) for dc in (0, 1)][1:]

    def body(v_ref, out_ref, send, recv):
        x, y, cc = _my_place()
        me = 4 * x + 2 * y + cc

        def peer(rel):
            dx, dy, dc = rel
            return (1 - x if dx else x, 1 - y if dy else y, 1 - cc if dc else cc)

        def rc(s, slot, to):
            return pltpu.make_async_remote_copy(src_ref=v_ref, dst_ref=out_ref.at[slot], send_sem=send.at[s],
                                                recv_sem=recv.at[s], device_id=to, device_id_type=MESH)

        sent = []
        for s, rel in enumerate(rels):
            cp = rc(s, me, peer(rel))
            cp.start()
            sent.append(cp)
        out_ref[me] = v_ref[...]
        for s, rel in enumerate(rels):
            px, py, pc = peer(rel)
            rc(s, 4 * px + 2 * py + pc, (px, py, pc)).wait_recv()
        for cp in sent:
            cp.wait_send()

    return pl.pallas_call(
        body, name=name, out_shape=jax.ShapeDtypeStruct((N_DEV, r, c), F32),
        in_specs=[pl.BlockSpec(memory_space=pltpu.VMEM)], out_specs=pl.BlockSpec(memory_space=pltpu.VMEM),
        scratch_shapes=[pltpu.SemaphoreType.DMA((N_DEV - 1,)), pltpu.SemaphoreType.DMA((N_DEV - 1,))],
        compiler_params=_cp((N_DEV + 2) * 4 * r * c + 8 * MIB),
    )(v)


def _rs_pair_exchange(grads, name):
    n = len(grads)

    def body(*refs):
        ins, outs = refs[:n], refs[n:2 * n]
        send, recv = refs[2 * n:]
        x, y, c = _my_place()
        cps = []
        for t in range(n):
            cp = pltpu.make_async_remote_copy(src_ref=ins[t].at[:, 1 - c], dst_ref=outs[t], send_sem=send.at[t],
                                              recv_sem=recv.at[t], device_id=(x, y, 1 - c), device_id_type=MESH)
            cp.start()
            cps.append(cp)
        for cp in cps:
            cp.wait()

    any_spec = pl.BlockSpec(memory_space=pl.ANY)
    return pl.pallas_call(
        body, name=name, in_specs=[any_spec] * n, out_specs=[any_spec] * n,
        out_shape=[jax.ShapeDtypeStruct((g.shape[0],) + g.shape[2:], F32) for g in grads],
        scratch_shapes=[pltpu.SemaphoreType.DMA((n,)), pltpu.SemaphoreType.DMA((n,))],
        compiler_params=_cp(16 * MIB),
    )(*grads)


def _rs_pair_add(place, grad, got, name):
    nk, _, r2, c = grad.shape
    tr = min(TR_ELT, r2)

    def body(place_ref, g_ref, r_ref, o16_ref, o32_ref):
        del place_ref
        s = g_ref[...] + r_ref[...]
        o32_ref[...] = s
        o16_ref[...] = s.astype(BF16)

    blk = pl.BlockSpec((None, tr, c), lambda k, i, p: (k, i, 0))
    return pl.pallas_call(
        body, name=name,
        grid_spec=pltpu.PrefetchScalarGridSpec(
            num_scalar_prefetch=1, grid=(nk, r2 // tr),
            in_specs=[pl.BlockSpec((None, None, tr, c), lambda k, i, p: (k, p[0], i, 0)), blk],
            out_specs=[blk, blk]),
        out_shape=[jax.ShapeDtypeStruct((nk, r2, c), BF16), jax.ShapeDtypeStruct((nk, r2, c), F32)],
        compiler_params=_cp(2 * 14 * tr * c + 8 * MIB),
    )(place, grad, got)


_HBM_SPEC = pl.BlockSpec(memory_space=pltpu.HBM)
_SEM_SPEC = pl.BlockSpec(memory_space=pltpu.SEMAPHORE)
_DATAFLOW = pltpu.SideEffectType.DATAFLOW_SIDE_EFFECTING


def _split_copy_start(srcs, land_shapes, plan, n_sems, name):
    ns, nl = len(srcs), len(land_shapes)

    def body(*refs):
        src_refs, land_refs = refs[:ns], refs[ns:ns + nl]
        send, recv = refs[ns + nl], refs[ns + nl + 1]
        token = refs[-1]
        sends, _ = plan(src_refs, land_refs)
        for src, dst, to, si, ri in sends:
            pltpu.make_async_remote_copy(src_ref=src, dst_ref=dst, send_sem=send.at[si], recv_sem=recv.at[ri],
                                         device_id=to, device_id_type=MESH).start()
        token[...] = jnp.zeros_like(token)

    lands = [lax.empty(shp, dt) for shp, dt in land_shapes]
    through = [pltpu.HBM(a.shape, a.dtype) for a in srcs] + [pltpu.HBM(shp, dt) for shp, dt in land_shapes]
    out = pl.pallas_call(
        body, name=name,
        out_shape=(pltpu.SemaphoreType.DMA((n_sems,)), pltpu.SemaphoreType.DMA((n_sems,)), *through,
                   jax.ShapeDtypeStruct((SUBLANES, SLAB), F32)),
        in_specs=[_HBM_SPEC] * (ns + nl),
        out_specs=(_SEM_SPEC, _SEM_SPEC, *([_HBM_SPEC] * (ns + nl)), pl.BlockSpec(memory_space=pltpu.VMEM)),
        input_output_aliases={i: 2 + i for i in range(ns + nl)},
        compiler_params=pltpu.CompilerParams(has_side_effects=_DATAFLOW),
    )(*[pltpu.with_memory_space_constraint(a, pltpu.HBM) for a in (*srcs, *lands)])
    return out[0], out[1], list(out[2:2 + ns]), list(out[2 + ns:2 + ns + nl]), out[-1]


def _split_copy_wait(send, recv, srcs, lands, plan, after, name):
    ns, nl, na = len(srcs), len(lands), len(after)

    def body(*refs):
        src_refs, land_refs = refs[:ns], refs[ns:ns + nl]
        send_ref, recv_ref = refs[ns + nl], refs[ns + nl + 1]
        sends, arrivals = plan(src_refs, land_refs)
        for src, dst, to, si, ri in sends:
            pltpu.make_async_remote_copy(src_ref=src, dst_ref=dst, send_sem=send_ref.at[si], recv_sem=recv_ref.at[ri],
                                         device_id=to, device_id_type=MESH).wait_send()
        for (src, _, to, si, _), (view, ri) in zip(sends, arrivals):
            pltpu.make_async_remote_copy(src_ref=view, dst_ref=view, send_sem=send_ref.at[si], recv_sem=recv_ref.at[ri],
                                         device_id=to, device_id_type=MESH).wait_recv()

    out = pl.pallas_call(
        body, name=name,
        out_shape=[pltpu.HBM(a.shape, a.dtype) for a in (*srcs, *lands)],
        in_specs=[_HBM_SPEC] * (ns + nl) + [_SEM_SPEC, _SEM_SPEC] + [pl.BlockSpec(memory_space=pl.ANY)] * na,
        out_specs=[_HBM_SPEC] * (ns + nl),
        input_output_aliases={i: i for i in range(ns + nl)},
        compiler_params=pltpu.CompilerParams(has_side_effects=_DATAFLOW),
    )(*srcs, *lands, send, recv, *after)
    return list(out[:ns]), list(out[ns:])


def _chip_exchange_plan(n):
    per = N_CHIPS - 1

    def plan(srcs, lands):
        x, y, c = _my_place()
        sends, arrivals = [], []
        for t in range(n):
            for j, (cx, cy) in enumerate(_other_chips(x, y)):
                sends.append((srcs[t].at[2 * cx + cy], lands[t].at[j], (cx, cy, c), t * per + j, t * per + j))
                arrivals.append((lands[t].at[j], t * per + j))
        return sends, arrivals

    return plan


def _gather_half_plan(n):
    per = N_CHIPS - 1

    def plan(srcs, lands):
        x, y, c = _my_place()
        k = 2 * x + y
        sends, arrivals = [], []
        for t in range(n):
            for j, (cx, cy) in enumerate(_other_chips(x, y)):
                sends.append((srcs[t].at[c], lands[t].at[k, c], (cx, cy, c), t * per + j, t * per + j))
                arrivals.append((lands[t].at[2 * cx + cy, c], t * per + j))
        return sends, arrivals

    return plan


def _gather_complete(own, landed, name):
    def body(own_ref, in_ref, out_ref, send, recv, local):
        x, y, c = _my_place()
        sib = (x, y, 1 - c)
        mine = pltpu.make_async_copy(own_ref, out_ref.at[2 * x + y], local)
        mine.start()
        sent = []
        for j, (cx, cy) in enumerate(_other_chips(x, y)):
            half = (2 * cx + cy, c)
            cp = pltpu.make_async_remote_copy(src_ref=in_ref.at[half], dst_ref=out_ref.at[half], send_sem=send.at[j],
                                              recv_sem=recv.at[j], device_id=sib, device_id_type=MESH)
            cp.start()
            sent.append(cp)
        for j, (cx, cy) in enumerate(_other_chips(x, y)):
            other = out_ref.at[2 * cx + cy, 1 - c]
            pltpu.make_async_remote_copy(src_ref=other, dst_ref=other, send_sem=send.at[j], recv_sem=recv.at[j],
                                         device_id=sib, device_id_type=MESH).wait_recv()
        for cp in sent:
            cp.wait_send()
        mine.wait()

    any_spec = pl.BlockSpec(memory_space=pl.ANY)
    per = N_CHIPS - 1
    return pl.pallas_call(
        body, name=name, in_specs=[any_spec, any_spec], out_specs=any_spec,
        out_shape=jax.ShapeDtypeStruct(landed.shape, landed.dtype), input_output_aliases={1: 0},
        scratch_shapes=[pltpu.SemaphoreType.DMA((per,)), pltpu.SemaphoreType.DMA((per,)), pltpu.SemaphoreType.DMA],
        compiler_params=_cp(16 * MIB),
    )(own, landed)


def _gather_direct_plan(n):
    per = 2 * (N_CHIPS - 1)

    def plan(srcs, lands):
        x, y, c = _my_place()
        k = 2 * x + y
        sends, arrivals = [], []
        for t in range(n):
            for j, (cx, cy) in enumerate(_other_chips(x, y)):
                for core in (0, 1):
                    sends.append((srcs[t].at[c], lands[t].at[k, c], (cx, cy, core),
                                  t * per + 2 * j + core, t * per + 2 * j + c))
                    arrivals.append((lands[t].at[2 * cx + cy, core], t * per + 2 * j + core))
        return sends, arrivals

    return plan


def _rs_chip_add(place, sums32, got, name):
    _, r2, c = sums32.shape
    tr = min(TR_ELT, r2)

    def body(place_ref, s_ref, q_ref, o_ref):
        del place_ref
        o_ref[...] = ((s_ref[...] + q_ref[0].astype(F32)) + q_ref[1].astype(F32)) + q_ref[2].astype(F32)

    return pl.pallas_call(
        body, name=name,
        grid_spec=pltpu.PrefetchScalarGridSpec(
            num_scalar_prefetch=1, grid=(r2 // tr,),
            in_specs=[pl.BlockSpec((None, tr, c), lambda i, p: (p[1], i, 0)),
                      pl.BlockSpec((N_CHIPS - 1, tr, c), lambda i, p: (0, i, 0))],
            out_specs=pl.BlockSpec((tr, c), lambda i, p: (i, 0))),
        out_shape=jax.ShapeDtypeStruct((r2, c), F32),
        compiler_params=_cp(2 * 14 * tr * c + 8 * MIB),
    )(place, sums32, got)


def _rs_pair_share(halves, name):
    n = len(halves)

    def body(*refs):
        ins, outs = refs[:n], refs[n:2 * n]
        send, recv, local = refs[2 * n:]
        x, y, c = _my_place()
        cps, mine = [], []
        for t in range(n):
            lc = pltpu.make_async_copy(ins[t], outs[t].at[c], local.at[t])
            lc.start()
            mine.append(lc)
            cp = pltpu.make_async_remote_copy(src_ref=ins[t], dst_ref=outs[t].at[c], send_sem=send.at[t],
                                              recv_sem=recv.at[t], device_id=(x, y, 1 - c), device_id_type=MESH)
            cp.start()
            cps.append(cp)
        for t in range(n):
            other = outs[t].at[1 - c]
            pltpu.make_async_remote_copy(src_ref=other, dst_ref=other, send_sem=send.at[t], recv_sem=recv.at[t],
                                         device_id=(x, y, 1 - c), device_id_type=MESH).wait_recv()
        for cp in cps:
            cp.wait_send()
        for lc in mine:
            lc.wait()

    any_spec = pl.BlockSpec(memory_space=pl.ANY)
    return pl.pallas_call(
        body, name=name, in_specs=[any_spec] * n, out_specs=[any_spec] * n,
        out_shape=[jax.ShapeDtypeStruct((2,) + h.shape, F32) for h in halves],
        scratch_shapes=[pltpu.SemaphoreType.DMA((n,)), pltpu.SemaphoreType.DMA((n,)), pltpu.SemaphoreType.DMA((n,))],
        compiler_params=_cp(16 * MIB),
    )(*halves)


_PACK_TILE = SUBLANES * SLAB


def _pack(arrays):
    rows = []
    for a in arrays:
        flat = a.reshape(-1).astype(F32)
        padded = -(-flat.shape[0] // _PACK_TILE) * _PACK_TILE
        rows.append(jnp.pad(flat, (0, padded - flat.shape[0])).reshape(-1, SLAB))
    n_rows = sum(r.shape[0] for r in rows)
    if n_rows % (2 * SUBLANES):
        rows.append(jnp.zeros((SUBLANES, SLAB), F32))
    return jnp.concatenate(rows, axis=0)


def _unpack(packed, shapes):
    out, row = [], 0
    for shp in shapes:
        size = math.prod(shp)
        nrow = -(-size // _PACK_TILE) * SUBLANES
        out.append(packed[row:row + nrow].reshape(-1)[:size].reshape(shp))
        row += nrow
    return out


def kernel(x, norm_pre_g, w_in, conv_w, conv_b, ssm_a_re, ssm_a_im, ssm_log_dt, ssm_b_re, ssm_b_im, ssm_c_re, ssm_c_im, ssm_d, w_glu, b_glu, w_out, norm_post_g, loss_target, m_norm_pre_g, m_w_in, m_conv_w, m_conv_b, m_ssm_a_re, m_ssm_a_im, m_ssm_log_dt, m_ssm_b_re, m_ssm_b_im, m_ssm_c_re, m_ssm_c_im, m_ssm_d, m_w_glu, m_b_glu, m_w_out, m_norm_post_g, v_norm_pre_g, v_w_in, v_conv_w, v_conv_b, v_ssm_a_re, v_ssm_a_im, v_ssm_log_dt, v_ssm_b_re, v_ssm_b_im, v_ssm_c_re, v_ssm_c_im, v_ssm_d, v_w_glu, v_b_glu, v_w_out, v_norm_post_g):
    xs, tgt = x[0], loss_target[0]
    t, d = xs.shape
    dc = conv_b.shape[0]
    dssm = ssm_d.shape[0]
    g, p = ssm_a_re.shape
    h = SSM_H
    nq = dssm // SLAB
    n_shard = w_in.shape[1]
    steps = min(T_SCAN, t) // SUBLANES
    mx, my, mc = _my_place()
    chip = 2 * mx + my
    place = jnp.stack([mc, chip]).astype(jnp.int32)

    cw_cols = conv_w.shape[1]
    cw_pad = -(-cw_cols // SLAB) * SLAB
    cw_blk = jnp.zeros((SUBLANES, cw_pad), F32).at[:conv_w.shape[0], :cw_cols].set(conv_w)
    cw_all = _allgather_flat(cw_blk, "allgather_conv_w")
    conv_w_full = jnp.concatenate([cw_all[2 * k, :, :cw_cols] for k in range(N_CHIPS)], axis=1)

    halves = lambda a: a.reshape(2, a.shape[0] // 2, a.shape[1])
    win_half = halves(_cast_bf16(w_in, "cast_w_in", cw_all[0, :, :SLAB]))
    win_plan = _gather_half_plan(1)
    win_send, win_recv, win_srcs, win_lands, win_token = _split_copy_start(
        [win_half], [((N_CHIPS,) + win_half.shape, BF16)], win_plan, N_CHIPS - 1, "gather_w_in_start")
    behind_w_in = win_token[0:1, 0:1]

    expand = jnp.repeat(jnp.eye(p, dtype=F32), h, axis=1)
    b_re2, b_im2 = ssm_b_re.reshape(g, p * h), ssm_b_im.reshape(g, p * h)
    log_dt2 = ssm_log_dt.reshape(g, 1) + behind_w_in
    lbr, lbi, bbr2, bbi2, pw3 = _zoh_fwd(ssm_a_re, ssm_a_im, log_dt2, b_re2, b_im2, expand, steps)
    lam = jnp.stack([lbr.reshape(g * p), lbi.reshape(g * p)])
    pw = pw3.reshape(2, g * p)
    to_slab_b = lambda b2: _blockdiag(b2.reshape(nq, GROUPS_PER_SLAB, p, h).transpose(0, 1, 3, 2))
    bq = jnp.concatenate([to_slab_b(bbr2), to_slab_b(bbi2)], axis=2).astype(BF16)
    to_slab_c = lambda c3: _blockdiag(c3.reshape(nq, GROUPS_PER_SLAB, h, p).transpose(0, 1, 3, 2))
    cq = jnp.concatenate([to_slab_c(ssm_c_re), to_slab_c(-ssm_c_im)], axis=1).astype(BF16)

    g_pre2, g_post2 = norm_pre_g.reshape(1, d), norm_post_g.reshape(1, d)
    conv_b2, b_glu2, d_skip2 = conv_b.reshape(1, dc), b_glu.reshape(1, dssm), ssm_d.reshape(1, dssm)
    u_block = 4 * dc // dssm

    hb = _prenorm(xs, g_pre2 + behind_w_in)
    proj_own = _inproj_own(place, hb, win_srcs[0].reshape(d, n_shard))
    win_own, win_landed = _split_copy_wait(win_send, win_recv, win_srcs, win_lands, win_plan,
                                           [proj_own, bq, cq], "gather_w_in_wait")
    win_g = _gather_complete(win_own[0], win_landed[0], "gather_w_in_complete")
    win_b = win_g.reshape(N_CHIPS, d, n_shard)
    gathered = win_g[0, 0, :SUBLANES, :SLAB].astype(F32)
    side = [halves(_cast_bf16(w_out, "cast_w_out", gathered)), halves(_cast_bf16(w_glu, "cast_w_glu", gathered))]
    side_plan = _gather_direct_plan(len(side))
    side_sems = 2 * (N_CHIPS - 1) * len(side)
    ag_send, ag_recv, ag_srcs, ag_lands, ag_token = _split_copy_start(
        side, [((N_CHIPS,) + a.shape, BF16) for a in side], side_plan, side_sems, "gather_side_weights_start")
    proj = _inproj_rest(place, hb, win_b, proj_own, ag_token)
    y, cin = _ssm_fwd(proj, u_block, bq, cq, lam, pw, d_skip2)
    side_own, side_all = _split_copy_wait(ag_send, ag_recv, ag_srcs, ag_lands, side_plan, [cin],
                                          "gather_side_weights_wait")
    wout_g, wglu_g = [lax.dynamic_update_index_in_dim(all_, own, chip, 0) for own, all_ in zip(side_own, side_all)]
    wout_b = wout_g.reshape(dc + dssm, d)
    wglu_b = wglu_g.reshape(dssm, dssm)
    mix = _gate_fwd(proj, y, conv_w_full, conv_b2, wglu_b, b_glu2, dc)
    loss_blk, dout, dob, dmix, gg_post = _outproj(mix, wout_b, xs, tgt, g_post2)

    def reduce_start(grads, tags, group):
        got = _rs_pair_exchange(grads, "rs_pair_exchange_" + group)
        sums = [_rs_pair_add(place, gt, rt, "rs_pair_add_" + tg) for gt, rt, tg in zip(grads, got, tags)]
        plan = _chip_exchange_plan(len(grads))
        started = _split_copy_start([s16 for s16, _ in sums],
                                    [((N_CHIPS - 1,) + s16.shape[1:], BF16) for s16, _ in sums], plan,
                                    (N_CHIPS - 1) * len(grads), "rs_chip_exchange_" + group + "_start")
        return plan, started, [s32 for _, s32 in sums]

    def reduce_finish(plan, started, sums32, tags, group, after):
        send, recv, srcs, lands, _ = started
        _, landed = _split_copy_wait(send, recv, srcs, lands, plan, after, "rs_chip_exchange_" + group + "_wait")
        mine = [_rs_chip_add(place, s32, qt, "rs_chip_add_" + tg) for s32, qt, tg in zip(sums32, landed, tags)]
        full = _rs_pair_share(mine, "rs_pair_share_" + group)
        return [f.reshape(2 * f.shape[1], f.shape[2]) for f in full]

    gw_out = _matmul_tn(mix, dob, 1, "grad_w_out")
    dproj, dy, gsmall, gw_glu = _gate_bwd(proj, y, dmix, conv_w_full, conv_b2, wglu_b, b_glu2, dc)
    rs_a = reduce_start([gw_out.reshape(N_CHIPS, 2, (dc + dssm) // (2 * N_CHIPS), d),
                         gw_glu.reshape(N_CHIPS, 2, dssm // (2 * N_CHIPS), dssm)], ["w_out", "w_glu"], "a")
    dproj, gb_dense, gc_dense, glam, gd = _ssm_bwd(proj, u_block, dy, cin, dproj, bq, cq, lam, pw,
                                                   d_skip2 + rs_a[1][4][0:1, 0:1])
    gw_in = _matmul_tn(hb, dproj, N_CHIPS, "grad_w_in")
    rs_b = reduce_start([gw_in.reshape(N_CHIPS, 2, d // 2, n_shard)], ["w_in"], "b")
    gx, gg_pre = _dh_prenorm_bwd(dproj, win_b, xs, dout, g_pre2 + rs_b[1][4][0:1, 0:1])

    gb4 = gb_dense.reshape(nq, SLAB, 2, SLAB_STATES)
    g_bbr2 = _blockdiag_take(gb4[:, :, 0, :], h, p).transpose(0, 1, 3, 2).reshape(g, p * h)
    g_bbi2 = _blockdiag_take(gb4[:, :, 1, :], h, p).transpose(0, 1, 3, 2).reshape(g, p * h)
    gc4 = gc_dense.reshape(nq, SLAB, 2, SLAB_STATES)
    g_c_re = _blockdiag_take(gc4[:, :, 0, :], h, p).reshape(g, h, p)
    g_c_im = -_blockdiag_take(gc4[:, :, 1, :], h, p).reshape(g, h, p)
    g_a_re, g_a_im, g_ld, g_b_re2, g_b_im2 = _zoh_bwd(
        ssm_a_re, ssm_a_im, log_dt2, b_re2, b_im2, expand,
        glam[0].reshape(g, p), glam[1].reshape(g, p), g_bbr2, g_bbi2)

    small_names = ["norm_pre_g", "conv_w", "conv_b", "ssm_a_re", "ssm_a_im", "ssm_log_dt", "ssm_b_re", "ssm_b_im",
                   "ssm_c_re", "ssm_c_im", "ssm_d", "b_glu", "norm_post_g", "loss"]
    small_g = {
        "norm_pre_g": gg_pre[0], "conv_w": gsmall[0:3], "conv_b": gsmall[3], "ssm_a_re": g_a_re, "ssm_a_im": g_a_im,
        "ssm_log_dt": g_ld.reshape(g), "ssm_b_re": g_b_re2.reshape(g, p, h), "ssm_b_im": g_b_im2.reshape(g, p, h),
        "ssm_c_re": g_c_re, "ssm_c_im": g_c_im, "ssm_d": gd[0], "b_glu": gsmall[4], "norm_post_g": gg_post[0],
        "loss": loss_blk[0, 0:1],
    }
    zeros_cw = jnp.zeros((conv_w.shape[0], dc), F32)
    one0 = jnp.zeros((1,), F32)
    small_w = dict(norm_pre_g=norm_pre_g, conv_w=zeros_cw, conv_b=conv_b, ssm_a_re=ssm_a_re, ssm_a_im=ssm_a_im,
                   ssm_log_dt=ssm_log_dt, ssm_b_re=ssm_b_re, ssm_b_im=ssm_b_im, ssm_c_re=ssm_c_re, ssm_c_im=ssm_c_im,
                   ssm_d=ssm_d, b_glu=b_glu, norm_post_g=norm_post_g, loss=one0)
    small_m = dict(norm_pre_g=m_norm_pre_g, conv_w=zeros_cw, conv_b=m_conv_b, ssm_a_re=m_ssm_a_re, ssm_a_im=m_ssm_a_im,
                   ssm_log_dt=m_ssm_log_dt, ssm_b_re=m_ssm_b_re, ssm_b_im=m_ssm_b_im, ssm_c_re=m_ssm_c_re,
                   ssm_c_im=m_ssm_c_im, ssm_d=m_ssm_d, b_glu=m_b_glu, norm_post_g=m_norm_post_g, loss=one0)
    small_v = dict(norm_pre_g=v_norm_pre_g, conv_w=zeros_cw, conv_b=v_conv_b, ssm_a_re=v_ssm_a_re, ssm_a_im=v_ssm_a_im,
                   ssm_log_dt=v_ssm_log_dt, ssm_b_re=v_ssm_b_re, ssm_b_im=v_ssm_b_im, ssm_c_re=v_ssm_c_re,
                   ssm_c_im=v_ssm_c_im, ssm_d=v_ssm_d, b_glu=v_b_glu, norm_post_g=v_norm_post_g, loss=one0)
    shapes = [small_w[nm].shape for nm in small_names]
    g_pack = _allreduce_small(_pack([small_g[nm] for nm in small_names]))
    packs = _adamw_small(g_pack, _pack([small_w[nm] for nm in small_names]),
                         _pack([small_m[nm] for nm in small_names]), _pack([small_v[nm] for nm in small_names]))
    sg, sd, sm, sv = [dict(zip(small_names, _unpack(pk, shapes))) for pk in (g_pack, *packs)]
    loss = sg["loss"][0]

    g_cw = lax.dynamic_slice_in_dim(sg["conv_w"], chip * cw_cols, cw_cols, axis=1)
    pad_cw = lambda a: jnp.zeros((SUBLANES, cw_pad), F32).at[:a.shape[0], :cw_cols].set(a)
    cut_cw = lambda a: a[:conv_w.shape[0], :cw_cols]
    d_cw, m_cw, v_cw = [cut_cw(a) for a in _adamw(pad_cw(conv_w), pad_cw(g_cw), pad_cw(m_conv_w), pad_cw(v_conv_w),
                                                  "adamw_conv_w")]

    g_wout, g_wglu = reduce_finish(*rs_a, ["w_out", "w_glu"], "a", [d_cw, packs[0]])
    (g_win,) = reduce_finish(*rs_b, ["w_in"], "b", [g_wout])
    d_win, m_win, v_win = _adamw(w_in, g_win, m_w_in, v_w_in, "adamw_w_in")
    d_wout, m_wout, v_wout = _adamw(w_out, g_wout, m_w_out, v_w_out, "adamw_w_out")
    d_wglu, m_wglu, v_wglu = _adamw(w_glu, g_wglu, m_w_glu, v_w_glu, "adamw_w_glu")

    order = ["norm_pre_g", "w_in", "conv_w", "conv_b", "ssm_a_re", "ssm_a_im", "ssm_log_dt", "ssm_b_re", "ssm_b_im",
             "ssm_c_re", "ssm_c_im", "ssm_d", "w_glu", "b_glu", "w_out", "norm_post_g"]
    grads, deltas, new_m, new_v = dict(sg), dict(sd), dict(sm), dict(sv)
    grads.update(w_in=g_win, w_out=g_wout, w_glu=g_wglu, conv_w=g_cw)
    deltas.update(w_in=d_win, w_out=d_wout, w_glu=d_wglu, conv_w=d_cw)
    new_m.update(w_in=m_win, w_out=m_wout, w_glu=m_wglu, conv_w=m_cw)
    new_v.update(w_in=v_win, w_out=v_wout, w_glu=v_wglu, conv_w=v_cw)
    return (loss, gx[None], *[grads[nm] for nm in order], *[deltas[nm] for nm in order],
            *[new_m[nm] for nm in order], *[new_v[nm] for nm in order])
```

```python
import functools
import math

import jax
import jax.numpy as jnp
from jax import lax
from jax.experimental import pallas as pl
from jax.experimental.pallas import tpu as pltpu

F32 = jnp.float32
BF16 = jnp.bfloat16
MESH = pl.DeviceIdType.MESH

EPS = 1e-6
SSM_H = 16
SSM_P = 64
GROUPS_PER_SLAB = 8
SLAB = 128
SLAB_STATES = GROUPS_PER_SLAB * SSM_P
N_CHIPS = 4
N_DEV = 8

ADAM_LR = 0.001
ADAM_B1 = 0.9
ADAM_B2 = 0.999
ADAM_EPS = 1e-08
ADAM_WD = 0.01
ADAM_STEP = 10

MIB = 1024 * 1024
VMEM_CAP = 48 * MIB
SUBLANES = 8

TM_NORM = 512
TM_PROJ = 512
TM_GATE = 256
TM_OUT = 256
TM_DH = 256
T_SCAN = 256
TK_TN = 512
SCAN_UNROLL = 4
TM_TN = 1024
TR_ELT = 256


def _cp(vmem_bytes, **kw):
    return pltpu.CompilerParams(vmem_limit_bytes=int(min(VMEM_CAP, max(16 * MIB, vmem_bytes))), **kw)


def _pallas(body, **kw):
    if "grid" not in kw and "grid_spec" not in kw:
        return pl.pallas_call(body, **kw)
    pin = lambda s: pltpu.HBM(s.shape, s.dtype) if isinstance(s, jax.ShapeDtypeStruct) else s
    out_shape = kw.pop("out_shape")
    out_shape = [pin(s) for s in out_shape] if isinstance(out_shape, (list, tuple)) else pin(out_shape)
    call = pl.pallas_call(body, out_shape=out_shape, **kw)

    def run(*args):
        return call(*[pltpu.with_memory_space_constraint(a, pltpu.HBM) if jnp.issubdtype(a.dtype, jnp.floating) else a
                      for a in args])

    return run


def _my_place():
    return lax.axis_index("x"), lax.axis_index("y"), lax.axis_index("c")


def _other_chips(x, y):
    return [(1 - x, y), (x, 1 - y), (1 - x, 1 - y)]


def _silu(z):
    s = jax.nn.sigmoid(z)
    return z * s, s


def _dsilu(z, s):
    return s * (1.0 + z * (1.0 - s))


_GELU_K = math.sqrt(2.0 / math.pi)
_GELU_C = 0.044715


def _gelu(y):
    th = jnp.tanh(_GELU_K * (y + _GELU_C * y * y * y))
    return 0.5 * y * (1.0 + th), th


def _dgelu(y, th):
    return 0.5 * (1.0 + th) + 0.5 * y * (1.0 - th * th) * _GELU_K * (1.0 + 3.0 * _GELU_C * y * y)


def _cast_bf16(w, name, after=None):
    r, c = w.shape
    tr = min(TR_ELT, r)
    extra = [] if after is None else [after]

    def body(w_ref, *rest):
        rest[-1][...] = w_ref[...].astype(BF16)

    return _pallas(
        body, name=name, grid=(r // tr,),
        in_specs=[pl.BlockSpec((tr, c), lambda i: (i, 0))] + [pl.BlockSpec((SUBLANES, SLAB), lambda i: (0, 0))] * len(extra),
        out_specs=pl.BlockSpec((tr, c), lambda i: (i, 0)),
        out_shape=jax.ShapeDtypeStruct((r, c), BF16),
        compiler_params=_cp(12 * tr * c),
    )(w, *extra)


def _prenorm(x, g):
    t, d = x.shape
    tm = min(TM_NORM, t)

    def body(x_ref, g_ref, h_ref):
        xv = x_ref[...]
        r = lax.rsqrt(jnp.mean(xv * xv, axis=-1, keepdims=True) + EPS)
        h_ref[...] = (xv * r * g_ref[...]).astype(BF16)

    return _pallas(
        body, name="prenorm", grid=(t // tm,),
        in_specs=[pl.BlockSpec((tm, d), lambda i: (i, 0)), pl.BlockSpec((1, d), lambda i: (0, 0))],
        out_specs=pl.BlockSpec((tm, d), lambda i: (i, 0)),
        out_shape=jax.ShapeDtypeStruct((t, d), BF16),
        compiler_params=_cp(20 * tm * d),
    )(x, g)


def _inproj_own(place, a, w_own):
    t, k = a.shape
    n = w_own.shape[1]
    tm = min(TM_PROJ, t)

    def body(place_ref, a_ref, b_ref, o_ref):
        del place_ref
        o_ref[...] = jnp.dot(a_ref[...], b_ref[...], preferred_element_type=F32)

    return _pallas(
        body, name="inproj_own",
        grid_spec=pltpu.PrefetchScalarGridSpec(
            num_scalar_prefetch=1, grid=(t // tm,),
            in_specs=[pl.BlockSpec((tm, k), lambda i, p: (i, 0)), pl.BlockSpec((k, n), lambda i, p: (0, 0))],
            out_specs=pl.BlockSpec((tm, n), lambda i, p: (i, p[1]))),
        out_shape=jax.ShapeDtypeStruct((t, N_CHIPS * n), F32),
        compiler_params=_cp(2 * (2 * tm * k + 2 * k * n + 4 * tm * n) + 4 * MIB),
    )(place, a, w_own)


def _inproj_rest(place, a, b, partial, after):
    t, k = a.shape
    nb, _, n = b.shape
    tm = min(TM_PROJ, t)
    shard = lambda s, p: (p[1] + 1 + s) % nb

    def body(place_ref, a_ref, b_ref, after_ref, partial_ref, o_ref):
        del place_ref, after_ref, partial_ref
        o_ref[...] = jnp.dot(a_ref[...], b_ref[...], preferred_element_type=F32)

    return _pallas(
        body, name="inproj_rest",
        grid_spec=pltpu.PrefetchScalarGridSpec(
            num_scalar_prefetch=1, grid=(nb - 1, t // tm),
            in_specs=[pl.BlockSpec((tm, k), lambda s, i, p: (i, 0)),
                      pl.BlockSpec((None, k, n), lambda s, i, p: (shard(s, p), 0, 0)),
                      pl.BlockSpec((SUBLANES, SLAB), lambda s, i, p: (0, 0)),
                      pl.BlockSpec(memory_space=pl.ANY)],
            out_specs=pl.BlockSpec((tm, n), lambda s, i, p: (i, shard(s, p)))),
        out_shape=jax.ShapeDtypeStruct((t, nb * n), F32),
        input_output_aliases={4: 0},
        compiler_params=_cp(2 * (2 * tm * k + 2 * k * n + 4 * tm * n) + 4 * MIB),
    )(place, a, b, after, partial)


def _matmul_tn(a, b, nb, name):
    t, m = a.shape
    n = b.shape[1] // nb
    tk = min(TK_TN, t)
    tma = min(TM_TN, m)

    def body(a_ref, b_ref, o_ref):
        @pl.when(pl.program_id(2) == 0)
        def _():
            o_ref[...] = jnp.zeros_like(o_ref)

        o_ref[...] += lax.dot_general(a_ref[...], b_ref[...], (((0,), (0,)), ((), ())), preferred_element_type=F32)

    return _pallas(
        body, name=name, grid=(nb, m // tma, t // tk),
        in_specs=[pl.BlockSpec((tk, tma), lambda j, i, k: (k, i)), pl.BlockSpec((tk, n), lambda j, i, k: (k, j))],
        out_specs=pl.BlockSpec((None, tma, n), lambda j, i, k: (j, i, 0)),
        out_shape=jax.ShapeDtypeStruct((nb, m, n), F32),
        compiler_params=_cp(2 * (2 * tk * tma + 2 * tk * n + 4 * tma * n) + 8 * MIB),
    )(a, b)


def _outproj(mix, w_out, x, tgt, g_post):
    t, dm = mix.shape
    d = w_out.shape[1]
    tm = min(TM_OUT, t)

    def body(mix_ref, w_ref, x_ref, t_ref, g_ref, loss_ref, dout_ref, do_ref, dmix_ref, gg_ref):
        @pl.when(pl.program_id(0) == 0)
        def _():
            loss_ref[...] = jnp.zeros_like(loss_ref)
            gg_ref[...] = jnp.zeros_like(gg_ref)

        w = w_ref[...]
        o = jnp.dot(mix_ref[...], w, preferred_element_type=F32)
        r = lax.rsqrt(jnp.mean(o * o, axis=-1, keepdims=True) + EPS)
        nh = o * r
        g = g_ref[...]
        e = x_ref[...] + nh * g - t_ref[...]
        loss_ref[...] += jnp.sum(e * e) * (0.5 / d)
        dout = e * (1.0 / d)
        dout_ref[...] = dout
        gg_ref[0:1, :] += jnp.sum(dout * nh, axis=0, keepdims=True)
        dn = dout * g
        do = r * (dn - nh * jnp.mean(dn * nh, axis=-1, keepdims=True))
        dob = do.astype(BF16)
        do_ref[...] = dob
        dmix_ref[...] = lax.dot_general(dob, w, (((1,), (1,)), ((), ())), preferred_element_type=F32)

    row = lambda i: (i, 0)
    fixed = lambda i: (0, 0)
    return _pallas(
        body, name="outproj", grid=(t // tm,),
        in_specs=[pl.BlockSpec((tm, dm), row), pl.BlockSpec((dm, d), fixed), pl.BlockSpec((tm, d), row),
                  pl.BlockSpec((tm, d), row), pl.BlockSpec((1, d), fixed)],
        out_specs=[pl.BlockSpec((SUBLANES, SLAB), fixed), pl.BlockSpec((tm, d), row), pl.BlockSpec((tm, d), row),
                   pl.BlockSpec((tm, dm), row), pl.BlockSpec((SUBLANES, d), fixed)],
        out_shape=[jax.ShapeDtypeStruct((SUBLANES, SLAB), F32), jax.ShapeDtypeStruct((t, d), F32),
                   jax.ShapeDtypeStruct((t, d), BF16), jax.ShapeDtypeStruct((t, dm), F32),
                   jax.ShapeDtypeStruct((SUBLANES, d), F32)],
        compiler_params=_cp(2 * (2 * dm * d + tm * (2 * dm + 4 * d * 3 + 2 * d + 4 * dm)) + 16 * MIB),
    )(mix, w_out, x, tgt, g_post)


def _dh_prenorm_bwd(dproj, w_in, x, dout, g_pre):
    t, d = x.shape
    nb, _, n = w_in.shape
    tm = min(TM_DH, t)

    def body(dp_ref, w_ref, x_ref, dout_ref, g_ref, dx_ref, gg_ref):
        @pl.when(pl.program_id(0) == 0)
        def _():
            gg_ref[...] = jnp.zeros_like(gg_ref)

        dh = None
        for k in range(nb):
            part = lax.dot_general(dp_ref[:, k * n:(k + 1) * n], w_ref[k], (((1,), (1,)), ((), ())),
                                   preferred_element_type=F32)
            dh = part if dh is None else dh + part
        xv = x_ref[...]
        r = lax.rsqrt(jnp.mean(xv * xv, axis=-1, keepdims=True) + EPS)
        xh = xv * r
        gg_ref[0:1, :] += jnp.sum(dh * xh, axis=0, keepdims=True)
        dg = dh * g_ref[...]
        dx_ref[...] = dout_ref[...] + r * (dg - xh * jnp.mean(dg * xh, axis=-1, keepdims=True))

    row = lambda i: (i, 0)
    fixed = lambda i: (0, 0)
    w_spec = pl.BlockSpec(w_in.shape, lambda i: (0, 0, 0), pipeline_mode=pl.Buffered(1))
    return _pallas(
        body, name="dh_prenorm_bwd", grid=(t // tm,),
        in_specs=[pl.BlockSpec((tm, nb * n), row), w_spec,
                  pl.BlockSpec((tm, d), row), pl.BlockSpec((tm, d), row), pl.BlockSpec((1, d), fixed)],
        out_specs=[pl.BlockSpec((tm, d), row), pl.BlockSpec((SUBLANES, d), fixed)],
        out_shape=[jax.ShapeDtypeStruct((t, d), F32), jax.ShapeDtypeStruct((SUBLANES, d), F32)],
        compiler_params=_cp(2 * nb * d * n + 2 * (2 * tm * nb * n + 12 * tm * d) + 16 * tm * d + 4 * MIB),
    )(dproj, w_in, x, dout, g_pre)


def _adamw(w, g, m, v, name):
    r, c = w.shape
    tr = min(TR_ELT, r)
    c1 = 1.0 - ADAM_B1 ** ADAM_STEP
    c2 = 1.0 - ADAM_B2 ** ADAM_STEP

    def body(w_ref, g_ref, m_ref, v_ref, d_ref, mo_ref, vo_ref):
        gv = g_ref[...]
        mn = ADAM_B1 * m_ref[...] + (1.0 - ADAM_B1) * gv
        vn = ADAM_B2 * v_ref[...] + (1.0 - ADAM_B2) * (gv * gv)
        d_ref[...] = -ADAM_LR * ((mn / c1) / (jnp.sqrt(vn / c2) + ADAM_EPS) + ADAM_WD * w_ref[...])
        mo_ref[...] = mn
        vo_ref[...] = vn

    spec = pl.BlockSpec((tr, c), lambda i: (i, 0))
    sds = jax.ShapeDtypeStruct((r, c), F32)
    return _pallas(
        body, name=name, grid=(r // tr,), in_specs=[spec] * 4, out_specs=[spec] * 3, out_shape=[sds] * 3,
        compiler_params=_cp(2 * 7 * 4 * tr * c + 8 * MIB),
    )(w, g, m, v)


def _adamw_small(g, w, m, v):
    r, c = w.shape
    c1 = 1.0 - ADAM_B1 ** ADAM_STEP
    c2 = 1.0 - ADAM_B2 ** ADAM_STEP

    def body(g_ref, w_ref, m_ref, v_ref, d_ref, mo_ref, vo_ref):
        gv = g_ref[...]
        mn = ADAM_B1 * m_ref[...] + (1.0 - ADAM_B1) * gv
        vn = ADAM_B2 * v_ref[...] + (1.0 - ADAM_B2) * (gv * gv)
        d_ref[...] = -ADAM_LR * ((mn / c1) / (jnp.sqrt(vn / c2) + ADAM_EPS) + ADAM_WD * w_ref[...])
        mo_ref[...] = mn
        vo_ref[...] = vn

    sds = jax.ShapeDtypeStruct((r, c), F32)
    return _pallas(
        body, name="adamw_small", out_shape=[sds] * 3,
        compiler_params=_cp(12 * 4 * r * c + 8 * MIB),
    )(g, w, m, v)


def _allreduce_small(v):
    r, c = v.shape
    assert r % (2 * SUBLANES) == 0
    h = r // 2

    def body(v_ref, out_ref, got_ref, chip_ref, slots_ref, send, recv):
        x, y, cc = _my_place()
        k = 2 * x + y
        sib = (x, y, 1 - cc)

        def rc(s, src, dst, to):
            return pltpu.make_async_remote_copy(src_ref=src, dst_ref=dst, send_sem=send.at[s], recv_sem=recv.at[s],
                                                device_id=to, device_id_type=MESH)

        pair = rc(0, v_ref, got_ref, sib)
        pair.start()
        pair.wait()
        chip_ref[...] = v_ref[...] + got_ref[...]
        mine = pl.ds(pl.multiple_of(cc * h, SUBLANES), h)
        theirs = pl.ds(pl.multiple_of((1 - cc) * h, SUBLANES), h)
        chips = _other_chips(x, y)
        sent = []
        for j, (cx, cy) in enumerate(chips):
            cp = rc(1 + j, chip_ref.at[mine], slots_ref.at[k], (cx, cy, cc))
            cp.start()
            sent.append(cp)
        slots_ref[k] = chip_ref[mine, :]
        for j, (cx, cy) in enumerate(chips):
            rc(1 + j, chip_ref.at[mine], slots_ref.at[2 * cx + cy], (cx, cy, cc)).wait_recv()
        total = slots_ref[0]
        for kk in range(1, N_CHIPS):
            total = total + slots_ref[kk]
        out_ref[mine, :] = total
        for cp in sent:
            cp.wait_send()
        share = rc(N_CHIPS, out_ref.at[mine], out_ref.at[mine], sib)
        share.start()
        rc(N_CHIPS, out_ref.at[theirs], out_ref.at[theirs], sib).wait_recv()
        share.wait_send()

    vm = pl.BlockSpec(memory_space=pltpu.VMEM)
    return _pallas(
        body, name="allreduce_small", out_shape=jax.ShapeDtypeStruct((r, c), F32), in_specs=[vm], out_specs=vm,
        scratch_shapes=[pltpu.VMEM((r, c), F32), pltpu.VMEM((r, c), F32), pltpu.VMEM((N_CHIPS, h, c), F32),
                        pltpu.SemaphoreType.DMA((N_CHIPS + 1,)), pltpu.SemaphoreType.DMA((N_CHIPS + 1,))],
        compiler_params=_cp(6 * 4 * r * c + 8 * MIB),
    )(v)


def _zoh(a_re, a_im, log_dt, b_re2, b_im2, expand):
    dt = jnp.exp(log_dt)
    mag = jnp.exp(a_re * dt)
    lbr, lbi = mag * jnp.cos(a_im * dt), mag * jnp.sin(a_im * dt)
    nr, ni = lbr - 1.0, lbi
    den = a_re * a_re + a_im * a_im
    qr = (nr * a_re + ni * a_im) / den
    qi = (ni * a_re - nr * a_im) / den
    qr2 = jnp.dot(qr, expand, precision=lax.Precision.HIGHEST, preferred_element_type=F32)
    qi2 = jnp.dot(qi, expand, precision=lax.Precision.HIGHEST, preferred_element_type=F32)
    return lbr, lbi, qr2 * b_re2 - qi2 * b_im2, qr2 * b_im2 + qi2 * b_re2


def _zoh_fwd(a_re, a_im, log_dt, b_re2, b_im2, expand, power):
    g, p = a_re.shape

    def body(ar_ref, ai_ref, ld_ref, br_ref, bi_ref, e_ref, lbr_ref, lbi_ref, bbr_ref, bbi_ref, pw_ref):
        ar, ai, ld = ar_ref[...], ai_ref[...], ld_ref[...]
        lbr, lbi, bbr, bbi = _zoh(ar, ai, ld, br_ref[...], bi_ref[...], e_ref[...])
        lbr_ref[...], lbi_ref[...], bbr_ref[...], bbi_ref[...] = lbr, lbi, bbr, bbi
        dt = jnp.exp(ld) * float(power)
        mag = jnp.exp(ar * dt)
        pw_ref[0] = mag * jnp.cos(ai * dt)
        pw_ref[1] = mag * jnp.sin(ai * dt)

    gp = jax.ShapeDtypeStruct((g, p), F32)
    gph = jax.ShapeDtypeStruct(b_re2.shape, F32)
    return _pallas(
        body, name="zoh_fwd", out_shape=[gp, gp, gph, gph, jax.ShapeDtypeStruct((2, g, p), F32)],
        compiler_params=_cp(16 * MIB),
    )(a_re, a_im, log_dt, b_re2, b_im2, expand)


def _zoh_bwd(a_re, a_im, log_dt, b_re2, b_im2, expand, g_lbr, g_lbi, g_bbr, g_bbi):
    def body(ar_ref, ai_ref, ld_ref, br_ref, bi_ref, e_ref, c0, c1, c2, c3, gar, gai, gld, gbr, gbi):
        e = e_ref[...]
        _, vjp = jax.vjp(lambda a, b, c, d, f: _zoh(a, b, c, d, f, e),
                         ar_ref[...], ai_ref[...], ld_ref[...], br_ref[...], bi_ref[...])
        gar[...], gai[...], gld[...], gbr[...], gbi[...] = vjp((c0[...], c1[...], c2[...], c3[...]))

    sds = lambda a: jax.ShapeDtypeStruct(a.shape, F32)
    return _pallas(
        body, name="zoh_bwd", out_shape=[sds(a_re), sds(a_im), sds(log_dt), sds(b_re2), sds(b_im2)],
        compiler_params=_cp(16 * MIB),
    )(a_re, a_im, log_dt, b_re2, b_im2, expand, g_lbr, g_lbi, g_bbr, g_bbi)


def _blockdiag(blocks):
    nq, _, r, c = blocks.shape
    eye = jnp.eye(GROUPS_PER_SLAB, dtype=blocks.dtype)
    out = blocks[:, :, :, None, :] * eye[None, :, None, :, None]
    return out.reshape(nq, GROUPS_PER_SLAB * r, GROUPS_PER_SLAB * c)


def _blockdiag_take(dense, r, c):
    nq = dense.shape[0]
    d5 = dense.reshape(nq, GROUPS_PER_SLAB, r, GROUPS_PER_SLAB, c)
    return jnp.stack([d5[:, i, :, i, :] for i in range(GROUPS_PER_SLAB)], axis=1)


def _scan_slab(s_ref, row0, q, lam_ref, pw_ref, car_ref, steps, reverse, prev_ref=None, prev_row0=0, glam_ref=None):
    sign = -1.0 if reverse else 1.0
    half = SLAB_STATES // SLAB
    cols = [(q * 2 * half + m, q * 2 * half + half + m, q * SLAB_STATES + m * SLAB) for m in range(half)]
    nm = len(cols)
    full = (SUBLANES, SLAB)
    lam = [(jnp.broadcast_to(lam_ref[0:1, pl.ds(cl, SLAB)], full),
            jnp.broadcast_to(sign * lam_ref[1:2, pl.ds(cl, SLAB)], full)) for (_, _, cl) in cols]

    def step_rows(jj, base):
        j = (steps - 1 - jj) if reverse else jj
        return j, pl.ds(pl.multiple_of(base + j * SUBLANES, SUBLANES), SUBLANES)

    def pass1(jj, car):
        _, rows = step_rows(jj, row0)
        out = []
        for m, (cr, ci, _) in enumerate(cols):
            sr, si = car[2 * m], car[2 * m + 1]
            lr, li = lam[m]
            nr = lr * sr - li * si + s_ref[cr, rows, :]
            ni = lr * si + li * sr + s_ref[ci, rows, :]
            s_ref[cr, rows, :] = nr
            s_ref[ci, rows, :] = ni
            out += [nr, ni]
        return tuple(out)

    def unrolled(step_fn):
        def outer(jo, carry):
            for k in range(SCAN_UNROLL):
                carry = step_fn(jo * SCAN_UNROLL + k, carry)
            return carry
        return outer

    assert steps % SCAN_UNROLL == 0
    ends = lax.fori_loop(0, steps // SCAN_UNROLL, unrolled(pass1), tuple(jnp.zeros(full, F32) for _ in range(2 * nm)))

    entry = []
    for m, (cr, ci, cl) in enumerate(cols):
        ljr = pw_ref[0:1, pl.ds(cl, SLAB)]
        lji = sign * pw_ref[1:2, pl.ds(cl, SLAB)]
        c_r = car_ref[cr, 0:1, :]
        c_i = car_ref[ci, 0:1, :]
        rows_r, rows_i = [None] * SUBLANES, [None] * SUBLANES
        order = range(SUBLANES - 1, -1, -1) if reverse else range(SUBLANES)
        for b in order:
            rows_r[b], rows_i[b] = c_r, c_i
            e_r, e_i = ends[2 * m][b:b + 1], ends[2 * m + 1][b:b + 1]
            c_r, c_i = ljr * c_r - lji * c_i + e_r, ljr * c_i + lji * c_r + e_i
        car_ref[cr, 0:1, :] = c_r
        car_ref[ci, 0:1, :] = c_i
        entry.append((jnp.concatenate(rows_r, axis=0), jnp.concatenate(rows_i, axis=0)))

    def pass2(jj, carry):
        j, rows = step_rows(jj, row0)
        decayed, acc = carry[:2 * nm], carry[2 * nm:]
        out_d, out_a = [], []
        for m, (cr, ci, cl) in enumerate(cols):
            lr, li = lam[m]
            dr, di = decayed[2 * m], decayed[2 * m + 1]
            dr, di = lr * dr - li * di, lr * di + li * dr
            nr = s_ref[cr, rows, :] + dr
            ni = s_ref[ci, rows, :] + di
            s_ref[cr, rows, :] = nr
            s_ref[ci, rows, :] = ni
            out_d += [dr, di]
            if prev_ref is not None:
                prow = pl.ds(pl.multiple_of(prev_row0 + (j - 1) * SUBLANES, SUBLANES), SUBLANES)
                qr = prev_ref[cr, prow, :]
                qi = prev_ref[ci, prow, :]
                out_a += [acc[2 * m] + (nr * qr + ni * qi), acc[2 * m + 1] + (ni * qr - nr * qi)]
        return tuple(out_d) + tuple(out_a)

    n_acc = 2 * nm if prev_ref is not None else 0
    init = tuple(e for pair in entry for e in pair) + tuple(jnp.zeros(full, F32) for _ in range(n_acc))
    accs = lax.fori_loop(0, steps // SCAN_UNROLL, unrolled(pass2), init)[2 * nm:]
    if prev_ref is not None:
        for m, (_, _, cl) in enumerate(cols):
            glam_ref[0:1, pl.ds(cl, SLAB)] += jnp.sum(accs[2 * m], axis=0, keepdims=True)
            glam_ref[1:2, pl.ds(cl, SLAB)] += jnp.sum(accs[2 * m + 1], axis=0, keepdims=True)


def _permute_rows_f32(perm_bf16, v):
    hi = v.astype(BF16)
    lo = (v - hi.astype(F32)).astype(BF16)
    return (jnp.dot(perm_bf16, hi, preferred_element_type=F32) + jnp.dot(perm_bf16, lo, preferred_element_type=F32))


def _put_slab(s_ref, rows, q, val):
    per = 2 * SLAB_STATES // SLAB
    for i in range(per):
        s_ref[q * per + i, rows, :] = val[:, i * SLAB:(i + 1) * SLAB]


def _get_slab(s_ref, rows, q):
    per = 2 * SLAB_STATES // SLAB
    return jnp.concatenate([s_ref[q * per + i, rows, :] for i in range(per)], axis=1)


def _step_major_perm(tt):
    r = jnp.arange(tt)
    held = (r % SUBLANES) * (tt // SUBLANES) + r // SUBLANES
    return held[:, None] == r[None, :]


def _ssm_fwd(proj, u_block, bq, cq, lam, pw, d_skip):
    t = proj.shape[0]
    nq, ds, w2 = bq.shape
    assert ds == SLAB and w2 == 2 * SLAB_STATES
    dssm = nq * SLAB
    width = nq * w2
    ntile = width // SLAB
    tt = min(T_SCAN, t)
    steps = tt // SUBLANES
    perm = _step_major_perm(tt)
    pm, pmt = perm.astype(BF16), perm.T.astype(BF16)

    def body(u_ref, pm_ref, pmt_ref, bq_ref, cq_ref, lam_ref, pw_ref, d_ref, y_ref, cin_ref, s_ref, car_ref):
        @pl.when(pl.program_id(0) == 0)
        def _():
            car_ref[...] = jnp.zeros_like(car_ref)

        cin_ref[...] = jnp.broadcast_to(car_ref[:, 0:1, :], cin_ref.shape)
        u = u_ref[...]
        ub = jnp.dot(pm_ref[...], u.astype(BF16), preferred_element_type=F32).astype(BF16)
        everything = slice(None)
        for q in range(nq):
            _put_slab(s_ref, everything, q,
                      jnp.dot(ub[:, q * SLAB:(q + 1) * SLAB], bq_ref[q], preferred_element_type=F32))
        for q in range(nq):
            _scan_slab(s_ref, 0, q, lam_ref, pw_ref, car_ref, steps, reverse=False)
        y_sm = jnp.concatenate(
            [jnp.dot(_get_slab(s_ref, everything, q).astype(BF16), cq_ref[q], preferred_element_type=F32)
             for q in range(nq)], axis=1)
        y_ref[...] = _permute_rows_f32(pmt_ref[...], y_sm) + d_ref[...] * u

    c3 = lambda i: (0, 0, 0)
    c2 = lambda i: (0, 0)
    return _pallas(
        body, name="ssm_fwd", grid=(t // tt,),
        in_specs=[pl.BlockSpec((tt, dssm), lambda i: (i, u_block)), pl.BlockSpec((tt, tt), c2),
                  pl.BlockSpec((tt, tt), c2), pl.BlockSpec(bq.shape, c3),
                  pl.BlockSpec(cq.shape, c3), pl.BlockSpec(lam.shape, c2), pl.BlockSpec(pw.shape, c2),
                  pl.BlockSpec((1, dssm), c2)],
        out_specs=[pl.BlockSpec((tt, dssm), lambda i: (i, 0)),
                   pl.BlockSpec((None, ntile, SUBLANES, SLAB), lambda i: (i, 0, 0, 0))],
        out_shape=[jax.ShapeDtypeStruct((t, dssm), F32), jax.ShapeDtypeStruct((t // tt, ntile, SUBLANES, SLAB), F32)],
        scratch_shapes=[pltpu.VMEM((ntile, tt, SLAB), F32), pltpu.VMEM((ntile, SUBLANES, SLAB), F32)],
        compiler_params=_cp(2 * (8 * tt * dssm + 4 * nq * ds * w2) + 4 * tt * width + 16 * MIB,
                            dimension_semantics=("arbitrary",)),
    )(proj, pm, pmt, bq, cq, lam, pw, d_skip)


def _ssm_bwd(proj, u_block, dy, cin, dproj, bq, cq, lam, pw, d_skip):
    t = proj.shape[0]
    nq, ds, w2 = bq.shape
    dssm = nq * SLAB
    width = nq * w2
    ntile = width // SLAB
    tt = min(T_SCAN, t)
    nt = t // tt
    steps = tt // SUBLANES
    halo = SUBLANES
    perm = _step_major_perm(tt)
    pm, pmt = perm.astype(BF16), perm.T.astype(BF16)

    def body(u_ref, dy_ref, cin_ref, dp_any, pm_ref, pmt_ref, bq_ref, cq_ref, lam_ref, pw_ref, d_ref,
             du_ref, gb_ref, gc_ref, glam_ref, gd_ref, s_ref, gs_ref, car_f, car_b):
        del dp_any

        @pl.when(pl.program_id(0) == 0)
        def _():
            car_b[...] = jnp.zeros_like(car_b)
            gb_ref[...] = jnp.zeros_like(gb_ref)
            gc_ref[...] = jnp.zeros_like(gc_ref)
            glam_ref[...] = jnp.zeros_like(glam_ref)
            gd_ref[...] = jnp.zeros_like(gd_ref)

        u = u_ref[...]
        dyv = dy_ref[...]
        gd_ref[0:1, :] += jnp.sum(dyv * u, axis=0, keepdims=True)
        pmv = pm_ref[...]
        ub = jnp.dot(pmv, u.astype(BF16), preferred_element_type=F32).astype(BF16)
        dyb = jnp.dot(pmv, dyv.astype(BF16), preferred_element_type=F32).astype(BF16)
        car_f[...] = cin_ref[...]
        data = slice(halo, halo + tt)
        everything = slice(None)
        for q in range(nq):
            _put_slab(s_ref, data, q, jnp.dot(ub[:, q * SLAB:(q + 1) * SLAB], bq_ref[q], preferred_element_type=F32))
        for q in range(nq):
            _scan_slab(s_ref, halo, q, lam_ref, pw_ref, car_f, steps, reverse=False)
        last_step = s_ref[:, halo + tt - SUBLANES:halo + tt, :]
        s_ref[:, 0:halo, :] = jnp.concatenate([cin_ref[:, 0:1, :], last_step[:, 0:SUBLANES - 1, :]], axis=1)
        tn = (((0,), (0,)), ((), ()))
        nt_dims = (((1,), (1,)), ((), ()))
        for q in range(nq):
            sl = slice(q * SLAB, (q + 1) * SLAB)
            gc_ref[q] += lax.dot_general(dyb[:, sl], _get_slab(s_ref, data, q).astype(BF16), tn,
                                         preferred_element_type=F32)
            _put_slab(gs_ref, everything, q,
                      lax.dot_general(dyb[:, sl], cq_ref[q], nt_dims, preferred_element_type=F32))
        for q in range(nq):
            _scan_slab(gs_ref, 0, q, lam_ref, pw_ref, car_b, steps, reverse=True,
                       prev_ref=s_ref, prev_row0=halo, glam_ref=glam_ref)
        du_parts = []
        for q in range(nq):
            sl = slice(q * SLAB, (q + 1) * SLAB)
            gsb = _get_slab(gs_ref, everything, q).astype(BF16)
            du_parts.append(lax.dot_general(gsb, bq_ref[q], nt_dims, preferred_element_type=F32))
            gb_ref[q] += lax.dot_general(ub[:, sl], gsb, tn, preferred_element_type=F32)
        du_sm = jnp.concatenate(du_parts, axis=1)
        du_ref[...] = (_permute_rows_f32(pmt_ref[...], du_sm) + dyv * d_ref[...]).astype(BF16)

    c3 = lambda i: (0, 0, 0)
    c2 = lambda i: (0, 0)
    rev = lambda i: (nt - 1 - i, 0)
    dense = jax.ShapeDtypeStruct((nq, SLAB, w2), F32)
    gp = lam.shape[1]
    return _pallas(
        body, name="ssm_bwd", grid=(nt,),
        in_specs=[pl.BlockSpec((tt, dssm), lambda i: (nt - 1 - i, u_block)), pl.BlockSpec((tt, dssm), rev),
                  pl.BlockSpec((None, ntile, SUBLANES, SLAB), lambda i: (nt - 1 - i, 0, 0, 0)),
                  pl.BlockSpec(memory_space=pl.ANY), pl.BlockSpec((tt, tt), c2), pl.BlockSpec((tt, tt), c2),
                  pl.BlockSpec(bq.shape, c3), pl.BlockSpec(cq.shape, c3),
                  pl.BlockSpec(lam.shape, c2), pl.BlockSpec(pw.shape, c2), pl.BlockSpec((1, dssm), c2)],
        out_specs=[pl.BlockSpec((tt, dssm), lambda i: (nt - 1 - i, u_block)), pl.BlockSpec(dense.shape, c3),
                   pl.BlockSpec(dense.shape, c3), pl.BlockSpec((SUBLANES, gp), c2), pl.BlockSpec((SUBLANES, dssm), c2)],
        out_shape=[jax.ShapeDtypeStruct(dproj.shape, dproj.dtype), dense, dense,
                   jax.ShapeDtypeStruct((SUBLANES, gp), F32), jax.ShapeDtypeStruct((SUBLANES, dssm), F32)],
        scratch_shapes=[pltpu.VMEM((ntile, tt + halo, SLAB), F32), pltpu.VMEM((ntile, tt, SLAB), F32),
                        pltpu.VMEM((ntile, SUBLANES, SLAB), F32), pltpu.VMEM((ntile, SUBLANES, SLAB), F32)],
        input_output_aliases={3: 0},
        compiler_params=_cp(2 * (10 * tt * dssm + 4 * nq * ds * w2 + 8 * nq * SLAB * w2)
                            + 8 * tt * width + 12 * MIB, dimension_semantics=("arbitrary",)),
    )(proj, dy, cin, dproj, pm, pmt, bq, cq, lam, pw, d_skip)


def _gate_fwd(proj, y, conv_w, conv_b, w_glu, b_glu, dc):
    t = proj.shape[0]
    dssm = y.shape[1]
    assert dc == dssm
    tm = min(TM_GATE, t)
    halo = SUBLANES

    def body(b_ref, c_ref, v_ref, zc_ref, zs_ref, y_ref, cw_ref, cb_ref, wg_ref, bg_ref, mix_ref, cv_buf):
        @pl.when(pl.program_id(0) == 0)
        def _():
            cv_buf[0:halo, :] = jnp.zeros((halo, dc), F32)

        cv = c_ref[...] * v_ref[...]
        cv_buf[halo:, :] = cv
        conv = (cb_ref[...] + cw_ref[2:3, :] * cv + cw_ref[1:2, :] * cv_buf[halo - 1:halo - 1 + tm, :]
                + cw_ref[0:1, :] * cv_buf[halo - 2:halo - 2 + tm, :])
        sz, _ = _silu(zc_ref[...])
        mix_ref[:, 0:dc] = (b_ref[...] * conv * sz).astype(BF16)
        cv_buf[0:halo, :] = cv_buf[tm:tm + halo, :]
        ge, _ = _gelu(y_ref[...])
        gl = jnp.dot(ge.astype(BF16), wg_ref[...], preferred_element_type=F32) + bg_ref[...]
        szs, _ = _silu(zs_ref[...])
        mix_ref[:, dc:] = (ge * jax.nn.sigmoid(gl) * szs).astype(BF16)

    col = lambda j: pl.BlockSpec((tm, dc), lambda i, j=j: (i, j))
    fixed = lambda i: (0, 0)
    return _pallas(
        body, name="gate_fwd", grid=(t // tm,),
        in_specs=[col(0), col(1), col(2), col(3), col(5), pl.BlockSpec((tm, dssm), lambda i: (i, 0)),
                  pl.BlockSpec(conv_w.shape, fixed), pl.BlockSpec((1, dc), fixed),
                  pl.BlockSpec(w_glu.shape, fixed), pl.BlockSpec((1, dssm), fixed)],
        out_specs=pl.BlockSpec((tm, dc + dssm), lambda i: (i, 0)),
        out_shape=jax.ShapeDtypeStruct((t, dc + dssm), BF16),
        scratch_shapes=[pltpu.VMEM((tm + halo, dc), F32)],
        compiler_params=_cp(2 * (6 * 4 * tm * dc + 2 * tm * (dc + dssm) + 2 * dssm * dssm) + 24 * tm * dc + 8 * MIB,
                            dimension_semantics=("arbitrary",)),
    )(proj, proj, proj, proj, proj, y, conv_w, conv_b, w_glu, b_glu)


def _gate_bwd(proj, y, dmix, conv_w, conv_b, w_glu, b_glu, dc):
    t = proj.shape[0]
    dssm = y.shape[1]
    tm = min(TM_GATE, t)
    nt = t // tm
    halo = SUBLANES
    blocks_per_tile = tm // halo

    def body(b_ref, c_ref, v_ref, zc_ref, zs_ref, cp_ref, vp_ref, y_ref, dm_ref, cw_ref, cb_ref, wg_ref, bg_ref,
             dp_ref, dy_ref, gs_ref, gwg_ref, cv_buf, dc_buf):
        i = pl.program_id(0)

        @pl.when(i == 0)
        def _():
            dc_buf[tm:, :] = jnp.zeros((halo, dc), F32)
            gs_ref[...] = jnp.zeros_like(gs_ref)
            gwg_ref[...] = jnp.zeros_like(gwg_ref)

        first_tile = (i == nt - 1)
        bv, cg, vv, zc = b_ref[...], c_ref[...], v_ref[...], zc_ref[...]
        cv = cg * vv
        cv_buf[0:halo, :] = jnp.where(first_tile, 0.0, cp_ref[...] * vp_ref[...])
        cv_buf[halo:, :] = cv
        w0, w1, w2 = cw_ref[0:1, :], cw_ref[1:2, :], cw_ref[2:3, :]
        conv = (cb_ref[...] + w2 * cv + w1 * cv_buf[halo - 1:halo - 1 + tm, :]
                + w0 * cv_buf[halo - 2:halo - 2 + tm, :])
        sz, sgc = _silu(zc)
        dyc = dm_ref[:, 0:dc]
        dp_ref[:, 0:dc] = (dyc * conv * sz).astype(BF16)
        dp_ref[:, 3 * dc:4 * dc] = (dyc * bv * conv * _dsilu(zc, sgc)).astype(BF16)
        dconv = dyc * bv * sz
        dc_buf[0:tm, :] = dconv
        d1 = dc_buf[1:1 + tm, :]
        d2 = dc_buf[2:2 + tm, :]
        dcv = w2 * dconv + w1 * d1 + w0 * d2
        dp_ref[:, dc:2 * dc] = (dcv * vv).astype(BF16)
        dp_ref[:, 2 * dc:3 * dc] = (dcv * cg).astype(BF16)
        gs_ref[0:1, :] += jnp.sum(cv * d2, axis=0, keepdims=True)
        gs_ref[1:2, :] += jnp.sum(cv * d1, axis=0, keepdims=True)
        gs_ref[2:3, :] += jnp.sum(cv * dconv, axis=0, keepdims=True)
        gs_ref[3:4, :] += jnp.sum(dconv, axis=0, keepdims=True)
        dc_buf[tm:, :] = dc_buf[0:halo, :]

        yv, zs = y_ref[...], zs_ref[...]
        ge, th = _gelu(yv)
        geb = ge.astype(BF16)
        wg = wg_ref[...]
        gl = jnp.dot(geb, wg, preferred_element_type=F32) + bg_ref[...]
        sg = jax.nn.sigmoid(gl)
        szs, sgs = _silu(zs)
        dys = dm_ref[:, dc:]
        ys = ge * sg
        dp_ref[:, 4 * dc:5 * dc] = jnp.zeros((tm, dc), BF16)
        dp_ref[:, 5 * dc:] = (dys * ys * _dsilu(zs, sgs)).astype(BF16)
        d_ys = dys * szs
        dgl = d_ys * ge * sg * (1.0 - sg)
        dglb = dgl.astype(BF16)
        gs_ref[4:5, :] += jnp.sum(dgl, axis=0, keepdims=True)
        gwg_ref[...] += lax.dot_general(geb, dglb, (((0,), (0,)), ((), ())), preferred_element_type=F32)
        dge = d_ys * sg + lax.dot_general(dglb, wg, (((1,), (1,)), ((), ())), preferred_element_type=F32)
        dy_ref[...] = dge * _dgelu(yv, th)

    col = lambda j: pl.BlockSpec((tm, dc), lambda i, j=j: (nt - 1 - i, j))
    prev = lambda j: pl.BlockSpec((halo, dc), lambda i, j=j: (jnp.maximum((nt - 1 - i) * blocks_per_tile - 1, 0), j))
    rev = lambda i: (nt - 1 - i, 0)
    fixed = lambda i: (0, 0)
    return _pallas(
        body, name="gate_bwd", grid=(nt,),
        in_specs=[col(0), col(1), col(2), col(3), col(5), prev(1), prev(2), pl.BlockSpec((tm, dssm), rev),
                  pl.BlockSpec((tm, dc + dssm), rev), pl.BlockSpec(conv_w.shape, fixed), pl.BlockSpec((1, dc), fixed),
                  pl.BlockSpec(w_glu.shape, fixed), pl.BlockSpec((1, dssm), fixed)],
        out_specs=[pl.BlockSpec((tm, 6 * dc), rev), pl.BlockSpec((tm, dssm), rev),
                   pl.BlockSpec((2 * SUBLANES, dc), fixed), pl.BlockSpec((dssm, dssm), fixed)],
        out_shape=[jax.ShapeDtypeStruct((t, 6 * dc), BF16), jax.ShapeDtypeStruct((t, dssm), F32),
                   jax.ShapeDtypeStruct((2 * SUBLANES, dc), F32), jax.ShapeDtypeStruct((dssm, dssm), F32)],
        scratch_shapes=[pltpu.VMEM((tm + halo, dc), F32), pltpu.VMEM((tm + halo, dc), F32)],
        compiler_params=_cp(2 * (6 * 4 * tm * dc + 8 * tm * dc + 12 * tm * dc + 4 * tm * dc + 6 * dssm * dssm)
                            + 40 * tm * dc + 8 * MIB, dimension_semantics=("arbitrary",)),
    )(proj, proj, proj, proj, proj, proj, proj, y, dmix, conv_w, conv_b, w_glu, b_glu)


def _allgather_halves(parts, name):
    n = len(parts)
    per = 2 * (N_CHIPS - 1)

    def body(*refs):
        ins, outs = refs[:n], refs[n:2 * n]
        send, recv, local = refs[2 * n:]
        x, y, c = _my_place()
        k = 2 * x + y
        sib = (x, y, 1 - c)
        chips = _other_chips(x, y)

        def rc(t, s, src, dst, to):
            return pltpu.make_async_remote_copy(src_ref=src, dst_ref=dst, send_sem=send.at[t * per + s],
                                                recv_sem=recv.at[t * per + s], device_id=to, device_id_type=MESH)

        mine = [pltpu.make_async_copy(ins[t], outs[t].at[k], local.at[t]) for t in range(n)]
        for cp in mine:
            cp.start()
        sent = []
        for t in range(n):
            for j, (cx, cy) in enumerate(chips):
                cp = rc(t, j, ins[t].at[c], outs[t].at[k, c], (cx, cy, c))
                cp.start()
                sent.append(cp)
        for t in range(n):
            for j, (cx, cy) in enumerate(chips):
                landed = outs[t].at[2 * cx + cy, c]
                rc(t, j, landed, landed, (cx, cy, c)).wait_recv()
                cp = rc(t, N_CHIPS - 1 + j, landed, landed, sib)
                cp.start()
                sent.append(cp)
        for t in range(n):
            for j, (cx, cy) in enumerate(chips):
                other = outs[t].at[2 * cx + cy, 1 - c]
                rc(t, N_CHIPS - 1 + j, other, other, sib).wait_recv()
        for cp in sent:
            cp.wait_send()
        for cp in mine:
            cp.wait()

    any_spec = pl.BlockSpec(memory_space=pl.ANY)
    return _pallas(
        body, name=name, in_specs=[any_spec] * n, out_specs=[any_spec] * n,
        out_shape=[jax.ShapeDtypeStruct((N_CHIPS,) + p.shape, p.dtype) for p in parts],
        scratch_shapes=[pltpu.SemaphoreType.DMA((n * per,)), pltpu.SemaphoreType.DMA((n * per,)),
                        pltpu.SemaphoreType.DMA((n,))],
        compiler_params=_cp(16 * MIB),
    )(*parts)


def _allgather_flat(v, name):
    r, c = v.shape
    rels = [(dx, dy, dc) for dx in (0, 1) for dy in (0, 1) for dc in (0, 1)][1:]

    def body(v_ref, out_ref, send, recv):
        x, y, cc = _my_place()
        me = 4 * x + 2 * y + cc

        def peer(rel):
            dx, dy, dc = rel
            return (1 - x if dx else x, 1 - y if dy else y, 1 - cc if dc else cc)

        def rc(s, slot, to):
            return pltpu.make_async_remote_copy(src_ref=v_ref, dst_ref=out_ref.at[slot], send_sem=send.at[s],
                                                recv_sem=recv.at[s], device_id=to, device_id_type=MESH)

        sent = []
        for s, rel in enumerate(rels):
            cp = rc(s, me, peer(rel))
            cp.start()
            sent.append(cp)
        out_ref[me] = v_ref[...]
        for s, rel in enumerate(rels):
            px, py, pc = peer(rel)
            rc(s, 4 * px + 2 * py + pc, (px, py, pc)).wait_recv()
        for cp in sent:
            cp.wait_send()

    return _pallas(
        body, name=name, out_shape=jax.ShapeDtypeStruct((N_DEV, r, c), F32),
        in_specs=[pl.BlockSpec(memory_space=pltpu.VMEM)], out_specs=pl.BlockSpec(memory_space=pltpu.VMEM),
        scratch_shapes=[pltpu.SemaphoreType.DMA((N_DEV - 1,)), pltpu.SemaphoreType.DMA((N_DEV - 1,))],
        compiler_params=_cp((N_DEV + 2) * 4 * r * c + 8 * MIB),
    )(v)


def _rs_pair_exchange(grads, name):
    n = len(grads)

    def body(*refs):
        ins, outs = refs[:n], refs[n:2 * n]
        send, recv = refs[2 * n:]
        x, y, c = _my_place()
        cps = []
        for t in range(n):
            cp = pltpu.make_async_remote_copy(src_ref=ins[t].at[:, 1 - c], dst_ref=outs[t], send_sem=send.at[t],
                                              recv_sem=recv.at[t], device_id=(x, y, 1 - c), device_id_type=MESH)
            cp.start()
            cps.append(cp)
        for cp in cps:
            cp.wait()

    any_spec = pl.BlockSpec(memory_space=pl.ANY)
    return _pallas(
        body, name=name, in_specs=[any_spec] * n, out_specs=[any_spec] * n,
        out_shape=[jax.ShapeDtypeStruct((g.shape[0],) + g.shape[2:], F32) for g in grads],
        scratch_shapes=[pltpu.SemaphoreType.DMA((n,)), pltpu.SemaphoreType.DMA((n,))],
        compiler_params=_cp(16 * MIB),
    )(*grads)


def _rs_pair_add(place, grad, got, name):
    nk, _, r2, c = grad.shape
    tr = min(TR_ELT, r2)

    def body(place_ref, g_ref, r_ref, o16_ref, o32_ref):
        del place_ref
        s = g_ref[...] + r_ref[...]
        o32_ref[...] = s
        o16_ref[...] = s.astype(BF16)

    blk = pl.BlockSpec((None, tr, c), lambda k, i, p: (k, i, 0))
    return _pallas(
        body, name=name,
        grid_spec=pltpu.PrefetchScalarGridSpec(
            num_scalar_prefetch=1, grid=(nk, r2 // tr),
            in_specs=[pl.BlockSpec((None, None, tr, c), lambda k, i, p: (k, p[0], i, 0)), blk],
            out_specs=[blk, blk]),
        out_shape=[jax.ShapeDtypeStruct((nk, r2, c), BF16), jax.ShapeDtypeStruct((nk, r2, c), F32)],
        compiler_params=_cp(2 * 14 * tr * c + 8 * MIB),
    )(place, grad, got)


_HBM_SPEC = pl.BlockSpec(memory_space=pltpu.HBM)
_SEM_SPEC = pl.BlockSpec(memory_space=pltpu.SEMAPHORE)
_DATAFLOW = pltpu.SideEffectType.DATAFLOW_SIDE_EFFECTING


def _split_copy_start(srcs, land_shapes, plan, n_sems, name):
    ns, nl = len(srcs), len(land_shapes)

    def body(*refs):
        src_refs, land_refs = refs[:ns], refs[ns:ns + nl]
        send, recv = refs[ns + nl], refs[ns + nl + 1]
        token = refs[-1]
        sends, _ = plan(src_refs, land_refs)
        for src, dst, to, si, ri in sends:
            pltpu.make_async_remote_copy(src_ref=src, dst_ref=dst, send_sem=send.at[si], recv_sem=recv.at[ri],
                                         device_id=to, device_id_type=MESH).start()
        token[...] = jnp.zeros_like(token)

    lands = [lax.empty(shp, dt) for shp, dt in land_shapes]
    through = [pltpu.HBM(a.shape, a.dtype) for a in srcs] + [pltpu.HBM(shp, dt) for shp, dt in land_shapes]
    out = pl.pallas_call(
        body, name=name,
        out_shape=(pltpu.SemaphoreType.DMA((n_sems,)), pltpu.SemaphoreType.DMA((n_sems,)), *through,
                   jax.ShapeDtypeStruct((SUBLANES, SLAB), F32)),
        in_specs=[_HBM_SPEC] * (ns + nl),
        out_specs=(_SEM_SPEC, _SEM_SPEC, *([_HBM_SPEC] * (ns + nl)), pl.BlockSpec(memory_space=pltpu.VMEM)),
        input_output_aliases={i: 2 + i for i in range(ns + nl)},
        compiler_params=pltpu.CompilerParams(has_side_effects=_DATAFLOW),
    )(*[pltpu.with_memory_space_constraint(a, pltpu.HBM) for a in (*srcs, *lands)])
    return out[0], out[1], list(out[2:2 + ns]), list(out[2 + ns:2 + ns + nl]), out[-1]


def _split_copy_wait(send, recv, srcs, lands, plan, after, name):
    ns, nl, na = len(srcs), len(lands), len(after)

    def body(*refs):
        src_refs, land_refs = refs[:ns], refs[ns:ns + nl]
        send_ref, recv_ref = refs[ns + nl], refs[ns + nl + 1]
        sends, arrivals = plan(src_refs, land_refs)
        for src, dst, to, si, ri in sends:
            pltpu.make_async_remote_copy(src_ref=src, dst_ref=dst, send_sem=send_ref.at[si], recv_sem=recv_ref.at[ri],
                                         device_id=to, device_id_type=MESH).wait_send()
        for (src, _, to, si, _), (view, ri) in zip(sends, arrivals):
            pltpu.make_async_remote_copy(src_ref=view, dst_ref=view, send_sem=send_ref.at[si], recv_sem=recv_ref.at[ri],
                                         device_id=to, device_id_type=MESH).wait_recv()

    out = pl.pallas_call(
        body, name=name,
        out_shape=[pltpu.HBM(a.shape, a.dtype) for a in (*srcs, *lands)],
        in_specs=[_HBM_SPEC] * (ns + nl) + [_SEM_SPEC, _SEM_SPEC] + [pl.BlockSpec(memory_space=pl.ANY)] * na,
        out_specs=[_HBM_SPEC] * (ns + nl),
        input_output_aliases={i: i for i in range(ns + nl)},
        compiler_params=pltpu.CompilerParams(has_side_effects=_DATAFLOW),
    )(*srcs, *lands, send, recv, *after)
    return list(out[:ns]), list(out[ns:])


def _chip_exchange_plan(n):
    per = N_CHIPS - 1

    def plan(srcs, lands):
        x, y, c = _my_place()
        sends, arrivals = [], []
        for t in range(n):
            for j, (cx, cy) in enumerate(_other_chips(x, y)):
                sends.append((srcs[t].at[2 * cx + cy], lands[t].at[j], (cx, cy, c), t * per + j, t * per + j))
                arrivals.append((lands[t].at[j], t * per + j))
        return sends, arrivals

    return plan


def _gather_half_plan(n):
    per = N_CHIPS - 1

    def plan(srcs, lands):
        x, y, c = _my_place()
        k = 2 * x + y
        sends, arrivals = [], []
        for t in range(n):
            for j, (cx, cy) in enumerate(_other_chips(x, y)):
                sends.append((srcs[t].at[c], lands[t].at[k, c], (cx, cy, c), t * per + j, t * per + j))
                arrivals.append((lands[t].at[2 * cx + cy, c], t * per + j))
        return sends, arrivals

    return plan


def _gather_complete(own, landed, name):
    def body(own_ref, in_ref, out_ref, send, recv, local):
        x, y, c = _my_place()
        sib = (x, y, 1 - c)
        mine = pltpu.make_async_copy(own_ref, out_ref.at[2 * x + y], local)
        mine.start()
        sent = []
        for j, (cx, cy) in enumerate(_other_chips(x, y)):
            half = (2 * cx + cy, c)
            cp = pltpu.make_async_remote_copy(src_ref=in_ref.at[half], dst_ref=out_ref.at[half], send_sem=send.at[j],
                                              recv_sem=recv.at[j], device_id=sib, device_id_type=MESH)
            cp.start()
            sent.append(cp)
        for j, (cx, cy) in enumerate(_other_chips(x, y)):
            other = out_ref.at[2 * cx + cy, 1 - c]
            pltpu.make_async_remote_copy(src_ref=other, dst_ref=other, send_sem=send.at[j], recv_sem=recv.at[j],
                                         device_id=sib, device_id_type=MESH).wait_recv()
        for cp in sent:
            cp.wait_send()
        mine.wait()

    any_spec = pl.BlockSpec(memory_space=pl.ANY)
    per = N_CHIPS - 1
    return _pallas(
        body, name=name, in_specs=[any_spec, any_spec], out_specs=any_spec,
        out_shape=jax.ShapeDtypeStruct(landed.shape, landed.dtype), input_output_aliases={1: 0},
        scratch_shapes=[pltpu.SemaphoreType.DMA((per,)), pltpu.SemaphoreType.DMA((per,)), pltpu.SemaphoreType.DMA],
        compiler_params=_cp(16 * MIB),
    )(own, landed)


def _gather_direct_plan(n):
    per = 2 * (N_CHIPS - 1)

    def plan(srcs, lands):
        x, y, c = _my_place()
        k = 2 * x + y
        sends, arrivals = [], []
        for t in range(n):
            for j, (cx, cy) in enumerate(_other_chips(x, y)):
                for core in (0, 1):
                    sends.append((srcs[t].at[c], lands[t].at[k, c], (cx, cy, core),
                                  t * per + 2 * j + core, t * per + 2 * j + c))
                    arrivals.append((lands[t].at[2 * cx + cy, core], t * per + 2 * j + core))
        return sends, arrivals

    return plan


def _rs_chip_add(place, sums32, got, name):
    _, r2, c = sums32.shape
    tr = min(TR_ELT, r2)

    def body(place_ref, s_ref, q_ref, o_ref):
        del place_ref
        o_ref[...] = ((s_ref[...] + q_ref[0].astype(F32)) + q_ref[1].astype(F32)) + q_ref[2].astype(F32)

    return _pallas(
        body, name=name,
        grid_spec=pltpu.PrefetchScalarGridSpec(
            num_scalar_prefetch=1, grid=(r2 // tr,),
            in_specs=[pl.BlockSpec((None, tr, c), lambda i, p: (p[1], i, 0)),
                      pl.BlockSpec((N_CHIPS - 1, tr, c), lambda i, p: (0, i, 0))],
            out_specs=pl.BlockSpec((tr, c), lambda i, p: (i, 0))),
        out_shape=jax.ShapeDtypeStruct((r2, c), F32),
        compiler_params=_cp(2 * 14 * tr * c + 8 * MIB),
    )(place, sums32, got)


def _rs_pair_share(halves, name):
    n = len(halves)

    def body(*refs):
        ins, outs = refs[:n], refs[n:2 * n]
        send, recv, local = refs[2 * n:]
        x, y, c = _my_place()
        cps, mine = [], []
        for t in range(n):
            lc = pltpu.make_async_copy(ins[t], outs[t].at[c], local.at[t])
            lc.start()
            mine.append(lc)
            cp = pltpu.make_async_remote_copy(src_ref=ins[t], dst_ref=outs[t].at[c], send_sem=send.at[t],
                                              recv_sem=recv.at[t], device_id=(x, y, 1 - c), device_id_type=MESH)
            cp.start()
            cps.append(cp)
        for t in range(n):
            other = outs[t].at[1 - c]
            pltpu.make_async_remote_copy(src_ref=other, dst_ref=other, send_sem=send.at[t], recv_sem=recv.at[t],
                                         device_id=(x, y, 1 - c), device_id_type=MESH).wait_recv()
        for cp in cps:
            cp.wait_send()
        for lc in mine:
            lc.wait()

    any_spec = pl.BlockSpec(memory_space=pl.ANY)
    return _pallas(
        body, name=name, in_specs=[any_spec] * n, out_specs=[any_spec] * n,
        out_shape=[jax.ShapeDtypeStruct((2,) + h.shape, F32) for h in halves],
        scratch_shapes=[pltpu.SemaphoreType.DMA((n,)), pltpu.SemaphoreType.DMA((n,)), pltpu.SemaphoreType.DMA((n,))],
        compiler_params=_cp(16 * MIB),
    )(*halves)


_PACK_TILE = SUBLANES * SLAB


def _pack(arrays):
    rows = []
    for a in arrays:
        flat = a.reshape(-1).astype(F32)
        padded = -(-flat.shape[0] // _PACK_TILE) * _PACK_TILE
        rows.append(jnp.pad(flat, (0, padded - flat.shape[0])).reshape(-1, SLAB))
    n_rows = sum(r.shape[0] for r in rows)
    if n_rows % (2 * SUBLANES):
        rows.append(jnp.zeros((SUBLANES, SLAB), F32))
    return jnp.concatenate(rows, axis=0)


def _unpack(packed, shapes):
    out, row = [], 0
    for shp in shapes:
        size = math.prod(shp)
        nrow = -(-size // _PACK_TILE) * SUBLANES
        out.append(packed[row:row + nrow].reshape(-1)[:size].reshape(shp))
        row += nrow
    return out


def kernel(x, norm_pre_g, w_in, conv_w, conv_b, ssm_a_re, ssm_a_im, ssm_log_dt, ssm_b_re, ssm_b_im, ssm_c_re, ssm_c_im, ssm_d, w_glu, b_glu, w_out, norm_post_g, loss_target, m_norm_pre_g, m_w_in, m_conv_w, m_conv_b, m_ssm_a_re, m_ssm_a_im, m_ssm_log_dt, m_ssm_b_re, m_ssm_b_im, m_ssm_c_re, m_ssm_c_im, m_ssm_d, m_w_glu, m_b_glu, m_w_out, m_norm_post_g, v_norm_pre_g, v_w_in, v_conv_w, v_conv_b, v_ssm_a_re, v_ssm_a_im, v_ssm_log_dt, v_ssm_b_re, v_ssm_b_im, v_ssm_c_re, v_ssm_c_im, v_ssm_d, v_w_glu, v_b_glu, v_w_out, v_norm_post_g):
    xs, tgt = x[0], loss_target[0]
    t, d = xs.shape
    dc = conv_b.shape[0]
    dssm = ssm_d.shape[0]
    g, p = ssm_a_re.shape
    h = SSM_H
    nq = dssm // SLAB
    n_shard = w_in.shape[1]
    steps = min(T_SCAN, t) // SUBLANES
    mx, my, mc = _my_place()
    chip = 2 * mx + my
    place = jnp.stack([mc, chip]).astype(jnp.int32)

    cw_cols = conv_w.shape[1]
    cw_pad = -(-cw_cols // SLAB) * SLAB
    cw_blk = jnp.zeros((SUBLANES, cw_pad), F32).at[:conv_w.shape[0], :cw_cols].set(conv_w)
    cw_all = _allgather_flat(cw_blk, "allgather_conv_w")
    conv_w_full = jnp.concatenate([cw_all[2 * k, :, :cw_cols] for k in range(N_CHIPS)], axis=1)

    halves = lambda a: a.reshape(2, a.shape[0] // 2, a.shape[1])
    win_half = halves(_cast_bf16(w_in, "cast_w_in", cw_all[0, :, :SLAB]))
    win_plan = _gather_half_plan(1)
    win_send, win_recv, win_srcs, win_lands, win_token = _split_copy_start(
        [win_half], [((N_CHIPS,) + win_half.shape, BF16)], win_plan, N_CHIPS - 1, "gather_w_in_start")
    behind_w_in = win_token[0:1, 0:1]

    expand = jnp.repeat(jnp.eye(p, dtype=F32), h, axis=1)
    b_re2, b_im2 = ssm_b_re.reshape(g, p * h), ssm_b_im.reshape(g, p * h)
    log_dt2 = ssm_log_dt.reshape(g, 1) + behind_w_in
    lbr, lbi, bbr2, bbi2, pw3 = _zoh_fwd(ssm_a_re, ssm_a_im, log_dt2, b_re2, b_im2, expand, steps)
    lam = jnp.stack([lbr.reshape(g * p), lbi.reshape(g * p)])
    pw = pw3.reshape(2, g * p)
    to_slab_b = lambda b2: _blockdiag(b2.reshape(nq, GROUPS_PER_SLAB, p, h).transpose(0, 1, 3, 2))
    bq = jnp.concatenate([to_slab_b(bbr2), to_slab_b(bbi2)], axis=2).astype(BF16)
    to_slab_c = lambda c3: _blockdiag(c3.reshape(nq, GROUPS_PER_SLAB, h, p).transpose(0, 1, 3, 2))
    cq = jnp.concatenate([to_slab_c(ssm_c_re), to_slab_c(-ssm_c_im)], axis=1).astype(BF16)

    g_pre2, g_post2 = norm_pre_g.reshape(1, d), norm_post_g.reshape(1, d)
    conv_b2, b_glu2, d_skip2 = conv_b.reshape(1, dc), b_glu.reshape(1, dssm), ssm_d.reshape(1, dssm)
    u_block = 4 * dc // dssm

    hb = _prenorm(xs, g_pre2 + behind_w_in)
    proj_own = _inproj_own(place, hb, win_srcs[0].reshape(d, n_shard))
    win_own, win_landed = _split_copy_wait(win_send, win_recv, win_srcs, win_lands, win_plan,
                                           [proj_own, bq, cq], "gather_w_in_wait")
    win_g = _gather_complete(win_own[0], win_landed[0], "gather_w_in_complete")
    win_b = win_g.reshape(N_CHIPS, d, n_shard)
    gathered = win_g[0, 0, :SUBLANES, :SLAB].astype(F32)
    side = [halves(_cast_bf16(w_out, "cast_w_out", gathered)), halves(_cast_bf16(w_glu, "cast_w_glu", gathered))]
    side_plan = _gather_direct_plan(len(side))
    side_sems = 2 * (N_CHIPS - 1) * len(side)
    ag_send, ag_recv, ag_srcs, ag_lands, ag_token = _split_copy_start(
        side, [((N_CHIPS,) + a.shape, BF16) for a in side], side_plan, side_sems, "gather_side_weights_start")
    proj = _inproj_rest(place, hb, win_b, proj_own, ag_token)
    y, cin = _ssm_fwd(proj, u_block, bq, cq, lam, pw, d_skip2)
    side_own, side_all = _split_copy_wait(ag_send, ag_recv, ag_srcs, ag_lands, side_plan, [cin],
                                          "gather_side_weights_wait")
    wout_g, wglu_g = [lax.dynamic_update_index_in_dim(all_, own, chip, 0) for own, all_ in zip(side_own, side_all)]
    wout_b = wout_g.reshape(dc + dssm, d)
    wglu_b = wglu_g.reshape(dssm, dssm)
    mix = _gate_fwd(proj, y, conv_w_full, conv_b2, wglu_b, b_glu2, dc)
    loss_blk, dout, dob, dmix, gg_post = _outproj(mix, wout_b, xs, tgt, g_post2)

    def reduce_start(grads, tags, group):
        got = _rs_pair_exchange(grads, "rs_pair_exchange_" + group)
        sums = [_rs_pair_add(place, gt, rt, "rs_pair_add_" + tg) for gt, rt, tg in zip(grads, got, tags)]
        plan = _chip_exchange_plan(len(grads))
        started = _split_copy_start([s16 for s16, _ in sums],
                                    [((N_CHIPS - 1,) + s16.shape[1:], BF16) for s16, _ in sums], plan,
                                    (N_CHIPS - 1) * len(grads), "rs_chip_exchange_" + group + "_start")
        return plan, started, [s32 for _, s32 in sums]

    def reduce_finish(plan, started, sums32, tags, group, after):
        send, recv, srcs, lands, _ = started
        _, landed = _split_copy_wait(send, recv, srcs, lands, plan, after, "rs_chip_exchange_" + group + "_wait")
        mine = [_rs_chip_add(place, s32, qt, "rs_chip_add_" + tg) for s32, qt, tg in zip(sums32, landed, tags)]
        full = _rs_pair_share(mine, "rs_pair_share_" + group)
        return [f.reshape(2 * f.shape[1], f.shape[2]) for f in full]

    gw_out = _matmul_tn(mix, dob, 1, "grad_w_out")
    dproj, dy, gsmall, gw_glu = _gate_bwd(proj, y, dmix, conv_w_full, conv_b2, wglu_b, b_glu2, dc)
    rs_a = reduce_start([gw_out.reshape(N_CHIPS, 2, (dc + dssm) // (2 * N_CHIPS), d),
                         gw_glu.reshape(N_CHIPS, 2, dssm // (2 * N_CHIPS), dssm)], ["w_out", "w_glu"], "a")
    dproj, gb_dense, gc_dense, glam, gd = _ssm_bwd(proj, u_block, dy, cin, dproj, bq, cq, lam, pw,
                                                   d_skip2 + rs_a[1][4][0:1, 0:1])
    gw_in = _matmul_tn(hb, dproj, N_CHIPS, "grad_w_in")
    rs_b = reduce_start([gw_in.reshape(N_CHIPS, 2, d // 2, n_shard)], ["w_in"], "b")
    gx, gg_pre = _dh_prenorm_bwd(dproj, win_b, xs, dout, g_pre2 + rs_b[1][4][0:1, 0:1])

    gb4 = gb_dense.reshape(nq, SLAB, 2, SLAB_STATES)
    g_bbr2 = _blockdiag_take(gb4[:, :, 0, :], h, p).transpose(0, 1, 3, 2).reshape(g, p * h)
    g_bbi2 = _blockdiag_take(gb4[:, :, 1, :], h, p).transpose(0, 1, 3, 2).reshape(g, p * h)
    gc4 = gc_dense.reshape(nq, SLAB, 2, SLAB_STATES)
    g_c_re = _blockdiag_take(gc4[:, :, 0, :], h, p).reshape(g, h, p)
    g_c_im = -_blockdiag_take(gc4[:, :, 1, :], h, p).reshape(g, h, p)
    g_a_re, g_a_im, g_ld, g_b_re2, g_b_im2 = _zoh_bwd(
        ssm_a_re, ssm_a_im, log_dt2, b_re2, b_im2, expand,
        glam[0].reshape(g, p), glam[1].reshape(g, p), g_bbr2, g_bbi2)

    small_names = ["norm_pre_g", "conv_w", "conv_b", "ssm_a_re", "ssm_a_im", "ssm_log_dt", "ssm_b_re", "ssm_b_im",
                   "ssm_c_re", "ssm_c_im", "ssm_d", "b_glu", "norm_post_g", "loss"]
    small_g = {
        "norm_pre_g": gg_pre[0], "conv_w": gsmall[0:3], "conv_b": gsmall[3], "ssm_a_re": g_a_re, "ssm_a_im": g_a_im,
        "ssm_log_dt": g_ld.reshape(g), "ssm_b_re": g_b_re2.reshape(g, p, h), "ssm_b_im": g_b_im2.reshape(g, p, h),
        "ssm_c_re": g_c_re, "ssm_c_im": g_c_im, "ssm_d": gd[0], "b_glu": gsmall[4], "norm_post_g": gg_post[0],
        "loss": loss_blk[0, 0:1],
    }
    zeros_cw = jnp.zeros((conv_w.shape[0], dc), F32)
    one0 = jnp.zeros((1,), F32)
    small_w = dict(norm_pre_g=norm_pre_g, conv_w=zeros_cw, conv_b=conv_b, ssm_a_re=ssm_a_re, ssm_a_im=ssm_a_im,
                   ssm_log_dt=ssm_log_dt, ssm_b_re=ssm_b_re, ssm_b_im=ssm_b_im, ssm_c_re=ssm_c_re, ssm_c_im=ssm_c_im,
                   ssm_d=ssm_d, b_glu=b_glu, norm_post_g=norm_post_g, loss=one0)
    small_m = dict(norm_pre_g=m_norm_pre_g, conv_w=zeros_cw, conv_b=m_conv_b, ssm_a_re=m_ssm_a_re, ssm_a_im=m_ssm_a_im,
                   ssm_log_dt=m_ssm_log_dt, ssm_b_re=m_ssm_b_re, ssm_b_im=m_ssm_b_im, ssm_c_re=m_ssm_c_re,
                   ssm_c_im=m_ssm_c_im, ssm_d=m_ssm_d, b_glu=m_b_glu, norm_post_g=m_norm_post_g, loss=one0)
    small_v = dict(norm_pre_g=v_norm_pre_g, conv_w=zeros_cw, conv_b=v_conv_b, ssm_a_re=v_ssm_a_re, ssm_a_im=v_ssm_a_im,
                   ssm_log_dt=v_ssm_log_dt, ssm_b_re=v_ssm_b_re, ssm_b_im=v_ssm_b_im, ssm_c_re=v_ssm_c_re,
                   ssm_c_im=v_ssm_c_im, ssm_d=v_ssm_d, b_glu=v_b_glu, norm_post_g=v_norm_post_g, loss=one0)
    shapes = [small_w[nm].shape for nm in small_names]
    g_pack = _allreduce_small(_pack([small_g[nm] for nm in small_names]))
    packs = _adamw_small(g_pack, _pack([small_w[nm] for nm in small_names]),
                         _pack([small_m[nm] for nm in small_names]), _pack([small_v[nm] for nm in small_names]))
    sg, sd, sm, sv = [dict(zip(small_names, _unpack(pk, shapes))) for pk in (g_pack, *packs)]
    loss = sg["loss"][0]

    g_cw = lax.dynamic_slice_in_dim(sg["conv_w"], chip * cw_cols, cw_cols, axis=1)
    pad_cw = lambda a: jnp.zeros((SUBLANES, cw_pad), F32).at[:a.shape[0], :cw_cols].set(a)
    cut_cw = lambda a: a[:conv_w.shape[0], :cw_cols]
    d_cw, m_cw, v_cw = [cut_cw(a) for a in _adamw(pad_cw(conv_w), pad_cw(g_cw), pad_cw(m_conv_w), pad_cw(v_conv_w),
                                                  "adamw_conv_w")]

    g_wout, g_wglu = reduce_finish(*rs_a, ["w_out", "w_glu"], "a", [d_cw, packs[0]])
    (g_win,) = reduce_finish(*rs_b, ["w_in"], "b", [g_wout])
    d_win, m_win, v_win = _adamw(w_in, g_win, m_w_in, v_w_in, "adamw_w_in")
    d_wout, m_wout, v_wout = _adamw(w_out, g_wout, m_w_out, v_w_out, "adamw_w_out")
    d_wglu, m_wglu, v_wglu = _adamw(w_glu, g_wglu, m_w_glu, v_w_glu, "adamw_w_glu")

    order = ["norm_pre_g", "w_in", "conv_w", "conv_b", "ssm_a_re", "ssm_a_im", "ssm_log_dt", "ssm_b_re", "ssm_b_im",
             "ssm_c_re", "ssm_c_im", "ssm_d", "w_glu", "b_glu", "w_out", "norm_post_g"]
    grads, deltas, new_m, new_v = dict(sg), dict(sd), dict(sm), dict(sv)
    grads.update(w_in=g_win, w_out=g_wout, w_glu=g_wglu, conv_w=g_cw)
    deltas.update(w_in=d_win, w_out=d_wout, w_glu=d_wglu, conv_w=d_cw)
    new_m.update(w_in=m_win, w_out=m_wout, w_glu=m_wglu, conv_w=m_cw)
    new_v.update(w_in=v_win, w_out=v_wout, w_glu=v_wglu, conv_w=v_cw)
    return (loss, gx[None], *[grads[nm] for nm in order], *[deltas[nm] for nm in order],
            *[new_m[nm] for nm in order], *[new_v[nm] for nm in order])
```

```python
import functools
import math

import jax
import jax.numpy as jnp
from jax import lax
from jax.experimental import pallas as pl
from jax.experimental.pallas import tpu as pltpu

F32 = jnp.float32
BF16 = jnp.bfloat16
MESH = pl.DeviceIdType.MESH

EPS = 1e-6
SSM_H = 16
SSM_P = 64
GROUPS_PER_SLAB = 8
SLAB = 128
SLAB_STATES = GROUPS_PER_SLAB * SSM_P
N_CHIPS = 4
N_DEV = 8

ADAM_LR = 0.001
ADAM_B1 = 0.9
ADAM_B2 = 0.999
ADAM_EPS = 1e-08
ADAM_WD = 0.01
ADAM_STEP = 10

MIB = 1024 * 1024
VMEM_CAP = 48 * MIB
SUBLANES = 8

TM_NORM = 512
TM_PROJ = 512
TM_GATE = 256
TM_OUT = 256
TM_DH = 256
T_SCAN = 256
TK_TN = 512
SCAN_UNROLL = 4
TM_TN = 1024
TR_ELT = 256


def _cp(vmem_bytes, **kw):
    return pltpu.CompilerParams(vmem_limit_bytes=int(min(VMEM_CAP, max(16 * MIB, vmem_bytes))), **kw)


def _pallas(body, **kw):
    if "grid" not in kw and "grid_spec" not in kw:
        return pl.pallas_call(body, **kw)
    pin = lambda s: pltpu.HBM(s.shape, s.dtype) if isinstance(s, jax.ShapeDtypeStruct) else s
    out_shape = kw.pop("out_shape")
    out_shape = [pin(s) for s in out_shape] if isinstance(out_shape, (list, tuple)) else pin(out_shape)
    call = pl.pallas_call(body, out_shape=out_shape, **kw)

    def run(*args):
        return call(*[pltpu.with_memory_space_constraint(a, pltpu.HBM) if jnp.issubdtype(a.dtype, jnp.floating) else a
                      for a in args])

    return run


def _my_place():
    return lax.axis_index("x"), lax.axis_index("y"), lax.axis_index("c")


def _other_chips(x, y):
    return [(1 - x, y), (x, 1 - y), (1 - x, 1 - y)]


def _silu(z):
    s = jax.nn.sigmoid(z)
    return z * s, s


def _dsilu(z, s):
    return s * (1.0 + z * (1.0 - s))


_GELU_K = math.sqrt(2.0 / math.pi)
_GELU_C = 0.044715


def _gelu(y):
    th = jnp.tanh(_GELU_K * (y + _GELU_C * y * y * y))
    return 0.5 * y * (1.0 + th), th


def _dgelu(y, th):
    return 0.5 * (1.0 + th) + 0.5 * y * (1.0 - th * th) * _GELU_K * (1.0 + 3.0 * _GELU_C * y * y)


def _cast_bf16(w, name, after=None):
    r, c = w.shape
    tr = min(TR_ELT, r)
    extra = [] if after is None else [after]

    def body(w_ref, *rest):
        rest[-1][...] = w_ref[...].astype(BF16)

    return _pallas(
        body, name=name, grid=(r // tr,),
        in_specs=[pl.BlockSpec((tr, c), lambda i: (i, 0))] + [pl.BlockSpec((SUBLANES, SLAB), lambda i: (0, 0))] * len(extra),
        out_specs=pl.BlockSpec((tr, c), lambda i: (i, 0)),
        out_shape=jax.ShapeDtypeStruct((r, c), BF16),
        compiler_params=_cp(12 * tr * c),
    )(w, *extra)


def _prenorm(x, g):
    t, d = x.shape
    tm = min(TM_NORM, t)

    def body(x_ref, g_ref, h_ref):
        xv = x_ref[...]
        r = lax.rsqrt(jnp.mean(xv * xv, axis=-1, keepdims=True) + EPS)
        h_ref[...] = (xv * r * g_ref[...]).astype(BF16)

    return _pallas(
        body, name="prenorm", grid=(t // tm,),
        in_specs=[pl.BlockSpec((tm, d), lambda i: (i, 0)), pl.BlockSpec((1, d), lambda i: (0, 0))],
        out_specs=pl.BlockSpec((tm, d), lambda i: (i, 0)),
        out_shape=jax.ShapeDtypeStruct((t, d), BF16),
        compiler_params=_cp(20 * tm * d),
    )(x, g)


def _inproj_own(place, a, w_own):
    t, k = a.shape
    n = w_own.shape[1]
    tm = min(TM_PROJ, t)

    def body(place_ref, a_ref, b_ref, o_ref):
        del place_ref
        o_ref[...] = jnp.dot(a_ref[...], b_ref[...], preferred_element_type=F32)

    return _pallas(
        body, name="inproj_own",
        grid_spec=pltpu.PrefetchScalarGridSpec(
            num_scalar_prefetch=1, grid=(t // tm,),
            in_specs=[pl.BlockSpec((tm, k), lambda i, p: (i, 0)), pl.BlockSpec((k, n), lambda i, p: (0, 0))],
            out_specs=pl.BlockSpec((tm, n), lambda i, p: (i, p[1]))),
        out_shape=jax.ShapeDtypeStruct((t, N_CHIPS * n), F32),
        compiler_params=_cp(2 * (2 * tm * k + 2 * k * n + 4 * tm * n) + 4 * MIB),
    )(place, a, w_own)


def _inproj_rest(place, a, b, partial, after):
    t, k = a.shape
    nb, _, n = b.shape
    tm = min(TM_PROJ, t)
    shard = lambda s, p: (p[1] + 1 + s) % nb

    def body(place_ref, a_ref, b_ref, after_ref, partial_ref, o_ref):
        del place_ref, after_ref, partial_ref
        o_ref[...] = jnp.dot(a_ref[...], b_ref[...], preferred_element_type=F32)

    return _pallas(
        body, name="inproj_rest",
        grid_spec=pltpu.PrefetchScalarGridSpec(
            num_scalar_prefetch=1, grid=(nb - 1, t // tm),
            in_specs=[pl.BlockSpec((tm, k), lambda s, i, p: (i, 0)),
                      pl.BlockSpec((None, k, n), lambda s, i, p: (shard(s, p), 0, 0)),
                      pl.BlockSpec((SUBLANES, SLAB), lambda s, i, p: (0, 0)),
                      pl.BlockSpec(memory_space=pl.ANY)],
            out_specs=pl.BlockSpec((tm, n), lambda s, i, p: (i, shard(s, p)))),
        out_shape=jax.ShapeDtypeStruct((t, nb * n), F32),
        input_output_aliases={4: 0},
        compiler_params=_cp(2 * (2 * tm * k + 2 * k * n + 4 * tm * n) + 4 * MIB),
    )(place, a, b, after, partial)


def _matmul_tn(a, b, nb, name):
    t, m = a.shape
    n = b.shape[1] // nb
    tk = min(TK_TN, t)
    tma = min(TM_TN, m)

    def body(a_ref, b_ref, o_ref, o16_ref):
        k = pl.program_id(2)

        @pl.when(k == 0)
        def _():
            o_ref[...] = jnp.zeros_like(o_ref)

        o_ref[...] += lax.dot_general(a_ref[...], b_ref[...], (((0,), (0,)), ((), ())), preferred_element_type=F32)

        @pl.when(k == t // tk - 1)
        def _():
            o16_ref[...] = o_ref[...].astype(BF16)

    blk = pl.BlockSpec((None, tma, n), lambda j, i, k: (j, i, 0))
    return _pallas(
        body, name=name, grid=(nb, m // tma, t // tk),
        in_specs=[pl.BlockSpec((tk, tma), lambda j, i, k: (k, i)), pl.BlockSpec((tk, n), lambda j, i, k: (k, j))],
        out_specs=[blk, blk],
        out_shape=[jax.ShapeDtypeStruct((nb, m, n), F32), jax.ShapeDtypeStruct((nb, m, n), BF16)],
        compiler_params=_cp(2 * (2 * tk * tma + 2 * tk * n + 6 * tma * n) + 8 * MIB),
    )(a, b)


def _outproj(mix, w_out, x, tgt, g_post):
    t, dm = mix.shape
    d = w_out.shape[1]
    tm = min(TM_OUT, t)

    def body(mix_ref, w_ref, x_ref, t_ref, g_ref, loss_ref, dout_ref, do_ref, dmix_ref, gg_ref):
        @pl.when(pl.program_id(0) == 0)
        def _():
            loss_ref[...] = jnp.zeros_like(loss_ref)
            gg_ref[...] = jnp.zeros_like(gg_ref)

        w = w_ref[...]
        o = jnp.dot(mix_ref[...], w, preferred_element_type=F32)
        r = lax.rsqrt(jnp.mean(o * o, axis=-1, keepdims=True) + EPS)
        nh = o * r
        g = g_ref[...]
        e = x_ref[...] + nh * g - t_ref[...]
        loss_ref[...] += jnp.sum(e * e) * (0.5 / d)
        dout = e * (1.0 / d)
        dout_ref[...] = dout
        gg_ref[0:1, :] += jnp.sum(dout * nh, axis=0, keepdims=True)
        dn = dout * g
        do = r * (dn - nh * jnp.mean(dn * nh, axis=-1, keepdims=True))
        dob = do.astype(BF16)
        do_ref[...] = dob
        dmix_ref[...] = lax.dot_general(dob, w, (((1,), (1,)), ((), ())), preferred_element_type=F32)

    row = lambda i: (i, 0)
    fixed = lambda i: (0, 0)
    return _pallas(
        body, name="outproj", grid=(t // tm,),
        in_specs=[pl.BlockSpec((tm, dm), row), pl.BlockSpec((dm, d), fixed), pl.BlockSpec((tm, d), row),
                  pl.BlockSpec((tm, d), row), pl.BlockSpec((1, d), fixed)],
        out_specs=[pl.BlockSpec((SUBLANES, SLAB), fixed), pl.BlockSpec((tm, d), row), pl.BlockSpec((tm, d), row),
                   pl.BlockSpec((tm, dm), row), pl.BlockSpec((SUBLANES, d), fixed)],
        out_shape=[jax.ShapeDtypeStruct((SUBLANES, SLAB), F32), jax.ShapeDtypeStruct((t, d), F32),
                   jax.ShapeDtypeStruct((t, d), BF16), jax.ShapeDtypeStruct((t, dm), F32),
                   jax.ShapeDtypeStruct((SUBLANES, d), F32)],
        compiler_params=_cp(2 * (2 * dm * d + tm * (2 * dm + 4 * d * 3 + 2 * d + 4 * dm)) + 16 * MIB),
    )(mix, w_out, x, tgt, g_post)


def _dh_prenorm_bwd(dproj, w_in, x, dout, g_pre):
    t, d = x.shape
    nb, _, n = w_in.shape
    tm = min(TM_DH, t)

    def body(dp_ref, w_ref, x_ref, dout_ref, g_ref, dx_ref, gg_ref):
        @pl.when(pl.program_id(0) == 0)
        def _():
            gg_ref[...] = jnp.zeros_like(gg_ref)

        dh = None
        for k in range(nb):
            part = lax.dot_general(dp_ref[:, k * n:(k + 1) * n], w_ref[k], (((1,), (1,)), ((), ())),
                                   preferred_element_type=F32)
            dh = part if dh is None else dh + part
        xv = x_ref[...]
        r = lax.rsqrt(jnp.mean(xv * xv, axis=-1, keepdims=True) + EPS)
        xh = xv * r
        gg_ref[0:1, :] += jnp.sum(dh * xh, axis=0, keepdims=True)
        dg = dh * g_ref[...]
        dx_ref[...] = dout_ref[...] + r * (dg - xh * jnp.mean(dg * xh, axis=-1, keepdims=True))

    row = lambda i: (i, 0)
    fixed = lambda i: (0, 0)
    w_spec = pl.BlockSpec(w_in.shape, lambda i: (0, 0, 0), pipeline_mode=pl.Buffered(1))
    return _pallas(
        body, name="dh_prenorm_bwd", grid=(t // tm,),
        in_specs=[pl.BlockSpec((tm, nb * n), row), w_spec,
                  pl.BlockSpec((tm, d), row), pl.BlockSpec((tm, d), row), pl.BlockSpec((1, d), fixed)],
        out_specs=[pl.BlockSpec((tm, d), row), pl.BlockSpec((SUBLANES, d), fixed)],
        out_shape=[jax.ShapeDtypeStruct((t, d), F32), jax.ShapeDtypeStruct((SUBLANES, d), F32)],
        compiler_params=_cp(2 * nb * d * n + 2 * (2 * tm * nb * n + 12 * tm * d) + 16 * tm * d + 4 * MIB),
    )(dproj, w_in, x, dout, g_pre)


def _adamw(w, g, m, v, name):
    r, c = w.shape
    tr = min(TR_ELT, r)
    c1 = 1.0 - ADAM_B1 ** ADAM_STEP
    c2 = 1.0 - ADAM_B2 ** ADAM_STEP

    def body(w_ref, g_ref, m_ref, v_ref, d_ref, mo_ref, vo_ref):
        gv = g_ref[...]
        mn = ADAM_B1 * m_ref[...] + (1.0 - ADAM_B1) * gv
        vn = ADAM_B2 * v_ref[...] + (1.0 - ADAM_B2) * (gv * gv)
        d_ref[...] = -ADAM_LR * ((mn / c1) / (jnp.sqrt(vn / c2) + ADAM_EPS) + ADAM_WD * w_ref[...])
        mo_ref[...] = mn
        vo_ref[...] = vn

    spec = pl.BlockSpec((tr, c), lambda i: (i, 0))
    sds = jax.ShapeDtypeStruct((r, c), F32)
    return _pallas(
        body, name=name, grid=(r // tr,), in_specs=[spec] * 4, out_specs=[spec] * 3, out_shape=[sds] * 3,
        compiler_params=_cp(2 * 7 * 4 * tr * c + 8 * MIB),
    )(w, g, m, v)


def _adamw_small(g, w, m, v):
    r, c = w.shape
    c1 = 1.0 - ADAM_B1 ** ADAM_STEP
    c2 = 1.0 - ADAM_B2 ** ADAM_STEP

    def body(g_ref, w_ref, m_ref, v_ref, d_ref, mo_ref, vo_ref):
        gv = g_ref[...]
        mn = ADAM_B1 * m_ref[...] + (1.0 - ADAM_B1) * gv
        vn = ADAM_B2 * v_ref[...] + (1.0 - ADAM_B2) * (gv * gv)
        d_ref[...] = -ADAM_LR * ((mn / c1) / (jnp.sqrt(vn / c2) + ADAM_EPS) + ADAM_WD * w_ref[...])
        mo_ref[...] = mn
        vo_ref[...] = vn

    sds = jax.ShapeDtypeStruct((r, c), F32)
    return _pallas(
        body, name="adamw_small", out_shape=[sds] * 3,
        compiler_params=_cp(12 * 4 * r * c + 8 * MIB),
    )(g, w, m, v)


def _allreduce_small(v):
    r, c = v.shape
    assert r % (2 * SUBLANES) == 0
    h = r // 2

    def body(v_ref, out_ref, got_ref, chip_ref, slots_ref, send, recv):
        x, y, cc = _my_place()
        k = 2 * x + y
        sib = (x, y, 1 - cc)

        def rc(s, src, dst, to):
            return pltpu.make_async_remote_copy(src_ref=src, dst_ref=dst, send_sem=send.at[s], recv_sem=recv.at[s],
                                                device_id=to, device_id_type=MESH)

        pair = rc(0, v_ref, got_ref, sib)
        pair.start()
        pair.wait()
        chip_ref[...] = v_ref[...] + got_ref[...]
        mine = pl.ds(pl.multiple_of(cc * h, SUBLANES), h)
        theirs = pl.ds(pl.multiple_of((1 - cc) * h, SUBLANES), h)
        chips = _other_chips(x, y)
        sent = []
        for j, (cx, cy) in enumerate(chips):
            cp = rc(1 + j, chip_ref.at[mine], slots_ref.at[k], (cx, cy, cc))
            cp.start()
            sent.append(cp)
        slots_ref[k] = chip_ref[mine, :]
        for j, (cx, cy) in enumerate(chips):
            rc(1 + j, chip_ref.at[mine], slots_ref.at[2 * cx + cy], (cx, cy, cc)).wait_recv()
        total = slots_ref[0]
        for kk in range(1, N_CHIPS):
            total = total + slots_ref[kk]
        out_ref[mine, :] = total
        for cp in sent:
            cp.wait_send()
        share = rc(N_CHIPS, out_ref.at[mine], out_ref.at[mine], sib)
        share.start()
        rc(N_CHIPS, out_ref.at[theirs], out_ref.at[theirs], sib).wait_recv()
        share.wait_send()

    vm = pl.BlockSpec(memory_space=pltpu.VMEM)
    return _pallas(
        body, name="allreduce_small", out_shape=jax.ShapeDtypeStruct((r, c), F32), in_specs=[vm], out_specs=vm,
        scratch_shapes=[pltpu.VMEM((r, c), F32), pltpu.VMEM((r, c), F32), pltpu.VMEM((N_CHIPS, h, c), F32),
                        pltpu.SemaphoreType.DMA((N_CHIPS + 1,)), pltpu.SemaphoreType.DMA((N_CHIPS + 1,))],
        compiler_params=_cp(6 * 4 * r * c + 8 * MIB),
    )(v)


def _zoh(a_re, a_im, log_dt, b_re2, b_im2, expand):
    dt = jnp.exp(log_dt)
    mag = jnp.exp(a_re * dt)
    lbr, lbi = mag * jnp.cos(a_im * dt), mag * jnp.sin(a_im * dt)
    nr, ni = lbr - 1.0, lbi
    den = a_re * a_re + a_im * a_im
    qr = (nr * a_re + ni * a_im) / den
    qi = (ni * a_re - nr * a_im) / den
    qr2 = jnp.dot(qr, expand, precision=lax.Precision.HIGHEST, preferred_element_type=F32)
    qi2 = jnp.dot(qi, expand, precision=lax.Precision.HIGHEST, preferred_element_type=F32)
    return lbr, lbi, qr2 * b_re2 - qi2 * b_im2, qr2 * b_im2 + qi2 * b_re2


def _zoh_fwd(a_re, a_im, log_dt, b_re2, b_im2, expand, power):
    g, p = a_re.shape

    def body(ar_ref, ai_ref, ld_ref, br_ref, bi_ref, e_ref, lbr_ref, lbi_ref, bbr_ref, bbi_ref, pw_ref):
        ar, ai, ld = ar_ref[...], ai_ref[...], ld_ref[...]
        lbr, lbi, bbr, bbi = _zoh(ar, ai, ld, br_ref[...], bi_ref[...], e_ref[...])
        lbr_ref[...], lbi_ref[...], bbr_ref[...], bbi_ref[...] = lbr, lbi, bbr, bbi
        dt = jnp.exp(ld) * float(power)
        mag = jnp.exp(ar * dt)
        pw_ref[0] = mag * jnp.cos(ai * dt)
        pw_ref[1] = mag * jnp.sin(ai * dt)

    gp = jax.ShapeDtypeStruct((g, p), F32)
    gph = jax.ShapeDtypeStruct(b_re2.shape, F32)
    return _pallas(
        body, name="zoh_fwd", out_shape=[gp, gp, gph, gph, jax.ShapeDtypeStruct((2, g, p), F32)],
        compiler_params=_cp(16 * MIB),
    )(a_re, a_im, log_dt, b_re2, b_im2, expand)


def _zoh_bwd(a_re, a_im, log_dt, b_re2, b_im2, expand, g_lbr, g_lbi, g_bbr, g_bbi):
    def body(ar_ref, ai_ref, ld_ref, br_ref, bi_ref, e_ref, c0, c1, c2, c3, gar, gai, gld, gbr, gbi):
        e = e_ref[...]
        _, vjp = jax.vjp(lambda a, b, c, d, f: _zoh(a, b, c, d, f, e),
                         ar_ref[...], ai_ref[...], ld_ref[...], br_ref[...], bi_ref[...])
        gar[...], gai[...], gld[...], gbr[...], gbi[...] = vjp((c0[...], c1[...], c2[...], c3[...]))

    sds = lambda a: jax.ShapeDtypeStruct(a.shape, F32)
    return _pallas(
        body, name="zoh_bwd", out_shape=[sds(a_re), sds(a_im), sds(log_dt), sds(b_re2), sds(b_im2)],
        compiler_params=_cp(16 * MIB),
    )(a_re, a_im, log_dt, b_re2, b_im2, expand, g_lbr, g_lbi, g_bbr, g_bbi)


def _blockdiag(blocks):
    nq, _, r, c = blocks.shape
    eye = jnp.eye(GROUPS_PER_SLAB, dtype=blocks.dtype)
    out = blocks[:, :, :, None, :] * eye[None, :, None, :, None]
    return out.reshape(nq, GROUPS_PER_SLAB * r, GROUPS_PER_SLAB * c)


def _blockdiag_take(dense, r, c):
    nq = dense.shape[0]
    d5 = dense.reshape(nq, GROUPS_PER_SLAB, r, GROUPS_PER_SLAB, c)
    return jnp.stack([d5[:, i, :, i, :] for i in range(GROUPS_PER_SLAB)], axis=1)


def _scan_slab(s_ref, row0, q, lam_ref, pw_ref, car_ref, steps, reverse, prev_ref=None, prev_row0=0, glam_ref=None):
    sign = -1.0 if reverse else 1.0
    half = SLAB_STATES // SLAB
    cols = [(q * 2 * half + m, q * 2 * half + half + m, q * SLAB_STATES + m * SLAB) for m in range(half)]
    nm = len(cols)
    full = (SUBLANES, SLAB)
    lam = [(jnp.broadcast_to(lam_ref[0:1, pl.ds(cl, SLAB)], full),
            jnp.broadcast_to(sign * lam_ref[1:2, pl.ds(cl, SLAB)], full)) for (_, _, cl) in cols]

    def step_rows(jj, base):
        j = (steps - 1 - jj) if reverse else jj
        return j, pl.ds(pl.multiple_of(base + j * SUBLANES, SUBLANES), SUBLANES)

    def pass1(jj, car):
        _, rows = step_rows(jj, row0)
        out = []
        for m, (cr, ci, _) in enumerate(cols):
            sr, si = car[2 * m], car[2 * m + 1]
            lr, li = lam[m]
            nr = lr * sr - li * si + s_ref[cr, rows, :]
            ni = lr * si + li * sr + s_ref[ci, rows, :]
            s_ref[cr, rows, :] = nr
            s_ref[ci, rows, :] = ni
            out += [nr, ni]
        return tuple(out)

    def unrolled(step_fn):
        def outer(jo, carry):
            for k in range(SCAN_UNROLL):
                carry = step_fn(jo * SCAN_UNROLL + k, carry)
            return carry
        return outer

    assert steps % SCAN_UNROLL == 0
    ends = lax.fori_loop(0, steps // SCAN_UNROLL, unrolled(pass1), tuple(jnp.zeros(full, F32) for _ in range(2 * nm)))

    entry = []
    for m, (cr, ci, cl) in enumerate(cols):
        ljr = pw_ref[0:1, pl.ds(cl, SLAB)]
        lji = sign * pw_ref[1:2, pl.ds(cl, SLAB)]
        c_r = car_ref[cr, 0:1, :]
        c_i = car_ref[ci, 0:1, :]
        rows_r, rows_i = [None] * SUBLANES, [None] * SUBLANES
        order = range(SUBLANES - 1, -1, -1) if reverse else range(SUBLANES)
        for b in order:
            rows_r[b], rows_i[b] = c_r, c_i
            e_r, e_i = ends[2 * m][b:b + 1], ends[2 * m + 1][b:b + 1]
            c_r, c_i = ljr * c_r - lji * c_i + e_r, ljr * c_i + lji * c_r + e_i
        car_ref[cr, 0:1, :] = c_r
        car_ref[ci, 0:1, :] = c_i
        entry.append((jnp.concatenate(rows_r, axis=0), jnp.concatenate(rows_i, axis=0)))

    def pass2(jj, carry):
        j, rows = step_rows(jj, row0)
        decayed, acc = carry[:2 * nm], carry[2 * nm:]
        out_d, out_a = [], []
        for m, (cr, ci, cl) in enumerate(cols):
            lr, li = lam[m]
            dr, di = decayed[2 * m], decayed[2 * m + 1]
            dr, di = lr * dr - li * di, lr * di + li * dr
            nr = s_ref[cr, rows, :] + dr
            ni = s_ref[ci, rows, :] + di
            s_ref[cr, rows, :] = nr
            s_ref[ci, rows, :] = ni
            out_d += [dr, di]
            if prev_ref is not None:
                prow = pl.ds(pl.multiple_of(prev_row0 + (j - 1) * SUBLANES, SUBLANES), SUBLANES)
                qr = prev_ref[cr, prow, :]
                qi = prev_ref[ci, prow, :]
                out_a += [acc[2 * m] + (nr * qr + ni * qi), acc[2 * m + 1] + (ni * qr - nr * qi)]
        return tuple(out_d) + tuple(out_a)

    n_acc = 2 * nm if prev_ref is not None else 0
    init = tuple(e for pair in entry for e in pair) + tuple(jnp.zeros(full, F32) for _ in range(n_acc))
    accs = lax.fori_loop(0, steps // SCAN_UNROLL, unrolled(pass2), init)[2 * nm:]
    if prev_ref is not None:
        for m, (_, _, cl) in enumerate(cols):
            glam_ref[0:1, pl.ds(cl, SLAB)] += jnp.sum(accs[2 * m], axis=0, keepdims=True)
            glam_ref[1:2, pl.ds(cl, SLAB)] += jnp.sum(accs[2 * m + 1], axis=0, keepdims=True)


def _permute_rows_f32(perm_bf16, v):
    hi = v.astype(BF16)
    lo = (v - hi.astype(F32)).astype(BF16)
    return (jnp.dot(perm_bf16, hi, preferred_element_type=F32) + jnp.dot(perm_bf16, lo, preferred_element_type=F32))


def _put_slab(s_ref, rows, q, val):
    per = 2 * SLAB_STATES // SLAB
    for i in range(per):
        s_ref[q * per + i, rows, :] = val[:, i * SLAB:(i + 1) * SLAB]


def _get_slab(s_ref, rows, q):
    per = 2 * SLAB_STATES // SLAB
    return jnp.concatenate([s_ref[q * per + i, rows, :] for i in range(per)], axis=1)


def _step_major_perm(tt):
    r = jnp.arange(tt)
    held = (r % SUBLANES) * (tt // SUBLANES) + r // SUBLANES
    return held[:, None] == r[None, :]


def _ssm_fwd(proj, u_block, bq, cq, lam, pw, d_skip):
    t = proj.shape[0]
    nq, ds, w2 = bq.shape
    assert ds == SLAB and w2 == 2 * SLAB_STATES
    dssm = nq * SLAB
    width = nq * w2
    ntile = width // SLAB
    tt = min(T_SCAN, t)
    steps = tt // SUBLANES
    perm = _step_major_perm(tt)
    pm, pmt = perm.astype(BF16), perm.T.astype(BF16)

    def body(u_ref, pm_ref, pmt_ref, bq_ref, cq_ref, lam_ref, pw_ref, d_ref, y_ref, cin_ref, s_ref, car_ref):
        @pl.when(pl.program_id(0) == 0)
        def _():
            car_ref[...] = jnp.zeros_like(car_ref)

        cin_ref[...] = jnp.broadcast_to(car_ref[:, 0:1, :], cin_ref.shape)
        u = u_ref[...]
        ub = jnp.dot(pm_ref[...], u.astype(BF16), preferred_element_type=F32).astype(BF16)
        everything = slice(None)
        for q in range(nq):
            _put_slab(s_ref, everything, q,
                      jnp.dot(ub[:, q * SLAB:(q + 1) * SLAB], bq_ref[q], preferred_element_type=F32))
        for q in range(nq):
            _scan_slab(s_ref, 0, q, lam_ref, pw_ref, car_ref, steps, reverse=False)
        y_sm = jnp.concatenate(
            [jnp.dot(_get_slab(s_ref, everything, q).astype(BF16), cq_ref[q], preferred_element_type=F32)
             for q in range(nq)], axis=1)
        y_ref[...] = _permute_rows_f32(pmt_ref[...], y_sm) + d_ref[...] * u

    c3 = lambda i: (0, 0, 0)
    c2 = lambda i: (0, 0)
    return _pallas(
        body, name="ssm_fwd", grid=(t // tt,),
        in_specs=[pl.BlockSpec((tt, dssm), lambda i: (i, u_block)), pl.BlockSpec((tt, tt), c2),
                  pl.BlockSpec((tt, tt), c2), pl.BlockSpec(bq.shape, c3),
                  pl.BlockSpec(cq.shape, c3), pl.BlockSpec(lam.shape, c2), pl.BlockSpec(pw.shape, c2),
                  pl.BlockSpec((1, dssm), c2)],
        out_specs=[pl.BlockSpec((tt, dssm), lambda i: (i, 0)),
                   pl.BlockSpec((None, ntile, SUBLANES, SLAB), lambda i: (i, 0, 0, 0))],
        out_shape=[jax.ShapeDtypeStruct((t, dssm), F32), jax.ShapeDtypeStruct((t // tt, ntile, SUBLANES, SLAB), F32)],
        scratch_shapes=[pltpu.VMEM((ntile, tt, SLAB), F32), pltpu.VMEM((ntile, SUBLANES, SLAB), F32)],
        compiler_params=_cp(2 * (8 * tt * dssm + 4 * nq * ds * w2) + 4 * tt * width + 16 * MIB,
                            dimension_semantics=("arbitrary",)),
    )(proj, pm, pmt, bq, cq, lam, pw, d_skip)


def _ssm_bwd(proj, u_block, dy, cin, dproj, bq, cq, lam, pw, d_skip):
    t = proj.shape[0]
    nq, ds, w2 = bq.shape
    dssm = nq * SLAB
    width = nq * w2
    ntile = width // SLAB
    tt = min(T_SCAN, t)
    nt = t // tt
    steps = tt // SUBLANES
    halo = SUBLANES
    perm = _step_major_perm(tt)
    pm, pmt = perm.astype(BF16), perm.T.astype(BF16)

    def body(u_ref, dy_ref, cin_ref, dp_any, pm_ref, pmt_ref, bq_ref, cq_ref, lam_ref, pw_ref, d_ref,
             du_ref, gb_ref, gc_ref, glam_ref, gd_ref, s_ref, gs_ref, car_f, car_b):
        del dp_any

        @pl.when(pl.program_id(0) == 0)
        def _():
            car_b[...] = jnp.zeros_like(car_b)
            gb_ref[...] = jnp.zeros_like(gb_ref)
            gc_ref[...] = jnp.zeros_like(gc_ref)
            glam_ref[...] = jnp.zeros_like(glam_ref)
            gd_ref[...] = jnp.zeros_like(gd_ref)

        u = u_ref[...]
        dyv = dy_ref[...]
        gd_ref[0:1, :] += jnp.sum(dyv * u, axis=0, keepdims=True)
        pmv = pm_ref[...]
        ub = jnp.dot(pmv, u.astype(BF16), preferred_element_type=F32).astype(BF16)
        dyb = jnp.dot(pmv, dyv.astype(BF16), preferred_element_type=F32).astype(BF16)
        car_f[...] = cin_ref[...]
        data = slice(halo, halo + tt)
        everything = slice(None)
        for q in range(nq):
            _put_slab(s_ref, data, q, jnp.dot(ub[:, q * SLAB:(q + 1) * SLAB], bq_ref[q], preferred_element_type=F32))
        for q in range(nq):
            _scan_slab(s_ref, halo, q, lam_ref, pw_ref, car_f, steps, reverse=False)
        last_step = s_ref[:, halo + tt - SUBLANES:halo + tt, :]
        s_ref[:, 0:halo, :] = jnp.concatenate([cin_ref[:, 0:1, :], last_step[:, 0:SUBLANES - 1, :]], axis=1)
        tn = (((0,), (0,)), ((), ()))
        nt_dims = (((1,), (1,)), ((), ()))
        for q in range(nq):
            sl = slice(q * SLAB, (q + 1) * SLAB)
            gc_ref[q] += lax.dot_general(dyb[:, sl], _get_slab(s_ref, data, q).astype(BF16), tn,
                                         preferred_element_type=F32)
            _put_slab(gs_ref, everything, q,
                      lax.dot_general(dyb[:, sl], cq_ref[q], nt_dims, preferred_element_type=F32))
        for q in range(nq):
            _scan_slab(gs_ref, 0, q, lam_ref, pw_ref, car_b, steps, reverse=True,
                       prev_ref=s_ref, prev_row0=halo, glam_ref=glam_ref)
        du_parts = []
        for q in range(nq):
            sl = slice(q * SLAB, (q + 1) * SLAB)
            gsb = _get_slab(gs_ref, everything, q).astype(BF16)
            du_parts.append(lax.dot_general(gsb, bq_ref[q], nt_dims, preferred_element_type=F32))
            gb_ref[q] += lax.dot_general(ub[:, sl], gsb, tn, preferred_element_type=F32)
        du_sm = jnp.concatenate(du_parts, axis=1)
        du_ref[...] = (_permute_rows_f32(pmt_ref[...], du_sm) + dyv * d_ref[...]).astype(BF16)

    c3 = lambda i: (0, 0, 0)
    c2 = lambda i: (0, 0)
    rev = lambda i: (nt - 1 - i, 0)
    dense = jax.ShapeDtypeStruct((nq, SLAB, w2), F32)
    gp = lam.shape[1]
    return _pallas(
        body, name="ssm_bwd", grid=(nt,),
        in_specs=[pl.BlockSpec((tt, dssm), lambda i: (nt - 1 - i, u_block)), pl.BlockSpec((tt, dssm), rev),
                  pl.BlockSpec((None, ntile, SUBLANES, SLAB), lambda i: (nt - 1 - i, 0, 0, 0)),
                  pl.BlockSpec(memory_space=pl.ANY), pl.BlockSpec((tt, tt), c2), pl.BlockSpec((tt, tt), c2),
                  pl.BlockSpec(bq.shape, c3), pl.BlockSpec(cq.shape, c3),
                  pl.BlockSpec(lam.shape, c2), pl.BlockSpec(pw.shape, c2), pl.BlockSpec((1, dssm), c2)],
        out_specs=[pl.BlockSpec((tt, dssm), lambda i: (nt - 1 - i, u_block)), pl.BlockSpec(dense.shape, c3),
                   pl.BlockSpec(dense.shape, c3), pl.BlockSpec((SUBLANES, gp), c2), pl.BlockSpec((SUBLANES, dssm), c2)],
        out_shape=[jax.ShapeDtypeStruct(dproj.shape, dproj.dtype), dense, dense,
                   jax.ShapeDtypeStruct((SUBLANES, gp), F32), jax.ShapeDtypeStruct((SUBLANES, dssm), F32)],
        scratch_shapes=[pltpu.VMEM((ntile, tt + halo, SLAB), F32), pltpu.VMEM((ntile, tt, SLAB), F32),
                        pltpu.VMEM((ntile, SUBLANES, SLAB), F32), pltpu.VMEM((ntile, SUBLANES, SLAB), F32)],
        input_output_aliases={3: 0},
        compiler_params=_cp(2 * (10 * tt * dssm + 4 * nq * ds * w2 + 8 * nq * SLAB * w2)
                            + 8 * tt * width + 12 * MIB, dimension_semantics=("arbitrary",)),
    )(proj, dy, cin, dproj, pm, pmt, bq, cq, lam, pw, d_skip)


def _gate_fwd(proj, y, conv_w, conv_b, w_glu, b_glu, dc):
    t = proj.shape[0]
    dssm = y.shape[1]
    assert dc == dssm
    tm = min(TM_GATE, t)
    halo = SUBLANES

    def body(b_ref, c_ref, v_ref, zc_ref, zs_ref, y_ref, cw_ref, cb_ref, wg_ref, bg_ref, mix_ref, cv_buf):
        @pl.when(pl.program_id(0) == 0)
        def _():
            cv_buf[0:halo, :] = jnp.zeros((halo, dc), F32)

        cv = c_ref[...] * v_ref[...]
        cv_buf[halo:, :] = cv
        conv = (cb_ref[...] + cw_ref[2:3, :] * cv + cw_ref[1:2, :] * cv_buf[halo - 1:halo - 1 + tm, :]
                + cw_ref[0:1, :] * cv_buf[halo - 2:halo - 2 + tm, :])
        sz, _ = _silu(zc_ref[...])
        mix_ref[:, 0:dc] = (b_ref[...] * conv * sz).astype(BF16)
        cv_buf[0:halo, :] = cv_buf[tm:tm + halo, :]
        ge, _ = _gelu(y_ref[...])
        gl = jnp.dot(ge.astype(BF16), wg_ref[...], preferred_element_type=F32) + bg_ref[...]
        szs, _ = _silu(zs_ref[...])
        mix_ref[:, dc:] = (ge * jax.nn.sigmoid(gl) * szs).astype(BF16)

    col = lambda j: pl.BlockSpec((tm, dc), lambda i, j=j: (i, j))
    fixed = lambda i: (0, 0)
    return _pallas(
        body, name="gate_fwd", grid=(t // tm,),
        in_specs=[col(0), col(1), col(2), col(3), col(5), pl.BlockSpec((tm, dssm), lambda i: (i, 0)),
                  pl.BlockSpec(conv_w.shape, fixed), pl.BlockSpec((1, dc), fixed),
                  pl.BlockSpec(w_glu.shape, fixed), pl.BlockSpec((1, dssm), fixed)],
        out_specs=pl.BlockSpec((tm, dc + dssm), lambda i: (i, 0)),
        out_shape=jax.ShapeDtypeStruct((t, dc + dssm), BF16),
        scratch_shapes=[pltpu.VMEM((tm + halo, dc), F32)],
        compiler_params=_cp(2 * (6 * 4 * tm * dc + 2 * tm * (dc + dssm) + 2 * dssm * dssm) + 24 * tm * dc + 8 * MIB,
                            dimension_semantics=("arbitrary",)),
    )(proj, proj, proj, proj, proj, y, conv_w, conv_b, w_glu, b_glu)


def _gate_bwd(proj, y, dmix, conv_w, conv_b, w_glu, b_glu, dc):
    t = proj.shape[0]
    dssm = y.shape[1]
    tm = min(TM_GATE, t)
    nt = t // tm
    halo = SUBLANES
    blocks_per_tile = tm // halo

    def body(b_ref, c_ref, v_ref, zc_ref, zs_ref, cp_ref, vp_ref, y_ref, dm_ref, cw_ref, cb_ref, wg_ref, bg_ref,
             dp_ref, dy_ref, gs_ref, gwg_ref, cv_buf, dc_buf):
        i = pl.program_id(0)

        @pl.when(i == 0)
        def _():
            dc_buf[tm:, :] = jnp.zeros((halo, dc), F32)
            gs_ref[...] = jnp.zeros_like(gs_ref)
            gwg_ref[...] = jnp.zeros_like(gwg_ref)

        first_tile = (i == nt - 1)
        bv, cg, vv, zc = b_ref[...], c_ref[...], v_ref[...], zc_ref[...]
        cv = cg * vv
        cv_buf[0:halo, :] = jnp.where(first_tile, 0.0, cp_ref[...] * vp_ref[...])
        cv_buf[halo:, :] = cv
        w0, w1, w2 = cw_ref[0:1, :], cw_ref[1:2, :], cw_ref[2:3, :]
        conv = (cb_ref[...] + w2 * cv + w1 * cv_buf[halo - 1:halo - 1 + tm, :]
                + w0 * cv_buf[halo - 2:halo - 2 + tm, :])
        sz, sgc = _silu(zc)
        dyc = dm_ref[:, 0:dc]
        dp_ref[:, 0:dc] = (dyc * conv * sz).astype(BF16)
        dp_ref[:, 3 * dc:4 * dc] = (dyc * bv * conv * _dsilu(zc, sgc)).astype(BF16)
        dconv = dyc * bv * sz
        dc_buf[0:tm, :] = dconv
        d1 = dc_buf[1:1 + tm, :]
        d2 = dc_buf[2:2 + tm, :]
        dcv = w2 * dconv + w1 * d1 + w0 * d2
        dp_ref[:, dc:2 * dc] = (dcv * vv).astype(BF16)
        dp_ref[:, 2 * dc:3 * dc] = (dcv * cg).astype(BF16)
        gs_ref[0:1, :] += jnp.sum(cv * d2, axis=0, keepdims=True)
        gs_ref[1:2, :] += jnp.sum(cv * d1, axis=0, keepdims=True)
        gs_ref[2:3, :] += jnp.sum(cv * dconv, axis=0, keepdims=True)
        gs_ref[3:4, :] += jnp.sum(dconv, axis=0, keepdims=True)
        dc_buf[tm:, :] = dc_buf[0:halo, :]

        yv, zs = y_ref[...], zs_ref[...]
        ge, th = _gelu(yv)
        geb = ge.astype(BF16)
        wg = wg_ref[...]
        gl = jnp.dot(geb, wg, preferred_element_type=F32) + bg_ref[...]
        sg = jax.nn.sigmoid(gl)
        szs, sgs = _silu(zs)
        dys = dm_ref[:, dc:]
        ys = ge * sg
        dp_ref[:, 4 * dc:5 * dc] = jnp.zeros((tm, dc), BF16)
        dp_ref[:, 5 * dc:] = (dys * ys * _dsilu(zs, sgs)).astype(BF16)
        d_ys = dys * szs
        dgl = d_ys * ge * sg * (1.0 - sg)
        dglb = dgl.astype(BF16)
        gs_ref[4:5, :] += jnp.sum(dgl, axis=0, keepdims=True)
        gwg_ref[...] += lax.dot_general(geb, dglb, (((0,), (0,)), ((), ())), preferred_element_type=F32)
        dge = d_ys * sg + lax.dot_general(dglb, wg, (((1,), (1,)), ((), ())), preferred_element_type=F32)
        dy_ref[...] = dge * _dgelu(yv, th)

    col = lambda j: pl.BlockSpec((tm, dc), lambda i, j=j: (nt - 1 - i, j))
    prev = lambda j: pl.BlockSpec((halo, dc), lambda i, j=j: (jnp.maximum((nt - 1 - i) * blocks_per_tile - 1, 0), j))
    rev = lambda i: (nt - 1 - i, 0)
    fixed = lambda i: (0, 0)
    return _pallas(
        body, name="gate_bwd", grid=(nt,),
        in_specs=[col(0), col(1), col(2), col(3), col(5), prev(1), prev(2), pl.BlockSpec((tm, dssm), rev),
                  pl.BlockSpec((tm, dc + dssm), rev), pl.BlockSpec(conv_w.shape, fixed), pl.BlockSpec((1, dc), fixed),
                  pl.BlockSpec(w_glu.shape, fixed), pl.BlockSpec((1, dssm), fixed)],
        out_specs=[pl.BlockSpec((tm, 6 * dc), rev), pl.BlockSpec((tm, dssm), rev),
                   pl.BlockSpec((2 * SUBLANES, dc), fixed), pl.BlockSpec((dssm, dssm), fixed)],
        out_shape=[jax.ShapeDtypeStruct((t, 6 * dc), BF16), jax.ShapeDtypeStruct((t, dssm), F32),
                   jax.ShapeDtypeStruct((2 * SUBLANES, dc), F32), jax.ShapeDtypeStruct((dssm, dssm), F32)],
        scratch_shapes=[pltpu.VMEM((tm + halo, dc), F32), pltpu.VMEM((tm + halo, dc), F32)],
        compiler_params=_cp(2 * (6 * 4 * tm * dc + 8 * tm * dc + 12 * tm * dc + 4 * tm * dc + 6 * dssm * dssm)
                            + 40 * tm * dc + 8 * MIB, dimension_semantics=("arbitrary",)),
    )(proj, proj, proj, proj, proj, proj, proj, y, dmix, conv_w, conv_b, w_glu, b_glu)


def _allgather_halves(parts, name):
    n = len(parts)
    per = 2 * (N_CHIPS - 1)

    def body(*refs):
        ins, outs = refs[:n], refs[n:2 * n]
        send, recv, local = refs[2 * n:]
        x, y, c = _my_place()
        k = 2 * x + y
        sib = (x, y, 1 - c)
        chips = _other_chips(x, y)

        def rc(t, s, src, dst, to):
            return pltpu.make_async_remote_copy(src_ref=src, dst_ref=dst, send_sem=send.at[t * per + s],
                                                recv_sem=recv.at[t * per + s], device_id=to, device_id_type=MESH)

        mine = [pltpu.make_async_copy(ins[t], outs[t].at[k], local.at[t]) for t in range(n)]
        for cp in mine:
            cp.start()
        sent = []
        for t in range(n):
            for j, (cx, cy) in enumerate(chips):
                cp = rc(t, j, ins[t].at[c], outs[t].at[k, c], (cx, cy, c))
                cp.start()
                sent.append(cp)
        for t in range(n):
            for j, (cx, cy) in enumerate(chips):
                landed = outs[t].at[2 * cx + cy, c]
                rc(t, j, landed, landed, (cx, cy, c)).wait_recv()
                cp = rc(t, N_CHIPS - 1 + j, landed, landed, sib)
                cp.start()
                sent.append(cp)
        for t in range(n):
            for j, (cx, cy) in enumerate(chips):
                other = outs[t].at[2 * cx + cy, 1 - c]
                rc(t, N_CHIPS - 1 + j, other, other, sib).wait_recv()
        for cp in sent:
            cp.wait_send()
        for cp in mine:
            cp.wait()

    any_spec = pl.BlockSpec(memory_space=pl.ANY)
    return _pallas(
        body, name=name, in_specs=[any_spec] * n, out_specs=[any_spec] * n,
        out_shape=[jax.ShapeDtypeStruct((N_CHIPS,) + p.shape, p.dtype) for p in parts],
        scratch_shapes=[pltpu.SemaphoreType.DMA((n * per,)), pltpu.SemaphoreType.DMA((n * per,)),
                        pltpu.SemaphoreType.DMA((n,))],
        compiler_params=_cp(16 * MIB),
    )(*parts)


def _allgather_flat(v, name):
    r, c = v.shape
    rels = [(dx, dy, dc) for dx in (0, 1) for dy in (0, 1) for dc in (0, 1)][1:]

    def body(v_ref, out_ref, send, recv):
        x, y, cc = _my_place()
        me = 4 * x + 2 * y + cc

        def peer(rel):
            dx, dy, dc = rel
            return (1 - x if dx else x, 1 - y if dy else y, 1 - cc if dc else cc)

        def rc(s, slot, to):
            return pltpu.make_async_remote_copy(src_ref=v_ref, dst_ref=out_ref.at[slot], send_sem=send.at[s],
                                                recv_sem=recv.at[s], device_id=to, device_id_type=MESH)

        sent = []
        for s, rel in enumerate(rels):
            cp = rc(s, me, peer(rel))
            cp.start()
            sent.append(cp)
        out_ref[me] = v_ref[...]
        for s, rel in enumerate(rels):
            px, py, pc = peer(rel)
            rc(s, 4 * px + 2 * py + pc, (px, py, pc)).wait_recv()
        for cp in sent:
            cp.wait_send()

    return _pallas(
        body, name=name, out_shape=jax.ShapeDtypeStruct((N_DEV, r, c), F32),
        in_specs=[pl.BlockSpec(memory_space=pltpu.VMEM)], out_specs=pl.BlockSpec(memory_space=pltpu.VMEM),
        scratch_shapes=[pltpu.SemaphoreType.DMA((N_DEV - 1,)), pltpu.SemaphoreType.DMA((N_DEV - 1,))],
        compiler_params=_cp((N_DEV + 2) * 4 * r * c + 8 * MIB),
    )(v)


def _rs_pair_exchange(grads, name):
    n = len(grads)

    def body(*refs):
        ins, outs = refs[:n], refs[n:2 * n]
        send, recv = refs[2 * n:]
        x, y, c = _my_place()
        cps = []
        for t in range(n):
            cp = pltpu.make_async_remote_copy(src_ref=ins[t].at[:, 1 - c], dst_ref=outs[t], send_sem=send.at[t],
                                              recv_sem=recv.at[t], device_id=(x, y, 1 - c), device_id_type=MESH)
            cp.start()
            cps.append(cp)
        for cp in cps:
            cp.wait()

    any_spec = pl.BlockSpec(memory_space=pl.ANY)
    return _pallas(
        body, name=name, in_specs=[any_spec] * n, out_specs=[any_spec] * n,
        out_shape=[jax.ShapeDtypeStruct((g.shape[0],) + g.shape[2:], g.dtype) for g in grads],
        scratch_shapes=[pltpu.SemaphoreType.DMA((n,)), pltpu.SemaphoreType.DMA((n,))],
        compiler_params=_cp(16 * MIB),
    )(*grads)


def _rs_pair_add(place, grad, got, name):
    nk, _, r2, c = grad.shape
    tr = min(TR_ELT, r2)

    def body(place_ref, g_ref, r_ref, o16_ref, o32_ref):
        del place_ref
        s = g_ref[...] + r_ref[...].astype(F32)
        o32_ref[...] = s
        o16_ref[...] = s.astype(BF16)

    blk = pl.BlockSpec((None, tr, c), lambda k, i, p: (k, i, 0))
    return _pallas(
        body, name=name,
        grid_spec=pltpu.PrefetchScalarGridSpec(
            num_scalar_prefetch=1, grid=(nk, r2 // tr),
            in_specs=[pl.BlockSpec((None, None, tr, c), lambda k, i, p: (k, p[0], i, 0)), blk],
            out_specs=[blk, blk]),
        out_shape=[jax.ShapeDtypeStruct((nk, r2, c), BF16), jax.ShapeDtypeStruct((nk, r2, c), F32)],
        compiler_params=_cp(2 * 14 * tr * c + 8 * MIB),
    )(place, grad, got)


_HBM_SPEC = pl.BlockSpec(memory_space=pltpu.HBM)
_SEM_SPEC = pl.BlockSpec(memory_space=pltpu.SEMAPHORE)
_DATAFLOW = pltpu.SideEffectType.DATAFLOW_SIDE_EFFECTING


def _split_copy_start(srcs, land_shapes, plan, n_sems, name):
    ns, nl = len(srcs), len(land_shapes)

    def body(*refs):
        src_refs, land_refs = refs[:ns], refs[ns:ns + nl]
        send, recv = refs[ns + nl], refs[ns + nl + 1]
        token = refs[-1]
        sends, _ = plan(src_refs, land_refs)
        for src, dst, to, si, ri in sends:
            pltpu.make_async_remote_copy(src_ref=src, dst_ref=dst, send_sem=send.at[si], recv_sem=recv.at[ri],
                                         device_id=to, device_id_type=MESH).start()
        token[...] = jnp.zeros_like(token)

    lands = [lax.empty(shp, dt) for shp, dt in land_shapes]
    through = [pltpu.HBM(a.shape, a.dtype) for a in srcs] + [pltpu.HBM(shp, dt) for shp, dt in land_shapes]
    out = pl.pallas_call(
        body, name=name,
        out_shape=(pltpu.SemaphoreType.DMA((n_sems,)), pltpu.SemaphoreType.DMA((n_sems,)), *through,
                   jax.ShapeDtypeStruct((SUBLANES, SLAB), F32)),
        in_specs=[_HBM_SPEC] * (ns + nl),
        out_specs=(_SEM_SPEC, _SEM_SPEC, *([_HBM_SPEC] * (ns + nl)), pl.BlockSpec(memory_space=pltpu.VMEM)),
        input_output_aliases={i: 2 + i for i in range(ns + nl)},
        compiler_params=pltpu.CompilerParams(has_side_effects=_DATAFLOW),
    )(*[pltpu.with_memory_space_constraint(a, pltpu.HBM) for a in (*srcs, *lands)])
    return out[0], out[1], list(out[2:2 + ns]), list(out[2 + ns:2 + ns + nl]), out[-1]


def _split_copy_wait(send, recv, srcs, lands, plan, after, name):
    ns, nl, na = len(srcs), len(lands), len(after)

    def body(*refs):
        src_refs, land_refs = refs[:ns], refs[ns:ns + nl]
        send_ref, recv_ref = refs[ns + nl], refs[ns + nl + 1]
        sends, arrivals = plan(src_refs, land_refs)
        for src, dst, to, si, ri in sends:
            pltpu.make_async_remote_copy(src_ref=src, dst_ref=dst, send_sem=send_ref.at[si], recv_sem=recv_ref.at[ri],
                                         device_id=to, device_id_type=MESH).wait_send()
        for (src, _, to, si, _), (view, ri) in zip(sends, arrivals):
            pltpu.make_async_remote_copy(src_ref=view, dst_ref=view, send_sem=send_ref.at[si], recv_sem=recv_ref.at[ri],
                                         device_id=to, device_id_type=MESH).wait_recv()

    out = pl.pallas_call(
        body, name=name,
        out_shape=[pltpu.HBM(a.shape, a.dtype) for a in (*srcs, *lands)],
        in_specs=[_HBM_SPEC] * (ns + nl) + [_SEM_SPEC, _SEM_SPEC] + [pl.BlockSpec(memory_space=pl.ANY)] * na,
        out_specs=[_HBM_SPEC] * (ns + nl),
        input_output_aliases={i: i for i in range(ns + nl)},
        compiler_params=pltpu.CompilerParams(has_side_effects=_DATAFLOW),
    )(*srcs, *lands, send, recv, *after)
    return list(out[:ns]), list(out[ns:])


def _chip_exchange_plan(n):
    per = N_CHIPS - 1

    def plan(srcs, lands):
        x, y, c = _my_place()
        sends, arrivals = [], []
        for t in range(n):
            for j, (cx, cy) in enumerate(_other_chips(x, y)):
                sends.append((srcs[t].at[2 * cx + cy], lands[t].at[j], (cx, cy, c), t * per + j, t * per + j))
                arrivals.append((lands[t].at[j], t * per + j))
        return sends, arrivals

    return plan


def _gather_half_plan(n):
    per = N_CHIPS - 1

    def plan(srcs, lands):
        x, y, c = _my_place()
        k = 2 * x + y
        sends, arrivals = [], []
        for t in range(n):
            for j, (cx, cy) in enumerate(_other_chips(x, y)):
                sends.append((srcs[t].at[c], lands[t].at[k, c], (cx, cy, c), t * per + j, t * per + j))
                arrivals.append((lands[t].at[2 * cx + cy, c], t * per + j))
        return sends, arrivals

    return plan


def _gather_complete(own, landed, name):
    def body(own_ref, in_ref, out_ref, send, recv, local):
        x, y, c = _my_place()
        sib = (x, y, 1 - c)
        mine = pltpu.make_async_copy(own_ref, out_ref.at[2 * x + y], local)
        mine.start()
        sent = []
        for j, (cx, cy) in enumerate(_other_chips(x, y)):
            half = (2 * cx + cy, c)
            cp = pltpu.make_async_remote_copy(src_ref=in_ref.at[half], dst_ref=out_ref.at[half], send_sem=send.at[j],
                                              recv_sem=recv.at[j], device_id=sib, device_id_type=MESH)
            cp.start()
            sent.append(cp)
        for j, (cx, cy) in enumerate(_other_chips(x, y)):
            other = out_ref.at[2 * cx + cy, 1 - c]
            pltpu.make_async_remote_copy(src_ref=other, dst_ref=other, send_sem=send.at[j], recv_sem=recv.at[j],
                                         device_id=sib, device_id_type=MESH).wait_recv()
        for cp in sent:
            cp.wait_send()
        mine.wait()

    any_spec = pl.BlockSpec(memory_space=pl.ANY)
    per = N_CHIPS - 1
    return _pallas(
        body, name=name, in_specs=[any_spec, any_spec], out_specs=any_spec,
        out_shape=jax.ShapeDtypeStruct(landed.shape, landed.dtype), input_output_aliases={1: 0},
        scratch_shapes=[pltpu.SemaphoreType.DMA((per,)), pltpu.SemaphoreType.DMA((per,)), pltpu.SemaphoreType.DMA],
        compiler_params=_cp(16 * MIB),
    )(own, landed)


def _gather_direct_plan(n):
    per = 2 * (N_CHIPS - 1)

    def plan(srcs, lands):
        x, y, c = _my_place()
        k = 2 * x + y
        sends, arrivals = [], []
        for t in range(n):
            for j, (cx, cy) in enumerate(_other_chips(x, y)):
                for core in (0, 1):
                    sends.append((srcs[t].at[c], lands[t].at[k, c], (cx, cy, core),
                                  t * per + 2 * j + core, t * per + 2 * j + c))
                    arrivals.append((lands[t].at[2 * cx + cy, core], t * per + 2 * j + core))
        return sends, arrivals

    return plan


def _rs_chip_add(place, sums32, got, name):
    _, r2, c = sums32.shape
    tr = min(TR_ELT, r2)

    def body(place_ref, s_ref, q_ref, o_ref):
        del place_ref
        o_ref[...] = ((s_ref[...] + q_ref[0].astype(F32)) + q_ref[1].astype(F32)) + q_ref[2].astype(F32)

    return _pallas(
        body, name=name,
        grid_spec=pltpu.PrefetchScalarGridSpec(
            num_scalar_prefetch=1, grid=(r2 // tr,),
            in_specs=[pl.BlockSpec((None, tr, c), lambda i, p: (p[1], i, 0)),
                      pl.BlockSpec((N_CHIPS - 1, tr, c), lambda i, p: (0, i, 0))],
            out_specs=pl.BlockSpec((tr, c), lambda i, p: (i, 0))),
        out_shape=jax.ShapeDtypeStruct((r2, c), F32),
        compiler_params=_cp(2 * 14 * tr * c + 8 * MIB),
    )(place, sums32, got)


def _rs_pair_share(halves, name):
    n = len(halves)

    def body(*refs):
        ins, outs = refs[:n], refs[n:2 * n]
        send, recv, local = refs[2 * n:]
        x, y, c = _my_place()
        cps, mine = [], []
        for t in range(n):
            lc = pltpu.make_async_copy(ins[t], outs[t].at[c], local.at[t])
            lc.start()
            mine.append(lc)
            cp = pltpu.make_async_remote_copy(src_ref=ins[t], dst_ref=outs[t].at[c], send_sem=send.at[t],
                                              recv_sem=recv.at[t], device_id=(x, y, 1 - c), device_id_type=MESH)
            cp.start()
            cps.append(cp)
        for t in range(n):
            other = outs[t].at[1 - c]
            pltpu.make_async_remote_copy(src_ref=other, dst_ref=other, send_sem=send.at[t], recv_sem=recv.at[t],
                                         device_id=(x, y, 1 - c), device_id_type=MESH).wait_recv()
        for cp in cps:
            cp.wait_send()
        for lc in mine:
            lc.wait()

    any_spec = pl.BlockSpec(memory_space=pl.ANY)
    return _pallas(
        body, name=name, in_specs=[any_spec] * n, out_specs=[any_spec] * n,
        out_shape=[jax.ShapeDtypeStruct((2,) + h.shape, F32) for h in halves],
        scratch_shapes=[pltpu.SemaphoreType.DMA((n,)), pltpu.SemaphoreType.DMA((n,)), pltpu.SemaphoreType.DMA((n,))],
        compiler_params=_cp(16 * MIB),
    )(*halves)


_PACK_TILE = SUBLANES * SLAB


def _pack(arrays):
    rows = []
    for a in arrays:
        flat = a.reshape(-1).astype(F32)
        padded = -(-flat.shape[0] // _PACK_TILE) * _PACK_TILE
        rows.append(jnp.pad(flat, (0, padded - flat.shape[0])).reshape(-1, SLAB))
    n_rows = sum(r.shape[0] for r in rows)
    if n_rows % (2 * SUBLANES):
        rows.append(jnp.zeros((SUBLANES, SLAB), F32))
    return jnp.concatenate(rows, axis=0)


def _unpack(packed, shapes):
    out, row = [], 0
    for shp in shapes:
        size = math.prod(shp)
        nrow = -(-size // _PACK_TILE) * SUBLANES
        out.append(packed[row:row + nrow].reshape(-1)[:size].reshape(shp))
        row += nrow
    return out


def kernel(x, norm_pre_g, w_in, conv_w, conv_b, ssm_a_re, ssm_a_im, ssm_log_dt, ssm_b_re, ssm_b_im, ssm_c_re, ssm_c_im, ssm_d, w_glu, b_glu, w_out, norm_post_g, loss_target, m_norm_pre_g, m_w_in, m_conv_w, m_conv_b, m_ssm_a_re, m_ssm_a_im, m_ssm_log_dt, m_ssm_b_re, m_ssm_b_im, m_ssm_c_re, m_ssm_c_im, m_ssm_d, m_w_glu, m_b_glu, m_w_out, m_norm_post_g, v_norm_pre_g, v_w_in, v_conv_w, v_conv_b, v_ssm_a_re, v_ssm_a_im, v_ssm_log_dt, v_ssm_b_re, v_ssm_b_im, v_ssm_c_re, v_ssm_c_im, v_ssm_d, v_w_glu, v_b_glu, v_w_out, v_norm_post_g):
    xs, tgt = x[0], loss_target[0]
    t, d = xs.shape
    dc = conv_b.shape[0]
    dssm = ssm_d.shape[0]
    g, p = ssm_a_re.shape
    h = SSM_H
    nq = dssm // SLAB
    n_shard = w_in.shape[1]
    steps = min(T_SCAN, t) // SUBLANES
    mx, my, mc = _my_place()
    chip = 2 * mx + my
    place = jnp.stack([mc, chip]).astype(jnp.int32)

    cw_cols = conv_w.shape[1]
    cw_pad = -(-cw_cols // SLAB) * SLAB
    cw_blk = jnp.zeros((SUBLANES, cw_pad), F32).at[:conv_w.shape[0], :cw_cols].set(conv_w)
    cw_all = _allgather_flat(cw_blk, "allgather_conv_w")
    conv_w_full = jnp.concatenate([cw_all[2 * k, :, :cw_cols] for k in range(N_CHIPS)], axis=1)

    halves = lambda a: a.reshape(2, a.shape[0] // 2, a.shape[1])
    win_half = halves(_cast_bf16(w_in, "cast_w_in", cw_all[0, :, :SLAB]))
    win_plan = _gather_half_plan(1)
    win_send, win_recv, win_srcs, win_lands, win_token = _split_copy_start(
        [win_half], [((N_CHIPS,) + win_half.shape, BF16)], win_plan, N_CHIPS - 1, "gather_w_in_start")
    behind_w_in = win_token[0:1, 0:1]

    small_names = ["norm_pre_g", "conv_w", "conv_b", "ssm_a_re", "ssm_a_im", "ssm_log_dt", "ssm_b_re", "ssm_b_im",
                   "ssm_c_re", "ssm_c_im", "ssm_d", "b_glu", "norm_post_g", "loss"]
    zeros_cw = jnp.zeros((conv_w.shape[0], dc), F32)
    one0 = jnp.zeros((1,), F32)
    small_w = dict(norm_pre_g=norm_pre_g, conv_w=zeros_cw, conv_b=conv_b, ssm_a_re=ssm_a_re, ssm_a_im=ssm_a_im,
                   ssm_log_dt=ssm_log_dt, ssm_b_re=ssm_b_re, ssm_b_im=ssm_b_im, ssm_c_re=ssm_c_re, ssm_c_im=ssm_c_im,
                   ssm_d=ssm_d, b_glu=b_glu, norm_post_g=norm_post_g, loss=one0)
    small_m = dict(norm_pre_g=m_norm_pre_g, conv_w=zeros_cw, conv_b=m_conv_b, ssm_a_re=m_ssm_a_re, ssm_a_im=m_ssm_a_im,
                   ssm_log_dt=m_ssm_log_dt, ssm_b_re=m_ssm_b_re, ssm_b_im=m_ssm_b_im, ssm_c_re=m_ssm_c_re,
                   ssm_c_im=m_ssm_c_im, ssm_d=m_ssm_d, b_glu=m_b_glu, norm_post_g=m_norm_post_g, loss=one0)
    small_v = dict(norm_pre_g=v_norm_pre_g, conv_w=zeros_cw, conv_b=v_conv_b, ssm_a_re=v_ssm_a_re, ssm_a_im=v_ssm_a_im,
                   ssm_log_dt=v_ssm_log_dt, ssm_b_re=v_ssm_b_re, ssm_b_im=v_ssm_b_im, ssm_c_re=v_ssm_c_re,
                   ssm_c_im=v_ssm_c_im, ssm_d=v_ssm_d, b_glu=v_b_glu, norm_post_g=v_norm_post_g, loss=one0)
    w_pack, m_pack, v_pack = [_pack([group[nm] for nm in small_names]) + behind_w_in
                              for group in (small_w, small_m, small_v)]

    expand = jnp.repeat(jnp.eye(p, dtype=F32), h, axis=1)
    b_re2, b_im2 = ssm_b_re.reshape(g, p * h), ssm_b_im.reshape(g, p * h)
    log_dt2 = ssm_log_dt.reshape(g, 1) + behind_w_in
    lbr, lbi, bbr2, bbi2, pw3 = _zoh_fwd(ssm_a_re, ssm_a_im, log_dt2, b_re2, b_im2, expand, steps)
    lam = jnp.stack([lbr.reshape(g * p), lbi.reshape(g * p)])
    pw = pw3.reshape(2, g * p)
    to_slab_b = lambda b2: _blockdiag(b2.reshape(nq, GROUPS_PER_SLAB, p, h).transpose(0, 1, 3, 2))
    bq = jnp.concatenate([to_slab_b(bbr2), to_slab_b(bbi2)], axis=2).astype(BF16)
    to_slab_c = lambda c3: _blockdiag(c3.reshape(nq, GROUPS_PER_SLAB, h, p).transpose(0, 1, 3, 2))
    cq = jnp.concatenate([to_slab_c(ssm_c_re), to_slab_c(-ssm_c_im)], axis=1).astype(BF16)

    g_pre2, g_post2 = norm_pre_g.reshape(1, d), norm_post_g.reshape(1, d)
    conv_b2, b_glu2, d_skip2 = conv_b.reshape(1, dc), b_glu.reshape(1, dssm), ssm_d.reshape(1, dssm)
    u_block = 4 * dc // dssm

    hb = _prenorm(xs, g_pre2 + behind_w_in)
    proj_own = _inproj_own(place, hb, win_srcs[0].reshape(d, n_shard))
    win_own, win_landed = _split_copy_wait(win_send, win_recv, win_srcs, win_lands, win_plan,
                                           [proj_own, bq, cq, w_pack, m_pack, v_pack], "gather_w_in_wait")
    win_g = _gather_complete(win_own[0], win_landed[0], "gather_w_in_complete")
    win_b = win_g.reshape(N_CHIPS, d, n_shard)
    gathered = win_g[0, 0, :SUBLANES, :SLAB].astype(F32)
    side = [halves(_cast_bf16(w_out, "cast_w_out", gathered)), halves(_cast_bf16(w_glu, "cast_w_glu", gathered))]
    side_plan = _gather_direct_plan(len(side))
    side_sems = 2 * (N_CHIPS - 1) * len(side)
    ag_send, ag_recv, ag_srcs, ag_lands, ag_token = _split_copy_start(
        side, [((N_CHIPS,) + a.shape, BF16) for a in side], side_plan, side_sems, "gather_side_weights_start")
    proj = _inproj_rest(place, hb, win_b, proj_own, ag_token)
    y, cin = _ssm_fwd(proj, u_block, bq, cq, lam, pw, d_skip2)
    side_own, side_all = _split_copy_wait(ag_send, ag_recv, ag_srcs, ag_lands, side_plan, [cin],
                                          "gather_side_weights_wait")
    wout_g, wglu_g = [lax.dynamic_update_index_in_dim(all_, own, chip, 0) for own, all_ in zip(side_own, side_all)]
    wout_b = wout_g.reshape(dc + dssm, d)
    wglu_b = wglu_g.reshape(dssm, dssm)
    mix = _gate_fwd(proj, y, conv_w_full, conv_b2, wglu_b, b_glu2, dc)
    loss_blk, dout, dob, dmix, gg_post = _outproj(mix, wout_b, xs, tgt, g_post2)

    def reduce_start(grads, to_send, tags, group):
        got = _rs_pair_exchange(to_send, "rs_pair_exchange_" + group)
        sums = [_rs_pair_add(place, gt, rt, "rs_pair_add_" + tg) for gt, rt, tg in zip(grads, got, tags)]
        plan = _chip_exchange_plan(len(grads))
        started = _split_copy_start([s16 for s16, _ in sums],
                                    [((N_CHIPS - 1,) + s16.shape[1:], BF16) for s16, _ in sums], plan,
                                    (N_CHIPS - 1) * len(grads), "rs_chip_exchange_" + group + "_start")
        return plan, started, [s32 for _, s32 in sums]

    def reduce_finish(plan, started, sums32, tags, group, after):
        send, recv, srcs, lands, _ = started
        _, landed = _split_copy_wait(send, recv, srcs, lands, plan, after, "rs_chip_exchange_" + group + "_wait")
        mine = [_rs_chip_add(place, s32, qt, "rs_chip_add_" + tg) for s32, qt, tg in zip(sums32, landed, tags)]
        full = _rs_pair_share(mine, "rs_pair_share_" + group)
        return [f.reshape(2 * f.shape[1], f.shape[2]) for f in full]

    gw_out, gw_out16 = _matmul_tn(mix, dob, 1, "grad_w_out")
    dproj, dy, gsmall, gw_glu = _gate_bwd(proj, y, dmix, conv_w_full, conv_b2, wglu_b, b_glu2, dc)
    as_out = lambda a: a.reshape(N_CHIPS, 2, (dc + dssm) // (2 * N_CHIPS), d)
    gw_glu4 = gw_glu.reshape(N_CHIPS, 2, dssm // (2 * N_CHIPS), dssm)
    rs_a = reduce_start([as_out(gw_out), gw_glu4], [as_out(gw_out16), gw_glu4], ["w_out", "w_glu"], "a")
    dproj, gb_dense, gc_dense, glam, gd = _ssm_bwd(proj, u_block, dy, cin, dproj, bq, cq, lam, pw,
                                                   d_skip2 + rs_a[1][4][0:1, 0:1])
    gw_in, gw_in16 = _matmul_tn(hb, dproj, N_CHIPS, "grad_w_in")
    as_in = lambda a: a.reshape(N_CHIPS, 2, d // 2, n_shard)
    rs_b = reduce_start([as_in(gw_in)], [as_in(gw_in16)], ["w_in"], "b")
    gx, gg_pre = _dh_prenorm_bwd(dproj, win_b, xs, dout, g_pre2 + rs_b[1][4][0:1, 0:1])

    gb4 = gb_dense.reshape(nq, SLAB, 2, SLAB_STATES)
    g_bbr2 = _blockdiag_take(gb4[:, :, 0, :], h, p).transpose(0, 1, 3, 2).reshape(g, p * h)
    g_bbi2 = _blockdiag_take(gb4[:, :, 1, :], h, p).transpose(0, 1, 3, 2).reshape(g, p * h)
    gc4 = gc_dense.reshape(nq, SLAB, 2, SLAB_STATES)
    g_c_re = _blockdiag_take(gc4[:, :, 0, :], h, p).reshape(g, h, p)
    g_c_im = -_blockdiag_take(gc4[:, :, 1, :], h, p).reshape(g, h, p)
    g_a_re, g_a_im, g_ld, g_b_re2, g_b_im2 = _zoh_bwd(
        ssm_a_re, ssm_a_im, log_dt2, b_re2, b_im2, expand,
        glam[0].reshape(g, p), glam[1].reshape(g, p), g_bbr2, g_bbi2)

    small_g = {
        "norm_pre_g": gg_pre[0], "conv_w": gsmall[0:3], "conv_b": gsmall[3], "ssm_a_re": g_a_re, "ssm_a_im": g_a_im,
        "ssm_log_dt": g_ld.reshape(g), "ssm_b_re": g_b_re2.reshape(g, p, h), "ssm_b_im": g_b_im2.reshape(g, p, h),
        "ssm_c_re": g_c_re, "ssm_c_im": g_c_im, "ssm_d": gd[0], "b_glu": gsmall[4], "norm_post_g": gg_post[0],
        "loss": loss_blk[0, 0:1],
    }
    shapes = [small_w[nm].shape for nm in small_names]
    g_pack = _allreduce_small(_pack([small_g[nm] for nm in small_names]))
    packs = _adamw_small(g_pack, w_pack, m_pack, v_pack)
    sg, sd, sm, sv = [dict(zip(small_names, _unpack(pk, shapes))) for pk in (g_pack, *packs)]
    loss = sg["loss"][0]

    g_cw = lax.dynamic_slice_in_dim(sg["conv_w"], chip * cw_cols, cw_cols, axis=1)
    pad_cw = lambda a: jnp.zeros((SUBLANES, cw_pad), F32).at[:a.shape[0], :cw_cols].set(a)
    cut_cw = lambda a: a[:conv_w.shape[0], :cw_cols]
    d_cw, m_cw, v_cw = [cut_cw(a) for a in _adamw(pad_cw(conv_w), pad_cw(g_cw), pad_cw(m_conv_w), pad_cw(v_conv_w),
                                                  "adamw_conv_w")]

    g_wout, g_wglu = reduce_finish(*rs_a, ["w_out", "w_glu"], "a", [d_cw, packs[0]])
    (g_win,) = reduce_finish(*rs_b, ["w_in"], "b", [g_wout])
    d_win, m_win, v_win = _adamw(w_in, g_win, m_w_in, v_w_in, "adamw_w_in")
    d_wout, m_wout, v_wout = _adamw(w_out, g_wout, m_w_out, v_w_out, "adamw_w_out")
    d_wglu, m_wglu, v_wglu = _adamw(w_glu, g_wglu, m_w_glu, v_w_glu, "adamw_w_glu")

    order = ["norm_pre_g", "w_in", "conv_w", "conv_b", "ssm_a_re", "ssm_a_im", "ssm_log_dt", "ssm_b_re", "ssm_b_im",
             "ssm_c_re", "ssm_c_im", "ssm_d", "w_glu", "b_glu", "w_out", "norm_post_g"]
    grads, deltas, new_m, new_v = dict(sg), dict(sd), dict(sm), dict(sv)
    grads.update(w_in=g_win, w_out=g_wout, w_glu=g_wglu, conv_w=g_cw)
    deltas.update(w_in=d_win, w_out=d_wout, w_glu=d_wglu, conv_w=d_cw)
    new_m.update(w_in=m_win, w_out=m_wout, w_glu=m_wglu, conv_w=m_cw)
    new_v.update(w_in=v_win, w_out=v_wout, w_glu=v_wglu, conv_w=v_cw)
    return (loss, gx[None], *[grads[nm] for nm in order], *[deltas[nm] for nm in order],
            *[new_m[nm] for nm in order], *[new_v[nm] for nm in order])
```

```python
import functools
import math

import jax
import jax.numpy as jnp
from jax import lax
from jax.experimental import pallas as pl
from jax.experimental.pallas import tpu as pltpu

F32 = jnp.float32
BF16 = jnp.bfloat16
MESH = pl.DeviceIdType.MESH

EPS = 1e-6
SSM_H = 16
SSM_P = 64
GROUPS_PER_SLAB = 8
SLAB = 128
SLAB_STATES = GROUPS_PER_SLAB * SSM_P
N_CHIPS = 4
N_DEV = 8

ADAM_LR = 0.001
ADAM_B1 = 0.9
ADAM_B2 = 0.999
ADAM_EPS = 1e-08
ADAM_WD = 0.01
ADAM_STEP = 10

MIB = 1024 * 1024
VMEM_CAP = 48 * MIB
SUBLANES = 8

TM_NORM = 512
TM_PROJ = 512
TM_GATE = 256
TM_OUT = 256
TM_DH = 256
T_SCAN = 256
TK_TN = 512
SCAN_UNROLL = 4
TM_TN = 1024
TR_ELT = 256


def _cp(vmem_bytes, **kw):
    return pltpu.CompilerParams(vmem_limit_bytes=int(min(VMEM_CAP, max(16 * MIB, vmem_bytes))), **kw)


def _pallas(body, **kw):
    if "grid" not in kw and "grid_spec" not in kw:
        return pl.pallas_call(body, **kw)
    pin = lambda s: pltpu.HBM(s.shape, s.dtype) if isinstance(s, jax.ShapeDtypeStruct) else s
    out_shape = kw.pop("out_shape")
    out_shape = [pin(s) for s in out_shape] if isinstance(out_shape, (list, tuple)) else pin(out_shape)
    call = pl.pallas_call(body, out_shape=out_shape, **kw)

    def run(*args):
        return call(*[pltpu.with_memory_space_constraint(a, pltpu.HBM) if jnp.issubdtype(a.dtype, jnp.floating) else a
                      for a in args])

    return run


def _my_place():
    return lax.axis_index("x"), lax.axis_index("y"), lax.axis_index("c")


def _other_chips(x, y):
    return [(1 - x, y), (x, 1 - y), (1 - x, 1 - y)]


def _silu(z):
    s = jax.nn.sigmoid(z)
    return z * s, s


def _dsilu(z, s):
    return s * (1.0 + z * (1.0 - s))


_GELU_K = math.sqrt(2.0 / math.pi)
_GELU_C = 0.044715


def _gelu(y):
    th = jnp.tanh(_GELU_K * (y + _GELU_C * y * y * y))
    return 0.5 * y * (1.0 + th), th


def _dgelu(y, th):
    return 0.5 * (1.0 + th) + 0.5 * y * (1.0 - th * th) * _GELU_K * (1.0 + 3.0 * _GELU_C * y * y)


def _cast_bf16(w, name, after=None):
    r, c = w.shape
    tr = min(TR_ELT, r)
    extra = [] if after is None else [after]

    def body(w_ref, *rest):
        rest[-1][...] = w_ref[...].astype(BF16)

    return _pallas(
        body, name=name, grid=(r // tr,),
        in_specs=[pl.BlockSpec((tr, c), lambda i: (i, 0))] + [pl.BlockSpec((SUBLANES, SLAB), lambda i: (0, 0))] * len(extra),
        out_specs=pl.BlockSpec((tr, c), lambda i: (i, 0)),
        out_shape=jax.ShapeDtypeStruct((r, c), BF16),
        compiler_params=_cp(12 * tr * c),
    )(w, *extra)


def _prenorm(x, g):
    t, d = x.shape
    tm = min(TM_NORM, t)

    def body(x_ref, g_ref, h_ref):
        xv = x_ref[...]
        r = lax.rsqrt(jnp.mean(xv * xv, axis=-1, keepdims=True) + EPS)
        h_ref[...] = (xv * r * g_ref[...]).astype(BF16)

    return _pallas(
        body, name="prenorm", grid=(t // tm,),
        in_specs=[pl.BlockSpec((tm, d), lambda i: (i, 0)), pl.BlockSpec((1, d), lambda i: (0, 0))],
        out_specs=pl.BlockSpec((tm, d), lambda i: (i, 0)),
        out_shape=jax.ShapeDtypeStruct((t, d), BF16),
        compiler_params=_cp(20 * tm * d),
    )(x, g)


def _inproj_own(place, a, w_own):
    t, k = a.shape
    n = w_own.shape[1]
    tm = min(TM_PROJ, t)

    def body(place_ref, a_ref, b_ref, o_ref):
        del place_ref
        o_ref[...] = jnp.dot(a_ref[...], b_ref[...], preferred_element_type=F32)

    return _pallas(
        body, name="inproj_own",
        grid_spec=pltpu.PrefetchScalarGridSpec(
            num_scalar_prefetch=1, grid=(t // tm,),
            in_specs=[pl.BlockSpec((tm, k), lambda i, p: (i, 0)), pl.BlockSpec((k, n), lambda i, p: (0, 0))],
            out_specs=pl.BlockSpec((tm, n), lambda i, p: (i, p[1]))),
        out_shape=jax.ShapeDtypeStruct((t, N_CHIPS * n), F32),
        compiler_params=_cp(2 * (2 * tm * k + 2 * k * n + 4 * tm * n) + 4 * MIB),
    )(place, a, w_own)


def _inproj_rest(place, a, b, partial, after):
    t, k = a.shape
    nb, _, n = b.shape
    tm = min(TM_PROJ, t)
    shard = lambda s, p: (p[1] + 1 + s) % nb

    def body(place_ref, a_ref, b_ref, after_ref, partial_ref, o_ref):
        del place_ref, after_ref, partial_ref
        o_ref[...] = jnp.dot(a_ref[...], b_ref[...], preferred_element_type=F32)

    return _pallas(
        body, name="inproj_rest",
        grid_spec=pltpu.PrefetchScalarGridSpec(
            num_scalar_prefetch=1, grid=(nb - 1, t // tm),
            in_specs=[pl.BlockSpec((tm, k), lambda s, i, p: (i, 0)),
                      pl.BlockSpec((None, k, n), lambda s, i, p: (shard(s, p), 0, 0)),
                      pl.BlockSpec((SUBLANES, SLAB), lambda s, i, p: (0, 0)),
                      pl.BlockSpec(memory_space=pl.ANY)],
            out_specs=pl.BlockSpec((tm, n), lambda s, i, p: (i, shard(s, p)))),
        out_shape=jax.ShapeDtypeStruct((t, nb * n), F32),
        input_output_aliases={4: 0},
        compiler_params=_cp(2 * (2 * tm * k + 2 * k * n + 4 * tm * n) + 4 * MIB),
    )(place, a, b, after, partial)


def _matmul_tn(a, b, nb, name):
    t, m = a.shape
    n = b.shape[1] // nb
    tk = min(TK_TN, t)
    tma = min(TM_TN, m)

    def body(a_ref, b_ref, o_ref, o16_ref):
        k = pl.program_id(2)

        @pl.when(k == 0)
        def _():
            o_ref[...] = jnp.zeros_like(o_ref)

        o_ref[...] += lax.dot_general(a_ref[...], b_ref[...], (((0,), (0,)), ((), ())), preferred_element_type=F32)

        @pl.when(k == t // tk - 1)
        def _():
            o16_ref[...] = o_ref[...].astype(BF16)

    blk = pl.BlockSpec((None, tma, n), lambda j, i, k: (j, i, 0))
    return _pallas(
        body, name=name, grid=(nb, m // tma, t // tk),
        in_specs=[pl.BlockSpec((tk, tma), lambda j, i, k: (k, i)), pl.BlockSpec((tk, n), lambda j, i, k: (k, j))],
        out_specs=[blk, blk],
        out_shape=[jax.ShapeDtypeStruct((nb, m, n), F32), jax.ShapeDtypeStruct((nb, m, n), BF16)],
        compiler_params=_cp(2 * (2 * tk * tma + 2 * tk * n + 6 * tma * n) + 8 * MIB),
    )(a, b)


def _outproj(mix, w_out, x, tgt, g_post):
    t, dm = mix.shape
    d = w_out.shape[1]
    tm = min(TM_OUT, t)

    def body(mix_ref, w_ref, x_ref, t_ref, g_ref, loss_ref, dout_ref, do_ref, dmix_ref, gg_ref):
        @pl.when(pl.program_id(0) == 0)
        def _():
            loss_ref[...] = jnp.zeros_like(loss_ref)
            gg_ref[...] = jnp.zeros_like(gg_ref)

        w = w_ref[...]
        o = jnp.dot(mix_ref[...], w, preferred_element_type=F32)
        r = lax.rsqrt(jnp.mean(o * o, axis=-1, keepdims=True) + EPS)
        nh = o * r
        g = g_ref[...]
        e = x_ref[...] + nh * g - t_ref[...]
        loss_ref[...] += jnp.sum(e * e) * (0.5 / d)
        dout = e * (1.0 / d)
        dout_ref[...] = dout
        gg_ref[0:1, :] += jnp.sum(dout * nh, axis=0, keepdims=True)
        dn = dout * g
        do = r * (dn - nh * jnp.mean(dn * nh, axis=-1, keepdims=True))
        dob = do.astype(BF16)
        do_ref[...] = dob
        dmix_ref[...] = lax.dot_general(dob, w, (((1,), (1,)), ((), ())), preferred_element_type=F32)

    row = lambda i: (i, 0)
    fixed = lambda i: (0, 0)
    return _pallas(
        body, name="outproj", grid=(t // tm,),
        in_specs=[pl.BlockSpec((tm, dm), row), pl.BlockSpec((dm, d), fixed), pl.BlockSpec((tm, d), row),
                  pl.BlockSpec((tm, d), row), pl.BlockSpec((1, d), fixed)],
        out_specs=[pl.BlockSpec((SUBLANES, SLAB), fixed), pl.BlockSpec((tm, d), row), pl.BlockSpec((tm, d), row),
                   pl.BlockSpec((tm, dm), row), pl.BlockSpec((SUBLANES, d), fixed)],
        out_shape=[jax.ShapeDtypeStruct((SUBLANES, SLAB), F32), jax.ShapeDtypeStruct((t, d), F32),
                   jax.ShapeDtypeStruct((t, d), BF16), jax.ShapeDtypeStruct((t, dm), F32),
                   jax.ShapeDtypeStruct((SUBLANES, d), F32)],
        compiler_params=_cp(2 * (2 * dm * d + tm * (2 * dm + 4 * d * 3 + 2 * d + 4 * dm)) + 16 * MIB),
    )(mix, w_out, x, tgt, g_post)


def _dh_prenorm_bwd(dproj, w_in, x, dout, g_pre):
    t, d = x.shape
    nb, _, n = w_in.shape
    tm = min(TM_DH, t)

    def body(dp_ref, w_ref, x_ref, dout_ref, g_ref, dx_ref, gg_ref):
        @pl.when(pl.program_id(0) == 0)
        def _():
            gg_ref[...] = jnp.zeros_like(gg_ref)

        dh = None
        for k in range(nb):
            part = lax.dot_general(dp_ref[:, k * n:(k + 1) * n], w_ref[k], (((1,), (1,)), ((), ())),
                                   preferred_element_type=F32)
            dh = part if dh is None else dh + part
        xv = x_ref[...]
        r = lax.rsqrt(jnp.mean(xv * xv, axis=-1, keepdims=True) + EPS)
        xh = xv * r
        gg_ref[0:1, :] += jnp.sum(dh * xh, axis=0, keepdims=True)
        dg = dh * g_ref[...]
        dx_ref[...] = dout_ref[...] + r * (dg - xh * jnp.mean(dg * xh, axis=-1, keepdims=True))

    row = lambda i: (i, 0)
    fixed = lambda i: (0, 0)
    w_spec = pl.BlockSpec(w_in.shape, lambda i: (0, 0, 0), pipeline_mode=pl.Buffered(1))
    return _pallas(
        body, name="dh_prenorm_bwd", grid=(t // tm,),
        in_specs=[pl.BlockSpec((tm, nb * n), row), w_spec,
                  pl.BlockSpec((tm, d), row), pl.BlockSpec((tm, d), row), pl.BlockSpec((1, d), fixed)],
        out_specs=[pl.BlockSpec((tm, d), row), pl.BlockSpec((SUBLANES, d), fixed)],
        out_shape=[jax.ShapeDtypeStruct((t, d), F32), jax.ShapeDtypeStruct((SUBLANES, d), F32)],
        compiler_params=_cp(2 * nb * d * n + 2 * (2 * tm * nb * n + 12 * tm * d) + 16 * tm * d + 4 * MIB),
    )(dproj, w_in, x, dout, g_pre)


def _adamw(w, g, m, v, name):
    r, c = w.shape
    tr = min(TR_ELT, r)
    c1 = 1.0 - ADAM_B1 ** ADAM_STEP
    c2 = 1.0 - ADAM_B2 ** ADAM_STEP

    def body(w_ref, g_ref, m_ref, v_ref, d_ref, mo_ref, vo_ref):
        gv = g_ref[...]
        mn = ADAM_B1 * m_ref[...] + (1.0 - ADAM_B1) * gv
        vn = ADAM_B2 * v_ref[...] + (1.0 - ADAM_B2) * (gv * gv)
        d_ref[...] = -ADAM_LR * ((mn / c1) / (jnp.sqrt(vn / c2) + ADAM_EPS) + ADAM_WD * w_ref[...])
        mo_ref[...] = mn
        vo_ref[...] = vn

    spec = pl.BlockSpec((tr, c), lambda i: (i, 0))
    sds = jax.ShapeDtypeStruct((r, c), F32)
    return _pallas(
        body, name=name, grid=(r // tr,), in_specs=[spec] * 4, out_specs=[spec] * 3, out_shape=[sds] * 3,
        compiler_params=_cp(2 * 7 * 4 * tr * c + 8 * MIB),
    )(w, g, m, v)


def _adamw_small(g, w, m, v):
    r, c = w.shape
    c1 = 1.0 - ADAM_B1 ** ADAM_STEP
    c2 = 1.0 - ADAM_B2 ** ADAM_STEP

    def body(g_ref, w_ref, m_ref, v_ref, d_ref, mo_ref, vo_ref):
        gv = g_ref[...]
        mn = ADAM_B1 * m_ref[...] + (1.0 - ADAM_B1) * gv
        vn = ADAM_B2 * v_ref[...] + (1.0 - ADAM_B2) * (gv * gv)
        d_ref[...] = -ADAM_LR * ((mn / c1) / (jnp.sqrt(vn / c2) + ADAM_EPS) + ADAM_WD * w_ref[...])
        mo_ref[...] = mn
        vo_ref[...] = vn

    sds = jax.ShapeDtypeStruct((r, c), F32)
    return _pallas(
        body, name="adamw_small", out_shape=[sds] * 3,
        compiler_params=_cp(12 * 4 * r * c + 8 * MIB),
    )(g, w, m, v)


def _allreduce_small(v):
    r, c = v.shape
    assert r % (2 * SUBLANES) == 0
    h = r // 2

    def body(v_ref, out_ref, got_ref, chip_ref, slots_ref, send, recv):
        x, y, cc = _my_place()
        k = 2 * x + y
        sib = (x, y, 1 - cc)

        def rc(s, src, dst, to):
            return pltpu.make_async_remote_copy(src_ref=src, dst_ref=dst, send_sem=send.at[s], recv_sem=recv.at[s],
                                                device_id=to, device_id_type=MESH)

        pair = rc(0, v_ref, got_ref, sib)
        pair.start()
        pair.wait()
        chip_ref[...] = v_ref[...] + got_ref[...]
        mine = pl.ds(pl.multiple_of(cc * h, SUBLANES), h)
        theirs = pl.ds(pl.multiple_of((1 - cc) * h, SUBLANES), h)
        chips = _other_chips(x, y)
        sent = []
        for j, (cx, cy) in enumerate(chips):
            cp = rc(1 + j, chip_ref.at[mine], slots_ref.at[k], (cx, cy, cc))
            cp.start()
            sent.append(cp)
        slots_ref[k] = chip_ref[mine, :]
        for j, (cx, cy) in enumerate(chips):
            rc(1 + j, chip_ref.at[mine], slots_ref.at[2 * cx + cy], (cx, cy, cc)).wait_recv()
        total = slots_ref[0]
        for kk in range(1, N_CHIPS):
            total = total + slots_ref[kk]
        out_ref[mine, :] = total
        for cp in sent:
            cp.wait_send()
        share = rc(N_CHIPS, out_ref.at[mine], out_ref.at[mine], sib)
        share.start()
        rc(N_CHIPS, out_ref.at[theirs], out_ref.at[theirs], sib).wait_recv()
        share.wait_send()

    vm = pl.BlockSpec(memory_space=pltpu.VMEM)
    return _pallas(
        body, name="allreduce_small", out_shape=jax.ShapeDtypeStruct((r, c), F32), in_specs=[vm], out_specs=vm,
        scratch_shapes=[pltpu.VMEM((r, c), F32), pltpu.VMEM((r, c), F32), pltpu.VMEM((N_CHIPS, h, c), F32),
                        pltpu.SemaphoreType.DMA((N_CHIPS + 1,)), pltpu.SemaphoreType.DMA((N_CHIPS + 1,))],
        compiler_params=_cp(6 * 4 * r * c + 8 * MIB),
    )(v)


def _zoh(a_re, a_im, log_dt, b_re2, b_im2, expand):
    dt = jnp.exp(log_dt)
    mag = jnp.exp(a_re * dt)
    lbr, lbi = mag * jnp.cos(a_im * dt), mag * jnp.sin(a_im * dt)
    nr, ni = lbr - 1.0, lbi
    den = a_re * a_re + a_im * a_im
    qr = (nr * a_re + ni * a_im) / den
    qi = (ni * a_re - nr * a_im) / den
    qr2 = jnp.dot(qr, expand, precision=lax.Precision.HIGHEST, preferred_element_type=F32)
    qi2 = jnp.dot(qi, expand, precision=lax.Precision.HIGHEST, preferred_element_type=F32)
    return lbr, lbi, qr2 * b_re2 - qi2 * b_im2, qr2 * b_im2 + qi2 * b_re2


def _zoh_fwd(a_re, a_im, log_dt, b_re2, b_im2, expand, power):
    g, p = a_re.shape

    def body(ar_ref, ai_ref, ld_ref, br_ref, bi_ref, e_ref, lbr_ref, lbi_ref, bbr_ref, bbi_ref, pw_ref):
        ar, ai, ld = ar_ref[...], ai_ref[...], ld_ref[...]
        lbr, lbi, bbr, bbi = _zoh(ar, ai, ld, br_ref[...], bi_ref[...], e_ref[...])
        lbr_ref[...], lbi_ref[...], bbr_ref[...], bbi_ref[...] = lbr, lbi, bbr, bbi
        dt = jnp.exp(ld) * float(power)
        mag = jnp.exp(ar * dt)
        pw_ref[0] = mag * jnp.cos(ai * dt)
        pw_ref[1] = mag * jnp.sin(ai * dt)

    gp = jax.ShapeDtypeStruct((g, p), F32)
    gph = jax.ShapeDtypeStruct(b_re2.shape, F32)
    return _pallas(
        body, name="zoh_fwd", out_shape=[gp, gp, gph, gph, jax.ShapeDtypeStruct((2, g, p), F32)],
        compiler_params=_cp(16 * MIB),
    )(a_re, a_im, log_dt, b_re2, b_im2, expand)


def _zoh_bwd(a_re, a_im, log_dt, b_re2, b_im2, expand, g_lbr, g_lbi, g_bbr, g_bbi):
    def body(ar_ref, ai_ref, ld_ref, br_ref, bi_ref, e_ref, c0, c1, c2, c3, gar, gai, gld, gbr, gbi):
        e = e_ref[...]
        _, vjp = jax.vjp(lambda a, b, c, d, f: _zoh(a, b, c, d, f, e),
                         ar_ref[...], ai_ref[...], ld_ref[...], br_ref[...], bi_ref[...])
        gar[...], gai[...], gld[...], gbr[...], gbi[...] = vjp((c0[...], c1[...], c2[...], c3[...]))

    sds = lambda a: jax.ShapeDtypeStruct(a.shape, F32)
    return _pallas(
        body, name="zoh_bwd", out_shape=[sds(a_re), sds(a_im), sds(log_dt), sds(b_re2), sds(b_im2)],
        compiler_params=_cp(16 * MIB),
    )(a_re, a_im, log_dt, b_re2, b_im2, expand, g_lbr, g_lbi, g_bbr, g_bbi)


def _blockdiag(blocks):
    nq, _, r, c = blocks.shape
    eye = jnp.eye(GROUPS_PER_SLAB, dtype=blocks.dtype)
    out = blocks[:, :, :, None, :] * eye[None, :, None, :, None]
    return out.reshape(nq, GROUPS_PER_SLAB * r, GROUPS_PER_SLAB * c)


def _blockdiag_take(dense, r, c):
    nq = dense.shape[0]
    d5 = dense.reshape(nq, GROUPS_PER_SLAB, r, GROUPS_PER_SLAB, c)
    return jnp.stack([d5[:, i, :, i, :] for i in range(GROUPS_PER_SLAB)], axis=1)


def _scan_slab(s_ref, row0, q, lam_ref, pw_ref, car_ref, steps, reverse, prev_ref=None, prev_row0=0, glam_ref=None):
    sign = -1.0 if reverse else 1.0
    half = SLAB_STATES // SLAB
    cols = [(q * 2 * half + m, q * 2 * half + half + m, q * SLAB_STATES + m * SLAB) for m in range(half)]
    nm = len(cols)
    full = (SUBLANES, SLAB)
    lam = [(jnp.broadcast_to(lam_ref[0:1, pl.ds(cl, SLAB)], full),
            jnp.broadcast_to(sign * lam_ref[1:2, pl.ds(cl, SLAB)], full)) for (_, _, cl) in cols]

    def step_rows(jj, base):
        j = (steps - 1 - jj) if reverse else jj
        return j, pl.ds(pl.multiple_of(base + j * SUBLANES, SUBLANES), SUBLANES)

    def pass1(jj, car):
        _, rows = step_rows(jj, row0)
        out = []
        for m, (cr, ci, _) in enumerate(cols):
            sr, si = car[2 * m], car[2 * m + 1]
            lr, li = lam[m]
            nr = lr * sr - li * si + s_ref[cr, rows, :]
            ni = lr * si + li * sr + s_ref[ci, rows, :]
            s_ref[cr, rows, :] = nr
            s_ref[ci, rows, :] = ni
            out += [nr, ni]
        return tuple(out)

    def unrolled(step_fn):
        def outer(jo, carry):
            for k in range(SCAN_UNROLL):
                carry = step_fn(jo * SCAN_UNROLL + k, carry)
            return carry
        return outer

    assert steps % SCAN_UNROLL == 0
    ends = lax.fori_loop(0, steps // SCAN_UNROLL, unrolled(pass1), tuple(jnp.zeros(full, F32) for _ in range(2 * nm)))

    entry = []
    for m, (cr, ci, cl) in enumerate(cols):
        ljr = pw_ref[0:1, pl.ds(cl, SLAB)]
        lji = sign * pw_ref[1:2, pl.ds(cl, SLAB)]
        c_r = car_ref[cr, 0:1, :]
        c_i = car_ref[ci, 0:1, :]
        rows_r, rows_i = [None] * SUBLANES, [None] * SUBLANES
        order = range(SUBLANES - 1, -1, -1) if reverse else range(SUBLANES)
        for b in order:
            rows_r[b], rows_i[b] = c_r, c_i
            e_r, e_i = ends[2 * m][b:b + 1], ends[2 * m + 1][b:b + 1]
            c_r, c_i = ljr * c_r - lji * c_i + e_r, ljr * c_i + lji * c_r + e_i
        car_ref[cr, 0:1, :] = c_r
        car_ref[ci, 0:1, :] = c_i
        entry.append((jnp.concatenate(rows_r, axis=0), jnp.concatenate(rows_i, axis=0)))

    def pass2(jj, carry):
        j, rows = step_rows(jj, row0)
        decayed, acc = carry[:2 * nm], carry[2 * nm:]
        out_d, out_a = [], []
        for m, (cr, ci, cl) in enumerate(cols):
            lr, li = lam[m]
            dr, di = decayed[2 * m], decayed[2 * m + 1]
            dr, di = lr * dr - li * di, lr * di + li * dr
            nr = s_ref[cr, rows, :] + dr
            ni = s_ref[ci, rows, :] + di
            s_ref[cr, rows, :] = nr
            s_ref[ci, rows, :] = ni
            out_d += [dr, di]
            if prev_ref is not None:
                prow = pl.ds(pl.multiple_of(prev_row0 + (j - 1) * SUBLANES, SUBLANES), SUBLANES)
                qr = prev_ref[cr, prow, :]
                qi = prev_ref[ci, prow, :]
                out_a += [acc[2 * m] + (nr * qr + ni * qi), acc[2 * m + 1] + (ni * qr - nr * qi)]
        return tuple(out_d) + tuple(out_a)

    n_acc = 2 * nm if prev_ref is not None else 0
    init = tuple(e for pair in entry for e in pair) + tuple(jnp.zeros(full, F32) for _ in range(n_acc))
    accs = lax.fori_loop(0, steps // SCAN_UNROLL, unrolled(pass2), init)[2 * nm:]
    if prev_ref is not None:
        for m, (_, _, cl) in enumerate(cols):
            glam_ref[0:1, pl.ds(cl, SLAB)] += jnp.sum(accs[2 * m], axis=0, keepdims=True)
            glam_ref[1:2, pl.ds(cl, SLAB)] += jnp.sum(accs[2 * m + 1], axis=0, keepdims=True)


def _permute_rows_f32(perm_bf16, v):
    hi = v.astype(BF16)
    lo = (v - hi.astype(F32)).astype(BF16)
    return (jnp.dot(perm_bf16, hi, preferred_element_type=F32) + jnp.dot(perm_bf16, lo, preferred_element_type=F32))


def _put_slab(s_ref, rows, q, val):
    per = 2 * SLAB_STATES // SLAB
    for i in range(per):
        s_ref[q * per + i, rows, :] = val[:, i * SLAB:(i + 1) * SLAB]


def _get_slab(s_ref, rows, q):
    per = 2 * SLAB_STATES // SLAB
    return jnp.concatenate([s_ref[q * per + i, rows, :] for i in range(per)], axis=1)


def _step_major_perm(tt):
    r = jnp.arange(tt)
    held = (r % SUBLANES) * (tt // SUBLANES) + r // SUBLANES
    return held[:, None] == r[None, :]


def _ssm_fwd(proj, u_block, bq, cq, lam, pw, d_skip):
    t = proj.shape[0]
    nq, ds, w2 = bq.shape
    assert ds == SLAB and w2 == 2 * SLAB_STATES
    dssm = nq * SLAB
    width = nq * w2
    ntile = width // SLAB
    tt = min(T_SCAN, t)
    steps = tt // SUBLANES
    perm = _step_major_perm(tt)
    pm, pmt = perm.astype(BF16), perm.T.astype(BF16)

    def body(u_ref, pm_ref, pmt_ref, bq_ref, cq_ref, lam_ref, pw_ref, d_ref, y_ref, cin_ref, s_ref, car_ref):
        @pl.when(pl.program_id(0) == 0)
        def _():
            car_ref[...] = jnp.zeros_like(car_ref)

        cin_ref[...] = jnp.broadcast_to(car_ref[:, 0:1, :], cin_ref.shape)
        u = u_ref[...]
        ub = jnp.dot(pm_ref[...], u.astype(BF16), preferred_element_type=F32).astype(BF16)
        everything = slice(None)
        for q in range(nq):
            _put_slab(s_ref, everything, q,
                      jnp.dot(ub[:, q * SLAB:(q + 1) * SLAB], bq_ref[q], preferred_element_type=F32))
        for q in range(nq):
            _scan_slab(s_ref, 0, q, lam_ref, pw_ref, car_ref, steps, reverse=False)
        y_sm = jnp.concatenate(
            [jnp.dot(_get_slab(s_ref, everything, q).astype(BF16), cq_ref[q], preferred_element_type=F32)
             for q in range(nq)], axis=1)
        y_ref[...] = _permute_rows_f32(pmt_ref[...], y_sm) + d_ref[...] * u

    c3 = lambda i: (0, 0, 0)
    c2 = lambda i: (0, 0)
    return _pallas(
        body, name="ssm_fwd", grid=(t // tt,),
        in_specs=[pl.BlockSpec((tt, dssm), lambda i: (i, u_block)), pl.BlockSpec((tt, tt), c2),
                  pl.BlockSpec((tt, tt), c2), pl.BlockSpec(bq.shape, c3),
                  pl.BlockSpec(cq.shape, c3), pl.BlockSpec(lam.shape, c2), pl.BlockSpec(pw.shape, c2),
                  pl.BlockSpec((1, dssm), c2)],
        out_specs=[pl.BlockSpec((tt, dssm), lambda i: (i, 0)),
                   pl.BlockSpec((None, ntile, SUBLANES, SLAB), lambda i: (i, 0, 0, 0))],
        out_shape=[jax.ShapeDtypeStruct((t, dssm), F32), jax.ShapeDtypeStruct((t // tt, ntile, SUBLANES, SLAB), F32)],
        scratch_shapes=[pltpu.VMEM((ntile, tt, SLAB), F32), pltpu.VMEM((ntile, SUBLANES, SLAB), F32)],
        compiler_params=_cp(2 * (8 * tt * dssm + 4 * nq * ds * w2) + 4 * tt * width + 16 * MIB,
                            dimension_semantics=("arbitrary",)),
    )(proj, pm, pmt, bq, cq, lam, pw, d_skip)


def _ssm_bwd(proj, u_block, dy, cin, dproj, bq, cq, lam, pw, d_skip):
    t = proj.shape[0]
    nq, ds, w2 = bq.shape
    dssm = nq * SLAB
    width = nq * w2
    ntile = width // SLAB
    tt = min(T_SCAN, t)
    nt = t // tt
    steps = tt // SUBLANES
    halo = SUBLANES
    perm = _step_major_perm(tt)
    pm, pmt = perm.astype(BF16), perm.T.astype(BF16)

    def body(u_ref, dy_ref, cin_ref, dp_any, pm_ref, pmt_ref, bq_ref, cq_ref, lam_ref, pw_ref, d_ref,
             du_ref, gb_ref, gc_ref, glam_ref, gd_ref, s_ref, gs_ref, car_f, car_b):
        del dp_any

        @pl.when(pl.program_id(0) == 0)
        def _():
            car_b[...] = jnp.zeros_like(car_b)
            gb_ref[...] = jnp.zeros_like(gb_ref)
            gc_ref[...] = jnp.zeros_like(gc_ref)
            glam_ref[...] = jnp.zeros_like(glam_ref)
            gd_ref[...] = jnp.zeros_like(gd_ref)

        u = u_ref[...]
        dyv = dy_ref[...]
        gd_ref[0:1, :] += jnp.sum(dyv * u, axis=0, keepdims=True)
        pmv = pm_ref[...]
        ub = jnp.dot(pmv, u.astype(BF16), preferred_element_type=F32).astype(BF16)
        dyb = jnp.dot(pmv, dyv.astype(BF16), preferred_element_type=F32).astype(BF16)
        car_f[...] = cin_ref[...]
        data = slice(halo, halo + tt)
        everything = slice(None)
        for q in range(nq):
            _put_slab(s_ref, data, q, jnp.dot(ub[:, q * SLAB:(q + 1) * SLAB], bq_ref[q], preferred_element_type=F32))
        for q in range(nq):
            _scan_slab(s_ref, halo, q, lam_ref, pw_ref, car_f, steps, reverse=False)
        last_step = s_ref[:, halo + tt - SUBLANES:halo + tt, :]
        s_ref[:, 0:halo, :] = jnp.concatenate([cin_ref[:, 0:1, :], last_step[:, 0:SUBLANES - 1, :]], axis=1)
        tn = (((0,), (0,)), ((), ()))
        nt_dims = (((1,), (1,)), ((), ()))
        for q in range(nq):
            sl = slice(q * SLAB, (q + 1) * SLAB)
            gc_ref[q] += lax.dot_general(dyb[:, sl], _get_slab(s_ref, data, q).astype(BF16), tn,
                                         preferred_element_type=F32)
            _put_slab(gs_ref, everything, q,
                      lax.dot_general(dyb[:, sl], cq_ref[q], nt_dims, preferred_element_type=F32))
        for q in range(nq):
            _scan_slab(gs_ref, 0, q, lam_ref, pw_ref, car_b, steps, reverse=True,
                       prev_ref=s_ref, prev_row0=halo, glam_ref=glam_ref)
        du_parts = []
        for q in range(nq):
            sl = slice(q * SLAB, (q + 1) * SLAB)
            gsb = _get_slab(gs_ref, everything, q).astype(BF16)
            du_parts.append(lax.dot_general(gsb, bq_ref[q], nt_dims, preferred_element_type=F32))
            gb_ref[q] += lax.dot_general(ub[:, sl], gsb, tn, preferred_element_type=F32)
        du_sm = jnp.concatenate(du_parts, axis=1)
        du_ref[...] = (_permute_rows_f32(pmt_ref[...], du_sm) + dyv * d_ref[...]).astype(BF16)

    c3 = lambda i: (0, 0, 0)
    c2 = lambda i: (0, 0)
    rev = lambda i: (nt - 1 - i, 0)
    dense = jax.ShapeDtypeStruct((nq, SLAB, w2), F32)
    gp = lam.shape[1]
    return _pallas(
        body, name="ssm_bwd", grid=(nt,),
        in_specs=[pl.BlockSpec((tt, dssm), lambda i: (nt - 1 - i, u_block)), pl.BlockSpec((tt, dssm), rev),
                  pl.BlockSpec((None, ntile, SUBLANES, SLAB), lambda i: (nt - 1 - i, 0, 0, 0)),
                  pl.BlockSpec(memory_space=pl.ANY), pl.BlockSpec((tt, tt), c2), pl.BlockSpec((tt, tt), c2),
                  pl.BlockSpec(bq.shape, c3), pl.BlockSpec(cq.shape, c3),
                  pl.BlockSpec(lam.shape, c2), pl.BlockSpec(pw.shape, c2), pl.BlockSpec((1, dssm), c2)],
        out_specs=[pl.BlockSpec((tt, dssm), lambda i: (nt - 1 - i, u_block)), pl.BlockSpec(dense.shape, c3),
                   pl.BlockSpec(dense.shape, c3), pl.BlockSpec((SUBLANES, gp), c2), pl.BlockSpec((SUBLANES, dssm), c2)],
        out_shape=[jax.ShapeDtypeStruct(dproj.shape, dproj.dtype), dense, dense,
                   jax.ShapeDtypeStruct((SUBLANES, gp), F32), jax.ShapeDtypeStruct((SUBLANES, dssm), F32)],
        scratch_shapes=[pltpu.VMEM((ntile, tt + halo, SLAB), F32), pltpu.VMEM((ntile, tt, SLAB), F32),
                        pltpu.VMEM((ntile, SUBLANES, SLAB), F32), pltpu.VMEM((ntile, SUBLANES, SLAB), F32)],
        input_output_aliases={3: 0},
        compiler_params=_cp(2 * (10 * tt * dssm + 4 * nq * ds * w2 + 8 * nq * SLAB * w2)
                            + 8 * tt * width + 12 * MIB, dimension_semantics=("arbitrary",)),
    )(proj, dy, cin, dproj, pm, pmt, bq, cq, lam, pw, d_skip)


def _gate_fwd(proj, y, conv_w, conv_b, w_glu, b_glu, dc):
    t = proj.shape[0]
    dssm = y.shape[1]
    assert dc == dssm
    tm = min(TM_GATE, t)
    halo = SUBLANES

    def body(b_ref, c_ref, v_ref, zc_ref, zs_ref, y_ref, cw_ref, cb_ref, wg_ref, bg_ref, mix_ref, cv_buf):
        @pl.when(pl.program_id(0) == 0)
        def _():
            cv_buf[0:halo, :] = jnp.zeros((halo, dc), F32)

        cv = c_ref[...] * v_ref[...]
        cv_buf[halo:, :] = cv
        conv = (cb_ref[...] + cw_ref[2:3, :] * cv + cw_ref[1:2, :] * cv_buf[halo - 1:halo - 1 + tm, :]
                + cw_ref[0:1, :] * cv_buf[halo - 2:halo - 2 + tm, :])
        sz, _ = _silu(zc_ref[...])
        mix_ref[:, 0:dc] = (b_ref[...] * conv * sz).astype(BF16)
        cv_buf[0:halo, :] = cv_buf[tm:tm + halo, :]
        ge, _ = _gelu(y_ref[...])
        gl = jnp.dot(ge.astype(BF16), wg_ref[...], preferred_element_type=F32) + bg_ref[...]
        szs, _ = _silu(zs_ref[...])
        mix_ref[:, dc:] = (ge * jax.nn.sigmoid(gl) * szs).astype(BF16)

    col = lambda j: pl.BlockSpec((tm, dc), lambda i, j=j: (i, j))
    fixed = lambda i: (0, 0)
    return _pallas(
        body, name="gate_fwd", grid=(t // tm,),
        in_specs=[col(0), col(1), col(2), col(3), col(5), pl.BlockSpec((tm, dssm), lambda i: (i, 0)),
                  pl.BlockSpec(conv_w.shape, fixed), pl.BlockSpec((1, dc), fixed),
                  pl.BlockSpec(w_glu.shape, fixed), pl.BlockSpec((1, dssm), fixed)],
        out_specs=pl.BlockSpec((tm, dc + dssm), lambda i: (i, 0)),
        out_shape=jax.ShapeDtypeStruct((t, dc + dssm), BF16),
        scratch_shapes=[pltpu.VMEM((tm + halo, dc), F32)],
        compiler_params=_cp(2 * (6 * 4 * tm * dc + 2 * tm * (dc + dssm) + 2 * dssm * dssm) + 24 * tm * dc + 8 * MIB,
                            dimension_semantics=("arbitrary",)),
    )(proj, proj, proj, proj, proj, y, conv_w, conv_b, w_glu, b_glu)


def _gate_bwd(proj, y, dmix, conv_w, conv_b, w_glu, b_glu, dc):
    t = proj.shape[0]
    dssm = y.shape[1]
    tm = min(TM_GATE, t)
    nt = t // tm
    halo = SUBLANES
    blocks_per_tile = tm // halo

    rb = 2 * SUBLANES
    assert tm % rb == 0
    n_chunk = tm // rb

    def body(b_ref, c_ref, v_ref, zc_ref, zs_ref, cp_ref, vp_ref, y_ref, dm_ref, cw_ref, cb_ref, wg_ref, bg_ref,
             dp_ref, dy_ref, gs_ref, gwg_ref, cv_buf, dc_buf, sh1_buf, sh2_buf, geb_buf, dglb_buf, ge_buf, th_buf):
        i = pl.program_id(0)

        @pl.when(i == 0)
        def _():
            dc_buf[tm:, :] = jnp.zeros((halo, dc), F32)
            gs_ref[...] = jnp.zeros_like(gs_ref)
            gwg_ref[...] = jnp.zeros_like(gwg_ref)

        def rows(r):
            return pl.ds(r * rb, rb)

        def rows_after_halo(r):
            return pl.ds(halo + r * rb, rb)

        def chunks(step, carry):
            for r in range(n_chunk):
                carry = step(r, carry)
            return carry

        def fold(v):
            return v[0:SUBLANES] + v[SUBLANES:rb]

        first_tile = (i == nt - 1)
        cv_buf[0:halo, :] = jnp.where(first_tile, 0.0, cp_ref[...] * vp_ref[...])
        cv_buf[halo:, :] = c_ref[...] * v_ref[...]
        sh1_buf[...] = cv_buf[halo - 1:halo - 1 + tm, :]
        sh2_buf[...] = cv_buf[halo - 2:halo - 2 + tm, :]
        w0, w1, w2, cb = cw_ref[0:1, :], cw_ref[1:2, :], cw_ref[2:3, :], cb_ref[...]

        def conv_pass_a(r, carry):
            rs = rows(r)
            bv, zc, dyc = b_ref[rs, :], zc_ref[rs, :], dm_ref[rs, 0:dc]
            conv = cb + w2 * cv_buf[rows_after_halo(r), :] + w1 * sh1_buf[rs, :] + w0 * sh2_buf[rs, :]
            sz, sgc = _silu(zc)
            dp_ref[rs, 0:dc] = (dyc * conv * sz).astype(BF16)
            dp_ref[rs, 3 * dc:4 * dc] = (dyc * bv * conv * _dsilu(zc, sgc)).astype(BF16)
            dc_buf[rs, :] = dyc * bv * sz
            return carry

        chunks(conv_pass_a, 0)
        sh1_buf[...] = dc_buf[1:1 + tm, :]
        sh2_buf[...] = dc_buf[2:2 + tm, :]

        def conv_pass_b(r, acc):
            rs = rows(r)
            dconv, d1, d2 = dc_buf[rs, :], sh1_buf[rs, :], sh2_buf[rs, :]
            cv = cv_buf[rows_after_halo(r), :]
            dcv = w2 * dconv + w1 * d1 + w0 * d2
            dp_ref[rs, dc:2 * dc] = (dcv * v_ref[rs, :]).astype(BF16)
            dp_ref[rs, 2 * dc:3 * dc] = (dcv * c_ref[rs, :]).astype(BF16)
            return (acc[0] + fold(cv * d2), acc[1] + fold(cv * d1), acc[2] + fold(cv * dconv), acc[3] + fold(dconv))

        zero = jnp.zeros((SUBLANES, dc), F32)
        sums = chunks(conv_pass_b, (zero, zero, zero, zero))
        for k in range(4):
            gs_ref[k:k + 1, :] += jnp.sum(sums[k], axis=0, keepdims=True)
        dc_buf[tm:, :] = dc_buf[0:halo, :]

        def glu_pass_a(r, carry):
            rs = rows(r)
            ge, th = _gelu(y_ref[rs, :])
            geb_buf[rs, :] = ge.astype(BF16)
            ge_buf[rs, :] = ge
            th_buf[rs, :] = th
            return carry

        chunks(glu_pass_a, 0)
        wg = wg_ref[...]
        gl_buf = cv_buf.at[halo:halo + tm]
        keep_buf = dc_buf.at[0:tm]
        gl_buf[...] = jnp.dot(geb_buf[...], wg, preferred_element_type=F32) + bg_ref[...]
        dp_ref[:, 4 * dc:5 * dc] = jnp.zeros((tm, dc), BF16)

        def glu_pass_b(r, acc):
            rs = rows(r)
            ge = ge_buf[rs, :]
            sg = jax.nn.sigmoid(gl_buf[rs, :])
            zs, dys = zs_ref[rs, :], dm_ref[rs, dc:]
            szs, sgs = _silu(zs)
            dp_ref[rs, 5 * dc:] = (dys * ge * sg * _dsilu(zs, sgs)).astype(BF16)
            d_ys = dys * szs
            dgl = d_ys * ge * sg * (1.0 - sg)
            dglb_buf[rs, :] = dgl.astype(BF16)
            keep_buf[rs, :] = d_ys * sg
            return acc + fold(dgl)

        gs_ref[4:5, :] += jnp.sum(chunks(glu_pass_b, zero), axis=0, keepdims=True)
        dglb = dglb_buf[...]
        gwg_ref[...] += lax.dot_general(geb_buf[...], dglb, (((0,), (0,)), ((), ())), preferred_element_type=F32)
        gl_buf[...] = lax.dot_general(dglb, wg, (((1,), (1,)), ((), ())), preferred_element_type=F32)

        def glu_pass_c(r, carry):
            rs = rows(r)
            dy_ref[rs, :] = (keep_buf[rs, :] + gl_buf[rs, :]) * _dgelu(y_ref[rs, :], th_buf[rs, :])
            return carry

        chunks(glu_pass_c, 0)

    col = lambda j: pl.BlockSpec((tm, dc), lambda i, j=j: (nt - 1 - i, j))
    prev = lambda j: pl.BlockSpec((halo, dc), lambda i, j=j: (jnp.maximum((nt - 1 - i) * blocks_per_tile - 1, 0), j))
    rev = lambda i: (nt - 1 - i, 0)
    fixed = lambda i: (0, 0)
    return _pallas(
        body, name="gate_bwd", grid=(nt,),
        in_specs=[col(0), col(1), col(2), col(3), col(5), prev(1), prev(2), pl.BlockSpec((tm, dssm), rev),
                  pl.BlockSpec((tm, dc + dssm), rev), pl.BlockSpec(conv_w.shape, fixed), pl.BlockSpec((1, dc), fixed),
                  pl.BlockSpec(w_glu.shape, fixed), pl.BlockSpec((1, dssm), fixed)],
        out_specs=[pl.BlockSpec((tm, 6 * dc), rev), pl.BlockSpec((tm, dssm), rev),
                   pl.BlockSpec((2 * SUBLANES, dc), fixed), pl.BlockSpec((dssm, dssm), fixed)],
        out_shape=[jax.ShapeDtypeStruct((t, 6 * dc), BF16), jax.ShapeDtypeStruct((t, dssm), F32),
                   jax.ShapeDtypeStruct((2 * SUBLANES, dc), F32), jax.ShapeDtypeStruct((dssm, dssm), F32)],
        scratch_shapes=[pltpu.VMEM((tm + halo, dc), F32), pltpu.VMEM((tm + halo, dc), F32), pltpu.VMEM((tm, dc), F32),
                        pltpu.VMEM((tm, dc), F32), pltpu.VMEM((tm, dssm), BF16), pltpu.VMEM((tm, dssm), BF16),
                        pltpu.VMEM((tm, dssm), F32), pltpu.VMEM((tm, dssm), F32)],
        compiler_params=_cp(2 * (6 * 4 * tm * dc + 8 * tm * dc + 12 * tm * dc + 4 * tm * dc + 6 * dssm * dssm)
                            + 28 * tm * dc + 8 * MIB, dimension_semantics=("arbitrary",)),
    )(proj, proj, proj, proj, proj, proj, proj, y, dmix, conv_w, conv_b, w_glu, b_glu)


def _allgather_halves(parts, name):
    n = len(parts)
    per = 2 * (N_CHIPS - 1)

    def body(*refs):
        ins, outs = refs[:n], refs[n:2 * n]
        send, recv, local = refs[2 * n:]
        x, y, c = _my_place()
        k = 2 * x + y
        sib = (x, y, 1 - c)
        chips = _other_chips(x, y)

        def rc(t, s, src, dst, to):
            return pltpu.make_async_remote_copy(src_ref=src, dst_ref=dst, send_sem=send.at[t * per + s],
                                                recv_sem=recv.at[t * per + s], device_id=to, device_id_type=MESH)

        mine = [pltpu.make_async_copy(ins[t], outs[t].at[k], local.at[t]) for t in range(n)]
        for cp in mine:
            cp.start()
        sent = []
        for t in range(n):
            for j, (cx, cy) in enumerate(chips):
                cp = rc(t, j, ins[t].at[c], outs[t].at[k, c], (cx, cy, c))
                cp.start()
                sent.append(cp)
        for t in range(n):
            for j, (cx, cy) in enumerate(chips):
                landed = outs[t].at[2 * cx + cy, c]
                rc(t, j, landed, landed, (cx, cy, c)).wait_recv()
                cp = rc(t, N_CHIPS - 1 + j, landed, landed, sib)
                cp.start()
                sent.append(cp)
        for t in range(n):
            for j, (cx, cy) in enumerate(chips):
                other = outs[t].at[2 * cx + cy, 1 - c]
                rc(t, N_CHIPS - 1 + j, other, other, sib).wait_recv()
        for cp in sent:
            cp.wait_send()
        for cp in mine:
            cp.wait()

    any_spec = pl.BlockSpec(memory_space=pl.ANY)
    return _pallas(
        body, name=name, in_specs=[any_spec] * n, out_specs=[any_spec] * n,
        out_shape=[jax.ShapeDtypeStruct((N_CHIPS,) + p.shape, p.dtype) for p in parts],
        scratch_shapes=[pltpu.SemaphoreType.DMA((n * per,)), pltpu.SemaphoreType.DMA((n * per,)),
                        pltpu.SemaphoreType.DMA((n,))],
        compiler_params=_cp(16 * MIB),
    )(*parts)


def _allgather_flat(v, name):
    r, c = v.shape
    rels = [(dx, dy, dc) for dx in (0, 1) for dy in (0, 1) for dc in (0, 1)][1:]

    def body(v_ref, out_ref, send, recv):
        x, y, cc = _my_place()
        me = 4 * x + 2 * y + cc

        def peer(rel):
            dx, dy, dc = rel
            return (1 - x if dx else x, 1 - y if dy else y, 1 - cc if dc else cc)

        def rc(s, slot, to):
            return pltpu.make_async_remote_copy(src_ref=v_ref, dst_ref=out_ref.at[slot], send_sem=send.at[s],
                                                recv_sem=recv.at[s], device_id=to, device_id_type=MESH)

        sent = []
        for s, rel in enumerate(rels):
            cp = rc(s, me, peer(rel))
            cp.start()
            sent.append(cp)
        out_ref[me] = v_ref[...]
        for s, rel in enumerate(rels):
            px, py, pc = peer(rel)
            rc(s, 4 * px + 2 * py + pc, (px, py, pc)).wait_recv()
        for cp in sent:
            cp.wait_send()

    return _pallas(
        body, name=name, out_shape=jax.ShapeDtypeStruct((N_DEV, r, c), F32),
        in_specs=[pl.BlockSpec(memory_space=pltpu.VMEM)], out_specs=pl.BlockSpec(memory_space=pltpu.VMEM),
        scratch_shapes=[pltpu.SemaphoreType.DMA((N_DEV - 1,)), pltpu.SemaphoreType.DMA((N_DEV - 1,))],
        compiler_params=_cp((N_DEV + 2) * 4 * r * c + 8 * MIB),
    )(v)


def _rs_pair_exchange(grads, name):
    n = len(grads)

    def body(*refs):
        ins, outs = refs[:n], refs[n:2 * n]
        send, recv = refs[2 * n:]
        x, y, c = _my_place()
        cps = []
        for t in range(n):
            cp = pltpu.make_async_remote_copy(src_ref=ins[t].at[:, 1 - c], dst_ref=outs[t], send_sem=send.at[t],
                                              recv_sem=recv.at[t], device_id=(x, y, 1 - c), device_id_type=MESH)
            cp.start()
            cps.append(cp)
        for cp in cps:
            cp.wait()

    any_spec = pl.BlockSpec(memory_space=pl.ANY)
    return _pallas(
        body, name=name, in_specs=[any_spec] * n, out_specs=[any_spec] * n,
        out_shape=[jax.ShapeDtypeStruct((g.shape[0],) + g.shape[2:], g.dtype) for g in grads],
        scratch_shapes=[pltpu.SemaphoreType.DMA((n,)), pltpu.SemaphoreType.DMA((n,))],
        compiler_params=_cp(16 * MIB),
    )(*grads)


def _rs_pair_add(place, grad, got, name):
    nk, _, r2, c = grad.shape
    tr = min(TR_ELT, r2)

    def body(place_ref, g_ref, r_ref, o16_ref, o32_ref):
        del place_ref
        s = g_ref[...] + r_ref[...].astype(F32)
        o32_ref[...] = s
        o16_ref[...] = s.astype(BF16)

    blk = pl.BlockSpec((None, tr, c), lambda k, i, p: (k, i, 0))
    return _pallas(
        body, name=name,
        grid_spec=pltpu.PrefetchScalarGridSpec(
            num_scalar_prefetch=1, grid=(nk, r2 // tr),
            in_specs=[pl.BlockSpec((None, None, tr, c), lambda k, i, p: (k, p[0], i, 0)), blk],
            out_specs=[blk, blk]),
        out_shape=[jax.ShapeDtypeStruct((nk, r2, c), BF16), jax.ShapeDtypeStruct((nk, r2, c), F32)],
        compiler_params=_cp(2 * 14 * tr * c + 8 * MIB),
    )(place, grad, got)


_HBM_SPEC = pl.BlockSpec(memory_space=pltpu.HBM)
_SEM_SPEC = pl.BlockSpec(memory_space=pltpu.SEMAPHORE)
_DATAFLOW = pltpu.SideEffectType.DATAFLOW_SIDE_EFFECTING


def _split_copy_start(srcs, land_shapes, plan, n_sems, name, after=()):
    ns, nl, na = len(srcs), len(land_shapes), len(after)

    def body(*refs):
        src_refs, land_refs = refs[:ns], refs[ns:ns + nl]
        send, recv = refs[ns + nl + na], refs[ns + nl + na + 1]
        token = refs[-1]
        sends, _ = plan(src_refs, land_refs)
        for src, dst, to, si, ri in sends:
            pltpu.make_async_remote_copy(src_ref=src, dst_ref=dst, send_sem=send.at[si], recv_sem=recv.at[ri],
                                         device_id=to, device_id_type=MESH).start()
        token[...] = jnp.zeros_like(token)

    lands = [lax.empty(shp, dt) for shp, dt in land_shapes]
    through = [pltpu.HBM(a.shape, a.dtype) for a in srcs] + [pltpu.HBM(shp, dt) for shp, dt in land_shapes]
    out = pl.pallas_call(
        body, name=name,
        out_shape=(pltpu.SemaphoreType.DMA((n_sems,)), pltpu.SemaphoreType.DMA((n_sems,)), *through,
                   jax.ShapeDtypeStruct((SUBLANES, SLAB), F32)),
        in_specs=[_HBM_SPEC] * (ns + nl) + [pl.BlockSpec(memory_space=pl.ANY)] * na,
        out_specs=(_SEM_SPEC, _SEM_SPEC, *([_HBM_SPEC] * (ns + nl)), pl.BlockSpec(memory_space=pltpu.VMEM)),
        input_output_aliases={i: 2 + i for i in range(ns + nl)},
        compiler_params=pltpu.CompilerParams(has_side_effects=_DATAFLOW),
    )(*[pltpu.with_memory_space_constraint(a, pltpu.HBM) for a in (*srcs, *lands)], *after)
    return out[0], out[1], list(out[2:2 + ns]), list(out[2 + ns:2 + ns + nl]), out[-1]


def _split_copy_wait(send, recv, srcs, lands, plan, after, name):
    ns, nl, na = len(srcs), len(lands), len(after)

    def body(*refs):
        src_refs, land_refs = refs[:ns], refs[ns:ns + nl]
        send_ref, recv_ref = refs[ns + nl], refs[ns + nl + 1]
        sends, arrivals = plan(src_refs, land_refs)
        for src, dst, to, si, ri in sends:
            pltpu.make_async_remote_copy(src_ref=src, dst_ref=dst, send_sem=send_ref.at[si], recv_sem=recv_ref.at[ri],
                                         device_id=to, device_id_type=MESH).wait_send()
        for (src, _, to, si, _), (view, ri) in zip(sends, arrivals):
            pltpu.make_async_remote_copy(src_ref=view, dst_ref=view, send_sem=send_ref.at[si], recv_sem=recv_ref.at[ri],
                                         device_id=to, device_id_type=MESH).wait_recv()

    out = pl.pallas_call(
        body, name=name,
        out_shape=[pltpu.HBM(a.shape, a.dtype) for a in (*srcs, *lands)],
        in_specs=[_HBM_SPEC] * (ns + nl) + [_SEM_SPEC, _SEM_SPEC] + [pl.BlockSpec(memory_space=pl.ANY)] * na,
        out_specs=[_HBM_SPEC] * (ns + nl),
        input_output_aliases={i: i for i in range(ns + nl)},
        compiler_params=pltpu.CompilerParams(has_side_effects=_DATAFLOW),
    )(*srcs, *lands, send, recv, *after)
    return list(out[:ns]), list(out[ns:])


def _chip_exchange_plan(n):
    per = N_CHIPS - 1

    def plan(srcs, lands):
        x, y, c = _my_place()
        sends, arrivals = [], []
        for t in range(n):
            for j, (cx, cy) in enumerate(_other_chips(x, y)):
                sends.append((srcs[t].at[2 * cx + cy], lands[t].at[j], (cx, cy, c), t * per + j, t * per + j))
                arrivals.append((lands[t].at[j], t * per + j))
        return sends, arrivals

    return plan


def _gather_half_plan(n):
    per = N_CHIPS - 1

    def plan(srcs, lands):
        x, y, c = _my_place()
        k = 2 * x + y
        sends, arrivals = [], []
        for t in range(n):
            for j, (cx, cy) in enumerate(_other_chips(x, y)):
                sends.append((srcs[t].at[c], lands[t].at[k, c], (cx, cy, c), t * per + j, t * per + j))
                arrivals.append((lands[t].at[2 * cx + cy, c], t * per + j))
        return sends, arrivals

    return plan


def _gather_complete(own, landed, name):
    def body(own_ref, in_ref, out_ref, send, recv, local):
        x, y, c = _my_place()
        sib = (x, y, 1 - c)
        mine = pltpu.make_async_copy(own_ref, out_ref.at[2 * x + y], local)
        mine.start()
        sent = []
        for j, (cx, cy) in enumerate(_other_chips(x, y)):
            half = (2 * cx + cy, c)
            cp = pltpu.make_async_remote_copy(src_ref=in_ref.at[half], dst_ref=out_ref.at[half], send_sem=send.at[j],
                                              recv_sem=recv.at[j], device_id=sib, device_id_type=MESH)
            cp.start()
            sent.append(cp)
        for j, (cx, cy) in enumerate(_other_chips(x, y)):
            other = out_ref.at[2 * cx + cy, 1 - c]
            pltpu.make_async_remote_copy(src_ref=other, dst_ref=other, send_sem=send.at[j], recv_sem=recv.at[j],
                                         device_id=sib, device_id_type=MESH).wait_recv()
        for cp in sent:
            cp.wait_send()
        mine.wait()

    any_spec = pl.BlockSpec(memory_space=pl.ANY)
    per = N_CHIPS - 1
    return _pallas(
        body, name=name, in_specs=[any_spec, any_spec], out_specs=any_spec,
        out_shape=jax.ShapeDtypeStruct(landed.shape, landed.dtype), input_output_aliases={1: 0},
        scratch_shapes=[pltpu.SemaphoreType.DMA((per,)), pltpu.SemaphoreType.DMA((per,)), pltpu.SemaphoreType.DMA],
        compiler_params=_cp(16 * MIB),
    )(own, landed)


def _gather_direct_plan(n):
    per = 2 * (N_CHIPS - 1)

    def plan(srcs, lands):
        x, y, c = _my_place()
        k = 2 * x + y
        sends, arrivals = [], []
        for t in range(n):
            for j, (cx, cy) in enumerate(_other_chips(x, y)):
                for core in (0, 1):
                    sends.append((srcs[t].at[c], lands[t].at[k, c], (cx, cy, core),
                                  t * per + 2 * j + core, t * per + 2 * j + c))
                    arrivals.append((lands[t].at[2 * cx + cy, core], t * per + 2 * j + core))
        return sends, arrivals

    return plan


def _rs_chip_add(place, sums32, got, name):
    _, r2, c = sums32.shape
    tr = min(TR_ELT, r2)

    def body(place_ref, s_ref, q_ref, o_ref):
        del place_ref
        o_ref[...] = ((s_ref[...] + q_ref[0].astype(F32)) + q_ref[1].astype(F32)) + q_ref[2].astype(F32)

    return _pallas(
        body, name=name,
        grid_spec=pltpu.PrefetchScalarGridSpec(
            num_scalar_prefetch=1, grid=(r2 // tr,),
            in_specs=[pl.BlockSpec((None, tr, c), lambda i, p: (p[1], i, 0)),
                      pl.BlockSpec((N_CHIPS - 1, tr, c), lambda i, p: (0, i, 0))],
            out_specs=pl.BlockSpec((tr, c), lambda i, p: (i, 0))),
        out_shape=jax.ShapeDtypeStruct((r2, c), F32),
        compiler_params=_cp(2 * 14 * tr * c + 8 * MIB),
    )(place, sums32, got)


def _rs_pair_share(halves, name):
    n = len(halves)

    def body(*refs):
        ins, outs = refs[:n], refs[n:2 * n]
        send, recv, local = refs[2 * n:]
        x, y, c = _my_place()
        cps, mine = [], []
        for t in range(n):
            lc = pltpu.make_async_copy(ins[t], outs[t].at[c], local.at[t])
            lc.start()
            mine.append(lc)
            cp = pltpu.make_async_remote_copy(src_ref=ins[t], dst_ref=outs[t].at[c], send_sem=send.at[t],
                                              recv_sem=recv.at[t], device_id=(x, y, 1 - c), device_id_type=MESH)
            cp.start()
            cps.append(cp)
        for t in range(n):
            other = outs[t].at[1 - c]
            pltpu.make_async_remote_copy(src_ref=other, dst_ref=other, send_sem=send.at[t], recv_sem=recv.at[t],
                                         device_id=(x, y, 1 - c), device_id_type=MESH).wait_recv()
        for cp in cps:
            cp.wait_send()
        for lc in mine:
            lc.wait()

    any_spec = pl.BlockSpec(memory_space=pl.ANY)
    return _pallas(
        body, name=name, in_specs=[any_spec] * n, out_specs=[any_spec] * n,
        out_shape=[jax.ShapeDtypeStruct((2,) + h.shape, F32) for h in halves],
        scratch_shapes=[pltpu.SemaphoreType.DMA((n,)), pltpu.SemaphoreType.DMA((n,)), pltpu.SemaphoreType.DMA((n,))],
        compiler_params=_cp(16 * MIB),
    )(*halves)


_PACK_TILE = SUBLANES * SLAB


def _pack(arrays):
    rows = []
    for a in arrays:
        flat = a.reshape(-1).astype(F32)
        padded = -(-flat.shape[0] // _PACK_TILE) * _PACK_TILE
        rows.append(jnp.pad(flat, (0, padded - flat.shape[0])).reshape(-1, SLAB))
    n_rows = sum(r.shape[0] for r in rows)
    if n_rows % (2 * SUBLANES):
        rows.append(jnp.zeros((SUBLANES, SLAB), F32))
    return jnp.concatenate(rows, axis=0)


def _unpack(packed, shapes):
    out, row = [], 0
    for shp in shapes:
        size = math.prod(shp)
        nrow = -(-size // _PACK_TILE) * SUBLANES
        out.append(packed[row:row + nrow].reshape(-1)[:size].reshape(shp))
        row += nrow
    return out


def kernel(x, norm_pre_g, w_in, conv_w, conv_b, ssm_a_re, ssm_a_im, ssm_log_dt, ssm_b_re, ssm_b_im, ssm_c_re, ssm_c_im, ssm_d, w_glu, b_glu, w_out, norm_post_g, loss_target, m_norm_pre_g, m_w_in, m_conv_w, m_conv_b, m_ssm_a_re, m_ssm_a_im, m_ssm_log_dt, m_ssm_b_re, m_ssm_b_im, m_ssm_c_re, m_ssm_c_im, m_ssm_d, m_w_glu, m_b_glu, m_w_out, m_norm_post_g, v_norm_pre_g, v_w_in, v_conv_w, v_conv_b, v_ssm_a_re, v_ssm_a_im, v_ssm_log_dt, v_ssm_b_re, v_ssm_b_im, v_ssm_c_re, v_ssm_c_im, v_ssm_d, v_w_glu, v_b_glu, v_w_out, v_norm_post_g):
    xs, tgt = x[0], loss_target[0]
    t, d = xs.shape
    dc = conv_b.shape[0]
    dssm = ssm_d.shape[0]
    g, p = ssm_a_re.shape
    h = SSM_H
    nq = dssm // SLAB
    n_shard = w_in.shape[1]
    steps = min(T_SCAN, t) // SUBLANES
    mx, my, mc = _my_place()
    chip = 2 * mx + my
    place = jnp.stack([mc, chip]).astype(jnp.int32)

    cw_cols = conv_w.shape[1]
    cw_pad = -(-cw_cols // SLAB) * SLAB
    cw_blk = jnp.zeros((SUBLANES, cw_pad), F32).at[:conv_w.shape[0], :cw_cols].set(conv_w)
    cw_all = _allgather_flat(cw_blk, "allgather_conv_w")
    conv_w_full = jnp.concatenate([cw_all[2 * k, :, :cw_cols] for k in range(N_CHIPS)], axis=1)

    halves = lambda a: a.reshape(2, a.shape[0] // 2, a.shape[1])
    win_half = halves(_cast_bf16(w_in, "cast_w_in", cw_all[0, :, :SLAB]))
    win_plan = _gather_half_plan(1)
    win_send, win_recv, win_srcs, win_lands, win_token = _split_copy_start(
        [win_half], [((N_CHIPS,) + win_half.shape, BF16)], win_plan, N_CHIPS - 1, "gather_w_in_start")
    behind_w_in = win_token[0:1, 0:1]

    small_names = ["norm_pre_g", "conv_w", "conv_b", "ssm_a_re", "ssm_a_im", "ssm_log_dt", "ssm_b_re", "ssm_b_im",
                   "ssm_c_re", "ssm_c_im", "ssm_d", "b_glu", "norm_post_g", "loss"]
    zeros_cw = jnp.zeros((conv_w.shape[0], dc), F32)
    one0 = jnp.zeros((1,), F32)
    small_w = dict(norm_pre_g=norm_pre_g, conv_w=zeros_cw, conv_b=conv_b, ssm_a_re=ssm_a_re, ssm_a_im=ssm_a_im,
                   ssm_log_dt=ssm_log_dt, ssm_b_re=ssm_b_re, ssm_b_im=ssm_b_im, ssm_c_re=ssm_c_re, ssm_c_im=ssm_c_im,
                   ssm_d=ssm_d, b_glu=b_glu, norm_post_g=norm_post_g, loss=one0)
    small_m = dict(norm_pre_g=m_norm_pre_g, conv_w=zeros_cw, conv_b=m_conv_b, ssm_a_re=m_ssm_a_re, ssm_a_im=m_ssm_a_im,
                   ssm_log_dt=m_ssm_log_dt, ssm_b_re=m_ssm_b_re, ssm_b_im=m_ssm_b_im, ssm_c_re=m_ssm_c_re,
                   ssm_c_im=m_ssm_c_im, ssm_d=m_ssm_d, b_glu=m_b_glu, norm_post_g=m_norm_post_g, loss=one0)
    small_v = dict(norm_pre_g=v_norm_pre_g, conv_w=zeros_cw, conv_b=v_conv_b, ssm_a_re=v_ssm_a_re, ssm_a_im=v_ssm_a_im,
                   ssm_log_dt=v_ssm_log_dt, ssm_b_re=v_ssm_b_re, ssm_b_im=v_ssm_b_im, ssm_c_re=v_ssm_c_re,
                   ssm_c_im=v_ssm_c_im, ssm_d=v_ssm_d, b_glu=v_b_glu, norm_post_g=v_norm_post_g, loss=one0)
    w_pack, m_pack, v_pack = [_pack([group[nm] for nm in small_names]) + behind_w_in
                              for group in (small_w, small_m, small_v)]
    side = [halves(_cast_bf16(w_out, "cast_w_out", win_token)), halves(_cast_bf16(w_glu, "cast_w_glu", win_token))]

    expand = jnp.repeat(jnp.eye(p, dtype=F32), h, axis=1)
    b_re2, b_im2 = ssm_b_re.reshape(g, p * h), ssm_b_im.reshape(g, p * h)
    log_dt2 = ssm_log_dt.reshape(g, 1) + behind_w_in
    lbr, lbi, bbr2, bbi2, pw3 = _zoh_fwd(ssm_a_re, ssm_a_im, log_dt2, b_re2, b_im2, expand, steps)
    lam = jnp.stack([lbr.reshape(g * p), lbi.reshape(g * p)])
    pw = pw3.reshape(2, g * p)
    to_slab_b = lambda b2: _blockdiag(b2.reshape(nq, GROUPS_PER_SLAB, p, h).transpose(0, 1, 3, 2))
    bq = jnp.concatenate([to_slab_b(bbr2), to_slab_b(bbi2)], axis=2).astype(BF16)
    to_slab_c = lambda c3: _blockdiag(c3.reshape(nq, GROUPS_PER_SLAB, h, p).transpose(0, 1, 3, 2))
    cq = jnp.concatenate([to_slab_c(ssm_c_re), to_slab_c(-ssm_c_im)], axis=1).astype(BF16)

    g_pre2, g_post2 = norm_pre_g.reshape(1, d), norm_post_g.reshape(1, d)
    conv_b2, b_glu2, d_skip2 = conv_b.reshape(1, dc), b_glu.reshape(1, dssm), ssm_d.reshape(1, dssm)
    u_block = 4 * dc // dssm

    hb = _prenorm(xs, g_pre2 + behind_w_in)
    proj_own = _inproj_own(place, hb, win_srcs[0].reshape(d, n_shard))
    win_own, win_landed = _split_copy_wait(win_send, win_recv, win_srcs, win_lands, win_plan,
                                           [proj_own, bq, cq, w_pack, m_pack, v_pack, *side], "gather_w_in_wait")
    win_g = _gather_complete(win_own[0], win_landed[0], "gather_w_in_complete")
    win_b = win_g.reshape(N_CHIPS, d, n_shard)
    side_plan = _gather_direct_plan(len(side))
    side_sems = 2 * (N_CHIPS - 1) * len(side)
    ag_send, ag_recv, ag_srcs, ag_lands, ag_token = _split_copy_start(
        side, [((N_CHIPS,) + a.shape, BF16) for a in side], side_plan, side_sems, "gather_side_weights_start",
        after=[win_g])
    proj = _inproj_rest(place, hb, win_b, proj_own, ag_token)
    y, cin = _ssm_fwd(proj, u_block, bq, cq, lam, pw, d_skip2)
    side_own, side_all = _split_copy_wait(ag_send, ag_recv, ag_srcs, ag_lands, side_plan, [cin],
                                          "gather_side_weights_wait")
    wout_g, wglu_g = [lax.dynamic_update_index_in_dim(all_, own, chip, 0) for own, all_ in zip(side_own, side_all)]
    wout_b = wout_g.reshape(dc + dssm, d)
    wglu_b = wglu_g.reshape(dssm, dssm)
    mix = _gate_fwd(proj, y, conv_w_full, conv_b2, wglu_b, b_glu2, dc)
    loss_blk, dout, dob, dmix, gg_post = _outproj(mix, wout_b, xs, tgt, g_post2)

    def reduce_start(grads, to_send, tags, group):
        got = _rs_pair_exchange(to_send, "rs_pair_exchange_" + group)
        sums = [_rs_pair_add(place, gt, rt, "rs_pair_add_" + tg) for gt, rt, tg in zip(grads, got, tags)]
        plan = _chip_exchange_plan(len(grads))
        started = _split_copy_start([s16 for s16, _ in sums],
                                    [((N_CHIPS - 1,) + s16.shape[1:], BF16) for s16, _ in sums], plan,
                                    (N_CHIPS - 1) * len(grads), "rs_chip_exchange_" + group + "_start")
        return plan, started, [s32 for _, s32 in sums]

    def reduce_finish(plan, started, sums32, tags, group, after):
        send, recv, srcs, lands, _ = started
        _, landed = _split_copy_wait(send, recv, srcs, lands, plan, after, "rs_chip_exchange_" + group + "_wait")
        mine = [_rs_chip_add(place, s32, qt, "rs_chip_add_" + tg) for s32, qt, tg in zip(sums32, landed, tags)]
        full = _rs_pair_share(mine, "rs_pair_share_" + group)
        return [f.reshape(2 * f.shape[1], f.shape[2]) for f in full]

    gw_out, gw_out16 = _matmul_tn(mix, dob, 1, "grad_w_out")
    dproj, dy, gsmall, gw_glu = _gate_bwd(proj, y, dmix, conv_w_full, conv_b2, wglu_b, b_glu2, dc)
    as_out = lambda a: a.reshape(N_CHIPS, 2, (dc + dssm) // (2 * N_CHIPS), d)
    gw_glu4 = gw_glu.reshape(N_CHIPS, 2, dssm // (2 * N_CHIPS), dssm)
    rs_a = reduce_start([as_out(gw_out), gw_glu4], [as_out(gw_out16), gw_glu4], ["w_out", "w_glu"], "a")
    dproj, gb_dense, gc_dense, glam, gd = _ssm_bwd(proj, u_block, dy, cin, dproj, bq, cq, lam, pw,
                                                   d_skip2 + rs_a[1][4][0:1, 0:1])
    gw_in, gw_in16 = _matmul_tn(hb, dproj, N_CHIPS, "grad_w_in")
    as_in = lambda a: a.reshape(N_CHIPS, 2, d // 2, n_shard)
    rs_b = reduce_start([as_in(gw_in)], [as_in(gw_in16)], ["w_in"], "b")
    gx, gg_pre = _dh_prenorm_bwd(dproj, win_b, xs, dout, g_pre2 + rs_b[1][4][0:1, 0:1])

    gb4 = gb_dense.reshape(nq, SLAB, 2, SLAB_STATES)
    g_bbr2 = _blockdiag_take(gb4[:, :, 0, :], h, p).transpose(0, 1, 3, 2).reshape(g, p * h)
    g_bbi2 = _blockdiag_take(gb4[:, :, 1, :], h, p).transpose(0, 1, 3, 2).reshape(g, p * h)
    gc4 = gc_dense.reshape(nq, SLAB, 2, SLAB_STATES)
    g_c_re = _blockdiag_take(gc4[:, :, 0, :], h, p).reshape(g, h, p)
    g_c_im = -_blockdiag_take(gc4[:, :, 1, :], h, p).reshape(g, h, p)
    g_a_re, g_a_im, g_ld, g_b_re2, g_b_im2 = _zoh_bwd(
        ssm_a_re, ssm_a_im, log_dt2, b_re2, b_im2, expand,
        glam[0].reshape(g, p), glam[1].reshape(g, p), g_bbr2, g_bbi2)

    small_g = {
        "norm_pre_g": gg_pre[0], "conv_w": gsmall[0:3], "conv_b": gsmall[3], "ssm_a_re": g_a_re, "ssm_a_im": g_a_im,
        "ssm_log_dt": g_ld.reshape(g), "ssm_b_re": g_b_re2.reshape(g, p, h), "ssm_b_im": g_b_im2.reshape(g, p, h),
        "ssm_c_re": g_c_re, "ssm_c_im": g_c_im, "ssm_d": gd[0], "b_glu": gsmall[4], "norm_post_g": gg_post[0],
        "loss": loss_blk[0, 0:1],
    }
    shapes = [small_w[nm].shape for nm in small_names]
    g_pack = _allreduce_small(_pack([small_g[nm] for nm in small_names]))
    packs = _adamw_small(g_pack, w_pack, m_pack, v_pack)
    sg, sd, sm, sv = [dict(zip(small_names, _unpack(pk, shapes))) for pk in (g_pack, *packs)]
    loss = sg["loss"][0]

    g_cw = lax.dynamic_slice_in_dim(sg["conv_w"], chip * cw_cols, cw_cols, axis=1)
    pad_cw = lambda a: jnp.zeros((SUBLANES, cw_pad), F32).at[:a.shape[0], :cw_cols].set(a)
    cut_cw = lambda a: a[:conv_w.shape[0], :cw_cols]
    d_cw, m_cw, v_cw = [cut_cw(a) for a in _adamw(pad_cw(conv_w), pad_cw(g_cw), pad_cw(m_conv_w), pad_cw(v_conv_w),
                                                  "adamw_conv_w")]

    g_wout, g_wglu = reduce_finish(*rs_a, ["w_out", "w_glu"], "a", [d_cw, packs[0]])
    (g_win,) = reduce_finish(*rs_b, ["w_in"], "b", [g_wout])
    d_win, m_win, v_win = _adamw(w_in, g_win, m_w_in, v_w_in, "adamw_w_in")
    d_wout, m_wout, v_wout = _adamw(w_out, g_wout, m_w_out, v_w_out, "adamw_w_out")
    d_wglu, m_wglu, v_wglu = _adamw(w_glu, g_wglu, m_w_glu, v_w_glu, "adamw_w_glu")

    order = ["norm_pre_g", "w_in", "conv_w", "conv_b", "ssm_a_re", "ssm_a_im", "ssm_log_dt", "ssm_b_re", "ssm_b_im",
             "ssm_c_re", "ssm_c_im", "ssm_d", "w_glu", "b_glu", "w_out", "norm_post_g"]
    grads, deltas, new_m, new_v = dict(sg), dict(sd), dict(sm), dict(sv)
    grads.update(w_in=g_win, w_out=g_wout, w_glu=g_wglu, conv_w=g_cw)
    deltas.update(w_in=d_win, w_out=d_wout, w_glu=d_wglu, conv_w=d_cw)
    new_m.update(w_in=m_win, w_out=m_wout, w_glu=m_wglu, conv_w=m_cw)
    new_v.update(w_in=v_win, w_out=v_wout, w_glu=v_wglu, conv_w=v_cw)
    return (loss, gx[None], *[grads[nm] for nm in order], *[deltas[nm] for nm in order],
            *[new_m[nm] for nm in order], *[new_v[nm] for nm in order])
```

```python
import functools
import math

import jax
import jax.numpy as jnp
from jax import lax
from jax.experimental import pallas as pl
from jax.experimental.pallas import tpu as pltpu

F32 = jnp.float32
BF16 = jnp.bfloat16
MESH = pl.DeviceIdType.MESH

EPS = 1e-6
SSM_H = 16
SSM_P = 64
GROUPS_PER_SLAB = 8
SLAB = 128
SLAB_STATES = GROUPS_PER_SLAB * SSM_P
N_CHIPS = 4
N_DEV = 8

ADAM_LR = 0.001
ADAM_B1 = 0.9
ADAM_B2 = 0.999
ADAM_EPS = 1e-08
ADAM_WD = 0.01
ADAM_STEP = 10

MIB = 1024 * 1024
VMEM_CAP = 48 * MIB
SUBLANES = 8

TM_NORM = 512
TM_PROJ = 512
TM_GATE = 256
TM_OUT = 256
TM_DH = 256
T_SCAN = 256
TK_TN = 512
TM_TN = 1024
TR_ELT = 256


def _cp(vmem_bytes, **kw):
    return pltpu.CompilerParams(vmem_limit_bytes=int(min(VMEM_CAP, max(16 * MIB, vmem_bytes))), **kw)


def _pallas(body, **kw):
    if "grid" not in kw and "grid_spec" not in kw:
        return pl.pallas_call(body, **kw)
    pin = lambda s: pltpu.HBM(s.shape, s.dtype) if isinstance(s, jax.ShapeDtypeStruct) else s
    out_shape = kw.pop("out_shape")
    out_shape = [pin(s) for s in out_shape] if isinstance(out_shape, (list, tuple)) else pin(out_shape)
    call = pl.pallas_call(body, out_shape=out_shape, **kw)

    def run(*args):
        return call(*[pltpu.with_memory_space_constraint(a, pltpu.HBM) if jnp.issubdtype(a.dtype, jnp.floating) else a
                      for a in args])

    return run


def _my_place():
    return lax.axis_index("x"), lax.axis_index("y"), lax.axis_index("c")


def _other_chips(x, y):
    return [(1 - x, y), (x, 1 - y), (1 - x, 1 - y)]


def _silu(z):
    s = jax.nn.sigmoid(z)
    return z * s, s


def _dsilu(z, s):
    return s * (1.0 + z * (1.0 - s))


_GELU_K = math.sqrt(2.0 / math.pi)
_GELU_C = 0.044715


def _gelu(y):
    th = jnp.tanh(_GELU_K * (y + _GELU_C * y * y * y))
    return 0.5 * y * (1.0 + th), th


def _dgelu(y, th):
    return 0.5 * (1.0 + th) + 0.5 * y * (1.0 - th * th) * _GELU_K * (1.0 + 3.0 * _GELU_C * y * y)


def _cast_bf16(w, name, after=None):
    r, c = w.shape
    tr = min(TR_ELT, r)
    extra = [] if after is None else [after]

    def body(w_ref, *rest):
        rest[-1][...] = w_ref[...].astype(BF16)

    return _pallas(
        body, name=name, grid=(r // tr,),
        in_specs=[pl.BlockSpec((tr, c), lambda i: (i, 0))] + [pl.BlockSpec((SUBLANES, SLAB), lambda i: (0, 0))] * len(extra),
        out_specs=pl.BlockSpec((tr, c), lambda i: (i, 0)),
        out_shape=jax.ShapeDtypeStruct((r, c), BF16),
        compiler_params=_cp(12 * tr * c),
    )(w, *extra)


def _prenorm(x, g):
    t, d = x.shape
    tm = min(TM_NORM, t)

    def body(x_ref, g_ref, h_ref):
        xv = x_ref[...]
        r = lax.rsqrt(jnp.mean(xv * xv, axis=-1, keepdims=True) + EPS)
        h_ref[...] = (xv * r * g_ref[...]).astype(BF16)

    return _pallas(
        body, name="prenorm", grid=(t // tm,),
        in_specs=[pl.BlockSpec((tm, d), lambda i: (i, 0)), pl.BlockSpec((1, d), lambda i: (0, 0))],
        out_specs=pl.BlockSpec((tm, d), lambda i: (i, 0)),
        out_shape=jax.ShapeDtypeStruct((t, d), BF16),
        compiler_params=_cp(20 * tm * d),
    )(x, g)


def _inproj_own(place, a, w_own):
    t, k = a.shape
    n = w_own.shape[1]
    tm = min(TM_PROJ, t)

    def body(place_ref, a_ref, b_ref, o_ref):
        del place_ref
        o_ref[...] = jnp.dot(a_ref[...], b_ref[...], preferred_element_type=F32)

    return _pallas(
        body, name="inproj_own",
        grid_spec=pltpu.PrefetchScalarGridSpec(
            num_scalar_prefetch=1, grid=(t // tm,),
            in_specs=[pl.BlockSpec((tm, k), lambda i, p: (i, 0)), pl.BlockSpec((k, n), lambda i, p: (0, 0))],
            out_specs=pl.BlockSpec((tm, n), lambda i, p: (i, p[1]))),
        out_shape=jax.ShapeDtypeStruct((t, N_CHIPS * n), F32),
        compiler_params=_cp(2 * (2 * tm * k + 2 * k * n + 4 * tm * n) + 4 * MIB),
    )(place, a, w_own)


def _inproj_rest(place, a, b, partial, after):
    t, k = a.shape
    nb, _, n = b.shape
    tm = min(TM_PROJ, t)
    shard = lambda s, p: (p[1] + 1 + s) % nb

    def body(place_ref, a_ref, b_ref, after_ref, partial_ref, o_ref):
        del place_ref, after_ref, partial_ref
        o_ref[...] = jnp.dot(a_ref[...], b_ref[...], preferred_element_type=F32)

    return _pallas(
        body, name="inproj_rest",
        grid_spec=pltpu.PrefetchScalarGridSpec(
            num_scalar_prefetch=1, grid=(nb - 1, t // tm),
            in_specs=[pl.BlockSpec((tm, k), lambda s, i, p: (i, 0)),
                      pl.BlockSpec((None, k, n), lambda s, i, p: (shard(s, p), 0, 0)),
                      pl.BlockSpec((SUBLANES, SLAB), lambda s, i, p: (0, 0)),
                      pl.BlockSpec(memory_space=pl.ANY)],
            out_specs=pl.BlockSpec((tm, n), lambda s, i, p: (i, shard(s, p)))),
        out_shape=jax.ShapeDtypeStruct((t, nb * n), F32),
        input_output_aliases={4: 0},
        compiler_params=_cp(2 * (2 * tm * k + 2 * k * n + 4 * tm * n) + 4 * MIB),
    )(place, a, b, after, partial)


def _matmul_tn(a, b, nb, name):
    t, m = a.shape
    n = b.shape[1] // nb
    tk = min(TK_TN, t)
    tma = min(TM_TN, m)

    def body(a_ref, b_ref, o_ref, o16_ref):
        k = pl.program_id(2)

        @pl.when(k == 0)
        def _():
            o_ref[...] = jnp.zeros_like(o_ref)

        o_ref[...] += lax.dot_general(a_ref[...], b_ref[...], (((0,), (0,)), ((), ())), preferred_element_type=F32)

        @pl.when(k == t // tk - 1)
        def _():
            o16_ref[...] = o_ref[...].astype(BF16)

    blk = pl.BlockSpec((None, tma, n), lambda j, i, k: (j, i, 0))
    return _pallas(
        body, name=name, grid=(nb, m // tma, t // tk),
        in_specs=[pl.BlockSpec((tk, tma), lambda j, i, k: (k, i)), pl.BlockSpec((tk, n), lambda j, i, k: (k, j))],
        out_specs=[blk, blk],
        out_shape=[jax.ShapeDtypeStruct((nb, m, n), F32), jax.ShapeDtypeStruct((nb, m, n), BF16)],
        compiler_params=_cp(2 * (2 * tk * tma + 2 * tk * n + 6 * tma * n) + 8 * MIB),
    )(a, b)


def _outproj(mix, w_out, x, tgt, g_post):
    t, dm = mix.shape
    d = w_out.shape[1]
    tm = min(TM_OUT, t)

    def body(mix_ref, w_ref, x_ref, t_ref, g_ref, loss_ref, dout_ref, do_ref, dmix_ref, gg_ref):
        @pl.when(pl.program_id(0) == 0)
        def _():
            loss_ref[...] = jnp.zeros_like(loss_ref)
            gg_ref[...] = jnp.zeros_like(gg_ref)

        w = w_ref[...]
        o = jnp.dot(mix_ref[...], w, preferred_element_type=F32)
        r = lax.rsqrt(jnp.mean(o * o, axis=-1, keepdims=True) + EPS)
        nh = o * r
        g = g_ref[...]
        e = x_ref[...] + nh * g - t_ref[...]
        loss_ref[...] += jnp.sum(e * e) * (0.5 / d)
        dout = e * (1.0 / d)
        dout_ref[...] = dout
        gg_ref[0:1, :] += jnp.sum(dout * nh, axis=0, keepdims=True)
        dn = dout * g
        do = r * (dn - nh * jnp.mean(dn * nh, axis=-1, keepdims=True))
        dob = do.astype(BF16)
        do_ref[...] = dob
        dmix_ref[...] = lax.dot_general(dob, w, (((1,), (1,)), ((), ())), preferred_element_type=F32)

    row = lambda i: (i, 0)
    fixed = lambda i: (0, 0)
    return _pallas(
        body, name="outproj", grid=(t // tm,),
        in_specs=[pl.BlockSpec((tm, dm), row), pl.BlockSpec((dm, d), fixed), pl.BlockSpec((tm, d), row),
                  pl.BlockSpec((tm, d), row), pl.BlockSpec((1, d), fixed)],
        out_specs=[pl.BlockSpec((SUBLANES, SLAB), fixed), pl.BlockSpec((tm, d), row), pl.BlockSpec((tm, d), row),
                   pl.BlockSpec((tm, dm), row), pl.BlockSpec((SUBLANES, d), fixed)],
        out_shape=[jax.ShapeDtypeStruct((SUBLANES, SLAB), F32), jax.ShapeDtypeStruct((t, d), F32),
                   jax.ShapeDtypeStruct((t, d), BF16), jax.ShapeDtypeStruct((t, dm), F32),
                   jax.ShapeDtypeStruct((SUBLANES, d), F32)],
        compiler_params=_cp(2 * (2 * dm * d + tm * (2 * dm + 4 * d * 3 + 2 * d + 4 * dm)) + 16 * MIB),
    )(mix, w_out, x, tgt, g_post)


def _dh_prenorm_bwd(dproj, w_in, x, dout, g_pre):
    t, d = x.shape
    nb, _, n = w_in.shape
    tm = min(TM_DH, t)

    def body(dp_ref, w_ref, x_ref, dout_ref, g_ref, dx_ref, gg_ref):
        @pl.when(pl.program_id(0) == 0)
        def _():
            gg_ref[...] = jnp.zeros_like(gg_ref)

        dh = None
        for k in range(nb):
            part = lax.dot_general(dp_ref[:, k * n:(k + 1) * n], w_ref[k], (((1,), (1,)), ((), ())),
                                   preferred_element_type=F32)
            dh = part if dh is None else dh + part
        xv = x_ref[...]
        r = lax.rsqrt(jnp.mean(xv * xv, axis=-1, keepdims=True) + EPS)
        xh = xv * r
        gg_ref[0:1, :] += jnp.sum(dh * xh, axis=0, keepdims=True)
        dg = dh * g_ref[...]
        dx_ref[...] = dout_ref[...] + r * (dg - xh * jnp.mean(dg * xh, axis=-1, keepdims=True))

    row = lambda i: (i, 0)
    fixed = lambda i: (0, 0)
    w_spec = pl.BlockSpec(w_in.shape, lambda i: (0, 0, 0), pipeline_mode=pl.Buffered(1))
    return _pallas(
        body, name="dh_prenorm_bwd", grid=(t // tm,),
        in_specs=[pl.BlockSpec((tm, nb * n), row), w_spec,
                  pl.BlockSpec((tm, d), row), pl.BlockSpec((tm, d), row), pl.BlockSpec((1, d), fixed)],
        out_specs=[pl.BlockSpec((tm, d), row), pl.BlockSpec((SUBLANES, d), fixed)],
        out_shape=[jax.ShapeDtypeStruct((t, d), F32), jax.ShapeDtypeStruct((SUBLANES, d), F32)],
        compiler_params=_cp(2 * nb * d * n + 2 * (2 * tm * nb * n + 12 * tm * d) + 16 * tm * d + 4 * MIB),
    )(dproj, w_in, x, dout, g_pre)


def _adamw(w, g, m, v, name):
    r, c = w.shape
    tr = min(TR_ELT, r)
    c1 = 1.0 - ADAM_B1 ** ADAM_STEP
    c2 = 1.0 - ADAM_B2 ** ADAM_STEP

    def body(w_ref, g_ref, m_ref, v_ref, d_ref, mo_ref, vo_ref):
        gv = g_ref[...]
        mn = ADAM_B1 * m_ref[...] + (1.0 - ADAM_B1) * gv
        vn = ADAM_B2 * v_ref[...] + (1.0 - ADAM_B2) * (gv * gv)
        d_ref[...] = -ADAM_LR * ((mn / c1) / (jnp.sqrt(vn / c2) + ADAM_EPS) + ADAM_WD * w_ref[...])
        mo_ref[...] = mn
        vo_ref[...] = vn

    spec = pl.BlockSpec((tr, c), lambda i: (i, 0))
    sds = jax.ShapeDtypeStruct((r, c), F32)
    return _pallas(
        body, name=name, grid=(r // tr,), in_specs=[spec] * 4, out_specs=[spec] * 3, out_shape=[sds] * 3,
        compiler_params=_cp(2 * 7 * 4 * tr * c + 8 * MIB),
    )(w, g, m, v)


def _adamw_small(g, w, m, v):
    r, c = w.shape
    c1 = 1.0 - ADAM_B1 ** ADAM_STEP
    c2 = 1.0 - ADAM_B2 ** ADAM_STEP

    def body(g_ref, w_ref, m_ref, v_ref, d_ref, mo_ref, vo_ref):
        gv = g_ref[...]
        mn = ADAM_B1 * m_ref[...] + (1.0 - ADAM_B1) * gv
        vn = ADAM_B2 * v_ref[...] + (1.0 - ADAM_B2) * (gv * gv)
        d_ref[...] = -ADAM_LR * ((mn / c1) / (jnp.sqrt(vn / c2) + ADAM_EPS) + ADAM_WD * w_ref[...])
        mo_ref[...] = mn
        vo_ref[...] = vn

    sds = jax.ShapeDtypeStruct((r, c), F32)
    return _pallas(
        body, name="adamw_small", out_shape=[sds] * 3,
        compiler_params=_cp(12 * 4 * r * c + 8 * MIB),
    )(g, w, m, v)


def _allreduce_small(v):
    r, c = v.shape
    assert r % (2 * SUBLANES) == 0
    h = r // 2

    def body(v_ref, out_ref, got_ref, chip_ref, slots_ref, send, recv):
        x, y, cc = _my_place()
        k = 2 * x + y
        sib = (x, y, 1 - cc)

        def rc(s, src, dst, to):
            return pltpu.make_async_remote_copy(src_ref=src, dst_ref=dst, send_sem=send.at[s], recv_sem=recv.at[s],
                                                device_id=to, device_id_type=MESH)

        pair = rc(0, v_ref, got_ref, sib)
        pair.start()
        pair.wait()
        chip_ref[...] = v_ref[...] + got_ref[...]
        mine = pl.ds(pl.multiple_of(cc * h, SUBLANES), h)
        theirs = pl.ds(pl.multiple_of((1 - cc) * h, SUBLANES), h)
        chips = _other_chips(x, y)
        sent = []
        for j, (cx, cy) in enumerate(chips):
            cp = rc(1 + j, chip_ref.at[mine], slots_ref.at[k], (cx, cy, cc))
            cp.start()
            sent.append(cp)
        slots_ref[k] = chip_ref[mine, :]
        for j, (cx, cy) in enumerate(chips):
            rc(1 + j, chip_ref.at[mine], slots_ref.at[2 * cx + cy], (cx, cy, cc)).wait_recv()
        total = slots_ref[0]
        for kk in range(1, N_CHIPS):
            total = total + slots_ref[kk]
        out_ref[mine, :] = total
        for cp in sent:
            cp.wait_send()
        share = rc(N_CHIPS, out_ref.at[mine], out_ref.at[mine], sib)
        share.start()
        rc(N_CHIPS, out_ref.at[theirs], out_ref.at[theirs], sib).wait_recv()
        share.wait_send()

    vm = pl.BlockSpec(memory_space=pltpu.VMEM)
    return _pallas(
        body, name="allreduce_small", out_shape=jax.ShapeDtypeStruct((r, c), F32), in_specs=[vm], out_specs=vm,
        scratch_shapes=[pltpu.VMEM((r, c), F32), pltpu.VMEM((r, c), F32), pltpu.VMEM((N_CHIPS, h, c), F32),
                        pltpu.SemaphoreType.DMA((N_CHIPS + 1,)), pltpu.SemaphoreType.DMA((N_CHIPS + 1,))],
        compiler_params=_cp(6 * 4 * r * c + 8 * MIB),
    )(v)


def _zoh(a_re, a_im, log_dt, b_re2, b_im2, expand):
    dt = jnp.exp(log_dt)
    mag = jnp.exp(a_re * dt)
    lbr, lbi = mag * jnp.cos(a_im * dt), mag * jnp.sin(a_im * dt)
    nr, ni = lbr - 1.0, lbi
    den = a_re * a_re + a_im * a_im
    qr = (nr * a_re + ni * a_im) / den
    qi = (ni * a_re - nr * a_im) / den
    qr2 = jnp.dot(qr, expand, precision=lax.Precision.HIGHEST, preferred_element_type=F32)
    qi2 = jnp.dot(qi, expand, precision=lax.Precision.HIGHEST, preferred_element_type=F32)
    return lbr, lbi, qr2 * b_re2 - qi2 * b_im2, qr2 * b_im2 + qi2 * b_re2


def _zoh_fwd(a_re, a_im, log_dt, b_re2, b_im2, expand, power):
    g, p = a_re.shape

    def body(ar_ref, ai_ref, ld_ref, br_ref, bi_ref, e_ref, lbr_ref, lbi_ref, bbr_ref, bbi_ref, pw_ref):
        ar, ai, ld = ar_ref[...], ai_ref[...], ld_ref[...]
        lbr, lbi, bbr, bbi = _zoh(ar, ai, ld, br_ref[...], bi_ref[...], e_ref[...])
        lbr_ref[...], lbi_ref[...], bbr_ref[...], bbi_ref[...] = lbr, lbi, bbr, bbi
        dt = jnp.exp(ld) * float(power)
        mag = jnp.exp(ar * dt)
        pw_ref[0] = mag * jnp.cos(ai * dt)
        pw_ref[1] = mag * jnp.sin(ai * dt)

    gp = jax.ShapeDtypeStruct((g, p), F32)
    gph = jax.ShapeDtypeStruct(b_re2.shape, F32)
    return _pallas(
        body, name="zoh_fwd", out_shape=[gp, gp, gph, gph, jax.ShapeDtypeStruct((2, g, p), F32)],
        compiler_params=_cp(16 * MIB),
    )(a_re, a_im, log_dt, b_re2, b_im2, expand)


def _zoh_bwd(a_re, a_im, log_dt, b_re2, b_im2, expand, g_lbr, g_lbi, g_bbr, g_bbi):
    def body(ar_ref, ai_ref, ld_ref, br_ref, bi_ref, e_ref, c0, c1, c2, c3, gar, gai, gld, gbr, gbi):
        e = e_ref[...]
        _, vjp = jax.vjp(lambda a, b, c, d, f: _zoh(a, b, c, d, f, e),
                         ar_ref[...], ai_ref[...], ld_ref[...], br_ref[...], bi_ref[...])
        gar[...], gai[...], gld[...], gbr[...], gbi[...] = vjp((c0[...], c1[...], c2[...], c3[...]))

    sds = lambda a: jax.ShapeDtypeStruct(a.shape, F32)
    return _pallas(
        body, name="zoh_bwd", out_shape=[sds(a_re), sds(a_im), sds(log_dt), sds(b_re2), sds(b_im2)],
        compiler_params=_cp(16 * MIB),
    )(a_re, a_im, log_dt, b_re2, b_im2, expand, g_lbr, g_lbi, g_bbr, g_bbi)


def _blockdiag(blocks):
    nq, _, r, c = blocks.shape
    eye = jnp.eye(GROUPS_PER_SLAB, dtype=blocks.dtype)
    out = blocks[:, :, :, None, :] * eye[None, :, None, :, None]
    return out.reshape(nq, GROUPS_PER_SLAB * r, GROUPS_PER_SLAB * c)


def _blockdiag_take(dense, r, c):
    nq = dense.shape[0]
    d5 = dense.reshape(nq, GROUPS_PER_SLAB, r, GROUPS_PER_SLAB, c)
    return jnp.stack([d5[:, i, :, i, :] for i in range(GROUPS_PER_SLAB)], axis=1)


def _scan_slab(s_ref, row0, q, lam_ref, pw_ref, car_ref, steps, reverse, prev_ref=None, prev_row0=0, glam_ref=None):
    sign = -1.0 if reverse else 1.0
    half = SLAB_STATES // SLAB
    cols = [(q * 2 * half + m, q * 2 * half + half + m, q * SLAB_STATES + m * SLAB) for m in range(half)]
    nm = len(cols)
    full = (SUBLANES, SLAB)
    lam = [(jnp.broadcast_to(lam_ref[0:1, pl.ds(cl, SLAB)], full),
            jnp.broadcast_to(sign * lam_ref[1:2, pl.ds(cl, SLAB)], full)) for (_, _, cl) in cols]

    def step_rows(jj, base):
        j = (steps - 1 - jj) if reverse else jj
        return j, pl.ds(base + j * SUBLANES, SUBLANES)

    def pass1(jj, car):
        _, rows = step_rows(jj, row0)
        out = []
        for m, (cr, ci, _) in enumerate(cols):
            sr, si = car[2 * m], car[2 * m + 1]
            lr, li = lam[m]
            nr = lr * sr - li * si + s_ref[cr, rows, :]
            ni = lr * si + li * sr + s_ref[ci, rows, :]
            s_ref[cr, rows, :] = nr
            s_ref[ci, rows, :] = ni
            out += [nr, ni]
        return tuple(out)

    def run_steps(step_fn, carry):
        for jj in range(steps):
            carry = step_fn(jj, carry)
        return carry

    ends = run_steps(pass1, tuple(jnp.zeros(full, F32) for _ in range(2 * nm)))

    entry = []
    for m, (cr, ci, cl) in enumerate(cols):
        ljr = pw_ref[0:1, pl.ds(cl, SLAB)]
        lji = sign * pw_ref[1:2, pl.ds(cl, SLAB)]
        c_r = car_ref[cr, 0:1, :]
        c_i = car_ref[ci, 0:1, :]
        rows_r, rows_i = [None] * SUBLANES, [None] * SUBLANES
        order = range(SUBLANES - 1, -1, -1) if reverse else range(SUBLANES)
        for b in order:
            rows_r[b], rows_i[b] = c_r, c_i
            e_r, e_i = ends[2 * m][b:b + 1], ends[2 * m + 1][b:b + 1]
            c_r, c_i = ljr * c_r - lji * c_i + e_r, ljr * c_i + lji * c_r + e_i
        car_ref[cr, 0:1, :] = c_r
        car_ref[ci, 0:1, :] = c_i
        entry.append((jnp.concatenate(rows_r, axis=0), jnp.concatenate(rows_i, axis=0)))

    def pass2(jj, carry):
        j, rows = step_rows(jj, row0)
        decayed, acc = carry[:2 * nm], carry[2 * nm:]
        out_d, out_a = [], []
        for m, (cr, ci, cl) in enumerate(cols):
            lr, li = lam[m]
            dr, di = decayed[2 * m], decayed[2 * m + 1]
            dr, di = lr * dr - li * di, lr * di + li * dr
            nr = s_ref[cr, rows, :] + dr
            ni = s_ref[ci, rows, :] + di
            s_ref[cr, rows, :] = nr
            s_ref[ci, rows, :] = ni
            out_d += [dr, di]
            if prev_ref is not None:
                prow = pl.ds(prev_row0 + (j - 1) * SUBLANES, SUBLANES)
                qr = prev_ref[cr, prow, :]
                qi = prev_ref[ci, prow, :]
                out_a += [acc[2 * m] + (nr * qr + ni * qi), acc[2 * m + 1] + (ni * qr - nr * qi)]
        return tuple(out_d) + tuple(out_a)

    n_acc = 2 * nm if prev_ref is not None else 0
    init = tuple(e for pair in entry for e in pair) + tuple(jnp.zeros(full, F32) for _ in range(n_acc))
    accs = run_steps(pass2, init)[2 * nm:]
    if prev_ref is not None:
        for m, (_, _, cl) in enumerate(cols):
            glam_ref[0:1, pl.ds(cl, SLAB)] += jnp.sum(accs[2 * m], axis=0, keepdims=True)
            glam_ref[1:2, pl.ds(cl, SLAB)] += jnp.sum(accs[2 * m + 1], axis=0, keepdims=True)


def _permute_rows_f32(perm_bf16, v):
    hi = v.astype(BF16)
    lo = (v - hi.astype(F32)).astype(BF16)
    return (jnp.dot(perm_bf16, hi, preferred_element_type=F32) + jnp.dot(perm_bf16, lo, preferred_element_type=F32))


def _put_slab(s_ref, rows, q, val):
    per = 2 * SLAB_STATES // SLAB
    for i in range(per):
        s_ref[q * per + i, rows, :] = val[:, i * SLAB:(i + 1) * SLAB]


def _get_slab(s_ref, rows, q):
    per = 2 * SLAB_STATES // SLAB
    return jnp.concatenate([s_ref[q * per + i, rows, :] for i in range(per)], axis=1)


def _step_major_perm(tt):
    r = jnp.arange(tt)
    held = (r % SUBLANES) * (tt // SUBLANES) + r // SUBLANES
    return held[:, None] == r[None, :]


def _ssm_fwd(proj, u_block, bq, cq, lam, pw, d_skip):
    t = proj.shape[0]
    nq, ds, w2 = bq.shape
    assert ds == SLAB and w2 == 2 * SLAB_STATES
    dssm = nq * SLAB
    width = nq * w2
    ntile = width // SLAB
    tt = min(T_SCAN, t)
    steps = tt // SUBLANES
    perm = _step_major_perm(tt)
    pm, pmt = perm.astype(BF16), perm.T.astype(BF16)

    def body(u_ref, pm_ref, pmt_ref, bq_ref, cq_ref, lam_ref, pw_ref, d_ref, y_ref, cin_ref, s_ref, car_ref):
        @pl.when(pl.program_id(0) == 0)
        def _():
            car_ref[...] = jnp.zeros_like(car_ref)

        cin_ref[...] = jnp.broadcast_to(car_ref[:, 0:1, :], cin_ref.shape)
        u = u_ref[...]
        ub = jnp.dot(pm_ref[...], u.astype(BF16), preferred_element_type=F32).astype(BF16)
        everything = slice(None)
        for q in range(nq):
            _put_slab(s_ref, everything, q,
                      jnp.dot(ub[:, q * SLAB:(q + 1) * SLAB], bq_ref[q], preferred_element_type=F32))
        for q in range(nq):
            _scan_slab(s_ref, 0, q, lam_ref, pw_ref, car_ref, steps, reverse=False)
        y_sm = jnp.concatenate(
            [jnp.dot(_get_slab(s_ref, everything, q).astype(BF16), cq_ref[q], preferred_element_type=F32)
             for q in range(nq)], axis=1)
        y_ref[...] = _permute_rows_f32(pmt_ref[...], y_sm) + d_ref[...] * u

    c3 = lambda i: (0, 0, 0)
    c2 = lambda i: (0, 0)
    return _pallas(
        body, name="ssm_fwd", grid=(t // tt,),
        in_specs=[pl.BlockSpec((tt, dssm), lambda i: (i, u_block)), pl.BlockSpec((tt, tt), c2),
                  pl.BlockSpec((tt, tt), c2), pl.BlockSpec(bq.shape, c3),
                  pl.BlockSpec(cq.shape, c3), pl.BlockSpec(lam.shape, c2), pl.BlockSpec(pw.shape, c2),
                  pl.BlockSpec((1, dssm), c2)],
        out_specs=[pl.BlockSpec((tt, dssm), lambda i: (i, 0)),
                   pl.BlockSpec((None, ntile, SUBLANES, SLAB), lambda i: (i, 0, 0, 0))],
        out_shape=[jax.ShapeDtypeStruct((t, dssm), F32), jax.ShapeDtypeStruct((t // tt, ntile, SUBLANES, SLAB), F32)],
        scratch_shapes=[pltpu.VMEM((ntile, tt, SLAB), F32), pltpu.VMEM((ntile, SUBLANES, SLAB), F32)],
        compiler_params=_cp(2 * (8 * tt * dssm + 4 * nq * ds * w2) + 4 * tt * width + 16 * MIB,
                            dimension_semantics=("arbitrary",)),
    )(proj, pm, pmt, bq, cq, lam, pw, d_skip)


def _ssm_bwd(proj, u_block, dy, cin, dproj, bq, cq, lam, pw, d_skip):
    t = proj.shape[0]
    nq, ds, w2 = bq.shape
    dssm = nq * SLAB
    width = nq * w2
    ntile = width // SLAB
    tt = min(T_SCAN, t)
    nt = t // tt
    steps = tt // SUBLANES
    halo = SUBLANES
    perm = _step_major_perm(tt)
    pm, pmt = perm.astype(BF16), perm.T.astype(BF16)

    def body(u_ref, dy_ref, cin_ref, dp_any, pm_ref, pmt_ref, bq_ref, cq_ref, lam_ref, pw_ref, d_ref,
             du_ref, gb_ref, gc_ref, glam_ref, gd_ref, s_ref, gs_ref, car_f, car_b):
        del dp_any

        @pl.when(pl.program_id(0) == 0)
        def _():
            car_b[...] = jnp.zeros_like(car_b)
            gb_ref[...] = jnp.zeros_like(gb_ref)
            gc_ref[...] = jnp.zeros_like(gc_ref)
            glam_ref[...] = jnp.zeros_like(glam_ref)
            gd_ref[...] = jnp.zeros_like(gd_ref)

        u = u_ref[...]
        dyv = dy_ref[...]
        gd_ref[0:1, :] += jnp.sum(dyv * u, axis=0, keepdims=True)
        pmv = pm_ref[...]
        ub = jnp.dot(pmv, u.astype(BF16), preferred_element_type=F32).astype(BF16)
        dyb = jnp.dot(pmv, dyv.astype(BF16), preferred_element_type=F32).astype(BF16)
        car_f[...] = cin_ref[...]
        data = slice(halo, halo + tt)
        everything = slice(None)
        for q in range(nq):
            _put_slab(s_ref, data, q, jnp.dot(ub[:, q * SLAB:(q + 1) * SLAB], bq_ref[q], preferred_element_type=F32))
        for q in range(nq):
            _scan_slab(s_ref, halo, q, lam_ref, pw_ref, car_f, steps, reverse=False)
        last_step = s_ref[:, halo + tt - SUBLANES:halo + tt, :]
        s_ref[:, 0:halo, :] = jnp.concatenate([cin_ref[:, 0:1, :], last_step[:, 0:SUBLANES - 1, :]], axis=1)
        tn = (((0,), (0,)), ((), ()))
        nt_dims = (((1,), (1,)), ((), ()))
        for q in range(nq):
            sl = slice(q * SLAB, (q + 1) * SLAB)
            gc_ref[q] += lax.dot_general(dyb[:, sl], _get_slab(s_ref, data, q).astype(BF16), tn,
                                         preferred_element_type=F32)
            _put_slab(gs_ref, everything, q,
                      lax.dot_general(dyb[:, sl], cq_ref[q], nt_dims, preferred_element_type=F32))
        for q in range(nq):
            _scan_slab(gs_ref, 0, q, lam_ref, pw_ref, car_b, steps, reverse=True,
                       prev_ref=s_ref, prev_row0=halo, glam_ref=glam_ref)
        du_parts = []
        for q in range(nq):
            sl = slice(q * SLAB, (q + 1) * SLAB)
            gsb = _get_slab(gs_ref, everything, q).astype(BF16)
            du_parts.append(lax.dot_general(gsb, bq_ref[q], nt_dims, preferred_element_type=F32))
            gb_ref[q] += lax.dot_general(ub[:, sl], gsb, tn, preferred_element_type=F32)
        du_sm = jnp.concatenate(du_parts, axis=1)
        du_ref[...] = (_permute_rows_f32(pmt_ref[...], du_sm) + dyv * d_ref[...]).astype(BF16)

    c3 = lambda i: (0, 0, 0)
    c2 = lambda i: (0, 0)
    rev = lambda i: (nt - 1 - i, 0)
    dense = jax.ShapeDtypeStruct((nq, SLAB, w2), F32)
    gp = lam.shape[1]
    return _pallas(
        body, name="ssm_bwd", grid=(nt,),
        in_specs=[pl.BlockSpec((tt, dssm), lambda i: (nt - 1 - i, u_block)), pl.BlockSpec((tt, dssm), rev),
                  pl.BlockSpec((None, ntile, SUBLANES, SLAB), lambda i: (nt - 1 - i, 0, 0, 0)),
                  pl.BlockSpec(memory_space=pl.ANY), pl.BlockSpec((tt, tt), c2), pl.BlockSpec((tt, tt), c2),
                  pl.BlockSpec(bq.shape, c3), pl.BlockSpec(cq.shape, c3),
                  pl.BlockSpec(lam.shape, c2), pl.BlockSpec(pw.shape, c2), pl.BlockSpec((1, dssm), c2)],
        out_specs=[pl.BlockSpec((tt, dssm), lambda i: (nt - 1 - i, u_block)), pl.BlockSpec(dense.shape, c3),
                   pl.BlockSpec(dense.shape, c3), pl.BlockSpec((SUBLANES, gp), c2), pl.BlockSpec((SUBLANES, dssm), c2)],
        out_shape=[jax.ShapeDtypeStruct(dproj.shape, dproj.dtype), dense, dense,
                   jax.ShapeDtypeStruct((SUBLANES, gp), F32), jax.ShapeDtypeStruct((SUBLANES, dssm), F32)],
        scratch_shapes=[pltpu.VMEM((ntile, tt + halo, SLAB), F32), pltpu.VMEM((ntile, tt, SLAB), F32),
                        pltpu.VMEM((ntile, SUBLANES, SLAB), F32), pltpu.VMEM((ntile, SUBLANES, SLAB), F32)],
        input_output_aliases={3: 0},
        compiler_params=_cp(2 * (10 * tt * dssm + 4 * nq * ds * w2 + 8 * nq * SLAB * w2)
                            + 8 * tt * width + 12 * MIB, dimension_semantics=("arbitrary",)),
    )(proj, dy, cin, dproj, pm, pmt, bq, cq, lam, pw, d_skip)


def _gate_fwd(proj, y, conv_w, conv_b, w_glu, b_glu, dc):
    t = proj.shape[0]
    dssm = y.shape[1]
    assert dc == dssm
    tm = min(TM_GATE, t)
    halo = SUBLANES

    def body(b_ref, c_ref, v_ref, zc_ref, zs_ref, y_ref, cw_ref, cb_ref, wg_ref, bg_ref, mix_ref, cv_buf):
        @pl.when(pl.program_id(0) == 0)
        def _():
            cv_buf[0:halo, :] = jnp.zeros((halo, dc), F32)

        cv = c_ref[...] * v_ref[...]
        cv_buf[halo:, :] = cv
        conv = (cb_ref[...] + cw_ref[2:3, :] * cv + cw_ref[1:2, :] * cv_buf[halo - 1:halo - 1 + tm, :]
                + cw_ref[0:1, :] * cv_buf[halo - 2:halo - 2 + tm, :])
        sz, _ = _silu(zc_ref[...])
        mix_ref[:, 0:dc] = (b_ref[...] * conv * sz).astype(BF16)
        cv_buf[0:halo, :] = cv_buf[tm:tm + halo, :]
        ge, _ = _gelu(y_ref[...])
        gl = jnp.dot(ge.astype(BF16), wg_ref[...], preferred_element_type=F32) + bg_ref[...]
        szs, _ = _silu(zs_ref[...])
        mix_ref[:, dc:] = (ge * jax.nn.sigmoid(gl) * szs).astype(BF16)

    col = lambda j: pl.BlockSpec((tm, dc), lambda i, j=j: (i, j))
    fixed = lambda i: (0, 0)
    return _pallas(
        body, name="gate_fwd", grid=(t // tm,),
        in_specs=[col(0), col(1), col(2), col(3), col(5), pl.BlockSpec((tm, dssm), lambda i: (i, 0)),
                  pl.BlockSpec(conv_w.shape, fixed), pl.BlockSpec((1, dc), fixed),
                  pl.BlockSpec(w_glu.shape, fixed), pl.BlockSpec((1, dssm), fixed)],
        out_specs=pl.BlockSpec((tm, dc + dssm), lambda i: (i, 0)),
        out_shape=jax.ShapeDtypeStruct((t, dc + dssm), BF16),
        scratch_shapes=[pltpu.VMEM((tm + halo, dc), F32)],
        compiler_params=_cp(2 * (6 * 4 * tm * dc + 2 * tm * (dc + dssm) + 2 * dssm * dssm) + 24 * tm * dc + 8 * MIB,
                            dimension_semantics=("arbitrary",)),
    )(proj, proj, proj, proj, proj, y, conv_w, conv_b, w_glu, b_glu)


def _gate_bwd(proj, y, dmix, conv_w, conv_b, w_glu, b_glu, dc):
    t = proj.shape[0]
    dssm = y.shape[1]
    tm = min(TM_GATE, t)
    nt = t // tm
    halo = SUBLANES
    blocks_per_tile = tm // halo

    rb = 2 * SUBLANES
    assert tm % rb == 0
    n_chunk = tm // rb

    def body(b_ref, c_ref, v_ref, zc_ref, zs_ref, cp_ref, vp_ref, y_ref, dm_ref, cw_ref, cb_ref, wg_ref, bg_ref,
             dp_ref, dy_ref, gs_ref, gwg_ref, cv_buf, dc_buf, sh1_buf, sh2_buf, geb_buf, dglb_buf, ge_buf, th_buf):
        i = pl.program_id(0)

        @pl.when(i == 0)
        def _():
            dc_buf[tm:, :] = jnp.zeros((halo, dc), F32)
            gs_ref[...] = jnp.zeros_like(gs_ref)
            gwg_ref[...] = jnp.zeros_like(gwg_ref)

        def rows(r):
            return pl.ds(r * rb, rb)

        def rows_after_halo(r):
            return pl.ds(halo + r * rb, rb)

        def chunks(step, carry):
            for r in range(n_chunk):
                carry = step(r, carry)
            return carry

        def fold(v):
            return v[0:SUBLANES] + v[SUBLANES:rb]

        first_tile = (i == nt - 1)
        cv_buf[0:halo, :] = jnp.where(first_tile, 0.0, cp_ref[...] * vp_ref[...])
        cv_buf[halo:, :] = c_ref[...] * v_ref[...]
        sh1_buf[...] = cv_buf[halo - 1:halo - 1 + tm, :]
        sh2_buf[...] = cv_buf[halo - 2:halo - 2 + tm, :]
        w0, w1, w2, cb = cw_ref[0:1, :], cw_ref[1:2, :], cw_ref[2:3, :], cb_ref[...]

        def conv_pass_a(r, carry):
            rs = rows(r)
            bv, zc, dyc = b_ref[rs, :], zc_ref[rs, :], dm_ref[rs, 0:dc]
            conv = cb + w2 * cv_buf[rows_after_halo(r), :] + w1 * sh1_buf[rs, :] + w0 * sh2_buf[rs, :]
            sz, sgc = _silu(zc)
            dp_ref[rs, 0:dc] = (dyc * conv * sz).astype(BF16)
            dp_ref[rs, 3 * dc:4 * dc] = (dyc * bv * conv * _dsilu(zc, sgc)).astype(BF16)
            dc_buf[rs, :] = dyc * bv * sz
            return carry

        chunks(conv_pass_a, 0)
        sh1_buf[...] = dc_buf[1:1 + tm, :]
        sh2_buf[...] = dc_buf[2:2 + tm, :]

        def conv_pass_b(r, acc):
            rs = rows(r)
            dconv, d1, d2 = dc_buf[rs, :], sh1_buf[rs, :], sh2_buf[rs, :]
            cv = cv_buf[rows_after_halo(r), :]
            dcv = w2 * dconv + w1 * d1 + w0 * d2
            dp_ref[rs, dc:2 * dc] = (dcv * v_ref[rs, :]).astype(BF16)
            dp_ref[rs, 2 * dc:3 * dc] = (dcv * c_ref[rs, :]).astype(BF16)
            return (acc[0] + fold(cv * d2), acc[1] + fold(cv * d1), acc[2] + fold(cv * dconv), acc[3] + fold(dconv))

        zero = jnp.zeros((SUBLANES, dc), F32)
        sums = chunks(conv_pass_b, (zero, zero, zero, zero))
        for k in range(4):
            gs_ref[k:k + 1, :] += jnp.sum(sums[k], axis=0, keepdims=True)
        dc_buf[tm:, :] = dc_buf[0:halo, :]

        def glu_pass_a(r, carry):
            rs = rows(r)
            ge, th = _gelu(y_ref[rs, :])
            geb_buf[rs, :] = ge.astype(BF16)
            ge_buf[rs, :] = ge
            th_buf[rs, :] = th
            return carry

        chunks(glu_pass_a, 0)
        wg = wg_ref[...]
        gl_buf = cv_buf.at[halo:halo + tm]
        keep_buf = dc_buf.at[0:tm]
        gl_buf[...] = jnp.dot(geb_buf[...], wg, preferred_element_type=F32) + bg_ref[...]
        dp_ref[:, 4 * dc:5 * dc] = jnp.zeros((tm, dc), BF16)

        def glu_pass_b(r, acc):
            rs = rows(r)
            ge = ge_buf[rs, :]
            sg = jax.nn.sigmoid(gl_buf[rs, :])
            zs, dys = zs_ref[rs, :], dm_ref[rs, dc:]
            szs, sgs = _silu(zs)
            dp_ref[rs, 5 * dc:] = (dys * ge * sg * _dsilu(zs, sgs)).astype(BF16)
            d_ys = dys * szs
            dgl = d_ys * ge * sg * (1.0 - sg)
            dglb_buf[rs, :] = dgl.astype(BF16)
            keep_buf[rs, :] = d_ys * sg
            return acc + fold(dgl)

        gs_ref[4:5, :] += jnp.sum(chunks(glu_pass_b, zero), axis=0, keepdims=True)
        dglb = dglb_buf[...]
        gwg_ref[...] += lax.dot_general(geb_buf[...], dglb, (((0,), (0,)), ((), ())), preferred_element_type=F32)
        gl_buf[...] = lax.dot_general(dglb, wg, (((1,), (1,)), ((), ())), preferred_element_type=F32)

        def glu_pass_c(r, carry):
            rs = rows(r)
            dy_ref[rs, :] = (keep_buf[rs, :] + gl_buf[rs, :]) * _dgelu(y_ref[rs, :], th_buf[rs, :])
            return carry

        chunks(glu_pass_c, 0)

    col = lambda j: pl.BlockSpec((tm, dc), lambda i, j=j: (nt - 1 - i, j))
    prev = lambda j: pl.BlockSpec((halo, dc), lambda i, j=j: (jnp.maximum((nt - 1 - i) * blocks_per_tile - 1, 0), j))
    rev = lambda i: (nt - 1 - i, 0)
    fixed = lambda i: (0, 0)
    return _pallas(
        body, name="gate_bwd", grid=(nt,),
        in_specs=[col(0), col(1), col(2), col(3), col(5), prev(1), prev(2), pl.BlockSpec((tm, dssm), rev),
                  pl.BlockSpec((tm, dc + dssm), rev), pl.BlockSpec(conv_w.shape, fixed), pl.BlockSpec((1, dc), fixed),
                  pl.BlockSpec(w_glu.shape, fixed), pl.BlockSpec((1, dssm), fixed)],
        out_specs=[pl.BlockSpec((tm, 6 * dc), rev), pl.BlockSpec((tm, dssm), rev),
                   pl.BlockSpec((2 * SUBLANES, dc), fixed), pl.BlockSpec((dssm, dssm), fixed)],
        out_shape=[jax.ShapeDtypeStruct((t, 6 * dc), BF16), jax.ShapeDtypeStruct((t, dssm), F32),
                   jax.ShapeDtypeStruct((2 * SUBLANES, dc), F32), jax.ShapeDtypeStruct((dssm, dssm), F32)],
        scratch_shapes=[pltpu.VMEM((tm + halo, dc), F32), pltpu.VMEM((tm + halo, dc), F32), pltpu.VMEM((tm, dc), F32),
                        pltpu.VMEM((tm, dc), F32), pltpu.VMEM((tm, dssm), BF16), pltpu.VMEM((tm, dssm), BF16),
                        pltpu.VMEM((tm, dssm), F32), pltpu.VMEM((tm, dssm), F32)],
        compiler_params=_cp(2 * (6 * 4 * tm * dc + 8 * tm * dc + 12 * tm * dc + 4 * tm * dc + 6 * dssm * dssm)
                            + 28 * tm * dc + 8 * MIB, dimension_semantics=("arbitrary",)),
    )(proj, proj, proj, proj, proj, proj, proj, y, dmix, conv_w, conv_b, w_glu, b_glu)


def _allgather_halves(parts, name):
    n = len(parts)
    per = 2 * (N_CHIPS - 1)

    def body(*refs):
        ins, outs = refs[:n], refs[n:2 * n]
        send, recv, local = refs[2 * n:]
        x, y, c = _my_place()
        k = 2 * x + y
        sib = (x, y, 1 - c)
        chips = _other_chips(x, y)

        def rc(t, s, src, dst, to):
            return pltpu.make_async_remote_copy(src_ref=src, dst_ref=dst, send_sem=send.at[t * per + s],
                                                recv_sem=recv.at[t * per + s], device_id=to, device_id_type=MESH)

        mine = [pltpu.make_async_copy(ins[t], outs[t].at[k], local.at[t]) for t in range(n)]
        for cp in mine:
            cp.start()
        sent = []
        for t in range(n):
            for j, (cx, cy) in enumerate(chips):
                cp = rc(t, j, ins[t].at[c], outs[t].at[k, c], (cx, cy, c))
                cp.start()
                sent.append(cp)
        for t in range(n):
            for j, (cx, cy) in enumerate(chips):
                landed = outs[t].at[2 * cx + cy, c]
                rc(t, j, landed, landed, (cx, cy, c)).wait_recv()
                cp = rc(t, N_CHIPS - 1 + j, landed, landed, sib)
                cp.start()
                sent.append(cp)
        for t in range(n):
            for j, (cx, cy) in enumerate(chips):
                other = outs[t].at[2 * cx + cy, 1 - c]
                rc(t, N_CHIPS - 1 + j, other, other, sib).wait_recv()
        for cp in sent:
            cp.wait_send()
        for cp in mine:
            cp.wait()

    any_spec = pl.BlockSpec(memory_space=pl.ANY)
    return _pallas(
        body, name=name, in_specs=[any_spec] * n, out_specs=[any_spec] * n,
        out_shape=[jax.ShapeDtypeStruct((N_CHIPS,) + p.shape, p.dtype) for p in parts],
        scratch_shapes=[pltpu.SemaphoreType.DMA((n * per,)), pltpu.SemaphoreType.DMA((n * per,)),
                        pltpu.SemaphoreType.DMA((n,))],
        compiler_params=_cp(16 * MIB),
    )(*parts)


def _allgather_flat(v, name):
    r, c = v.shape
    rels = [(dx, dy, dc) for dx in (0, 1) for dy in (0, 1) for dc in (0, 1)][1:]

    def body(v_ref, out_ref, send, recv):
        x, y, cc = _my_place()
        me = 4 * x + 2 * y + cc

        def peer(rel):
            dx, dy, dc = rel
            return (1 - x if dx else x, 1 - y if dy else y, 1 - cc if dc else cc)

        def rc(s, slot, to):
            return pltpu.make_async_remote_copy(src_ref=v_ref, dst_ref=out_ref.at[slot], send_sem=send.at[s],
                                                recv_sem=recv.at[s], device_id=to, device_id_type=MESH)

        sent = []
        for s, rel in enumerate(rels):
            cp = rc(s, me, peer(rel))
            cp.start()
            sent.append(cp)
        out_ref[me] = v_ref[...]
        for s, rel in enumerate(rels):
            px, py, pc = peer(rel)
            rc(s, 4 * px + 2 * py + pc, (px, py, pc)).wait_recv()
        for cp in sent:
            cp.wait_send()

    return _pallas(
        body, name=name, out_shape=jax.ShapeDtypeStruct((N_DEV, r, c), F32),
        in_specs=[pl.BlockSpec(memory_space=pltpu.VMEM)], out_specs=pl.BlockSpec(memory_space=pltpu.VMEM),
        scratch_shapes=[pltpu.SemaphoreType.DMA((N_DEV - 1,)), pltpu.SemaphoreType.DMA((N_DEV - 1,))],
        compiler_params=_cp((N_DEV + 2) * 4 * r * c + 8 * MIB),
    )(v)


def _rs_pair_exchange(grads, name):
    n = len(grads)

    def body(*refs):
        ins, outs = refs[:n], refs[n:2 * n]
        send, recv = refs[2 * n:]
        x, y, c = _my_place()
        cps = []
        for t in range(n):
            cp = pltpu.make_async_remote_copy(src_ref=ins[t].at[:, 1 - c], dst_ref=outs[t], send_sem=send.at[t],
                                              recv_sem=recv.at[t], device_id=(x, y, 1 - c), device_id_type=MESH)
            cp.start()
            cps.append(cp)
        for cp in cps:
            cp.wait()

    any_spec = pl.BlockSpec(memory_space=pl.ANY)
    return _pallas(
        body, name=name, in_specs=[any_spec] * n, out_specs=[any_spec] * n,
        out_shape=[jax.ShapeDtypeStruct((g.shape[0],) + g.shape[2:], g.dtype) for g in grads],
        scratch_shapes=[pltpu.SemaphoreType.DMA((n,)), pltpu.SemaphoreType.DMA((n,))],
        compiler_params=_cp(16 * MIB),
    )(*grads)


def _rs_pair_add(place, grad, got, name):
    nk, _, r2, c = grad.shape
    tr = min(TR_ELT, r2)

    def body(place_ref, g_ref, r_ref, o16_ref, o32_ref):
        del place_ref
        s = g_ref[...] + r_ref[...].astype(F32)
        o32_ref[...] = s
        o16_ref[...] = s.astype(BF16)

    blk = pl.BlockSpec((None, tr, c), lambda k, i, p: (k, i, 0))
    return _pallas(
        body, name=name,
        grid_spec=pltpu.PrefetchScalarGridSpec(
            num_scalar_prefetch=1, grid=(nk, r2 // tr),
            in_specs=[pl.BlockSpec((None, None, tr, c), lambda k, i, p: (k, p[0], i, 0)), blk],
            out_specs=[blk, blk]),
        out_shape=[jax.ShapeDtypeStruct((nk, r2, c), BF16), jax.ShapeDtypeStruct((nk, r2, c), F32)],
        compiler_params=_cp(2 * 14 * tr * c + 8 * MIB),
    )(place, grad, got)


_HBM_SPEC = pl.BlockSpec(memory_space=pltpu.HBM)
_SEM_SPEC = pl.BlockSpec(memory_space=pltpu.SEMAPHORE)
_DATAFLOW = pltpu.SideEffectType.DATAFLOW_SIDE_EFFECTING


def _split_copy_start(srcs, land_shapes, plan, n_sems, name, after=()):
    ns, nl, na = len(srcs), len(land_shapes), len(after)

    def body(*refs):
        src_refs, land_refs = refs[:ns], refs[ns:ns + nl]
        send, recv = refs[ns + nl + na], refs[ns + nl + na + 1]
        token = refs[-1]
        sends, _ = plan(src_refs, land_refs)
        for src, dst, to, si, ri in sends:
            pltpu.make_async_remote_copy(src_ref=src, dst_ref=dst, send_sem=send.at[si], recv_sem=recv.at[ri],
                                         device_id=to, device_id_type=MESH).start()
        token[...] = jnp.zeros_like(token)

    lands = [lax.empty(shp, dt) for shp, dt in land_shapes]
    through = [pltpu.HBM(a.shape, a.dtype) for a in srcs] + [pltpu.HBM(shp, dt) for shp, dt in land_shapes]
    out = pl.pallas_call(
        body, name=name,
        out_shape=(pltpu.SemaphoreType.DMA((n_sems,)), pltpu.SemaphoreType.DMA((n_sems,)), *through,
                   jax.ShapeDtypeStruct((SUBLANES, SLAB), F32)),
        in_specs=[_HBM_SPEC] * (ns + nl) + [pl.BlockSpec(memory_space=pl.ANY)] * na,
        out_specs=(_SEM_SPEC, _SEM_SPEC, *([_HBM_SPEC] * (ns + nl)), pl.BlockSpec(memory_space=pltpu.VMEM)),
        input_output_aliases={i: 2 + i for i in range(ns + nl)},
        compiler_params=pltpu.CompilerParams(has_side_effects=_DATAFLOW),
    )(*[pltpu.with_memory_space_constraint(a, pltpu.HBM) for a in (*srcs, *lands)], *after)
    return out[0], out[1], list(out[2:2 + ns]), list(out[2 + ns:2 + ns + nl]), out[-1]


def _split_copy_wait(send, recv, srcs, lands, plan, after, name):
    ns, nl, na = len(srcs), len(lands), len(after)

    def body(*refs):
        src_refs, land_refs = refs[:ns], refs[ns:ns + nl]
        send_ref, recv_ref = refs[ns + nl], refs[ns + nl + 1]
        sends, arrivals = plan(src_refs, land_refs)
        for src, dst, to, si, ri in sends:
            pltpu.make_async_remote_copy(src_ref=src, dst_ref=dst, send_sem=send_ref.at[si], recv_sem=recv_ref.at[ri],
                                         device_id=to, device_id_type=MESH).wait_send()
        for (src, _, to, si, _), (view, ri) in zip(sends, arrivals):
            pltpu.make_async_remote_copy(src_ref=view, dst_ref=view, send_sem=send_ref.at[si], recv_sem=recv_ref.at[ri],
                                         device_id=to, device_id_type=MESH).wait_recv()

    out = pl.pallas_call(
        body, name=name,
        out_shape=[pltpu.HBM(a.shape, a.dtype) for a in (*srcs, *lands)],
        in_specs=[_HBM_SPEC] * (ns + nl) + [_SEM_SPEC, _SEM_SPEC] + [pl.BlockSpec(memory_space=pl.ANY)] * na,
        out_specs=[_HBM_SPEC] * (ns + nl),
        input_output_aliases={i: i for i in range(ns + nl)},
        compiler_params=pltpu.CompilerParams(has_side_effects=_DATAFLOW),
    )(*srcs, *lands, send, recv, *after)
    return list(out[:ns]), list(out[ns:])


def _chip_exchange_plan(n):
    per = N_CHIPS - 1

    def plan(srcs, lands):
        x, y, c = _my_place()
        sends, arrivals = [], []
        for t in range(n):
            for j, (cx, cy) in enumerate(_other_chips(x, y)):
                sends.append((srcs[t].at[2 * cx + cy], lands[t].at[j], (cx, cy, c), t * per + j, t * per + j))
                arrivals.append((lands[t].at[j], t * per + j))
        return sends, arrivals

    return plan


def _gather_half_plan(n):
    per = N_CHIPS - 1

    def plan(srcs, lands):
        x, y, c = _my_place()
        k = 2 * x + y
        sends, arrivals = [], []
        for t in range(n):
            for j, (cx, cy) in enumerate(_other_chips(x, y)):
                sends.append((srcs[t].at[c], lands[t].at[k, c], (cx, cy, c), t * per + j, t * per + j))
                arrivals.append((lands[t].at[2 * cx + cy, c], t * per + j))
        return sends, arrivals

    return plan


def _gather_complete(own, landed, name):
    def body(own_ref, in_ref, out_ref, send, recv, local):
        x, y, c = _my_place()
        sib = (x, y, 1 - c)
        mine = pltpu.make_async_copy(own_ref, out_ref.at[2 * x + y], local)
        mine.start()
        sent = []
        for j, (cx, cy) in enumerate(_other_chips(x, y)):
            half = (2 * cx + cy, c)
            cp = pltpu.make_async_remote_copy(src_ref=in_ref.at[half], dst_ref=out_ref.at[half], send_sem=send.at[j],
                                              recv_sem=recv.at[j], device_id=sib, device_id_type=MESH)
            cp.start()
            sent.append(cp)
        for j, (cx, cy) in enumerate(_other_chips(x, y)):
            other = out_ref.at[2 * cx + cy, 1 - c]
            pltpu.make_async_remote_copy(src_ref=other, dst_ref=other, send_sem=send.at[j], recv_sem=recv.at[j],
                                         device_id=sib, device_id_type=MESH).wait_recv()
        for cp in sent:
            cp.wait_send()
        mine.wait()

    any_spec = pl.BlockSpec(memory_space=pl.ANY)
    per = N_CHIPS - 1
    return _pallas(
        body, name=name, in_specs=[any_spec, any_spec], out_specs=any_spec,
        out_shape=jax.ShapeDtypeStruct(landed.shape, landed.dtype), input_output_aliases={1: 0},
        scratch_shapes=[pltpu.SemaphoreType.DMA((per,)), pltpu.SemaphoreType.DMA((per,)), pltpu.SemaphoreType.DMA],
        compiler_params=_cp(16 * MIB),
    )(own, landed)


def _gather_direct_plan(n):
    per = 2 * (N_CHIPS - 1)

    def plan(srcs, lands):
        x, y, c = _my_place()
        k = 2 * x + y
        sends, arrivals = [], []
        for t in range(n):
            for j, (cx, cy) in enumerate(_other_chips(x, y)):
                for core in (0, 1):
                    sends.append((srcs[t].at[c], lands[t].at[k, c], (cx, cy, core),
                                  t * per + 2 * j + core, t * per + 2 * j + c))
                    arrivals.append((lands[t].at[2 * cx + cy, core], t * per + 2 * j + core))
        return sends, arrivals

    return plan


def _rs_chip_add(place, sums32, got, name):
    _, r2, c = sums32.shape
    tr = min(TR_ELT, r2)

    def body(place_ref, s_ref, q_ref, o_ref):
        del place_ref
        o_ref[...] = ((s_ref[...] + q_ref[0].astype(F32)) + q_ref[1].astype(F32)) + q_ref[2].astype(F32)

    return _pallas(
        body, name=name,
        grid_spec=pltpu.PrefetchScalarGridSpec(
            num_scalar_prefetch=1, grid=(r2 // tr,),
            in_specs=[pl.BlockSpec((None, tr, c), lambda i, p: (p[1], i, 0)),
                      pl.BlockSpec((N_CHIPS - 1, tr, c), lambda i, p: (0, i, 0))],
            out_specs=pl.BlockSpec((tr, c), lambda i, p: (i, 0))),
        out_shape=jax.ShapeDtypeStruct((r2, c), F32),
        compiler_params=_cp(2 * 14 * tr * c + 8 * MIB),
    )(place, sums32, got)


def _rs_pair_share(halves, name):
    n = len(halves)

    def body(*refs):
        ins, outs = refs[:n], refs[n:2 * n]
        send, recv, local = refs[2 * n:]
        x, y, c = _my_place()
        cps, mine = [], []
        for t in range(n):
            lc = pltpu.make_async_copy(ins[t], outs[t].at[c], local.at[t])
            lc.start()
            mine.append(lc)
            cp = pltpu.make_async_remote_copy(src_ref=ins[t], dst_ref=outs[t].at[c], send_sem=send.at[t],
                                              recv_sem=recv.at[t], device_id=(x, y, 1 - c), device_id_type=MESH)
            cp.start()
            cps.append(cp)
        for t in range(n):
            other = outs[t].at[1 - c]
            pltpu.make_async_remote_copy(src_ref=other, dst_ref=other, send_sem=send.at[t], recv_sem=recv.at[t],
                                         device_id=(x, y, 1 - c), device_id_type=MESH).wait_recv()
        for cp in cps:
            cp.wait_send()
        for lc in mine:
            lc.wait()

    any_spec = pl.BlockSpec(memory_space=pl.ANY)
    return _pallas(
        body, name=name, in_specs=[any_spec] * n, out_specs=[any_spec] * n,
        out_shape=[jax.ShapeDtypeStruct((2,) + h.shape, F32) for h in halves],
        scratch_shapes=[pltpu.SemaphoreType.DMA((n,)), pltpu.SemaphoreType.DMA((n,)), pltpu.SemaphoreType.DMA((n,))],
        compiler_params=_cp(16 * MIB),
    )(*halves)


_PACK_TILE = SUBLANES * SLAB


def _pack(arrays):
    rows = []
    for a in arrays:
        flat = a.reshape(-1).astype(F32)
        padded = -(-flat.shape[0] // _PACK_TILE) * _PACK_TILE
        rows.append(jnp.pad(flat, (0, padded - flat.shape[0])).reshape(-1, SLAB))
    n_rows = sum(r.shape[0] for r in rows)
    if n_rows % (2 * SUBLANES):
        rows.append(jnp.zeros((SUBLANES, SLAB), F32))
    return jnp.concatenate(rows, axis=0)


def _unpack(packed, shapes):
    out, row = [], 0
    for shp in shapes:
        size = math.prod(shp)
        nrow = -(-size // _PACK_TILE) * SUBLANES
        out.append(packed[row:row + nrow].reshape(-1)[:size].reshape(shp))
        row += nrow
    return out


def kernel(x, norm_pre_g, w_in, conv_w, conv_b, ssm_a_re, ssm_a_im, ssm_log_dt, ssm_b_re, ssm_b_im, ssm_c_re, ssm_c_im, ssm_d, w_glu, b_glu, w_out, norm_post_g, loss_target, m_norm_pre_g, m_w_in, m_conv_w, m_conv_b, m_ssm_a_re, m_ssm_a_im, m_ssm_log_dt, m_ssm_b_re, m_ssm_b_im, m_ssm_c_re, m_ssm_c_im, m_ssm_d, m_w_glu, m_b_glu, m_w_out, m_norm_post_g, v_norm_pre_g, v_w_in, v_conv_w, v_conv_b, v_ssm_a_re, v_ssm_a_im, v_ssm_log_dt, v_ssm_b_re, v_ssm_b_im, v_ssm_c_re, v_ssm_c_im, v_ssm_d, v_w_glu, v_b_glu, v_w_out, v_norm_post_g):
    xs, tgt = x[0], loss_target[0]
    t, d = xs.shape
    dc = conv_b.shape[0]
    dssm = ssm_d.shape[0]
    g, p = ssm_a_re.shape
    h = SSM_H
    nq = dssm // SLAB
    n_shard = w_in.shape[1]
    steps = min(T_SCAN, t) // SUBLANES
    mx, my, mc = _my_place()
    chip = 2 * mx + my
    place = jnp.stack([mc, chip]).astype(jnp.int32)

    cw_cols = conv_w.shape[1]
    cw_pad = -(-cw_cols // SLAB) * SLAB
    cw_blk = jnp.zeros((SUBLANES, cw_pad), F32).at[:conv_w.shape[0], :cw_cols].set(conv_w)
    cw_all = _allgather_flat(cw_blk, "allgather_conv_w")
    conv_w_full = jnp.concatenate([cw_all[2 * k, :, :cw_cols] for k in range(N_CHIPS)], axis=1)

    halves = lambda a: a.reshape(2, a.shape[0] // 2, a.shape[1])
    win_half = halves(_cast_bf16(w_in, "cast_w_in", cw_all[0, :, :SLAB]))
    win_plan = _gather_half_plan(1)
    win_send, win_recv, win_srcs, win_lands, win_token = _split_copy_start(
        [win_half], [((N_CHIPS,) + win_half.shape, BF16)], win_plan, N_CHIPS - 1, "gather_w_in_start")
    behind_w_in = win_token[0:1, 0:1]

    small_names = ["norm_pre_g", "conv_w", "conv_b", "ssm_a_re", "ssm_a_im", "ssm_log_dt", "ssm_b_re", "ssm_b_im",
                   "ssm_c_re", "ssm_c_im", "ssm_d", "b_glu", "norm_post_g", "loss"]
    zeros_cw = jnp.zeros((conv_w.shape[0], dc), F32)
    one0 = jnp.zeros((1,), F32)
    small_w = dict(norm_pre_g=norm_pre_g, conv_w=zeros_cw, conv_b=conv_b, ssm_a_re=ssm_a_re, ssm_a_im=ssm_a_im,
                   ssm_log_dt=ssm_log_dt, ssm_b_re=ssm_b_re, ssm_b_im=ssm_b_im, ssm_c_re=ssm_c_re, ssm_c_im=ssm_c_im,
                   ssm_d=ssm_d, b_glu=b_glu, norm_post_g=norm_post_g, loss=one0)
    small_m = dict(norm_pre_g=m_norm_pre_g, conv_w=zeros_cw, conv_b=m_conv_b, ssm_a_re=m_ssm_a_re, ssm_a_im=m_ssm_a_im,
                   ssm_log_dt=m_ssm_log_dt, ssm_b_re=m_ssm_b_re, ssm_b_im=m_ssm_b_im, ssm_c_re=m_ssm_c_re,
                   ssm_c_im=m_ssm_c_im, ssm_d=m_ssm_d, b_glu=m_b_glu, norm_post_g=m_norm_post_g, loss=one0)
    small_v = dict(norm_pre_g=v_norm_pre_g, conv_w=zeros_cw, conv_b=v_conv_b, ssm_a_re=v_ssm_a_re, ssm_a_im=v_ssm_a_im,
                   ssm_log_dt=v_ssm_log_dt, ssm_b_re=v_ssm_b_re, ssm_b_im=v_ssm_b_im, ssm_c_re=v_ssm_c_re,
                   ssm_c_im=v_ssm_c_im, ssm_d=v_ssm_d, b_glu=v_b_glu, norm_post_g=v_norm_post_g, loss=one0)
    w_pack, m_pack, v_pack = [_pack([group[nm] for nm in small_names]) + behind_w_in
                              for group in (small_w, small_m, small_v)]
    side = [halves(_cast_bf16(w_out, "cast_w_out", win_token)), halves(_cast_bf16(w_glu, "cast_w_glu", win_token))]

    expand = jnp.repeat(jnp.eye(p, dtype=F32), h, axis=1)
    b_re2, b_im2 = ssm_b_re.reshape(g, p * h), ssm_b_im.reshape(g, p * h)
    log_dt2 = ssm_log_dt.reshape(g, 1) + behind_w_in
    lbr, lbi, bbr2, bbi2, pw3 = _zoh_fwd(ssm_a_re, ssm_a_im, log_dt2, b_re2, b_im2, expand, steps)
    lam = jnp.stack([lbr.reshape(g * p), lbi.reshape(g * p)])
    pw = pw3.reshape(2, g * p)
    to_slab_b = lambda b2: _blockdiag(b2.reshape(nq, GROUPS_PER_SLAB, p, h).transpose(0, 1, 3, 2))
    bq = jnp.concatenate([to_slab_b(bbr2), to_slab_b(bbi2)], axis=2).astype(BF16)
    to_slab_c = lambda c3: _blockdiag(c3.reshape(nq, GROUPS_PER_SLAB, h, p).transpose(0, 1, 3, 2))
    cq = jnp.concatenate([to_slab_c(ssm_c_re), to_slab_c(-ssm_c_im)], axis=1).astype(BF16)

    g_pre2, g_post2 = norm_pre_g.reshape(1, d), norm_post_g.reshape(1, d)
    conv_b2, b_glu2, d_skip2 = conv_b.reshape(1, dc), b_glu.reshape(1, dssm), ssm_d.reshape(1, dssm)
    u_block = 4 * dc // dssm

    hb = _prenorm(xs, g_pre2 + behind_w_in)
    proj_own = _inproj_own(place, hb, win_srcs[0].reshape(d, n_shard))
    win_own, win_landed = _split_copy_wait(win_send, win_recv, win_srcs, win_lands, win_plan,
                                           [proj_own, bq, cq, w_pack, m_pack, v_pack, *side], "gather_w_in_wait")
    win_g = _gather_complete(win_own[0], win_landed[0], "gather_w_in_complete")
    win_b = win_g.reshape(N_CHIPS, d, n_shard)
    side_plan = _gather_direct_plan(len(side))
    side_sems = 2 * (N_CHIPS - 1) * len(side)
    ag_send, ag_recv, ag_srcs, ag_lands, ag_token = _split_copy_start(
        side, [((N_CHIPS,) + a.shape, BF16) for a in side], side_plan, side_sems, "gather_side_weights_start",
        after=[win_g])
    proj = _inproj_rest(place, hb, win_b, proj_own, ag_token)
    y, cin = _ssm_fwd(proj, u_block, bq, cq, lam, pw, d_skip2)
    side_own, side_all = _split_copy_wait(ag_send, ag_recv, ag_srcs, ag_lands, side_plan, [cin],
                                          "gather_side_weights_wait")
    wout_g, wglu_g = [lax.dynamic_update_index_in_dim(all_, own, chip, 0) for own, all_ in zip(side_own, side_all)]
    wout_b = wout_g.reshape(dc + dssm, d)
    wglu_b = wglu_g.reshape(dssm, dssm)
    mix = _gate_fwd(proj, y, conv_w_full, conv_b2, wglu_b, b_glu2, dc)
    loss_blk, dout, dob, dmix, gg_post = _outproj(mix, wout_b, xs, tgt, g_post2)

    def reduce_start(grads, to_send, tags, group):
        got = _rs_pair_exchange(to_send, "rs_pair_exchange_" + group)
        sums = [_rs_pair_add(place, gt, rt, "rs_pair_add_" + tg) for gt, rt, tg in zip(grads, got, tags)]
        plan = _chip_exchange_plan(len(grads))
        started = _split_copy_start([s16 for s16, _ in sums],
                                    [((N_CHIPS - 1,) + s16.shape[1:], BF16) for s16, _ in sums], plan,
                                    (N_CHIPS - 1) * len(grads), "rs_chip_exchange_" + group + "_start")
        return plan, started, [s32 for _, s32 in sums]

    def reduce_finish(plan, started, sums32, tags, group, after):
        send, recv, srcs, lands, _ = started
        _, landed = _split_copy_wait(send, recv, srcs, lands, plan, after, "rs_chip_exchange_" + group + "_wait")
        mine = [_rs_chip_add(place, s32, qt, "rs_chip_add_" + tg) for s32, qt, tg in zip(sums32, landed, tags)]
        full = _rs_pair_share(mine, "rs_pair_share_" + group)
        return [f.reshape(2 * f.shape[1], f.shape[2]) for f in full]

    gw_out, gw_out16 = _matmul_tn(mix, dob, 1, "grad_w_out")
    dproj, dy, gsmall, gw_glu = _gate_bwd(proj, y, dmix, conv_w_full, conv_b2, wglu_b, b_glu2, dc)
    as_out = lambda a: a.reshape(N_CHIPS, 2, (dc + dssm) // (2 * N_CHIPS), d)
    gw_glu4 = gw_glu.reshape(N_CHIPS, 2, dssm // (2 * N_CHIPS), dssm)
    rs_a = reduce_start([as_out(gw_out), gw_glu4], [as_out(gw_out16), gw_glu4], ["w_out", "w_glu"], "a")
    dproj, gb_dense, gc_dense, glam, gd = _ssm_bwd(proj, u_block, dy, cin, dproj, bq, cq, lam, pw,
                                                   d_skip2 + rs_a[1][4][0:1, 0:1])
    gw_in, gw_in16 = _matmul_tn(hb, dproj, N_CHIPS, "grad_w_in")
    as_in = lambda a: a.reshape(N_CHIPS, 2, d // 2, n_shard)
    rs_b = reduce_start([as_in(gw_in)], [as_in(gw_in16)], ["w_in"], "b")
    gx, gg_pre = _dh_prenorm_bwd(dproj, win_b, xs, dout, g_pre2 + rs_b[1][4][0:1, 0:1])

    gb4 = gb_dense.reshape(nq, SLAB, 2, SLAB_STATES)
    g_bbr2 = _blockdiag_take(gb4[:, :, 0, :], h, p).transpose(0, 1, 3, 2).reshape(g, p * h)
    g_bbi2 = _blockdiag_take(gb4[:, :, 1, :], h, p).transpose(0, 1, 3, 2).reshape(g, p * h)
    gc4 = gc_dense.reshape(nq, SLAB, 2, SLAB_STATES)
    g_c_re = _blockdiag_take(gc4[:, :, 0, :], h, p).reshape(g, h, p)
    g_c_im = -_blockdiag_take(gc4[:, :, 1, :], h, p).reshape(g, h, p)
    g_a_re, g_a_im, g_ld, g_b_re2, g_b_im2 = _zoh_bwd(
        ssm_a_re, ssm_a_im, log_dt2, b_re2, b_im2, expand,
        glam[0].reshape(g, p), glam[1].reshape(g, p), g_bbr2, g_bbi2)

    small_g = {
        "norm_pre_g": gg_pre[0], "conv_w": gsmall[0:3], "conv_b": gsmall[3], "ssm_a_re": g_a_re, "ssm_a_im": g_a_im,
        "ssm_log_dt": g_ld.reshape(g), "ssm_b_re": g_b_re2.reshape(g, p, h), "ssm_b_im": g_b_im2.reshape(g, p, h),
        "ssm_c_re": g_c_re, "ssm_c_im": g_c_im, "ssm_d": gd[0], "b_glu": gsmall[4], "norm_post_g": gg_post[0],
        "loss": loss_blk[0, 0:1],
    }
    shapes = [small_w[nm].shape for nm in small_names]
    g_pack = _allreduce_small(_pack([small_g[nm] for nm in small_names]))
    packs = _adamw_small(g_pack, w_pack, m_pack, v_pack)
    sg, sd, sm, sv = [dict(zip(small_names, _unpack(pk, shapes))) for pk in (g_pack, *packs)]
    loss = sg["loss"][0]

    g_cw = lax.dynamic_slice_in_dim(sg["conv_w"], chip * cw_cols, cw_cols, axis=1)
    pad_cw = lambda a: jnp.zeros((SUBLANES, cw_pad), F32).at[:a.shape[0], :cw_cols].set(a)
    cut_cw = lambda a: a[:conv_w.shape[0], :cw_cols]
    d_cw, m_cw, v_cw = [cut_cw(a) for a in _adamw(pad_cw(conv_w), pad_cw(g_cw), pad_cw(m_conv_w), pad_cw(v_conv_w),
                                                  "adamw_conv_w")]

    g_wout, g_wglu = reduce_finish(*rs_a, ["w_out", "w_glu"], "a", [d_cw, packs[0]])
    (g_win,) = reduce_finish(*rs_b, ["w_in"], "b", [g_wout])
    d_win, m_win, v_win = _adamw(w_in, g_win, m_w_in, v_w_in, "adamw_w_in")
    d_wout, m_wout, v_wout = _adamw(w_out, g_wout, m_w_out, v_w_out, "adamw_w_out")
    d_wglu, m_wglu, v_wglu = _adamw(w_glu, g_wglu, m_w_glu, v_w_glu, "adamw_w_glu")

    order = ["norm_pre_g", "w_in", "conv_w", "conv_b", "ssm_a_re", "ssm_a_im", "ssm_log_dt", "ssm_b_re", "ssm_b_im",
             "ssm_c_re", "ssm_c_im", "ssm_d", "w_glu", "b_glu", "w_out", "norm_post_g"]
    grads, deltas, new_m, new_v = dict(sg), dict(sd), dict(sm), dict(sv)
    grads.update(w_in=g_win, w_out=g_wout, w_glu=g_wglu, conv_w=g_cw)
    deltas.update(w_in=d_win, w_out=d_wout, w_glu=d_wglu, conv_w=d_cw)
    new_m.update(w_in=m_win, w_out=m_wout, w_glu=m_wglu, conv_w=m_cw)
    new_v.update(w_in=v_win, w_out=v_wout, w_glu=v_wglu, conv_w=v_cw)
    return (loss, gx[None], *[grads[nm] for nm in order], *[deltas[nm] for nm in order],
            *[new_m[nm] for nm in order], *[new_v[nm] for nm in order])
```

```python
import functools
import math

import jax
import jax.numpy as jnp
from jax import lax
from jax.experimental import pallas as pl
from jax.experimental.pallas import tpu as pltpu

F32 = jnp.float32
BF16 = jnp.bfloat16
MESH = pl.DeviceIdType.MESH

EPS = 1e-6
SSM_H = 16
SSM_P = 64
GROUPS_PER_SLAB = 8
SLAB = 128
SLAB_STATES = GROUPS_PER_SLAB * SSM_P
N_CHIPS = 4
N_DEV = 8

ADAM_LR = 0.001
ADAM_B1 = 0.9
ADAM_B2 = 0.999
ADAM_EPS = 1e-08
ADAM_WD = 0.01
ADAM_STEP = 10

MIB = 1024 * 1024
VMEM_CAP = 48 * MIB
SUBLANES = 8

TM_NORM = 512
TM_PROJ = 512
TM_GATE = 256
TM_OUT = 256
TM_DH = 256
T_SCAN = 256
TK_TN = 512
TM_TN = 1024
TR_ELT = 256


def _cp(vmem_bytes, **kw):
    return pltpu.CompilerParams(vmem_limit_bytes=int(min(VMEM_CAP, max(16 * MIB, vmem_bytes))), **kw)


def _pallas(body, **kw):
    if "grid" not in kw and "grid_spec" not in kw:
        return pl.pallas_call(body, **kw)
    pin = lambda s: pltpu.HBM(s.shape, s.dtype) if isinstance(s, jax.ShapeDtypeStruct) else s
    out_shape = kw.pop("out_shape")
    out_shape = [pin(s) for s in out_shape] if isinstance(out_shape, (list, tuple)) else pin(out_shape)
    call = pl.pallas_call(body, out_shape=out_shape, **kw)

    def run(*args):
        return call(*[pltpu.with_memory_space_constraint(a, pltpu.HBM) if jnp.issubdtype(a.dtype, jnp.floating) else a
                      for a in args])

    return run


def _my_place():
    return lax.axis_index("x"), lax.axis_index("y"), lax.axis_index("c")


def _other_chips(x, y):
    return [(1 - x, y), (x, 1 - y), (1 - x, 1 - y)]


def _silu(z):
    s = jax.nn.sigmoid(z)
    return z * s, s


def _dsilu(z, s):
    return s * (1.0 + z * (1.0 - s))


_GELU_K = math.sqrt(2.0 / math.pi)
_GELU_C = 0.044715


def _gelu(y):
    th = jnp.tanh(_GELU_K * (y + _GELU_C * y * y * y))
    return 0.5 * y * (1.0 + th), th


def _dgelu(y, th):
    return 0.5 * (1.0 + th) + 0.5 * y * (1.0 - th * th) * _GELU_K * (1.0 + 3.0 * _GELU_C * y * y)


def _cast_bf16(w, name, after=None):
    r, c = w.shape
    tr = min(TR_ELT, r)
    extra = [] if after is None else [after]

    def body(w_ref, *rest):
        rest[-1][...] = w_ref[...].astype(BF16)

    return _pallas(
        body, name=name, grid=(r // tr,),
        in_specs=[pl.BlockSpec((tr, c), lambda i: (i, 0))] + [pl.BlockSpec((SUBLANES, SLAB), lambda i: (0, 0))] * len(extra),
        out_specs=pl.BlockSpec((tr, c), lambda i: (i, 0)),
        out_shape=jax.ShapeDtypeStruct((r, c), BF16),
        compiler_params=_cp(12 * tr * c),
    )(w, *extra)


def _prenorm(x, g):
    t, d = x.shape
    tm = min(TM_NORM, t)

    def body(x_ref, g_ref, h_ref):
        xv = x_ref[...]
        r = lax.rsqrt(jnp.mean(xv * xv, axis=-1, keepdims=True) + EPS)
        h_ref[...] = (xv * r * g_ref[...]).astype(BF16)

    return _pallas(
        body, name="prenorm", grid=(t // tm,),
        in_specs=[pl.BlockSpec((tm, d), lambda i: (i, 0)), pl.BlockSpec((1, d), lambda i: (0, 0))],
        out_specs=pl.BlockSpec((tm, d), lambda i: (i, 0)),
        out_shape=jax.ShapeDtypeStruct((t, d), BF16),
        compiler_params=_cp(20 * tm * d),
    )(x, g)


def _inproj_own(place, a, w_own):
    t, k = a.shape
    n = w_own.shape[1]
    tm = min(TM_PROJ, t)

    def body(place_ref, a_ref, b_ref, o_ref):
        del place_ref
        o_ref[...] = jnp.dot(a_ref[...], b_ref[...], preferred_element_type=F32)

    return _pallas(
        body, name="inproj_own",
        grid_spec=pltpu.PrefetchScalarGridSpec(
            num_scalar_prefetch=1, grid=(t // tm,),
            in_specs=[pl.BlockSpec((tm, k), lambda i, p: (i, 0)), pl.BlockSpec((k, n), lambda i, p: (0, 0))],
            out_specs=pl.BlockSpec((tm, n), lambda i, p: (i, p[1]))),
        out_shape=jax.ShapeDtypeStruct((t, N_CHIPS * n), F32),
        compiler_params=_cp(2 * (2 * tm * k + 2 * k * n + 4 * tm * n) + 4 * MIB),
    )(place, a, w_own)


def _inproj_rest(place, a, b, partial, after):
    t, k = a.shape
    nb, _, n = b.shape
    tm = min(TM_PROJ, t)
    shard = lambda s, p: (p[1] + 1 + s) % nb

    def body(place_ref, a_ref, b_ref, after_ref, partial_ref, o_ref):
        del place_ref, after_ref, partial_ref
        o_ref[...] = jnp.dot(a_ref[...], b_ref[...], preferred_element_type=F32)

    return _pallas(
        body, name="inproj_rest",
        grid_spec=pltpu.PrefetchScalarGridSpec(
            num_scalar_prefetch=1, grid=(nb - 1, t // tm),
            in_specs=[pl.BlockSpec((tm, k), lambda s, i, p: (i, 0)),
                      pl.BlockSpec((None, k, n), lambda s, i, p: (shard(s, p), 0, 0)),
                      pl.BlockSpec((SUBLANES, SLAB), lambda s, i, p: (0, 0)),
                      pl.BlockSpec(memory_space=pl.ANY)],
            out_specs=pl.BlockSpec((tm, n), lambda s, i, p: (i, shard(s, p)))),
        out_shape=jax.ShapeDtypeStruct((t, nb * n), F32),
        input_output_aliases={4: 0},
        compiler_params=_cp(2 * (2 * tm * k + 2 * k * n + 4 * tm * n) + 4 * MIB),
    )(place, a, b, after, partial)


def _matmul_tn(a, b, nb, name):
    t, m = a.shape
    n = b.shape[1] // nb
    tk = min(TK_TN, t)
    tma = min(TM_TN, m)

    def body(a_ref, b_ref, o_ref, o16_ref):
        k = pl.program_id(2)

        @pl.when(k == 0)
        def _():
            o_ref[...] = jnp.zeros_like(o_ref)

        o_ref[...] += lax.dot_general(a_ref[...], b_ref[...], (((0,), (0,)), ((), ())), preferred_element_type=F32)

        @pl.when(k == t // tk - 1)
        def _():
            o16_ref[...] = o_ref[...].astype(BF16)

    blk = pl.BlockSpec((None, tma, n), lambda j, i, k: (j, i, 0))
    return _pallas(
        body, name=name, grid=(nb, m // tma, t // tk),
        in_specs=[pl.BlockSpec((tk, tma), lambda j, i, k: (k, i)), pl.BlockSpec((tk, n), lambda j, i, k: (k, j))],
        out_specs=[blk, blk],
        out_shape=[jax.ShapeDtypeStruct((nb, m, n), F32), jax.ShapeDtypeStruct((nb, m, n), BF16)],
        compiler_params=_cp(2 * (2 * tk * tma + 2 * tk * n + 6 * tma * n) + 8 * MIB),
    )(a, b)


def _outproj(mix, w_out, x, tgt, g_post):
    t, dm = mix.shape
    d = w_out.shape[1]
    tm = min(TM_OUT, t)

    def body(mix_ref, w_ref, x_ref, t_ref, g_ref, loss_ref, dout_ref, do_ref, dmix_ref, gg_ref):
        @pl.when(pl.program_id(0) == 0)
        def _():
            loss_ref[...] = jnp.zeros_like(loss_ref)
            gg_ref[...] = jnp.zeros_like(gg_ref)

        w = w_ref[...]
        o = jnp.dot(mix_ref[...], w, preferred_element_type=F32)
        r = lax.rsqrt(jnp.mean(o * o, axis=-1, keepdims=True) + EPS)
        nh = o * r
        g = g_ref[...]
        e = x_ref[...] + nh * g - t_ref[...]
        loss_ref[...] += jnp.sum(e * e) * (0.5 / d)
        dout = e * (1.0 / d)
        dout_ref[...] = dout
        gg_ref[0:1, :] += jnp.sum(dout * nh, axis=0, keepdims=True)
        dn = dout * g
        do = r * (dn - nh * jnp.mean(dn * nh, axis=-1, keepdims=True))
        dob = do.astype(BF16)
        do_ref[...] = dob
        dmix_ref[...] = lax.dot_general(dob, w, (((1,), (1,)), ((), ())), preferred_element_type=F32)

    row = lambda i: (i, 0)
    fixed = lambda i: (0, 0)
    return _pallas(
        body, name="outproj", grid=(t // tm,),
        in_specs=[pl.BlockSpec((tm, dm), row), pl.BlockSpec((dm, d), fixed), pl.BlockSpec((tm, d), row),
                  pl.BlockSpec((tm, d), row), pl.BlockSpec((1, d), fixed)],
        out_specs=[pl.BlockSpec((SUBLANES, SLAB), fixed), pl.BlockSpec((tm, d), row), pl.BlockSpec((tm, d), row),
                   pl.BlockSpec((tm, dm), row), pl.BlockSpec((SUBLANES, d), fixed)],
        out_shape=[jax.ShapeDtypeStruct((SUBLANES, SLAB), F32), jax.ShapeDtypeStruct((t, d), F32),
                   jax.ShapeDtypeStruct((t, d), BF16), jax.ShapeDtypeStruct((t, dm), F32),
                   jax.ShapeDtypeStruct((SUBLANES, d), F32)],
        compiler_params=_cp(2 * (2 * dm * d + tm * (2 * dm + 4 * d * 3 + 2 * d + 4 * dm)) + 16 * MIB),
    )(mix, w_out, x, tgt, g_post)


def _dh_prenorm_bwd(dproj, w_in, x, dout, g_pre):
    t, d = x.shape
    nb, _, n = w_in.shape
    tm = min(TM_DH, t)

    def body(dp_ref, w_ref, x_ref, dout_ref, g_ref, dx_ref, gg_ref):
        @pl.when(pl.program_id(0) == 0)
        def _():
            gg_ref[...] = jnp.zeros_like(gg_ref)

        dh = None
        for k in range(nb):
            part = lax.dot_general(dp_ref[:, k * n:(k + 1) * n], w_ref[k], (((1,), (1,)), ((), ())),
                                   preferred_element_type=F32)
            dh = part if dh is None else dh + part
        xv = x_ref[...]
        r = lax.rsqrt(jnp.mean(xv * xv, axis=-1, keepdims=True) + EPS)
        xh = xv * r
        gg_ref[0:1, :] += jnp.sum(dh * xh, axis=0, keepdims=True)
        dg = dh * g_ref[...]
        dx_ref[...] = dout_ref[...] + r * (dg - xh * jnp.mean(dg * xh, axis=-1, keepdims=True))

    row = lambda i: (i, 0)
    fixed = lambda i: (0, 0)
    w_spec = pl.BlockSpec(w_in.shape, lambda i: (0, 0, 0), pipeline_mode=pl.Buffered(1))
    return _pallas(
        body, name="dh_prenorm_bwd", grid=(t // tm,),
        in_specs=[pl.BlockSpec((tm, nb * n), row), w_spec,
                  pl.BlockSpec((tm, d), row), pl.BlockSpec((tm, d), row), pl.BlockSpec((1, d), fixed)],
        out_specs=[pl.BlockSpec((tm, d), row), pl.BlockSpec((SUBLANES, d), fixed)],
        out_shape=[jax.ShapeDtypeStruct((t, d), F32), jax.ShapeDtypeStruct((SUBLANES, d), F32)],
        compiler_params=_cp(2 * nb * d * n + 2 * (2 * tm * nb * n + 12 * tm * d) + 16 * tm * d + 4 * MIB),
    )(dproj, w_in, x, dout, g_pre)


def _adamw(w, g, m, v, name):
    r, c = w.shape
    tr = min(TR_ELT, r)
    c1 = 1.0 - ADAM_B1 ** ADAM_STEP
    c2 = 1.0 - ADAM_B2 ** ADAM_STEP

    def body(w_ref, g_ref, m_ref, v_ref, d_ref, mo_ref, vo_ref):
        gv = g_ref[...]
        mn = ADAM_B1 * m_ref[...] + (1.0 - ADAM_B1) * gv
        vn = ADAM_B2 * v_ref[...] + (1.0 - ADAM_B2) * (gv * gv)
        d_ref[...] = -ADAM_LR * ((mn / c1) / (jnp.sqrt(vn / c2) + ADAM_EPS) + ADAM_WD * w_ref[...])
        mo_ref[...] = mn
        vo_ref[...] = vn

    spec = pl.BlockSpec((tr, c), lambda i: (i, 0))
    sds = jax.ShapeDtypeStruct((r, c), F32)
    return _pallas(
        body, name=name, grid=(r // tr,), in_specs=[spec] * 4, out_specs=[spec] * 3, out_shape=[sds] * 3,
        compiler_params=_cp(2 * 7 * 4 * tr * c + 8 * MIB),
    )(w, g, m, v)


def _adamw_small(g, w, m, v):
    r, c = w.shape
    c1 = 1.0 - ADAM_B1 ** ADAM_STEP
    c2 = 1.0 - ADAM_B2 ** ADAM_STEP

    def body(g_ref, w_ref, m_ref, v_ref, d_ref, mo_ref, vo_ref):
        gv = g_ref[...]
        mn = ADAM_B1 * m_ref[...] + (1.0 - ADAM_B1) * gv
        vn = ADAM_B2 * v_ref[...] + (1.0 - ADAM_B2) * (gv * gv)
        d_ref[...] = -ADAM_LR * ((mn / c1) / (jnp.sqrt(vn / c2) + ADAM_EPS) + ADAM_WD * w_ref[...])
        mo_ref[...] = mn
        vo_ref[...] = vn

    sds = jax.ShapeDtypeStruct((r, c), F32)
    return _pallas(
        body, name="adamw_small", out_shape=[sds] * 3,
        compiler_params=_cp(12 * 4 * r * c + 8 * MIB),
    )(g, w, m, v)


def _allreduce_small(v):
    r, c = v.shape
    assert r % (2 * SUBLANES) == 0
    h = r // 2

    def body(v_ref, out_ref, got_ref, chip_ref, slots_ref, send, recv):
        x, y, cc = _my_place()
        k = 2 * x + y
        sib = (x, y, 1 - cc)

        def rc(s, src, dst, to):
            return pltpu.make_async_remote_copy(src_ref=src, dst_ref=dst, send_sem=send.at[s], recv_sem=recv.at[s],
                                                device_id=to, device_id_type=MESH)

        pair = rc(0, v_ref, got_ref, sib)
        pair.start()
        pair.wait()
        chip_ref[...] = v_ref[...] + got_ref[...]
        mine = pl.ds(pl.multiple_of(cc * h, SUBLANES), h)
        theirs = pl.ds(pl.multiple_of((1 - cc) * h, SUBLANES), h)
        chips = _other_chips(x, y)
        sent = []
        for j, (cx, cy) in enumerate(chips):
            cp = rc(1 + j, chip_ref.at[mine], slots_ref.at[k], (cx, cy, cc))
            cp.start()
            sent.append(cp)
        slots_ref[k] = chip_ref[mine, :]
        for j, (cx, cy) in enumerate(chips):
            rc(1 + j, chip_ref.at[mine], slots_ref.at[2 * cx + cy], (cx, cy, cc)).wait_recv()
        total = slots_ref[0]
        for kk in range(1, N_CHIPS):
            total = total + slots_ref[kk]
        out_ref[mine, :] = total
        for cp in sent:
            cp.wait_send()
        share = rc(N_CHIPS, out_ref.at[mine], out_ref.at[mine], sib)
        share.start()
        rc(N_CHIPS, out_ref.at[theirs], out_ref.at[theirs], sib).wait_recv()
        share.wait_send()

    vm = pl.BlockSpec(memory_space=pltpu.VMEM)
    return _pallas(
        body, name="allreduce_small", out_shape=jax.ShapeDtypeStruct((r, c), F32), in_specs=[vm], out_specs=vm,
        scratch_shapes=[pltpu.VMEM((r, c), F32), pltpu.VMEM((r, c), F32), pltpu.VMEM((N_CHIPS, h, c), F32),
                        pltpu.SemaphoreType.DMA((N_CHIPS + 1,)), pltpu.SemaphoreType.DMA((N_CHIPS + 1,))],
        compiler_params=_cp(6 * 4 * r * c + 8 * MIB),
    )(v)


def _zoh(a_re, a_im, log_dt, b_re2, b_im2, expand):
    dt = jnp.exp(log_dt)
    mag = jnp.exp(a_re * dt)
    lbr, lbi = mag * jnp.cos(a_im * dt), mag * jnp.sin(a_im * dt)
    nr, ni = lbr - 1.0, lbi
    den = a_re * a_re + a_im * a_im
    qr = (nr * a_re + ni * a_im) / den
    qi = (ni * a_re - nr * a_im) / den
    qr2 = jnp.dot(qr, expand, precision=lax.Precision.HIGHEST, preferred_element_type=F32)
    qi2 = jnp.dot(qi, expand, precision=lax.Precision.HIGHEST, preferred_element_type=F32)
    return lbr, lbi, qr2 * b_re2 - qi2 * b_im2, qr2 * b_im2 + qi2 * b_re2


def _zoh_fwd(a_re, a_im, log_dt, b_re2, b_im2, expand, power):
    g, p = a_re.shape

    def body(ar_ref, ai_ref, ld_ref, br_ref, bi_ref, e_ref, lbr_ref, lbi_ref, bbr_ref, bbi_ref, pw_ref):
        ar, ai, ld = ar_ref[...], ai_ref[...], ld_ref[...]
        lbr, lbi, bbr, bbi = _zoh(ar, ai, ld, br_ref[...], bi_ref[...], e_ref[...])
        lbr_ref[...], lbi_ref[...], bbr_ref[...], bbi_ref[...] = lbr, lbi, bbr, bbi
        dt = jnp.exp(ld) * float(power)
        mag = jnp.exp(ar * dt)
        pw_ref[0] = mag * jnp.cos(ai * dt)
        pw_ref[1] = mag * jnp.sin(ai * dt)

    gp = jax.ShapeDtypeStruct((g, p), F32)
    gph = jax.ShapeDtypeStruct(b_re2.shape, F32)
    return _pallas(
        body, name="zoh_fwd", out_shape=[gp, gp, gph, gph, jax.ShapeDtypeStruct((2, g, p), F32)],
        compiler_params=_cp(16 * MIB),
    )(a_re, a_im, log_dt, b_re2, b_im2, expand)


def _zoh_bwd(a_re, a_im, log_dt, b_re2, b_im2, expand, g_lbr, g_lbi, g_bbr, g_bbi):
    def body(ar_ref, ai_ref, ld_ref, br_ref, bi_ref, e_ref, c0, c1, c2, c3, gar, gai, gld, gbr, gbi):
        e = e_ref[...]
        _, vjp = jax.vjp(lambda a, b, c, d, f: _zoh(a, b, c, d, f, e),
                         ar_ref[...], ai_ref[...], ld_ref[...], br_ref[...], bi_ref[...])
        gar[...], gai[...], gld[...], gbr[...], gbi[...] = vjp((c0[...], c1[...], c2[...], c3[...]))

    sds = lambda a: jax.ShapeDtypeStruct(a.shape, F32)
    return _pallas(
        body, name="zoh_bwd", out_shape=[sds(a_re), sds(a_im), sds(log_dt), sds(b_re2), sds(b_im2)],
        compiler_params=_cp(16 * MIB),
    )(a_re, a_im, log_dt, b_re2, b_im2, expand, g_lbr, g_lbi, g_bbr, g_bbi)


def _blockdiag(blocks):
    nq, _, r, c = blocks.shape
    eye = jnp.eye(GROUPS_PER_SLAB, dtype=blocks.dtype)
    out = blocks[:, :, :, None, :] * eye[None, :, None, :, None]
    return out.reshape(nq, GROUPS_PER_SLAB * r, GROUPS_PER_SLAB * c)


def _blockdiag_take(dense, r, c):
    nq = dense.shape[0]
    d5 = dense.reshape(nq, GROUPS_PER_SLAB, r, GROUPS_PER_SLAB, c)
    return jnp.stack([d5[:, i, :, i, :] for i in range(GROUPS_PER_SLAB)], axis=1)


def _scan_slab(s_ref, row0, q, lam_ref, pw_ref, car_ref, steps, reverse, prev_ref=None, prev_row0=0, glam_ref=None):
    sign = -1.0 if reverse else 1.0
    half = SLAB_STATES // SLAB
    cols = [(q * 2 * half + m, q * 2 * half + half + m, q * SLAB_STATES + m * SLAB) for m in range(half)]
    nm = len(cols)
    full = (SUBLANES, SLAB)
    lam = [(jnp.broadcast_to(lam_ref[0:1, pl.ds(cl, SLAB)], full),
            jnp.broadcast_to(sign * lam_ref[1:2, pl.ds(cl, SLAB)], full)) for (_, _, cl) in cols]

    def step_rows(jj, base):
        j = (steps - 1 - jj) if reverse else jj
        return j, pl.ds(base + j * SUBLANES, SUBLANES)

    def pass1(jj, car):
        _, rows = step_rows(jj, row0)
        out = []
        for m, (cr, ci, _) in enumerate(cols):
            sr, si = car[2 * m], car[2 * m + 1]
            lr, li = lam[m]
            nr = lr * sr - li * si + s_ref[cr, rows, :]
            ni = lr * si + li * sr + s_ref[ci, rows, :]
            s_ref[cr, rows, :] = nr
            s_ref[ci, rows, :] = ni
            out += [nr, ni]
        return tuple(out)

    def run_steps(step_fn, carry):
        for jj in range(steps):
            carry = step_fn(jj, carry)
        return carry

    ends = run_steps(pass1, tuple(jnp.zeros(full, F32) for _ in range(2 * nm)))

    entry = []
    for m, (cr, ci, cl) in enumerate(cols):
        ljr = pw_ref[0:1, pl.ds(cl, SLAB)]
        lji = sign * pw_ref[1:2, pl.ds(cl, SLAB)]
        c_r = car_ref[cr, 0:1, :]
        c_i = car_ref[ci, 0:1, :]
        rows_r, rows_i = [None] * SUBLANES, [None] * SUBLANES
        order = range(SUBLANES - 1, -1, -1) if reverse else range(SUBLANES)
        for b in order:
            rows_r[b], rows_i[b] = c_r, c_i
            e_r, e_i = ends[2 * m][b:b + 1], ends[2 * m + 1][b:b + 1]
            c_r, c_i = ljr * c_r - lji * c_i + e_r, ljr * c_i + lji * c_r + e_i
        car_ref[cr, 0:1, :] = c_r
        car_ref[ci, 0:1, :] = c_i
        entry.append((jnp.concatenate(rows_r, axis=0), jnp.concatenate(rows_i, axis=0)))

    def pass2(jj, carry):
        j, rows = step_rows(jj, row0)
        decayed, acc = carry[:2 * nm], carry[2 * nm:]
        out_d, out_a = [], []
        for m, (cr, ci, cl) in enumerate(cols):
            lr, li = lam[m]
            dr, di = decayed[2 * m], decayed[2 * m + 1]
            dr, di = lr * dr - li * di, lr * di + li * dr
            nr = s_ref[cr, rows, :] + dr
            ni = s_ref[ci, rows, :] + di
            s_ref[cr, rows, :] = nr
            s_ref[ci, rows, :] = ni
            out_d += [dr, di]
            if prev_ref is not None:
                prow = pl.ds(prev_row0 + (j - 1) * SUBLANES, SUBLANES)
                qr = prev_ref[cr, prow, :]
                qi = prev_ref[ci, prow, :]
                out_a += [acc[2 * m] + (nr * qr + ni * qi), acc[2 * m + 1] + (ni * qr - nr * qi)]
        return tuple(out_d) + tuple(out_a)

    n_acc = 2 * nm if prev_ref is not None else 0
    init = tuple(e for pair in entry for e in pair) + tuple(jnp.zeros(full, F32) for _ in range(n_acc))
    accs = run_steps(pass2, init)[2 * nm:]
    if prev_ref is not None:
        for m, (_, _, cl) in enumerate(cols):
            glam_ref[0:1, pl.ds(cl, SLAB)] += jnp.sum(accs[2 * m], axis=0, keepdims=True)
            glam_ref[1:2, pl.ds(cl, SLAB)] += jnp.sum(accs[2 * m + 1], axis=0, keepdims=True)


def _permute_rows_f32(perm_bf16, v):
    hi = v.astype(BF16)
    lo = (v - hi.astype(F32)).astype(BF16)
    return (jnp.dot(perm_bf16, hi, preferred_element_type=F32) + jnp.dot(perm_bf16, lo, preferred_element_type=F32))


def _put_slab(s_ref, rows, q, val):
    per = 2 * SLAB_STATES // SLAB
    for i in range(per):
        s_ref[q * per + i, rows, :] = val[:, i * SLAB:(i + 1) * SLAB]


def _get_slab(s_ref, rows, q):
    per = 2 * SLAB_STATES // SLAB
    return jnp.concatenate([s_ref[q * per + i, rows, :] for i in range(per)], axis=1)


def _step_major_perm(tt):
    r = jnp.arange(tt)
    held = (r % SUBLANES) * (tt // SUBLANES) + r // SUBLANES
    return held[:, None] == r[None, :]


def _ssm_fwd(proj, u_block, bq, cq, lam, pw, d_skip):
    t = proj.shape[0]
    nq, ds, w2 = bq.shape
    assert ds == SLAB and w2 == 2 * SLAB_STATES
    dssm = nq * SLAB
    width = nq * w2
    ntile = width // SLAB
    tt = min(T_SCAN, t)
    steps = tt // SUBLANES
    perm = _step_major_perm(tt)
    pm, pmt = perm.astype(BF16), perm.T.astype(BF16)

    def body(u_ref, pm_ref, pmt_ref, bq_ref, cq_ref, lam_ref, pw_ref, d_ref, y_ref, cin_ref, s_ref, car_ref):
        @pl.when(pl.program_id(0) == 0)
        def _():
            car_ref[...] = jnp.zeros_like(car_ref)

        cin_ref[...] = jnp.broadcast_to(car_ref[:, 0:1, :], cin_ref.shape)
        u = u_ref[...]
        ub = jnp.dot(pm_ref[...], u.astype(BF16), preferred_element_type=F32).astype(BF16)
        everything = slice(None)
        for q in range(nq):
            _put_slab(s_ref, everything, q,
                      jnp.dot(ub[:, q * SLAB:(q + 1) * SLAB], bq_ref[q], preferred_element_type=F32))
        for q in range(nq):
            _scan_slab(s_ref, 0, q, lam_ref, pw_ref, car_ref, steps, reverse=False)
        y_sm = jnp.concatenate(
            [jnp.dot(_get_slab(s_ref, everything, q).astype(BF16), cq_ref[q], preferred_element_type=F32)
             for q in range(nq)], axis=1)
        y_ref[...] = _permute_rows_f32(pmt_ref[...], y_sm) + d_ref[...] * u

    c3 = lambda i: (0, 0, 0)
    c2 = lambda i: (0, 0)
    return _pallas(
        body, name="ssm_fwd", grid=(t // tt,),
        in_specs=[pl.BlockSpec((tt, dssm), lambda i: (i, u_block)), pl.BlockSpec((tt, tt), c2),
                  pl.BlockSpec((tt, tt), c2), pl.BlockSpec(bq.shape, c3),
                  pl.BlockSpec(cq.shape, c3), pl.BlockSpec(lam.shape, c2), pl.BlockSpec(pw.shape, c2),
                  pl.BlockSpec((1, dssm), c2)],
        out_specs=[pl.BlockSpec((tt, dssm), lambda i: (i, 0)),
                   pl.BlockSpec((None, ntile, SUBLANES, SLAB), lambda i: (i, 0, 0, 0))],
        out_shape=[jax.ShapeDtypeStruct((t, dssm), F32), jax.ShapeDtypeStruct((t // tt, ntile, SUBLANES, SLAB), F32)],
        scratch_shapes=[pltpu.VMEM((ntile, tt, SLAB), F32), pltpu.VMEM((ntile, SUBLANES, SLAB), F32)],
        compiler_params=_cp(2 * (8 * tt * dssm + 4 * nq * ds * w2) + 4 * tt * width + 16 * MIB,
                            dimension_semantics=("arbitrary",)),
    )(proj, pm, pmt, bq, cq, lam, pw, d_skip)


def _ssm_bwd(proj, u_block, dy, cin, dproj, bq, cq, lam, pw, d_skip):
    t = proj.shape[0]
    nq, ds, w2 = bq.shape
    dssm = nq * SLAB
    width = nq * w2
    ntile = width // SLAB
    tt = min(T_SCAN, t)
    nt = t // tt
    steps = tt // SUBLANES
    halo = SUBLANES
    perm = _step_major_perm(tt)
    pm, pmt = perm.astype(BF16), perm.T.astype(BF16)

    def body(u_ref, dy_ref, cin_ref, dp_any, pm_ref, pmt_ref, bq_ref, cq_ref, lam_ref, pw_ref, d_ref,
             du_ref, gb_ref, gc_ref, glam_ref, gd_ref, s_ref, gs_ref, car_f, car_b):
        del dp_any

        @pl.when(pl.program_id(0) == 0)
        def _():
            car_b[...] = jnp.zeros_like(car_b)
            gb_ref[...] = jnp.zeros_like(gb_ref)
            gc_ref[...] = jnp.zeros_like(gc_ref)
            glam_ref[...] = jnp.zeros_like(glam_ref)
            gd_ref[...] = jnp.zeros_like(gd_ref)

        u = u_ref[...]
        dyv = dy_ref[...]
        gd_ref[0:1, :] += jnp.sum(dyv * u, axis=0, keepdims=True)
        pmv = pm_ref[...]
        ub = jnp.dot(pmv, u.astype(BF16), preferred_element_type=F32).astype(BF16)
        dyb = jnp.dot(pmv, dyv.astype(BF16), preferred_element_type=F32).astype(BF16)
        car_f[...] = cin_ref[...]
        data = slice(halo, halo + tt)
        everything = slice(None)
        for q in range(nq):
            _put_slab(s_ref, data, q, jnp.dot(ub[:, q * SLAB:(q + 1) * SLAB], bq_ref[q], preferred_element_type=F32))
        for q in range(nq):
            _scan_slab(s_ref, halo, q, lam_ref, pw_ref, car_f, steps, reverse=False)
        last_step = s_ref[:, halo + tt - SUBLANES:halo + tt, :]
        s_ref[:, 0:halo, :] = jnp.concatenate([cin_ref[:, 0:1, :], last_step[:, 0:SUBLANES - 1, :]], axis=1)
        tn = (((0,), (0,)), ((), ()))
        nt_dims = (((1,), (1,)), ((), ()))
        for q in range(nq):
            sl = slice(q * SLAB, (q + 1) * SLAB)
            gc_ref[q] += lax.dot_general(dyb[:, sl], _get_slab(s_ref, data, q).astype(BF16), tn,
                                         preferred_element_type=F32)
            _put_slab(gs_ref, everything, q,
                      lax.dot_general(dyb[:, sl], cq_ref[q], nt_dims, preferred_element_type=F32))
        for q in range(nq):
            _scan_slab(gs_ref, 0, q, lam_ref, pw_ref, car_b, steps, reverse=True,
                       prev_ref=s_ref, prev_row0=halo, glam_ref=glam_ref)
        du_parts = []
        for q in range(nq):
            sl = slice(q * SLAB, (q + 1) * SLAB)
            gsb = _get_slab(gs_ref, everything, q).astype(BF16)
            du_parts.append(lax.dot_general(gsb, bq_ref[q], nt_dims, preferred_element_type=F32))
            gb_ref[q] += lax.dot_general(ub[:, sl], gsb, tn, preferred_element_type=F32)
        du_sm = jnp.concatenate(du_parts, axis=1)
        du_ref[...] = (_permute_rows_f32(pmt_ref[...], du_sm) + dyv * d_ref[...]).astype(BF16)

    c3 = lambda i: (0, 0, 0)
    c2 = lambda i: (0, 0)
    rev = lambda i: (nt - 1 - i, 0)
    dense = jax.ShapeDtypeStruct((nq, SLAB, w2), F32)
    gp = lam.shape[1]
    return _pallas(
        body, name="ssm_bwd", grid=(nt,),
        in_specs=[pl.BlockSpec((tt, dssm), lambda i: (nt - 1 - i, u_block)), pl.BlockSpec((tt, dssm), rev),
                  pl.BlockSpec((None, ntile, SUBLANES, SLAB), lambda i: (nt - 1 - i, 0, 0, 0)),
                  pl.BlockSpec(memory_space=pl.ANY), pl.BlockSpec((tt, tt), c2), pl.BlockSpec((tt, tt), c2),
                  pl.BlockSpec(bq.shape, c3), pl.BlockSpec(cq.shape, c3),
                  pl.BlockSpec(lam.shape, c2), pl.BlockSpec(pw.shape, c2), pl.BlockSpec((1, dssm), c2)],
        out_specs=[pl.BlockSpec((tt, dssm), lambda i: (nt - 1 - i, u_block)), pl.BlockSpec(dense.shape, c3),
                   pl.BlockSpec(dense.shape, c3), pl.BlockSpec((SUBLANES, gp), c2), pl.BlockSpec((SUBLANES, dssm), c2)],
        out_shape=[jax.ShapeDtypeStruct(dproj.shape, dproj.dtype), dense, dense,
                   jax.ShapeDtypeStruct((SUBLANES, gp), F32), jax.ShapeDtypeStruct((SUBLANES, dssm), F32)],
        scratch_shapes=[pltpu.VMEM((ntile, tt + halo, SLAB), F32), pltpu.VMEM((ntile, tt, SLAB), F32),
                        pltpu.VMEM((ntile, SUBLANES, SLAB), F32), pltpu.VMEM((ntile, SUBLANES, SLAB), F32)],
        input_output_aliases={3: 0},
        compiler_params=_cp(2 * (10 * tt * dssm + 4 * nq * ds * w2 + 8 * nq * SLAB * w2)
                            + 8 * tt * width + 12 * MIB, dimension_semantics=("arbitrary",)),
    )(proj, dy, cin, dproj, pm, pmt, bq, cq, lam, pw, d_skip)


def _gate_fwd(proj, y, conv_w, conv_b, w_glu, b_glu, dc):
    t = proj.shape[0]
    dssm = y.shape[1]
    assert dc == dssm
    tm = min(TM_GATE, t)
    halo = SUBLANES

    rb = 2 * SUBLANES
    assert tm % rb == 0

    def body(b_ref, c_ref, v_ref, zc_ref, zs_ref, y_ref, cw_ref, cb_ref, wg_ref, bg_ref, mix_ref,
             cv_buf, ge_buf, geb_buf, gl_buf):
        @pl.when(pl.program_id(0) == 0)
        def _():
            cv_buf[0:halo, :] = jnp.zeros((halo, dc), F32)

        cv_buf[halo:, :] = c_ref[...] * v_ref[...]
        w0, w1, w2, cb = cw_ref[0:1, :], cw_ref[1:2, :], cw_ref[2:3, :], cb_ref[...]
        chunks = [pl.ds(r * rb, rb) for r in range(tm // rb)]
        for r, rs in enumerate(chunks):
            ge, _ = _gelu(y_ref[rs, :])
            ge_buf[rs, :] = ge
            geb_buf[rs, :] = ge.astype(BF16)
        gl_buf[...] = jnp.dot(geb_buf[...], wg_ref[...], preferred_element_type=F32) + bg_ref[...]
        for r, rs in enumerate(chunks):
            at = halo + r * rb
            conv = (cb + w2 * cv_buf[pl.ds(at, rb), :] + w1 * cv_buf[pl.ds(at - 1, rb), :]
                    + w0 * cv_buf[pl.ds(at - 2, rb), :])
            sz, _ = _silu(zc_ref[rs, :])
            mix_ref[rs, 0:dc] = (b_ref[rs, :] * conv * sz).astype(BF16)
        cv_buf[0:halo, :] = cv_buf[tm:tm + halo, :]
        for r, rs in enumerate(chunks):
            szs, _ = _silu(zs_ref[rs, :])
            mix_ref[rs, dc:] = (ge_buf[rs, :] * jax.nn.sigmoid(gl_buf[rs, :]) * szs).astype(BF16)

    col = lambda j: pl.BlockSpec((tm, dc), lambda i, j=j: (i, j))
    fixed = lambda i: (0, 0)
    return _pallas(
        body, name="gate_fwd", grid=(t // tm,),
        in_specs=[col(0), col(1), col(2), col(3), col(5), pl.BlockSpec((tm, dssm), lambda i: (i, 0)),
                  pl.BlockSpec(conv_w.shape, fixed), pl.BlockSpec((1, dc), fixed),
                  pl.BlockSpec(w_glu.shape, fixed), pl.BlockSpec((1, dssm), fixed)],
        out_specs=pl.BlockSpec((tm, dc + dssm), lambda i: (i, 0)),
        out_shape=jax.ShapeDtypeStruct((t, dc + dssm), BF16),
        scratch_shapes=[pltpu.VMEM((tm + halo, dc), F32), pltpu.VMEM((tm, dssm), F32), pltpu.VMEM((tm, dssm), BF16),
                        pltpu.VMEM((tm, dssm), F32)],
        compiler_params=_cp(2 * (6 * 4 * tm * dc + 2 * tm * (dc + dssm) + 2 * dssm * dssm) + 16 * tm * dc + 8 * MIB,
                            dimension_semantics=("arbitrary",)),
    )(proj, proj, proj, proj, proj, y, conv_w, conv_b, w_glu, b_glu)


def _gate_bwd(proj, y, dmix, conv_w, conv_b, w_glu, b_glu, dc):
    t = proj.shape[0]
    dssm = y.shape[1]
    tm = min(TM_GATE, t)
    nt = t // tm
    halo = SUBLANES
    blocks_per_tile = tm // halo

    rb = 2 * SUBLANES
    assert tm % rb == 0
    n_chunk = tm // rb

    def body(b_ref, c_ref, v_ref, zc_ref, zs_ref, cp_ref, vp_ref, y_ref, dm_ref, cw_ref, cb_ref, wg_ref, bg_ref,
             dp_ref, dy_ref, gs_ref, gwg_ref, cv_buf, dc_buf, geb_buf, dglb_buf, ge_buf, th_buf):
        i = pl.program_id(0)

        @pl.when(i == 0)
        def _():
            dc_buf[tm:, :] = jnp.zeros((halo, dc), F32)
            gs_ref[...] = jnp.zeros_like(gs_ref)
            gwg_ref[...] = jnp.zeros_like(gwg_ref)

        def rows(r):
            return pl.ds(r * rb, rb)

        def rows_after_halo(r):
            return pl.ds(halo + r * rb, rb)

        def chunks(step, carry):
            for r in range(n_chunk):
                carry = step(r, carry)
            return carry

        def fold(v):
            return v[0:SUBLANES] + v[SUBLANES:rb]

        first_tile = (i == nt - 1)
        cv_buf[0:halo, :] = jnp.where(first_tile, 0.0, cp_ref[...] * vp_ref[...])
        cv_buf[halo:, :] = c_ref[...] * v_ref[...]
        w0, w1, w2, cb = cw_ref[0:1, :], cw_ref[1:2, :], cw_ref[2:3, :], cb_ref[...]

        def conv_pass_a(r, carry):
            rs = rows(r)
            bv, zc, dyc = b_ref[rs, :], zc_ref[rs, :], dm_ref[rs, 0:dc]
            conv = (cb + w2 * cv_buf[rows_after_halo(r), :] + w1 * cv_buf[pl.ds(halo - 1 + r * rb, rb), :]
                    + w0 * cv_buf[pl.ds(halo - 2 + r * rb, rb), :])
            sz, sgc = _silu(zc)
            dp_ref[rs, 0:dc] = (dyc * conv * sz).astype(BF16)
            dp_ref[rs, 3 * dc:4 * dc] = (dyc * bv * conv * _dsilu(zc, sgc)).astype(BF16)
            dc_buf[rs, :] = dyc * bv * sz
            return carry

        chunks(conv_pass_a, 0)

        def conv_pass_b(r, acc):
            rs = rows(r)
            dconv, d1, d2 = dc_buf[rs, :], dc_buf[pl.ds(r * rb + 1, rb), :], dc_buf[pl.ds(r * rb + 2, rb), :]
            cv = cv_buf[rows_after_halo(r), :]
            dcv = w2 * dconv + w1 * d1 + w0 * d2
            dp_ref[rs, dc:2 * dc] = (dcv * v_ref[rs, :]).astype(BF16)
            dp_ref[rs, 2 * dc:3 * dc] = (dcv * c_ref[rs, :]).astype(BF16)
            return (acc[0] + fold(cv * d2), acc[1] + fold(cv * d1), acc[2] + fold(cv * dconv), acc[3] + fold(dconv))

        zero = jnp.zeros((SUBLANES, dc), F32)
        sums = chunks(conv_pass_b, (zero, zero, zero, zero))
        for k in range(4):
            gs_ref[k:k + 1, :] += jnp.sum(sums[k], axis=0, keepdims=True)
        dc_buf[tm:, :] = dc_buf[0:halo, :]

        def glu_pass_a(r, carry):
            rs = rows(r)
            ge, th = _gelu(y_ref[rs, :])
            geb_buf[rs, :] = ge.astype(BF16)
            ge_buf[rs, :] = ge
            th_buf[rs, :] = th
            return carry

        chunks(glu_pass_a, 0)
        wg = wg_ref[...]
        gl_buf = cv_buf.at[halo:halo + tm]
        keep_buf = dc_buf.at[0:tm]
        gl_buf[...] = jnp.dot(geb_buf[...], wg, preferred_element_type=F32) + bg_ref[...]
        dp_ref[:, 4 * dc:5 * dc] = jnp.zeros((tm, dc), BF16)

        def glu_pass_b(r, acc):
            rs = rows(r)
            ge = ge_buf[rs, :]
            sg = jax.nn.sigmoid(gl_buf[rs, :])
            zs, dys = zs_ref[rs, :], dm_ref[rs, dc:]
            szs, sgs = _silu(zs)
            dp_ref[rs, 5 * dc:] = (dys * ge * sg * _dsilu(zs, sgs)).astype(BF16)
            d_ys = dys * szs
            dgl = d_ys * ge * sg * (1.0 - sg)
            dglb_buf[rs, :] = dgl.astype(BF16)
            keep_buf[rs, :] = d_ys * sg
            return acc + fold(dgl)

        gs_ref[4:5, :] += jnp.sum(chunks(glu_pass_b, zero), axis=0, keepdims=True)
        dglb = dglb_buf[...]
        gwg_ref[...] += lax.dot_general(geb_buf[...], dglb, (((0,), (0,)), ((), ())), preferred_element_type=F32)
        gl_buf[...] = lax.dot_general(dglb, wg, (((1,), (1,)), ((), ())), preferred_element_type=F32)

        def glu_pass_c(r, carry):
            rs = rows(r)
            dy_ref[rs, :] = (keep_buf[rs, :] + gl_buf[rs, :]) * _dgelu(y_ref[rs, :], th_buf[rs, :])
            return carry

        chunks(glu_pass_c, 0)

    col = lambda j: pl.BlockSpec((tm, dc), lambda i, j=j: (nt - 1 - i, j))
    prev = lambda j: pl.BlockSpec((halo, dc), lambda i, j=j: (jnp.maximum((nt - 1 - i) * blocks_per_tile - 1, 0), j))
    rev = lambda i: (nt - 1 - i, 0)
    fixed = lambda i: (0, 0)
    return _pallas(
        body, name="gate_bwd", grid=(nt,),
        in_specs=[col(0), col(1), col(2), col(3), col(5), prev(1), prev(2), pl.BlockSpec((tm, dssm), rev),
                  pl.BlockSpec((tm, dc + dssm), rev), pl.BlockSpec(conv_w.shape, fixed), pl.BlockSpec((1, dc), fixed),
                  pl.BlockSpec(w_glu.shape, fixed), pl.BlockSpec((1, dssm), fixed)],
        out_specs=[pl.BlockSpec((tm, 6 * dc), rev), pl.BlockSpec((tm, dssm), rev),
                   pl.BlockSpec((2 * SUBLANES, dc), fixed), pl.BlockSpec((dssm, dssm), fixed)],
        out_shape=[jax.ShapeDtypeStruct((t, 6 * dc), BF16), jax.ShapeDtypeStruct((t, dssm), F32),
                   jax.ShapeDtypeStruct((2 * SUBLANES, dc), F32), jax.ShapeDtypeStruct((dssm, dssm), F32)],
        scratch_shapes=[pltpu.VMEM((tm + halo, dc), F32), pltpu.VMEM((tm + halo, dc), F32),
                        pltpu.VMEM((tm, dssm), BF16), pltpu.VMEM((tm, dssm), BF16),
                        pltpu.VMEM((tm, dssm), F32), pltpu.VMEM((tm, dssm), F32)],
        compiler_params=_cp(2 * (6 * 4 * tm * dc + 8 * tm * dc + 12 * tm * dc + 4 * tm * dc + 6 * dssm * dssm)
                            + 20 * tm * dc + 8 * MIB, dimension_semantics=("arbitrary",)),
    )(proj, proj, proj, proj, proj, proj, proj, y, dmix, conv_w, conv_b, w_glu, b_glu)


def _allgather_halves(parts, name):
    n = len(parts)
    per = 2 * (N_CHIPS - 1)

    def body(*refs):
        ins, outs = refs[:n], refs[n:2 * n]
        send, recv, local = refs[2 * n:]
        x, y, c = _my_place()
        k = 2 * x + y
        sib = (x, y, 1 - c)
        chips = _other_chips(x, y)

        def rc(t, s, src, dst, to):
            return pltpu.make_async_remote_copy(src_ref=src, dst_ref=dst, send_sem=send.at[t * per + s],
                                                recv_sem=recv.at[t * per + s], device_id=to, device_id_type=MESH)

        mine = [pltpu.make_async_copy(ins[t], outs[t].at[k], local.at[t]) for t in range(n)]
        for cp in mine:
            cp.start()
        sent = []
        for t in range(n):
            for j, (cx, cy) in enumerate(chips):
                cp = rc(t, j, ins[t].at[c], outs[t].at[k, c], (cx, cy, c))
                cp.start()
                sent.append(cp)
        for t in range(n):
            for j, (cx, cy) in enumerate(chips):
                landed = outs[t].at[2 * cx + cy, c]
                rc(t, j, landed, landed, (cx, cy, c)).wait_recv()
                cp = rc(t, N_CHIPS - 1 + j, landed, landed, sib)
                cp.start()
                sent.append(cp)
        for t in range(n):
            for j, (cx, cy) in enumerate(chips):
                other = outs[t].at[2 * cx + cy, 1 - c]
                rc(t, N_CHIPS - 1 + j, other, other, sib).wait_recv()
        for cp in sent:
            cp.wait_send()
        for cp in mine:
            cp.wait()

    any_spec = pl.BlockSpec(memory_space=pl.ANY)
    return _pallas(
        body, name=name, in_specs=[any_spec] * n, out_specs=[any_spec] * n,
        out_shape=[jax.ShapeDtypeStruct((N_CHIPS,) + p.shape, p.dtype) for p in parts],
        scratch_shapes=[pltpu.SemaphoreType.DMA((n * per,)), pltpu.SemaphoreType.DMA((n * per,)),
                        pltpu.SemaphoreType.DMA((n,))],
        compiler_params=_cp(16 * MIB),
    )(*parts)


def _allgather_flat(v, name):
    r, c = v.shape
    rels = [(dx, dy, dc) for dx in (0, 1) for dy in (0, 1) for dc in (0, 1)][1:]

    def body(v_ref, out_ref, send, recv):
        x, y, cc = _my_place()
        me = 4 * x + 2 * y + cc

        def peer(rel):
            dx, dy, dc = rel
            return (1 - x if dx else x, 1 - y if dy else y, 1 - cc if dc else cc)

        def rc(s, slot, to):
            return pltpu.make_async_remote_copy(src_ref=v_ref, dst_ref=out_ref.at[slot], send_sem=send.at[s],
                                                recv_sem=recv.at[s], device_id=to, device_id_type=MESH)

        sent = []
        for s, rel in enumerate(rels):
            cp = rc(s, me, peer(rel))
            cp.start()
            sent.append(cp)
        out_ref[me] = v_ref[...]
        for s, rel in enumerate(rels):
            px, py, pc = peer(rel)
            rc(s, 4 * px + 2 * py + pc, (px, py, pc)).wait_recv()
        for cp in sent:
            cp.wait_send()

    return _pallas(
        body, name=name, out_shape=jax.ShapeDtypeStruct((N_DEV, r, c), F32),
        in_specs=[pl.BlockSpec(memory_space=pltpu.VMEM)], out_specs=pl.BlockSpec(memory_space=pltpu.VMEM),
        scratch_shapes=[pltpu.SemaphoreType.DMA((N_DEV - 1,)), pltpu.SemaphoreType.DMA((N_DEV - 1,))],
        compiler_params=_cp((N_DEV + 2) * 4 * r * c + 8 * MIB),
    )(v)


def _rs_pair_exchange(grads, name):
    n = len(grads)

    def body(*refs):
        ins, outs = refs[:n], refs[n:2 * n]
        send, recv = refs[2 * n:]
        x, y, c = _my_place()
        cps = []
        for t in range(n):
            cp = pltpu.make_async_remote_copy(src_ref=ins[t].at[:, 1 - c], dst_ref=outs[t], send_sem=send.at[t],
                                              recv_sem=recv.at[t], device_id=(x, y, 1 - c), device_id_type=MESH)
            cp.start()
            cps.append(cp)
        for cp in cps:
            cp.wait()

    any_spec = pl.BlockSpec(memory_space=pl.ANY)
    return _pallas(
        body, name=name, in_specs=[any_spec] * n, out_specs=[any_spec] * n,
        out_shape=[jax.ShapeDtypeStruct((g.shape[0],) + g.shape[2:], g.dtype) for g in grads],
        scratch_shapes=[pltpu.SemaphoreType.DMA((n,)), pltpu.SemaphoreType.DMA((n,))],
        compiler_params=_cp(16 * MIB),
    )(*grads)


def _rs_pair_add(place, grad, got, name):
    nk, _, r2, c = grad.shape
    tr = min(TR_ELT, r2)

    def body(place_ref, g_ref, r_ref, o16_ref, o32_ref):
        del place_ref
        s = g_ref[...] + r_ref[...].astype(F32)
        o32_ref[...] = s
        o16_ref[...] = s.astype(BF16)

    blk = pl.BlockSpec((None, tr, c), lambda k, i, p: (k, i, 0))
    return _pallas(
        body, name=name,
        grid_spec=pltpu.PrefetchScalarGridSpec(
            num_scalar_prefetch=1, grid=(nk, r2 // tr),
            in_specs=[pl.BlockSpec((None, None, tr, c), lambda k, i, p: (k, p[0], i, 0)), blk],
            out_specs=[blk, blk]),
        out_shape=[jax.ShapeDtypeStruct((nk, r2, c), BF16), jax.ShapeDtypeStruct((nk, r2, c), F32)],
        compiler_params=_cp(2 * 14 * tr * c + 8 * MIB),
    )(place, grad, got)


_HBM_SPEC = pl.BlockSpec(memory_space=pltpu.HBM)
_SEM_SPEC = pl.BlockSpec(memory_space=pltpu.SEMAPHORE)
_DATAFLOW = pltpu.SideEffectType.DATAFLOW_SIDE_EFFECTING


def _split_copy_start(srcs, land_shapes, plan, n_sems, name, after=()):
    ns, nl, na = len(srcs), len(land_shapes), len(after)

    def body(*refs):
        src_refs, land_refs = refs[:ns], refs[ns:ns + nl]
        send, recv = refs[ns + nl + na], refs[ns + nl + na + 1]
        token = refs[-1]
        sends, _ = plan(src_refs, land_refs)
        for src, dst, to, si, ri in sends:
            pltpu.make_async_remote_copy(src_ref=src, dst_ref=dst, send_sem=send.at[si], recv_sem=recv.at[ri],
                                         device_id=to, device_id_type=MESH).start()
        token[...] = jnp.zeros_like(token)

    lands = [lax.empty(shp, dt) for shp, dt in land_shapes]
    through = [pltpu.HBM(a.shape, a.dtype) for a in srcs] + [pltpu.HBM(shp, dt) for shp, dt in land_shapes]
    out = pl.pallas_call(
        body, name=name,
        out_shape=(pltpu.SemaphoreType.DMA((n_sems,)), pltpu.SemaphoreType.DMA((n_sems,)), *through,
                   jax.ShapeDtypeStruct((SUBLANES, SLAB), F32)),
        in_specs=[_HBM_SPEC] * (ns + nl) + [pl.BlockSpec(memory_space=pl.ANY)] * na,
        out_specs=(_SEM_SPEC, _SEM_SPEC, *([_HBM_SPEC] * (ns + nl)), pl.BlockSpec(memory_space=pltpu.VMEM)),
        input_output_aliases={i: 2 + i for i in range(ns + nl)},
        compiler_params=pltpu.CompilerParams(has_side_effects=_DATAFLOW),
    )(*[pltpu.with_memory_space_constraint(a, pltpu.HBM) for a in (*srcs, *lands)], *after)
    return out[0], out[1], list(out[2:2 + ns]), list(out[2 + ns:2 + ns + nl]), out[-1]


def _split_copy_wait(send, recv, srcs, lands, plan, after, name):
    ns, nl, na = len(srcs), len(lands), len(after)

    def body(*refs):
        src_refs, land_refs = refs[:ns], refs[ns:ns + nl]
        send_ref, recv_ref = refs[ns + nl], refs[ns + nl + 1]
        sends, arrivals = plan(src_refs, land_refs)
        for src, dst, to, si, ri in sends:
            pltpu.make_async_remote_copy(src_ref=src, dst_ref=dst, send_sem=send_ref.at[si], recv_sem=recv_ref.at[ri],
                                         device_id=to, device_id_type=MESH).wait_send()
        for (src, _, to, si, _), (view, ri) in zip(sends, arrivals):
            pltpu.make_async_remote_copy(src_ref=view, dst_ref=view, send_sem=send_ref.at[si], recv_sem=recv_ref.at[ri],
                                         device_id=to, device_id_type=MESH).wait_recv()

    out = pl.pallas_call(
        body, name=name,
        out_shape=[pltpu.HBM(a.shape, a.dtype) for a in (*srcs, *lands)],
        in_specs=[_HBM_SPEC] * (ns + nl) + [_SEM_SPEC, _SEM_SPEC] + [pl.BlockSpec(memory_space=pl.ANY)] * na,
        out_specs=[_HBM_SPEC] * (ns + nl),
        input_output_aliases={i: i for i in range(ns + nl)},
        compiler_params=pltpu.CompilerParams(has_side_effects=_DATAFLOW),
    )(*srcs, *lands, send, recv, *after)
    return list(out[:ns]), list(out[ns:])


def _chip_exchange_plan(n):
    per = N_CHIPS - 1

    def plan(srcs, lands):
        x, y, c = _my_place()
        sends, arrivals = [], []
        for t in range(n):
            for j, (cx, cy) in enumerate(_other_chips(x, y)):
                sends.append((srcs[t].at[2 * cx + cy], lands[t].at[j], (cx, cy, c), t * per + j, t * per + j))
                arrivals.append((lands[t].at[j], t * per + j))
        return sends, arrivals

    return plan


def _gather_half_plan(n):
    per = N_CHIPS - 1

    def plan(srcs, lands):
        x, y, c = _my_place()
        k = 2 * x + y
        sends, arrivals = [], []
        for t in range(n):
            for j, (cx, cy) in enumerate(_other_chips(x, y)):
                sends.append((srcs[t].at[c], lands[t].at[k, c], (cx, cy, c), t * per + j, t * per + j))
                arrivals.append((lands[t].at[2 * cx + cy, c], t * per + j))
        return sends, arrivals

    return plan


def _gather_complete(own, landed, name):
    def body(own_ref, in_ref, out_ref, send, recv, local):
        x, y, c = _my_place()
        sib = (x, y, 1 - c)
        mine = pltpu.make_async_copy(own_ref, out_ref.at[2 * x + y], local)
        mine.start()
        sent = []
        for j, (cx, cy) in enumerate(_other_chips(x, y)):
            half = (2 * cx + cy, c)
            cp = pltpu.make_async_remote_copy(src_ref=in_ref.at[half], dst_ref=out_ref.at[half], send_sem=send.at[j],
                                              recv_sem=recv.at[j], device_id=sib, device_id_type=MESH)
            cp.start()
            sent.append(cp)
        for j, (cx, cy) in enumerate(_other_chips(x, y)):
            other = out_ref.at[2 * cx + cy, 1 - c]
            pltpu.make_async_remote_copy(src_ref=other, dst_ref=other, send_sem=send.at[j], recv_sem=recv.at[j],
                                         device_id=sib, device_id_type=MESH).wait_recv()
        for cp in sent:
            cp.wait_send()
        mine.wait()

    any_spec = pl.BlockSpec(memory_space=pl.ANY)
    per = N_CHIPS - 1
    return _pallas(
        body, name=name, in_specs=[any_spec, any_spec], out_specs=any_spec,
        out_shape=jax.ShapeDtypeStruct(landed.shape, landed.dtype), input_output_aliases={1: 0},
        scratch_shapes=[pltpu.SemaphoreType.DMA((per,)), pltpu.SemaphoreType.DMA((per,)), pltpu.SemaphoreType.DMA],
        compiler_params=_cp(16 * MIB),
    )(own, landed)


def _gather_direct_plan(n):
    per = 2 * (N_CHIPS - 1)

    def plan(srcs, lands):
        x, y, c = _my_place()
        k = 2 * x + y
        sends, arrivals = [], []
        for t in range(n):
            for j, (cx, cy) in enumerate(_other_chips(x, y)):
                for core in (0, 1):
                    sends.append((srcs[t].at[c], lands[t].at[k, c], (cx, cy, core),
                                  t * per + 2 * j + core, t * per + 2 * j + c))
                    arrivals.append((lands[t].at[2 * cx + cy, core], t * per + 2 * j + core))
        return sends, arrivals

    return plan


def _rs_chip_add(place, sums32, got, name):
    _, r2, c = sums32.shape
    tr = min(TR_ELT, r2)

    def body(place_ref, s_ref, q_ref, o_ref):
        del place_ref
        o_ref[...] = ((s_ref[...] + q_ref[0].astype(F32)) + q_ref[1].astype(F32)) + q_ref[2].astype(F32)

    return _pallas(
        body, name=name,
        grid_spec=pltpu.PrefetchScalarGridSpec(
            num_scalar_prefetch=1, grid=(r2 // tr,),
            in_specs=[pl.BlockSpec((None, tr, c), lambda i, p: (p[1], i, 0)),
                      pl.BlockSpec((N_CHIPS - 1, tr, c), lambda i, p: (0, i, 0))],
            out_specs=pl.BlockSpec((tr, c), lambda i, p: (i, 0))),
        out_shape=jax.ShapeDtypeStruct((r2, c), F32),
        compiler_params=_cp(2 * 14 * tr * c + 8 * MIB),
    )(place, sums32, got)


def _rs_pair_share(halves, name):
    n = len(halves)

    def body(*refs):
        ins, outs = refs[:n], refs[n:2 * n]
        send, recv, local = refs[2 * n:]
        x, y, c = _my_place()
        cps, mine = [], []
        for t in range(n):
            lc = pltpu.make_async_copy(ins[t], outs[t].at[c], local.at[t])
            lc.start()
            mine.append(lc)
            cp = pltpu.make_async_remote_copy(src_ref=ins[t], dst_ref=outs[t].at[c], send_sem=send.at[t],
                                              recv_sem=recv.at[t], device_id=(x, y, 1 - c), device_id_type=MESH)
            cp.start()
            cps.append(cp)
        for t in range(n):
            other = outs[t].at[1 - c]
            pltpu.make_async_remote_copy(src_ref=other, dst_ref=other, send_sem=send.at[t], recv_sem=recv.at[t],
                                         device_id=(x, y, 1 - c), device_id_type=MESH).wait_recv()
        for cp in cps:
            cp.wait_send()
        for lc in mine:
            lc.wait()

    any_spec = pl.BlockSpec(memory_space=pl.ANY)
    return _pallas(
        body, name=name, in_specs=[any_spec] * n, out_specs=[any_spec] * n,
        out_shape=[jax.ShapeDtypeStruct((2,) + h.shape, F32) for h in halves],
        scratch_shapes=[pltpu.SemaphoreType.DMA((n,)), pltpu.SemaphoreType.DMA((n,)), pltpu.SemaphoreType.DMA((n,))],
        compiler_params=_cp(16 * MIB),
    )(*halves)


_PACK_TILE = SUBLANES * SLAB


def _pack(arrays):
    rows = []
    for a in arrays:
        flat = a.reshape(-1).astype(F32)
        padded = -(-flat.shape[0] // _PACK_TILE) * _PACK_TILE
        rows.append(jnp.pad(flat, (0, padded - flat.shape[0])).reshape(-1, SLAB))
    n_rows = sum(r.shape[0] for r in rows)
    if n_rows % (2 * SUBLANES):
        rows.append(jnp.zeros((SUBLANES, SLAB), F32))
    return jnp.concatenate(rows, axis=0)


def _unpack(packed, shapes):
    out, row = [], 0
    for shp in shapes:
        size = math.prod(shp)
        nrow = -(-size // _PACK_TILE) * SUBLANES
        out.append(packed[row:row + nrow].reshape(-1)[:size].reshape(shp))
        row += nrow
    return out


def kernel(x, norm_pre_g, w_in, conv_w, conv_b, ssm_a_re, ssm_a_im, ssm_log_dt, ssm_b_re, ssm_b_im, ssm_c_re, ssm_c_im, ssm_d, w_glu, b_glu, w_out, norm_post_g, loss_target, m_norm_pre_g, m_w_in, m_conv_w, m_conv_b, m_ssm_a_re, m_ssm_a_im, m_ssm_log_dt, m_ssm_b_re, m_ssm_b_im, m_ssm_c_re, m_ssm_c_im, m_ssm_d, m_w_glu, m_b_glu, m_w_out, m_norm_post_g, v_norm_pre_g, v_w_in, v_conv_w, v_conv_b, v_ssm_a_re, v_ssm_a_im, v_ssm_log_dt, v_ssm_b_re, v_ssm_b_im, v_ssm_c_re, v_ssm_c_im, v_ssm_d, v_w_glu, v_b_glu, v_w_out, v_norm_post_g):
    xs, tgt = x[0], loss_target[0]
    t, d = xs.shape
    dc = conv_b.shape[0]
    dssm = ssm_d.shape[0]
    g, p = ssm_a_re.shape
    h = SSM_H
    nq = dssm // SLAB
    n_shard = w_in.shape[1]
    steps = min(T_SCAN, t) // SUBLANES
    mx, my, mc = _my_place()
    chip = 2 * mx + my
    place = jnp.stack([mc, chip]).astype(jnp.int32)

    cw_cols = conv_w.shape[1]
    cw_pad = -(-cw_cols // SLAB) * SLAB
    cw_blk = jnp.zeros((SUBLANES, cw_pad), F32).at[:conv_w.shape[0], :cw_cols].set(conv_w)
    cw_all = _allgather_flat(cw_blk, "allgather_conv_w")
    conv_w_full = jnp.concatenate([cw_all[2 * k, :, :cw_cols] for k in range(N_CHIPS)], axis=1)

    halves = lambda a: a.reshape(2, a.shape[0] // 2, a.shape[1])
    win_half = halves(_cast_bf16(w_in, "cast_w_in", cw_all[0, :, :SLAB]))
    win_plan = _gather_half_plan(1)
    win_send, win_recv, win_srcs, win_lands, win_token = _split_copy_start(
        [win_half], [((N_CHIPS,) + win_half.shape, BF16)], win_plan, N_CHIPS - 1, "gather_w_in_start")
    behind_w_in = win_token[0:1, 0:1]

    small_names = ["norm_pre_g", "conv_w", "conv_b", "ssm_a_re", "ssm_a_im", "ssm_log_dt", "ssm_b_re", "ssm_b_im",
                   "ssm_c_re", "ssm_c_im", "ssm_d", "b_glu", "norm_post_g", "loss"]
    zeros_cw = jnp.zeros((conv_w.shape[0], dc), F32)
    one0 = jnp.zeros((1,), F32)
    small_w = dict(norm_pre_g=norm_pre_g, conv_w=zeros_cw, conv_b=conv_b, ssm_a_re=ssm_a_re, ssm_a_im=ssm_a_im,
                   ssm_log_dt=ssm_log_dt, ssm_b_re=ssm_b_re, ssm_b_im=ssm_b_im, ssm_c_re=ssm_c_re, ssm_c_im=ssm_c_im,
                   ssm_d=ssm_d, b_glu=b_glu, norm_post_g=norm_post_g, loss=one0)
    small_m = dict(norm_pre_g=m_norm_pre_g, conv_w=zeros_cw, conv_b=m_conv_b, ssm_a_re=m_ssm_a_re, ssm_a_im=m_ssm_a_im,
                   ssm_log_dt=m_ssm_log_dt, ssm_b_re=m_ssm_b_re, ssm_b_im=m_ssm_b_im, ssm_c_re=m_ssm_c_re,
                   ssm_c_im=m_ssm_c_im, ssm_d=m_ssm_d, b_glu=m_b_glu, norm_post_g=m_norm_post_g, loss=one0)
    small_v = dict(norm_pre_g=v_norm_pre_g, conv_w=zeros_cw, conv_b=v_conv_b, ssm_a_re=v_ssm_a_re, ssm_a_im=v_ssm_a_im,
                   ssm_log_dt=v_ssm_log_dt, ssm_b_re=v_ssm_b_re, ssm_b_im=v_ssm_b_im, ssm_c_re=v_ssm_c_re,
                   ssm_c_im=v_ssm_c_im, ssm_d=v_ssm_d, b_glu=v_b_glu, norm_post_g=v_norm_post_g, loss=one0)
    w_pack, m_pack, v_pack = [_pack([group[nm] for nm in small_names]) + behind_w_in
                              for group in (small_w, small_m, small_v)]
    side = [halves(_cast_bf16(w_out, "cast_w_out", win_token)), halves(_cast_bf16(w_glu, "cast_w_glu", win_token))]

    expand = jnp.repeat(jnp.eye(p, dtype=F32), h, axis=1)
    b_re2, b_im2 = ssm_b_re.reshape(g, p * h), ssm_b_im.reshape(g, p * h)
    log_dt2 = ssm_log_dt.reshape(g, 1) + behind_w_in
    lbr, lbi, bbr2, bbi2, pw3 = _zoh_fwd(ssm_a_re, ssm_a_im, log_dt2, b_re2, b_im2, expand, steps)
    lam = jnp.stack([lbr.reshape(g * p), lbi.reshape(g * p)])
    pw = pw3.reshape(2, g * p)
    to_slab_b = lambda b2: _blockdiag(b2.reshape(nq, GROUPS_PER_SLAB, p, h).transpose(0, 1, 3, 2))
    bq = jnp.concatenate([to_slab_b(bbr2), to_slab_b(bbi2)], axis=2).astype(BF16)
    to_slab_c = lambda c3: _blockdiag(c3.reshape(nq, GROUPS_PER_SLAB, h, p).transpose(0, 1, 3, 2))
    cq = jnp.concatenate([to_slab_c(ssm_c_re), to_slab_c(-ssm_c_im)], axis=1).astype(BF16)

    g_pre2, g_post2 = norm_pre_g.reshape(1, d), norm_post_g.reshape(1, d)
    conv_b2, b_glu2, d_skip2 = conv_b.reshape(1, dc), b_glu.reshape(1, dssm), ssm_d.reshape(1, dssm)
    u_block = 4 * dc // dssm

    hb = _prenorm(xs, g_pre2 + behind_w_in)
    proj_own = _inproj_own(place, hb, win_srcs[0].reshape(d, n_shard))
    win_own, win_landed = _split_copy_wait(win_send, win_recv, win_srcs, win_lands, win_plan,
                                           [proj_own, bq, cq, w_pack, m_pack, v_pack, *side], "gather_w_in_wait")
    win_g = _gather_complete(win_own[0], win_landed[0], "gather_w_in_complete")
    win_b = win_g.reshape(N_CHIPS, d, n_shard)
    side_plan = _gather_direct_plan(len(side))
    side_sems = 2 * (N_CHIPS - 1) * len(side)
    ag_send, ag_recv, ag_srcs, ag_lands, ag_token = _split_copy_start(
        side, [((N_CHIPS,) + a.shape, BF16) for a in side], side_plan, side_sems, "gather_side_weights_start",
        after=[win_g])
    proj = _inproj_rest(place, hb, win_b, proj_own, ag_token)
    y, cin = _ssm_fwd(proj, u_block, bq, cq, lam, pw, d_skip2)
    side_own, side_all = _split_copy_wait(ag_send, ag_recv, ag_srcs, ag_lands, side_plan, [cin],
                                          "gather_side_weights_wait")
    wout_g, wglu_g = [lax.dynamic_update_index_in_dim(all_, own, chip, 0) for own, all_ in zip(side_own, side_all)]
    wout_b = wout_g.reshape(dc + dssm, d)
    wglu_b = wglu_g.reshape(dssm, dssm)
    mix = _gate_fwd(proj, y, conv_w_full, conv_b2, wglu_b, b_glu2, dc)
    loss_blk, dout, dob, dmix, gg_post = _outproj(mix, wout_b, xs, tgt, g_post2)

    def reduce_start(grads, to_send, tags, group):
        got = _rs_pair_exchange(to_send, "rs_pair_exchange_" + group)
        sums = [_rs_pair_add(place, gt, rt, "rs_pair_add_" + tg) for gt, rt, tg in zip(grads, got, tags)]
        plan = _chip_exchange_plan(len(grads))
        started = _split_copy_start([s16 for s16, _ in sums],
                                    [((N_CHIPS - 1,) + s16.shape[1:], BF16) for s16, _ in sums], plan,
                                    (N_CHIPS - 1) * len(grads), "rs_chip_exchange_" + group + "_start")
        return plan, started, [s32 for _, s32 in sums]

    def reduce_finish(plan, started, sums32, tags, group, after):
        send, recv, srcs, lands, _ = started
        _, landed = _split_copy_wait(send, recv, srcs, lands, plan, after, "rs_chip_exchange_" + group + "_wait")
        mine = [_rs_chip_add(place, s32, qt, "rs_chip_add_" + tg) for s32, qt, tg in zip(sums32, landed, tags)]
        full = _rs_pair_share(mine, "rs_pair_share_" + group)
        return [f.reshape(2 * f.shape[1], f.shape[2]) for f in full]

    gw_out, gw_out16 = _matmul_tn(mix, dob, 1, "grad_w_out")
    dproj, dy, gsmall, gw_glu = _gate_bwd(proj, y, dmix, conv_w_full, conv_b2, wglu_b, b_glu2, dc)
    as_out = lambda a: a.reshape(N_CHIPS, 2, (dc + dssm) // (2 * N_CHIPS), d)
    gw_glu4 = gw_glu.reshape(N_CHIPS, 2, dssm // (2 * N_CHIPS), dssm)
    rs_a = reduce_start([as_out(gw_out), gw_glu4], [as_out(gw_out16), gw_glu4], ["w_out", "w_glu"], "a")
    dproj, gb_dense, gc_dense, glam, gd = _ssm_bwd(proj, u_block, dy, cin, dproj, bq, cq, lam, pw,
                                                   d_skip2 + rs_a[1][4][0:1, 0:1])
    gw_in, gw_in16 = _matmul_tn(hb, dproj, N_CHIPS, "grad_w_in")
    as_in = lambda a: a.reshape(N_CHIPS, 2, d // 2, n_shard)
    rs_b = reduce_start([as_in(gw_in)], [as_in(gw_in16)], ["w_in"], "b")
    gx, gg_pre = _dh_prenorm_bwd(dproj, win_b, xs, dout, g_pre2 + rs_b[1][4][0:1, 0:1])

    gb4 = gb_dense.reshape(nq, SLAB, 2, SLAB_STATES)
    g_bbr2 = _blockdiag_take(gb4[:, :, 0, :], h, p).transpose(0, 1, 3, 2).reshape(g, p * h)
    g_bbi2 = _blockdiag_take(gb4[:, :, 1, :], h, p).transpose(0, 1, 3, 2).reshape(g, p * h)
    gc4 = gc_dense.reshape(nq, SLAB, 2, SLAB_STATES)
    g_c_re = _blockdiag_take(gc4[:, :, 0, :], h, p).reshape(g, h, p)
    g_c_im = -_blockdiag_take(gc4[:, :, 1, :], h, p).reshape(g, h, p)
    g_a_re, g_a_im, g_ld, g_b_re2, g_b_im2 = _zoh_bwd(
        ssm_a_re, ssm_a_im, log_dt2, b_re2, b_im2, expand,
        glam[0].reshape(g, p), glam[1].reshape(g, p), g_bbr2, g_bbi2)

    small_g = {
        "norm_pre_g": gg_pre[0], "conv_w": gsmall[0:3], "conv_b": gsmall[3], "ssm_a_re": g_a_re, "ssm_a_im": g_a_im,
        "ssm_log_dt": g_ld.reshape(g), "ssm_b_re": g_b_re2.reshape(g, p, h), "ssm_b_im": g_b_im2.reshape(g, p, h),
        "ssm_c_re": g_c_re, "ssm_c_im": g_c_im, "ssm_d": gd[0], "b_glu": gsmall[4], "norm_post_g": gg_post[0],
        "loss": loss_blk[0, 0:1],
    }
    shapes = [small_w[nm].shape for nm in small_names]
    g_pack = _allreduce_small(_pack([small_g[nm] for nm in small_names]))
    packs = _adamw_small(g_pack, w_pack, m_pack, v_pack)
    sg, sd, sm, sv = [dict(zip(small_names, _unpack(pk, shapes))) for pk in (g_pack, *packs)]
    loss = sg["loss"][0]

    g_cw = lax.dynamic_slice_in_dim(sg["conv_w"], chip * cw_cols, cw_cols, axis=1)
    pad_cw = lambda a: jnp.zeros((SUBLANES, cw_pad), F32).at[:a.shape[0], :cw_cols].set(a)
    cut_cw = lambda a: a[:conv_w.shape[0], :cw_cols]
    d_cw, m_cw, v_cw = [cut_cw(a) for a in _adamw(pad_cw(conv_w), pad_cw(g_cw), pad_cw(m_conv_w), pad_cw(v_conv_w),
                                                  "adamw_conv_w")]

    g_wout, g_wglu = reduce_finish(*rs_a, ["w_out", "w_glu"], "a", [d_cw, packs[0]])
    (g_win,) = reduce_finish(*rs_b, ["w_in"], "b", [g_wout])
    d_win, m_win, v_win = _adamw(w_in, g_win, m_w_in, v_w_in, "adamw_w_in")
    d_wout, m_wout, v_wout = _adamw(w_out, g_wout, m_w_out, v_w_out, "adamw_w_out")
    d_wglu, m_wglu, v_wglu = _adamw(w_glu, g_wglu, m_w_glu, v_w_glu, "adamw_w_glu")

    order = ["norm_pre_g", "w_in", "conv_w", "conv_b", "ssm_a_re", "ssm_a_im", "ssm_log_dt", "ssm_b_re", "ssm_b_im",
             "ssm_c_re", "ssm_c_im", "ssm_d", "w_glu", "b_glu", "w_out", "norm_post_g"]
    grads, deltas, new_m, new_v = dict(sg), dict(sd), dict(sm), dict(sv)
    grads.update(w_in=g_win, w_out=g_wout, w_glu=g_wglu, conv_w=g_cw)
    deltas.update(w_in=d_win, w_out=d_wout, w_glu=d_wglu, conv_w=d_cw)
    new_m.update(w_in=m_win, w_out=m_wout, w_glu=m_wglu, conv_w=m_cw)
    new_v.update(w_in=v_win, w_out=v_wout, w_glu=v_wglu, conv_w=v_cw)
    return (loss, gx[None], *[grads[nm] for nm in order], *[deltas[nm] for nm in order],
            *[new_m[nm] for nm in order], *[new_v[nm] for nm in order])
```

```python
import math

import jax
import jax.numpy as jnp
from jax import lax
from jax.experimental import pallas as pl
from jax.experimental.pallas import tpu as pltpu

F32 = jnp.float32
BF16 = jnp.bfloat16
MESH = pl.DeviceIdType.MESH

EPS = 1e-6
SSM_H = 16
SSM_P = 64
GROUPS_PER_SLAB = 8
SLAB = 128
SLAB_STATES = GROUPS_PER_SLAB * SSM_P
N_CHIPS = 4
N_DEV = 8

ADAM_LR = 0.001
ADAM_B1 = 0.9
ADAM_B2 = 0.999
ADAM_EPS = 1e-08
ADAM_WD = 0.01
ADAM_STEP = 10

MIB = 1024 * 1024
VMEM_CAP = 48 * MIB
SUBLANES = 8

TM_NORM = 512
TM_PROJ = 512
TM_GATE = 256
TM_OUT = 256
TM_DH = 256
T_SCAN = 256
TK_TN = 512
TM_TN = 1024
TR_ELT = 256


def _cp(vmem_bytes, **kw):
    return pltpu.CompilerParams(vmem_limit_bytes=int(min(VMEM_CAP, max(16 * MIB, vmem_bytes))), **kw)


def _pallas(body, **kw):
    if "grid" not in kw and "grid_spec" not in kw:
        return pl.pallas_call(body, **kw)
    pin = lambda s: pltpu.HBM(s.shape, s.dtype) if isinstance(s, jax.ShapeDtypeStruct) else s
    out_shape = kw.pop("out_shape")
    out_shape = [pin(s) for s in out_shape] if isinstance(out_shape, (list, tuple)) else pin(out_shape)
    call = pl.pallas_call(body, out_shape=out_shape, **kw)

    def run(*args):
        return call(*[pltpu.with_memory_space_constraint(a, pltpu.HBM) if jnp.issubdtype(a.dtype, jnp.floating) else a
                      for a in args])

    return run


def _my_place():
    return lax.axis_index("x"), lax.axis_index("y"), lax.axis_index("c")


def _other_chips(x, y):
    return [(1 - x, y), (x, 1 - y), (1 - x, 1 - y)]


def _silu(z):
    s = jax.nn.sigmoid(z)
    return z * s, s


def _dsilu(z, s):
    return s * (1.0 + z * (1.0 - s))


_GELU_K = math.sqrt(2.0 / math.pi)
_GELU_C = 0.044715


def _gelu(y):
    th = jnp.tanh(_GELU_K * (y + _GELU_C * y * y * y))
    return 0.5 * y * (1.0 + th), th


def _dgelu(y, th):
    return 0.5 * (1.0 + th) + 0.5 * y * (1.0 - th * th) * _GELU_K * (1.0 + 3.0 * _GELU_C * y * y)


def _cast_bf16(w, name, after=None):
    r, c = w.shape
    tr = min(TR_ELT, r)
    extra = [] if after is None else [after]

    def body(w_ref, *rest):
        rest[-1][...] = w_ref[...].astype(BF16)

    return _pallas(
        body, name=name, grid=(r // tr,),
        in_specs=[pl.BlockSpec((tr, c), lambda i: (i, 0))] + [pl.BlockSpec((SUBLANES, SLAB), lambda i: (0, 0))] * len(extra),
        out_specs=pl.BlockSpec((tr, c), lambda i: (i, 0)),
        out_shape=jax.ShapeDtypeStruct((r, c), BF16),
        compiler_params=_cp(12 * tr * c),
    )(w, *extra)


def _prenorm(x, g):
    t, d = x.shape
    tm = min(TM_NORM, t)

    def body(x_ref, g_ref, h_ref):
        xv = x_ref[...]
        r = lax.rsqrt(jnp.mean(xv * xv, axis=-1, keepdims=True) + EPS)
        h_ref[...] = (xv * r * g_ref[...]).astype(BF16)

    return _pallas(
        body, name="prenorm", grid=(t // tm,),
        in_specs=[pl.BlockSpec((tm, d), lambda i: (i, 0)), pl.BlockSpec((1, d), lambda i: (0, 0))],
        out_specs=pl.BlockSpec((tm, d), lambda i: (i, 0)),
        out_shape=jax.ShapeDtypeStruct((t, d), BF16),
        compiler_params=_cp(20 * tm * d),
    )(x, g)


def _inproj_own(place, a, w_own):
    t, k = a.shape
    n = w_own.shape[1]
    tm = min(TM_PROJ, t)

    def body(place_ref, a_ref, b_ref, o_ref):
        del place_ref
        o_ref[...] = jnp.dot(a_ref[...], b_ref[...], preferred_element_type=F32)

    return _pallas(
        body, name="inproj_own",
        grid_spec=pltpu.PrefetchScalarGridSpec(
            num_scalar_prefetch=1, grid=(t // tm,),
            in_specs=[pl.BlockSpec((tm, k), lambda i, p: (i, 0)), pl.BlockSpec((k, n), lambda i, p: (0, 0))],
            out_specs=pl.BlockSpec((tm, n), lambda i, p: (i, p[1]))),
        out_shape=jax.ShapeDtypeStruct((t, N_CHIPS * n), F32),
        compiler_params=_cp(2 * (2 * tm * k + 2 * k * n + 4 * tm * n) + 4 * MIB),
    )(place, a, w_own)


def _inproj_rest(place, a, b, partial, after):
    t, k = a.shape
    nb, _, n = b.shape
    tm = min(TM_PROJ, t)
    shard = lambda s, p: (p[1] + 1 + s) % nb

    def body(place_ref, a_ref, b_ref, after_ref, partial_ref, o_ref):
        del place_ref, after_ref, partial_ref
        o_ref[...] = jnp.dot(a_ref[...], b_ref[...], preferred_element_type=F32)

    return _pallas(
        body, name="inproj_rest",
        grid_spec=pltpu.PrefetchScalarGridSpec(
            num_scalar_prefetch=1, grid=(nb - 1, t // tm),
            in_specs=[pl.BlockSpec((tm, k), lambda s, i, p: (i, 0)),
                      pl.BlockSpec((None, k, n), lambda s, i, p: (shard(s, p), 0, 0)),
                      pl.BlockSpec((SUBLANES, SLAB), lambda s, i, p: (0, 0)),
                      pl.BlockSpec(memory_space=pl.ANY)],
            out_specs=pl.BlockSpec((tm, n), lambda s, i, p: (i, shard(s, p)))),
        out_shape=jax.ShapeDtypeStruct((t, nb * n), F32),
        input_output_aliases={4: 0},
        compiler_params=_cp(2 * (2 * tm * k + 2 * k * n + 4 * tm * n) + 4 * MIB),
    )(place, a, b, after, partial)


def _matmul_tn(a, b, nb, name):
    t, m = a.shape
    n = b.shape[1] // nb
    tk = min(TK_TN, t)
    tma = min(TM_TN, m)

    def body(a_ref, b_ref, o_ref, o16_ref):
        k = pl.program_id(2)

        @pl.when(k == 0)
        def _():
            o_ref[...] = jnp.zeros_like(o_ref)

        o_ref[...] += lax.dot_general(a_ref[...], b_ref[...], (((0,), (0,)), ((), ())), preferred_element_type=F32)

        @pl.when(k == t // tk - 1)
        def _():
            o16_ref[...] = o_ref[...].astype(BF16)

    blk = pl.BlockSpec((None, tma, n), lambda j, i, k: (j, i, 0))
    return _pallas(
        body, name=name, grid=(nb, m // tma, t // tk),
        in_specs=[pl.BlockSpec((tk, tma), lambda j, i, k: (k, i)), pl.BlockSpec((tk, n), lambda j, i, k: (k, j))],
        out_specs=[blk, blk],
        out_shape=[jax.ShapeDtypeStruct((nb, m, n), F32), jax.ShapeDtypeStruct((nb, m, n), BF16)],
        compiler_params=_cp(2 * (2 * tk * tma + 2 * tk * n + 6 * tma * n) + 8 * MIB),
    )(a, b)


def _outproj(mix, w_out, x, tgt, g_post):
    t, dm = mix.shape
    d = w_out.shape[1]
    tm = min(TM_OUT, t)

    def body(mix_ref, w_ref, x_ref, t_ref, g_ref, loss_ref, dout_ref, do_ref, dmix_ref, gg_ref):
        @pl.when(pl.program_id(0) == 0)
        def _():
            loss_ref[...] = jnp.zeros_like(loss_ref)
            gg_ref[...] = jnp.zeros_like(gg_ref)

        w = w_ref[...]
        o = jnp.dot(mix_ref[...], w, preferred_element_type=F32)
        r = lax.rsqrt(jnp.mean(o * o, axis=-1, keepdims=True) + EPS)
        nh = o * r
        g = g_ref[...]
        e = x_ref[...] + nh * g - t_ref[...]
        loss_ref[...] += jnp.sum(e * e) * (0.5 / d)
        dout = e * (1.0 / d)
        dout_ref[...] = dout
        gg_ref[0:1, :] += jnp.sum(dout * nh, axis=0, keepdims=True)
        dn = dout * g
        do = r * (dn - nh * jnp.mean(dn * nh, axis=-1, keepdims=True))
        dob = do.astype(BF16)
        do_ref[...] = dob
        dmix_ref[...] = lax.dot_general(dob, w, (((1,), (1,)), ((), ())), preferred_element_type=F32)

    row = lambda i: (i, 0)
    fixed = lambda i: (0, 0)
    return _pallas(
        body, name="outproj", grid=(t // tm,),
        in_specs=[pl.BlockSpec((tm, dm), row), pl.BlockSpec((dm, d), fixed), pl.BlockSpec((tm, d), row),
                  pl.BlockSpec((tm, d), row), pl.BlockSpec((1, d), fixed)],
        out_specs=[pl.BlockSpec((SUBLANES, SLAB), fixed), pl.BlockSpec((tm, d), row), pl.BlockSpec((tm, d), row),
                   pl.BlockSpec((tm, dm), row), pl.BlockSpec((SUBLANES, d), fixed)],
        out_shape=[jax.ShapeDtypeStruct((SUBLANES, SLAB), F32), jax.ShapeDtypeStruct((t, d), F32),
                   jax.ShapeDtypeStruct((t, d), BF16), jax.ShapeDtypeStruct((t, dm), F32),
                   jax.ShapeDtypeStruct((SUBLANES, d), F32)],
        compiler_params=_cp(2 * (2 * dm * d + tm * (2 * dm + 4 * d * 3 + 2 * d + 4 * dm)) + 16 * MIB),
    )(mix, w_out, x, tgt, g_post)


def _dh_prenorm_bwd(dproj, w_in, x, dout, g_pre):
    t, d = x.shape
    nb, _, n = w_in.shape
    tm = min(TM_DH, t)

    def body(dp_ref, w_ref, x_ref, dout_ref, g_ref, dx_ref, gg_ref):
        @pl.when(pl.program_id(0) == 0)
        def _():
            gg_ref[...] = jnp.zeros_like(gg_ref)

        dh = None
        for k in range(nb):
            part = lax.dot_general(dp_ref[:, k * n:(k + 1) * n], w_ref[k], (((1,), (1,)), ((), ())),
                                   preferred_element_type=F32)
            dh = part if dh is None else dh + part
        xv = x_ref[...]
        r = lax.rsqrt(jnp.mean(xv * xv, axis=-1, keepdims=True) + EPS)
        xh = xv * r
        gg_ref[0:1, :] += jnp.sum(dh * xh, axis=0, keepdims=True)
        dg = dh * g_ref[...]
        dx_ref[...] = dout_ref[...] + r * (dg - xh * jnp.mean(dg * xh, axis=-1, keepdims=True))

    row = lambda i: (i, 0)
    fixed = lambda i: (0, 0)
    w_spec = pl.BlockSpec(w_in.shape, lambda i: (0, 0, 0), pipeline_mode=pl.Buffered(1))
    return _pallas(
        body, name="dh_prenorm_bwd", grid=(t // tm,),
        in_specs=[pl.BlockSpec((tm, nb * n), row), w_spec,
                  pl.BlockSpec((tm, d), row), pl.BlockSpec((tm, d), row), pl.BlockSpec((1, d), fixed)],
        out_specs=[pl.BlockSpec((tm, d), row), pl.BlockSpec((SUBLANES, d), fixed)],
        out_shape=[jax.ShapeDtypeStruct((t, d), F32), jax.ShapeDtypeStruct((SUBLANES, d), F32)],
        compiler_params=_cp(2 * nb * d * n + 2 * (2 * tm * nb * n + 12 * tm * d) + 16 * tm * d + 4 * MIB),
    )(dproj, w_in, x, dout, g_pre)


def _adamw(w, g, m, v, name):
    r, c = w.shape
    tr = min(TR_ELT, r)
    c1 = 1.0 - ADAM_B1 ** ADAM_STEP
    c2 = 1.0 - ADAM_B2 ** ADAM_STEP

    def body(w_ref, g_ref, m_ref, v_ref, d_ref, mo_ref, vo_ref):
        gv = g_ref[...]
        mn = ADAM_B1 * m_ref[...] + (1.0 - ADAM_B1) * gv
        vn = ADAM_B2 * v_ref[...] + (1.0 - ADAM_B2) * (gv * gv)
        d_ref[...] = -ADAM_LR * ((mn / c1) / (jnp.sqrt(vn / c2) + ADAM_EPS) + ADAM_WD * w_ref[...])
        mo_ref[...] = mn
        vo_ref[...] = vn

    spec = pl.BlockSpec((tr, c), lambda i: (i, 0))
    sds = jax.ShapeDtypeStruct((r, c), F32)
    return _pallas(
        body, name=name, grid=(r // tr,), in_specs=[spec] * 4, out_specs=[spec] * 3, out_shape=[sds] * 3,
        compiler_params=_cp(2 * 7 * 4 * tr * c + 8 * MIB),
    )(w, g, m, v)


def _adamw_small(g, w, m, v):
    r, c = w.shape
    c1 = 1.0 - ADAM_B1 ** ADAM_STEP
    c2 = 1.0 - ADAM_B2 ** ADAM_STEP

    def body(g_ref, w_ref, m_ref, v_ref, d_ref, mo_ref, vo_ref):
        gv = g_ref[...]
        mn = ADAM_B1 * m_ref[...] + (1.0 - ADAM_B1) * gv
        vn = ADAM_B2 * v_ref[...] + (1.0 - ADAM_B2) * (gv * gv)
        d_ref[...] = -ADAM_LR * ((mn / c1) / (jnp.sqrt(vn / c2) + ADAM_EPS) + ADAM_WD * w_ref[...])
        mo_ref[...] = mn
        vo_ref[...] = vn

    sds = jax.ShapeDtypeStruct((r, c), F32)
    return _pallas(
        body, name="adamw_small", out_shape=[sds] * 3,
        compiler_params=_cp(12 * 4 * r * c + 8 * MIB),
    )(g, w, m, v)


def _allreduce_small(v):
    r, c = v.shape
    assert r % (2 * SUBLANES) == 0
    h = r // 2

    def body(v_ref, out_ref, got_ref, chip_ref, slots_ref, send, recv):
        x, y, cc = _my_place()
        k = 2 * x + y
        sib = (x, y, 1 - cc)

        def rc(s, src, dst, to):
            return pltpu.make_async_remote_copy(src_ref=src, dst_ref=dst, send_sem=send.at[s], recv_sem=recv.at[s],
                                                device_id=to, device_id_type=MESH)

        pair = rc(0, v_ref, got_ref, sib)
        pair.start()
        pair.wait()
        chip_ref[...] = v_ref[...] + got_ref[...]
        mine = pl.ds(pl.multiple_of(cc * h, SUBLANES), h)
        theirs = pl.ds(pl.multiple_of((1 - cc) * h, SUBLANES), h)
        chips = _other_chips(x, y)
        sent = []
        for j, (cx, cy) in enumerate(chips):
            cp = rc(1 + j, chip_ref.at[mine], slots_ref.at[k], (cx, cy, cc))
            cp.start()
            sent.append(cp)
        slots_ref[k] = chip_ref[mine, :]
        for j, (cx, cy) in enumerate(chips):
            rc(1 + j, chip_ref.at[mine], slots_ref.at[2 * cx + cy], (cx, cy, cc)).wait_recv()
        total = slots_ref[0]
        for kk in range(1, N_CHIPS):
            total = total + slots_ref[kk]
        out_ref[mine, :] = total
        for cp in sent:
            cp.wait_send()
        share = rc(N_CHIPS, out_ref.at[mine], out_ref.at[mine], sib)
        share.start()
        rc(N_CHIPS, out_ref.at[theirs], out_ref.at[theirs], sib).wait_recv()
        share.wait_send()

    vm = pl.BlockSpec(memory_space=pltpu.VMEM)
    return _pallas(
        body, name="allreduce_small", out_shape=jax.ShapeDtypeStruct((r, c), F32), in_specs=[vm], out_specs=vm,
        scratch_shapes=[pltpu.VMEM((r, c), F32), pltpu.VMEM((r, c), F32), pltpu.VMEM((N_CHIPS, h, c), F32),
                        pltpu.SemaphoreType.DMA((N_CHIPS + 1,)), pltpu.SemaphoreType.DMA((N_CHIPS + 1,))],
        compiler_params=_cp(6 * 4 * r * c + 8 * MIB),
    )(v)


def _zoh(a_re, a_im, log_dt, b_re2, b_im2, expand):
    dt = jnp.exp(log_dt)
    mag = jnp.exp(a_re * dt)
    lbr, lbi = mag * jnp.cos(a_im * dt), mag * jnp.sin(a_im * dt)
    nr, ni = lbr - 1.0, lbi
    den = a_re * a_re + a_im * a_im
    qr = (nr * a_re + ni * a_im) / den
    qi = (ni * a_re - nr * a_im) / den
    qr2 = jnp.dot(qr, expand, precision=lax.Precision.HIGHEST, preferred_element_type=F32)
    qi2 = jnp.dot(qi, expand, precision=lax.Precision.HIGHEST, preferred_element_type=F32)
    return lbr, lbi, qr2 * b_re2 - qi2 * b_im2, qr2 * b_im2 + qi2 * b_re2


def _zoh_fwd(a_re, a_im, log_dt, b_re2, b_im2, expand, power):
    g, p = a_re.shape

    def body(ar_ref, ai_ref, ld_ref, br_ref, bi_ref, e_ref, lbr_ref, lbi_ref, bbr_ref, bbi_ref, pw_ref):
        ar, ai, ld = ar_ref[...], ai_ref[...], ld_ref[...]
        lbr, lbi, bbr, bbi = _zoh(ar, ai, ld, br_ref[...], bi_ref[...], e_ref[...])
        lbr_ref[...], lbi_ref[...], bbr_ref[...], bbi_ref[...] = lbr, lbi, bbr, bbi
        dt = jnp.exp(ld) * float(power)
        mag = jnp.exp(ar * dt)
        pw_ref[0] = mag * jnp.cos(ai * dt)
        pw_ref[1] = mag * jnp.sin(ai * dt)

    gp = jax.ShapeDtypeStruct((g, p), F32)
    gph = jax.ShapeDtypeStruct(b_re2.shape, F32)
    return _pallas(
        body, name="zoh_fwd", out_shape=[gp, gp, gph, gph, jax.ShapeDtypeStruct((2, g, p), F32)],
        compiler_params=_cp(16 * MIB),
    )(a_re, a_im, log_dt, b_re2, b_im2, expand)


def _zoh_bwd(a_re, a_im, log_dt, b_re2, b_im2, expand, g_lbr, g_lbi, g_bbr, g_bbi):
    def body(ar_ref, ai_ref, ld_ref, br_ref, bi_ref, e_ref, c0, c1, c2, c3, gar, gai, gld, gbr, gbi):
        e = e_ref[...]
        _, vjp = jax.vjp(lambda a, b, c, d, f: _zoh(a, b, c, d, f, e),
                         ar_ref[...], ai_ref[...], ld_ref[...], br_ref[...], bi_ref[...])
        gar[...], gai[...], gld[...], gbr[...], gbi[...] = vjp((c0[...], c1[...], c2[...], c3[...]))

    sds = lambda a: jax.ShapeDtypeStruct(a.shape, F32)
    return _pallas(
        body, name="zoh_bwd", out_shape=[sds(a_re), sds(a_im), sds(log_dt), sds(b_re2), sds(b_im2)],
        compiler_params=_cp(16 * MIB),
    )(a_re, a_im, log_dt, b_re2, b_im2, expand, g_lbr, g_lbi, g_bbr, g_bbi)


def _blockdiag(blocks):
    nq, _, r, c = blocks.shape
    eye = jnp.eye(GROUPS_PER_SLAB, dtype=blocks.dtype)
    out = blocks[:, :, :, None, :] * eye[None, :, None, :, None]
    return out.reshape(nq, GROUPS_PER_SLAB * r, GROUPS_PER_SLAB * c)


def _blockdiag_take(dense, r, c):
    nq = dense.shape[0]
    d5 = dense.reshape(nq, GROUPS_PER_SLAB, r, GROUPS_PER_SLAB, c)
    return jnp.stack([d5[:, i, :, i, :] for i in range(GROUPS_PER_SLAB)], axis=1)


def _scan_slab(s_ref, row0, q, lam_ref, pw_ref, car_ref, steps, reverse, prev_ref=None, prev_row0=0, glam_ref=None):
    sign = -1.0 if reverse else 1.0
    half = SLAB_STATES // SLAB
    cols = [(q * 2 * half + m, q * 2 * half + half + m, q * SLAB_STATES + m * SLAB) for m in range(half)]
    nm = len(cols)
    full = (SUBLANES, SLAB)
    lam = [(jnp.broadcast_to(lam_ref[0:1, pl.ds(cl, SLAB)], full),
            jnp.broadcast_to(sign * lam_ref[1:2, pl.ds(cl, SLAB)], full)) for (_, _, cl) in cols]

    def step_rows(jj, base):
        j = (steps - 1 - jj) if reverse else jj
        return j, pl.ds(base + j * SUBLANES, SUBLANES)

    def pass1(jj, car):
        _, rows = step_rows(jj, row0)
        out = []
        for m, (cr, ci, _) in enumerate(cols):
            sr, si = car[2 * m], car[2 * m + 1]
            lr, li = lam[m]
            nr = lr * sr - li * si + s_ref[cr, rows, :]
            ni = lr * si + li * sr + s_ref[ci, rows, :]
            s_ref[cr, rows, :] = nr
            s_ref[ci, rows, :] = ni
            out += [nr, ni]
        return tuple(out)

    def run_steps(step_fn, carry):
        for jj in range(steps):
            carry = step_fn(jj, carry)
        return carry

    ends = run_steps(pass1, tuple(jnp.zeros(full, F32) for _ in range(2 * nm)))

    entry = []
    for m, (cr, ci, cl) in enumerate(cols):
        ljr = pw_ref[0:1, pl.ds(cl, SLAB)]
        lji = sign * pw_ref[1:2, pl.ds(cl, SLAB)]
        c_r = car_ref[cr, 0:1, :]
        c_i = car_ref[ci, 0:1, :]
        rows_r, rows_i = [None] * SUBLANES, [None] * SUBLANES
        order = range(SUBLANES - 1, -1, -1) if reverse else range(SUBLANES)
        for b in order:
            rows_r[b], rows_i[b] = c_r, c_i
            e_r, e_i = ends[2 * m][b:b + 1], ends[2 * m + 1][b:b + 1]
            c_r, c_i = ljr * c_r - lji * c_i + e_r, ljr * c_i + lji * c_r + e_i
        car_ref[cr, 0:1, :] = c_r
        car_ref[ci, 0:1, :] = c_i
        entry.append((jnp.concatenate(rows_r, axis=0), jnp.concatenate(rows_i, axis=0)))

    def pass2(jj, carry):
        j, rows = step_rows(jj, row0)
        decayed, acc = carry[:2 * nm], carry[2 * nm:]
        out_d, out_a = [], []
        for m, (cr, ci, cl) in enumerate(cols):
            lr, li = lam[m]
            dr, di = decayed[2 * m], decayed[2 * m + 1]
            dr, di = lr * dr - li * di, lr * di + li * dr
            nr = s_ref[cr, rows, :] + dr
            ni = s_ref[ci, rows, :] + di
            s_ref[cr, rows, :] = nr
            s_ref[ci, rows, :] = ni
            out_d += [dr, di]
            if prev_ref is not None:
                prow = pl.ds(prev_row0 + (j - 1) * SUBLANES, SUBLANES)
                qr = prev_ref[cr, prow, :]
                qi = prev_ref[ci, prow, :]
                out_a += [acc[2 * m] + (nr * qr + ni * qi), acc[2 * m + 1] + (ni * qr - nr * qi)]
        return tuple(out_d) + tuple(out_a)

    n_acc = 2 * nm if prev_ref is not None else 0
    init = tuple(e for pair in entry for e in pair) + tuple(jnp.zeros(full, F32) for _ in range(n_acc))
    accs = run_steps(pass2, init)[2 * nm:]
    if prev_ref is not None:
        for m, (_, _, cl) in enumerate(cols):
            glam_ref[0:1, pl.ds(cl, SLAB)] += jnp.sum(accs[2 * m], axis=0, keepdims=True)
            glam_ref[1:2, pl.ds(cl, SLAB)] += jnp.sum(accs[2 * m + 1], axis=0, keepdims=True)


def _permute_rows_f32(perm_bf16, v):
    hi = v.astype(BF16)
    lo = (v - hi.astype(F32)).astype(BF16)
    return (jnp.dot(perm_bf16, hi, preferred_element_type=F32) + jnp.dot(perm_bf16, lo, preferred_element_type=F32))


def _put_slab(s_ref, rows, q, val):
    per = 2 * SLAB_STATES // SLAB
    for i in range(per):
        s_ref[q * per + i, rows, :] = val[:, i * SLAB:(i + 1) * SLAB]


def _get_slab(s_ref, rows, q):
    per = 2 * SLAB_STATES // SLAB
    return jnp.concatenate([s_ref[q * per + i, rows, :] for i in range(per)], axis=1)


def _step_major_perm(tt):
    r = jnp.arange(tt)
    held = (r % SUBLANES) * (tt // SUBLANES) + r // SUBLANES
    return held[:, None] == r[None, :]


def _ssm_fwd(proj, u_block, bq, cq, lam, pw, d_skip):
    t = proj.shape[0]
    nq, ds, w2 = bq.shape
    assert ds == SLAB and w2 == 2 * SLAB_STATES
    dssm = nq * SLAB
    width = nq * w2
    ntile = width // SLAB
    tt = min(T_SCAN, t)
    steps = tt // SUBLANES
    perm = _step_major_perm(tt)
    pm, pmt = perm.astype(BF16), perm.T.astype(BF16)

    def body(u_ref, pm_ref, pmt_ref, bq_ref, cq_ref, lam_ref, pw_ref, d_ref, y_ref, cin_ref, s_ref, car_ref):
        @pl.when(pl.program_id(0) == 0)
        def _():
            car_ref[...] = jnp.zeros_like(car_ref)

        cin_ref[...] = jnp.broadcast_to(car_ref[:, 0:1, :], cin_ref.shape)
        u = u_ref[...]
        ub = jnp.dot(pm_ref[...], u.astype(BF16), preferred_element_type=F32).astype(BF16)
        everything = slice(None)
        for q in range(nq):
            _put_slab(s_ref, everything, q,
                      jnp.dot(ub[:, q * SLAB:(q + 1) * SLAB], bq_ref[q], preferred_element_type=F32))
        for q in range(nq):
            _scan_slab(s_ref, 0, q, lam_ref, pw_ref, car_ref, steps, reverse=False)
        y_sm = jnp.concatenate(
            [jnp.dot(_get_slab(s_ref, everything, q).astype(BF16), cq_ref[q], preferred_element_type=F32)
             for q in range(nq)], axis=1)
        y_ref[...] = _permute_rows_f32(pmt_ref[...], y_sm) + d_ref[...] * u

    c3 = lambda i: (0, 0, 0)
    c2 = lambda i: (0, 0)
    return _pallas(
        body, name="ssm_fwd", grid=(t // tt,),
        in_specs=[pl.BlockSpec((tt, dssm), lambda i: (i, u_block)), pl.BlockSpec((tt, tt), c2),
                  pl.BlockSpec((tt, tt), c2), pl.BlockSpec(bq.shape, c3),
                  pl.BlockSpec(cq.shape, c3), pl.BlockSpec(lam.shape, c2), pl.BlockSpec(pw.shape, c2),
                  pl.BlockSpec((1, dssm), c2)],
        out_specs=[pl.BlockSpec((tt, dssm), lambda i: (i, 0)),
                   pl.BlockSpec((None, ntile, SUBLANES, SLAB), lambda i: (i, 0, 0, 0))],
        out_shape=[jax.ShapeDtypeStruct((t, dssm), F32), jax.ShapeDtypeStruct((t // tt, ntile, SUBLANES, SLAB), F32)],
        scratch_shapes=[pltpu.VMEM((ntile, tt, SLAB), F32), pltpu.VMEM((ntile, SUBLANES, SLAB), F32)],
        compiler_params=_cp(2 * (8 * tt * dssm + 4 * nq * ds * w2) + 4 * tt * width + 16 * MIB,
                            dimension_semantics=("arbitrary",)),
    )(proj, pm, pmt, bq, cq, lam, pw, d_skip)


def _ssm_bwd(proj, u_block, dy, cin, dproj, bq, cq, lam, pw, d_skip):
    t = proj.shape[0]
    nq, ds, w2 = bq.shape
    dssm = nq * SLAB
    width = nq * w2
    ntile = width // SLAB
    tt = min(T_SCAN, t)
    nt = t // tt
    steps = tt // SUBLANES
    halo = SUBLANES
    perm = _step_major_perm(tt)
    pm, pmt = perm.astype(BF16), perm.T.astype(BF16)

    def body(u_ref, dy_ref, cin_ref, dp_any, pm_ref, pmt_ref, bq_ref, cq_ref, lam_ref, pw_ref, d_ref,
             du_ref, gb_ref, gc_ref, glam_ref, gd_ref, s_ref, gs_ref, car_f, car_b):
        del dp_any

        @pl.when(pl.program_id(0) == 0)
        def _():
            car_b[...] = jnp.zeros_like(car_b)
            gb_ref[...] = jnp.zeros_like(gb_ref)
            gc_ref[...] = jnp.zeros_like(gc_ref)
            glam_ref[...] = jnp.zeros_like(glam_ref)
            gd_ref[...] = jnp.zeros_like(gd_ref)

        u = u_ref[...]
        dyv = dy_ref[...]
        gd_ref[0:1, :] += jnp.sum(dyv * u, axis=0, keepdims=True)
        pmv = pm_ref[...]
        ub = jnp.dot(pmv, u.astype(BF16), preferred_element_type=F32).astype(BF16)
        dyb = jnp.dot(pmv, dyv.astype(BF16), preferred_element_type=F32).astype(BF16)
        car_f[...] = cin_ref[...]
        data = slice(halo, halo + tt)
        everything = slice(None)
        for q in range(nq):
            _put_slab(s_ref, data, q, jnp.dot(ub[:, q * SLAB:(q + 1) * SLAB], bq_ref[q], preferred_element_type=F32))
        for q in range(nq):
            _scan_slab(s_ref, halo, q, lam_ref, pw_ref, car_f, steps, reverse=False)
        last_step = s_ref[:, halo + tt - SUBLANES:halo + tt, :]
        s_ref[:, 0:halo, :] = jnp.concatenate([cin_ref[:, 0:1, :], last_step[:, 0:SUBLANES - 1, :]], axis=1)
        tn = (((0,), (0,)), ((), ()))
        nt_dims = (((1,), (1,)), ((), ()))
        for q in range(nq):
            sl = slice(q * SLAB, (q + 1) * SLAB)
            gc_ref[q] += lax.dot_general(dyb[:, sl], _get_slab(s_ref, data, q).astype(BF16), tn,
                                         preferred_element_type=F32)
            _put_slab(gs_ref, everything, q,
                      lax.dot_general(dyb[:, sl], cq_ref[q], nt_dims, preferred_element_type=F32))
        for q in range(nq):
            _scan_slab(gs_ref, 0, q, lam_ref, pw_ref, car_b, steps, reverse=True,
                       prev_ref=s_ref, prev_row0=halo, glam_ref=glam_ref)
        du_parts = []
        for q in range(nq):
            sl = slice(q * SLAB, (q + 1) * SLAB)
            gsb = _get_slab(gs_ref, everything, q).astype(BF16)
            du_parts.append(lax.dot_general(gsb, bq_ref[q], nt_dims, preferred_element_type=F32))
            gb_ref[q] += lax.dot_general(ub[:, sl], gsb, tn, preferred_element_type=F32)
        du_sm = jnp.concatenate(du_parts, axis=1)
        du_ref[...] = (_permute_rows_f32(pmt_ref[...], du_sm) + dyv * d_ref[...]).astype(BF16)

    c3 = lambda i: (0, 0, 0)
    c2 = lambda i: (0, 0)
    rev = lambda i: (nt - 1 - i, 0)
    dense = jax.ShapeDtypeStruct((nq, SLAB, w2), F32)
    gp = lam.shape[1]
    return _pallas(
        body, name="ssm_bwd", grid=(nt,),
        in_specs=[pl.BlockSpec((tt, dssm), lambda i: (nt - 1 - i, u_block)), pl.BlockSpec((tt, dssm), rev),
                  pl.BlockSpec((None, ntile, SUBLANES, SLAB), lambda i: (nt - 1 - i, 0, 0, 0)),
                  pl.BlockSpec(memory_space=pl.ANY), pl.BlockSpec((tt, tt), c2), pl.BlockSpec((tt, tt), c2),
                  pl.BlockSpec(bq.shape, c3), pl.BlockSpec(cq.shape, c3),
                  pl.BlockSpec(lam.shape, c2), pl.BlockSpec(pw.shape, c2), pl.BlockSpec((1, dssm), c2)],
        out_specs=[pl.BlockSpec((tt, dssm), lambda i: (nt - 1 - i, u_block)), pl.BlockSpec(dense.shape, c3),
                   pl.BlockSpec(dense.shape, c3), pl.BlockSpec((SUBLANES, gp), c2), pl.BlockSpec((SUBLANES, dssm), c2)],
        out_shape=[jax.ShapeDtypeStruct(dproj.shape, dproj.dtype), dense, dense,
                   jax.ShapeDtypeStruct((SUBLANES, gp), F32), jax.ShapeDtypeStruct((SUBLANES, dssm), F32)],
        scratch_shapes=[pltpu.VMEM((ntile, tt + halo, SLAB), F32), pltpu.VMEM((ntile, tt, SLAB), F32),
                        pltpu.VMEM((ntile, SUBLANES, SLAB), F32), pltpu.VMEM((ntile, SUBLANES, SLAB), F32)],
        input_output_aliases={3: 0},
        compiler_params=_cp(2 * (10 * tt * dssm + 4 * nq * ds * w2 + 8 * nq * SLAB * w2)
                            + 8 * tt * width + 12 * MIB, dimension_semantics=("arbitrary",)),
    )(proj, dy, cin, dproj, pm, pmt, bq, cq, lam, pw, d_skip)


def _gate_fwd(proj, y, conv_w, conv_b, w_glu, b_glu, dc):
    t = proj.shape[0]
    dssm = y.shape[1]
    assert dc == dssm
    tm = min(TM_GATE, t)
    halo = SUBLANES

    rb = 2 * SUBLANES
    assert tm % rb == 0

    def body(b_ref, c_ref, v_ref, zc_ref, zs_ref, y_ref, cw_ref, cb_ref, wg_ref, bg_ref, mix_ref,
             cv_buf, sh1_buf, sh2_buf, ge_buf, geb_buf, gl_buf):
        @pl.when(pl.program_id(0) == 0)
        def _():
            cv_buf[0:halo, :] = jnp.zeros((halo, dc), F32)

        cv_buf[halo:, :] = c_ref[...] * v_ref[...]
        sh1_buf[...] = cv_buf[halo - 1:halo - 1 + tm, :]
        sh2_buf[...] = cv_buf[halo - 2:halo - 2 + tm, :]
        w0, w1, w2, cb = cw_ref[0:1, :], cw_ref[1:2, :], cw_ref[2:3, :], cb_ref[...]
        chunks = [pl.ds(r * rb, rb) for r in range(tm // rb)]
        for r, rs in enumerate(chunks):
            ge, _ = _gelu(y_ref[rs, :])
            ge_buf[rs, :] = ge
            geb_buf[rs, :] = ge.astype(BF16)
        gl_buf[...] = jnp.dot(geb_buf[...], wg_ref[...], preferred_element_type=F32) + bg_ref[...]
        for r, rs in enumerate(chunks):
            conv = cb + w2 * cv_buf[pl.ds(halo + r * rb, rb), :] + w1 * sh1_buf[rs, :] + w0 * sh2_buf[rs, :]
            sz, _ = _silu(zc_ref[rs, :])
            mix_ref[rs, 0:dc] = (b_ref[rs, :] * conv * sz).astype(BF16)
        cv_buf[0:halo, :] = cv_buf[tm:tm + halo, :]
        for r, rs in enumerate(chunks):
            szs, _ = _silu(zs_ref[rs, :])
            mix_ref[rs, dc:] = (ge_buf[rs, :] * jax.nn.sigmoid(gl_buf[rs, :]) * szs).astype(BF16)

    col = lambda j: pl.BlockSpec((tm, dc), lambda i, j=j: (i, j))
    fixed = lambda i: (0, 0)
    return _pallas(
        body, name="gate_fwd", grid=(t // tm,),
        in_specs=[col(0), col(1), col(2), col(3), col(5), pl.BlockSpec((tm, dssm), lambda i: (i, 0)),
                  pl.BlockSpec(conv_w.shape, fixed), pl.BlockSpec((1, dc), fixed),
                  pl.BlockSpec(w_glu.shape, fixed), pl.BlockSpec((1, dssm), fixed)],
        out_specs=pl.BlockSpec((tm, dc + dssm), lambda i: (i, 0)),
        out_shape=jax.ShapeDtypeStruct((t, dc + dssm), BF16),
        scratch_shapes=[pltpu.VMEM((tm + halo, dc), F32), pltpu.VMEM((tm, dc), F32), pltpu.VMEM((tm, dc), F32),
                        pltpu.VMEM((tm, dssm), F32), pltpu.VMEM((tm, dssm), BF16), pltpu.VMEM((tm, dssm), F32)],
        compiler_params=_cp(2 * (6 * 4 * tm * dc + 2 * tm * (dc + dssm) + 2 * dssm * dssm) + 24 * tm * dc + 8 * MIB,
                            dimension_semantics=("arbitrary",)),
    )(proj, proj, proj, proj, proj, y, conv_w, conv_b, w_glu, b_glu)


def _gate_bwd(proj, y, dmix, conv_w, conv_b, w_glu, b_glu, dc):
    t = proj.shape[0]
    dssm = y.shape[1]
    tm = min(TM_GATE, t)
    nt = t // tm
    halo = SUBLANES
    blocks_per_tile = tm // halo

    rb = 2 * SUBLANES
    assert tm % rb == 0
    n_chunk = tm // rb

    def body(b_ref, c_ref, v_ref, zc_ref, zs_ref, cp_ref, vp_ref, y_ref, dm_ref, cw_ref, cb_ref, wg_ref, bg_ref,
             dp_ref, dy_ref, gs_ref, gwg_ref, cv_buf, dc_buf, sh1_buf, sh2_buf, geb_buf, dglb_buf, ge_buf, th_buf):
        i = pl.program_id(0)

        @pl.when(i == 0)
        def _():
            dc_buf[tm:, :] = jnp.zeros((halo, dc), F32)
            gs_ref[...] = jnp.zeros_like(gs_ref)
            gwg_ref[...] = jnp.zeros_like(gwg_ref)

        def rows(r):
            return pl.ds(r * rb, rb)

        def rows_after_halo(r):
            return pl.ds(halo + r * rb, rb)

        def chunks(step, carry):
            for r in range(n_chunk):
                carry = step(r, carry)
            return carry

        def fold(v):
            return v[0:SUBLANES] + v[SUBLANES:rb]

        first_tile = (i == nt - 1)
        cv_buf[0:halo, :] = jnp.where(first_tile, 0.0, cp_ref[...] * vp_ref[...])
        cv_buf[halo:, :] = c_ref[...] * v_ref[...]
        sh1_buf[...] = cv_buf[halo - 1:halo - 1 + tm, :]
        sh2_buf[...] = cv_buf[halo - 2:halo - 2 + tm, :]
        w0, w1, w2, cb = cw_ref[0:1, :], cw_ref[1:2, :], cw_ref[2:3, :], cb_ref[...]

        def conv_pass_a(r, carry):
            rs = rows(r)
            bv, zc, dyc = b_ref[rs, :], zc_ref[rs, :], dm_ref[rs, 0:dc]
            conv = cb + w2 * cv_buf[rows_after_halo(r), :] + w1 * sh1_buf[rs, :] + w0 * sh2_buf[rs, :]
            sz, sgc = _silu(zc)
            dp_ref[rs, 0:dc] = (dyc * conv * sz).astype(BF16)
            dp_ref[rs, 3 * dc:4 * dc] = (dyc * bv * conv * _dsilu(zc, sgc)).astype(BF16)
            dc_buf[rs, :] = dyc * bv * sz
            return carry

        chunks(conv_pass_a, 0)
        sh1_buf[...] = dc_buf[1:1 + tm, :]
        sh2_buf[...] = dc_buf[2:2 + tm, :]

        def conv_pass_b(r, acc):
            rs = rows(r)
            dconv, d1, d2 = dc_buf[rs, :], sh1_buf[rs, :], sh2_buf[rs, :]
            cv = cv_buf[rows_after_halo(r), :]
            dcv = w2 * dconv + w1 * d1 + w0 * d2
            dp_ref[rs, dc:2 * dc] = (dcv * v_ref[rs, :]).astype(BF16)
            dp_ref[rs, 2 * dc:3 * dc] = (dcv * c_ref[rs, :]).astype(BF16)
            return (acc[0] + fold(cv * d2), acc[1] + fold(cv * d1), acc[2] + fold(cv * dconv), acc[3] + fold(dconv))

        zero = jnp.zeros((SUBLANES, dc), F32)
        sums = chunks(conv_pass_b, (zero, zero, zero, zero))
        for k in range(4):
            gs_ref[k:k + 1, :] += jnp.sum(sums[k], axis=0, keepdims=True)
        dc_buf[tm:, :] = dc_buf[0:halo, :]

        def glu_pass_a(r, carry):
            rs = rows(r)
            ge, th = _gelu(y_ref[rs, :])
            geb_buf[rs, :] = ge.astype(BF16)
            ge_buf[rs, :] = ge
            th_buf[rs, :] = th
            return carry

        chunks(glu_pass_a, 0)
        wg = wg_ref[...]
        gl_buf = cv_buf.at[halo:halo + tm]
        keep_buf = dc_buf.at[0:tm]
        gl_buf[...] = jnp.dot(geb_buf[...], wg, preferred_element_type=F32) + bg_ref[...]
        dp_ref[:, 4 * dc:5 * dc] = jnp.zeros((tm, dc), BF16)

        def glu_pass_b(r, acc):
            rs = rows(r)
            ge = ge_buf[rs, :]
            sg = jax.nn.sigmoid(gl_buf[rs, :])
            zs, dys = zs_ref[rs, :], dm_ref[rs, dc:]
            szs, sgs = _silu(zs)
            dp_ref[rs, 5 * dc:] = (dys * ge * sg * _dsilu(zs, sgs)).astype(BF16)
            d_ys = dys * szs
            dgl = d_ys * ge * sg * (1.0 - sg)
            dglb_buf[rs, :] = dgl.astype(BF16)
            keep_buf[rs, :] = d_ys * sg
            return acc + fold(dgl)

        gs_ref[4:5, :] += jnp.sum(chunks(glu_pass_b, zero), axis=0, keepdims=True)
        dglb = dglb_buf[...]
        gwg_ref[...] += lax.dot_general(geb_buf[...], dglb, (((0,), (0,)), ((), ())), preferred_element_type=F32)
        gl_buf[...] = lax.dot_general(dglb, wg, (((1,), (1,)), ((), ())), preferred_element_type=F32)

        def glu_pass_c(r, carry):
            rs = rows(r)
            dy_ref[rs, :] = (keep_buf[rs, :] + gl_buf[rs, :]) * _dgelu(y_ref[rs, :], th_buf[rs, :])
            return carry

        chunks(glu_pass_c, 0)

    col = lambda j: pl.BlockSpec((tm, dc), lambda i, j=j: (nt - 1 - i, j))
    prev = lambda j: pl.BlockSpec((halo, dc), lambda i, j=j: (jnp.maximum((nt - 1 - i) * blocks_per_tile - 1, 0), j))
    rev = lambda i: (nt - 1 - i, 0)
    fixed = lambda i: (0, 0)
    return _pallas(
        body, name="gate_bwd", grid=(nt,),
        in_specs=[col(0), col(1), col(2), col(3), col(5), prev(1), prev(2), pl.BlockSpec((tm, dssm), rev),
                  pl.BlockSpec((tm, dc + dssm), rev), pl.BlockSpec(conv_w.shape, fixed), pl.BlockSpec((1, dc), fixed),
                  pl.BlockSpec(w_glu.shape, fixed), pl.BlockSpec((1, dssm), fixed)],
        out_specs=[pl.BlockSpec((tm, 6 * dc), rev), pl.BlockSpec((tm, dssm), rev),
                   pl.BlockSpec((2 * SUBLANES, dc), fixed), pl.BlockSpec((dssm, dssm), fixed)],
        out_shape=[jax.ShapeDtypeStruct((t, 6 * dc), BF16), jax.ShapeDtypeStruct((t, dssm), F32),
                   jax.ShapeDtypeStruct((2 * SUBLANES, dc), F32), jax.ShapeDtypeStruct((dssm, dssm), F32)],
        scratch_shapes=[pltpu.VMEM((tm + halo, dc), F32), pltpu.VMEM((tm + halo, dc), F32), pltpu.VMEM((tm, dc), F32),
                        pltpu.VMEM((tm, dc), F32), pltpu.VMEM((tm, dssm), BF16), pltpu.VMEM((tm, dssm), BF16),
                        pltpu.VMEM((tm, dssm), F32), pltpu.VMEM((tm, dssm), F32)],
        compiler_params=_cp(2 * (6 * 4 * tm * dc + 8 * tm * dc + 12 * tm * dc + 4 * tm * dc + 6 * dssm * dssm)
                            + 28 * tm * dc + 8 * MIB, dimension_semantics=("arbitrary",)),
    )(proj, proj, proj, proj, proj, proj, proj, y, dmix, conv_w, conv_b, w_glu, b_glu)


def _allgather_flat(v, name):
    r, c = v.shape
    rels = [(dx, dy, dc) for dx in (0, 1) for dy in (0, 1) for dc in (0, 1)][1:]

    def body(v_ref, out_ref, send, recv):
        x, y, cc = _my_place()
        me = 4 * x + 2 * y + cc

        def peer(rel):
            dx, dy, dc = rel
            return (1 - x if dx else x, 1 - y if dy else y, 1 - cc if dc else cc)

        def rc(s, slot, to):
            return pltpu.make_async_remote_copy(src_ref=v_ref, dst_ref=out_ref.at[slot], send_sem=send.at[s],
                                                recv_sem=recv.at[s], device_id=to, device_id_type=MESH)

        sent = []
        for s, rel in enumerate(rels):
            cp = rc(s, me, peer(rel))
            cp.start()
            sent.append(cp)
        out_ref[me] = v_ref[...]
        for s, rel in enumerate(rels):
            px, py, pc = peer(rel)
            rc(s, 4 * px + 2 * py + pc, (px, py, pc)).wait_recv()
        for cp in sent:
            cp.wait_send()

    return _pallas(
        body, name=name, out_shape=jax.ShapeDtypeStruct((N_DEV, r, c), F32),
        in_specs=[pl.BlockSpec(memory_space=pltpu.VMEM)], out_specs=pl.BlockSpec(memory_space=pltpu.VMEM),
        scratch_shapes=[pltpu.SemaphoreType.DMA((N_DEV - 1,)), pltpu.SemaphoreType.DMA((N_DEV - 1,))],
        compiler_params=_cp((N_DEV + 2) * 4 * r * c + 8 * MIB),
    )(v)


def _rs_pair_exchange(grads, name):
    n = len(grads)

    def body(*refs):
        ins, outs = refs[:n], refs[n:2 * n]
        send, recv = refs[2 * n:]
        x, y, c = _my_place()
        cps = []
        for t in range(n):
            cp = pltpu.make_async_remote_copy(src_ref=ins[t].at[:, 1 - c], dst_ref=outs[t], send_sem=send.at[t],
                                              recv_sem=recv.at[t], device_id=(x, y, 1 - c), device_id_type=MESH)
            cp.start()
            cps.append(cp)
        for cp in cps:
            cp.wait()

    any_spec = pl.BlockSpec(memory_space=pl.ANY)
    return _pallas(
        body, name=name, in_specs=[any_spec] * n, out_specs=[any_spec] * n,
        out_shape=[jax.ShapeDtypeStruct((g.shape[0],) + g.shape[2:], g.dtype) for g in grads],
        scratch_shapes=[pltpu.SemaphoreType.DMA((n,)), pltpu.SemaphoreType.DMA((n,))],
        compiler_params=_cp(16 * MIB),
    )(*grads)


def _rs_pair_add(place, grad, got, name):
    nk, _, r2, c = grad.shape
    tr = min(TR_ELT, r2)

    def body(place_ref, g_ref, r_ref, o16_ref, o32_ref):
        del place_ref
        s = g_ref[...] + r_ref[...].astype(F32)
        o32_ref[...] = s
        o16_ref[...] = s.astype(BF16)

    blk = pl.BlockSpec((None, tr, c), lambda k, i, p: (k, i, 0))
    return _pallas(
        body, name=name,
        grid_spec=pltpu.PrefetchScalarGridSpec(
            num_scalar_prefetch=1, grid=(nk, r2 // tr),
            in_specs=[pl.BlockSpec((None, None, tr, c), lambda k, i, p: (k, p[0], i, 0)), blk],
            out_specs=[blk, blk]),
        out_shape=[jax.ShapeDtypeStruct((nk, r2, c), BF16), jax.ShapeDtypeStruct((nk, r2, c), F32)],
        compiler_params=_cp(2 * 14 * tr * c + 8 * MIB),
    )(place, grad, got)


_HBM_SPEC = pl.BlockSpec(memory_space=pltpu.HBM)
_SEM_SPEC = pl.BlockSpec(memory_space=pltpu.SEMAPHORE)
_DATAFLOW = pltpu.SideEffectType.DATAFLOW_SIDE_EFFECTING


def _split_copy_start(srcs, land_shapes, plan, n_sems, name, after=()):
    ns, nl, na = len(srcs), len(land_shapes), len(after)

    def body(*refs):
        src_refs, land_refs = refs[:ns], refs[ns:ns + nl]
        send, recv = refs[ns + nl + na], refs[ns + nl + na + 1]
        token = refs[-1]
        sends, _ = plan(src_refs, land_refs)
        for src, dst, to, si, ri in sends:
            pltpu.make_async_remote_copy(src_ref=src, dst_ref=dst, send_sem=send.at[si], recv_sem=recv.at[ri],
                                         device_id=to, device_id_type=MESH).start()
        token[...] = jnp.zeros_like(token)

    lands = [lax.empty(shp, dt) for shp, dt in land_shapes]
    through = [pltpu.HBM(a.shape, a.dtype) for a in srcs] + [pltpu.HBM(shp, dt) for shp, dt in land_shapes]
    out = pl.pallas_call(
        body, name=name,
        out_shape=(pltpu.SemaphoreType.DMA((n_sems,)), pltpu.SemaphoreType.DMA((n_sems,)), *through,
                   jax.ShapeDtypeStruct((SUBLANES, SLAB), F32)),
        in_specs=[_HBM_SPEC] * (ns + nl) + [pl.BlockSpec(memory_space=pl.ANY)] * na,
        out_specs=(_SEM_SPEC, _SEM_SPEC, *([_HBM_SPEC] * (ns + nl)), pl.BlockSpec(memory_space=pltpu.VMEM)),
        input_output_aliases={i: 2 + i for i in range(ns + nl)},
        compiler_params=pltpu.CompilerParams(has_side_effects=_DATAFLOW),
    )(*[pltpu.with_memory_space_constraint(a, pltpu.HBM) for a in (*srcs, *lands)], *after)
    return out[0], out[1], list(out[2:2 + ns]), list(out[2 + ns:2 + ns + nl]), out[-1]


def _split_copy_wait(send, recv, srcs, lands, plan, after, name):
    ns, nl, na = len(srcs), len(lands), len(after)

    def body(*refs):
        src_refs, land_refs = refs[:ns], refs[ns:ns + nl]
        send_ref, recv_ref = refs[ns + nl], refs[ns + nl + 1]
        sends, arrivals = plan(src_refs, land_refs)
        for src, dst, to, si, ri in sends:
            pltpu.make_async_remote_copy(src_ref=src, dst_ref=dst, send_sem=send_ref.at[si], recv_sem=recv_ref.at[ri],
                                         device_id=to, device_id_type=MESH).wait_send()
        for (src, _, to, si, _), (view, ri) in zip(sends, arrivals):
            pltpu.make_async_remote_copy(src_ref=view, dst_ref=view, send_sem=send_ref.at[si], recv_sem=recv_ref.at[ri],
                                         device_id=to, device_id_type=MESH).wait_recv()

    out = pl.pallas_call(
        body, name=name,
        out_shape=[pltpu.HBM(a.shape, a.dtype) for a in (*srcs, *lands)],
        in_specs=[_HBM_SPEC] * (ns + nl) + [_SEM_SPEC, _SEM_SPEC] + [pl.BlockSpec(memory_space=pl.ANY)] * na,
        out_specs=[_HBM_SPEC] * (ns + nl),
        input_output_aliases={i: i for i in range(ns + nl)},
        compiler_params=pltpu.CompilerParams(has_side_effects=_DATAFLOW),
    )(*srcs, *lands, send, recv, *after)
    return list(out[:ns]), list(out[ns:])


def _chip_exchange_plan(n):
    per = N_CHIPS - 1

    def plan(srcs, lands):
        x, y, c = _my_place()
        sends, arrivals = [], []
        for t in range(n):
            for j, (cx, cy) in enumerate(_other_chips(x, y)):
                sends.append((srcs[t].at[2 * cx + cy], lands[t].at[j], (cx, cy, c), t * per + j, t * per + j))
                arrivals.append((lands[t].at[j], t * per + j))
        return sends, arrivals

    return plan


def _gather_half_plan(n):
    per = N_CHIPS - 1

    def plan(srcs, lands):
        x, y, c = _my_place()
        k = 2 * x + y
        sends, arrivals = [], []
        for t in range(n):
            for j, (cx, cy) in enumerate(_other_chips(x, y)):
                sends.append((srcs[t].at[c], lands[t].at[k, c], (cx, cy, c), t * per + j, t * per + j))
                arrivals.append((lands[t].at[2 * cx + cy, c], t * per + j))
        return sends, arrivals

    return plan


def _gather_complete(own, landed, name):
    def body(own_ref, in_ref, out_ref, send, recv, local):
        x, y, c = _my_place()
        sib = (x, y, 1 - c)
        mine = pltpu.make_async_copy(own_ref, out_ref.at[2 * x + y], local)
        mine.start()
        sent = []
        for j, (cx, cy) in enumerate(_other_chips(x, y)):
            half = (2 * cx + cy, c)
            cp = pltpu.make_async_remote_copy(src_ref=in_ref.at[half], dst_ref=out_ref.at[half], send_sem=send.at[j],
                                              recv_sem=recv.at[j], device_id=sib, device_id_type=MESH)
            cp.start()
            sent.append(cp)
        for j, (cx, cy) in enumerate(_other_chips(x, y)):
            other = out_ref.at[2 * cx + cy, 1 - c]
            pltpu.make_async_remote_copy(src_ref=other, dst_ref=other, send_sem=send.at[j], recv_sem=recv.at[j],
                                         device_id=sib, device_id_type=MESH).wait_recv()
        for cp in sent:
            cp.wait_send()
        mine.wait()

    any_spec = pl.BlockSpec(memory_space=pl.ANY)
    per = N_CHIPS - 1
    return _pallas(
        body, name=name, in_specs=[any_spec, any_spec], out_specs=any_spec,
        out_shape=jax.ShapeDtypeStruct(landed.shape, landed.dtype), input_output_aliases={1: 0},
        scratch_shapes=[pltpu.SemaphoreType.DMA((per,)), pltpu.SemaphoreType.DMA((per,)), pltpu.SemaphoreType.DMA],
        compiler_params=_cp(16 * MIB),
    )(own, landed)


def _gather_direct_plan(n):
    per = 2 * (N_CHIPS - 1)

    def plan(srcs, lands):
        x, y, c = _my_place()
        k = 2 * x + y
        sends, arrivals = [], []
        for t in range(n):
            for j, (cx, cy) in enumerate(_other_chips(x, y)):
                for core in (0, 1):
                    sends.append((srcs[t].at[c], lands[t].at[k, c], (cx, cy, core),
                                  t * per + 2 * j + core, t * per + 2 * j + c))
                    arrivals.append((lands[t].at[2 * cx + cy, core], t * per + 2 * j + core))
        return sends, arrivals

    return plan


def _rs_chip_add(place, sums32, got, name):
    _, r2, c = sums32.shape
    tr = min(TR_ELT, r2)

    def body(place_ref, s_ref, q_ref, o_ref):
        del place_ref
        o_ref[...] = ((s_ref[...] + q_ref[0].astype(F32)) + q_ref[1].astype(F32)) + q_ref[2].astype(F32)

    return _pallas(
        body, name=name,
        grid_spec=pltpu.PrefetchScalarGridSpec(
            num_scalar_prefetch=1, grid=(r2 // tr,),
            in_specs=[pl.BlockSpec((None, tr, c), lambda i, p: (p[1], i, 0)),
                      pl.BlockSpec((N_CHIPS - 1, tr, c), lambda i, p: (0, i, 0))],
            out_specs=pl.BlockSpec((tr, c), lambda i, p: (i, 0))),
        out_shape=jax.ShapeDtypeStruct((r2, c), F32),
        compiler_params=_cp(2 * 14 * tr * c + 8 * MIB),
    )(place, sums32, got)


def _rs_pair_share(halves, name):
    n = len(halves)

    def body(*refs):
        ins, outs = refs[:n], refs[n:2 * n]
        send, recv, local = refs[2 * n:]
        x, y, c = _my_place()
        cps, mine = [], []
        for t in range(n):
            lc = pltpu.make_async_copy(ins[t], outs[t].at[c], local.at[t])
            lc.start()
            mine.append(lc)
            cp = pltpu.make_async_remote_copy(src_ref=ins[t], dst_ref=outs[t].at[c], send_sem=send.at[t],
                                              recv_sem=recv.at[t], device_id=(x, y, 1 - c), device_id_type=MESH)
            cp.start()
            cps.append(cp)
        for t in range(n):
            other = outs[t].at[1 - c]
            pltpu.make_async_remote_copy(src_ref=other, dst_ref=other, send_sem=send.at[t], recv_sem=recv.at[t],
                                         device_id=(x, y, 1 - c), device_id_type=MESH).wait_recv()
        for cp in cps:
            cp.wait_send()
        for lc in mine:
            lc.wait()

    any_spec = pl.BlockSpec(memory_space=pl.ANY)
    return _pallas(
        body, name=name, in_specs=[any_spec] * n, out_specs=[any_spec] * n,
        out_shape=[jax.ShapeDtypeStruct((2,) + h.shape, F32) for h in halves],
        scratch_shapes=[pltpu.SemaphoreType.DMA((n,)), pltpu.SemaphoreType.DMA((n,)), pltpu.SemaphoreType.DMA((n,))],
        compiler_params=_cp(16 * MIB),
    )(*halves)


_PACK_TILE = SUBLANES * SLAB


def _pack(arrays):
    rows = []
    for a in arrays:
        flat = a.reshape(-1).astype(F32)
        padded = -(-flat.shape[0] // _PACK_TILE) * _PACK_TILE
        rows.append(jnp.pad(flat, (0, padded - flat.shape[0])).reshape(-1, SLAB))
    n_rows = sum(r.shape[0] for r in rows)
    if n_rows % (2 * SUBLANES):
        rows.append(jnp.zeros((SUBLANES, SLAB), F32))
    return jnp.concatenate(rows, axis=0)


def _unpack(packed, shapes):
    out, row = [], 0
    for shp in shapes:
        size = math.prod(shp)
        nrow = -(-size // _PACK_TILE) * SUBLANES
        out.append(packed[row:row + nrow].reshape(-1)[:size].reshape(shp))
        row += nrow
    return out


def kernel(x, norm_pre_g, w_in, conv_w, conv_b, ssm_a_re, ssm_a_im, ssm_log_dt, ssm_b_re, ssm_b_im, ssm_c_re, ssm_c_im, ssm_d, w_glu, b_glu, w_out, norm_post_g, loss_target, m_norm_pre_g, m_w_in, m_conv_w, m_conv_b, m_ssm_a_re, m_ssm_a_im, m_ssm_log_dt, m_ssm_b_re, m_ssm_b_im, m_ssm_c_re, m_ssm_c_im, m_ssm_d, m_w_glu, m_b_glu, m_w_out, m_norm_post_g, v_norm_pre_g, v_w_in, v_conv_w, v_conv_b, v_ssm_a_re, v_ssm_a_im, v_ssm_log_dt, v_ssm_b_re, v_ssm_b_im, v_ssm_c_re, v_ssm_c_im, v_ssm_d, v_w_glu, v_b_glu, v_w_out, v_norm_post_g):
    xs, tgt = x[0], loss_target[0]
    t, d = xs.shape
    dc = conv_b.shape[0]
    dssm = ssm_d.shape[0]
    g, p = ssm_a_re.shape
    h = SSM_H
    nq = dssm // SLAB
    n_shard = w_in.shape[1]
    steps = min(T_SCAN, t) // SUBLANES
    mx, my, mc = _my_place()
    chip = 2 * mx + my
    place = jnp.stack([mc, chip]).astype(jnp.int32)

    cw_cols = conv_w.shape[1]
    cw_pad = -(-cw_cols // SLAB) * SLAB
    cw_blk = jnp.zeros((SUBLANES, cw_pad), F32).at[:conv_w.shape[0], :cw_cols].set(conv_w)
    cw_all = _allgather_flat(cw_blk, "allgather_conv_w")
    conv_w_full = jnp.concatenate([cw_all[2 * k, :, :cw_cols] for k in range(N_CHIPS)], axis=1)

    halves = lambda a: a.reshape(2, a.shape[0] // 2, a.shape[1])
    win_half = halves(_cast_bf16(w_in, "cast_w_in", cw_all[0, :, :SLAB]))
    win_plan = _gather_half_plan(1)
    win_send, win_recv, win_srcs, win_lands, win_token = _split_copy_start(
        [win_half], [((N_CHIPS,) + win_half.shape, BF16)], win_plan, N_CHIPS - 1, "gather_w_in_start")
    behind_w_in = win_token[0:1, 0:1]

    small_names = ["norm_pre_g", "conv_w", "conv_b", "ssm_a_re", "ssm_a_im", "ssm_log_dt", "ssm_b_re", "ssm_b_im",
                   "ssm_c_re", "ssm_c_im", "ssm_d", "b_glu", "norm_post_g", "loss"]
    zeros_cw = jnp.zeros((conv_w.shape[0], dc), F32)
    one0 = jnp.zeros((1,), F32)
    small_w = dict(norm_pre_g=norm_pre_g, conv_w=zeros_cw, conv_b=conv_b, ssm_a_re=ssm_a_re, ssm_a_im=ssm_a_im,
                   ssm_log_dt=ssm_log_dt, ssm_b_re=ssm_b_re, ssm_b_im=ssm_b_im, ssm_c_re=ssm_c_re, ssm_c_im=ssm_c_im,
                   ssm_d=ssm_d, b_glu=b_glu, norm_post_g=norm_post_g, loss=one0)
    small_m = dict(norm_pre_g=m_norm_pre_g, conv_w=zeros_cw, conv_b=m_conv_b, ssm_a_re=m_ssm_a_re, ssm_a_im=m_ssm_a_im,
                   ssm_log_dt=m_ssm_log_dt, ssm_b_re=m_ssm_b_re, ssm_b_im=m_ssm_b_im, ssm_c_re=m_ssm_c_re,
                   ssm_c_im=m_ssm_c_im, ssm_d=m_ssm_d, b_glu=m_b_glu, norm_post_g=m_norm_post_g, loss=one0)
    small_v = dict(norm_pre_g=v_norm_pre_g, conv_w=zeros_cw, conv_b=v_conv_b, ssm_a_re=v_ssm_a_re, ssm_a_im=v_ssm_a_im,
                   ssm_log_dt=v_ssm_log_dt, ssm_b_re=v_ssm_b_re, ssm_b_im=v_ssm_b_im, ssm_c_re=v_ssm_c_re,
                   ssm_c_im=v_ssm_c_im, ssm_d=v_ssm_d, b_glu=v_b_glu, norm_post_g=v_norm_post_g, loss=one0)
    w_pack, m_pack, v_pack = [_pack([group[nm] for nm in small_names]) + behind_w_in
                              for group in (small_w, small_m, small_v)]
    side = [halves(_cast_bf16(w_out, "cast_w_out", win_token)), halves(_cast_bf16(w_glu, "cast_w_glu", win_token))]

    expand = jnp.repeat(jnp.eye(p, dtype=F32), h, axis=1)
    b_re2, b_im2 = ssm_b_re.reshape(g, p * h), ssm_b_im.reshape(g, p * h)
    log_dt2 = ssm_log_dt.reshape(g, 1) + behind_w_in
    lbr, lbi, bbr2, bbi2, pw3 = _zoh_fwd(ssm_a_re, ssm_a_im, log_dt2, b_re2, b_im2, expand, steps)
    lam = jnp.stack([lbr.reshape(g * p), lbi.reshape(g * p)])
    pw = pw3.reshape(2, g * p)
    to_slab_b = lambda b2: _blockdiag(b2.reshape(nq, GROUPS_PER_SLAB, p, h).transpose(0, 1, 3, 2))
    bq = jnp.concatenate([to_slab_b(bbr2), to_slab_b(bbi2)], axis=2).astype(BF16)
    to_slab_c = lambda c3: _blockdiag(c3.reshape(nq, GROUPS_PER_SLAB, h, p).transpose(0, 1, 3, 2))
    cq = jnp.concatenate([to_slab_c(ssm_c_re), to_slab_c(-ssm_c_im)], axis=1).astype(BF16)

    g_pre2, g_post2 = norm_pre_g.reshape(1, d), norm_post_g.reshape(1, d)
    conv_b2, b_glu2, d_skip2 = conv_b.reshape(1, dc), b_glu.reshape(1, dssm), ssm_d.reshape(1, dssm)
    u_block = 4 * dc // dssm

    hb = _prenorm(xs, g_pre2 + behind_w_in)
    proj_own = _inproj_own(place, hb, win_srcs[0].reshape(d, n_shard))
    win_own, win_landed = _split_copy_wait(win_send, win_recv, win_srcs, win_lands, win_plan,
                                           [proj_own, bq, cq, w_pack, m_pack, v_pack, *side], "gather_w_in_wait")
    win_g = _gather_complete(win_own[0], win_landed[0], "gather_w_in_complete")
    win_b = win_g.reshape(N_CHIPS, d, n_shard)
    side_plan = _gather_direct_plan(len(side))
    side_sems = 2 * (N_CHIPS - 1) * len(side)
    ag_send, ag_recv, ag_srcs, ag_lands, ag_token = _split_copy_start(
        side, [((N_CHIPS,) + a.shape, BF16) for a in side], side_plan, side_sems, "gather_side_weights_start",
        after=[win_g])
    proj = _inproj_rest(place, hb, win_b, proj_own, ag_token)
    y, cin = _ssm_fwd(proj, u_block, bq, cq, lam, pw, d_skip2)
    side_own, side_all = _split_copy_wait(ag_send, ag_recv, ag_srcs, ag_lands, side_plan, [cin],
                                          "gather_side_weights_wait")
    wout_g, wglu_g = [lax.dynamic_update_index_in_dim(all_, own, chip, 0) for own, all_ in zip(side_own, side_all)]
    wout_b = wout_g.reshape(dc + dssm, d)
    wglu_b = wglu_g.reshape(dssm, dssm)
    mix = _gate_fwd(proj, y, conv_w_full, conv_b2, wglu_b, b_glu2, dc)
    loss_blk, dout, dob, dmix, gg_post = _outproj(mix, wout_b, xs, tgt, g_post2)

    def reduce_start(grads, to_send, tags, group):
        got = _rs_pair_exchange(to_send, "rs_pair_exchange_" + group)
        sums = [_rs_pair_add(place, gt, rt, "rs_pair_add_" + tg) for gt, rt, tg in zip(grads, got, tags)]
        plan = _chip_exchange_plan(len(grads))
        started = _split_copy_start([s16 for s16, _ in sums],
                                    [((N_CHIPS - 1,) + s16.shape[1:], BF16) for s16, _ in sums], plan,
                                    (N_CHIPS - 1) * len(grads), "rs_chip_exchange_" + group + "_start")
        return plan, started, [s32 for _, s32 in sums]

    def reduce_finish(plan, started, sums32, tags, group, after):
        send, recv, srcs, lands, _ = started
        _, landed = _split_copy_wait(send, recv, srcs, lands, plan, after, "rs_chip_exchange_" + group + "_wait")
        mine = [_rs_chip_add(place, s32, qt, "rs_chip_add_" + tg) for s32, qt, tg in zip(sums32, landed, tags)]
        full = _rs_pair_share(mine, "rs_pair_share_" + group)
        return [f.reshape(2 * f.shape[1], f.shape[2]) for f in full]

    gw_out, gw_out16 = _matmul_tn(mix, dob, 1, "grad_w_out")
    dproj, dy, gsmall, gw_glu = _gate_bwd(proj, y, dmix, conv_w_full, conv_b2, wglu_b, b_glu2, dc)
    as_out = lambda a: a.reshape(N_CHIPS, 2, (dc + dssm) // (2 * N_CHIPS), d)
    gw_glu4 = gw_glu.reshape(N_CHIPS, 2, dssm // (2 * N_CHIPS), dssm)
    rs_a = reduce_start([as_out(gw_out), gw_glu4], [as_out(gw_out16), gw_glu4], ["w_out", "w_glu"], "a")
    dproj, gb_dense, gc_dense, glam, gd = _ssm_bwd(proj, u_block, dy, cin, dproj, bq, cq, lam, pw,
                                                   d_skip2 + rs_a[1][4][0:1, 0:1])
    gw_in, gw_in16 = _matmul_tn(hb, dproj, N_CHIPS, "grad_w_in")
    as_in = lambda a: a.reshape(N_CHIPS, 2, d // 2, n_shard)
    rs_b = reduce_start([as_in(gw_in)], [as_in(gw_in16)], ["w_in"], "b")
    gx, gg_pre = _dh_prenorm_bwd(dproj, win_b, xs, dout, g_pre2 + rs_b[1][4][0:1, 0:1])

    gb4 = gb_dense.reshape(nq, SLAB, 2, SLAB_STATES)
    g_bbr2 = _blockdiag_take(gb4[:, :, 0, :], h, p).transpose(0, 1, 3, 2).reshape(g, p * h)
    g_bbi2 = _blockdiag_take(gb4[:, :, 1, :], h, p).transpose(0, 1, 3, 2).reshape(g, p * h)
    gc4 = gc_dense.reshape(nq, SLAB, 2, SLAB_STATES)
    g_c_re = _blockdiag_take(gc4[:, :, 0, :], h, p).reshape(g, h, p)
    g_c_im = -_blockdiag_take(gc4[:, :, 1, :], h, p).reshape(g, h, p)
    g_a_re, g_a_im, g_ld, g_b_re2, g_b_im2 = _zoh_bwd(
        ssm_a_re, ssm_a_im, log_dt2, b_re2, b_im2, expand,
        glam[0].reshape(g, p), glam[1].reshape(g, p), g_bbr2, g_bbi2)

    small_g = {
        "norm_pre_g": gg_pre[0], "conv_w": gsmall[0:3], "conv_b": gsmall[3], "ssm_a_re": g_a_re, "ssm_a_im": g_a_im,
        "ssm_log_dt": g_ld.reshape(g), "ssm_b_re": g_b_re2.reshape(g, p, h), "ssm_b_im": g_b_im2.reshape(g, p, h),
        "ssm_c_re": g_c_re, "ssm_c_im": g_c_im, "ssm_d": gd[0], "b_glu": gsmall[4], "norm_post_g": gg_post[0],
        "loss": loss_blk[0, 0:1],
    }
    shapes = [small_w[nm].shape for nm in small_names]
    g_pack = _allreduce_small(_pack([small_g[nm] for nm in small_names]))
    packs = _adamw_small(g_pack, w_pack, m_pack, v_pack)
    sg, sd, sm, sv = [dict(zip(small_names, _unpack(pk, shapes))) for pk in (g_pack, *packs)]
    loss = sg["loss"][0]

    g_cw = lax.dynamic_slice_in_dim(sg["conv_w"], chip * cw_cols, cw_cols, axis=1)
    pad_cw = lambda a: jnp.zeros((SUBLANES, cw_pad), F32).at[:a.shape[0], :cw_cols].set(a)
    cut_cw = lambda a: a[:conv_w.shape[0], :cw_cols]
    d_cw, m_cw, v_cw = [cut_cw(a) for a in _adamw(pad_cw(conv_w), pad_cw(g_cw), pad_cw(m_conv_w), pad_cw(v_conv_w),
                                                  "adamw_conv_w")]

    g_wout, g_wglu = reduce_finish(*rs_a, ["w_out", "w_glu"], "a", [d_cw, packs[0]])
    (g_win,) = reduce_finish(*rs_b, ["w_in"], "b", [g_wout])
    d_win, m_win, v_win = _adamw(w_in, g_win, m_w_in, v_w_in, "adamw_w_in")
    d_wout, m_wout, v_wout = _adamw(w_out, g_wout, m_w_out, v_w_out, "adamw_w_out")
    d_wglu, m_wglu, v_wglu = _adamw(w_glu, g_wglu, m_w_glu, v_w_glu, "adamw_w_glu")

    order = ["norm_pre_g", "w_in", "conv_w", "conv_b", "ssm_a_re", "ssm_a_im", "ssm_log_dt", "ssm_b_re", "ssm_b_im",
             "ssm_c_re", "ssm_c_im", "ssm_d", "w_glu", "b_glu", "w_out", "norm_post_g"]
    grads, deltas, new_m, new_v = dict(sg), dict(sd), dict(sm), dict(sv)
    grads.update(w_in=g_win, w_out=g_wout, w_glu=g_wglu, conv_w=g_cw)
    deltas.update(w_in=d_win, w_out=d_wout, w_glu=d_wglu, conv_w=d_cw)
    new_m.update(w_in=m_win, w_out=m_wout, w_glu=m_wglu, conv_w=m_cw)
    new_v.update(w_in=v_win, w_out=v_wout, w_glu=v_wglu, conv_w=v_cw)
    return (loss, gx[None], *[grads[nm] for nm in order], *[deltas[nm] for nm in order],
            *[new_m[nm] for nm in order], *[new_v[nm] for nm in order])
```

```python
import math

import jax
import jax.numpy as jnp
from jax import lax
from jax.experimental import pallas as pl
from jax.experimental.pallas import tpu as pltpu

F32 = jnp.float32
BF16 = jnp.bfloat16
MESH = pl.DeviceIdType.MESH

EPS = 1e-6
SSM_H = 16
SSM_P = 64
GROUPS_PER_SLAB = 8
SLAB = 128
SLAB_STATES = GROUPS_PER_SLAB * SSM_P
N_CHIPS = 4
N_DEV = 8

ADAM_LR = 0.001
ADAM_B1 = 0.9
ADAM_B2 = 0.999
ADAM_EPS = 1e-08
ADAM_WD = 0.01
ADAM_STEP = 10

MIB = 1024 * 1024
VMEM_CAP = 48 * MIB
SUBLANES = 8

TM_NORM = 512
TM_PROJ = 512
TM_GATE = 256
TM_OUT = 256
TM_DH = 256
T_SCAN = 256
TK_TN = 1024
TM_TN = 1024
TR_ELT = 256


def _cp(vmem_bytes, **kw):
    return pltpu.CompilerParams(vmem_limit_bytes=int(min(VMEM_CAP, max(16 * MIB, vmem_bytes))), **kw)


def _pallas(body, **kw):
    if "grid" not in kw and "grid_spec" not in kw:
        return pl.pallas_call(body, **kw)
    pin = lambda s: pltpu.HBM(s.shape, s.dtype) if isinstance(s, jax.ShapeDtypeStruct) else s
    out_shape = kw.pop("out_shape")
    out_shape = [pin(s) for s in out_shape] if isinstance(out_shape, (list, tuple)) else pin(out_shape)
    call = pl.pallas_call(body, out_shape=out_shape, **kw)

    def run(*args):
        return call(*[pltpu.with_memory_space_constraint(a, pltpu.HBM) if jnp.issubdtype(a.dtype, jnp.floating) else a
                      for a in args])

    return run


def _my_place():
    return lax.axis_index("x"), lax.axis_index("y"), lax.axis_index("c")


def _other_chips(x, y):
    return [(1 - x, y), (x, 1 - y), (1 - x, 1 - y)]


def _silu(z):
    s = jax.nn.sigmoid(z)
    return z * s, s


def _dsilu(z, s):
    return s * (1.0 + z * (1.0 - s))


_GELU_K = math.sqrt(2.0 / math.pi)
_GELU_C = 0.044715


def _gelu(y):
    th = jnp.tanh(_GELU_K * (y + _GELU_C * y * y * y))
    return 0.5 * y * (1.0 + th), th


def _dgelu(y, th):
    return 0.5 * (1.0 + th) + 0.5 * y * (1.0 - th * th) * _GELU_K * (1.0 + 3.0 * _GELU_C * y * y)


def _cast_bf16(w, name, after=None):
    r, c = w.shape
    tr = min(TR_ELT, r)
    extra = [] if after is None else [after]

    def body(w_ref, *rest):
        rest[-1][...] = w_ref[...].astype(BF16)

    return _pallas(
        body, name=name, grid=(r // tr,),
        in_specs=[pl.BlockSpec((tr, c), lambda i: (i, 0))] + [pl.BlockSpec((SUBLANES, SLAB), lambda i: (0, 0))] * len(extra),
        out_specs=pl.BlockSpec((tr, c), lambda i: (i, 0)),
        out_shape=jax.ShapeDtypeStruct((r, c), BF16),
        compiler_params=_cp(12 * tr * c),
    )(w, *extra)


def _prenorm(x, g):
    t, d = x.shape
    tm = min(TM_NORM, t)

    def body(x_ref, g_ref, h_ref):
        xv = x_ref[...]
        r = lax.rsqrt(jnp.mean(xv * xv, axis=-1, keepdims=True) + EPS)
        h_ref[...] = (xv * r * g_ref[...]).astype(BF16)

    return _pallas(
        body, name="prenorm", grid=(t // tm,),
        in_specs=[pl.BlockSpec((tm, d), lambda i: (i, 0)), pl.BlockSpec((1, d), lambda i: (0, 0))],
        out_specs=pl.BlockSpec((tm, d), lambda i: (i, 0)),
        out_shape=jax.ShapeDtypeStruct((t, d), BF16),
        compiler_params=_cp(20 * tm * d),
    )(x, g)


def _inproj_own(place, a, w_own):
    t, k = a.shape
    n = w_own.shape[1]
    tm = min(TM_PROJ, t)

    def body(place_ref, a_ref, b_ref, o_ref):
        del place_ref
        o_ref[...] = jnp.dot(a_ref[...], b_ref[...], preferred_element_type=F32)

    return _pallas(
        body, name="inproj_own",
        grid_spec=pltpu.PrefetchScalarGridSpec(
            num_scalar_prefetch=1, grid=(t // tm,),
            in_specs=[pl.BlockSpec((tm, k), lambda i, p: (i, 0)), pl.BlockSpec((k, n), lambda i, p: (0, 0))],
            out_specs=pl.BlockSpec((tm, n), lambda i, p: (i, p[1]))),
        out_shape=jax.ShapeDtypeStruct((t, N_CHIPS * n), F32),
        compiler_params=_cp(2 * (2 * tm * k + 2 * k * n + 4 * tm * n) + 4 * MIB),
    )(place, a, w_own)


def _inproj_rest(place, a, b, partial, after):
    t, k = a.shape
    nb, _, n = b.shape
    tm = min(TM_PROJ, t)
    shard = lambda s, p: (p[1] + 1 + s) % nb

    def body(place_ref, a_ref, b_ref, after_ref, partial_ref, o_ref):
        del place_ref, after_ref, partial_ref
        o_ref[...] = jnp.dot(a_ref[...], b_ref[...], preferred_element_type=F32)

    return _pallas(
        body, name="inproj_rest",
        grid_spec=pltpu.PrefetchScalarGridSpec(
            num_scalar_prefetch=1, grid=(nb - 1, t // tm),
            in_specs=[pl.BlockSpec((tm, k), lambda s, i, p: (i, 0)),
                      pl.BlockSpec((None, k, n), lambda s, i, p: (shard(s, p), 0, 0)),
                      pl.BlockSpec((SUBLANES, SLAB), lambda s, i, p: (0, 0)),
                      pl.BlockSpec(memory_space=pl.ANY)],
            out_specs=pl.BlockSpec((tm, n), lambda s, i, p: (i, shard(s, p)))),
        out_shape=jax.ShapeDtypeStruct((t, nb * n), F32),
        input_output_aliases={4: 0},
        compiler_params=_cp(2 * (2 * tm * k + 2 * k * n + 4 * tm * n) + 4 * MIB),
    )(place, a, b, after, partial)


def _matmul_tn(a, b, nb, name):
    t, m = a.shape
    n = b.shape[1] // nb
    tk = min(TK_TN, t)
    tma = min(TM_TN, m)

    def body(a_ref, b_ref, o_ref, o16_ref):
        k = pl.program_id(2)

        @pl.when(k == 0)
        def _():
            o_ref[...] = jnp.zeros_like(o_ref)

        o_ref[...] += lax.dot_general(a_ref[...], b_ref[...], (((0,), (0,)), ((), ())), preferred_element_type=F32)

        @pl.when(k == t // tk - 1)
        def _():
            o16_ref[...] = o_ref[...].astype(BF16)

    blk = pl.BlockSpec((None, tma, n), lambda j, i, k: (j, i, 0))
    return _pallas(
        body, name=name, grid=(nb, m // tma, t // tk),
        in_specs=[pl.BlockSpec((tk, tma), lambda j, i, k: (k, i)), pl.BlockSpec((tk, n), lambda j, i, k: (k, j))],
        out_specs=[blk, blk],
        out_shape=[jax.ShapeDtypeStruct((nb, m, n), F32), jax.ShapeDtypeStruct((nb, m, n), BF16)],
        compiler_params=_cp(2 * (2 * tk * tma + 2 * tk * n + 6 * tma * n) + 8 * MIB),
    )(a, b)


def _outproj(mix, w_out, x, tgt, g_post):
    t, dm = mix.shape
    d = w_out.shape[1]
    tm = min(TM_OUT, t)

    rb = 2 * SUBLANES
    assert tm % rb == 0

    def body(mix_ref, w_ref, x_ref, t_ref, g_ref, loss_ref, dout_ref, do_ref, dmix_ref, gg_ref, o_buf):
        @pl.when(pl.program_id(0) == 0)
        def _():
            loss_ref[...] = jnp.zeros_like(loss_ref)
            gg_ref[...] = jnp.zeros_like(gg_ref)

        w = w_ref[...]
        o_buf[...] = jnp.dot(mix_ref[...], w, preferred_element_type=F32)
        g = g_ref[...]
        fold = lambda v: v[0:SUBLANES] + v[SUBLANES:rb]
        sq_err = jnp.zeros((SUBLANES, d), F32)
        gg = jnp.zeros((SUBLANES, d), F32)
        for c in range(tm // rb):
            rs = pl.ds(c * rb, rb)
            o = o_buf[rs, :]
            r = lax.rsqrt(jnp.mean(o * o, axis=-1, keepdims=True) + EPS)
            nh = o * r
            e = x_ref[rs, :] + nh * g - t_ref[rs, :]
            sq_err = sq_err + fold(e * e)
            dout = e * (1.0 / d)
            dout_ref[rs, :] = dout
            gg = gg + fold(dout * nh)
            dn = dout * g
            do = r * (dn - nh * jnp.mean(dn * nh, axis=-1, keepdims=True))
            do_ref[rs, :] = do.astype(BF16)
        loss_ref[...] += jnp.sum(sq_err) * (0.5 / d)
        gg_ref[0:1, :] += jnp.sum(gg, axis=0, keepdims=True)
        dmix_ref[...] = lax.dot_general(do_ref[...], w, (((1,), (1,)), ((), ())), preferred_element_type=F32)

    row = lambda i: (i, 0)
    fixed = lambda i: (0, 0)
    return _pallas(
        body, name="outproj", grid=(t // tm,),
        in_specs=[pl.BlockSpec((tm, dm), row), pl.BlockSpec((dm, d), fixed), pl.BlockSpec((tm, d), row),
                  pl.BlockSpec((tm, d), row), pl.BlockSpec((1, d), fixed)],
        out_specs=[pl.BlockSpec((SUBLANES, SLAB), fixed), pl.BlockSpec((tm, d), row), pl.BlockSpec((tm, d), row),
                   pl.BlockSpec((tm, dm), row), pl.BlockSpec((SUBLANES, d), fixed)],
        out_shape=[jax.ShapeDtypeStruct((SUBLANES, SLAB), F32), jax.ShapeDtypeStruct((t, d), F32),
                   jax.ShapeDtypeStruct((t, d), BF16), jax.ShapeDtypeStruct((t, dm), F32),
                   jax.ShapeDtypeStruct((SUBLANES, d), F32)],
        scratch_shapes=[pltpu.VMEM((tm, d), F32)],
        compiler_params=_cp(2 * (2 * dm * d + tm * (2 * dm + 4 * d * 3 + 2 * d + 4 * dm)) + 4 * tm * d + 12 * MIB),
    )(mix, w_out, x, tgt, g_post)


def _dh_prenorm_bwd(dproj, w_in, x, dout, g_pre):
    t, d = x.shape
    nb, _, n = w_in.shape
    tm = min(TM_DH, t)

    def body(dp_ref, w_ref, x_ref, dout_ref, g_ref, dx_ref, gg_ref):
        @pl.when(pl.program_id(0) == 0)
        def _():
            gg_ref[...] = jnp.zeros_like(gg_ref)

        dh = None
        for k in range(nb):
            part = lax.dot_general(dp_ref[:, k * n:(k + 1) * n], w_ref[k], (((1,), (1,)), ((), ())),
                                   preferred_element_type=F32)
            dh = part if dh is None else dh + part
        xv = x_ref[...]
        r = lax.rsqrt(jnp.mean(xv * xv, axis=-1, keepdims=True) + EPS)
        xh = xv * r
        gg_ref[0:1, :] += jnp.sum(dh * xh, axis=0, keepdims=True)
        dg = dh * g_ref[...]
        dx_ref[...] = dout_ref[...] + r * (dg - xh * jnp.mean(dg * xh, axis=-1, keepdims=True))

    row = lambda i: (i, 0)
    fixed = lambda i: (0, 0)
    w_spec = pl.BlockSpec(w_in.shape, lambda i: (0, 0, 0), pipeline_mode=pl.Buffered(1))
    return _pallas(
        body, name="dh_prenorm_bwd", grid=(t // tm,),
        in_specs=[pl.BlockSpec((tm, nb * n), row), w_spec,
                  pl.BlockSpec((tm, d), row), pl.BlockSpec((tm, d), row), pl.BlockSpec((1, d), fixed)],
        out_specs=[pl.BlockSpec((tm, d), row), pl.BlockSpec((SUBLANES, d), fixed)],
        out_shape=[jax.ShapeDtypeStruct((t, d), F32), jax.ShapeDtypeStruct((SUBLANES, d), F32)],
        compiler_params=_cp(2 * nb * d * n + 2 * (2 * tm * nb * n + 12 * tm * d) + 16 * tm * d + 4 * MIB),
    )(dproj, w_in, x, dout, g_pre)


def _adamw(w, g, m, v, name):
    r, c = w.shape
    tr = min(TR_ELT, r)
    c1 = 1.0 - ADAM_B1 ** ADAM_STEP
    c2 = 1.0 - ADAM_B2 ** ADAM_STEP

    def body(w_ref, g_ref, m_ref, v_ref, go_ref, d_ref, mo_ref, vo_ref):
        gv = g_ref[...]
        go_ref[...] = gv
        mn = ADAM_B1 * m_ref[...] + (1.0 - ADAM_B1) * gv
        vn = ADAM_B2 * v_ref[...] + (1.0 - ADAM_B2) * (gv * gv)
        d_ref[...] = -ADAM_LR * ((mn / c1) / (jnp.sqrt(vn / c2) + ADAM_EPS) + ADAM_WD * w_ref[...])
        mo_ref[...] = mn
        vo_ref[...] = vn

    spec = pl.BlockSpec((tr, c), lambda i: (i, 0))
    sds = jax.ShapeDtypeStruct((r, c), F32)
    return _pallas(
        body, name=name, grid=(r // tr,), in_specs=[spec] * 4, out_specs=[spec] * 4, out_shape=[sds] * 4,
        compiler_params=_cp(2 * 8 * 4 * tr * c + 8 * MIB),
    )(w, g, m, v)


def _adamw_small(g, w, m, v):
    r, c = w.shape
    c1 = 1.0 - ADAM_B1 ** ADAM_STEP
    c2 = 1.0 - ADAM_B2 ** ADAM_STEP

    def body(g_ref, w_ref, m_ref, v_ref, d_ref, mo_ref, vo_ref):
        gv = g_ref[...]
        mn = ADAM_B1 * m_ref[...] + (1.0 - ADAM_B1) * gv
        vn = ADAM_B2 * v_ref[...] + (1.0 - ADAM_B2) * (gv * gv)
        d_ref[...] = -ADAM_LR * ((mn / c1) / (jnp.sqrt(vn / c2) + ADAM_EPS) + ADAM_WD * w_ref[...])
        mo_ref[...] = mn
        vo_ref[...] = vn

    sds = jax.ShapeDtypeStruct((r, c), F32)
    return _pallas(
        body, name="adamw_small", out_shape=[sds] * 3,
        compiler_params=_cp(12 * 4 * r * c + 8 * MIB),
    )(g, w, m, v)


def _allreduce_small(v):
    r, c = v.shape
    assert r % (2 * SUBLANES) == 0
    h = r // 2

    def body(v_ref, out_ref, got_ref, chip_ref, slots_ref, send, recv):
        x, y, cc = _my_place()
        k = 2 * x + y
        sib = (x, y, 1 - cc)

        def rc(s, src, dst, to):
            return pltpu.make_async_remote_copy(src_ref=src, dst_ref=dst, send_sem=send.at[s], recv_sem=recv.at[s],
                                                device_id=to, device_id_type=MESH)

        pair = rc(0, v_ref, got_ref, sib)
        pair.start()
        pair.wait()
        chip_ref[...] = v_ref[...] + got_ref[...]
        mine = pl.ds(pl.multiple_of(cc * h, SUBLANES), h)
        theirs = pl.ds(pl.multiple_of((1 - cc) * h, SUBLANES), h)
        chips = _other_chips(x, y)
        sent = []
        for j, (cx, cy) in enumerate(chips):
            cp = rc(1 + j, chip_ref.at[mine], slots_ref.at[k], (cx, cy, cc))
            cp.start()
            sent.append(cp)
        slots_ref[k] = chip_ref[mine, :]
        for j, (cx, cy) in enumerate(chips):
            rc(1 + j, chip_ref.at[mine], slots_ref.at[2 * cx + cy], (cx, cy, cc)).wait_recv()
        total = slots_ref[0]
        for kk in range(1, N_CHIPS):
            total = total + slots_ref[kk]
        out_ref[mine, :] = total
        for cp in sent:
            cp.wait_send()
        share = rc(N_CHIPS, out_ref.at[mine], out_ref.at[mine], sib)
        share.start()
        rc(N_CHIPS, out_ref.at[theirs], out_ref.at[theirs], sib).wait_recv()
        share.wait_send()

    vm = pl.BlockSpec(memory_space=pltpu.VMEM)
    return _pallas(
        body, name="allreduce_small", out_shape=jax.ShapeDtypeStruct((r, c), F32), in_specs=[vm], out_specs=vm,
        scratch_shapes=[pltpu.VMEM((r, c), F32), pltpu.VMEM((r, c), F32), pltpu.VMEM((N_CHIPS, h, c), F32),
                        pltpu.SemaphoreType.DMA((N_CHIPS + 1,)), pltpu.SemaphoreType.DMA((N_CHIPS + 1,))],
        compiler_params=_cp(6 * 4 * r * c + 8 * MIB),
    )(v)


def _zoh(a_re, a_im, log_dt, b_re2, b_im2, expand):
    dt = jnp.exp(log_dt)
    mag = jnp.exp(a_re * dt)
    lbr, lbi = mag * jnp.cos(a_im * dt), mag * jnp.sin(a_im * dt)
    nr, ni = lbr - 1.0, lbi
    den = a_re * a_re + a_im * a_im
    qr = (nr * a_re + ni * a_im) / den
    qi = (ni * a_re - nr * a_im) / den
    qr2 = jnp.dot(qr, expand, precision=lax.Precision.HIGHEST, preferred_element_type=F32)
    qi2 = jnp.dot(qi, expand, precision=lax.Precision.HIGHEST, preferred_element_type=F32)
    return lbr, lbi, qr2 * b_re2 - qi2 * b_im2, qr2 * b_im2 + qi2 * b_re2


def _zoh_fwd(a_re, a_im, log_dt, b_re2, b_im2, expand, power):
    g, p = a_re.shape

    def body(ar_ref, ai_ref, ld_ref, br_ref, bi_ref, e_ref, lbr_ref, lbi_ref, bbr_ref, bbi_ref, pw_ref):
        ar, ai, ld = ar_ref[...], ai_ref[...], ld_ref[...]
        lbr, lbi, bbr, bbi = _zoh(ar, ai, ld, br_ref[...], bi_ref[...], e_ref[...])
        lbr_ref[...], lbi_ref[...], bbr_ref[...], bbi_ref[...] = lbr, lbi, bbr, bbi
        dt = jnp.exp(ld) * float(power)
        mag = jnp.exp(ar * dt)
        pw_ref[0] = mag * jnp.cos(ai * dt)
        pw_ref[1] = mag * jnp.sin(ai * dt)

    gp = jax.ShapeDtypeStruct((g, p), F32)
    gph = jax.ShapeDtypeStruct(b_re2.shape, F32)
    return _pallas(
        body, name="zoh_fwd", out_shape=[gp, gp, gph, gph, jax.ShapeDtypeStruct((2, g, p), F32)],
        compiler_params=_cp(16 * MIB),
    )(a_re, a_im, log_dt, b_re2, b_im2, expand)


def _zoh_bwd(a_re, a_im, log_dt, b_re2, b_im2, expand, g_lbr, g_lbi, g_bbr, g_bbi):
    def body(ar_ref, ai_ref, ld_ref, br_ref, bi_ref, e_ref, c0, c1, c2, c3, gar, gai, gld, gbr, gbi):
        e = e_ref[...]
        _, vjp = jax.vjp(lambda a, b, c, d, f: _zoh(a, b, c, d, f, e),
                         ar_ref[...], ai_ref[...], ld_ref[...], br_ref[...], bi_ref[...])
        gar[...], gai[...], gld[...], gbr[...], gbi[...] = vjp((c0[...], c1[...], c2[...], c3[...]))

    sds = lambda a: jax.ShapeDtypeStruct(a.shape, F32)
    return _pallas(
        body, name="zoh_bwd", out_shape=[sds(a_re), sds(a_im), sds(log_dt), sds(b_re2), sds(b_im2)],
        compiler_params=_cp(16 * MIB),
    )(a_re, a_im, log_dt, b_re2, b_im2, expand, g_lbr, g_lbi, g_bbr, g_bbi)


def _blockdiag(blocks):
    nq, _, r, c = blocks.shape
    eye = jnp.eye(GROUPS_PER_SLAB, dtype=blocks.dtype)
    out = blocks[:, :, :, None, :] * eye[None, :, None, :, None]
    return out.reshape(nq, GROUPS_PER_SLAB * r, GROUPS_PER_SLAB * c)


def _blockdiag_take(dense, r, c):
    nq = dense.shape[0]
    d5 = dense.reshape(nq, GROUPS_PER_SLAB, r, GROUPS_PER_SLAB, c)
    return jnp.stack([d5[:, i, :, i, :] for i in range(GROUPS_PER_SLAB)], axis=1)


def _scan_slab(s_ref, row0, q, lam_ref, pw_ref, car_ref, steps, reverse, prev_ref=None, prev_row0=0, glam_ref=None):
    sign = -1.0 if reverse else 1.0
    half = SLAB_STATES // SLAB
    cols = [(q * 2 * half + m, q * 2 * half + half + m, q * SLAB_STATES + m * SLAB) for m in range(half)]
    nm = len(cols)
    full = (SUBLANES, SLAB)
    lam = [(jnp.broadcast_to(lam_ref[0:1, pl.ds(cl, SLAB)], full),
            jnp.broadcast_to(sign * lam_ref[1:2, pl.ds(cl, SLAB)], full)) for (_, _, cl) in cols]

    def step_rows(jj, base):
        j = (steps - 1 - jj) if reverse else jj
        return j, pl.ds(base + j * SUBLANES, SUBLANES)

    def pass1(jj, car):
        _, rows = step_rows(jj, row0)
        out = []
        for m, (cr, ci, _) in enumerate(cols):
            sr, si = car[2 * m], car[2 * m + 1]
            lr, li = lam[m]
            nr = lr * sr - li * si + s_ref[cr, rows, :]
            ni = lr * si + li * sr + s_ref[ci, rows, :]
            s_ref[cr, rows, :] = nr
            s_ref[ci, rows, :] = ni
            out += [nr, ni]
        return tuple(out)

    def run_steps(step_fn, carry):
        for jj in range(steps):
            carry = step_fn(jj, carry)
        return carry

    ends = run_steps(pass1, tuple(jnp.zeros(full, F32) for _ in range(2 * nm)))

    entry = []
    for m, (cr, ci, cl) in enumerate(cols):
        ljr = pw_ref[0:1, pl.ds(cl, SLAB)]
        lji = sign * pw_ref[1:2, pl.ds(cl, SLAB)]
        c_r = car_ref[cr, 0:1, :]
        c_i = car_ref[ci, 0:1, :]
        rows_r, rows_i = [None] * SUBLANES, [None] * SUBLANES
        order = range(SUBLANES - 1, -1, -1) if reverse else range(SUBLANES)
        for b in order:
            rows_r[b], rows_i[b] = c_r, c_i
            e_r, e_i = ends[2 * m][b:b + 1], ends[2 * m + 1][b:b + 1]
            c_r, c_i = ljr * c_r - lji * c_i + e_r, ljr * c_i + lji * c_r + e_i
        car_ref[cr, 0:1, :] = c_r
        car_ref[ci, 0:1, :] = c_i
        entry.append((jnp.concatenate(rows_r, axis=0), jnp.concatenate(rows_i, axis=0)))

    def pass2(jj, carry):
        j, rows = step_rows(jj, row0)
        decayed, acc = carry[:2 * nm], carry[2 * nm:]
        out_d, out_a = [], []
        for m, (cr, ci, cl) in enumerate(cols):
            lr, li = lam[m]
            dr, di = decayed[2 * m], decayed[2 * m + 1]
            dr, di = lr * dr - li * di, lr * di + li * dr
            nr = s_ref[cr, rows, :] + dr
            ni = s_ref[ci, rows, :] + di
            s_ref[cr, rows, :] = nr
            s_ref[ci, rows, :] = ni
            out_d += [dr, di]
            if prev_ref is not None:
                prow = pl.ds(prev_row0 + (j - 1) * SUBLANES, SUBLANES)
                qr = prev_ref[cr, prow, :]
                qi = prev_ref[ci, prow, :]
                out_a += [acc[2 * m] + (nr * qr + ni * qi), acc[2 * m + 1] + (ni * qr - nr * qi)]
        return tuple(out_d) + tuple(out_a)

    n_acc = 2 * nm if prev_ref is not None else 0
    init = tuple(e for pair in entry for e in pair) + tuple(jnp.zeros(full, F32) for _ in range(n_acc))
    accs = run_steps(pass2, init)[2 * nm:]
    if prev_ref is not None:
        for m, (_, _, cl) in enumerate(cols):
            glam_ref[0:1, pl.ds(cl, SLAB)] += jnp.sum(accs[2 * m], axis=0, keepdims=True)
            glam_ref[1:2, pl.ds(cl, SLAB)] += jnp.sum(accs[2 * m + 1], axis=0, keepdims=True)


def _permute_rows_f32(perm_bf16, v):
    hi = v.astype(BF16)
    lo = (v - hi.astype(F32)).astype(BF16)
    return (jnp.dot(perm_bf16, hi, preferred_element_type=F32) + jnp.dot(perm_bf16, lo, preferred_element_type=F32))


def _put_slab(s_ref, rows, q, val):
    per = 2 * SLAB_STATES // SLAB
    for i in range(per):
        s_ref[q * per + i, rows, :] = val[:, i * SLAB:(i + 1) * SLAB]


def _get_slab(s_ref, rows, q):
    per = 2 * SLAB_STATES // SLAB
    return jnp.concatenate([s_ref[q * per + i, rows, :] for i in range(per)], axis=1)


def _step_major_perm(tt):
    r = jnp.arange(tt)
    held = (r % SUBLANES) * (tt // SUBLANES) + r // SUBLANES
    return held[:, None] == r[None, :]


def _ssm_fwd(proj, u_block, bq, cq, lam, pw, d_skip):
    t = proj.shape[0]
    nq, ds, w2 = bq.shape
    assert ds == SLAB and w2 == 2 * SLAB_STATES
    dssm = nq * SLAB
    width = nq * w2
    ntile = width // SLAB
    tt = min(T_SCAN, t)
    steps = tt // SUBLANES
    perm = _step_major_perm(tt)
    pm, pmt = perm.astype(BF16), perm.T.astype(BF16)

    def body(u_ref, pm_ref, pmt_ref, bq_ref, cq_ref, lam_ref, pw_ref, d_ref, y_ref, cin_ref, s_ref, car_ref):
        @pl.when(pl.program_id(0) == 0)
        def _():
            car_ref[...] = jnp.zeros_like(car_ref)

        cin_ref[...] = jnp.broadcast_to(car_ref[:, 0:1, :], cin_ref.shape)
        u = u_ref[...]
        ub = jnp.dot(pm_ref[...], u.astype(BF16), preferred_element_type=F32).astype(BF16)
        everything = slice(None)
        for q in range(nq):
            _put_slab(s_ref, everything, q,
                      jnp.dot(ub[:, q * SLAB:(q + 1) * SLAB], bq_ref[q], preferred_element_type=F32))
        for q in range(nq):
            _scan_slab(s_ref, 0, q, lam_ref, pw_ref, car_ref, steps, reverse=False)
        y_sm = jnp.concatenate(
            [jnp.dot(_get_slab(s_ref, everything, q).astype(BF16), cq_ref[q], preferred_element_type=F32)
             for q in range(nq)], axis=1)
        y_ref[...] = _permute_rows_f32(pmt_ref[...], y_sm) + d_ref[...] * u

    c3 = lambda i: (0, 0, 0)
    c2 = lambda i: (0, 0)
    return _pallas(
        body, name="ssm_fwd", grid=(t // tt,),
        in_specs=[pl.BlockSpec((tt, dssm), lambda i: (i, u_block)), pl.BlockSpec((tt, tt), c2),
                  pl.BlockSpec((tt, tt), c2), pl.BlockSpec(bq.shape, c3),
                  pl.BlockSpec(cq.shape, c3), pl.BlockSpec(lam.shape, c2), pl.BlockSpec(pw.shape, c2),
                  pl.BlockSpec((1, dssm), c2)],
        out_specs=[pl.BlockSpec((tt, dssm), lambda i: (i, 0)),
                   pl.BlockSpec((None, ntile, SUBLANES, SLAB), lambda i: (i, 0, 0, 0))],
        out_shape=[jax.ShapeDtypeStruct((t, dssm), F32), jax.ShapeDtypeStruct((t // tt, ntile, SUBLANES, SLAB), F32)],
        scratch_shapes=[pltpu.VMEM((ntile, tt, SLAB), F32), pltpu.VMEM((ntile, SUBLANES, SLAB), F32)],
        compiler_params=_cp(2 * (8 * tt * dssm + 4 * nq * ds * w2) + 4 * tt * width + 16 * MIB,
                            dimension_semantics=("arbitrary",)),
    )(proj, pm, pmt, bq, cq, lam, pw, d_skip)


def _ssm_bwd(proj, u_block, dy, cin, dproj, bq, cq, lam, pw, d_skip):
    t = proj.shape[0]
    nq, ds, w2 = bq.shape
    dssm = nq * SLAB
    width = nq * w2
    ntile = width // SLAB
    tt = min(T_SCAN, t)
    nt = t // tt
    steps = tt // SUBLANES
    halo = SUBLANES
    perm = _step_major_perm(tt)
    pm, pmt = perm.astype(BF16), perm.T.astype(BF16)

    def body(u_ref, dy_ref, cin_ref, dp_any, pm_ref, pmt_ref, bq_ref, cq_ref, lam_ref, pw_ref, d_ref,
             du_ref, gb_ref, gc_ref, glam_ref, gd_ref, s_ref, gs_ref, car_f, car_b):
        del dp_any

        @pl.when(pl.program_id(0) == 0)
        def _():
            car_b[...] = jnp.zeros_like(car_b)
            gb_ref[...] = jnp.zeros_like(gb_ref)
            gc_ref[...] = jnp.zeros_like(gc_ref)
            glam_ref[...] = jnp.zeros_like(glam_ref)
            gd_ref[...] = jnp.zeros_like(gd_ref)

        u = u_ref[...]
        dyv = dy_ref[...]
        gd_ref[0:1, :] += jnp.sum(dyv * u, axis=0, keepdims=True)
        pmv = pm_ref[...]
        ub = jnp.dot(pmv, u.astype(BF16), preferred_element_type=F32).astype(BF16)
        dyb = jnp.dot(pmv, dyv.astype(BF16), preferred_element_type=F32).astype(BF16)
        car_f[...] = cin_ref[...]
        data = slice(halo, halo + tt)
        everything = slice(None)
        for q in range(nq):
            _put_slab(s_ref, data, q, jnp.dot(ub[:, q * SLAB:(q + 1) * SLAB], bq_ref[q], preferred_element_type=F32))
        for q in range(nq):
            _scan_slab(s_ref, halo, q, lam_ref, pw_ref, car_f, steps, reverse=False)
        last_step = s_ref[:, halo + tt - SUBLANES:halo + tt, :]
        s_ref[:, 0:halo, :] = jnp.concatenate([cin_ref[:, 0:1, :], last_step[:, 0:SUBLANES - 1, :]], axis=1)
        tn = (((0,), (0,)), ((), ()))
        nt_dims = (((1,), (1,)), ((), ()))
        for q in range(nq):
            sl = slice(q * SLAB, (q + 1) * SLAB)
            gc_ref[q] += lax.dot_general(dyb[:, sl], _get_slab(s_ref, data, q).astype(BF16), tn,
                                         preferred_element_type=F32)
            _put_slab(gs_ref, everything, q,
                      lax.dot_general(dyb[:, sl], cq_ref[q], nt_dims, preferred_element_type=F32))
        for q in range(nq):
            _scan_slab(gs_ref, 0, q, lam_ref, pw_ref, car_b, steps, reverse=True,
                       prev_ref=s_ref, prev_row0=halo, glam_ref=glam_ref)
        du_parts = []
        for q in range(nq):
            sl = slice(q * SLAB, (q + 1) * SLAB)
            gsb = _get_slab(gs_ref, everything, q).astype(BF16)
            du_parts.append(lax.dot_general(gsb, bq_ref[q], nt_dims, preferred_element_type=F32))
            gb_ref[q] += lax.dot_general(ub[:, sl], gsb, tn, preferred_element_type=F32)
        du_sm = jnp.concatenate(du_parts, axis=1)
        du_ref[...] = (_permute_rows_f32(pmt_ref[...], du_sm) + dyv * d_ref[...]).astype(BF16)

    c3 = lambda i: (0, 0, 0)
    c2 = lambda i: (0, 0)
    rev = lambda i: (nt - 1 - i, 0)
    dense = jax.ShapeDtypeStruct((nq, SLAB, w2), F32)
    gp = lam.shape[1]
    return _pallas(
        body, name="ssm_bwd", grid=(nt,),
        in_specs=[pl.BlockSpec((tt, dssm), lambda i: (nt - 1 - i, u_block)), pl.BlockSpec((tt, dssm), rev),
                  pl.BlockSpec((None, ntile, SUBLANES, SLAB), lambda i: (nt - 1 - i, 0, 0, 0)),
                  pl.BlockSpec(memory_space=pl.ANY), pl.BlockSpec((tt, tt), c2), pl.BlockSpec((tt, tt), c2),
                  pl.BlockSpec(bq.shape, c3), pl.BlockSpec(cq.shape, c3),
                  pl.BlockSpec(lam.shape, c2), pl.BlockSpec(pw.shape, c2), pl.BlockSpec((1, dssm), c2)],
        out_specs=[pl.BlockSpec((tt, dssm), lambda i: (nt - 1 - i, u_block)), pl.BlockSpec(dense.shape, c3),
                   pl.BlockSpec(dense.shape, c3), pl.BlockSpec((SUBLANES, gp), c2), pl.BlockSpec((SUBLANES, dssm), c2)],
        out_shape=[jax.ShapeDtypeStruct(dproj.shape, dproj.dtype), dense, dense,
                   jax.ShapeDtypeStruct((SUBLANES, gp), F32), jax.ShapeDtypeStruct((SUBLANES, dssm), F32)],
        scratch_shapes=[pltpu.VMEM((ntile, tt + halo, SLAB), F32), pltpu.VMEM((ntile, tt, SLAB), F32),
                        pltpu.VMEM((ntile, SUBLANES, SLAB), F32), pltpu.VMEM((ntile, SUBLANES, SLAB), F32)],
        input_output_aliases={3: 0},
        compiler_params=_cp(2 * (10 * tt * dssm + 4 * nq * ds * w2 + 8 * nq * SLAB * w2)
                            + 8 * tt * width + 12 * MIB, dimension_semantics=("arbitrary",)),
    )(proj, dy, cin, dproj, pm, pmt, bq, cq, lam, pw, d_skip)


def _gate_fwd(proj, y, conv_w, conv_b, w_glu, b_glu, dc):
    t = proj.shape[0]
    dssm = y.shape[1]
    assert dc == dssm
    tm = min(TM_GATE, t)
    halo = SUBLANES

    rb = 2 * SUBLANES
    assert tm % rb == 0

    def body(b_ref, c_ref, v_ref, zc_ref, zs_ref, y_ref, cw_ref, cb_ref, wg_ref, bg_ref, mix_ref,
             cv_buf, sh1_buf, sh2_buf, ge_buf, geb_buf, gl_buf):
        @pl.when(pl.program_id(0) == 0)
        def _():
            cv_buf[0:halo, :] = jnp.zeros((halo, dc), F32)

        cv_buf[halo:, :] = c_ref[...] * v_ref[...]
        sh1_buf[...] = cv_buf[halo - 1:halo - 1 + tm, :]
        sh2_buf[...] = cv_buf[halo - 2:halo - 2 + tm, :]
        w0, w1, w2, cb = cw_ref[0:1, :], cw_ref[1:2, :], cw_ref[2:3, :], cb_ref[...]
        chunks = [pl.ds(r * rb, rb) for r in range(tm // rb)]
        for r, rs in enumerate(chunks):
            ge, _ = _gelu(y_ref[rs, :])
            ge_buf[rs, :] = ge
            geb_buf[rs, :] = ge.astype(BF16)
        gl_buf[...] = jnp.dot(geb_buf[...], wg_ref[...], preferred_element_type=F32) + bg_ref[...]
        for r, rs in enumerate(chunks):
            conv = cb + w2 * cv_buf[pl.ds(halo + r * rb, rb), :] + w1 * sh1_buf[rs, :] + w0 * sh2_buf[rs, :]
            sz, _ = _silu(zc_ref[rs, :])
            mix_ref[rs, 0:dc] = (b_ref[rs, :] * conv * sz).astype(BF16)
        cv_buf[0:halo, :] = cv_buf[tm:tm + halo, :]
        for r, rs in enumerate(chunks):
            szs, _ = _silu(zs_ref[rs, :])
            mix_ref[rs, dc:] = (ge_buf[rs, :] * jax.nn.sigmoid(gl_buf[rs, :]) * szs).astype(BF16)

    col = lambda j: pl.BlockSpec((tm, dc), lambda i, j=j: (i, j))
    fixed = lambda i: (0, 0)
    return _pallas(
        body, name="gate_fwd", grid=(t // tm,),
        in_specs=[col(0), col(1), col(2), col(3), col(5), pl.BlockSpec((tm, dssm), lambda i: (i, 0)),
                  pl.BlockSpec(conv_w.shape, fixed), pl.BlockSpec((1, dc), fixed),
                  pl.BlockSpec(w_glu.shape, fixed), pl.BlockSpec((1, dssm), fixed)],
        out_specs=pl.BlockSpec((tm, dc + dssm), lambda i: (i, 0)),
        out_shape=jax.ShapeDtypeStruct((t, dc + dssm), BF16),
        scratch_shapes=[pltpu.VMEM((tm + halo, dc), F32), pltpu.VMEM((tm, dc), F32), pltpu.VMEM((tm, dc), F32),
                        pltpu.VMEM((tm, dssm), F32), pltpu.VMEM((tm, dssm), BF16), pltpu.VMEM((tm, dssm), F32)],
        compiler_params=_cp(2 * (6 * 4 * tm * dc + 2 * tm * (dc + dssm) + 2 * dssm * dssm) + 24 * tm * dc + 8 * MIB,
                            dimension_semantics=("arbitrary",)),
    )(proj, proj, proj, proj, proj, y, conv_w, conv_b, w_glu, b_glu)


def _gate_bwd(proj, y, dmix, conv_w, conv_b, w_glu, b_glu, dc):
    t = proj.shape[0]
    dssm = y.shape[1]
    tm = min(TM_GATE, t)
    nt = t // tm
    halo = SUBLANES
    blocks_per_tile = tm // halo

    rb = 2 * SUBLANES
    assert tm % rb == 0
    n_chunk = tm // rb

    def body(b_ref, c_ref, v_ref, zc_ref, zs_ref, cp_ref, vp_ref, y_ref, dm_ref, cw_ref, cb_ref, wg_ref, bg_ref,
             dp_ref, dy_ref, gs_ref, gwg_ref, cv_buf, dc_buf, sh1_buf, sh2_buf, geb_buf, dglb_buf, ge_buf, th_buf):
        i = pl.program_id(0)

        @pl.when(i == 0)
        def _():
            dc_buf[tm:, :] = jnp.zeros((halo, dc), F32)
            gs_ref[...] = jnp.zeros_like(gs_ref)
            gwg_ref[...] = jnp.zeros_like(gwg_ref)

        def rows(r):
            return pl.ds(r * rb, rb)

        def rows_after_halo(r):
            return pl.ds(halo + r * rb, rb)

        def chunks(step, carry):
            for r in range(n_chunk):
                carry = step(r, carry)
            return carry

        def fold(v):
            return v[0:SUBLANES] + v[SUBLANES:rb]

        first_tile = (i == nt - 1)
        cv_buf[0:halo, :] = jnp.where(first_tile, 0.0, cp_ref[...] * vp_ref[...])
        cv_buf[halo:, :] = c_ref[...] * v_ref[...]
        sh1_buf[...] = cv_buf[halo - 1:halo - 1 + tm, :]
        sh2_buf[...] = cv_buf[halo - 2:halo - 2 + tm, :]
        w0, w1, w2, cb = cw_ref[0:1, :], cw_ref[1:2, :], cw_ref[2:3, :], cb_ref[...]

        def conv_pass_a(r, carry):
            rs = rows(r)
            bv, zc, dyc = b_ref[rs, :], zc_ref[rs, :], dm_ref[rs, 0:dc]
            conv = cb + w2 * cv_buf[rows_after_halo(r), :] + w1 * sh1_buf[rs, :] + w0 * sh2_buf[rs, :]
            sz, sgc = _silu(zc)
            dp_ref[rs, 0:dc] = (dyc * conv * sz).astype(BF16)
            dp_ref[rs, 3 * dc:4 * dc] = (dyc * bv * conv * _dsilu(zc, sgc)).astype(BF16)
            dc_buf[rs, :] = dyc * bv * sz
            return carry

        chunks(conv_pass_a, 0)
        sh1_buf[...] = dc_buf[1:1 + tm, :]
        sh2_buf[...] = dc_buf[2:2 + tm, :]

        def conv_pass_b(r, acc):
            rs = rows(r)
            dconv, d1, d2 = dc_buf[rs, :], sh1_buf[rs, :], sh2_buf[rs, :]
            cv = cv_buf[rows_after_halo(r), :]
            dcv = w2 * dconv + w1 * d1 + w0 * d2
            dp_ref[rs, dc:2 * dc] = (dcv * v_ref[rs, :]).astype(BF16)
            dp_ref[rs, 2 * dc:3 * dc] = (dcv * c_ref[rs, :]).astype(BF16)
            return (acc[0] + fold(cv * d2), acc[1] + fold(cv * d1), acc[2] + fold(cv * dconv), acc[3] + fold(dconv))

        zero = jnp.zeros((SUBLANES, dc), F32)
        sums = chunks(conv_pass_b, (zero, zero, zero, zero))
        for k in range(4):
            gs_ref[k:k + 1, :] += jnp.sum(sums[k], axis=0, keepdims=True)
        dc_buf[tm:, :] = dc_buf[0:halo, :]

        def glu_pass_a(r, carry):
            rs = rows(r)
            ge, th = _gelu(y_ref[rs, :])
            geb_buf[rs, :] = ge.astype(BF16)
            ge_buf[rs, :] = ge
            th_buf[rs, :] = th
            return carry

        chunks(glu_pass_a, 0)
        wg = wg_ref[...]
        gl_buf = cv_buf.at[halo:halo + tm]
        keep_buf = dc_buf.at[0:tm]
        gl_buf[...] = jnp.dot(geb_buf[...], wg, preferred_element_type=F32) + bg_ref[...]
        dp_ref[:, 4 * dc:5 * dc] = jnp.zeros((tm, dc), BF16)

        def glu_pass_b(r, acc):
            rs = rows(r)
            ge = ge_buf[rs, :]
            sg = jax.nn.sigmoid(gl_buf[rs, :])
            zs, dys = zs_ref[rs, :], dm_ref[rs, dc:]
            szs, sgs = _silu(zs)
            dp_ref[rs, 5 * dc:] = (dys * ge * sg * _dsilu(zs, sgs)).astype(BF16)
            d_ys = dys * szs
            dgl = d_ys * ge * sg * (1.0 - sg)
            dglb_buf[rs, :] = dgl.astype(BF16)
            keep_buf[rs, :] = d_ys * sg
            return acc + fold(dgl)

        gs_ref[4:5, :] += jnp.sum(chunks(glu_pass_b, zero), axis=0, keepdims=True)
        dglb = dglb_buf[...]
        gwg_ref[...] += lax.dot_general(geb_buf[...], dglb, (((0,), (0,)), ((), ())), preferred_element_type=F32)
        gl_buf[...] = lax.dot_general(dglb, wg, (((1,), (1,)), ((), ())), preferred_element_type=F32)

        def glu_pass_c(r, carry):
            rs = rows(r)
            dy_ref[rs, :] = (keep_buf[rs, :] + gl_buf[rs, :]) * _dgelu(y_ref[rs, :], th_buf[rs, :])
            return carry

        chunks(glu_pass_c, 0)

    col = lambda j: pl.BlockSpec((tm, dc), lambda i, j=j: (nt - 1 - i, j))
    prev = lambda j: pl.BlockSpec((halo, dc), lambda i, j=j: (jnp.maximum((nt - 1 - i) * blocks_per_tile - 1, 0), j))
    rev = lambda i: (nt - 1 - i, 0)
    fixed = lambda i: (0, 0)
    return _pallas(
        body, name="gate_bwd", grid=(nt,),
        in_specs=[col(0), col(1), col(2), col(3), col(5), prev(1), prev(2), pl.BlockSpec((tm, dssm), rev),
                  pl.BlockSpec((tm, dc + dssm), rev), pl.BlockSpec(conv_w.shape, fixed), pl.BlockSpec((1, dc), fixed),
                  pl.BlockSpec(w_glu.shape, fixed), pl.BlockSpec((1, dssm), fixed)],
        out_specs=[pl.BlockSpec((tm, 6 * dc), rev), pl.BlockSpec((tm, dssm), rev),
                   pl.BlockSpec((2 * SUBLANES, dc), fixed), pl.BlockSpec((dssm, dssm), fixed)],
        out_shape=[jax.ShapeDtypeStruct((t, 6 * dc), BF16), jax.ShapeDtypeStruct((t, dssm), F32),
                   jax.ShapeDtypeStruct((2 * SUBLANES, dc), F32), jax.ShapeDtypeStruct((dssm, dssm), F32)],
        scratch_shapes=[pltpu.VMEM((tm + halo, dc), F32), pltpu.VMEM((tm + halo, dc), F32), pltpu.VMEM((tm, dc), F32),
                        pltpu.VMEM((tm, dc), F32), pltpu.VMEM((tm, dssm), BF16), pltpu.VMEM((tm, dssm), BF16),
                        pltpu.VMEM((tm, dssm), F32), pltpu.VMEM((tm, dssm), F32)],
        compiler_params=_cp(2 * (6 * 4 * tm * dc + 8 * tm * dc + 12 * tm * dc + 4 * tm * dc + 6 * dssm * dssm)
                            + 28 * tm * dc + 8 * MIB, dimension_semantics=("arbitrary",)),
    )(proj, proj, proj, proj, proj, proj, proj, y, dmix, conv_w, conv_b, w_glu, b_glu)


def _allgather_flat(v, name):
    r, c = v.shape
    rels = [(dx, dy, dc) for dx in (0, 1) for dy in (0, 1) for dc in (0, 1)][1:]

    def body(v_ref, out_ref, send, recv):
        x, y, cc = _my_place()
        me = 4 * x + 2 * y + cc

        def peer(rel):
            dx, dy, dc = rel
            return (1 - x if dx else x, 1 - y if dy else y, 1 - cc if dc else cc)

        def rc(s, slot, to):
            return pltpu.make_async_remote_copy(src_ref=v_ref, dst_ref=out_ref.at[slot], send_sem=send.at[s],
                                                recv_sem=recv.at[s], device_id=to, device_id_type=MESH)

        sent = []
        for s, rel in enumerate(rels):
            cp = rc(s, me, peer(rel))
            cp.start()
            sent.append(cp)
        out_ref[me] = v_ref[...]
        for s, rel in enumerate(rels):
            px, py, pc = peer(rel)
            rc(s, 4 * px + 2 * py + pc, (px, py, pc)).wait_recv()
        for cp in sent:
            cp.wait_send()

    return _pallas(
        body, name=name, out_shape=jax.ShapeDtypeStruct((N_DEV, r, c), F32),
        in_specs=[pl.BlockSpec(memory_space=pltpu.VMEM)], out_specs=pl.BlockSpec(memory_space=pltpu.VMEM),
        scratch_shapes=[pltpu.SemaphoreType.DMA((N_DEV - 1,)), pltpu.SemaphoreType.DMA((N_DEV - 1,))],
        compiler_params=_cp((N_DEV + 2) * 4 * r * c + 8 * MIB),
    )(v)


def _rs_pair_exchange(grads, name):
    n = len(grads)

    def body(*refs):
        ins, outs = refs[:n], refs[n:2 * n]
        send, recv = refs[2 * n:]
        x, y, c = _my_place()
        cps = []
        for t in range(n):
            cp = pltpu.make_async_remote_copy(src_ref=ins[t].at[:, 1 - c], dst_ref=outs[t], send_sem=send.at[t],
                                              recv_sem=recv.at[t], device_id=(x, y, 1 - c), device_id_type=MESH)
            cp.start()
            cps.append(cp)
        for cp in cps:
            cp.wait()

    any_spec = pl.BlockSpec(memory_space=pl.ANY)
    return _pallas(
        body, name=name, in_specs=[any_spec] * n, out_specs=[any_spec] * n,
        out_shape=[jax.ShapeDtypeStruct((g.shape[0],) + g.shape[2:], g.dtype) for g in grads],
        scratch_shapes=[pltpu.SemaphoreType.DMA((n,)), pltpu.SemaphoreType.DMA((n,))],
        compiler_params=_cp(16 * MIB),
    )(*grads)


def _rs_pair_add(place, grad, got, name):
    nk, _, r2, c = grad.shape
    tr = min(TR_ELT, r2)

    def body(place_ref, g_ref, r_ref, o16_ref, o32_ref):
        del place_ref
        s = g_ref[...] + r_ref[...].astype(F32)
        o32_ref[...] = s
        o16_ref[...] = s.astype(BF16)

    blk = pl.BlockSpec((None, tr, c), lambda k, i, p: (k, i, 0))
    return _pallas(
        body, name=name,
        grid_spec=pltpu.PrefetchScalarGridSpec(
            num_scalar_prefetch=1, grid=(nk, r2 // tr),
            in_specs=[pl.BlockSpec((None, None, tr, c), lambda k, i, p: (k, p[0], i, 0)), blk],
            out_specs=[blk, blk]),
        out_shape=[jax.ShapeDtypeStruct((nk, r2, c), BF16), jax.ShapeDtypeStruct((nk, r2, c), F32)],
        compiler_params=_cp(2 * 14 * tr * c + 8 * MIB),
    )(place, grad, got)


_HBM_SPEC = pl.BlockSpec(memory_space=pltpu.HBM)
_SEM_SPEC = pl.BlockSpec(memory_space=pltpu.SEMAPHORE)
_DATAFLOW = pltpu.SideEffectType.DATAFLOW_SIDE_EFFECTING


def _split_copy_start(srcs, land_shapes, plan, n_sems, name, after=()):
    ns, nl, na = len(srcs), len(land_shapes), len(after)

    def body(*refs):
        src_refs, land_refs = refs[:ns], refs[ns:ns + nl]
        send, recv = refs[ns + nl + na], refs[ns + nl + na + 1]
        token = refs[-1]
        sends, _ = plan(src_refs, land_refs)
        for src, dst, to, si, ri in sends:
            pltpu.make_async_remote_copy(src_ref=src, dst_ref=dst, send_sem=send.at[si], recv_sem=recv.at[ri],
                                         device_id=to, device_id_type=MESH).start()
        token[...] = jnp.zeros_like(token)

    lands = [lax.empty(shp, dt) for shp, dt in land_shapes]
    through = [pltpu.HBM(a.shape, a.dtype) for a in srcs] + [pltpu.HBM(shp, dt) for shp, dt in land_shapes]
    out = pl.pallas_call(
        body, name=name,
        out_shape=(pltpu.SemaphoreType.DMA((n_sems,)), pltpu.SemaphoreType.DMA((n_sems,)), *through,
                   jax.ShapeDtypeStruct((SUBLANES, SLAB), F32)),
        in_specs=[_HBM_SPEC] * (ns + nl) + [pl.BlockSpec(memory_space=pl.ANY)] * na,
        out_specs=(_SEM_SPEC, _SEM_SPEC, *([_HBM_SPEC] * (ns + nl)), pl.BlockSpec(memory_space=pltpu.VMEM)),
        input_output_aliases={i: 2 + i for i in range(ns + nl)},
        compiler_params=pltpu.CompilerParams(has_side_effects=_DATAFLOW),
    )(*[pltpu.with_memory_space_constraint(a, pltpu.HBM) for a in (*srcs, *lands)], *after)
    return out[0], out[1], list(out[2:2 + ns]), list(out[2 + ns:2 + ns + nl]), out[-1]


def _split_copy_wait(send, recv, srcs, lands, plan, after, name):
    ns, nl, na = len(srcs), len(lands), len(after)

    def body(*refs):
        src_refs, land_refs = refs[:ns], refs[ns:ns + nl]
        send_ref, recv_ref = refs[ns + nl], refs[ns + nl + 1]
        sends, arrivals = plan(src_refs, land_refs)
        for src, dst, to, si, ri in sends:
            pltpu.make_async_remote_copy(src_ref=src, dst_ref=dst, send_sem=send_ref.at[si], recv_sem=recv_ref.at[ri],
                                         device_id=to, device_id_type=MESH).wait_send()
        for (src, _, to, si, _), (view, ri) in zip(sends, arrivals):
            pltpu.make_async_remote_copy(src_ref=view, dst_ref=view, send_sem=send_ref.at[si], recv_sem=recv_ref.at[ri],
                                         device_id=to, device_id_type=MESH).wait_recv()

    out = pl.pallas_call(
        body, name=name,
        out_shape=[pltpu.HBM(a.shape, a.dtype) for a in (*srcs, *lands)],
        in_specs=[_HBM_SPEC] * (ns + nl) + [_SEM_SPEC, _SEM_SPEC] + [pl.BlockSpec(memory_space=pl.ANY)] * na,
        out_specs=[_HBM_SPEC] * (ns + nl),
        input_output_aliases={i: i for i in range(ns + nl)},
        compiler_params=pltpu.CompilerParams(has_side_effects=_DATAFLOW),
    )(*srcs, *lands, send, recv, *after)
    return list(out[:ns]), list(out[ns:])


def _chip_exchange_plan(n):
    per = N_CHIPS - 1

    def plan(srcs, lands):
        x, y, c = _my_place()
        sends, arrivals = [], []
        for t in range(n):
            for j, (cx, cy) in enumerate(_other_chips(x, y)):
                sends.append((srcs[t].at[2 * cx + cy], lands[t].at[j], (cx, cy, c), t * per + j, t * per + j))
                arrivals.append((lands[t].at[j], t * per + j))
        return sends, arrivals

    return plan


def _gather_half_plan(n):
    per = N_CHIPS - 1

    def plan(srcs, lands):
        x, y, c = _my_place()
        k = 2 * x + y
        sends, arrivals = [], []
        for t in range(n):
            for j, (cx, cy) in enumerate(_other_chips(x, y)):
                sends.append((srcs[t].at[c], lands[t].at[k, c], (cx, cy, c), t * per + j, t * per + j))
                arrivals.append((lands[t].at[2 * cx + cy, c], t * per + j))
        return sends, arrivals

    return plan


def _gather_complete(own, landed, name):
    def body(own_ref, in_ref, out_ref, send, recv, local):
        x, y, c = _my_place()
        sib = (x, y, 1 - c)
        mine = pltpu.make_async_copy(own_ref, out_ref.at[2 * x + y], local)
        mine.start()
        sent = []
        for j, (cx, cy) in enumerate(_other_chips(x, y)):
            half = (2 * cx + cy, c)
            cp = pltpu.make_async_remote_copy(src_ref=in_ref.at[half], dst_ref=out_ref.at[half], send_sem=send.at[j],
                                              recv_sem=recv.at[j], device_id=sib, device_id_type=MESH)
            cp.start()
            sent.append(cp)
        for j, (cx, cy) in enumerate(_other_chips(x, y)):
            other = out_ref.at[2 * cx + cy, 1 - c]
            pltpu.make_async_remote_copy(src_ref=other, dst_ref=other, send_sem=send.at[j], recv_sem=recv.at[j],
                                         device_id=sib, device_id_type=MESH).wait_recv()
        for cp in sent:
            cp.wait_send()
        mine.wait()

    any_spec = pl.BlockSpec(memory_space=pl.ANY)
    per = N_CHIPS - 1
    return _pallas(
        body, name=name, in_specs=[any_spec, any_spec], out_specs=any_spec,
        out_shape=jax.ShapeDtypeStruct(landed.shape, landed.dtype), input_output_aliases={1: 0},
        scratch_shapes=[pltpu.SemaphoreType.DMA((per,)), pltpu.SemaphoreType.DMA((per,)), pltpu.SemaphoreType.DMA],
        compiler_params=_cp(16 * MIB),
    )(own, landed)


def _gather_direct_plan(n):
    per = 2 * (N_CHIPS - 1)

    def plan(srcs, lands):
        x, y, c = _my_place()
        k = 2 * x + y
        sends, arrivals = [], []
        for t in range(n):
            for j, (cx, cy) in enumerate(_other_chips(x, y)):
                for core in (0, 1):
                    sends.append((srcs[t].at[c], lands[t].at[k, c], (cx, cy, core),
                                  t * per + 2 * j + core, t * per + 2 * j + c))
                    arrivals.append((lands[t].at[2 * cx + cy, core], t * per + 2 * j + core))
        return sends, arrivals

    return plan


def _rs_chip_add(place, sums32, got, name):
    _, r2, c = sums32.shape
    tr = min(TR_ELT, r2)

    def body(place_ref, s_ref, q_ref, o_ref):
        del place_ref
        o_ref[...] = ((s_ref[...] + q_ref[0].astype(F32)) + q_ref[1].astype(F32)) + q_ref[2].astype(F32)

    return _pallas(
        body, name=name,
        grid_spec=pltpu.PrefetchScalarGridSpec(
            num_scalar_prefetch=1, grid=(r2 // tr,),
            in_specs=[pl.BlockSpec((None, tr, c), lambda i, p: (p[1], i, 0)),
                      pl.BlockSpec((N_CHIPS - 1, tr, c), lambda i, p: (0, i, 0))],
            out_specs=pl.BlockSpec((tr, c), lambda i, p: (i, 0))),
        out_shape=jax.ShapeDtypeStruct((r2, c), F32),
        compiler_params=_cp(2 * 14 * tr * c + 8 * MIB),
    )(place, sums32, got)


def _rs_pair_share(halves, name):
    n = len(halves)

    def body(*refs):
        ins, outs = refs[:n], refs[n:2 * n]
        send, recv, local = refs[2 * n:]
        x, y, c = _my_place()
        cps, mine = [], []
        for t in range(n):
            lc = pltpu.make_async_copy(ins[t], outs[t].at[c], local.at[t])
            lc.start()
            mine.append(lc)
            cp = pltpu.make_async_remote_copy(src_ref=ins[t], dst_ref=outs[t].at[c], send_sem=send.at[t],
                                              recv_sem=recv.at[t], device_id=(x, y, 1 - c), device_id_type=MESH)
            cp.start()
            cps.append(cp)
        for t in range(n):
            other = outs[t].at[1 - c]
            pltpu.make_async_remote_copy(src_ref=other, dst_ref=other, send_sem=send.at[t], recv_sem=recv.at[t],
                                         device_id=(x, y, 1 - c), device_id_type=MESH).wait_recv()
        for cp in cps:
            cp.wait_send()
        for lc in mine:
            lc.wait()

    any_spec = pl.BlockSpec(memory_space=pl.ANY)
    return _pallas(
        body, name=name, in_specs=[any_spec] * n, out_specs=[any_spec] * n,
        out_shape=[jax.ShapeDtypeStruct((2,) + h.shape, F32) for h in halves],
        scratch_shapes=[pltpu.SemaphoreType.DMA((n,)), pltpu.SemaphoreType.DMA((n,)), pltpu.SemaphoreType.DMA((n,))],
        compiler_params=_cp(16 * MIB),
    )(*halves)


_PACK_TILE = SUBLANES * SLAB


def _pack(arrays):
    rows = []
    for a in arrays:
        flat = a.reshape(-1).astype(F32)
        padded = -(-flat.shape[0] // _PACK_TILE) * _PACK_TILE
        rows.append(jnp.pad(flat, (0, padded - flat.shape[0])).reshape(-1, SLAB))
    n_rows = sum(r.shape[0] for r in rows)
    if n_rows % (2 * SUBLANES):
        rows.append(jnp.zeros((SUBLANES, SLAB), F32))
    return jnp.concatenate(rows, axis=0)


def _unpack(packed, shapes):
    out, row = [], 0
    for shp in shapes:
        size = math.prod(shp)
        nrow = -(-size // _PACK_TILE) * SUBLANES
        out.append(packed[row:row + nrow].reshape(-1)[:size].reshape(shp))
        row += nrow
    return out


def kernel(x, norm_pre_g, w_in, conv_w, conv_b, ssm_a_re, ssm_a_im, ssm_log_dt, ssm_b_re, ssm_b_im, ssm_c_re, ssm_c_im, ssm_d, w_glu, b_glu, w_out, norm_post_g, loss_target, m_norm_pre_g, m_w_in, m_conv_w, m_conv_b, m_ssm_a_re, m_ssm_a_im, m_ssm_log_dt, m_ssm_b_re, m_ssm_b_im, m_ssm_c_re, m_ssm_c_im, m_ssm_d, m_w_glu, m_b_glu, m_w_out, m_norm_post_g, v_norm_pre_g, v_w_in, v_conv_w, v_conv_b, v_ssm_a_re, v_ssm_a_im, v_ssm_log_dt, v_ssm_b_re, v_ssm_b_im, v_ssm_c_re, v_ssm_c_im, v_ssm_d, v_w_glu, v_b_glu, v_w_out, v_norm_post_g):
    xs, tgt = x[0], loss_target[0]
    t, d = xs.shape
    dc = conv_b.shape[0]
    dssm = ssm_d.shape[0]
    g, p = ssm_a_re.shape
    h = SSM_H
    nq = dssm // SLAB
    n_shard = w_in.shape[1]
    steps = min(T_SCAN, t) // SUBLANES
    mx, my, mc = _my_place()
    chip = 2 * mx + my
    place = jnp.stack([mc, chip]).astype(jnp.int32)

    cw_cols = conv_w.shape[1]
    cw_pad = -(-cw_cols // SLAB) * SLAB
    cw_blk = jnp.zeros((SUBLANES, cw_pad), F32).at[:conv_w.shape[0], :cw_cols].set(conv_w)
    cw_all = _allgather_flat(cw_blk, "allgather_conv_w")
    conv_w_full = jnp.concatenate([cw_all[2 * k, :, :cw_cols] for k in range(N_CHIPS)], axis=1)

    halves = lambda a: a.reshape(2, a.shape[0] // 2, a.shape[1])
    win_half = halves(_cast_bf16(w_in, "cast_w_in", cw_all[0, :, :SLAB]))
    win_plan = _gather_half_plan(1)
    win_send, win_recv, win_srcs, win_lands, win_token = _split_copy_start(
        [win_half], [((N_CHIPS,) + win_half.shape, BF16)], win_plan, N_CHIPS - 1, "gather_w_in_start")
    behind_w_in = win_token[0:1, 0:1]

    small_names = ["norm_pre_g", "conv_w", "conv_b", "ssm_a_re", "ssm_a_im", "ssm_log_dt", "ssm_b_re", "ssm_b_im",
                   "ssm_c_re", "ssm_c_im", "ssm_d", "b_glu", "norm_post_g", "loss"]
    zeros_cw = jnp.zeros((conv_w.shape[0], dc), F32)
    one0 = jnp.zeros((1,), F32)
    small_w = dict(norm_pre_g=norm_pre_g, conv_w=zeros_cw, conv_b=conv_b, ssm_a_re=ssm_a_re, ssm_a_im=ssm_a_im,
                   ssm_log_dt=ssm_log_dt, ssm_b_re=ssm_b_re, ssm_b_im=ssm_b_im, ssm_c_re=ssm_c_re, ssm_c_im=ssm_c_im,
                   ssm_d=ssm_d, b_glu=b_glu, norm_post_g=norm_post_g, loss=one0)
    small_m = dict(norm_pre_g=m_norm_pre_g, conv_w=zeros_cw, conv_b=m_conv_b, ssm_a_re=m_ssm_a_re, ssm_a_im=m_ssm_a_im,
                   ssm_log_dt=m_ssm_log_dt, ssm_b_re=m_ssm_b_re, ssm_b_im=m_ssm_b_im, ssm_c_re=m_ssm_c_re,
                   ssm_c_im=m_ssm_c_im, ssm_d=m_ssm_d, b_glu=m_b_glu, norm_post_g=m_norm_post_g, loss=one0)
    small_v = dict(norm_pre_g=v_norm_pre_g, conv_w=zeros_cw, conv_b=v_conv_b, ssm_a_re=v_ssm_a_re, ssm_a_im=v_ssm_a_im,
                   ssm_log_dt=v_ssm_log_dt, ssm_b_re=v_ssm_b_re, ssm_b_im=v_ssm_b_im, ssm_c_re=v_ssm_c_re,
                   ssm_c_im=v_ssm_c_im, ssm_d=v_ssm_d, b_glu=v_b_glu, norm_post_g=v_norm_post_g, loss=one0)
    w_pack, m_pack, v_pack = [_pack([group[nm] for nm in small_names]) + behind_w_in
                              for group in (small_w, small_m, small_v)]
    side = [halves(_cast_bf16(w_out, "cast_w_out", win_token)), halves(_cast_bf16(w_glu, "cast_w_glu", win_token))]

    expand = jnp.repeat(jnp.eye(p, dtype=F32), h, axis=1)
    b_re2, b_im2 = ssm_b_re.reshape(g, p * h), ssm_b_im.reshape(g, p * h)
    log_dt2 = ssm_log_dt.reshape(g, 1) + behind_w_in
    lbr, lbi, bbr2, bbi2, pw3 = _zoh_fwd(ssm_a_re, ssm_a_im, log_dt2, b_re2, b_im2, expand, steps)
    lam = jnp.stack([lbr.reshape(g * p), lbi.reshape(g * p)])
    pw = pw3.reshape(2, g * p)
    to_slab_b = lambda b2: _blockdiag(b2.reshape(nq, GROUPS_PER_SLAB, p, h).transpose(0, 1, 3, 2))
    bq = jnp.concatenate([to_slab_b(bbr2), to_slab_b(bbi2)], axis=2).astype(BF16)
    to_slab_c = lambda c3: _blockdiag(c3.reshape(nq, GROUPS_PER_SLAB, h, p).transpose(0, 1, 3, 2))
    cq = jnp.concatenate([to_slab_c(ssm_c_re), to_slab_c(-ssm_c_im)], axis=1).astype(BF16)

    g_pre2, g_post2 = norm_pre_g.reshape(1, d), norm_post_g.reshape(1, d)
    conv_b2, b_glu2, d_skip2 = conv_b.reshape(1, dc), b_glu.reshape(1, dssm), ssm_d.reshape(1, dssm)
    u_block = 4 * dc // dssm

    hb = _prenorm(xs, g_pre2 + behind_w_in)
    proj_own = _inproj_own(place, hb, win_srcs[0].reshape(d, n_shard))
    win_own, win_landed = _split_copy_wait(win_send, win_recv, win_srcs, win_lands, win_plan,
                                           [proj_own, bq, cq, w_pack, m_pack, v_pack, *side], "gather_w_in_wait")
    win_g = _gather_complete(win_own[0], win_landed[0], "gather_w_in_complete")
    win_b = win_g.reshape(N_CHIPS, d, n_shard)
    side_plan = _gather_direct_plan(len(side))
    side_sems = 2 * (N_CHIPS - 1) * len(side)
    ag_send, ag_recv, ag_srcs, ag_lands, ag_token = _split_copy_start(
        side, [((N_CHIPS,) + a.shape, BF16) for a in side], side_plan, side_sems, "gather_side_weights_start",
        after=[win_g])
    proj = _inproj_rest(place, hb, win_b, proj_own, ag_token)
    y, cin = _ssm_fwd(proj, u_block, bq, cq, lam, pw, d_skip2)
    side_own, side_all = _split_copy_wait(ag_send, ag_recv, ag_srcs, ag_lands, side_plan, [cin],
                                          "gather_side_weights_wait")
    wout_g, wglu_g = [lax.dynamic_update_index_in_dim(all_, own, chip, 0) for own, all_ in zip(side_own, side_all)]
    wout_b = wout_g.reshape(dc + dssm, d)
    wglu_b = wglu_g.reshape(dssm, dssm)
    mix = _gate_fwd(proj, y, conv_w_full, conv_b2, wglu_b, b_glu2, dc)
    loss_blk, dout, dob, dmix, gg_post = _outproj(mix, wout_b, xs, tgt, g_post2)

    def reduce_start(grads, to_send, tags, group):
        got = _rs_pair_exchange(to_send, "rs_pair_exchange_" + group)
        sums = [_rs_pair_add(place, gt, rt, "rs_pair_add_" + tg) for gt, rt, tg in zip(grads, got, tags)]
        plan = _chip_exchange_plan(len(grads))
        started = _split_copy_start([s16 for s16, _ in sums],
                                    [((N_CHIPS - 1,) + s16.shape[1:], BF16) for s16, _ in sums], plan,
                                    (N_CHIPS - 1) * len(grads), "rs_chip_exchange_" + group + "_start")
        return plan, started, [s32 for _, s32 in sums]

    def reduce_finish(plan, started, sums32, tags, group, after):
        send, recv, srcs, lands, _ = started
        _, landed = _split_copy_wait(send, recv, srcs, lands, plan, after, "rs_chip_exchange_" + group + "_wait")
        mine = [_rs_chip_add(place, s32, qt, "rs_chip_add_" + tg) for s32, qt, tg in zip(sums32, landed, tags)]
        full = _rs_pair_share(mine, "rs_pair_share_" + group)
        return [f.reshape(2 * f.shape[1], f.shape[2]) for f in full]

    gw_out, gw_out16 = _matmul_tn(mix, dob, 1, "grad_w_out")
    dproj, dy, gsmall, gw_glu = _gate_bwd(proj, y, dmix, conv_w_full, conv_b2, wglu_b, b_glu2, dc)
    as_out = lambda a: a.reshape(N_CHIPS, 2, (dc + dssm) // (2 * N_CHIPS), d)
    gw_glu4 = gw_glu.reshape(N_CHIPS, 2, dssm // (2 * N_CHIPS), dssm)
    rs_a = reduce_start([as_out(gw_out), gw_glu4], [as_out(gw_out16), gw_glu4], ["w_out", "w_glu"], "a")
    dproj, gb_dense, gc_dense, glam, gd = _ssm_bwd(proj, u_block, dy, cin, dproj, bq, cq, lam, pw,
                                                   d_skip2 + rs_a[1][4][0:1, 0:1])
    gw_in, gw_in16 = _matmul_tn(hb, dproj, N_CHIPS, "grad_w_in")
    as_in = lambda a: a.reshape(N_CHIPS, 2, d // 2, n_shard)
    rs_b = reduce_start([as_in(gw_in)], [as_in(gw_in16)], ["w_in"], "b")
    gx, gg_pre = _dh_prenorm_bwd(dproj, win_b, xs, dout, g_pre2 + rs_b[1][4][0:1, 0:1])

    gb4 = gb_dense.reshape(nq, SLAB, 2, SLAB_STATES)
    g_bbr2 = _blockdiag_take(gb4[:, :, 0, :], h, p).transpose(0, 1, 3, 2).reshape(g, p * h)
    g_bbi2 = _blockdiag_take(gb4[:, :, 1, :], h, p).transpose(0, 1, 3, 2).reshape(g, p * h)
    gc4 = gc_dense.reshape(nq, SLAB, 2, SLAB_STATES)
    g_c_re = _blockdiag_take(gc4[:, :, 0, :], h, p).reshape(g, h, p)
    g_c_im = -_blockdiag_take(gc4[:, :, 1, :], h, p).reshape(g, h, p)
    g_a_re, g_a_im, g_ld, g_b_re2, g_b_im2 = _zoh_bwd(
        ssm_a_re, ssm_a_im, log_dt2, b_re2, b_im2, expand,
        glam[0].reshape(g, p), glam[1].reshape(g, p), g_bbr2, g_bbi2)

    small_g = {
        "norm_pre_g": gg_pre[0], "conv_w": gsmall[0:3], "conv_b": gsmall[3], "ssm_a_re": g_a_re, "ssm_a_im": g_a_im,
        "ssm_log_dt": g_ld.reshape(g), "ssm_b_re": g_b_re2.reshape(g, p, h), "ssm_b_im": g_b_im2.reshape(g, p, h),
        "ssm_c_re": g_c_re, "ssm_c_im": g_c_im, "ssm_d": gd[0], "b_glu": gsmall[4], "norm_post_g": gg_post[0],
        "loss": loss_blk[0, 0:1],
    }
    shapes = [small_w[nm].shape for nm in small_names]
    g_pack = _allreduce_small(_pack([small_g[nm] for nm in small_names]))
    packs = _adamw_small(g_pack, w_pack, m_pack, v_pack)
    sg, sd, sm, sv = [dict(zip(small_names, _unpack(pk, shapes))) for pk in (g_pack, *packs)]
    loss = sg["loss"][0]

    g_cw = lax.dynamic_slice_in_dim(sg["conv_w"], chip * cw_cols, cw_cols, axis=1)
    pad_cw = lambda a: jnp.zeros((SUBLANES, cw_pad), F32).at[:a.shape[0], :cw_cols].set(a)
    cut_cw = lambda a: a[:conv_w.shape[0], :cw_cols]
    _, d_cw, m_cw, v_cw = [cut_cw(a) for a in _adamw(pad_cw(conv_w), pad_cw(g_cw), pad_cw(m_conv_w), pad_cw(v_conv_w),
                                                     "adamw_conv_w")]

    g_wout, g_wglu = reduce_finish(*rs_a, ["w_out", "w_glu"], "a", [d_cw, packs[0]])
    (g_win,) = reduce_finish(*rs_b, ["w_in"], "b", [g_wout])
    g_win, d_win, m_win, v_win = _adamw(w_in, g_win, m_w_in, v_w_in, "adamw_w_in")
    g_wout, d_wout, m_wout, v_wout = _adamw(w_out, g_wout, m_w_out, v_w_out, "adamw_w_out")
    g_wglu, d_wglu, m_wglu, v_wglu = _adamw(w_glu, g_wglu, m_w_glu, v_w_glu, "adamw_w_glu")

    order = ["norm_pre_g", "w_in", "conv_w", "conv_b", "ssm_a_re", "ssm_a_im", "ssm_log_dt", "ssm_b_re", "ssm_b_im",
             "ssm_c_re", "ssm_c_im", "ssm_d", "w_glu", "b_glu", "w_out", "norm_post_g"]
    grads, deltas, new_m, new_v = dict(sg), dict(sd), dict(sm), dict(sv)
    grads.update(w_in=g_win, w_out=g_wout, w_glu=g_wglu, conv_w=g_cw)
    deltas.update(w_in=d_win, w_out=d_wout, w_glu=d_wglu, conv_w=d_cw)
    new_m.update(w_in=m_win, w_out=m_wout, w_glu=m_wglu, conv_w=m_cw)
    new_v.update(w_in=v_win, w_out=v_wout, w_glu=v_wglu, conv_w=v_cw)
    return (loss, gx[None], *[grads[nm] for nm in order], *[deltas[nm] for nm in order],
            *[new_m[nm] for nm in order], *[new_v[nm] for nm in order])
```

```python
import math

import jax
import jax.numpy as jnp
from jax import lax
from jax.experimental import pallas as pl
from jax.experimental.pallas import tpu as pltpu

F32 = jnp.float32
BF16 = jnp.bfloat16
MESH = pl.DeviceIdType.MESH

EPS = 1e-6
SSM_H = 16
SSM_P = 64
GROUPS_PER_SLAB = 8
SLAB = 128
SLAB_STATES = GROUPS_PER_SLAB * SSM_P
N_CHIPS = 4
N_DEV = 8

ADAM_LR = 0.001
ADAM_B1 = 0.9
ADAM_B2 = 0.999
ADAM_EPS = 1e-08
ADAM_WD = 0.01
ADAM_STEP = 10

MIB = 1024 * 1024
VMEM_CAP = 48 * MIB
SUBLANES = 8

TM_NORM = 512
TM_PROJ = 512
TM_GATE = 256
TM_OUT = 256
TM_DH = 256
T_SCAN = 256
TK_TN = 1024
TM_TN = 1024
TR_ELT = 256


def _cp(vmem_bytes, **kw):
    return pltpu.CompilerParams(vmem_limit_bytes=int(min(VMEM_CAP, max(16 * MIB, vmem_bytes))), **kw)


def _pallas(body, **kw):
    if "grid" not in kw and "grid_spec" not in kw:
        return pl.pallas_call(body, **kw)
    pin = lambda s: pltpu.HBM(s.shape, s.dtype) if isinstance(s, jax.ShapeDtypeStruct) else s
    out_shape = kw.pop("out_shape")
    out_shape = [pin(s) for s in out_shape] if isinstance(out_shape, (list, tuple)) else pin(out_shape)
    call = pl.pallas_call(body, out_shape=out_shape, **kw)

    def run(*args):
        return call(*[pltpu.with_memory_space_constraint(a, pltpu.HBM) if jnp.issubdtype(a.dtype, jnp.floating) else a
                      for a in args])

    return run


def _my_place():
    return lax.axis_index("x"), lax.axis_index("y"), lax.axis_index("c")


def _other_chips(x, y):
    return [(1 - x, y), (x, 1 - y), (1 - x, 1 - y)]


def _silu(z):
    s = jax.nn.sigmoid(z)
    return z * s, s


def _dsilu(z, s):
    return s * (1.0 + z * (1.0 - s))


_GELU_K = math.sqrt(2.0 / math.pi)
_GELU_C = 0.044715


def _gelu(y):
    th = jnp.tanh(_GELU_K * (y + _GELU_C * y * y * y))
    return 0.5 * y * (1.0 + th), th


def _dgelu(y, th):
    return 0.5 * (1.0 + th) + 0.5 * y * (1.0 - th * th) * _GELU_K * (1.0 + 3.0 * _GELU_C * y * y)


def _cast_bf16(w, name, after=None):
    r, c = w.shape
    tr = min(TR_ELT, r)
    extra = [] if after is None else [after]

    def body(w_ref, *rest):
        rest[-1][...] = w_ref[...].astype(BF16)

    return _pallas(
        body, name=name, grid=(r // tr,),
        in_specs=[pl.BlockSpec((tr, c), lambda i: (i, 0))] + [pl.BlockSpec((SUBLANES, SLAB), lambda i: (0, 0))] * len(extra),
        out_specs=pl.BlockSpec((tr, c), lambda i: (i, 0)),
        out_shape=jax.ShapeDtypeStruct((r, c), BF16),
        compiler_params=_cp(12 * tr * c),
    )(w, *extra)


def _prenorm(x, g):
    t, d = x.shape
    tm = min(TM_NORM, t)

    def body(x_ref, g_ref, h_ref):
        xv = x_ref[...]
        r = lax.rsqrt(jnp.mean(xv * xv, axis=-1, keepdims=True) + EPS)
        h_ref[...] = (xv * r * g_ref[...]).astype(BF16)

    return _pallas(
        body, name="prenorm", grid=(t // tm,),
        in_specs=[pl.BlockSpec((tm, d), lambda i: (i, 0)), pl.BlockSpec((1, d), lambda i: (0, 0))],
        out_specs=pl.BlockSpec((tm, d), lambda i: (i, 0)),
        out_shape=jax.ShapeDtypeStruct((t, d), BF16),
        compiler_params=_cp(20 * tm * d),
    )(x, g)


def _inproj_own(place, a, w_own):
    t, k = a.shape
    n = w_own.shape[1]
    tm = min(TM_PROJ, t)

    def body(place_ref, a_ref, b_ref, o_ref):
        del place_ref
        o_ref[...] = jnp.dot(a_ref[...], b_ref[...], preferred_element_type=F32)

    return _pallas(
        body, name="inproj_own",
        grid_spec=pltpu.PrefetchScalarGridSpec(
            num_scalar_prefetch=1, grid=(t // tm,),
            in_specs=[pl.BlockSpec((tm, k), lambda i, p: (i, 0)), pl.BlockSpec((k, n), lambda i, p: (0, 0))],
            out_specs=pl.BlockSpec((tm, n), lambda i, p: (i, p[1]))),
        out_shape=jax.ShapeDtypeStruct((t, N_CHIPS * n), F32),
        compiler_params=_cp(2 * (2 * tm * k + 2 * k * n + 4 * tm * n) + 4 * MIB),
    )(place, a, w_own)


def _inproj_rest(place, a, b, partial, after):
    t, k = a.shape
    nb, _, n = b.shape
    tm = min(TM_PROJ, t)
    shard = lambda s, p: (p[1] + 1 + s) % nb

    def body(place_ref, a_ref, b_ref, after_ref, partial_ref, o_ref):
        del place_ref, after_ref, partial_ref
        o_ref[...] = jnp.dot(a_ref[...], b_ref[...], preferred_element_type=F32)

    return _pallas(
        body, name="inproj_rest",
        grid_spec=pltpu.PrefetchScalarGridSpec(
            num_scalar_prefetch=1, grid=(nb - 1, t // tm),
            in_specs=[pl.BlockSpec((tm, k), lambda s, i, p: (i, 0)),
                      pl.BlockSpec((None, k, n), lambda s, i, p: (shard(s, p), 0, 0)),
                      pl.BlockSpec((SUBLANES, SLAB), lambda s, i, p: (0, 0)),
                      pl.BlockSpec(memory_space=pl.ANY)],
            out_specs=pl.BlockSpec((tm, n), lambda s, i, p: (i, shard(s, p)))),
        out_shape=jax.ShapeDtypeStruct((t, nb * n), F32),
        input_output_aliases={4: 0},
        compiler_params=_cp(2 * (2 * tm * k + 2 * k * n + 4 * tm * n) + 4 * MIB),
    )(place, a, b, after, partial)


def _matmul_tn(a, b, nb, name):
    t, m = a.shape
    n = b.shape[1] // nb
    tk = min(TK_TN, t)
    tma = min(TM_TN, m)

    def body(a_ref, b_ref, o_ref, o16_ref):
        k = pl.program_id(2)

        @pl.when(k == 0)
        def _():
            o_ref[...] = jnp.zeros_like(o_ref)

        o_ref[...] += lax.dot_general(a_ref[...], b_ref[...], (((0,), (0,)), ((), ())), preferred_element_type=F32)

        @pl.when(k == t // tk - 1)
        def _():
            o16_ref[...] = o_ref[...].astype(BF16)

    blk = pl.BlockSpec((None, tma, n), lambda j, i, k: (j, i, 0))
    return _pallas(
        body, name=name, grid=(nb, m // tma, t // tk),
        in_specs=[pl.BlockSpec((tk, tma), lambda j, i, k: (k, i)), pl.BlockSpec((tk, n), lambda j, i, k: (k, j))],
        out_specs=[blk, blk],
        out_shape=[jax.ShapeDtypeStruct((nb, m, n), F32), jax.ShapeDtypeStruct((nb, m, n), BF16)],
        compiler_params=_cp(2 * (2 * tk * tma + 2 * tk * n + 6 * tma * n) + 8 * MIB),
    )(a, b)


def _outproj(mix, w_out, x, tgt, g_post):
    t, dm = mix.shape
    d = w_out.shape[1]
    tm = min(TM_OUT, t)

    rb = 2 * SUBLANES
    assert tm % rb == 0

    def body(mix_ref, w_ref, x_ref, t_ref, g_ref, loss_ref, dout_ref, do_ref, dmix_ref, gg_ref, o_buf):
        @pl.when(pl.program_id(0) == 0)
        def _():
            loss_ref[...] = jnp.zeros_like(loss_ref)
            gg_ref[...] = jnp.zeros_like(gg_ref)

        w = w_ref[...]
        o_buf[...] = jnp.dot(mix_ref[...], w, preferred_element_type=F32)
        g = g_ref[...]
        fold = lambda v: v[0:SUBLANES] + v[SUBLANES:rb]
        sq_err = jnp.zeros((SUBLANES, d), F32)
        gg = jnp.zeros((SUBLANES, d), F32)
        for c in range(tm // rb):
            rs = pl.ds(c * rb, rb)
            o = o_buf[rs, :]
            r = lax.rsqrt(jnp.mean(o * o, axis=-1, keepdims=True) + EPS)
            nh = o * r
            e = x_ref[rs, :] + nh * g - t_ref[rs, :]
            sq_err = sq_err + fold(e * e)
            dout = e * (1.0 / d)
            dout_ref[rs, :] = dout
            gg = gg + fold(dout * nh)
            dn = dout * g
            do = r * (dn - nh * jnp.mean(dn * nh, axis=-1, keepdims=True))
            do_ref[rs, :] = do.astype(BF16)
        loss_ref[...] += jnp.sum(sq_err) * (0.5 / d)
        gg_ref[0:1, :] += jnp.sum(gg, axis=0, keepdims=True)
        dmix_ref[...] = lax.dot_general(do_ref[...], w, (((1,), (1,)), ((), ())), preferred_element_type=F32)

    row = lambda i: (i, 0)
    fixed = lambda i: (0, 0)
    return _pallas(
        body, name="outproj", grid=(t // tm,),
        in_specs=[pl.BlockSpec((tm, dm), row), pl.BlockSpec((dm, d), fixed), pl.BlockSpec((tm, d), row),
                  pl.BlockSpec((tm, d), row), pl.BlockSpec((1, d), fixed)],
        out_specs=[pl.BlockSpec((SUBLANES, SLAB), fixed), pl.BlockSpec((tm, d), row), pl.BlockSpec((tm, d), row),
                   pl.BlockSpec((tm, dm), row), pl.BlockSpec((SUBLANES, d), fixed)],
        out_shape=[jax.ShapeDtypeStruct((SUBLANES, SLAB), F32), jax.ShapeDtypeStruct((t, d), F32),
                   jax.ShapeDtypeStruct((t, d), BF16), jax.ShapeDtypeStruct((t, dm), F32),
                   jax.ShapeDtypeStruct((SUBLANES, d), F32)],
        scratch_shapes=[pltpu.VMEM((tm, d), F32)],
        compiler_params=_cp(2 * (2 * dm * d + tm * (2 * dm + 4 * d * 3 + 2 * d + 4 * dm)) + 4 * tm * d + 12 * MIB),
    )(mix, w_out, x, tgt, g_post)


def _dh_prenorm_bwd(dproj, w_in, x, dout, g_pre):
    t, d = x.shape
    nb, _, n = w_in.shape
    tm = min(TM_DH, t)

    def body(dp_ref, w_ref, x_ref, dout_ref, g_ref, dx_ref, gg_ref):
        @pl.when(pl.program_id(0) == 0)
        def _():
            gg_ref[...] = jnp.zeros_like(gg_ref)

        dh = None
        for k in range(nb):
            part = lax.dot_general(dp_ref[:, k * n:(k + 1) * n], w_ref[k], (((1,), (1,)), ((), ())),
                                   preferred_element_type=F32)
            dh = part if dh is None else dh + part
        xv = x_ref[...]
        r = lax.rsqrt(jnp.mean(xv * xv, axis=-1, keepdims=True) + EPS)
        xh = xv * r
        gg_ref[0:1, :] += jnp.sum(dh * xh, axis=0, keepdims=True)
        dg = dh * g_ref[...]
        dx_ref[...] = dout_ref[...] + r * (dg - xh * jnp.mean(dg * xh, axis=-1, keepdims=True))

    row = lambda i: (i, 0)
    fixed = lambda i: (0, 0)
    w_spec = pl.BlockSpec(w_in.shape, lambda i: (0, 0, 0), pipeline_mode=pl.Buffered(1))
    return _pallas(
        body, name="dh_prenorm_bwd", grid=(t // tm,),
        in_specs=[pl.BlockSpec((tm, nb * n), row), w_spec,
                  pl.BlockSpec((tm, d), row), pl.BlockSpec((tm, d), row), pl.BlockSpec((1, d), fixed)],
        out_specs=[pl.BlockSpec((tm, d), row), pl.BlockSpec((SUBLANES, d), fixed)],
        out_shape=[jax.ShapeDtypeStruct((t, d), F32), jax.ShapeDtypeStruct((SUBLANES, d), F32)],
        compiler_params=_cp(2 * nb * d * n + 2 * (2 * tm * nb * n + 12 * tm * d) + 16 * tm * d + 4 * MIB),
    )(dproj, w_in, x, dout, g_pre)


def _adamw(w, g, m, v, name):
    r, c = w.shape
    tr = min(TR_ELT, r)
    c1 = 1.0 - ADAM_B1 ** ADAM_STEP
    c2 = 1.0 - ADAM_B2 ** ADAM_STEP

    def body(w_ref, g_ref, m_ref, v_ref, go_ref, d_ref, mo_ref, vo_ref):
        gv = g_ref[...]
        go_ref[...] = gv
        mn = ADAM_B1 * m_ref[...] + (1.0 - ADAM_B1) * gv
        vn = ADAM_B2 * v_ref[...] + (1.0 - ADAM_B2) * (gv * gv)
        d_ref[...] = -ADAM_LR * ((mn / c1) / (jnp.sqrt(vn / c2) + ADAM_EPS) + ADAM_WD * w_ref[...])
        mo_ref[...] = mn
        vo_ref[...] = vn

    spec = pl.BlockSpec((tr, c), lambda i: (i, 0))
    sds = jax.ShapeDtypeStruct((r, c), F32)
    return _pallas(
        body, name=name, grid=(r // tr,), in_specs=[spec] * 4, out_specs=[spec] * 4, out_shape=[sds] * 4,
        compiler_params=_cp(2 * 8 * 4 * tr * c + 8 * MIB),
    )(w, g, m, v)


def _adamw_small(g, w, m, v):
    r, c = w.shape
    c1 = 1.0 - ADAM_B1 ** ADAM_STEP
    c2 = 1.0 - ADAM_B2 ** ADAM_STEP

    def body(g_ref, w_ref, m_ref, v_ref, d_ref, mo_ref, vo_ref):
        gv = g_ref[...]
        mn = ADAM_B1 * m_ref[...] + (1.0 - ADAM_B1) * gv
        vn = ADAM_B2 * v_ref[...] + (1.0 - ADAM_B2) * (gv * gv)
        d_ref[...] = -ADAM_LR * ((mn / c1) / (jnp.sqrt(vn / c2) + ADAM_EPS) + ADAM_WD * w_ref[...])
        mo_ref[...] = mn
        vo_ref[...] = vn

    sds = jax.ShapeDtypeStruct((r, c), F32)
    return _pallas(
        body, name="adamw_small", out_shape=[sds] * 3,
        compiler_params=_cp(12 * 4 * r * c + 8 * MIB),
    )(g, w, m, v)


def _allreduce_small(v):
    r, c = v.shape
    assert r % (2 * SUBLANES) == 0
    h = r // 2

    def body(v_ref, out_ref, got_ref, chip_ref, slots_ref, send, recv):
        x, y, cc = _my_place()
        k = 2 * x + y
        sib = (x, y, 1 - cc)

        def rc(s, src, dst, to):
            return pltpu.make_async_remote_copy(src_ref=src, dst_ref=dst, send_sem=send.at[s], recv_sem=recv.at[s],
                                                device_id=to, device_id_type=MESH)

        pair = rc(0, v_ref, got_ref, sib)
        pair.start()
        pair.wait()
        chip_ref[...] = v_ref[...] + got_ref[...]
        mine = pl.ds(pl.multiple_of(cc * h, SUBLANES), h)
        theirs = pl.ds(pl.multiple_of((1 - cc) * h, SUBLANES), h)
        chips = _other_chips(x, y)
        sent = []
        for j, (cx, cy) in enumerate(chips):
            cp = rc(1 + j, chip_ref.at[mine], slots_ref.at[k], (cx, cy, cc))
            cp.start()
            sent.append(cp)
        slots_ref[k] = chip_ref[mine, :]
        for j, (cx, cy) in enumerate(chips):
            rc(1 + j, chip_ref.at[mine], slots_ref.at[2 * cx + cy], (cx, cy, cc)).wait_recv()
        total = slots_ref[0]
        for kk in range(1, N_CHIPS):
            total = total + slots_ref[kk]
        out_ref[mine, :] = total
        for cp in sent:
            cp.wait_send()
        share = rc(N_CHIPS, out_ref.at[mine], out_ref.at[mine], sib)
        share.start()
        rc(N_CHIPS, out_ref.at[theirs], out_ref.at[theirs], sib).wait_recv()
        share.wait_send()

    vm = pl.BlockSpec(memory_space=pltpu.VMEM)
    return _pallas(
        body, name="allreduce_small", out_shape=jax.ShapeDtypeStruct((r, c), F32), in_specs=[vm], out_specs=vm,
        scratch_shapes=[pltpu.VMEM((r, c), F32), pltpu.VMEM((r, c), F32), pltpu.VMEM((N_CHIPS, h, c), F32),
                        pltpu.SemaphoreType.DMA((N_CHIPS + 1,)), pltpu.SemaphoreType.DMA((N_CHIPS + 1,))],
        compiler_params=_cp(6 * 4 * r * c + 8 * MIB),
    )(v)


def _zoh(a_re, a_im, log_dt, b_re2, b_im2, expand):
    dt = jnp.exp(log_dt)
    mag = jnp.exp(a_re * dt)
    lbr, lbi = mag * jnp.cos(a_im * dt), mag * jnp.sin(a_im * dt)
    nr, ni = lbr - 1.0, lbi
    den = a_re * a_re + a_im * a_im
    qr = (nr * a_re + ni * a_im) / den
    qi = (ni * a_re - nr * a_im) / den
    qr2 = jnp.dot(qr, expand, precision=lax.Precision.HIGHEST, preferred_element_type=F32)
    qi2 = jnp.dot(qi, expand, precision=lax.Precision.HIGHEST, preferred_element_type=F32)
    return lbr, lbi, qr2 * b_re2 - qi2 * b_im2, qr2 * b_im2 + qi2 * b_re2


def _zoh_fwd(a_re, a_im, log_dt, b_re2, b_im2, expand, power):
    g, p = a_re.shape

    def body(ar_ref, ai_ref, ld_ref, br_ref, bi_ref, e_ref, lbr_ref, lbi_ref, bbr_ref, bbi_ref, pw_ref):
        ar, ai, ld = ar_ref[...], ai_ref[...], ld_ref[...]
        lbr, lbi, bbr, bbi = _zoh(ar, ai, ld, br_ref[...], bi_ref[...], e_ref[...])
        lbr_ref[...], lbi_ref[...], bbr_ref[...], bbi_ref[...] = lbr, lbi, bbr, bbi
        dt = jnp.exp(ld) * float(power)
        mag = jnp.exp(ar * dt)
        pw_ref[0] = mag * jnp.cos(ai * dt)
        pw_ref[1] = mag * jnp.sin(ai * dt)

    gp = jax.ShapeDtypeStruct((g, p), F32)
    gph = jax.ShapeDtypeStruct(b_re2.shape, F32)
    return _pallas(
        body, name="zoh_fwd", out_shape=[gp, gp, gph, gph, jax.ShapeDtypeStruct((2, g, p), F32)],
        compiler_params=_cp(16 * MIB),
    )(a_re, a_im, log_dt, b_re2, b_im2, expand)


def _zoh_bwd(a_re, a_im, log_dt, b_re2, b_im2, expand, g_lbr, g_lbi, g_bbr, g_bbi):
    def body(ar_ref, ai_ref, ld_ref, br_ref, bi_ref, e_ref, c0, c1, c2, c3, gar, gai, gld, gbr, gbi):
        e = e_ref[...]
        _, vjp = jax.vjp(lambda a, b, c, d, f: _zoh(a, b, c, d, f, e),
                         ar_ref[...], ai_ref[...], ld_ref[...], br_ref[...], bi_ref[...])
        gar[...], gai[...], gld[...], gbr[...], gbi[...] = vjp((c0[...], c1[...], c2[...], c3[...]))

    sds = lambda a: jax.ShapeDtypeStruct(a.shape, F32)
    return _pallas(
        body, name="zoh_bwd", out_shape=[sds(a_re), sds(a_im), sds(log_dt), sds(b_re2), sds(b_im2)],
        compiler_params=_cp(16 * MIB),
    )(a_re, a_im, log_dt, b_re2, b_im2, expand, g_lbr, g_lbi, g_bbr, g_bbi)


def _blockdiag(blocks):
    nq, _, r, c = blocks.shape
    eye = jnp.eye(GROUPS_PER_SLAB, dtype=blocks.dtype)
    out = blocks[:, :, :, None, :] * eye[None, :, None, :, None]
    return out.reshape(nq, GROUPS_PER_SLAB * r, GROUPS_PER_SLAB * c)


def _blockdiag_take(dense, r, c):
    nq = dense.shape[0]
    d5 = dense.reshape(nq, GROUPS_PER_SLAB, r, GROUPS_PER_SLAB, c)
    return jnp.stack([d5[:, i, :, i, :] for i in range(GROUPS_PER_SLAB)], axis=1)


def _scan_slab(s_ref, row0, q, lam_ref, pw_ref, car_ref, steps, reverse, prev_ref=None, prev_row0=0, glam_ref=None):
    sign = -1.0 if reverse else 1.0
    half = SLAB_STATES // SLAB
    cols = [(q * 2 * half + m, q * 2 * half + half + m, q * SLAB_STATES + m * SLAB) for m in range(half)]
    nm = len(cols)
    full = (SUBLANES, SLAB)
    lam = [(jnp.broadcast_to(lam_ref[0:1, pl.ds(cl, SLAB)], full),
            jnp.broadcast_to(sign * lam_ref[1:2, pl.ds(cl, SLAB)], full)) for (_, _, cl) in cols]

    def step_rows(jj, base):
        j = (steps - 1 - jj) if reverse else jj
        return j, pl.ds(base + j * SUBLANES, SUBLANES)

    def pass1(jj, car):
        _, rows = step_rows(jj, row0)
        out = []
        for m, (cr, ci, _) in enumerate(cols):
            sr, si = car[2 * m], car[2 * m + 1]
            lr, li = lam[m]
            nr = lr * sr - li * si + s_ref[cr, rows, :]
            ni = lr * si + li * sr + s_ref[ci, rows, :]
            s_ref[cr, rows, :] = nr
            s_ref[ci, rows, :] = ni
            out += [nr, ni]
        return tuple(out)

    def run_steps(step_fn, carry):
        for jj in range(steps):
            carry = step_fn(jj, carry)
        return carry

    ends = run_steps(pass1, tuple(jnp.zeros(full, F32) for _ in range(2 * nm)))

    entry = []
    for m, (cr, ci, cl) in enumerate(cols):
        ljr = pw_ref[0:1, pl.ds(cl, SLAB)]
        lji = sign * pw_ref[1:2, pl.ds(cl, SLAB)]
        c_r = car_ref[cr, 0:1, :]
        c_i = car_ref[ci, 0:1, :]
        rows_r, rows_i = [None] * SUBLANES, [None] * SUBLANES
        order = range(SUBLANES - 1, -1, -1) if reverse else range(SUBLANES)
        for b in order:
            rows_r[b], rows_i[b] = c_r, c_i
            e_r, e_i = ends[2 * m][b:b + 1], ends[2 * m + 1][b:b + 1]
            c_r, c_i = ljr * c_r - lji * c_i + e_r, ljr * c_i + lji * c_r + e_i
        car_ref[cr, 0:1, :] = c_r
        car_ref[ci, 0:1, :] = c_i
        entry.append((jnp.concatenate(rows_r, axis=0), jnp.concatenate(rows_i, axis=0)))

    def pass2(jj, carry):
        j, rows = step_rows(jj, row0)
        decayed, acc = carry[:2 * nm], carry[2 * nm:]
        out_d, out_a = [], []
        for m, (cr, ci, cl) in enumerate(cols):
            lr, li = lam[m]
            dr, di = decayed[2 * m], decayed[2 * m + 1]
            dr, di = lr * dr - li * di, lr * di + li * dr
            nr = s_ref[cr, rows, :] + dr
            ni = s_ref[ci, rows, :] + di
            s_ref[cr, rows, :] = nr
            s_ref[ci, rows, :] = ni
            out_d += [dr, di]
            if prev_ref is not None:
                prow = pl.ds(prev_row0 + (j - 1) * SUBLANES, SUBLANES)
                qr = prev_ref[cr, prow, :]
                qi = prev_ref[ci, prow, :]
                out_a += [acc[2 * m] + (nr * qr + ni * qi), acc[2 * m + 1] + (ni * qr - nr * qi)]
        return tuple(out_d) + tuple(out_a)

    n_acc = 2 * nm if prev_ref is not None else 0
    init = tuple(e for pair in entry for e in pair) + tuple(jnp.zeros(full, F32) for _ in range(n_acc))
    accs = run_steps(pass2, init)[2 * nm:]
    if prev_ref is not None:
        for m, (_, _, cl) in enumerate(cols):
            glam_ref[0:1, pl.ds(cl, SLAB)] += jnp.sum(accs[2 * m], axis=0, keepdims=True)
            glam_ref[1:2, pl.ds(cl, SLAB)] += jnp.sum(accs[2 * m + 1], axis=0, keepdims=True)


def _permute_rows_f32(perm_bf16, v):
    hi = v.astype(BF16)
    lo = (v - hi.astype(F32)).astype(BF16)
    return (jnp.dot(perm_bf16, hi, preferred_element_type=F32) + jnp.dot(perm_bf16, lo, preferred_element_type=F32))


def _put_slab(s_ref, rows, q, val):
    per = 2 * SLAB_STATES // SLAB
    for i in range(per):
        s_ref[q * per + i, rows, :] = val[:, i * SLAB:(i + 1) * SLAB]


def _get_slab(s_ref, rows, q):
    per = 2 * SLAB_STATES // SLAB
    return jnp.concatenate([s_ref[q * per + i, rows, :] for i in range(per)], axis=1)


def _step_major_perm(tt):
    r = jnp.arange(tt)
    held = (r % SUBLANES) * (tt // SUBLANES) + r // SUBLANES
    return held[:, None] == r[None, :]


def _ssm_fwd(proj, u_block, bq, cq, lam, pw, d_skip):
    t = proj.shape[0]
    nq, ds, w2 = bq.shape
    assert ds == SLAB and w2 == 2 * SLAB_STATES
    dssm = nq * SLAB
    width = nq * w2
    ntile = width // SLAB
    tt = min(T_SCAN, t)
    steps = tt // SUBLANES
    perm = _step_major_perm(tt)
    pm, pmt = perm.astype(BF16), perm.T.astype(BF16)

    def body(u_ref, pm_ref, pmt_ref, bq_ref, cq_ref, lam_ref, pw_ref, d_ref, y_ref, cin_ref, s_ref, car_ref):
        @pl.when(pl.program_id(0) == 0)
        def _():
            car_ref[...] = jnp.zeros_like(car_ref)

        cin_ref[...] = jnp.broadcast_to(car_ref[:, 0:1, :], cin_ref.shape)
        u = u_ref[...]
        ub = jnp.dot(pm_ref[...], u.astype(BF16), preferred_element_type=F32).astype(BF16)
        everything = slice(None)
        for q in range(nq):
            _put_slab(s_ref, everything, q,
                      jnp.dot(ub[:, q * SLAB:(q + 1) * SLAB], bq_ref[q], preferred_element_type=F32))
        for q in range(nq):
            _scan_slab(s_ref, 0, q, lam_ref, pw_ref, car_ref, steps, reverse=False)
        y_sm = jnp.concatenate(
            [jnp.dot(_get_slab(s_ref, everything, q).astype(BF16), cq_ref[q], preferred_element_type=F32)
             for q in range(nq)], axis=1)
        y_ref[...] = _permute_rows_f32(pmt_ref[...], y_sm) + d_ref[...] * u

    c3 = lambda i: (0, 0, 0)
    c2 = lambda i: (0, 0)
    return _pallas(
        body, name="ssm_fwd", grid=(t // tt,),
        in_specs=[pl.BlockSpec((tt, dssm), lambda i: (i, u_block)), pl.BlockSpec((tt, tt), c2),
                  pl.BlockSpec((tt, tt), c2), pl.BlockSpec(bq.shape, c3),
                  pl.BlockSpec(cq.shape, c3), pl.BlockSpec(lam.shape, c2), pl.BlockSpec(pw.shape, c2),
                  pl.BlockSpec((1, dssm), c2)],
        out_specs=[pl.BlockSpec((tt, dssm), lambda i: (i, 0)),
                   pl.BlockSpec((None, ntile, SUBLANES, SLAB), lambda i: (i, 0, 0, 0))],
        out_shape=[jax.ShapeDtypeStruct((t, dssm), F32), jax.ShapeDtypeStruct((t // tt, ntile, SUBLANES, SLAB), F32)],
        scratch_shapes=[pltpu.VMEM((ntile, tt, SLAB), F32), pltpu.VMEM((ntile, SUBLANES, SLAB), F32)],
        compiler_params=_cp(2 * (8 * tt * dssm + 4 * nq * ds * w2) + 4 * tt * width + 16 * MIB,
                            dimension_semantics=("arbitrary",)),
    )(proj, pm, pmt, bq, cq, lam, pw, d_skip)


def _ssm_bwd(proj, u_block, dy, cin, dproj, bq, cq, lam, pw, d_skip):
    t = proj.shape[0]
    nq, ds, w2 = bq.shape
    dssm = nq * SLAB
    width = nq * w2
    ntile = width // SLAB
    tt = min(T_SCAN, t)
    nt = t // tt
    steps = tt // SUBLANES
    halo = SUBLANES
    perm = _step_major_perm(tt)
    pm, pmt = perm.astype(BF16), perm.T.astype(BF16)

    def body(u_ref, dy_ref, cin_ref, dp_any, pm_ref, pmt_ref, bq_ref, cq_ref, lam_ref, pw_ref, d_ref,
             du_ref, gb_ref, gc_ref, glam_ref, gd_ref, s_ref, gs_ref, car_f, car_b):
        del dp_any

        @pl.when(pl.program_id(0) == 0)
        def _():
            car_b[...] = jnp.zeros_like(car_b)
            gb_ref[...] = jnp.zeros_like(gb_ref)
            gc_ref[...] = jnp.zeros_like(gc_ref)
            glam_ref[...] = jnp.zeros_like(glam_ref)
            gd_ref[...] = jnp.zeros_like(gd_ref)

        u = u_ref[...]
        dyv = dy_ref[...]
        gd_ref[0:1, :] += jnp.sum(dyv * u, axis=0, keepdims=True)
        pmv = pm_ref[...]
        ub = jnp.dot(pmv, u.astype(BF16), preferred_element_type=F32).astype(BF16)
        dyb = jnp.dot(pmv, dyv.astype(BF16), preferred_element_type=F32).astype(BF16)
        car_f[...] = cin_ref[...]
        data = slice(halo, halo + tt)
        everything = slice(None)
        for q in range(nq):
            _put_slab(s_ref, data, q, jnp.dot(ub[:, q * SLAB:(q + 1) * SLAB], bq_ref[q], preferred_element_type=F32))
        for q in range(nq):
            _scan_slab(s_ref, halo, q, lam_ref, pw_ref, car_f, steps, reverse=False)
        last_step = s_ref[:, halo + tt - SUBLANES:halo + tt, :]
        s_ref[:, 0:halo, :] = jnp.concatenate([cin_ref[:, 0:1, :], last_step[:, 0:SUBLANES - 1, :]], axis=1)
        tn = (((0,), (0,)), ((), ()))
        nt_dims = (((1,), (1,)), ((), ()))
        for q in range(nq):
            sl = slice(q * SLAB, (q + 1) * SLAB)
            gc_ref[q] += lax.dot_general(dyb[:, sl], _get_slab(s_ref, data, q).astype(BF16), tn,
                                         preferred_element_type=F32)
            _put_slab(gs_ref, everything, q,
                      lax.dot_general(dyb[:, sl], cq_ref[q], nt_dims, preferred_element_type=F32))
        for q in range(nq):
            _scan_slab(gs_ref, 0, q, lam_ref, pw_ref, car_b, steps, reverse=True,
                       prev_ref=s_ref, prev_row0=halo, glam_ref=glam_ref)
        du_parts = []
        for q in range(nq):
            sl = slice(q * SLAB, (q + 1) * SLAB)
            gsb = _get_slab(gs_ref, everything, q).astype(BF16)
            du_parts.append(lax.dot_general(gsb, bq_ref[q], nt_dims, preferred_element_type=F32))
            gb_ref[q] += lax.dot_general(ub[:, sl], gsb, tn, preferred_element_type=F32)
        du_sm = jnp.concatenate(du_parts, axis=1)
        du_ref[...] = (_permute_rows_f32(pmt_ref[...], du_sm) + dyv * d_ref[...]).astype(BF16)

    c3 = lambda i: (0, 0, 0)
    c2 = lambda i: (0, 0)
    rev = lambda i: (nt - 1 - i, 0)
    dense = jax.ShapeDtypeStruct((nq, SLAB, w2), F32)
    gp = lam.shape[1]
    return _pallas(
        body, name="ssm_bwd", grid=(nt,),
        in_specs=[pl.BlockSpec((tt, dssm), lambda i: (nt - 1 - i, u_block)), pl.BlockSpec((tt, dssm), rev),
                  pl.BlockSpec((None, ntile, SUBLANES, SLAB), lambda i: (nt - 1 - i, 0, 0, 0)),
                  pl.BlockSpec(memory_space=pl.ANY), pl.BlockSpec((tt, tt), c2), pl.BlockSpec((tt, tt), c2),
                  pl.BlockSpec(bq.shape, c3), pl.BlockSpec(cq.shape, c3),
                  pl.BlockSpec(lam.shape, c2), pl.BlockSpec(pw.shape, c2), pl.BlockSpec((1, dssm), c2)],
        out_specs=[pl.BlockSpec((tt, dssm), lambda i: (nt - 1 - i, u_block)), pl.BlockSpec(dense.shape, c3),
                   pl.BlockSpec(dense.shape, c3), pl.BlockSpec((SUBLANES, gp), c2), pl.BlockSpec((SUBLANES, dssm), c2)],
        out_shape=[jax.ShapeDtypeStruct(dproj.shape, dproj.dtype), dense, dense,
                   jax.ShapeDtypeStruct((SUBLANES, gp), F32), jax.ShapeDtypeStruct((SUBLANES, dssm), F32)],
        scratch_shapes=[pltpu.VMEM((ntile, tt + halo, SLAB), F32), pltpu.VMEM((ntile, tt, SLAB), F32),
                        pltpu.VMEM((ntile, SUBLANES, SLAB), F32), pltpu.VMEM((ntile, SUBLANES, SLAB), F32)],
        input_output_aliases={3: 0},
        compiler_params=_cp(2 * (10 * tt * dssm + 4 * nq * ds * w2 + 8 * nq * SLAB * w2)
                            + 8 * tt * width + 12 * MIB, dimension_semantics=("arbitrary",)),
    )(proj, dy, cin, dproj, pm, pmt, bq, cq, lam, pw, d_skip)


def _gate_fwd(proj, y, conv_w, conv_b, w_glu, b_glu, dc):
    t = proj.shape[0]
    dssm = y.shape[1]
    assert dc == dssm
    tm = min(TM_GATE, t)
    halo = SUBLANES

    rb = 2 * SUBLANES
    assert tm % rb == 0

    def body(b_ref, c_ref, v_ref, zc_ref, zs_ref, y_ref, cw_ref, cb_ref, wg_ref, bg_ref, mix_ref,
             cv_buf, sh1_buf, sh2_buf, ge_buf, geb_buf, gl_buf):
        @pl.when(pl.program_id(0) == 0)
        def _():
            cv_buf[0:halo, :] = jnp.zeros((halo, dc), F32)

        cv_buf[halo:, :] = c_ref[...] * v_ref[...]
        sh1_buf[...] = cv_buf[halo - 1:halo - 1 + tm, :]
        sh2_buf[...] = cv_buf[halo - 2:halo - 2 + tm, :]
        w0, w1, w2, cb = cw_ref[0:1, :], cw_ref[1:2, :], cw_ref[2:3, :], cb_ref[...]
        chunks = [pl.ds(r * rb, rb) for r in range(tm // rb)]
        for r, rs in enumerate(chunks):
            ge, _ = _gelu(y_ref[rs, :])
            ge_buf[rs, :] = ge
            geb_buf[rs, :] = ge.astype(BF16)
        gl_buf[...] = jnp.dot(geb_buf[...], wg_ref[...], preferred_element_type=F32) + bg_ref[...]
        for r, rs in enumerate(chunks):
            conv = cb + w2 * cv_buf[pl.ds(halo + r * rb, rb), :] + w1 * sh1_buf[rs, :] + w0 * sh2_buf[rs, :]
            sz, _ = _silu(zc_ref[rs, :])
            mix_ref[rs, 0:dc] = (b_ref[rs, :] * conv * sz).astype(BF16)
        cv_buf[0:halo, :] = cv_buf[tm:tm + halo, :]
        for r, rs in enumerate(chunks):
            szs, _ = _silu(zs_ref[rs, :])
            mix_ref[rs, dc:] = (ge_buf[rs, :] * jax.nn.sigmoid(gl_buf[rs, :]) * szs).astype(BF16)

    col = lambda j: pl.BlockSpec((tm, dc), lambda i, j=j: (i, j))
    fixed = lambda i: (0, 0)
    return _pallas(
        body, name="gate_fwd", grid=(t // tm,),
        in_specs=[col(0), col(1), col(2), col(3), col(5), pl.BlockSpec((tm, dssm), lambda i: (i, 0)),
                  pl.BlockSpec(conv_w.shape, fixed), pl.BlockSpec((1, dc), fixed),
                  pl.BlockSpec(w_glu.shape, fixed), pl.BlockSpec((1, dssm), fixed)],
        out_specs=pl.BlockSpec((tm, dc + dssm), lambda i: (i, 0)),
        out_shape=jax.ShapeDtypeStruct((t, dc + dssm), BF16),
        scratch_shapes=[pltpu.VMEM((tm + halo, dc), F32), pltpu.VMEM((tm, dc), F32), pltpu.VMEM((tm, dc), F32),
                        pltpu.VMEM((tm, dssm), F32), pltpu.VMEM((tm, dssm), BF16), pltpu.VMEM((tm, dssm), F32)],
        compiler_params=_cp(2 * (6 * 4 * tm * dc + 2 * tm * (dc + dssm) + 2 * dssm * dssm) + 24 * tm * dc + 8 * MIB,
                            dimension_semantics=("arbitrary",)),
    )(proj, proj, proj, proj, proj, y, conv_w, conv_b, w_glu, b_glu)


def _gate_bwd(proj, y, dmix, conv_w, conv_b, w_glu, b_glu, dc):
    t = proj.shape[0]
    dssm = y.shape[1]
    tm = min(TM_GATE, t)
    nt = t // tm
    halo = SUBLANES
    blocks_per_tile = tm // halo

    rb = 2 * SUBLANES
    assert tm % rb == 0
    n_chunk = tm // rb

    def body(b_ref, c_ref, v_ref, zc_ref, zs_ref, cp_ref, vp_ref, y_ref, dm_ref, cw_ref, cb_ref, wg_ref, bg_ref,
             dp_ref, dy_ref, gs_ref, gwg_ref, cv_buf, dc_buf, sh1_buf, sh2_buf, geb_buf, dglb_buf, ge_buf, th_buf):
        i = pl.program_id(0)

        @pl.when(i == 0)
        def _():
            dc_buf[tm:, :] = jnp.zeros((halo, dc), F32)
            gs_ref[...] = jnp.zeros_like(gs_ref)
            gwg_ref[...] = jnp.zeros_like(gwg_ref)

        def rows(r):
            return pl.ds(r * rb, rb)

        def rows_after_halo(r):
            return pl.ds(halo + r * rb, rb)

        def chunks(step, carry):
            for r in range(n_chunk):
                carry = step(r, carry)
            return carry

        def fold(v):
            return v[0:SUBLANES] + v[SUBLANES:rb]

        first_tile = (i == nt - 1)
        cv_buf[0:halo, :] = jnp.where(first_tile, 0.0, cp_ref[...] * vp_ref[...])
        cv_buf[halo:, :] = c_ref[...] * v_ref[...]
        sh1_buf[...] = cv_buf[halo - 1:halo - 1 + tm, :]
        sh2_buf[...] = cv_buf[halo - 2:halo - 2 + tm, :]
        w0, w1, w2, cb = cw_ref[0:1, :], cw_ref[1:2, :], cw_ref[2:3, :], cb_ref[...]

        def conv_pass_a(r, carry):
            rs = rows(r)
            bv, zc, dyc = b_ref[rs, :], zc_ref[rs, :], dm_ref[rs, 0:dc]
            conv = cb + w2 * cv_buf[rows_after_halo(r), :] + w1 * sh1_buf[rs, :] + w0 * sh2_buf[rs, :]
            sz, sgc = _silu(zc)
            dp_ref[rs, 0:dc] = (dyc * conv * sz).astype(BF16)
            dp_ref[rs, 3 * dc:4 * dc] = (dyc * bv * conv * _dsilu(zc, sgc)).astype(BF16)
            dc_buf[rs, :] = dyc * bv * sz
            return carry

        chunks(conv_pass_a, 0)
        sh1_buf[...] = dc_buf[1:1 + tm, :]
        sh2_buf[...] = dc_buf[2:2 + tm, :]

        def conv_pass_b(r, acc):
            rs = rows(r)
            dconv, d1, d2 = dc_buf[rs, :], sh1_buf[rs, :], sh2_buf[rs, :]
            cv = cv_buf[rows_after_halo(r), :]
            dcv = w2 * dconv + w1 * d1 + w0 * d2
            dp_ref[rs, dc:2 * dc] = (dcv * v_ref[rs, :]).astype(BF16)
            dp_ref[rs, 2 * dc:3 * dc] = (dcv * c_ref[rs, :]).astype(BF16)
            return (acc[0] + fold(cv * d2), acc[1] + fold(cv * d1), acc[2] + fold(cv * dconv), acc[3] + fold(dconv))

        zero = jnp.zeros((SUBLANES, dc), F32)
        sums = chunks(conv_pass_b, (zero, zero, zero, zero))
        for k in range(4):
            gs_ref[k:k + 1, :] += jnp.sum(sums[k], axis=0, keepdims=True)
        dc_buf[tm:, :] = dc_buf[0:halo, :]

        def glu_pass_a(r, carry):
            rs = rows(r)
            ge, th = _gelu(y_ref[rs, :])
            geb_buf[rs, :] = ge.astype(BF16)
            ge_buf[rs, :] = ge
            th_buf[rs, :] = th
            return carry

        chunks(glu_pass_a, 0)
        wg = wg_ref[...]
        gl_buf = cv_buf.at[halo:halo + tm]
        keep_buf = dc_buf.at[0:tm]
        gl_buf[...] = jnp.dot(geb_buf[...], wg, preferred_element_type=F32) + bg_ref[...]
        dp_ref[:, 4 * dc:5 * dc] = jnp.zeros((tm, dc), BF16)

        def glu_pass_b(r, acc):
            rs = rows(r)
            ge = ge_buf[rs, :]
            sg = jax.nn.sigmoid(gl_buf[rs, :])
            zs, dys = zs_ref[rs, :], dm_ref[rs, dc:]
            szs, sgs = _silu(zs)
            dp_ref[rs, 5 * dc:] = (dys * ge * sg * _dsilu(zs, sgs)).astype(BF16)
            d_ys = dys * szs
            dgl = d_ys * ge * sg * (1.0 - sg)
            dglb_buf[rs, :] = dgl.astype(BF16)
            keep_buf[rs, :] = d_ys * sg
            return acc + fold(dgl)

        gs_ref[4:5, :] += jnp.sum(chunks(glu_pass_b, zero), axis=0, keepdims=True)
        dglb = dglb_buf[...]
        gwg_ref[...] += lax.dot_general(geb_buf[...], dglb, (((0,), (0,)), ((), ())), preferred_element_type=F32)
        gl_buf[...] = lax.dot_general(dglb, wg, (((1,), (1,)), ((), ())), preferred_element_type=F32)

        def glu_pass_c(r, carry):
            rs = rows(r)
            dy_ref[rs, :] = (keep_buf[rs, :] + gl_buf[rs, :]) * _dgelu(y_ref[rs, :], th_buf[rs, :])
            return carry

        chunks(glu_pass_c, 0)

    col = lambda j: pl.BlockSpec((tm, dc), lambda i, j=j: (nt - 1 - i, j))
    prev = lambda j: pl.BlockSpec((halo, dc), lambda i, j=j: (jnp.maximum((nt - 1 - i) * blocks_per_tile - 1, 0), j))
    rev = lambda i: (nt - 1 - i, 0)
    fixed = lambda i: (0, 0)
    return _pallas(
        body, name="gate_bwd", grid=(nt,),
        in_specs=[col(0), col(1), col(2), col(3), col(5), prev(1), prev(2), pl.BlockSpec((tm, dssm), rev),
                  pl.BlockSpec((tm, dc + dssm), rev), pl.BlockSpec(conv_w.shape, fixed), pl.BlockSpec((1, dc), fixed),
                  pl.BlockSpec(w_glu.shape, fixed), pl.BlockSpec((1, dssm), fixed)],
        out_specs=[pl.BlockSpec((tm, 6 * dc), rev), pl.BlockSpec((tm, dssm), rev),
                   pl.BlockSpec((2 * SUBLANES, dc), fixed), pl.BlockSpec((dssm, dssm), fixed)],
        out_shape=[jax.ShapeDtypeStruct((t, 6 * dc), BF16), jax.ShapeDtypeStruct((t, dssm), F32),
                   jax.ShapeDtypeStruct((2 * SUBLANES, dc), F32), jax.ShapeDtypeStruct((dssm, dssm), F32)],
        scratch_shapes=[pltpu.VMEM((tm + halo, dc), F32), pltpu.VMEM((tm + halo, dc), F32), pltpu.VMEM((tm, dc), F32),
                        pltpu.VMEM((tm, dc), F32), pltpu.VMEM((tm, dssm), BF16), pltpu.VMEM((tm, dssm), BF16),
                        pltpu.VMEM((tm, dssm), F32), pltpu.VMEM((tm, dssm), F32)],
        compiler_params=_cp(2 * (6 * 4 * tm * dc + 8 * tm * dc + 12 * tm * dc + 4 * tm * dc + 6 * dssm * dssm)
                            + 28 * tm * dc + 8 * MIB, dimension_semantics=("arbitrary",)),
    )(proj, proj, proj, proj, proj, proj, proj, y, dmix, conv_w, conv_b, w_glu, b_glu)


def _allgather_flat(v, name):
    r, c = v.shape
    rels = [(dx, dy, dc) for dx in (0, 1) for dy in (0, 1) for dc in (0, 1)][1:]

    def body(v_ref, out_ref, send, recv):
        x, y, cc = _my_place()
        me = 4 * x + 2 * y + cc

        def peer(rel):
            dx, dy, dc = rel
            return (1 - x if dx else x, 1 - y if dy else y, 1 - cc if dc else cc)

        def rc(s, slot, to):
            return pltpu.make_async_remote_copy(src_ref=v_ref, dst_ref=out_ref.at[slot], send_sem=send.at[s],
                                                recv_sem=recv.at[s], device_id=to, device_id_type=MESH)

        sent = []
        for s, rel in enumerate(rels):
            cp = rc(s, me, peer(rel))
            cp.start()
            sent.append(cp)
        out_ref[me] = v_ref[...]
        for s, rel in enumerate(rels):
            px, py, pc = peer(rel)
            rc(s, 4 * px + 2 * py + pc, (px, py, pc)).wait_recv()
        for cp in sent:
            cp.wait_send()

    return _pallas(
        body, name=name, out_shape=jax.ShapeDtypeStruct((N_DEV, r, c), F32),
        in_specs=[pl.BlockSpec(memory_space=pltpu.VMEM)], out_specs=pl.BlockSpec(memory_space=pltpu.VMEM),
        scratch_shapes=[pltpu.SemaphoreType.DMA((N_DEV - 1,)), pltpu.SemaphoreType.DMA((N_DEV - 1,))],
        compiler_params=_cp((N_DEV + 2) * 4 * r * c + 8 * MIB),
    )(v)


def _rs_pair_exchange(grads, name):
    n = len(grads)

    def body(*refs):
        ins, outs = refs[:n], refs[n:2 * n]
        send, recv = refs[2 * n:]
        x, y, c = _my_place()
        cps = []
        for t in range(n):
            cp = pltpu.make_async_remote_copy(src_ref=ins[t].at[:, 1 - c], dst_ref=outs[t], send_sem=send.at[t],
                                              recv_sem=recv.at[t], device_id=(x, y, 1 - c), device_id_type=MESH)
            cp.start()
            cps.append(cp)
        for cp in cps:
            cp.wait()

    any_spec = pl.BlockSpec(memory_space=pl.ANY)
    return _pallas(
        body, name=name, in_specs=[any_spec] * n, out_specs=[any_spec] * n,
        out_shape=[jax.ShapeDtypeStruct((g.shape[0],) + g.shape[2:], g.dtype) for g in grads],
        scratch_shapes=[pltpu.SemaphoreType.DMA((n,)), pltpu.SemaphoreType.DMA((n,))],
        compiler_params=_cp(16 * MIB),
    )(*grads)


def _rs_pair_add(place, grad, got, name):
    nk, _, r2, c = grad.shape
    tr = min(TR_ELT, r2)
    other = lambda s, p: (p[1] + 1 + s) % nk

    def send_body(place_ref, g_ref, r_ref, o_ref):
        del place_ref
        o_ref[...] = (g_ref[...] + r_ref[...].astype(F32)).astype(BF16)

    to_send = _pallas(
        send_body, name=name + "_send",
        grid_spec=pltpu.PrefetchScalarGridSpec(
            num_scalar_prefetch=1, grid=(nk - 1, r2 // tr),
            in_specs=[pl.BlockSpec((None, None, tr, c), lambda s, i, p: (other(s, p), p[0], i, 0)),
                      pl.BlockSpec((None, tr, c), lambda s, i, p: (other(s, p), i, 0))],
            out_specs=pl.BlockSpec((None, tr, c), lambda s, i, p: (other(s, p), i, 0))),
        out_shape=jax.ShapeDtypeStruct((nk, r2, c), BF16),
        compiler_params=_cp(2 * 8 * tr * c + 8 * MIB),
    )(place, grad, got)

    def keep_body(place_ref, g_ref, r_ref, o_ref):
        del place_ref
        o_ref[...] = g_ref[...] + r_ref[...].astype(F32)

    to_keep = _pallas(
        keep_body, name=name + "_keep",
        grid_spec=pltpu.PrefetchScalarGridSpec(
            num_scalar_prefetch=1, grid=(r2 // tr,),
            in_specs=[pl.BlockSpec((None, None, tr, c), lambda i, p: (p[1], p[0], i, 0)),
                      pl.BlockSpec((None, tr, c), lambda i, p: (p[1], i, 0))],
            out_specs=pl.BlockSpec((tr, c), lambda i, p: (i, 0))),
        out_shape=jax.ShapeDtypeStruct((r2, c), F32),
        compiler_params=_cp(2 * 10 * tr * c + 8 * MIB),
    )(place, grad, got)
    return to_send, to_keep


_HBM_SPEC = pl.BlockSpec(memory_space=pltpu.HBM)
_SEM_SPEC = pl.BlockSpec(memory_space=pltpu.SEMAPHORE)
_DATAFLOW = pltpu.SideEffectType.DATAFLOW_SIDE_EFFECTING


def _split_copy_start(srcs, land_shapes, plan, n_sems, name, after=()):
    ns, nl, na = len(srcs), len(land_shapes), len(after)

    def body(*refs):
        src_refs, land_refs = refs[:ns], refs[ns:ns + nl]
        send, recv = refs[ns + nl + na], refs[ns + nl + na + 1]
        token = refs[-1]
        sends, _ = plan(src_refs, land_refs)
        for src, dst, to, si, ri in sends:
            pltpu.make_async_remote_copy(src_ref=src, dst_ref=dst, send_sem=send.at[si], recv_sem=recv.at[ri],
                                         device_id=to, device_id_type=MESH).start()
        token[...] = jnp.zeros_like(token)

    lands = [lax.empty(shp, dt) for shp, dt in land_shapes]
    through = [pltpu.HBM(a.shape, a.dtype) for a in srcs] + [pltpu.HBM(shp, dt) for shp, dt in land_shapes]
    out = pl.pallas_call(
        body, name=name,
        out_shape=(pltpu.SemaphoreType.DMA((n_sems,)), pltpu.SemaphoreType.DMA((n_sems,)), *through,
                   jax.ShapeDtypeStruct((SUBLANES, SLAB), F32)),
        in_specs=[_HBM_SPEC] * (ns + nl) + [pl.BlockSpec(memory_space=pl.ANY)] * na,
        out_specs=(_SEM_SPEC, _SEM_SPEC, *([_HBM_SPEC] * (ns + nl)), pl.BlockSpec(memory_space=pltpu.VMEM)),
        input_output_aliases={i: 2 + i for i in range(ns + nl)},
        compiler_params=pltpu.CompilerParams(has_side_effects=_DATAFLOW),
    )(*[pltpu.with_memory_space_constraint(a, pltpu.HBM) for a in (*srcs, *lands)], *after)
    return out[0], out[1], list(out[2:2 + ns]), list(out[2 + ns:2 + ns + nl]), out[-1]


def _split_copy_wait(send, recv, srcs, lands, plan, after, name):
    ns, nl, na = len(srcs), len(lands), len(after)

    def body(*refs):
        src_refs, land_refs = refs[:ns], refs[ns:ns + nl]
        send_ref, recv_ref = refs[ns + nl], refs[ns + nl + 1]
        sends, arrivals = plan(src_refs, land_refs)
        for src, dst, to, si, ri in sends:
            pltpu.make_async_remote_copy(src_ref=src, dst_ref=dst, send_sem=send_ref.at[si], recv_sem=recv_ref.at[ri],
                                         device_id=to, device_id_type=MESH).wait_send()
        for (src, _, to, si, _), (view, ri) in zip(sends, arrivals):
            pltpu.make_async_remote_copy(src_ref=view, dst_ref=view, send_sem=send_ref.at[si], recv_sem=recv_ref.at[ri],
                                         device_id=to, device_id_type=MESH).wait_recv()

    out = pl.pallas_call(
        body, name=name,
        out_shape=[pltpu.HBM(a.shape, a.dtype) for a in (*srcs, *lands)],
        in_specs=[_HBM_SPEC] * (ns + nl) + [_SEM_SPEC, _SEM_SPEC] + [pl.BlockSpec(memory_space=pl.ANY)] * na,
        out_specs=[_HBM_SPEC] * (ns + nl),
        input_output_aliases={i: i for i in range(ns + nl)},
        compiler_params=pltpu.CompilerParams(has_side_effects=_DATAFLOW),
    )(*srcs, *lands, send, recv, *after)
    return list(out[:ns]), list(out[ns:])


def _chip_exchange_plan(n):
    per = N_CHIPS - 1

    def plan(srcs, lands):
        x, y, c = _my_place()
        sends, arrivals = [], []
        for t in range(n):
            for j, (cx, cy) in enumerate(_other_chips(x, y)):
                sends.append((srcs[t].at[2 * cx + cy], lands[t].at[j], (cx, cy, c), t * per + j, t * per + j))
                arrivals.append((lands[t].at[j], t * per + j))
        return sends, arrivals

    return plan


def _gather_half_plan(n):
    per = N_CHIPS - 1

    def plan(srcs, lands):
        x, y, c = _my_place()
        k = 2 * x + y
        sends, arrivals = [], []
        for t in range(n):
            for j, (cx, cy) in enumerate(_other_chips(x, y)):
                sends.append((srcs[t].at[c], lands[t].at[k, c], (cx, cy, c), t * per + j, t * per + j))
                arrivals.append((lands[t].at[2 * cx + cy, c], t * per + j))
        return sends, arrivals

    return plan


def _gather_complete(own, landed, name):
    def body(own_ref, in_ref, out_ref, send, recv, local):
        x, y, c = _my_place()
        sib = (x, y, 1 - c)
        mine = pltpu.make_async_copy(own_ref, out_ref.at[2 * x + y], local)
        mine.start()
        sent = []
        for j, (cx, cy) in enumerate(_other_chips(x, y)):
            half = (2 * cx + cy, c)
            cp = pltpu.make_async_remote_copy(src_ref=in_ref.at[half], dst_ref=out_ref.at[half], send_sem=send.at[j],
                                              recv_sem=recv.at[j], device_id=sib, device_id_type=MESH)
            cp.start()
            sent.append(cp)
        for j, (cx, cy) in enumerate(_other_chips(x, y)):
            other = out_ref.at[2 * cx + cy, 1 - c]
            pltpu.make_async_remote_copy(src_ref=other, dst_ref=other, send_sem=send.at[j], recv_sem=recv.at[j],
                                         device_id=sib, device_id_type=MESH).wait_recv()
        for cp in sent:
            cp.wait_send()
        mine.wait()

    any_spec = pl.BlockSpec(memory_space=pl.ANY)
    per = N_CHIPS - 1
    return _pallas(
        body, name=name, in_specs=[any_spec, any_spec], out_specs=any_spec,
        out_shape=jax.ShapeDtypeStruct(landed.shape, landed.dtype), input_output_aliases={1: 0},
        scratch_shapes=[pltpu.SemaphoreType.DMA((per,)), pltpu.SemaphoreType.DMA((per,)), pltpu.SemaphoreType.DMA],
        compiler_params=_cp(16 * MIB),
    )(own, landed)


def _gather_direct_plan(n):
    per = 2 * (N_CHIPS - 1)

    def plan(srcs, lands):
        x, y, c = _my_place()
        k = 2 * x + y
        sends, arrivals = [], []
        for t in range(n):
            for j, (cx, cy) in enumerate(_other_chips(x, y)):
                for core in (0, 1):
                    sends.append((srcs[t].at[c], lands[t].at[k, c], (cx, cy, core),
                                  t * per + 2 * j + core, t * per + 2 * j + c))
                    arrivals.append((lands[t].at[2 * cx + cy, core], t * per + 2 * j + core))
        return sends, arrivals

    return plan


def _rs_chip_add(kept, got, name):
    r2, c = kept.shape
    tr = min(TR_ELT, r2)

    def body(s_ref, q_ref, o_ref):
        o_ref[...] = ((s_ref[...] + q_ref[0].astype(F32)) + q_ref[1].astype(F32)) + q_ref[2].astype(F32)

    blk = pl.BlockSpec((tr, c), lambda i: (i, 0))
    return _pallas(
        body, name=name, grid=(r2 // tr,),
        in_specs=[blk, pl.BlockSpec((N_CHIPS - 1, tr, c), lambda i: (0, i, 0))], out_specs=blk,
        out_shape=jax.ShapeDtypeStruct((r2, c), F32),
        compiler_params=_cp(2 * 14 * tr * c + 8 * MIB),
    )(kept, got)


def _rs_pair_share(halves, name):
    n = len(halves)

    def body(*refs):
        ins, outs = refs[:n], refs[n:2 * n]
        send, recv, local = refs[2 * n:]
        x, y, c = _my_place()
        cps, mine = [], []
        for t in range(n):
            lc = pltpu.make_async_copy(ins[t], outs[t].at[c], local.at[t])
            lc.start()
            mine.append(lc)
            cp = pltpu.make_async_remote_copy(src_ref=ins[t], dst_ref=outs[t].at[c], send_sem=send.at[t],
                                              recv_sem=recv.at[t], device_id=(x, y, 1 - c), device_id_type=MESH)
            cp.start()
            cps.append(cp)
        for t in range(n):
            other = outs[t].at[1 - c]
            pltpu.make_async_remote_copy(src_ref=other, dst_ref=other, send_sem=send.at[t], recv_sem=recv.at[t],
                                         device_id=(x, y, 1 - c), device_id_type=MESH).wait_recv()
        for cp in cps:
            cp.wait_send()
        for lc in mine:
            lc.wait()

    any_spec = pl.BlockSpec(memory_space=pl.ANY)
    return _pallas(
        body, name=name, in_specs=[any_spec] * n, out_specs=[any_spec] * n,
        out_shape=[jax.ShapeDtypeStruct((2,) + h.shape, F32) for h in halves],
        scratch_shapes=[pltpu.SemaphoreType.DMA((n,)), pltpu.SemaphoreType.DMA((n,)), pltpu.SemaphoreType.DMA((n,))],
        compiler_params=_cp(16 * MIB),
    )(*halves)


_PACK_TILE = SUBLANES * SLAB


def _pack(arrays):
    rows = []
    for a in arrays:
        flat = a.reshape(-1).astype(F32)
        padded = -(-flat.shape[0] // _PACK_TILE) * _PACK_TILE
        rows.append(jnp.pad(flat, (0, padded - flat.shape[0])).reshape(-1, SLAB))
    n_rows = sum(r.shape[0] for r in rows)
    if n_rows % (2 * SUBLANES):
        rows.append(jnp.zeros((SUBLANES, SLAB), F32))
    return jnp.concatenate(rows, axis=0)


def _unpack(packed, shapes):
    out, row = [], 0
    for shp in shapes:
        size = math.prod(shp)
        nrow = -(-size // _PACK_TILE) * SUBLANES
        out.append(packed[row:row + nrow].reshape(-1)[:size].reshape(shp))
        row += nrow
    return out


def kernel(x, norm_pre_g, w_in, conv_w, conv_b, ssm_a_re, ssm_a_im, ssm_log_dt, ssm_b_re, ssm_b_im, ssm_c_re, ssm_c_im, ssm_d, w_glu, b_glu, w_out, norm_post_g, loss_target, m_norm_pre_g, m_w_in, m_conv_w, m_conv_b, m_ssm_a_re, m_ssm_a_im, m_ssm_log_dt, m_ssm_b_re, m_ssm_b_im, m_ssm_c_re, m_ssm_c_im, m_ssm_d, m_w_glu, m_b_glu, m_w_out, m_norm_post_g, v_norm_pre_g, v_w_in, v_conv_w, v_conv_b, v_ssm_a_re, v_ssm_a_im, v_ssm_log_dt, v_ssm_b_re, v_ssm_b_im, v_ssm_c_re, v_ssm_c_im, v_ssm_d, v_w_glu, v_b_glu, v_w_out, v_norm_post_g):
    xs, tgt = x[0], loss_target[0]
    t, d = xs.shape
    dc = conv_b.shape[0]
    dssm = ssm_d.shape[0]
    g, p = ssm_a_re.shape
    h = SSM_H
    nq = dssm // SLAB
    n_shard = w_in.shape[1]
    steps = min(T_SCAN, t) // SUBLANES
    mx, my, mc = _my_place()
    chip = 2 * mx + my
    place = jnp.stack([mc, chip]).astype(jnp.int32)

    cw_cols = conv_w.shape[1]
    cw_pad = -(-cw_cols // SLAB) * SLAB
    cw_blk = jnp.zeros((SUBLANES, cw_pad), F32).at[:conv_w.shape[0], :cw_cols].set(conv_w)
    cw_all = _allgather_flat(cw_blk, "allgather_conv_w")
    conv_w_full = jnp.concatenate([cw_all[2 * k, :, :cw_cols] for k in range(N_CHIPS)], axis=1)

    halves = lambda a: a.reshape(2, a.shape[0] // 2, a.shape[1])
    win_half = halves(_cast_bf16(w_in, "cast_w_in", cw_all[0, :, :SLAB]))
    win_plan = _gather_half_plan(1)
    win_send, win_recv, win_srcs, win_lands, win_token = _split_copy_start(
        [win_half], [((N_CHIPS,) + win_half.shape, BF16)], win_plan, N_CHIPS - 1, "gather_w_in_start")
    behind_w_in = win_token[0:1, 0:1]

    small_names = ["norm_pre_g", "conv_w", "conv_b", "ssm_a_re", "ssm_a_im", "ssm_log_dt", "ssm_b_re", "ssm_b_im",
                   "ssm_c_re", "ssm_c_im", "ssm_d", "b_glu", "norm_post_g", "loss"]
    zeros_cw = jnp.zeros((conv_w.shape[0], dc), F32)
    one0 = jnp.zeros((1,), F32)
    small_w = dict(norm_pre_g=norm_pre_g, conv_w=zeros_cw, conv_b=conv_b, ssm_a_re=ssm_a_re, ssm_a_im=ssm_a_im,
                   ssm_log_dt=ssm_log_dt, ssm_b_re=ssm_b_re, ssm_b_im=ssm_b_im, ssm_c_re=ssm_c_re, ssm_c_im=ssm_c_im,
                   ssm_d=ssm_d, b_glu=b_glu, norm_post_g=norm_post_g, loss=one0)
    small_m = dict(norm_pre_g=m_norm_pre_g, conv_w=zeros_cw, conv_b=m_conv_b, ssm_a_re=m_ssm_a_re, ssm_a_im=m_ssm_a_im,
                   ssm_log_dt=m_ssm_log_dt, ssm_b_re=m_ssm_b_re, ssm_b_im=m_ssm_b_im, ssm_c_re=m_ssm_c_re,
                   ssm_c_im=m_ssm_c_im, ssm_d=m_ssm_d, b_glu=m_b_glu, norm_post_g=m_norm_post_g, loss=one0)
    small_v = dict(norm_pre_g=v_norm_pre_g, conv_w=zeros_cw, conv_b=v_conv_b, ssm_a_re=v_ssm_a_re, ssm_a_im=v_ssm_a_im,
                   ssm_log_dt=v_ssm_log_dt, ssm_b_re=v_ssm_b_re, ssm_b_im=v_ssm_b_im, ssm_c_re=v_ssm_c_re,
                   ssm_c_im=v_ssm_c_im, ssm_d=v_ssm_d, b_glu=v_b_glu, norm_post_g=v_norm_post_g, loss=one0)
    w_pack, m_pack, v_pack = [_pack([group[nm] for nm in small_names]) + behind_w_in
                              for group in (small_w, small_m, small_v)]
    side = [halves(_cast_bf16(w_out, "cast_w_out", win_token)), halves(_cast_bf16(w_glu, "cast_w_glu", win_token))]

    expand = jnp.repeat(jnp.eye(p, dtype=F32), h, axis=1)
    b_re2, b_im2 = ssm_b_re.reshape(g, p * h), ssm_b_im.reshape(g, p * h)
    log_dt2 = ssm_log_dt.reshape(g, 1) + behind_w_in
    lbr, lbi, bbr2, bbi2, pw3 = _zoh_fwd(ssm_a_re, ssm_a_im, log_dt2, b_re2, b_im2, expand, steps)
    lam = jnp.stack([lbr.reshape(g * p), lbi.reshape(g * p)])
    pw = pw3.reshape(2, g * p)
    to_slab_b = lambda b2: _blockdiag(b2.reshape(nq, GROUPS_PER_SLAB, p, h).transpose(0, 1, 3, 2))
    bq = jnp.concatenate([to_slab_b(bbr2), to_slab_b(bbi2)], axis=2).astype(BF16)
    to_slab_c = lambda c3: _blockdiag(c3.reshape(nq, GROUPS_PER_SLAB, h, p).transpose(0, 1, 3, 2))
    cq = jnp.concatenate([to_slab_c(ssm_c_re), to_slab_c(-ssm_c_im)], axis=1).astype(BF16)

    g_pre2, g_post2 = norm_pre_g.reshape(1, d), norm_post_g.reshape(1, d)
    conv_b2, b_glu2, d_skip2 = conv_b.reshape(1, dc), b_glu.reshape(1, dssm), ssm_d.reshape(1, dssm)
    u_block = 4 * dc // dssm

    hb = _prenorm(xs, g_pre2 + behind_w_in)
    proj_own = _inproj_own(place, hb, win_srcs[0].reshape(d, n_shard))
    win_own, win_landed = _split_copy_wait(win_send, win_recv, win_srcs, win_lands, win_plan,
                                           [proj_own, bq, cq, w_pack, m_pack, v_pack, *side], "gather_w_in_wait")
    win_g = _gather_complete(win_own[0], win_landed[0], "gather_w_in_complete")
    win_b = win_g.reshape(N_CHIPS, d, n_shard)
    side_plan = _gather_direct_plan(len(side))
    side_sems = 2 * (N_CHIPS - 1) * len(side)
    ag_send, ag_recv, ag_srcs, ag_lands, ag_token = _split_copy_start(
        side, [((N_CHIPS,) + a.shape, BF16) for a in side], side_plan, side_sems, "gather_side_weights_start",
        after=[win_g])
    proj = _inproj_rest(place, hb, win_b, proj_own, ag_token)
    y, cin = _ssm_fwd(proj, u_block, bq, cq, lam, pw, d_skip2)
    side_own, side_all = _split_copy_wait(ag_send, ag_recv, ag_srcs, ag_lands, side_plan, [cin],
                                          "gather_side_weights_wait")
    wout_g, wglu_g = [lax.dynamic_update_index_in_dim(all_, own, chip, 0) for own, all_ in zip(side_own, side_all)]
    wout_b = wout_g.reshape(dc + dssm, d)
    wglu_b = wglu_g.reshape(dssm, dssm)
    mix = _gate_fwd(proj, y, conv_w_full, conv_b2, wglu_b, b_glu2, dc)
    loss_blk, dout, dob, dmix, gg_post = _outproj(mix, wout_b, xs, tgt, g_post2)

    def reduce_start(grads, to_send, tags, group):
        got = _rs_pair_exchange(to_send, "rs_pair_exchange_" + group)
        sums = [_rs_pair_add(place, gt, rt, "rs_pair_add_" + tg) for gt, rt, tg in zip(grads, got, tags)]
        plan = _chip_exchange_plan(len(grads))
        started = _split_copy_start([s16 for s16, _ in sums],
                                    [((N_CHIPS - 1,) + s16.shape[1:], BF16) for s16, _ in sums], plan,
                                    (N_CHIPS - 1) * len(grads), "rs_chip_exchange_" + group + "_start")
        return plan, started, [s32 for _, s32 in sums]

    def reduce_finish(plan, started, sums32, tags, group, after):
        send, recv, srcs, lands, _ = started
        _, landed = _split_copy_wait(send, recv, srcs, lands, plan, after, "rs_chip_exchange_" + group + "_wait")
        mine = [_rs_chip_add(s32, qt, "rs_chip_add_" + tg) for s32, qt, tg in zip(sums32, landed, tags)]
        full = _rs_pair_share(mine, "rs_pair_share_" + group)
        return [f.reshape(2 * f.shape[1], f.shape[2]) for f in full]

    gw_out, gw_out16 = _matmul_tn(mix, dob, 1, "grad_w_out")
    dproj, dy, gsmall, gw_glu = _gate_bwd(proj, y, dmix, conv_w_full, conv_b2, wglu_b, b_glu2, dc)
    as_out = lambda a: a.reshape(N_CHIPS, 2, (dc + dssm) // (2 * N_CHIPS), d)
    gw_glu4 = gw_glu.reshape(N_CHIPS, 2, dssm // (2 * N_CHIPS), dssm)
    rs_a = reduce_start([as_out(gw_out), gw_glu4], [as_out(gw_out16), gw_glu4], ["w_out", "w_glu"], "a")
    dproj, gb_dense, gc_dense, glam, gd = _ssm_bwd(proj, u_block, dy, cin, dproj, bq, cq, lam, pw,
                                                   d_skip2 + rs_a[1][4][0:1, 0:1])
    gw_in, gw_in16 = _matmul_tn(hb, dproj, N_CHIPS, "grad_w_in")
    as_in = lambda a: a.reshape(N_CHIPS, 2, d // 2, n_shard)
    rs_b = reduce_start([as_in(gw_in)], [as_in(gw_in16)], ["w_in"], "b")
    gx, gg_pre = _dh_prenorm_bwd(dproj, win_b, xs, dout, g_pre2 + rs_b[1][4][0:1, 0:1])

    gb4 = gb_dense.reshape(nq, SLAB, 2, SLAB_STATES)
    g_bbr2 = _blockdiag_take(gb4[:, :, 0, :], h, p).transpose(0, 1, 3, 2).reshape(g, p * h)
    g_bbi2 = _blockdiag_take(gb4[:, :, 1, :], h, p).transpose(0, 1, 3, 2).reshape(g, p * h)
    gc4 = gc_dense.reshape(nq, SLAB, 2, SLAB_STATES)
    g_c_re = _blockdiag_take(gc4[:, :, 0, :], h, p).reshape(g, h, p)
    g_c_im = -_blockdiag_take(gc4[:, :, 1, :], h, p).reshape(g, h, p)
    g_a_re, g_a_im, g_ld, g_b_re2, g_b_im2 = _zoh_bwd(
        ssm_a_re, ssm_a_im, log_dt2, b_re2, b_im2, expand,
        glam[0].reshape(g, p), glam[1].reshape(g, p), g_bbr2, g_bbi2)

    small_g = {
        "norm_pre_g": gg_pre[0], "conv_w": gsmall[0:3], "conv_b": gsmall[3], "ssm_a_re": g_a_re, "ssm_a_im": g_a_im,
        "ssm_log_dt": g_ld.reshape(g), "ssm_b_re": g_b_re2.reshape(g, p, h), "ssm_b_im": g_b_im2.reshape(g, p, h),
        "ssm_c_re": g_c_re, "ssm_c_im": g_c_im, "ssm_d": gd[0], "b_glu": gsmall[4], "norm_post_g": gg_post[0],
        "loss": loss_blk[0, 0:1],
    }
    shapes = [small_w[nm].shape for nm in small_names]
    g_pack = _allreduce_small(_pack([small_g[nm] for nm in small_names]))
    packs = _adamw_small(g_pack, w_pack, m_pack, v_pack)
    sg, sd, sm, sv = [dict(zip(small_names, _unpack(pk, shapes))) for pk in (g_pack, *packs)]
    loss = sg["loss"][0]

    g_cw = lax.dynamic_slice_in_dim(sg["conv_w"], chip * cw_cols, cw_cols, axis=1)
    pad_cw = lambda a: jnp.zeros((SUBLANES, cw_pad), F32).at[:a.shape[0], :cw_cols].set(a)
    cut_cw = lambda a: a[:conv_w.shape[0], :cw_cols]
    _, d_cw, m_cw, v_cw = [cut_cw(a) for a in _adamw(pad_cw(conv_w), pad_cw(g_cw), pad_cw(m_conv_w), pad_cw(v_conv_w),
                                                     "adamw_conv_w")]

    g_wout, g_wglu = reduce_finish(*rs_a, ["w_out", "w_glu"], "a", [d_cw, packs[0]])
    (g_win,) = reduce_finish(*rs_b, ["w_in"], "b", [g_wout])
    g_win, d_win, m_win, v_win = _adamw(w_in, g_win, m_w_in, v_w_in, "adamw_w_in")
    g_wout, d_wout, m_wout, v_wout = _adamw(w_out, g_wout, m_w_out, v_w_out, "adamw_w_out")
    g_wglu, d_wglu, m_wglu, v_wglu = _adamw(w_glu, g_wglu, m_w_glu, v_w_glu, "adamw_w_glu")

    order = ["norm_pre_g", "w_in", "conv_w", "conv_b", "ssm_a_re", "ssm_a_im", "ssm_log_dt", "ssm_b_re", "ssm_b_im",
             "ssm_c_re", "ssm_c_im", "ssm_d", "w_glu", "b_glu", "w_out", "norm_post_g"]
    grads, deltas, new_m, new_v = dict(sg), dict(sd), dict(sm), dict(sv)
    grads.update(w_in=g_win, w_out=g_wout, w_glu=g_wglu, conv_w=g_cw)
    deltas.update(w_in=d_win, w_out=d_wout, w_glu=d_wglu, conv_w=d_cw)
    new_m.update(w_in=m_win, w_out=m_wout, w_glu=m_wglu, conv_w=m_cw)
    new_v.update(w_in=v_win, w_out=v_wout, w_glu=v_wglu, conv_w=v_cw)
    return (loss, gx[None], *[grads[nm] for nm in order], *[deltas[nm] for nm in order],
            *[new_m[nm] for nm in order], *[new_v[nm] for nm in order])
```

```python
import math

import jax
import jax.numpy as jnp
from jax import lax
from jax.experimental import pallas as pl
from jax.experimental.pallas import tpu as pltpu

F32 = jnp.float32
BF16 = jnp.bfloat16
MESH = pl.DeviceIdType.MESH

EPS = 1e-6
SSM_H = 16
SSM_P = 64
GROUPS_PER_SLAB = 8
SLAB = 128
SLAB_STATES = GROUPS_PER_SLAB * SSM_P
N_CHIPS = 4
N_DEV = 8

ADAM_LR = 0.001
ADAM_B1 = 0.9
ADAM_B2 = 0.999
ADAM_EPS = 1e-08
ADAM_WD = 0.01
ADAM_STEP = 10

MIB = 1024 * 1024
VMEM_CAP = 48 * MIB
SUBLANES = 8

TM_NORM = 512
TM_PROJ = 512
TM_GATE = 256
TM_OUT = 256
TM_DH = 256
T_SCAN = 256
TK_TN = 1024
TM_TN = 1024
TR_ELT = 256


def _cp(vmem_bytes, **kw):
    return pltpu.CompilerParams(vmem_limit_bytes=int(min(VMEM_CAP, max(16 * MIB, vmem_bytes))), **kw)


def _pallas(body, **kw):
    if "grid" not in kw and "grid_spec" not in kw:
        return pl.pallas_call(body, **kw)
    pin = lambda s: pltpu.HBM(s.shape, s.dtype) if isinstance(s, jax.ShapeDtypeStruct) else s
    out_shape = kw.pop("out_shape")
    out_shape = [pin(s) for s in out_shape] if isinstance(out_shape, (list, tuple)) else pin(out_shape)
    call = pl.pallas_call(body, out_shape=out_shape, **kw)

    def run(*args):
        return call(*[pltpu.with_memory_space_constraint(a, pltpu.HBM) if jnp.issubdtype(a.dtype, jnp.floating) else a
                      for a in args])

    return run


def _my_place():
    return lax.axis_index("x"), lax.axis_index("y"), lax.axis_index("c")


def _other_chips(x, y):
    return [(1 - x, y), (x, 1 - y), (1 - x, 1 - y)]


def _silu(z):
    s = jax.nn.sigmoid(z)
    return z * s, s


def _dsilu(z, s):
    return s * (1.0 + z * (1.0 - s))


_GELU_K = math.sqrt(2.0 / math.pi)
_GELU_C = 0.044715


def _gelu(y):
    th = jnp.tanh(_GELU_K * (y + _GELU_C * y * y * y))
    return 0.5 * y * (1.0 + th), th


def _dgelu(y, th):
    return 0.5 * (1.0 + th) + 0.5 * y * (1.0 - th * th) * _GELU_K * (1.0 + 3.0 * _GELU_C * y * y)


def _cast_bf16(w, name, after=None):
    r, c = w.shape
    tr = min(TR_ELT, r)
    extra = [] if after is None else [after]

    def body(w_ref, *rest):
        rest[-1][...] = w_ref[...].astype(BF16)

    return _pallas(
        body, name=name, grid=(r // tr,),
        in_specs=[pl.BlockSpec((tr, c), lambda i: (i, 0))] + [pl.BlockSpec((SUBLANES, SLAB), lambda i: (0, 0))] * len(extra),
        out_specs=pl.BlockSpec((tr, c), lambda i: (i, 0)),
        out_shape=jax.ShapeDtypeStruct((r, c), BF16),
        compiler_params=_cp(12 * tr * c),
    )(w, *extra)


def _prenorm(x, g):
    t, d = x.shape
    tm = min(TM_NORM, t)

    def body(x_ref, g_ref, h_ref):
        xv = x_ref[...]
        r = lax.rsqrt(jnp.mean(xv * xv, axis=-1, keepdims=True) + EPS)
        h_ref[...] = (xv * r * g_ref[...]).astype(BF16)

    return _pallas(
        body, name="prenorm", grid=(t // tm,),
        in_specs=[pl.BlockSpec((tm, d), lambda i: (i, 0)), pl.BlockSpec((1, d), lambda i: (0, 0))],
        out_specs=pl.BlockSpec((tm, d), lambda i: (i, 0)),
        out_shape=jax.ShapeDtypeStruct((t, d), BF16),
        compiler_params=_cp(20 * tm * d),
    )(x, g)


def _inproj_own(place, a, w_own):
    t, k = a.shape
    n = w_own.shape[1]
    tm = min(TM_PROJ, t)

    def body(place_ref, a_ref, b_ref, o_ref):
        del place_ref
        o_ref[...] = jnp.dot(a_ref[...], b_ref[...], preferred_element_type=F32)

    return _pallas(
        body, name="inproj_own",
        grid_spec=pltpu.PrefetchScalarGridSpec(
            num_scalar_prefetch=1, grid=(t // tm,),
            in_specs=[pl.BlockSpec((tm, k), lambda i, p: (i, 0)), pl.BlockSpec((k, n), lambda i, p: (0, 0))],
            out_specs=pl.BlockSpec((tm, n), lambda i, p: (i, p[1]))),
        out_shape=jax.ShapeDtypeStruct((t, N_CHIPS * n), F32),
        compiler_params=_cp(2 * (2 * tm * k + 2 * k * n + 4 * tm * n) + 4 * MIB),
    )(place, a, w_own)


def _inproj_rest(place, a, b, partial, after):
    t, k = a.shape
    nb, _, n = b.shape
    tm = min(TM_PROJ, t)
    shard = lambda s, p: (p[1] + 1 + s) % nb

    def body(place_ref, a_ref, b_ref, after_ref, partial_ref, o_ref):
        del place_ref, after_ref, partial_ref
        o_ref[...] = jnp.dot(a_ref[...], b_ref[...], preferred_element_type=F32)

    return _pallas(
        body, name="inproj_rest",
        grid_spec=pltpu.PrefetchScalarGridSpec(
            num_scalar_prefetch=1, grid=(nb - 1, t // tm),
            in_specs=[pl.BlockSpec((tm, k), lambda s, i, p: (i, 0)),
                      pl.BlockSpec((None, k, n), lambda s, i, p: (shard(s, p), 0, 0)),
                      pl.BlockSpec((SUBLANES, SLAB), lambda s, i, p: (0, 0)),
                      pl.BlockSpec(memory_space=pl.ANY)],
            out_specs=pl.BlockSpec((tm, n), lambda s, i, p: (i, shard(s, p)))),
        out_shape=jax.ShapeDtypeStruct((t, nb * n), F32),
        input_output_aliases={4: 0},
        compiler_params=_cp(2 * (2 * tm * k + 2 * k * n + 4 * tm * n) + 4 * MIB),
    )(place, a, b, after, partial)


def _matmul_tn(a, b, nb, name):
    t, m = a.shape
    n = b.shape[1] // nb
    tk = min(TK_TN, t)
    tma = min(TM_TN, m)

    def body(a_ref, b_ref, o_ref, o16_ref):
        k = pl.program_id(2)

        @pl.when(k == 0)
        def _():
            o_ref[...] = jnp.zeros_like(o_ref)

        o_ref[...] += lax.dot_general(a_ref[...], b_ref[...], (((0,), (0,)), ((), ())), preferred_element_type=F32)

        @pl.when(k == t // tk - 1)
        def _():
            o16_ref[...] = o_ref[...].astype(BF16)

    blk = pl.BlockSpec((None, tma, n), lambda j, i, k: (j, i, 0))
    return _pallas(
        body, name=name, grid=(nb, m // tma, t // tk),
        in_specs=[pl.BlockSpec((tk, tma), lambda j, i, k: (k, i)), pl.BlockSpec((tk, n), lambda j, i, k: (k, j))],
        out_specs=[blk, blk],
        out_shape=[jax.ShapeDtypeStruct((nb, m, n), F32), jax.ShapeDtypeStruct((nb, m, n), BF16)],
        compiler_params=_cp(2 * (2 * tk * tma + 2 * tk * n + 6 * tma * n) + 8 * MIB),
    )(a, b)


def _outproj(mix, w_out, x, tgt, g_post):
    t, dm = mix.shape
    d = w_out.shape[1]
    tm = min(TM_OUT, t)

    rb = 2 * SUBLANES
    assert tm % rb == 0

    def body(mix_ref, w_ref, x_ref, t_ref, g_ref, loss_ref, dout_ref, do_ref, dmix_ref, gg_ref, o_buf):
        @pl.when(pl.program_id(0) == 0)
        def _():
            loss_ref[...] = jnp.zeros_like(loss_ref)
            gg_ref[...] = jnp.zeros_like(gg_ref)

        w = w_ref[...]
        o_buf[...] = jnp.dot(mix_ref[...], w, preferred_element_type=F32)
        g = g_ref[...]
        fold = lambda v: v[0:SUBLANES] + v[SUBLANES:rb]
        sq_err = jnp.zeros((SUBLANES, d), F32)
        gg = jnp.zeros((SUBLANES, d), F32)
        for c in range(tm // rb):
            rs = pl.ds(c * rb, rb)
            o = o_buf[rs, :]
            r = lax.rsqrt(jnp.mean(o * o, axis=-1, keepdims=True) + EPS)
            nh = o * r
            e = x_ref[rs, :] + nh * g - t_ref[rs, :]
            sq_err = sq_err + fold(e * e)
            dout = e * (1.0 / d)
            dout_ref[rs, :] = dout
            gg = gg + fold(dout * nh)
            dn = dout * g
            do = r * (dn - nh * jnp.mean(dn * nh, axis=-1, keepdims=True))
            do_ref[rs, :] = do.astype(BF16)
        loss_ref[...] += jnp.sum(sq_err) * (0.5 / d)
        gg_ref[0:1, :] += jnp.sum(gg, axis=0, keepdims=True)
        dmix_ref[...] = lax.dot_general(do_ref[...], w, (((1,), (1,)), ((), ())), preferred_element_type=F32)

    row = lambda i: (i, 0)
    fixed = lambda i: (0, 0)
    return _pallas(
        body, name="outproj", grid=(t // tm,),
        in_specs=[pl.BlockSpec((tm, dm), row), pl.BlockSpec((dm, d), fixed), pl.BlockSpec((tm, d), row),
                  pl.BlockSpec((tm, d), row), pl.BlockSpec((1, d), fixed)],
        out_specs=[pl.BlockSpec((SUBLANES, SLAB), fixed), pl.BlockSpec((tm, d), row), pl.BlockSpec((tm, d), row),
                   pl.BlockSpec((tm, dm), row), pl.BlockSpec((SUBLANES, d), fixed)],
        out_shape=[jax.ShapeDtypeStruct((SUBLANES, SLAB), F32), jax.ShapeDtypeStruct((t, d), F32),
                   jax.ShapeDtypeStruct((t, d), BF16), jax.ShapeDtypeStruct((t, dm), F32),
                   jax.ShapeDtypeStruct((SUBLANES, d), F32)],
        scratch_shapes=[pltpu.VMEM((tm, d), F32)],
        compiler_params=_cp(2 * (2 * dm * d + tm * (2 * dm + 4 * d * 3 + 2 * d + 4 * dm)) + 4 * tm * d + 12 * MIB),
    )(mix, w_out, x, tgt, g_post)


def _dh_prenorm_bwd(dproj, w_in, x, dout, g_pre):
    t, d = x.shape
    nb, _, n = w_in.shape
    tm = min(TM_DH, t)

    def body(dp_ref, w_ref, x_ref, dout_ref, g_ref, dx_ref, gg_ref):
        @pl.when(pl.program_id(0) == 0)
        def _():
            gg_ref[...] = jnp.zeros_like(gg_ref)

        dh = None
        for k in range(nb):
            part = lax.dot_general(dp_ref[:, k * n:(k + 1) * n], w_ref[k], (((1,), (1,)), ((), ())),
                                   preferred_element_type=F32)
            dh = part if dh is None else dh + part
        xv = x_ref[...]
        r = lax.rsqrt(jnp.mean(xv * xv, axis=-1, keepdims=True) + EPS)
        xh = xv * r
        gg_ref[0:1, :] += jnp.sum(dh * xh, axis=0, keepdims=True)
        dg = dh * g_ref[...]
        dx_ref[...] = dout_ref[...] + r * (dg - xh * jnp.mean(dg * xh, axis=-1, keepdims=True))

    row = lambda i: (i, 0)
    fixed = lambda i: (0, 0)
    w_spec = pl.BlockSpec(w_in.shape, lambda i: (0, 0, 0), pipeline_mode=pl.Buffered(1))
    return _pallas(
        body, name="dh_prenorm_bwd", grid=(t // tm,),
        in_specs=[pl.BlockSpec((tm, nb * n), row), w_spec,
                  pl.BlockSpec((tm, d), row), pl.BlockSpec((tm, d), row), pl.BlockSpec((1, d), fixed)],
        out_specs=[pl.BlockSpec((tm, d), row), pl.BlockSpec((SUBLANES, d), fixed)],
        out_shape=[jax.ShapeDtypeStruct((t, d), F32), jax.ShapeDtypeStruct((SUBLANES, d), F32)],
        compiler_params=_cp(2 * nb * d * n + 2 * (2 * tm * nb * n + 12 * tm * d) + 16 * tm * d + 4 * MIB),
    )(dproj, w_in, x, dout, g_pre)


def _adamw(w, g, m, v, name):
    r, c = w.shape
    tr = min(TR_ELT, r)
    c1 = 1.0 - ADAM_B1 ** ADAM_STEP
    c2 = 1.0 - ADAM_B2 ** ADAM_STEP

    def body(w_ref, g_ref, m_ref, v_ref, go_ref, d_ref, mo_ref, vo_ref):
        gv = g_ref[...]
        go_ref[...] = gv
        mn = ADAM_B1 * m_ref[...] + (1.0 - ADAM_B1) * gv
        vn = ADAM_B2 * v_ref[...] + (1.0 - ADAM_B2) * (gv * gv)
        d_ref[...] = -ADAM_LR * ((mn / c1) / (jnp.sqrt(vn / c2) + ADAM_EPS) + ADAM_WD * w_ref[...])
        mo_ref[...] = mn
        vo_ref[...] = vn

    spec = pl.BlockSpec((tr, c), lambda i: (i, 0))
    sds = jax.ShapeDtypeStruct((r, c), F32)
    return _pallas(
        body, name=name, grid=(r // tr,), in_specs=[spec] * 4, out_specs=[spec] * 4, out_shape=[sds] * 4,
        compiler_params=_cp(2 * 8 * 4 * tr * c + 8 * MIB),
    )(w, g, m, v)


def _adamw_small(g, w, m, v):
    r, c = w.shape
    c1 = 1.0 - ADAM_B1 ** ADAM_STEP
    c2 = 1.0 - ADAM_B2 ** ADAM_STEP

    def body(g_ref, w_ref, m_ref, v_ref, d_ref, mo_ref, vo_ref):
        gv = g_ref[...]
        mn = ADAM_B1 * m_ref[...] + (1.0 - ADAM_B1) * gv
        vn = ADAM_B2 * v_ref[...] + (1.0 - ADAM_B2) * (gv * gv)
        d_ref[...] = -ADAM_LR * ((mn / c1) / (jnp.sqrt(vn / c2) + ADAM_EPS) + ADAM_WD * w_ref[...])
        mo_ref[...] = mn
        vo_ref[...] = vn

    sds = jax.ShapeDtypeStruct((r, c), F32)
    return _pallas(
        body, name="adamw_small", out_shape=[sds] * 3,
        compiler_params=_cp(12 * 4 * r * c + 8 * MIB),
    )(g, w, m, v)


def _allreduce_small(v):
    r, c = v.shape
    assert r % (2 * SUBLANES) == 0
    h = r // 2

    def body(v_ref, out_ref, got_ref, chip_ref, slots_ref, send, recv):
        x, y, cc = _my_place()
        k = 2 * x + y
        sib = (x, y, 1 - cc)

        def rc(s, src, dst, to):
            return pltpu.make_async_remote_copy(src_ref=src, dst_ref=dst, send_sem=send.at[s], recv_sem=recv.at[s],
                                                device_id=to, device_id_type=MESH)

        pair = rc(0, v_ref, got_ref, sib)
        pair.start()
        pair.wait()
        chip_ref[...] = v_ref[...] + got_ref[...]
        mine = pl.ds(pl.multiple_of(cc * h, SUBLANES), h)
        theirs = pl.ds(pl.multiple_of((1 - cc) * h, SUBLANES), h)
        chips = _other_chips(x, y)
        sent = []
        for j, (cx, cy) in enumerate(chips):
            cp = rc(1 + j, chip_ref.at[mine], slots_ref.at[k], (cx, cy, cc))
            cp.start()
            sent.append(cp)
        slots_ref[k] = chip_ref[mine, :]
        for j, (cx, cy) in enumerate(chips):
            rc(1 + j, chip_ref.at[mine], slots_ref.at[2 * cx + cy], (cx, cy, cc)).wait_recv()
        total = slots_ref[0]
        for kk in range(1, N_CHIPS):
            total = total + slots_ref[kk]
        out_ref[mine, :] = total
        for cp in sent:
            cp.wait_send()
        share = rc(N_CHIPS, out_ref.at[mine], out_ref.at[mine], sib)
        share.start()
        rc(N_CHIPS, out_ref.at[theirs], out_ref.at[theirs], sib).wait_recv()
        share.wait_send()

    vm = pl.BlockSpec(memory_space=pltpu.VMEM)
    return _pallas(
        body, name="allreduce_small", out_shape=jax.ShapeDtypeStruct((r, c), F32), in_specs=[vm], out_specs=vm,
        scratch_shapes=[pltpu.VMEM((r, c), F32), pltpu.VMEM((r, c), F32), pltpu.VMEM((N_CHIPS, h, c), F32),
                        pltpu.SemaphoreType.DMA((N_CHIPS + 1,)), pltpu.SemaphoreType.DMA((N_CHIPS + 1,))],
        compiler_params=_cp(6 * 4 * r * c + 8 * MIB),
    )(v)


def _zoh(a_re, a_im, log_dt, b_re2, b_im2, expand):
    dt = jnp.exp(log_dt)
    mag = jnp.exp(a_re * dt)
    lbr, lbi = mag * jnp.cos(a_im * dt), mag * jnp.sin(a_im * dt)
    nr, ni = lbr - 1.0, lbi
    den = a_re * a_re + a_im * a_im
    qr = (nr * a_re + ni * a_im) / den
    qi = (ni * a_re - nr * a_im) / den
    qr2 = jnp.dot(qr, expand, precision=lax.Precision.HIGHEST, preferred_element_type=F32)
    qi2 = jnp.dot(qi, expand, precision=lax.Precision.HIGHEST, preferred_element_type=F32)
    return lbr, lbi, qr2 * b_re2 - qi2 * b_im2, qr2 * b_im2 + qi2 * b_re2


def _zoh_fwd(a_re, a_im, log_dt, b_re2, b_im2, expand, power):
    g, p = a_re.shape

    def body(ar_ref, ai_ref, ld_ref, br_ref, bi_ref, e_ref, lbr_ref, lbi_ref, bbr_ref, bbi_ref, pw_ref):
        ar, ai, ld = ar_ref[...], ai_ref[...], ld_ref[...]
        lbr, lbi, bbr, bbi = _zoh(ar, ai, ld, br_ref[...], bi_ref[...], e_ref[...])
        lbr_ref[...], lbi_ref[...], bbr_ref[...], bbi_ref[...] = lbr, lbi, bbr, bbi
        dt = jnp.exp(ld) * float(power)
        mag = jnp.exp(ar * dt)
        pw_ref[0] = mag * jnp.cos(ai * dt)
        pw_ref[1] = mag * jnp.sin(ai * dt)

    gp = jax.ShapeDtypeStruct((g, p), F32)
    gph = jax.ShapeDtypeStruct(b_re2.shape, F32)
    return _pallas(
        body, name="zoh_fwd", out_shape=[gp, gp, gph, gph, jax.ShapeDtypeStruct((2, g, p), F32)],
        compiler_params=_cp(16 * MIB),
    )(a_re, a_im, log_dt, b_re2, b_im2, expand)


def _zoh_bwd(a_re, a_im, log_dt, b_re2, b_im2, expand, g_lbr, g_lbi, g_bbr, g_bbi):
    def body(ar_ref, ai_ref, ld_ref, br_ref, bi_ref, e_ref, c0, c1, c2, c3, gar, gai, gld, gbr, gbi):
        e = e_ref[...]
        _, vjp = jax.vjp(lambda a, b, c, d, f: _zoh(a, b, c, d, f, e),
                         ar_ref[...], ai_ref[...], ld_ref[...], br_ref[...], bi_ref[...])
        gar[...], gai[...], gld[...], gbr[...], gbi[...] = vjp((c0[...], c1[...], c2[...], c3[...]))

    sds = lambda a: jax.ShapeDtypeStruct(a.shape, F32)
    return _pallas(
        body, name="zoh_bwd", out_shape=[sds(a_re), sds(a_im), sds(log_dt), sds(b_re2), sds(b_im2)],
        compiler_params=_cp(16 * MIB),
    )(a_re, a_im, log_dt, b_re2, b_im2, expand, g_lbr, g_lbi, g_bbr, g_bbi)


def _blockdiag(blocks):
    nq, _, r, c = blocks.shape
    eye = jnp.eye(GROUPS_PER_SLAB, dtype=blocks.dtype)
    out = blocks[:, :, :, None, :] * eye[None, :, None, :, None]
    return out.reshape(nq, GROUPS_PER_SLAB * r, GROUPS_PER_SLAB * c)


def _blockdiag_take(dense, r, c):
    nq = dense.shape[0]
    d5 = dense.reshape(nq, GROUPS_PER_SLAB, r, GROUPS_PER_SLAB, c)
    return jnp.stack([d5[:, i, :, i, :] for i in range(GROUPS_PER_SLAB)], axis=1)


def _scan_slab(s_ref, row0, q, lam_ref, pw_ref, car_ref, steps, reverse, prev_ref=None, prev_row0=0, glam_ref=None):
    sign = -1.0 if reverse else 1.0
    half = SLAB_STATES // SLAB
    cols = [(q * 2 * half + m, q * 2 * half + half + m, q * SLAB_STATES + m * SLAB) for m in range(half)]
    nm = len(cols)
    full = (SUBLANES, SLAB)
    lam = [(jnp.broadcast_to(lam_ref[0:1, pl.ds(cl, SLAB)], full),
            jnp.broadcast_to(sign * lam_ref[1:2, pl.ds(cl, SLAB)], full)) for (_, _, cl) in cols]

    def step_rows(jj, base):
        j = (steps - 1 - jj) if reverse else jj
        return j, pl.ds(base + j * SUBLANES, SUBLANES)

    def pass1(jj, car):
        _, rows = step_rows(jj, row0)
        out = []
        for m, (cr, ci, _) in enumerate(cols):
            sr, si = car[2 * m], car[2 * m + 1]
            lr, li = lam[m]
            nr = lr * sr - li * si + s_ref[cr, rows, :]
            ni = lr * si + li * sr + s_ref[ci, rows, :]
            s_ref[cr, rows, :] = nr
            s_ref[ci, rows, :] = ni
            out += [nr, ni]
        return tuple(out)

    def run_steps(step_fn, carry):
        for jj in range(steps):
            carry = step_fn(jj, carry)
        return carry

    ends = run_steps(pass1, tuple(jnp.zeros(full, F32) for _ in range(2 * nm)))

    entry = []
    for m, (cr, ci, cl) in enumerate(cols):
        ljr = pw_ref[0:1, pl.ds(cl, SLAB)]
        lji = sign * pw_ref[1:2, pl.ds(cl, SLAB)]
        c_r = car_ref[cr, 0:1, :]
        c_i = car_ref[ci, 0:1, :]
        rows_r, rows_i = [None] * SUBLANES, [None] * SUBLANES
        order = range(SUBLANES - 1, -1, -1) if reverse else range(SUBLANES)
        for b in order:
            rows_r[b], rows_i[b] = c_r, c_i
            e_r, e_i = ends[2 * m][b:b + 1], ends[2 * m + 1][b:b + 1]
            c_r, c_i = ljr * c_r - lji * c_i + e_r, ljr * c_i + lji * c_r + e_i
        car_ref[cr, 0:1, :] = c_r
        car_ref[ci, 0:1, :] = c_i
        entry.append((jnp.concatenate(rows_r, axis=0), jnp.concatenate(rows_i, axis=0)))

    def pass2(jj, carry):
        j, rows = step_rows(jj, row0)
        decayed, acc = carry[:2 * nm], carry[2 * nm:]
        out_d, out_a = [], []
        for m, (cr, ci, cl) in enumerate(cols):
            lr, li = lam[m]
            dr, di = decayed[2 * m], decayed[2 * m + 1]
            dr, di = lr * dr - li * di, lr * di + li * dr
            nr = s_ref[cr, rows, :] + dr
            ni = s_ref[ci, rows, :] + di
            s_ref[cr, rows, :] = nr
            s_ref[ci, rows, :] = ni
            out_d += [dr, di]
            if prev_ref is not None:
                prow = pl.ds(prev_row0 + (j - 1) * SUBLANES, SUBLANES)
                qr = prev_ref[cr, prow, :]
                qi = prev_ref[ci, prow, :]
                out_a += [acc[2 * m] + (nr * qr + ni * qi), acc[2 * m + 1] + (ni * qr - nr * qi)]
        return tuple(out_d) + tuple(out_a)

    n_acc = 2 * nm if prev_ref is not None else 0
    init = tuple(e for pair in entry for e in pair) + tuple(jnp.zeros(full, F32) for _ in range(n_acc))
    accs = run_steps(pass2, init)[2 * nm:]
    if prev_ref is not None:
        for m, (_, _, cl) in enumerate(cols):
            glam_ref[0:1, pl.ds(cl, SLAB)] += jnp.sum(accs[2 * m], axis=0, keepdims=True)
            glam_ref[1:2, pl.ds(cl, SLAB)] += jnp.sum(accs[2 * m + 1], axis=0, keepdims=True)


def _permute_rows_f32(perm_bf16, v):
    hi = v.astype(BF16)
    lo = (v - hi.astype(F32)).astype(BF16)
    return (jnp.dot(perm_bf16, hi, preferred_element_type=F32) + jnp.dot(perm_bf16, lo, preferred_element_type=F32))


def _put_slab(s_ref, rows, q, val):
    per = 2 * SLAB_STATES // SLAB
    for i in range(per):
        s_ref[q * per + i, rows, :] = val[:, i * SLAB:(i + 1) * SLAB]


def _get_slab(s_ref, rows, q):
    per = 2 * SLAB_STATES // SLAB
    return jnp.concatenate([s_ref[q * per + i, rows, :] for i in range(per)], axis=1)


def _step_major_perm(tt):
    r = jnp.arange(tt)
    held = (r % SUBLANES) * (tt // SUBLANES) + r // SUBLANES
    return held[:, None] == r[None, :]


def _ssm_fwd(proj, u_block, bq, cq, lam, pw, d_skip):
    t = proj.shape[0]
    nq, ds, w2 = bq.shape
    assert ds == SLAB and w2 == 2 * SLAB_STATES
    dssm = nq * SLAB
    width = nq * w2
    ntile = width // SLAB
    tt = min(T_SCAN, t)
    steps = tt // SUBLANES
    perm = _step_major_perm(tt)
    pm, pmt = perm.astype(BF16), perm.T.astype(BF16)

    def body(u_ref, pm_ref, pmt_ref, bq_ref, cq_ref, lam_ref, pw_ref, d_ref, y_ref, cin_ref, s_ref, car_ref):
        @pl.when(pl.program_id(0) == 0)
        def _():
            car_ref[...] = jnp.zeros_like(car_ref)

        cin_ref[...] = jnp.broadcast_to(car_ref[:, 0:1, :], cin_ref.shape)
        u = u_ref[...]
        ub = jnp.dot(pm_ref[...], u.astype(BF16), preferred_element_type=F32).astype(BF16)
        everything = slice(None)
        for q in range(nq):
            _put_slab(s_ref, everything, q,
                      jnp.dot(ub[:, q * SLAB:(q + 1) * SLAB], bq_ref[q], preferred_element_type=F32))
        for q in range(nq):
            _scan_slab(s_ref, 0, q, lam_ref, pw_ref, car_ref, steps, reverse=False)
        y_sm = jnp.concatenate(
            [jnp.dot(_get_slab(s_ref, everything, q).astype(BF16), cq_ref[q], preferred_element_type=F32)
             for q in range(nq)], axis=1)
        y_ref[...] = _permute_rows_f32(pmt_ref[...], y_sm) + d_ref[...] * u

    c3 = lambda i: (0, 0, 0)
    c2 = lambda i: (0, 0)
    return _pallas(
        body, name="ssm_fwd", grid=(t // tt,),
        in_specs=[pl.BlockSpec((tt, dssm), lambda i: (i, u_block)), pl.BlockSpec((tt, tt), c2),
                  pl.BlockSpec((tt, tt), c2), pl.BlockSpec(bq.shape, c3),
                  pl.BlockSpec(cq.shape, c3), pl.BlockSpec(lam.shape, c2), pl.BlockSpec(pw.shape, c2),
                  pl.BlockSpec((1, dssm), c2)],
        out_specs=[pl.BlockSpec((tt, dssm), lambda i: (i, 0)),
                   pl.BlockSpec((None, ntile, SUBLANES, SLAB), lambda i: (i, 0, 0, 0))],
        out_shape=[jax.ShapeDtypeStruct((t, dssm), F32), jax.ShapeDtypeStruct((t // tt, ntile, SUBLANES, SLAB), F32)],
        scratch_shapes=[pltpu.VMEM((ntile, tt, SLAB), F32), pltpu.VMEM((ntile, SUBLANES, SLAB), F32)],
        compiler_params=_cp(2 * (8 * tt * dssm + 4 * nq * ds * w2) + 4 * tt * width + 16 * MIB,
                            dimension_semantics=("arbitrary",)),
    )(proj, pm, pmt, bq, cq, lam, pw, d_skip)


def _ssm_bwd(proj, u_block, dy, cin, dproj, bq, cq, lam, pw, d_skip):
    t = proj.shape[0]
    nq, ds, w2 = bq.shape
    dssm = nq * SLAB
    width = nq * w2
    ntile = width // SLAB
    tt = min(T_SCAN, t)
    nt = t // tt
    steps = tt // SUBLANES
    halo = SUBLANES
    perm = _step_major_perm(tt)
    pm, pmt = perm.astype(BF16), perm.T.astype(BF16)

    def body(u_ref, dy_ref, cin_ref, dp_any, pm_ref, pmt_ref, bq_ref, cq_ref, lam_ref, pw_ref, d_ref,
             du_ref, gb_ref, gc_ref, glam_ref, gd_ref, s_ref, gs_ref, car_f, car_b):
        del dp_any

        @pl.when(pl.program_id(0) == 0)
        def _():
            car_b[...] = jnp.zeros_like(car_b)
            gb_ref[...] = jnp.zeros_like(gb_ref)
            gc_ref[...] = jnp.zeros_like(gc_ref)
            glam_ref[...] = jnp.zeros_like(glam_ref)
            gd_ref[...] = jnp.zeros_like(gd_ref)

        u = u_ref[...]
        dyv = dy_ref[...]
        gd_ref[0:1, :] += jnp.sum(dyv * u, axis=0, keepdims=True)
        pmv = pm_ref[...]
        ub = jnp.dot(pmv, u.astype(BF16), preferred_element_type=F32).astype(BF16)
        dyb = jnp.dot(pmv, dyv.astype(BF16), preferred_element_type=F32).astype(BF16)
        car_f[...] = cin_ref[...]
        data = slice(halo, halo + tt)
        everything = slice(None)
        for q in range(nq):
            _put_slab(s_ref, data, q, jnp.dot(ub[:, q * SLAB:(q + 1) * SLAB], bq_ref[q], preferred_element_type=F32))
        for q in range(nq):
            _scan_slab(s_ref, halo, q, lam_ref, pw_ref, car_f, steps, reverse=False)
        last_step = s_ref[:, halo + tt - SUBLANES:halo + tt, :]
        s_ref[:, 0:halo, :] = jnp.concatenate([cin_ref[:, 0:1, :], last_step[:, 0:SUBLANES - 1, :]], axis=1)
        tn = (((0,), (0,)), ((), ()))
        nt_dims = (((1,), (1,)), ((), ()))
        for q in range(nq):
            sl = slice(q * SLAB, (q + 1) * SLAB)
            gc_ref[q] += lax.dot_general(dyb[:, sl], _get_slab(s_ref, data, q).astype(BF16), tn,
                                         preferred_element_type=F32)
            _put_slab(gs_ref, everything, q,
                      lax.dot_general(dyb[:, sl], cq_ref[q], nt_dims, preferred_element_type=F32))
        for q in range(nq):
            _scan_slab(gs_ref, 0, q, lam_ref, pw_ref, car_b, steps, reverse=True,
                       prev_ref=s_ref, prev_row0=halo, glam_ref=glam_ref)
        du_parts = []
        for q in range(nq):
            sl = slice(q * SLAB, (q + 1) * SLAB)
            gsb = _get_slab(gs_ref, everything, q).astype(BF16)
            du_parts.append(lax.dot_general(gsb, bq_ref[q], nt_dims, preferred_element_type=F32))
            gb_ref[q] += lax.dot_general(ub[:, sl], gsb, tn, preferred_element_type=F32)
        du_sm = jnp.concatenate(du_parts, axis=1)
        du_ref[...] = (_permute_rows_f32(pmt_ref[...], du_sm) + dyv * d_ref[...]).astype(BF16)

    c3 = lambda i: (0, 0, 0)
    c2 = lambda i: (0, 0)
    rev = lambda i: (nt - 1 - i, 0)
    dense = jax.ShapeDtypeStruct((nq, SLAB, w2), F32)
    gp = lam.shape[1]
    return _pallas(
        body, name="ssm_bwd", grid=(nt,),
        in_specs=[pl.BlockSpec((tt, dssm), lambda i: (nt - 1 - i, u_block)), pl.BlockSpec((tt, dssm), rev),
                  pl.BlockSpec((None, ntile, SUBLANES, SLAB), lambda i: (nt - 1 - i, 0, 0, 0)),
                  pl.BlockSpec(memory_space=pl.ANY), pl.BlockSpec((tt, tt), c2), pl.BlockSpec((tt, tt), c2),
                  pl.BlockSpec(bq.shape, c3), pl.BlockSpec(cq.shape, c3),
                  pl.BlockSpec(lam.shape, c2), pl.BlockSpec(pw.shape, c2), pl.BlockSpec((1, dssm), c2)],
        out_specs=[pl.BlockSpec((tt, dssm), lambda i: (nt - 1 - i, u_block)), pl.BlockSpec(dense.shape, c3),
                   pl.BlockSpec(dense.shape, c3), pl.BlockSpec((SUBLANES, gp), c2), pl.BlockSpec((SUBLANES, dssm), c2)],
        out_shape=[jax.ShapeDtypeStruct(dproj.shape, dproj.dtype), dense, dense,
                   jax.ShapeDtypeStruct((SUBLANES, gp), F32), jax.ShapeDtypeStruct((SUBLANES, dssm), F32)],
        scratch_shapes=[pltpu.VMEM((ntile, tt + halo, SLAB), F32), pltpu.VMEM((ntile, tt, SLAB), F32),
                        pltpu.VMEM((ntile, SUBLANES, SLAB), F32), pltpu.VMEM((ntile, SUBLANES, SLAB), F32)],
        input_output_aliases={3: 0},
        compiler_params=_cp(2 * (10 * tt * dssm + 4 * nq * ds * w2 + 8 * nq * SLAB * w2)
                            + 8 * tt * width + 12 * MIB, dimension_semantics=("arbitrary",)),
    )(proj, dy, cin, dproj, pm, pmt, bq, cq, lam, pw, d_skip)


def _gate_fwd(proj, y, conv_w, conv_b, w_glu, b_glu, dc):
    t = proj.shape[0]
    dssm = y.shape[1]
    assert dc == dssm
    tm = min(TM_GATE, t)
    halo = SUBLANES

    rb = 2 * SUBLANES
    assert tm % rb == 0

    def body(b_ref, c_ref, v_ref, zc_ref, zs_ref, y_ref, cw_ref, cb_ref, wg_ref, bg_ref, mix_ref,
             cv_buf, sh1_buf, sh2_buf, ge_buf, geb_buf, gl_buf):
        @pl.when(pl.program_id(0) == 0)
        def _():
            cv_buf[0:halo, :] = jnp.zeros((halo, dc), F32)

        cv_buf[halo:, :] = c_ref[...] * v_ref[...]
        sh1_buf[...] = cv_buf[halo - 1:halo - 1 + tm, :]
        sh2_buf[...] = cv_buf[halo - 2:halo - 2 + tm, :]
        w0, w1, w2, cb = cw_ref[0:1, :], cw_ref[1:2, :], cw_ref[2:3, :], cb_ref[...]
        chunks = [pl.ds(r * rb, rb) for r in range(tm // rb)]
        for r, rs in enumerate(chunks):
            ge, _ = _gelu(y_ref[rs, :])
            ge_buf[rs, :] = ge
            geb_buf[rs, :] = ge.astype(BF16)
        gl_buf[...] = jnp.dot(geb_buf[...], wg_ref[...], preferred_element_type=F32) + bg_ref[...]
        for r, rs in enumerate(chunks):
            conv = cb + w2 * cv_buf[pl.ds(halo + r * rb, rb), :] + w1 * sh1_buf[rs, :] + w0 * sh2_buf[rs, :]
            sz, _ = _silu(zc_ref[rs, :])
            mix_ref[rs, 0:dc] = (b_ref[rs, :] * conv * sz).astype(BF16)
        cv_buf[0:halo, :] = cv_buf[tm:tm + halo, :]
        for r, rs in enumerate(chunks):
            szs, _ = _silu(zs_ref[rs, :])
            mix_ref[rs, dc:] = (ge_buf[rs, :] * jax.nn.sigmoid(gl_buf[rs, :]) * szs).astype(BF16)

    col = lambda j: pl.BlockSpec((tm, dc), lambda i, j=j: (i, j))
    fixed = lambda i: (0, 0)
    return _pallas(
        body, name="gate_fwd", grid=(t // tm,),
        in_specs=[col(0), col(1), col(2), col(3), col(5), pl.BlockSpec((tm, dssm), lambda i: (i, 0)),
                  pl.BlockSpec(conv_w.shape, fixed), pl.BlockSpec((1, dc), fixed),
                  pl.BlockSpec(w_glu.shape, fixed), pl.BlockSpec((1, dssm), fixed)],
        out_specs=pl.BlockSpec((tm, dc + dssm), lambda i: (i, 0)),
        out_shape=jax.ShapeDtypeStruct((t, dc + dssm), BF16),
        scratch_shapes=[pltpu.VMEM((tm + halo, dc), F32), pltpu.VMEM((tm, dc), F32), pltpu.VMEM((tm, dc), F32),
                        pltpu.VMEM((tm, dssm), F32), pltpu.VMEM((tm, dssm), BF16), pltpu.VMEM((tm, dssm), F32)],
        compiler_params=_cp(2 * (6 * 4 * tm * dc + 2 * tm * (dc + dssm) + 2 * dssm * dssm) + 24 * tm * dc + 8 * MIB,
                            dimension_semantics=("arbitrary",)),
    )(proj, proj, proj, proj, proj, y, conv_w, conv_b, w_glu, b_glu)


def _gate_bwd(proj, y, dmix, conv_w, conv_b, w_glu, b_glu, dc):
    t = proj.shape[0]
    dssm = y.shape[1]
    tm = min(TM_GATE, t)
    nt = t // tm
    halo = SUBLANES
    blocks_per_tile = tm // halo

    rb = 2 * SUBLANES
    assert tm % rb == 0
    n_chunk = tm // rb

    def body(b_ref, c_ref, v_ref, zc_ref, zs_ref, cp_ref, vp_ref, y_ref, dm_ref, cw_ref, cb_ref, wg_ref, bg_ref,
             dp_ref, dy_ref, gs_ref, gwg_ref, cv_buf, dc_buf, sh1_buf, sh2_buf, geb_buf, dglb_buf, ge_buf, th_buf):
        i = pl.program_id(0)

        @pl.when(i == 0)
        def _():
            dc_buf[tm:, :] = jnp.zeros((halo, dc), F32)
            gs_ref[...] = jnp.zeros_like(gs_ref)
            gwg_ref[...] = jnp.zeros_like(gwg_ref)

        def rows(r):
            return pl.ds(r * rb, rb)

        def rows_after_halo(r):
            return pl.ds(halo + r * rb, rb)

        def chunks(step, carry):
            for r in range(n_chunk):
                carry = step(r, carry)
            return carry

        def fold(v):
            return v[0:SUBLANES] + v[SUBLANES:rb]

        first_tile = (i == nt - 1)
        cv_buf[0:halo, :] = jnp.where(first_tile, 0.0, cp_ref[...] * vp_ref[...])
        cv_buf[halo:, :] = c_ref[...] * v_ref[...]
        sh1_buf[...] = cv_buf[halo - 1:halo - 1 + tm, :]
        sh2_buf[...] = cv_buf[halo - 2:halo - 2 + tm, :]
        w0, w1, w2, cb = cw_ref[0:1, :], cw_ref[1:2, :], cw_ref[2:3, :], cb_ref[...]

        def conv_pass_a(r, carry):
            rs = rows(r)
            bv, zc, dyc = b_ref[rs, :], zc_ref[rs, :], dm_ref[rs, 0:dc]
            conv = cb + w2 * cv_buf[rows_after_halo(r), :] + w1 * sh1_buf[rs, :] + w0 * sh2_buf[rs, :]
            sz, sgc = _silu(zc)
            dp_ref[rs, 0:dc] = (dyc * conv * sz).astype(BF16)
            dp_ref[rs, 3 * dc:4 * dc] = (dyc * bv * conv * _dsilu(zc, sgc)).astype(BF16)
            dc_buf[rs, :] = dyc * bv * sz
            return carry

        chunks(conv_pass_a, 0)
        sh1_buf[...] = dc_buf[1:1 + tm, :]
        sh2_buf[...] = dc_buf[2:2 + tm, :]

        def conv_pass_b(r, acc):
            rs = rows(r)
            dconv, d1, d2 = dc_buf[rs, :], sh1_buf[rs, :], sh2_buf[rs, :]
            cv = cv_buf[rows_after_halo(r), :]
            dcv = w2 * dconv + w1 * d1 + w0 * d2
            dp_ref[rs, dc:2 * dc] = (dcv * v_ref[rs, :]).astype(BF16)
            dp_ref[rs, 2 * dc:3 * dc] = (dcv * c_ref[rs, :]).astype(BF16)
            return (acc[0] + fold(cv * d2), acc[1] + fold(cv * d1), acc[2] + fold(cv * dconv), acc[3] + fold(dconv))

        zero = jnp.zeros((SUBLANES, dc), F32)
        sums = chunks(conv_pass_b, (zero, zero, zero, zero))
        for k in range(4):
            gs_ref[k:k + 1, :] += jnp.sum(sums[k], axis=0, keepdims=True)
        dc_buf[tm:, :] = dc_buf[0:halo, :]

        def glu_pass_a(r, carry):
            rs = rows(r)
            ge, th = _gelu(y_ref[rs, :])
            geb_buf[rs, :] = ge.astype(BF16)
            ge_buf[rs, :] = ge
            th_buf[rs, :] = th
            return carry

        chunks(glu_pass_a, 0)
        wg = wg_ref[...]
        gl_buf = cv_buf.at[halo:halo + tm]
        keep_buf = dc_buf.at[0:tm]
        gl_buf[...] = jnp.dot(geb_buf[...], wg, preferred_element_type=F32) + bg_ref[...]
        dp_ref[:, 4 * dc:5 * dc] = jnp.zeros((tm, dc), BF16)

        def glu_pass_b(r, acc):
            rs = rows(r)
            ge = ge_buf[rs, :]
            sg = jax.nn.sigmoid(gl_buf[rs, :])
            zs, dys = zs_ref[rs, :], dm_ref[rs, dc:]
            szs, sgs = _silu(zs)
            dp_ref[rs, 5 * dc:] = (dys * ge * sg * _dsilu(zs, sgs)).astype(BF16)
            d_ys = dys * szs
            dgl = d_ys * ge * sg * (1.0 - sg)
            dglb_buf[rs, :] = dgl.astype(BF16)
            keep_buf[rs, :] = d_ys * sg
            return acc + fold(dgl)

        gs_ref[4:5, :] += jnp.sum(chunks(glu_pass_b, zero), axis=0, keepdims=True)
        dglb = dglb_buf[...]
        gwg_ref[...] += lax.dot_general(geb_buf[...], dglb, (((0,), (0,)), ((), ())), preferred_element_type=F32)
        gl_buf[...] = lax.dot_general(dglb, wg, (((1,), (1,)), ((), ())), preferred_element_type=F32)

        def glu_pass_c(r, carry):
            rs = rows(r)
            dy_ref[rs, :] = (keep_buf[rs, :] + gl_buf[rs, :]) * _dgelu(y_ref[rs, :], th_buf[rs, :])
            return carry

        chunks(glu_pass_c, 0)

    col = lambda j: pl.BlockSpec((tm, dc), lambda i, j=j: (nt - 1 - i, j))
    prev = lambda j: pl.BlockSpec((halo, dc), lambda i, j=j: (jnp.maximum((nt - 1 - i) * blocks_per_tile - 1, 0), j))
    rev = lambda i: (nt - 1 - i, 0)
    fixed = lambda i: (0, 0)
    return _pallas(
        body, name="gate_bwd", grid=(nt,),
        in_specs=[col(0), col(1), col(2), col(3), col(5), prev(1), prev(2), pl.BlockSpec((tm, dssm), rev),
                  pl.BlockSpec((tm, dc + dssm), rev), pl.BlockSpec(conv_w.shape, fixed), pl.BlockSpec((1, dc), fixed),
                  pl.BlockSpec(w_glu.shape, fixed), pl.BlockSpec((1, dssm), fixed)],
        out_specs=[pl.BlockSpec((tm, 6 * dc), rev), pl.BlockSpec((tm, dssm), rev),
                   pl.BlockSpec((2 * SUBLANES, dc), fixed), pl.BlockSpec((dssm, dssm), fixed)],
        out_shape=[jax.ShapeDtypeStruct((t, 6 * dc), BF16), jax.ShapeDtypeStruct((t, dssm), F32),
                   jax.ShapeDtypeStruct((2 * SUBLANES, dc), F32), jax.ShapeDtypeStruct((dssm, dssm), F32)],
        scratch_shapes=[pltpu.VMEM((tm + halo, dc), F32), pltpu.VMEM((tm + halo, dc), F32), pltpu.VMEM((tm, dc), F32),
                        pltpu.VMEM((tm, dc), F32), pltpu.VMEM((tm, dssm), BF16), pltpu.VMEM((tm, dssm), BF16),
                        pltpu.VMEM((tm, dssm), F32), pltpu.VMEM((tm, dssm), F32)],
        compiler_params=_cp(2 * (6 * 4 * tm * dc + 8 * tm * dc + 12 * tm * dc + 4 * tm * dc + 6 * dssm * dssm)
                            + 28 * tm * dc + 8 * MIB, dimension_semantics=("arbitrary",)),
    )(proj, proj, proj, proj, proj, proj, proj, y, dmix, conv_w, conv_b, w_glu, b_glu)


def _allgather_flat(v, name):
    r, c = v.shape
    rels = [(dx, dy, dc) for dx in (0, 1) for dy in (0, 1) for dc in (0, 1)][1:]

    def body(v_ref, out_ref, send, recv):
        x, y, cc = _my_place()
        me = 4 * x + 2 * y + cc

        def peer(rel):
            dx, dy, dc = rel
            return (1 - x if dx else x, 1 - y if dy else y, 1 - cc if dc else cc)

        def rc(s, slot, to):
            return pltpu.make_async_remote_copy(src_ref=v_ref, dst_ref=out_ref.at[slot], send_sem=send.at[s],
                                                recv_sem=recv.at[s], device_id=to, device_id_type=MESH)

        sent = []
        for s, rel in enumerate(rels):
            cp = rc(s, me, peer(rel))
            cp.start()
            sent.append(cp)
        out_ref[me] = v_ref[...]
        for s, rel in enumerate(rels):
            px, py, pc = peer(rel)
            rc(s, 4 * px + 2 * py + pc, (px, py, pc)).wait_recv()
        for cp in sent:
            cp.wait_send()

    return _pallas(
        body, name=name, out_shape=jax.ShapeDtypeStruct((N_DEV, r, c), F32),
        in_specs=[pl.BlockSpec(memory_space=pltpu.VMEM)], out_specs=pl.BlockSpec(memory_space=pltpu.VMEM),
        scratch_shapes=[pltpu.SemaphoreType.DMA((N_DEV - 1,)), pltpu.SemaphoreType.DMA((N_DEV - 1,))],
        compiler_params=_cp((N_DEV + 2) * 4 * r * c + 8 * MIB),
    )(v)


def _rs_pair_exchange(grads, name):
    n = len(grads)

    def body(*refs):
        ins, outs = refs[:n], refs[n:2 * n]
        send, recv = refs[2 * n:]
        x, y, c = _my_place()
        cps = []
        for t in range(n):
            cp = pltpu.make_async_remote_copy(src_ref=ins[t].at[:, 1 - c], dst_ref=outs[t], send_sem=send.at[t],
                                              recv_sem=recv.at[t], device_id=(x, y, 1 - c), device_id_type=MESH)
            cp.start()
            cps.append(cp)
        for cp in cps:
            cp.wait()

    any_spec = pl.BlockSpec(memory_space=pl.ANY)
    return _pallas(
        body, name=name, in_specs=[any_spec] * n, out_specs=[any_spec] * n,
        out_shape=[jax.ShapeDtypeStruct((g.shape[0],) + g.shape[2:], g.dtype) for g in grads],
        scratch_shapes=[pltpu.SemaphoreType.DMA((n,)), pltpu.SemaphoreType.DMA((n,))],
        compiler_params=_cp(16 * MIB),
    )(*grads)


def _rs_pair_add(place, grad, got, name):
    nk, _, r2, c = grad.shape
    tr = min(TR_ELT, r2)
    other = lambda s, p: (p[1] + 1 + s) % nk

    def send_body(place_ref, g_ref, r_ref, o_ref):
        del place_ref
        o_ref[...] = (g_ref[...] + r_ref[...].astype(F32)).astype(BF16)

    to_send = _pallas(
        send_body, name=name + "_send",
        grid_spec=pltpu.PrefetchScalarGridSpec(
            num_scalar_prefetch=1, grid=(nk - 1, r2 // tr),
            in_specs=[pl.BlockSpec((None, None, tr, c), lambda s, i, p: (other(s, p), p[0], i, 0)),
                      pl.BlockSpec((None, tr, c), lambda s, i, p: (other(s, p), i, 0))],
            out_specs=pl.BlockSpec((None, tr, c), lambda s, i, p: (other(s, p), i, 0))),
        out_shape=jax.ShapeDtypeStruct((nk, r2, c), BF16),
        compiler_params=_cp(2 * 8 * tr * c + 8 * MIB),
    )(place, grad, got)

    def keep_body(place_ref, g_ref, r_ref, o_ref):
        del place_ref
        o_ref[...] = g_ref[...] + r_ref[...].astype(F32)

    to_keep = _pallas(
        keep_body, name=name + "_keep",
        grid_spec=pltpu.PrefetchScalarGridSpec(
            num_scalar_prefetch=1, grid=(r2 // tr,),
            in_specs=[pl.BlockSpec((None, None, tr, c), lambda i, p: (p[1], p[0], i, 0)),
                      pl.BlockSpec((None, tr, c), lambda i, p: (p[1], i, 0))],
            out_specs=pl.BlockSpec((tr, c), lambda i, p: (i, 0))),
        out_shape=jax.ShapeDtypeStruct((r2, c), F32),
        compiler_params=_cp(2 * 10 * tr * c + 8 * MIB),
    )(place, grad, got)
    return to_send, to_keep


_HBM_SPEC = pl.BlockSpec(memory_space=pltpu.HBM)
_SEM_SPEC = pl.BlockSpec(memory_space=pltpu.SEMAPHORE)
_DATAFLOW = pltpu.SideEffectType.DATAFLOW_SIDE_EFFECTING


def _split_copy_start(srcs, land_shapes, plan, n_sems, name, after=()):
    ns, nl, na = len(srcs), len(land_shapes), len(after)

    def body(*refs):
        src_refs, land_refs = refs[:ns], refs[ns:ns + nl]
        send, recv = refs[ns + nl + na], refs[ns + nl + na + 1]
        token = refs[-1]
        sends, _ = plan(src_refs, land_refs)
        for src, dst, to, si, ri in sends:
            pltpu.make_async_remote_copy(src_ref=src, dst_ref=dst, send_sem=send.at[si], recv_sem=recv.at[ri],
                                         device_id=to, device_id_type=MESH).start()
        token[...] = jnp.zeros_like(token)

    lands = [lax.empty(shp, dt) for shp, dt in land_shapes]
    through = [pltpu.HBM(a.shape, a.dtype) for a in srcs] + [pltpu.HBM(shp, dt) for shp, dt in land_shapes]
    out = pl.pallas_call(
        body, name=name,
        out_shape=(pltpu.SemaphoreType.DMA((n_sems,)), pltpu.SemaphoreType.DMA((n_sems,)), *through,
                   jax.ShapeDtypeStruct((SUBLANES, SLAB), F32)),
        in_specs=[_HBM_SPEC] * (ns + nl) + [pl.BlockSpec(memory_space=pl.ANY)] * na,
        out_specs=(_SEM_SPEC, _SEM_SPEC, *([_HBM_SPEC] * (ns + nl)), pl.BlockSpec(memory_space=pltpu.VMEM)),
        input_output_aliases={i: 2 + i for i in range(ns + nl)},
        compiler_params=pltpu.CompilerParams(has_side_effects=_DATAFLOW),
    )(*[pltpu.with_memory_space_constraint(a, pltpu.HBM) for a in (*srcs, *lands)], *after)
    return out[0], out[1], list(out[2:2 + ns]), list(out[2 + ns:2 + ns + nl]), out[-1]


def _split_copy_wait(send, recv, srcs, lands, plan, after, name):
    ns, nl, na = len(srcs), len(lands), len(after)

    def body(*refs):
        src_refs, land_refs = refs[:ns], refs[ns:ns + nl]
        send_ref, recv_ref = refs[ns + nl], refs[ns + nl + 1]
        sends, arrivals = plan(src_refs, land_refs)
        for src, dst, to, si, ri in sends:
            pltpu.make_async_remote_copy(src_ref=src, dst_ref=dst, send_sem=send_ref.at[si], recv_sem=recv_ref.at[ri],
                                         device_id=to, device_id_type=MESH).wait_send()
        for (src, _, to, si, _), (view, ri) in zip(sends, arrivals):
            pltpu.make_async_remote_copy(src_ref=view, dst_ref=view, send_sem=send_ref.at[si], recv_sem=recv_ref.at[ri],
                                         device_id=to, device_id_type=MESH).wait_recv()

    out = pl.pallas_call(
        body, name=name,
        out_shape=[pltpu.HBM(a.shape, a.dtype) for a in (*srcs, *lands)],
        in_specs=[_HBM_SPEC] * (ns + nl) + [_SEM_SPEC, _SEM_SPEC] + [pl.BlockSpec(memory_space=pl.ANY)] * na,
        out_specs=[_HBM_SPEC] * (ns + nl),
        input_output_aliases={i: i for i in range(ns + nl)},
        compiler_params=pltpu.CompilerParams(has_side_effects=_DATAFLOW),
    )(*srcs, *lands, send, recv, *after)
    return list(out[:ns]), list(out[ns:])


def _chip_exchange_plan(n):
    per = N_CHIPS - 1

    def plan(srcs, lands):
        x, y, c = _my_place()
        sends, arrivals = [], []
        for t in range(n):
            for j, (cx, cy) in enumerate(_other_chips(x, y)):
                sends.append((srcs[t].at[2 * cx + cy], lands[t].at[j], (cx, cy, c), t * per + j, t * per + j))
                arrivals.append((lands[t].at[j], t * per + j))
        return sends, arrivals

    return plan


def _gather_half_plan(n):
    per = N_CHIPS - 1

    def plan(srcs, lands):
        x, y, c = _my_place()
        k = 2 * x + y
        sends, arrivals = [], []
        for t in range(n):
            for j, (cx, cy) in enumerate(_other_chips(x, y)):
                sends.append((srcs[t].at[c], lands[t].at[k, c], (cx, cy, c), t * per + j, t * per + j))
                arrivals.append((lands[t].at[2 * cx + cy, c], t * per + j))
        return sends, arrivals

    return plan


def _gather_complete(own, landed, name):
    def body(own_ref, in_ref, out_ref, send, recv, local):
        x, y, c = _my_place()
        sib = (x, y, 1 - c)
        mine = pltpu.make_async_copy(own_ref, out_ref.at[2 * x + y], local)
        mine.start()
        sent = []
        for j, (cx, cy) in enumerate(_other_chips(x, y)):
            half = (2 * cx + cy, c)
            cp = pltpu.make_async_remote_copy(src_ref=in_ref.at[half], dst_ref=out_ref.at[half], send_sem=send.at[j],
                                              recv_sem=recv.at[j], device_id=sib, device_id_type=MESH)
            cp.start()
            sent.append(cp)
        for j, (cx, cy) in enumerate(_other_chips(x, y)):
            other = out_ref.at[2 * cx + cy, 1 - c]
            pltpu.make_async_remote_copy(src_ref=other, dst_ref=other, send_sem=send.at[j], recv_sem=recv.at[j],
                                         device_id=sib, device_id_type=MESH).wait_recv()
        for cp in sent:
            cp.wait_send()
        mine.wait()

    any_spec = pl.BlockSpec(memory_space=pl.ANY)
    per = N_CHIPS - 1
    return _pallas(
        body, name=name, in_specs=[any_spec, any_spec], out_specs=any_spec,
        out_shape=jax.ShapeDtypeStruct(landed.shape, landed.dtype), input_output_aliases={1: 0},
        scratch_shapes=[pltpu.SemaphoreType.DMA((per,)), pltpu.SemaphoreType.DMA((per,)), pltpu.SemaphoreType.DMA],
        compiler_params=_cp(16 * MIB),
    )(own, landed)


def _gather_direct_plan(n):
    per = 2 * (N_CHIPS - 1)

    def plan(srcs, lands):
        x, y, c = _my_place()
        k = 2 * x + y
        sends, arrivals = [], []
        for t in range(n):
            for j, (cx, cy) in enumerate(_other_chips(x, y)):
                for core in (0, 1):
                    sends.append((srcs[t].at[c], lands[t].at[k, c], (cx, cy, core),
                                  t * per + 2 * j + core, t * per + 2 * j + c))
                    arrivals.append((lands[t].at[2 * cx + cy, core], t * per + 2 * j + core))
        return sends, arrivals

    return plan


def _rs_chip_add(kept, got, name):
    r2, c = kept.shape
    tr = min(TR_ELT, r2)

    def body(s_ref, q_ref, o_ref):
        o_ref[...] = ((s_ref[...] + q_ref[0].astype(F32)) + q_ref[1].astype(F32)) + q_ref[2].astype(F32)

    blk = pl.BlockSpec((tr, c), lambda i: (i, 0))
    return _pallas(
        body, name=name, grid=(r2 // tr,),
        in_specs=[blk, pl.BlockSpec((N_CHIPS - 1, tr, c), lambda i: (0, i, 0))], out_specs=blk,
        out_shape=jax.ShapeDtypeStruct((r2, c), F32),
        compiler_params=_cp(2 * 14 * tr * c + 8 * MIB),
    )(kept, got)


def _rs_pair_share(halves, name):
    n = len(halves)

    def body(*refs):
        ins, outs = refs[:n], refs[n:2 * n]
        send, recv, local = refs[2 * n:]
        x, y, c = _my_place()
        cps, mine = [], []
        for t in range(n):
            lc = pltpu.make_async_copy(ins[t], outs[t].at[c], local.at[t])
            lc.start()
            mine.append(lc)
            cp = pltpu.make_async_remote_copy(src_ref=ins[t], dst_ref=outs[t].at[c], send_sem=send.at[t],
                                              recv_sem=recv.at[t], device_id=(x, y, 1 - c), device_id_type=MESH)
            cp.start()
            cps.append(cp)
        for t in range(n):
            other = outs[t].at[1 - c]
            pltpu.make_async_remote_copy(src_ref=other, dst_ref=other, send_sem=send.at[t], recv_sem=recv.at[t],
                                         device_id=(x, y, 1 - c), device_id_type=MESH).wait_recv()
        for cp in cps:
            cp.wait_send()
        for lc in mine:
            lc.wait()

    any_spec = pl.BlockSpec(memory_space=pl.ANY)
    return _pallas(
        body, name=name, in_specs=[any_spec] * n, out_specs=[any_spec] * n,
        out_shape=[jax.ShapeDtypeStruct((2,) + h.shape, F32) for h in halves],
        scratch_shapes=[pltpu.SemaphoreType.DMA((n,)), pltpu.SemaphoreType.DMA((n,)), pltpu.SemaphoreType.DMA((n,))],
        compiler_params=_cp(16 * MIB),
    )(*halves)


_PACK_TILE = SUBLANES * SLAB


def _pack(arrays):
    rows = []
    for a in arrays:
        flat = a.reshape(-1).astype(F32)
        padded = -(-flat.shape[0] // _PACK_TILE) * _PACK_TILE
        rows.append(jnp.pad(flat, (0, padded - flat.shape[0])).reshape(-1, SLAB))
    n_rows = sum(r.shape[0] for r in rows)
    if n_rows % (2 * SUBLANES):
        rows.append(jnp.zeros((SUBLANES, SLAB), F32))
    return jnp.concatenate(rows, axis=0)


def _unpack(packed, shapes):
    out, row = [], 0
    for shp in shapes:
        size = math.prod(shp)
        nrow = -(-size // _PACK_TILE) * SUBLANES
        out.append(packed[row:row + nrow].reshape(-1)[:size].reshape(shp))
        row += nrow
    return out


def kernel(x, norm_pre_g, w_in, conv_w, conv_b, ssm_a_re, ssm_a_im, ssm_log_dt, ssm_b_re, ssm_b_im, ssm_c_re, ssm_c_im, ssm_d, w_glu, b_glu, w_out, norm_post_g, loss_target, m_norm_pre_g, m_w_in, m_conv_w, m_conv_b, m_ssm_a_re, m_ssm_a_im, m_ssm_log_dt, m_ssm_b_re, m_ssm_b_im, m_ssm_c_re, m_ssm_c_im, m_ssm_d, m_w_glu, m_b_glu, m_w_out, m_norm_post_g, v_norm_pre_g, v_w_in, v_conv_w, v_conv_b, v_ssm_a_re, v_ssm_a_im, v_ssm_log_dt, v_ssm_b_re, v_ssm_b_im, v_ssm_c_re, v_ssm_c_im, v_ssm_d, v_w_glu, v_b_glu, v_w_out, v_norm_post_g):
    xs, tgt = x[0], loss_target[0]
    t, d = xs.shape
    dc = conv_b.shape[0]
    dssm = ssm_d.shape[0]
    g, p = ssm_a_re.shape
    h = SSM_H
    nq = dssm // SLAB
    n_shard = w_in.shape[1]
    steps = min(T_SCAN, t) // SUBLANES
    mx, my, mc = _my_place()
    chip = 2 * mx + my
    place = jnp.stack([mc, chip]).astype(jnp.int32)

    cw_cols = conv_w.shape[1]
    cw_pad = -(-cw_cols // SLAB) * SLAB
    cw_blk = jnp.zeros((SUBLANES, cw_pad), F32).at[:conv_w.shape[0], :cw_cols].set(conv_w)
    cw_all = _allgather_flat(cw_blk, "allgather_conv_w")
    conv_w_full = jnp.concatenate([cw_all[2 * k, :, :cw_cols] for k in range(N_CHIPS)], axis=1)

    halves = lambda a: a.reshape(2, a.shape[0] // 2, a.shape[1])
    win_half = halves(_cast_bf16(w_in, "cast_w_in", cw_all[0, :, :SLAB]))
    win_plan = _gather_half_plan(1)
    win_send, win_recv, win_srcs, win_lands, win_token = _split_copy_start(
        [win_half], [((N_CHIPS,) + win_half.shape, BF16)], win_plan, N_CHIPS - 1, "gather_w_in_start")
    behind_w_in = win_token[0:1, 0:1]

    small_names = ["norm_pre_g", "conv_w", "conv_b", "ssm_a_re", "ssm_a_im", "ssm_log_dt", "ssm_b_re", "ssm_b_im",
                   "ssm_c_re", "ssm_c_im", "ssm_d", "b_glu", "norm_post_g", "loss"]
    zeros_cw = jnp.zeros((conv_w.shape[0], dc), F32)
    one0 = jnp.zeros((1,), F32)
    hp = lambda b: b.transpose(0, 2, 1)
    small_w = dict(norm_pre_g=norm_pre_g, conv_w=zeros_cw, conv_b=conv_b, ssm_a_re=ssm_a_re, ssm_a_im=ssm_a_im,
                   ssm_log_dt=ssm_log_dt, ssm_b_re=hp(ssm_b_re), ssm_b_im=hp(ssm_b_im), ssm_c_re=ssm_c_re, ssm_c_im=ssm_c_im,
                   ssm_d=ssm_d, b_glu=b_glu, norm_post_g=norm_post_g, loss=one0)
    small_m = dict(norm_pre_g=m_norm_pre_g, conv_w=zeros_cw, conv_b=m_conv_b, ssm_a_re=m_ssm_a_re, ssm_a_im=m_ssm_a_im,
                   ssm_log_dt=m_ssm_log_dt, ssm_b_re=hp(m_ssm_b_re), ssm_b_im=hp(m_ssm_b_im), ssm_c_re=m_ssm_c_re,
                   ssm_c_im=m_ssm_c_im, ssm_d=m_ssm_d, b_glu=m_b_glu, norm_post_g=m_norm_post_g, loss=one0)
    small_v = dict(norm_pre_g=v_norm_pre_g, conv_w=zeros_cw, conv_b=v_conv_b, ssm_a_re=v_ssm_a_re, ssm_a_im=v_ssm_a_im,
                   ssm_log_dt=v_ssm_log_dt, ssm_b_re=hp(v_ssm_b_re), ssm_b_im=hp(v_ssm_b_im), ssm_c_re=v_ssm_c_re,
                   ssm_c_im=v_ssm_c_im, ssm_d=v_ssm_d, b_glu=v_b_glu, norm_post_g=v_norm_post_g, loss=one0)
    w_pack, m_pack, v_pack = [_pack([group[nm] for nm in small_names]) + behind_w_in
                              for group in (small_w, small_m, small_v)]
    side = [halves(_cast_bf16(w_out, "cast_w_out", win_token)), halves(_cast_bf16(w_glu, "cast_w_glu", win_token))]

    expand = jnp.tile(jnp.eye(p, dtype=F32), (1, h))
    b_re2, b_im2 = hp(ssm_b_re).reshape(g, h * p), hp(ssm_b_im).reshape(g, h * p)
    log_dt2 = ssm_log_dt.reshape(g, 1) + behind_w_in
    lbr, lbi, bbr2, bbi2, pw3 = _zoh_fwd(ssm_a_re, ssm_a_im, log_dt2, b_re2, b_im2, expand, steps)
    lam = jnp.stack([lbr.reshape(g * p), lbi.reshape(g * p)])
    pw = pw3.reshape(2, g * p)
    to_slab_b = lambda b2: _blockdiag(b2.reshape(nq, GROUPS_PER_SLAB, h, p))
    bq = jnp.concatenate([to_slab_b(bbr2), to_slab_b(bbi2)], axis=2).astype(BF16)
    to_slab_c = lambda c3: _blockdiag(c3.reshape(nq, GROUPS_PER_SLAB, h, p).transpose(0, 1, 3, 2))
    cq = jnp.concatenate([to_slab_c(ssm_c_re), to_slab_c(-ssm_c_im)], axis=1).astype(BF16)

    g_pre2, g_post2 = norm_pre_g.reshape(1, d), norm_post_g.reshape(1, d)
    conv_b2, b_glu2, d_skip2 = conv_b.reshape(1, dc), b_glu.reshape(1, dssm), ssm_d.reshape(1, dssm)
    u_block = 4 * dc // dssm

    hb = _prenorm(xs, g_pre2 + behind_w_in)
    proj_own = _inproj_own(place, hb, win_srcs[0].reshape(d, n_shard))
    win_own, win_landed = _split_copy_wait(win_send, win_recv, win_srcs, win_lands, win_plan,
                                           [proj_own, bq, cq, w_pack, m_pack, v_pack, *side], "gather_w_in_wait")
    win_g = _gather_complete(win_own[0], win_landed[0], "gather_w_in_complete")
    win_b = win_g.reshape(N_CHIPS, d, n_shard)
    side_plan = _gather_direct_plan(len(side))
    side_sems = 2 * (N_CHIPS - 1) * len(side)
    ag_send, ag_recv, ag_srcs, ag_lands, ag_token = _split_copy_start(
        side, [((N_CHIPS,) + a.shape, BF16) for a in side], side_plan, side_sems, "gather_side_weights_start",
        after=[win_g])
    proj = _inproj_rest(place, hb, win_b, proj_own, ag_token)
    y, cin = _ssm_fwd(proj, u_block, bq, cq, lam, pw, d_skip2)
    side_own, side_all = _split_copy_wait(ag_send, ag_recv, ag_srcs, ag_lands, side_plan, [cin],
                                          "gather_side_weights_wait")
    wout_g, wglu_g = [lax.dynamic_update_index_in_dim(all_, own, chip, 0) for own, all_ in zip(side_own, side_all)]
    wout_b = wout_g.reshape(dc + dssm, d)
    wglu_b = wglu_g.reshape(dssm, dssm)
    mix = _gate_fwd(proj, y, conv_w_full, conv_b2, wglu_b, b_glu2, dc)
    loss_blk, dout, dob, dmix, gg_post = _outproj(mix, wout_b, xs, tgt, g_post2)

    def reduce_start(grads, to_send, tags, group):
        got = _rs_pair_exchange(to_send, "rs_pair_exchange_" + group)
        sums = [_rs_pair_add(place, gt, rt, "rs_pair_add_" + tg) for gt, rt, tg in zip(grads, got, tags)]
        plan = _chip_exchange_plan(len(grads))
        started = _split_copy_start([s16 for s16, _ in sums],
                                    [((N_CHIPS - 1,) + s16.shape[1:], BF16) for s16, _ in sums], plan,
                                    (N_CHIPS - 1) * len(grads), "rs_chip_exchange_" + group + "_start")
        return plan, started, [s32 for _, s32 in sums]

    def reduce_finish(plan, started, sums32, tags, group, after):
        send, recv, srcs, lands, _ = started
        _, landed = _split_copy_wait(send, recv, srcs, lands, plan, after, "rs_chip_exchange_" + group + "_wait")
        mine = [_rs_chip_add(s32, qt, "rs_chip_add_" + tg) for s32, qt, tg in zip(sums32, landed, tags)]
        full = _rs_pair_share(mine, "rs_pair_share_" + group)
        return [f.reshape(2 * f.shape[1], f.shape[2]) for f in full]

    gw_out, gw_out16 = _matmul_tn(mix, dob, 1, "grad_w_out")
    dproj, dy, gsmall, gw_glu = _gate_bwd(proj, y, dmix, conv_w_full, conv_b2, wglu_b, b_glu2, dc)
    as_out = lambda a: a.reshape(N_CHIPS, 2, (dc + dssm) // (2 * N_CHIPS), d)
    gw_glu4 = gw_glu.reshape(N_CHIPS, 2, dssm // (2 * N_CHIPS), dssm)
    rs_a = reduce_start([as_out(gw_out), gw_glu4], [as_out(gw_out16), gw_glu4], ["w_out", "w_glu"], "a")
    dproj, gb_dense, gc_dense, glam, gd = _ssm_bwd(proj, u_block, dy, cin, dproj, bq, cq, lam, pw,
                                                   d_skip2 + rs_a[1][4][0:1, 0:1])
    gw_in, gw_in16 = _matmul_tn(hb, dproj, N_CHIPS, "grad_w_in")
    as_in = lambda a: a.reshape(N_CHIPS, 2, d // 2, n_shard)
    rs_b = reduce_start([as_in(gw_in)], [as_in(gw_in16)], ["w_in"], "b")
    gx, gg_pre = _dh_prenorm_bwd(dproj, win_b, xs, dout, g_pre2 + rs_b[1][4][0:1, 0:1])

    gb4 = gb_dense.reshape(nq, SLAB, 2, SLAB_STATES)
    g_bbr2 = _blockdiag_take(gb4[:, :, 0, :], h, p).reshape(g, h * p)
    g_bbi2 = _blockdiag_take(gb4[:, :, 1, :], h, p).reshape(g, h * p)
    gc4 = gc_dense.reshape(nq, SLAB, 2, SLAB_STATES)
    g_c_re = _blockdiag_take(gc4[:, :, 0, :], h, p).reshape(g, h, p)
    g_c_im = -_blockdiag_take(gc4[:, :, 1, :], h, p).reshape(g, h, p)
    g_a_re, g_a_im, g_ld, g_b_re2, g_b_im2 = _zoh_bwd(
        ssm_a_re, ssm_a_im, log_dt2, b_re2, b_im2, expand,
        glam[0].reshape(g, p), glam[1].reshape(g, p), g_bbr2, g_bbi2)

    small_g = {
        "norm_pre_g": gg_pre[0], "conv_w": gsmall[0:3], "conv_b": gsmall[3], "ssm_a_re": g_a_re, "ssm_a_im": g_a_im,
        "ssm_log_dt": g_ld.reshape(g), "ssm_b_re": g_b_re2.reshape(g, h, p), "ssm_b_im": g_b_im2.reshape(g, h, p),
        "ssm_c_re": g_c_re, "ssm_c_im": g_c_im, "ssm_d": gd[0], "b_glu": gsmall[4], "norm_post_g": gg_post[0],
        "loss": loss_blk[0, 0:1],
    }
    shapes = [small_w[nm].shape for nm in small_names]
    g_pack = _allreduce_small(_pack([small_g[nm] for nm in small_names]))
    packs = _adamw_small(g_pack, w_pack, m_pack, v_pack)
    sg, sd, sm, sv = [dict(zip(small_names, _unpack(pk, shapes))) for pk in (g_pack, *packs)]
    for group in (sg, sd, sm, sv):
        group.update(ssm_b_re=hp(group["ssm_b_re"]), ssm_b_im=hp(group["ssm_b_im"]))
    loss = sg["loss"][0]

    g_cw = lax.dynamic_slice_in_dim(sg["conv_w"], chip * cw_cols, cw_cols, axis=1)
    pad_cw = lambda a: jnp.zeros((SUBLANES, cw_pad), F32).at[:a.shape[0], :cw_cols].set(a)
    cut_cw = lambda a: a[:conv_w.shape[0], :cw_cols]
    _, d_cw, m_cw, v_cw = [cut_cw(a) for a in _adamw(pad_cw(conv_w), pad_cw(g_cw), pad_cw(m_conv_w), pad_cw(v_conv_w),
                                                     "adamw_conv_w")]

    g_wout, g_wglu = reduce_finish(*rs_a, ["w_out", "w_glu"], "a", [d_cw, packs[0]])
    (g_win,) = reduce_finish(*rs_b, ["w_in"], "b", [g_wout])
    g_win, d_win, m_win, v_win = _adamw(w_in, g_win, m_w_in, v_w_in, "adamw_w_in")
    g_wout, d_wout, m_wout, v_wout = _adamw(w_out, g_wout, m_w_out, v_w_out, "adamw_w_out")
    g_wglu, d_wglu, m_wglu, v_wglu = _adamw(w_glu, g_wglu, m_w_glu, v_w_glu, "adamw_w_glu")

    order = ["norm_pre_g", "w_in", "conv_w", "conv_b", "ssm_a_re", "ssm_a_im", "ssm_log_dt", "ssm_b_re", "ssm_b_im",
             "ssm_c_re", "ssm_c_im", "ssm_d", "w_glu", "b_glu", "w_out", "norm_post_g"]
    grads, deltas, new_m, new_v = dict(sg), dict(sd), dict(sm), dict(sv)
    grads.update(w_in=g_win, w_out=g_wout, w_glu=g_wglu, conv_w=g_cw)
    deltas.update(w_in=d_win, w_out=d_wout, w_glu=d_wglu, conv_w=d_cw)
    new_m.update(w_in=m_win, w_out=m_wout, w_glu=m_wglu, conv_w=m_cw)
    new_v.update(w_in=v_win, w_out=v_wout, w_glu=v_wglu, conv_w=v_cw)
    return (loss, gx[None], *[grads[nm] for nm in order], *[deltas[nm] for nm in order],
            *[new_m[nm] for nm in order], *[new_v[nm] for nm in order])
```

```python
import math

import jax
import jax.numpy as jnp
from jax import lax
from jax.experimental import pallas as pl
from jax.experimental.pallas import tpu as pltpu

F32 = jnp.float32
BF16 = jnp.bfloat16
MESH = pl.DeviceIdType.MESH

EPS = 1e-6
SSM_H = 16
SSM_P = 64
GROUPS_PER_SLAB = 8
SLAB = 128
SLAB_STATES = GROUPS_PER_SLAB * SSM_P
N_CHIPS = 4
N_DEV = 8

ADAM_LR = 0.001
ADAM_B1 = 0.9
ADAM_B2 = 0.999
ADAM_EPS = 1e-08
ADAM_WD = 0.01
ADAM_STEP = 10

MIB = 1024 * 1024
VMEM_CAP = 48 * MIB
SUBLANES = 8

TM_NORM = 512
TM_PROJ = 512
TM_GATE = 256
TM_OUT = 256
TM_DH = 256
T_SCAN = 256
TK_TN = 1024
TM_TN = 1024
TR_ELT = 256


def _cp(vmem_bytes, **kw):
    return pltpu.CompilerParams(vmem_limit_bytes=int(min(VMEM_CAP, max(16 * MIB, vmem_bytes))), **kw)


def _pallas(body, **kw):
    if "grid" not in kw and "grid_spec" not in kw:
        return pl.pallas_call(body, **kw)
    pin = lambda s: pltpu.HBM(s.shape, s.dtype) if isinstance(s, jax.ShapeDtypeStruct) else s
    out_shape = kw.pop("out_shape")
    out_shape = [pin(s) for s in out_shape] if isinstance(out_shape, (list, tuple)) else pin(out_shape)
    call = pl.pallas_call(body, out_shape=out_shape, **kw)

    def run(*args):
        return call(*[pltpu.with_memory_space_constraint(a, pltpu.HBM) if jnp.issubdtype(a.dtype, jnp.floating) else a
                      for a in args])

    return run


def _my_place():
    return lax.axis_index("x"), lax.axis_index("y"), lax.axis_index("c")


def _other_chips(x, y):
    return [(1 - x, y), (x, 1 - y), (1 - x, 1 - y)]


def _silu(z):
    s = jax.nn.sigmoid(z)
    return z * s, s


def _dsilu(z, s):
    return s * (1.0 + z * (1.0 - s))


_GELU_K = math.sqrt(2.0 / math.pi)
_GELU_C = 0.044715


def _gelu(y):
    th = jnp.tanh(_GELU_K * (y + _GELU_C * y * y * y))
    return 0.5 * y * (1.0 + th), th


def _dgelu(y, th):
    return 0.5 * (1.0 + th) + 0.5 * y * (1.0 - th * th) * _GELU_K * (1.0 + 3.0 * _GELU_C * y * y)


def _cast_bf16(w, name, after=None):
    r, c = w.shape
    tr = min(TR_ELT, r)
    extra = [] if after is None else [after]

    def body(w_ref, *rest):
        rest[-1][...] = w_ref[...].astype(BF16)

    return _pallas(
        body, name=name, grid=(r // tr,),
        in_specs=[pl.BlockSpec((tr, c), lambda i: (i, 0))] + [pl.BlockSpec((SUBLANES, SLAB), lambda i: (0, 0))] * len(extra),
        out_specs=pl.BlockSpec((tr, c), lambda i: (i, 0)),
        out_shape=jax.ShapeDtypeStruct((r, c), BF16),
        compiler_params=_cp(12 * tr * c),
    )(w, *extra)


def _prenorm(x, g):
    t, d = x.shape
    tm = min(TM_NORM, t)

    def body(x_ref, g_ref, h_ref):
        xv = x_ref[...]
        r = lax.rsqrt(jnp.mean(xv * xv, axis=-1, keepdims=True) + EPS)
        h_ref[...] = (xv * r * g_ref[...]).astype(BF16)

    return _pallas(
        body, name="prenorm", grid=(t // tm,),
        in_specs=[pl.BlockSpec((tm, d), lambda i: (i, 0)), pl.BlockSpec((1, d), lambda i: (0, 0))],
        out_specs=pl.BlockSpec((tm, d), lambda i: (i, 0)),
        out_shape=jax.ShapeDtypeStruct((t, d), BF16),
        compiler_params=_cp(20 * tm * d),
    )(x, g)


def _inproj_own(place, a, w_own):
    t, k = a.shape
    n = w_own.shape[1]
    tm = min(TM_PROJ, t)

    def body(place_ref, a_ref, b_ref, o_ref):
        del place_ref
        o_ref[...] = jnp.dot(a_ref[...], b_ref[...], preferred_element_type=F32)

    return _pallas(
        body, name="inproj_own",
        grid_spec=pltpu.PrefetchScalarGridSpec(
            num_scalar_prefetch=1, grid=(t // tm,),
            in_specs=[pl.BlockSpec((tm, k), lambda i, p: (i, 0)), pl.BlockSpec((k, n), lambda i, p: (0, 0))],
            out_specs=pl.BlockSpec((tm, n), lambda i, p: (i, p[1]))),
        out_shape=jax.ShapeDtypeStruct((t, N_CHIPS * n), F32),
        compiler_params=_cp(2 * (2 * tm * k + 2 * k * n + 4 * tm * n) + 4 * MIB),
    )(place, a, w_own)


def _inproj_rest(place, a, b, partial, after):
    t, k = a.shape
    nb, _, n = b.shape
    tm = min(TM_PROJ, t)
    shard = lambda s, p: (p[1] + 1 + s) % nb

    def body(place_ref, a_ref, b_ref, after_ref, partial_ref, o_ref):
        del place_ref, after_ref, partial_ref
        o_ref[...] = jnp.dot(a_ref[...], b_ref[...], preferred_element_type=F32)

    return _pallas(
        body, name="inproj_rest",
        grid_spec=pltpu.PrefetchScalarGridSpec(
            num_scalar_prefetch=1, grid=(nb - 1, t // tm),
            in_specs=[pl.BlockSpec((tm, k), lambda s, i, p: (i, 0)),
                      pl.BlockSpec((None, k, n), lambda s, i, p: (shard(s, p), 0, 0)),
                      pl.BlockSpec((SUBLANES, SLAB), lambda s, i, p: (0, 0)),
                      pl.BlockSpec(memory_space=pl.ANY)],
            out_specs=pl.BlockSpec((tm, n), lambda s, i, p: (i, shard(s, p)))),
        out_shape=jax.ShapeDtypeStruct((t, nb * n), F32),
        input_output_aliases={4: 0},
        compiler_params=_cp(2 * (2 * tm * k + 2 * k * n + 4 * tm * n) + 4 * MIB),
    )(place, a, b, after, partial)


def _matmul_tn(a, b, nb, name):
    t, m = a.shape
    n = b.shape[1] // nb
    tk = min(TK_TN, t)
    tma = min(TM_TN, m)

    def body(a_ref, b_ref, o_ref, o16_ref):
        k = pl.program_id(2)

        @pl.when(k == 0)
        def _():
            o_ref[...] = jnp.zeros_like(o_ref)

        o_ref[...] += lax.dot_general(a_ref[...], b_ref[...], (((0,), (0,)), ((), ())), preferred_element_type=F32)

        @pl.when(k == t // tk - 1)
        def _():
            o16_ref[...] = o_ref[...].astype(BF16)

    blk = pl.BlockSpec((None, tma, n), lambda j, i, k: (j, i, 0))
    return _pallas(
        body, name=name, grid=(nb, m // tma, t // tk),
        in_specs=[pl.BlockSpec((tk, tma), lambda j, i, k: (k, i)), pl.BlockSpec((tk, n), lambda j, i, k: (k, j))],
        out_specs=[blk, blk],
        out_shape=[jax.ShapeDtypeStruct((nb, m, n), F32), jax.ShapeDtypeStruct((nb, m, n), BF16)],
        compiler_params=_cp(2 * (2 * tk * tma + 2 * tk * n + 6 * tma * n) + 8 * MIB),
    )(a, b)


def _outproj(mix, w_out, x, tgt, g_post):
    t, dm = mix.shape
    d = w_out.shape[1]
    tm = min(TM_OUT, t)

    rb = 2 * SUBLANES
    assert tm % rb == 0

    def body(mix_ref, w_ref, x_ref, t_ref, g_ref, loss_ref, dout_ref, do_ref, dmix_ref, gg_ref, o_buf):
        @pl.when(pl.program_id(0) == 0)
        def _():
            loss_ref[...] = jnp.zeros_like(loss_ref)
            gg_ref[...] = jnp.zeros_like(gg_ref)

        w = w_ref[...]
        o_buf[...] = jnp.dot(mix_ref[...], w, preferred_element_type=F32)
        g = g_ref[...]
        fold = lambda v: v[0:SUBLANES] + v[SUBLANES:rb]
        sq_err = jnp.zeros((SUBLANES, d), F32)
        gg = jnp.zeros((SUBLANES, d), F32)
        for c in range(tm // rb):
            rs = pl.ds(c * rb, rb)
            o = o_buf[rs, :]
            r = lax.rsqrt(jnp.mean(o * o, axis=-1, keepdims=True) + EPS)
            nh = o * r
            e = x_ref[rs, :] + nh * g - t_ref[rs, :]
            sq_err = sq_err + fold(e * e)
            dout = e * (1.0 / d)
            dout_ref[rs, :] = dout
            gg = gg + fold(dout * nh)
            dn = dout * g
            do = r * (dn - nh * jnp.mean(dn * nh, axis=-1, keepdims=True))
            do_ref[rs, :] = do.astype(BF16)
        loss_ref[...] += jnp.sum(sq_err) * (0.5 / d)
        gg_ref[0:1, :] += jnp.sum(gg, axis=0, keepdims=True)
        dmix_ref[...] = lax.dot_general(do_ref[...], w, (((1,), (1,)), ((), ())), preferred_element_type=F32)

    row = lambda i: (i, 0)
    fixed = lambda i: (0, 0)
    return _pallas(
        body, name="outproj", grid=(t // tm,),
        in_specs=[pl.BlockSpec((tm, dm), row), pl.BlockSpec((dm, d), fixed), pl.BlockSpec((tm, d), row),
                  pl.BlockSpec((tm, d), row), pl.BlockSpec((1, d), fixed)],
        out_specs=[pl.BlockSpec((SUBLANES, SLAB), fixed), pl.BlockSpec((tm, d), row), pl.BlockSpec((tm, d), row),
                   pl.BlockSpec((tm, dm), row), pl.BlockSpec((SUBLANES, d), fixed)],
        out_shape=[jax.ShapeDtypeStruct((SUBLANES, SLAB), F32), jax.ShapeDtypeStruct((t, d), F32),
                   jax.ShapeDtypeStruct((t, d), BF16), jax.ShapeDtypeStruct((t, dm), F32),
                   jax.ShapeDtypeStruct((SUBLANES, d), F32)],
        scratch_shapes=[pltpu.VMEM((tm, d), F32)],
        compiler_params=_cp(2 * (2 * dm * d + tm * (2 * dm + 4 * d * 3 + 2 * d + 4 * dm)) + 4 * tm * d + 12 * MIB),
    )(mix, w_out, x, tgt, g_post)


def _dh_prenorm_bwd(dproj, w_in, x, dout, g_pre):
    t, d = x.shape
    nb, _, n = w_in.shape
    tm = min(TM_DH, t)

    def body(dp_ref, w_ref, x_ref, dout_ref, g_ref, dx_ref, gg_ref):
        @pl.when(pl.program_id(0) == 0)
        def _():
            gg_ref[...] = jnp.zeros_like(gg_ref)

        dh = None
        for k in range(nb):
            part = lax.dot_general(dp_ref[:, k * n:(k + 1) * n], w_ref[k], (((1,), (1,)), ((), ())),
                                   preferred_element_type=F32)
            dh = part if dh is None else dh + part
        xv = x_ref[...]
        r = lax.rsqrt(jnp.mean(xv * xv, axis=-1, keepdims=True) + EPS)
        xh = xv * r
        gg_ref[0:1, :] += jnp.sum(dh * xh, axis=0, keepdims=True)
        dg = dh * g_ref[...]
        dx_ref[...] = dout_ref[...] + r * (dg - xh * jnp.mean(dg * xh, axis=-1, keepdims=True))

    row = lambda i: (i, 0)
    fixed = lambda i: (0, 0)
    w_spec = pl.BlockSpec(w_in.shape, lambda i: (0, 0, 0), pipeline_mode=pl.Buffered(1))
    return _pallas(
        body, name="dh_prenorm_bwd", grid=(t // tm,),
        in_specs=[pl.BlockSpec((tm, nb * n), row), w_spec,
                  pl.BlockSpec((tm, d), row), pl.BlockSpec((tm, d), row), pl.BlockSpec((1, d), fixed)],
        out_specs=[pl.BlockSpec((tm, d), row), pl.BlockSpec((SUBLANES, d), fixed)],
        out_shape=[jax.ShapeDtypeStruct((t, d), F32), jax.ShapeDtypeStruct((SUBLANES, d), F32)],
        compiler_params=_cp(2 * nb * d * n + 2 * (2 * tm * nb * n + 12 * tm * d) + 16 * tm * d + 4 * MIB),
    )(dproj, w_in, x, dout, g_pre)


def _adamw(w, g, m, v, name):
    r, c = w.shape
    tr = min(TR_ELT, r)
    c1 = 1.0 - ADAM_B1 ** ADAM_STEP
    c2 = 1.0 - ADAM_B2 ** ADAM_STEP

    def body(w_ref, g_ref, m_ref, v_ref, go_ref, d_ref, mo_ref, vo_ref):
        gv = g_ref[...]
        go_ref[...] = gv
        mn = ADAM_B1 * m_ref[...] + (1.0 - ADAM_B1) * gv
        vn = ADAM_B2 * v_ref[...] + (1.0 - ADAM_B2) * (gv * gv)
        d_ref[...] = -ADAM_LR * ((mn / c1) / (jnp.sqrt(vn / c2) + ADAM_EPS) + ADAM_WD * w_ref[...])
        mo_ref[...] = mn
        vo_ref[...] = vn

    spec = pl.BlockSpec((tr, c), lambda i: (i, 0))
    sds = jax.ShapeDtypeStruct((r, c), F32)
    return _pallas(
        body, name=name, grid=(r // tr,), in_specs=[spec] * 4, out_specs=[spec] * 4, out_shape=[sds] * 4,
        compiler_params=_cp(2 * 8 * 4 * tr * c + 8 * MIB),
    )(w, g, m, v)


def _adamw_small(g, w, m, v):
    r, c = w.shape
    c1 = 1.0 - ADAM_B1 ** ADAM_STEP
    c2 = 1.0 - ADAM_B2 ** ADAM_STEP

    def body(g_ref, w_ref, m_ref, v_ref, d_ref, mo_ref, vo_ref):
        gv = g_ref[...]
        mn = ADAM_B1 * m_ref[...] + (1.0 - ADAM_B1) * gv
        vn = ADAM_B2 * v_ref[...] + (1.0 - ADAM_B2) * (gv * gv)
        d_ref[...] = -ADAM_LR * ((mn / c1) / (jnp.sqrt(vn / c2) + ADAM_EPS) + ADAM_WD * w_ref[...])
        mo_ref[...] = mn
        vo_ref[...] = vn

    sds = jax.ShapeDtypeStruct((r, c), F32)
    return _pallas(
        body, name="adamw_small", out_shape=[sds] * 3,
        compiler_params=_cp(12 * 4 * r * c + 8 * MIB),
    )(g, w, m, v)


def _allreduce_small(v):
    r, c = v.shape
    assert r % (2 * SUBLANES) == 0
    h = r // 2

    def body(v_ref, out_ref, got_ref, chip_ref, slots_ref, send, recv):
        x, y, cc = _my_place()
        k = 2 * x + y
        sib = (x, y, 1 - cc)

        def rc(s, src, dst, to):
            return pltpu.make_async_remote_copy(src_ref=src, dst_ref=dst, send_sem=send.at[s], recv_sem=recv.at[s],
                                                device_id=to, device_id_type=MESH)

        pair = rc(0, v_ref, got_ref, sib)
        pair.start()
        pair.wait()
        chip_ref[...] = v_ref[...] + got_ref[...]
        mine = pl.ds(pl.multiple_of(cc * h, SUBLANES), h)
        theirs = pl.ds(pl.multiple_of((1 - cc) * h, SUBLANES), h)
        chips = _other_chips(x, y)
        sent = []
        for j, (cx, cy) in enumerate(chips):
            cp = rc(1 + j, chip_ref.at[mine], slots_ref.at[k], (cx, cy, cc))
            cp.start()
            sent.append(cp)
        slots_ref[k] = chip_ref[mine, :]
        for j, (cx, cy) in enumerate(chips):
            rc(1 + j, chip_ref.at[mine], slots_ref.at[2 * cx + cy], (cx, cy, cc)).wait_recv()
        total = slots_ref[0]
        for kk in range(1, N_CHIPS):
            total = total + slots_ref[kk]
        out_ref[mine, :] = total
        for cp in sent:
            cp.wait_send()
        share = rc(N_CHIPS, out_ref.at[mine], out_ref.at[mine], sib)
        share.start()
        rc(N_CHIPS, out_ref.at[theirs], out_ref.at[theirs], sib).wait_recv()
        share.wait_send()

    vm = pl.BlockSpec(memory_space=pltpu.VMEM)
    return _pallas(
        body, name="allreduce_small", out_shape=jax.ShapeDtypeStruct((r, c), F32), in_specs=[vm], out_specs=vm,
        scratch_shapes=[pltpu.VMEM((r, c), F32), pltpu.VMEM((r, c), F32), pltpu.VMEM((N_CHIPS, h, c), F32),
                        pltpu.SemaphoreType.DMA((N_CHIPS + 1,)), pltpu.SemaphoreType.DMA((N_CHIPS + 1,))],
        compiler_params=_cp(6 * 4 * r * c + 8 * MIB),
    )(v)


def _zoh(a_re, a_im, log_dt, b_re2, b_im2, expand):
    dt = jnp.exp(log_dt)
    mag = jnp.exp(a_re * dt)
    lbr, lbi = mag * jnp.cos(a_im * dt), mag * jnp.sin(a_im * dt)
    nr, ni = lbr - 1.0, lbi
    den = a_re * a_re + a_im * a_im
    qr = (nr * a_re + ni * a_im) / den
    qi = (ni * a_re - nr * a_im) / den
    qr2 = jnp.dot(qr, expand, precision=lax.Precision.HIGHEST, preferred_element_type=F32)
    qi2 = jnp.dot(qi, expand, precision=lax.Precision.HIGHEST, preferred_element_type=F32)
    return lbr, lbi, qr2 * b_re2 - qi2 * b_im2, qr2 * b_im2 + qi2 * b_re2


def _zoh_fwd(a_re, a_im, log_dt, b_re2, b_im2, expand, power):
    g, p = a_re.shape

    def body(ar_ref, ai_ref, ld_ref, br_ref, bi_ref, e_ref, lbr_ref, lbi_ref, bbr_ref, bbi_ref, pw_ref):
        ar, ai, ld = ar_ref[...], ai_ref[...], ld_ref[...]
        lbr, lbi, bbr, bbi = _zoh(ar, ai, ld, br_ref[...], bi_ref[...], e_ref[...])
        lbr_ref[...], lbi_ref[...], bbr_ref[...], bbi_ref[...] = lbr, lbi, bbr, bbi
        dt = jnp.exp(ld) * float(power)
        mag = jnp.exp(ar * dt)
        pw_ref[0] = mag * jnp.cos(ai * dt)
        pw_ref[1] = mag * jnp.sin(ai * dt)

    gp = jax.ShapeDtypeStruct((g, p), F32)
    gph = jax.ShapeDtypeStruct(b_re2.shape, F32)
    return _pallas(
        body, name="zoh_fwd", out_shape=[gp, gp, gph, gph, jax.ShapeDtypeStruct((2, g, p), F32)],
        compiler_params=_cp(16 * MIB),
    )(a_re, a_im, log_dt, b_re2, b_im2, expand)


def _zoh_bwd(a_re, a_im, log_dt, b_re2, b_im2, expand, g_lbr, g_lbi, g_bbr, g_bbi):
    def body(ar_ref, ai_ref, ld_ref, br_ref, bi_ref, e_ref, c0, c1, c2, c3, gar, gai, gld, gbr, gbi):
        e = e_ref[...]
        _, vjp = jax.vjp(lambda a, b, c, d, f: _zoh(a, b, c, d, f, e),
                         ar_ref[...], ai_ref[...], ld_ref[...], br_ref[...], bi_ref[...])
        gar[...], gai[...], gld[...], gbr[...], gbi[...] = vjp((c0[...], c1[...], c2[...], c3[...]))

    sds = lambda a: jax.ShapeDtypeStruct(a.shape, F32)
    return _pallas(
        body, name="zoh_bwd", out_shape=[sds(a_re), sds(a_im), sds(log_dt), sds(b_re2), sds(b_im2)],
        compiler_params=_cp(16 * MIB),
    )(a_re, a_im, log_dt, b_re2, b_im2, expand, g_lbr, g_lbi, g_bbr, g_bbi)


def _blockdiag(blocks):
    nq, _, r, c = blocks.shape
    eye = jnp.eye(GROUPS_PER_SLAB, dtype=blocks.dtype)
    out = blocks[:, :, :, None, :] * eye[None, :, None, :, None]
    return out.reshape(nq, GROUPS_PER_SLAB * r, GROUPS_PER_SLAB * c)


def _blockdiag_take(dense, r, c):
    nq = dense.shape[0]
    d6 = dense.reshape(nq, GROUPS_PER_SLAB, r, 2, GROUPS_PER_SLAB, c)
    return jnp.stack([d6[:, i, :, :, i, :] for i in range(GROUPS_PER_SLAB)], axis=1)


def _scan_slab(s_ref, row0, q, lam_ref, pw_ref, car_ref, steps, reverse, prev_ref=None, prev_row0=0, glam_ref=None):
    sign = -1.0 if reverse else 1.0
    half = SLAB_STATES // SLAB
    cols = [(q * 2 * half + m, q * 2 * half + half + m, q * SLAB_STATES + m * SLAB) for m in range(half)]
    nm = len(cols)
    full = (SUBLANES, SLAB)
    lam = [(jnp.broadcast_to(lam_ref[0:1, pl.ds(cl, SLAB)], full),
            jnp.broadcast_to(sign * lam_ref[1:2, pl.ds(cl, SLAB)], full)) for (_, _, cl) in cols]

    def step_rows(jj, base):
        j = (steps - 1 - jj) if reverse else jj
        return j, pl.ds(base + j * SUBLANES, SUBLANES)

    def pass1(jj, car):
        _, rows = step_rows(jj, row0)
        out = []
        for m, (cr, ci, _) in enumerate(cols):
            sr, si = car[2 * m], car[2 * m + 1]
            lr, li = lam[m]
            nr = lr * sr - li * si + s_ref[cr, rows, :]
            ni = lr * si + li * sr + s_ref[ci, rows, :]
            s_ref[cr, rows, :] = nr
            s_ref[ci, rows, :] = ni
            out += [nr, ni]
        return tuple(out)

    def run_steps(step_fn, carry):
        for jj in range(steps):
            carry = step_fn(jj, carry)
        return carry

    ends = run_steps(pass1, tuple(jnp.zeros(full, F32) for _ in range(2 * nm)))

    entry = []
    for m, (cr, ci, cl) in enumerate(cols):
        ljr = pw_ref[0:1, pl.ds(cl, SLAB)]
        lji = sign * pw_ref[1:2, pl.ds(cl, SLAB)]
        c_r = car_ref[cr, 0:1, :]
        c_i = car_ref[ci, 0:1, :]
        rows_r, rows_i = [None] * SUBLANES, [None] * SUBLANES
        order = range(SUBLANES - 1, -1, -1) if reverse else range(SUBLANES)
        for b in order:
            rows_r[b], rows_i[b] = c_r, c_i
            e_r, e_i = ends[2 * m][b:b + 1], ends[2 * m + 1][b:b + 1]
            c_r, c_i = ljr * c_r - lji * c_i + e_r, ljr * c_i + lji * c_r + e_i
        car_ref[cr, 0:1, :] = c_r
        car_ref[ci, 0:1, :] = c_i
        entry.append((jnp.concatenate(rows_r, axis=0), jnp.concatenate(rows_i, axis=0)))

    def pass2(jj, carry):
        j, rows = step_rows(jj, row0)
        decayed, acc = carry[:2 * nm], carry[2 * nm:]
        out_d, out_a = [], []
        for m, (cr, ci, cl) in enumerate(cols):
            lr, li = lam[m]
            dr, di = decayed[2 * m], decayed[2 * m + 1]
            dr, di = lr * dr - li * di, lr * di + li * dr
            nr = s_ref[cr, rows, :] + dr
            ni = s_ref[ci, rows, :] + di
            s_ref[cr, rows, :] = nr
            s_ref[ci, rows, :] = ni
            out_d += [dr, di]
            if prev_ref is not None:
                prow = pl.ds(prev_row0 + (j - 1) * SUBLANES, SUBLANES)
                qr = prev_ref[cr, prow, :]
                qi = prev_ref[ci, prow, :]
                out_a += [acc[2 * m] + (nr * qr + ni * qi), acc[2 * m + 1] + (ni * qr - nr * qi)]
        return tuple(out_d) + tuple(out_a)

    n_acc = 2 * nm if prev_ref is not None else 0
    init = tuple(e for pair in entry for e in pair) + tuple(jnp.zeros(full, F32) for _ in range(n_acc))
    accs = run_steps(pass2, init)[2 * nm:]
    if prev_ref is not None:
        for m, (_, _, cl) in enumerate(cols):
            glam_ref[0:1, pl.ds(cl, SLAB)] += jnp.sum(accs[2 * m], axis=0, keepdims=True)
            glam_ref[1:2, pl.ds(cl, SLAB)] += jnp.sum(accs[2 * m + 1], axis=0, keepdims=True)


def _permute_rows_f32(perm_bf16, v):
    hi = v.astype(BF16)
    lo = (v - hi.astype(F32)).astype(BF16)
    return (jnp.dot(perm_bf16, hi, preferred_element_type=F32) + jnp.dot(perm_bf16, lo, preferred_element_type=F32))


def _put_slab(s_ref, rows, q, val):
    per = 2 * SLAB_STATES // SLAB
    for i in range(per):
        s_ref[q * per + i, rows, :] = val[:, i * SLAB:(i + 1) * SLAB]


def _get_slab(s_ref, rows, q):
    per = 2 * SLAB_STATES // SLAB
    return jnp.concatenate([s_ref[q * per + i, rows, :] for i in range(per)], axis=1)


def _step_major_perm(tt):
    r = jnp.arange(tt)
    held = (r % SUBLANES) * (tt // SUBLANES) + r // SUBLANES
    return held[:, None] == r[None, :]


def _ssm_fwd(proj, u_block, bq, cq, lam, pw, d_skip):
    t = proj.shape[0]
    nq, ds, w2 = bq.shape
    assert ds == SLAB and w2 == 2 * SLAB_STATES
    dssm = nq * SLAB
    width = nq * w2
    ntile = width // SLAB
    tt = min(T_SCAN, t)
    steps = tt // SUBLANES
    perm = _step_major_perm(tt)
    pm, pmt = perm.astype(BF16), perm.T.astype(BF16)

    def body(u_ref, pm_ref, pmt_ref, bq_ref, cq_ref, lam_ref, pw_ref, d_ref, y_ref, cin_ref, s_ref, car_ref):
        @pl.when(pl.program_id(0) == 0)
        def _():
            car_ref[...] = jnp.zeros_like(car_ref)

        cin_ref[...] = jnp.broadcast_to(car_ref[:, 0:1, :], cin_ref.shape)
        u = u_ref[...]
        ub = jnp.dot(pm_ref[...], u.astype(BF16), preferred_element_type=F32).astype(BF16)
        everything = slice(None)
        for q in range(nq):
            _put_slab(s_ref, everything, q,
                      jnp.dot(ub[:, q * SLAB:(q + 1) * SLAB], bq_ref[q], preferred_element_type=F32))
        for q in range(nq):
            _scan_slab(s_ref, 0, q, lam_ref, pw_ref, car_ref, steps, reverse=False)
        y_sm = jnp.concatenate(
            [jnp.dot(_get_slab(s_ref, everything, q).astype(BF16), cq_ref[q], preferred_element_type=F32)
             for q in range(nq)], axis=1)
        y_ref[...] = _permute_rows_f32(pmt_ref[...], y_sm) + d_ref[...] * u

    c3 = lambda i: (0, 0, 0)
    c2 = lambda i: (0, 0)
    return _pallas(
        body, name="ssm_fwd", grid=(t // tt,),
        in_specs=[pl.BlockSpec((tt, dssm), lambda i: (i, u_block)), pl.BlockSpec((tt, tt), c2),
                  pl.BlockSpec((tt, tt), c2), pl.BlockSpec(bq.shape, c3),
                  pl.BlockSpec(cq.shape, c3), pl.BlockSpec(lam.shape, c2), pl.BlockSpec(pw.shape, c2),
                  pl.BlockSpec((1, dssm), c2)],
        out_specs=[pl.BlockSpec((tt, dssm), lambda i: (i, 0)),
                   pl.BlockSpec((None, ntile, SUBLANES, SLAB), lambda i: (i, 0, 0, 0))],
        out_shape=[jax.ShapeDtypeStruct((t, dssm), F32), jax.ShapeDtypeStruct((t // tt, ntile, SUBLANES, SLAB), F32)],
        scratch_shapes=[pltpu.VMEM((ntile, tt, SLAB), F32), pltpu.VMEM((ntile, SUBLANES, SLAB), F32)],
        compiler_params=_cp(2 * (8 * tt * dssm + 4 * nq * ds * w2) + 4 * tt * width + 16 * MIB,
                            dimension_semantics=("arbitrary",)),
    )(proj, pm, pmt, bq, cq, lam, pw, d_skip)


def _ssm_bwd(proj, u_block, dy, cin, dproj, bq, cq, lam, pw, d_skip):
    t = proj.shape[0]
    nq, ds, w2 = bq.shape
    dssm = nq * SLAB
    width = nq * w2
    ntile = width // SLAB
    tt = min(T_SCAN, t)
    nt = t // tt
    steps = tt // SUBLANES
    halo = SUBLANES
    perm = _step_major_perm(tt)
    pm, pmt = perm.astype(BF16), perm.T.astype(BF16)

    def body(u_ref, dy_ref, cin_ref, dp_any, pm_ref, pmt_ref, bq_ref, cq_ref, lam_ref, pw_ref, d_ref,
             du_ref, gb_ref, gc_ref, glam_ref, gd_ref, s_ref, gs_ref, car_f, car_b):
        del dp_any

        @pl.when(pl.program_id(0) == 0)
        def _():
            car_b[...] = jnp.zeros_like(car_b)
            gb_ref[...] = jnp.zeros_like(gb_ref)
            gc_ref[...] = jnp.zeros_like(gc_ref)
            glam_ref[...] = jnp.zeros_like(glam_ref)
            gd_ref[...] = jnp.zeros_like(gd_ref)

        u = u_ref[...]
        dyv = dy_ref[...]
        gd_ref[0:1, :] += jnp.sum(dyv * u, axis=0, keepdims=True)
        pmv = pm_ref[...]
        ub = jnp.dot(pmv, u.astype(BF16), preferred_element_type=F32).astype(BF16)
        dyb = jnp.dot(pmv, dyv.astype(BF16), preferred_element_type=F32).astype(BF16)
        car_f[...] = cin_ref[...]
        data = slice(halo, halo + tt)
        everything = slice(None)
        for q in range(nq):
            _put_slab(s_ref, data, q, jnp.dot(ub[:, q * SLAB:(q + 1) * SLAB], bq_ref[q], preferred_element_type=F32))
        for q in range(nq):
            _scan_slab(s_ref, halo, q, lam_ref, pw_ref, car_f, steps, reverse=False)
        last_step = s_ref[:, halo + tt - SUBLANES:halo + tt, :]
        s_ref[:, 0:halo, :] = jnp.concatenate([cin_ref[:, 0:1, :], last_step[:, 0:SUBLANES - 1, :]], axis=1)
        tn = (((0,), (0,)), ((), ()))
        nt_dims = (((1,), (1,)), ((), ()))
        for q in range(nq):
            sl = slice(q * SLAB, (q + 1) * SLAB)
            gc_ref[q] += lax.dot_general(dyb[:, sl], _get_slab(s_ref, data, q).astype(BF16), tn,
                                         preferred_element_type=F32)
            _put_slab(gs_ref, everything, q,
                      lax.dot_general(dyb[:, sl], cq_ref[q], nt_dims, preferred_element_type=F32))
        for q in range(nq):
            _scan_slab(gs_ref, 0, q, lam_ref, pw_ref, car_b, steps, reverse=True,
                       prev_ref=s_ref, prev_row0=halo, glam_ref=glam_ref)
        du_parts = []
        for q in range(nq):
            sl = slice(q * SLAB, (q + 1) * SLAB)
            gsb = _get_slab(gs_ref, everything, q).astype(BF16)
            du_parts.append(lax.dot_general(gsb, bq_ref[q], nt_dims, preferred_element_type=F32))
            gb_ref[q] += lax.dot_general(ub[:, sl], gsb, tn, preferred_element_type=F32)
        du_sm = jnp.concatenate(du_parts, axis=1)
        du_ref[...] = (_permute_rows_f32(pmt_ref[...], du_sm) + dyv * d_ref[...]).astype(BF16)

    c3 = lambda i: (0, 0, 0)
    c2 = lambda i: (0, 0)
    rev = lambda i: (nt - 1 - i, 0)
    dense = jax.ShapeDtypeStruct((nq, SLAB, w2), F32)
    gp = lam.shape[1]
    return _pallas(
        body, name="ssm_bwd", grid=(nt,),
        in_specs=[pl.BlockSpec((tt, dssm), lambda i: (nt - 1 - i, u_block)), pl.BlockSpec((tt, dssm), rev),
                  pl.BlockSpec((None, ntile, SUBLANES, SLAB), lambda i: (nt - 1 - i, 0, 0, 0)),
                  pl.BlockSpec(memory_space=pl.ANY), pl.BlockSpec((tt, tt), c2), pl.BlockSpec((tt, tt), c2),
                  pl.BlockSpec(bq.shape, c3), pl.BlockSpec(cq.shape, c3),
                  pl.BlockSpec(lam.shape, c2), pl.BlockSpec(pw.shape, c2), pl.BlockSpec((1, dssm), c2)],
        out_specs=[pl.BlockSpec((tt, dssm), lambda i: (nt - 1 - i, u_block)), pl.BlockSpec(dense.shape, c3),
                   pl.BlockSpec(dense.shape, c3), pl.BlockSpec((SUBLANES, gp), c2), pl.BlockSpec((SUBLANES, dssm), c2)],
        out_shape=[jax.ShapeDtypeStruct(dproj.shape, dproj.dtype), dense, dense,
                   jax.ShapeDtypeStruct((SUBLANES, gp), F32), jax.ShapeDtypeStruct((SUBLANES, dssm), F32)],
        scratch_shapes=[pltpu.VMEM((ntile, tt + halo, SLAB), F32), pltpu.VMEM((ntile, tt, SLAB), F32),
                        pltpu.VMEM((ntile, SUBLANES, SLAB), F32), pltpu.VMEM((ntile, SUBLANES, SLAB), F32)],
        input_output_aliases={3: 0},
        compiler_params=_cp(2 * (10 * tt * dssm + 4 * nq * ds * w2 + 8 * nq * SLAB * w2)
                            + 8 * tt * width + 12 * MIB, dimension_semantics=("arbitrary",)),
    )(proj, dy, cin, dproj, pm, pmt, bq, cq, lam, pw, d_skip)


def _gate_fwd(proj, y, conv_w, conv_b, w_glu, b_glu, dc):
    t = proj.shape[0]
    dssm = y.shape[1]
    assert dc == dssm
    tm = min(TM_GATE, t)
    halo = SUBLANES

    rb = 2 * SUBLANES
    assert tm % rb == 0

    def body(b_ref, c_ref, v_ref, zc_ref, zs_ref, y_ref, cw_ref, cb_ref, wg_ref, bg_ref, mix_ref,
             cv_buf, sh1_buf, sh2_buf, ge_buf, geb_buf, gl_buf):
        @pl.when(pl.program_id(0) == 0)
        def _():
            cv_buf[0:halo, :] = jnp.zeros((halo, dc), F32)

        cv_buf[halo:, :] = c_ref[...] * v_ref[...]
        sh1_buf[...] = cv_buf[halo - 1:halo - 1 + tm, :]
        sh2_buf[...] = cv_buf[halo - 2:halo - 2 + tm, :]
        w0, w1, w2, cb = cw_ref[0:1, :], cw_ref[1:2, :], cw_ref[2:3, :], cb_ref[...]
        chunks = [pl.ds(r * rb, rb) for r in range(tm // rb)]
        for r, rs in enumerate(chunks):
            ge, _ = _gelu(y_ref[rs, :])
            ge_buf[rs, :] = ge
            geb_buf[rs, :] = ge.astype(BF16)
        gl_buf[...] = jnp.dot(geb_buf[...], wg_ref[...], preferred_element_type=F32) + bg_ref[...]
        for r, rs in enumerate(chunks):
            conv = cb + w2 * cv_buf[pl.ds(halo + r * rb, rb), :] + w1 * sh1_buf[rs, :] + w0 * sh2_buf[rs, :]
            sz, _ = _silu(zc_ref[rs, :])
            mix_ref[rs, 0:dc] = (b_ref[rs, :] * conv * sz).astype(BF16)
        cv_buf[0:halo, :] = cv_buf[tm:tm + halo, :]
        for r, rs in enumerate(chunks):
            szs, _ = _silu(zs_ref[rs, :])
            mix_ref[rs, dc:] = (ge_buf[rs, :] * jax.nn.sigmoid(gl_buf[rs, :]) * szs).astype(BF16)

    col = lambda j: pl.BlockSpec((tm, dc), lambda i, j=j: (i, j))
    fixed = lambda i: (0, 0)
    return _pallas(
        body, name="gate_fwd", grid=(t // tm,),
        in_specs=[col(0), col(1), col(2), col(3), col(5), pl.BlockSpec((tm, dssm), lambda i: (i, 0)),
                  pl.BlockSpec(conv_w.shape, fixed), pl.BlockSpec((1, dc), fixed),
                  pl.BlockSpec(w_glu.shape, fixed), pl.BlockSpec((1, dssm), fixed)],
        out_specs=pl.BlockSpec((tm, dc + dssm), lambda i: (i, 0)),
        out_shape=jax.ShapeDtypeStruct((t, dc + dssm), BF16),
        scratch_shapes=[pltpu.VMEM((tm + halo, dc), F32), pltpu.VMEM((tm, dc), F32), pltpu.VMEM((tm, dc), F32),
                        pltpu.VMEM((tm, dssm), F32), pltpu.VMEM((tm, dssm), BF16), pltpu.VMEM((tm, dssm), F32)],
        compiler_params=_cp(2 * (6 * 4 * tm * dc + 2 * tm * (dc + dssm) + 2 * dssm * dssm) + 24 * tm * dc + 8 * MIB,
                            dimension_semantics=("arbitrary",)),
    )(proj, proj, proj, proj, proj, y, conv_w, conv_b, w_glu, b_glu)


def _gate_bwd(proj, y, dmix, conv_w, conv_b, w_glu, b_glu, dc):
    t = proj.shape[0]
    dssm = y.shape[1]
    tm = min(TM_GATE, t)
    nt = t // tm
    halo = SUBLANES
    blocks_per_tile = tm // halo

    rb = 2 * SUBLANES
    assert tm % rb == 0
    n_chunk = tm // rb

    def body(b_ref, c_ref, v_ref, zc_ref, zs_ref, cp_ref, vp_ref, y_ref, dm_ref, cw_ref, cb_ref, wg_ref, bg_ref,
             dp_ref, dy_ref, gs_ref, gwg_ref, cv_buf, dc_buf, sh1_buf, sh2_buf, geb_buf, dglb_buf, ge_buf, th_buf):
        i = pl.program_id(0)

        @pl.when(i == 0)
        def _():
            dc_buf[tm:, :] = jnp.zeros((halo, dc), F32)
            gs_ref[...] = jnp.zeros_like(gs_ref)
            gwg_ref[...] = jnp.zeros_like(gwg_ref)

        def rows(r):
            return pl.ds(r * rb, rb)

        def rows_after_halo(r):
            return pl.ds(halo + r * rb, rb)

        def chunks(step, carry):
            for r in range(n_chunk):
                carry = step(r, carry)
            return carry

        def fold(v):
            return v[0:SUBLANES] + v[SUBLANES:rb]

        first_tile = (i == nt - 1)
        cv_buf[0:halo, :] = jnp.where(first_tile, 0.0, cp_ref[...] * vp_ref[...])
        cv_buf[halo:, :] = c_ref[...] * v_ref[...]
        sh1_buf[...] = cv_buf[halo - 1:halo - 1 + tm, :]
        sh2_buf[...] = cv_buf[halo - 2:halo - 2 + tm, :]
        w0, w1, w2, cb = cw_ref[0:1, :], cw_ref[1:2, :], cw_ref[2:3, :], cb_ref[...]

        def conv_pass_a(r, carry):
            rs = rows(r)
            bv, zc, dyc = b_ref[rs, :], zc_ref[rs, :], dm_ref[rs, 0:dc]
            conv = cb + w2 * cv_buf[rows_after_halo(r), :] + w1 * sh1_buf[rs, :] + w0 * sh2_buf[rs, :]
            sz, sgc = _silu(zc)
            dp_ref[rs, 0:dc] = (dyc * conv * sz).astype(BF16)
            dp_ref[rs, 3 * dc:4 * dc] = (dyc * bv * conv * _dsilu(zc, sgc)).astype(BF16)
            dc_buf[rs, :] = dyc * bv * sz
            return carry

        chunks(conv_pass_a, 0)
        sh1_buf[...] = dc_buf[1:1 + tm, :]
        sh2_buf[...] = dc_buf[2:2 + tm, :]

        def conv_pass_b(r, acc):
            rs = rows(r)
            dconv, d1, d2 = dc_buf[rs, :], sh1_buf[rs, :], sh2_buf[rs, :]
            cv = cv_buf[rows_after_halo(r), :]
            dcv = w2 * dconv + w1 * d1 + w0 * d2
            dp_ref[rs, dc:2 * dc] = (dcv * v_ref[rs, :]).astype(BF16)
            dp_ref[rs, 2 * dc:3 * dc] = (dcv * c_ref[rs, :]).astype(BF16)
            return (acc[0] + fold(cv * d2), acc[1] + fold(cv * d1), acc[2] + fold(cv * dconv), acc[3] + fold(dconv))

        zero = jnp.zeros((SUBLANES, dc), F32)
        sums = chunks(conv_pass_b, (zero, zero, zero, zero))
        for k in range(4):
            gs_ref[k:k + 1, :] += jnp.sum(sums[k], axis=0, keepdims=True)
        dc_buf[tm:, :] = dc_buf[0:halo, :]

        def glu_pass_a(r, carry):
            rs = rows(r)
            ge, th = _gelu(y_ref[rs, :])
            geb_buf[rs, :] = ge.astype(BF16)
            ge_buf[rs, :] = ge
            th_buf[rs, :] = th
            return carry

        chunks(glu_pass_a, 0)
        wg = wg_ref[...]
        gl_buf = cv_buf.at[halo:halo + tm]
        keep_buf = dc_buf.at[0:tm]
        gl_buf[...] = jnp.dot(geb_buf[...], wg, preferred_element_type=F32) + bg_ref[...]
        dp_ref[:, 4 * dc:5 * dc] = jnp.zeros((tm, dc), BF16)

        def glu_pass_b(r, acc):
            rs = rows(r)
            ge = ge_buf[rs, :]
            sg = jax.nn.sigmoid(gl_buf[rs, :])
            zs, dys = zs_ref[rs, :], dm_ref[rs, dc:]
            szs, sgs = _silu(zs)
            dp_ref[rs, 5 * dc:] = (dys * ge * sg * _dsilu(zs, sgs)).astype(BF16)
            d_ys = dys * szs
            dgl = d_ys * ge * sg * (1.0 - sg)
            dglb_buf[rs, :] = dgl.astype(BF16)
            keep_buf[rs, :] = d_ys * sg
            return acc + fold(dgl)

        gs_ref[4:5, :] += jnp.sum(chunks(glu_pass_b, zero), axis=0, keepdims=True)
        dglb = dglb_buf[...]
        gwg_ref[...] += lax.dot_general(geb_buf[...], dglb, (((0,), (0,)), ((), ())), preferred_element_type=F32)
        gl_buf[...] = lax.dot_general(dglb, wg, (((1,), (1,)), ((), ())), preferred_element_type=F32)

        def glu_pass_c(r, carry):
            rs = rows(r)
            dy_ref[rs, :] = (keep_buf[rs, :] + gl_buf[rs, :]) * _dgelu(y_ref[rs, :], th_buf[rs, :])
            return carry

        chunks(glu_pass_c, 0)

    col = lambda j: pl.BlockSpec((tm, dc), lambda i, j=j: (nt - 1 - i, j))
    prev = lambda j: pl.BlockSpec((halo, dc), lambda i, j=j: (jnp.maximum((nt - 1 - i) * blocks_per_tile - 1, 0), j))
    rev = lambda i: (nt - 1 - i, 0)
    fixed = lambda i: (0, 0)
    return _pallas(
        body, name="gate_bwd", grid=(nt,),
        in_specs=[col(0), col(1), col(2), col(3), col(5), prev(1), prev(2), pl.BlockSpec((tm, dssm), rev),
                  pl.BlockSpec((tm, dc + dssm), rev), pl.BlockSpec(conv_w.shape, fixed), pl.BlockSpec((1, dc), fixed),
                  pl.BlockSpec(w_glu.shape, fixed), pl.BlockSpec((1, dssm), fixed)],
        out_specs=[pl.BlockSpec((tm, 6 * dc), rev), pl.BlockSpec((tm, dssm), rev),
                   pl.BlockSpec((2 * SUBLANES, dc), fixed), pl.BlockSpec((dssm, dssm), fixed)],
        out_shape=[jax.ShapeDtypeStruct((t, 6 * dc), BF16), jax.ShapeDtypeStruct((t, dssm), F32),
                   jax.ShapeDtypeStruct((2 * SUBLANES, dc), F32), jax.ShapeDtypeStruct((dssm, dssm), F32)],
        scratch_shapes=[pltpu.VMEM((tm + halo, dc), F32), pltpu.VMEM((tm + halo, dc), F32), pltpu.VMEM((tm, dc), F32),
                        pltpu.VMEM((tm, dc), F32), pltpu.VMEM((tm, dssm), BF16), pltpu.VMEM((tm, dssm), BF16),
                        pltpu.VMEM((tm, dssm), F32), pltpu.VMEM((tm, dssm), F32)],
        compiler_params=_cp(2 * (6 * 4 * tm * dc + 8 * tm * dc + 12 * tm * dc + 4 * tm * dc + 6 * dssm * dssm)
                            + 28 * tm * dc + 8 * MIB, dimension_semantics=("arbitrary",)),
    )(proj, proj, proj, proj, proj, proj, proj, y, dmix, conv_w, conv_b, w_glu, b_glu)


def _allgather_flat(v, name):
    r, c = v.shape
    rels = [(dx, dy, dc) for dx in (0, 1) for dy in (0, 1) for dc in (0, 1)][1:]

    def body(v_ref, out_ref, send, recv):
        x, y, cc = _my_place()
        me = 4 * x + 2 * y + cc

        def peer(rel):
            dx, dy, dc = rel
            return (1 - x if dx else x, 1 - y if dy else y, 1 - cc if dc else cc)

        def rc(s, slot, to):
            return pltpu.make_async_remote_copy(src_ref=v_ref, dst_ref=out_ref.at[slot], send_sem=send.at[s],
                                                recv_sem=recv.at[s], device_id=to, device_id_type=MESH)

        sent = []
        for s, rel in enumerate(rels):
            cp = rc(s, me, peer(rel))
            cp.start()
            sent.append(cp)
        out_ref[me] = v_ref[...]
        for s, rel in enumerate(rels):
            px, py, pc = peer(rel)
            rc(s, 4 * px + 2 * py + pc, (px, py, pc)).wait_recv()
        for cp in sent:
            cp.wait_send()

    return _pallas(
        body, name=name, out_shape=jax.ShapeDtypeStruct((N_DEV, r, c), F32),
        in_specs=[pl.BlockSpec(memory_space=pltpu.VMEM)], out_specs=pl.BlockSpec(memory_space=pltpu.VMEM),
        scratch_shapes=[pltpu.SemaphoreType.DMA((N_DEV - 1,)), pltpu.SemaphoreType.DMA((N_DEV - 1,))],
        compiler_params=_cp((N_DEV + 2) * 4 * r * c + 8 * MIB),
    )(v)


def _rs_pair_exchange(grads, name):
    n = len(grads)

    def body(*refs):
        ins, outs = refs[:n], refs[n:2 * n]
        send, recv = refs[2 * n:]
        x, y, c = _my_place()
        cps = []
        for t in range(n):
            cp = pltpu.make_async_remote_copy(src_ref=ins[t].at[:, 1 - c], dst_ref=outs[t], send_sem=send.at[t],
                                              recv_sem=recv.at[t], device_id=(x, y, 1 - c), device_id_type=MESH)
            cp.start()
            cps.append(cp)
        for cp in cps:
            cp.wait()

    any_spec = pl.BlockSpec(memory_space=pl.ANY)
    return _pallas(
        body, name=name, in_specs=[any_spec] * n, out_specs=[any_spec] * n,
        out_shape=[jax.ShapeDtypeStruct((g.shape[0],) + g.shape[2:], g.dtype) for g in grads],
        scratch_shapes=[pltpu.SemaphoreType.DMA((n,)), pltpu.SemaphoreType.DMA((n,))],
        compiler_params=_cp(16 * MIB),
    )(*grads)


def _rs_pair_add(place, grad, got, name):
    nk, _, r2, c = grad.shape
    tr = min(TR_ELT, r2)
    other = lambda s, p: (p[1] + 1 + s) % nk

    def send_body(place_ref, g_ref, r_ref, o_ref):
        del place_ref
        o_ref[...] = (g_ref[...] + r_ref[...].astype(F32)).astype(BF16)

    to_send = _pallas(
        send_body, name=name + "_send",
        grid_spec=pltpu.PrefetchScalarGridSpec(
            num_scalar_prefetch=1, grid=(nk - 1, r2 // tr),
            in_specs=[pl.BlockSpec((None, None, tr, c), lambda s, i, p: (other(s, p), p[0], i, 0)),
                      pl.BlockSpec((None, tr, c), lambda s, i, p: (other(s, p), i, 0))],
            out_specs=pl.BlockSpec((None, tr, c), lambda s, i, p: (other(s, p), i, 0))),
        out_shape=jax.ShapeDtypeStruct((nk, r2, c), BF16),
        compiler_params=_cp(2 * 8 * tr * c + 8 * MIB),
    )(place, grad, got)

    def keep_body(place_ref, g_ref, r_ref, o_ref):
        del place_ref
        o_ref[...] = g_ref[...] + r_ref[...].astype(F32)

    to_keep = _pallas(
        keep_body, name=name + "_keep",
        grid_spec=pltpu.PrefetchScalarGridSpec(
            num_scalar_prefetch=1, grid=(r2 // tr,),
            in_specs=[pl.BlockSpec((None, None, tr, c), lambda i, p: (p[1], p[0], i, 0)),
                      pl.BlockSpec((None, tr, c), lambda i, p: (p[1], i, 0))],
            out_specs=pl.BlockSpec((tr, c), lambda i, p: (i, 0))),
        out_shape=jax.ShapeDtypeStruct((r2, c), F32),
        compiler_params=_cp(2 * 10 * tr * c + 8 * MIB),
    )(place, grad, got)
    return to_send, to_keep


_HBM_SPEC = pl.BlockSpec(memory_space=pltpu.HBM)
_SEM_SPEC = pl.BlockSpec(memory_space=pltpu.SEMAPHORE)
_DATAFLOW = pltpu.SideEffectType.DATAFLOW_SIDE_EFFECTING


def _split_copy_start(srcs, land_shapes, plan, n_sems, name, after=()):
    ns, nl, na = len(srcs), len(land_shapes), len(after)

    def body(*refs):
        src_refs, land_refs = refs[:ns], refs[ns:ns + nl]
        send, recv = refs[ns + nl + na], refs[ns + nl + na + 1]
        token = refs[-1]
        sends, _ = plan(src_refs, land_refs)
        for src, dst, to, si, ri in sends:
            pltpu.make_async_remote_copy(src_ref=src, dst_ref=dst, send_sem=send.at[si], recv_sem=recv.at[ri],
                                         device_id=to, device_id_type=MESH).start()
        token[...] = jnp.zeros_like(token)

    lands = [lax.empty(shp, dt) for shp, dt in land_shapes]
    through = [pltpu.HBM(a.shape, a.dtype) for a in srcs] + [pltpu.HBM(shp, dt) for shp, dt in land_shapes]
    out = pl.pallas_call(
        body, name=name,
        out_shape=(pltpu.SemaphoreType.DMA((n_sems,)), pltpu.SemaphoreType.DMA((n_sems,)), *through,
                   jax.ShapeDtypeStruct((SUBLANES, SLAB), F32)),
        in_specs=[_HBM_SPEC] * (ns + nl) + [pl.BlockSpec(memory_space=pl.ANY)] * na,
        out_specs=(_SEM_SPEC, _SEM_SPEC, *([_HBM_SPEC] * (ns + nl)), pl.BlockSpec(memory_space=pltpu.VMEM)),
        input_output_aliases={i: 2 + i for i in range(ns + nl)},
        compiler_params=pltpu.CompilerParams(has_side_effects=_DATAFLOW),
    )(*[pltpu.with_memory_space_constraint(a, pltpu.HBM) for a in (*srcs, *lands)], *after)
    return out[0], out[1], list(out[2:2 + ns]), list(out[2 + ns:2 + ns + nl]), out[-1]


def _split_copy_wait(send, recv, srcs, lands, plan, after, name):
    ns, nl, na = len(srcs), len(lands), len(after)

    def body(*refs):
        src_refs, land_refs = refs[:ns], refs[ns:ns + nl]
        send_ref, recv_ref = refs[ns + nl], refs[ns + nl + 1]
        sends, arrivals = plan(src_refs, land_refs)
        for src, dst, to, si, ri in sends:
            pltpu.make_async_remote_copy(src_ref=src, dst_ref=dst, send_sem=send_ref.at[si], recv_sem=recv_ref.at[ri],
                                         device_id=to, device_id_type=MESH).wait_send()
        for (src, _, to, si, _), (view, ri) in zip(sends, arrivals):
            pltpu.make_async_remote_copy(src_ref=view, dst_ref=view, send_sem=send_ref.at[si], recv_sem=recv_ref.at[ri],
                                         device_id=to, device_id_type=MESH).wait_recv()

    out = pl.pallas_call(
        body, name=name,
        out_shape=[pltpu.HBM(a.shape, a.dtype) for a in (*srcs, *lands)],
        in_specs=[_HBM_SPEC] * (ns + nl) + [_SEM_SPEC, _SEM_SPEC] + [pl.BlockSpec(memory_space=pl.ANY)] * na,
        out_specs=[_HBM_SPEC] * (ns + nl),
        input_output_aliases={i: i for i in range(ns + nl)},
        compiler_params=pltpu.CompilerParams(has_side_effects=_DATAFLOW),
    )(*srcs, *lands, send, recv, *after)
    return list(out[:ns]), list(out[ns:])


def _chip_exchange_plan(n):
    per = N_CHIPS - 1

    def plan(srcs, lands):
        x, y, c = _my_place()
        sends, arrivals = [], []
        for t in range(n):
            for j, (cx, cy) in enumerate(_other_chips(x, y)):
                sends.append((srcs[t].at[2 * cx + cy], lands[t].at[j], (cx, cy, c), t * per + j, t * per + j))
                arrivals.append((lands[t].at[j], t * per + j))
        return sends, arrivals

    return plan


def _gather_half_plan(n):
    per = N_CHIPS - 1

    def plan(srcs, lands):
        x, y, c = _my_place()
        k = 2 * x + y
        sends, arrivals = [], []
        for t in range(n):
            for j, (cx, cy) in enumerate(_other_chips(x, y)):
                sends.append((srcs[t].at[c], lands[t].at[k, c], (cx, cy, c), t * per + j, t * per + j))
                arrivals.append((lands[t].at[2 * cx + cy, c], t * per + j))
        return sends, arrivals

    return plan


def _gather_complete(own, landed, name):
    def body(own_ref, in_ref, out_ref, send, recv, local):
        x, y, c = _my_place()
        sib = (x, y, 1 - c)
        mine = pltpu.make_async_copy(own_ref, out_ref.at[2 * x + y], local)
        mine.start()
        sent = []
        for j, (cx, cy) in enumerate(_other_chips(x, y)):
            half = (2 * cx + cy, c)
            cp = pltpu.make_async_remote_copy(src_ref=in_ref.at[half], dst_ref=out_ref.at[half], send_sem=send.at[j],
                                              recv_sem=recv.at[j], device_id=sib, device_id_type=MESH)
            cp.start()
            sent.append(cp)
        for j, (cx, cy) in enumerate(_other_chips(x, y)):
            other = out_ref.at[2 * cx + cy, 1 - c]
            pltpu.make_async_remote_copy(src_ref=other, dst_ref=other, send_sem=send.at[j], recv_sem=recv.at[j],
                                         device_id=sib, device_id_type=MESH).wait_recv()
        for cp in sent:
            cp.wait_send()
        mine.wait()

    any_spec = pl.BlockSpec(memory_space=pl.ANY)
    per = N_CHIPS - 1
    return _pallas(
        body, name=name, in_specs=[any_spec, any_spec], out_specs=any_spec,
        out_shape=jax.ShapeDtypeStruct(landed.shape, landed.dtype), input_output_aliases={1: 0},
        scratch_shapes=[pltpu.SemaphoreType.DMA((per,)), pltpu.SemaphoreType.DMA((per,)), pltpu.SemaphoreType.DMA],
        compiler_params=_cp(16 * MIB),
    )(own, landed)


def _gather_direct_plan(n):
    per = 2 * (N_CHIPS - 1)

    def plan(srcs, lands):
        x, y, c = _my_place()
        k = 2 * x + y
        sends, arrivals = [], []
        for t in range(n):
            for j, (cx, cy) in enumerate(_other_chips(x, y)):
                for core in (0, 1):
                    sends.append((srcs[t].at[c], lands[t].at[k, c], (cx, cy, core),
                                  t * per + 2 * j + core, t * per + 2 * j + c))
                    arrivals.append((lands[t].at[2 * cx + cy, core], t * per + 2 * j + core))
        return sends, arrivals

    return plan


def _rs_chip_add(kept, got, name):
    r2, c = kept.shape
    tr = min(TR_ELT, r2)

    def body(s_ref, q_ref, o_ref):
        o_ref[...] = ((s_ref[...] + q_ref[0].astype(F32)) + q_ref[1].astype(F32)) + q_ref[2].astype(F32)

    blk = pl.BlockSpec((tr, c), lambda i: (i, 0))
    return _pallas(
        body, name=name, grid=(r2 // tr,),
        in_specs=[blk, pl.BlockSpec((N_CHIPS - 1, tr, c), lambda i: (0, i, 0))], out_specs=blk,
        out_shape=jax.ShapeDtypeStruct((r2, c), F32),
        compiler_params=_cp(2 * 14 * tr * c + 8 * MIB),
    )(kept, got)


def _rs_pair_share(halves, name):
    n = len(halves)

    def body(*refs):
        ins, outs = refs[:n], refs[n:2 * n]
        send, recv, local = refs[2 * n:]
        x, y, c = _my_place()
        cps, mine = [], []
        for t in range(n):
            lc = pltpu.make_async_copy(ins[t], outs[t].at[c], local.at[t])
            lc.start()
            mine.append(lc)
            cp = pltpu.make_async_remote_copy(src_ref=ins[t], dst_ref=outs[t].at[c], send_sem=send.at[t],
                                              recv_sem=recv.at[t], device_id=(x, y, 1 - c), device_id_type=MESH)
            cp.start()
            cps.append(cp)
        for t in range(n):
            other = outs[t].at[1 - c]
            pltpu.make_async_remote_copy(src_ref=other, dst_ref=other, send_sem=send.at[t], recv_sem=recv.at[t],
                                         device_id=(x, y, 1 - c), device_id_type=MESH).wait_recv()
        for cp in cps:
            cp.wait_send()
        for lc in mine:
            lc.wait()

    any_spec = pl.BlockSpec(memory_space=pl.ANY)
    return _pallas(
        body, name=name, in_specs=[any_spec] * n, out_specs=[any_spec] * n,
        out_shape=[jax.ShapeDtypeStruct((2,) + h.shape, F32) for h in halves],
        scratch_shapes=[pltpu.SemaphoreType.DMA((n,)), pltpu.SemaphoreType.DMA((n,)), pltpu.SemaphoreType.DMA((n,))],
        compiler_params=_cp(16 * MIB),
    )(*halves)


_PACK_TILE = SUBLANES * SLAB


def _pack(arrays):
    rows = []
    for a in arrays:
        flat = a.reshape(-1).astype(F32)
        padded = -(-flat.shape[0] // _PACK_TILE) * _PACK_TILE
        rows.append(jnp.pad(flat, (0, padded - flat.shape[0])).reshape(-1, SLAB))
    n_rows = sum(r.shape[0] for r in rows)
    if n_rows % (2 * SUBLANES):
        rows.append(jnp.zeros((SUBLANES, SLAB), F32))
    return jnp.concatenate(rows, axis=0)


def _unpack(packed, shapes):
    out, row = [], 0
    for shp in shapes:
        size = math.prod(shp)
        nrow = -(-size // _PACK_TILE) * SUBLANES
        out.append(packed[row:row + nrow].reshape(-1)[:size].reshape(shp))
        row += nrow
    return out


def kernel(x, norm_pre_g, w_in, conv_w, conv_b, ssm_a_re, ssm_a_im, ssm_log_dt, ssm_b_re, ssm_b_im, ssm_c_re, ssm_c_im, ssm_d, w_glu, b_glu, w_out, norm_post_g, loss_target, m_norm_pre_g, m_w_in, m_conv_w, m_conv_b, m_ssm_a_re, m_ssm_a_im, m_ssm_log_dt, m_ssm_b_re, m_ssm_b_im, m_ssm_c_re, m_ssm_c_im, m_ssm_d, m_w_glu, m_b_glu, m_w_out, m_norm_post_g, v_norm_pre_g, v_w_in, v_conv_w, v_conv_b, v_ssm_a_re, v_ssm_a_im, v_ssm_log_dt, v_ssm_b_re, v_ssm_b_im, v_ssm_c_re, v_ssm_c_im, v_ssm_d, v_w_glu, v_b_glu, v_w_out, v_norm_post_g):
    xs, tgt = x[0], loss_target[0]
    t, d = xs.shape
    dc = conv_b.shape[0]
    dssm = ssm_d.shape[0]
    g, p = ssm_a_re.shape
    h = SSM_H
    nq = dssm // SLAB
    n_shard = w_in.shape[1]
    steps = min(T_SCAN, t) // SUBLANES
    mx, my, mc = _my_place()
    chip = 2 * mx + my
    place = jnp.stack([mc, chip]).astype(jnp.int32)

    cw_cols = conv_w.shape[1]
    cw_pad = -(-cw_cols // SLAB) * SLAB
    cw_blk = jnp.zeros((SUBLANES, cw_pad), F32).at[:conv_w.shape[0], :cw_cols].set(conv_w)
    cw_all = _allgather_flat(cw_blk, "allgather_conv_w")
    conv_w_full = jnp.concatenate([cw_all[2 * k, :, :cw_cols] for k in range(N_CHIPS)], axis=1)

    halves = lambda a: a.reshape(2, a.shape[0] // 2, a.shape[1])
    win_half = halves(_cast_bf16(w_in, "cast_w_in", cw_all[0, :, :SLAB]))
    win_plan = _gather_half_plan(1)
    win_send, win_recv, win_srcs, win_lands, win_token = _split_copy_start(
        [win_half], [((N_CHIPS,) + win_half.shape, BF16)], win_plan, N_CHIPS - 1, "gather_w_in_start")
    behind_w_in = win_token[0:1, 0:1]

    small_names = ["norm_pre_g", "conv_w", "conv_b", "ssm_a_re", "ssm_a_im", "ssm_log_dt", "ssm_b_re", "ssm_b_im",
                   "ssm_c_re", "ssm_c_im", "ssm_d", "b_glu", "norm_post_g", "loss"]
    zeros_cw = jnp.zeros((conv_w.shape[0], dc), F32)
    one0 = jnp.zeros((1,), F32)
    hp = lambda b: b.transpose(0, 2, 1)
    small_w = dict(norm_pre_g=norm_pre_g, conv_w=zeros_cw, conv_b=conv_b, ssm_a_re=ssm_a_re, ssm_a_im=ssm_a_im,
                   ssm_log_dt=ssm_log_dt, ssm_b_re=hp(ssm_b_re), ssm_b_im=hp(ssm_b_im), ssm_c_re=ssm_c_re, ssm_c_im=ssm_c_im,
                   ssm_d=ssm_d, b_glu=b_glu, norm_post_g=norm_post_g, loss=one0)
    small_m = dict(norm_pre_g=m_norm_pre_g, conv_w=zeros_cw, conv_b=m_conv_b, ssm_a_re=m_ssm_a_re, ssm_a_im=m_ssm_a_im,
                   ssm_log_dt=m_ssm_log_dt, ssm_b_re=hp(m_ssm_b_re), ssm_b_im=hp(m_ssm_b_im), ssm_c_re=m_ssm_c_re,
                   ssm_c_im=m_ssm_c_im, ssm_d=m_ssm_d, b_glu=m_b_glu, norm_post_g=m_norm_post_g, loss=one0)
    small_v = dict(norm_pre_g=v_norm_pre_g, conv_w=zeros_cw, conv_b=v_conv_b, ssm_a_re=v_ssm_a_re, ssm_a_im=v_ssm_a_im,
                   ssm_log_dt=v_ssm_log_dt, ssm_b_re=hp(v_ssm_b_re), ssm_b_im=hp(v_ssm_b_im), ssm_c_re=v_ssm_c_re,
                   ssm_c_im=v_ssm_c_im, ssm_d=v_ssm_d, b_glu=v_b_glu, norm_post_g=v_norm_post_g, loss=one0)
    w_pack, m_pack, v_pack = [_pack([group[nm] for nm in small_names]) + behind_w_in
                              for group in (small_w, small_m, small_v)]
    side = [halves(_cast_bf16(w_out, "cast_w_out", win_token)), halves(_cast_bf16(w_glu, "cast_w_glu", win_token))]

    expand = jnp.tile(jnp.eye(p, dtype=F32), (1, h))
    b_re2, b_im2 = hp(ssm_b_re).reshape(g, h * p), hp(ssm_b_im).reshape(g, h * p)
    log_dt2 = ssm_log_dt.reshape(g, 1) + behind_w_in
    lbr, lbi, bbr2, bbi2, pw3 = _zoh_fwd(ssm_a_re, ssm_a_im, log_dt2, b_re2, b_im2, expand, steps)
    lam = jnp.stack([lbr.reshape(g * p), lbi.reshape(g * p)])
    pw = pw3.reshape(2, g * p)
    to_slab_b = lambda b2: _blockdiag(b2.reshape(nq, GROUPS_PER_SLAB, h, p))
    bq = jnp.concatenate([to_slab_b(bbr2), to_slab_b(bbi2)], axis=2).astype(BF16)
    to_slab_c = lambda c3: _blockdiag(c3.reshape(nq, GROUPS_PER_SLAB, h, p).transpose(0, 1, 3, 2))
    cq = jnp.concatenate([to_slab_c(ssm_c_re), to_slab_c(-ssm_c_im)], axis=1).astype(BF16)

    g_pre2, g_post2 = norm_pre_g.reshape(1, d), norm_post_g.reshape(1, d)
    conv_b2, b_glu2, d_skip2 = conv_b.reshape(1, dc), b_glu.reshape(1, dssm), ssm_d.reshape(1, dssm)
    u_block = 4 * dc // dssm

    hb = _prenorm(xs, g_pre2 + behind_w_in)
    proj_own = _inproj_own(place, hb, win_srcs[0].reshape(d, n_shard))
    win_own, win_landed = _split_copy_wait(win_send, win_recv, win_srcs, win_lands, win_plan,
                                           [proj_own, bq, cq, w_pack, m_pack, v_pack, *side], "gather_w_in_wait")
    win_g = _gather_complete(win_own[0], win_landed[0], "gather_w_in_complete")
    win_b = win_g.reshape(N_CHIPS, d, n_shard)
    side_plan = _gather_direct_plan(len(side))
    side_sems = 2 * (N_CHIPS - 1) * len(side)
    ag_send, ag_recv, ag_srcs, ag_lands, ag_token = _split_copy_start(
        side, [((N_CHIPS,) + a.shape, BF16) for a in side], side_plan, side_sems, "gather_side_weights_start",
        after=[win_g])
    proj = _inproj_rest(place, hb, win_b, proj_own, ag_token)
    y, cin = _ssm_fwd(proj, u_block, bq, cq, lam, pw, d_skip2)
    side_own, side_all = _split_copy_wait(ag_send, ag_recv, ag_srcs, ag_lands, side_plan, [cin],
                                          "gather_side_weights_wait")
    wout_g, wglu_g = [lax.dynamic_update_index_in_dim(all_, own, chip, 0) for own, all_ in zip(side_own, side_all)]
    wout_b = wout_g.reshape(dc + dssm, d)
    wglu_b = wglu_g.reshape(dssm, dssm)
    mix = _gate_fwd(proj, y, conv_w_full, conv_b2, wglu_b, b_glu2, dc)
    loss_blk, dout, dob, dmix, gg_post = _outproj(mix, wout_b, xs, tgt, g_post2)

    def reduce_start(grads, to_send, tags, group):
        got = _rs_pair_exchange(to_send, "rs_pair_exchange_" + group)
        sums = [_rs_pair_add(place, gt, rt, "rs_pair_add_" + tg) for gt, rt, tg in zip(grads, got, tags)]
        plan = _chip_exchange_plan(len(grads))
        started = _split_copy_start([s16 for s16, _ in sums],
                                    [((N_CHIPS - 1,) + s16.shape[1:], BF16) for s16, _ in sums], plan,
                                    (N_CHIPS - 1) * len(grads), "rs_chip_exchange_" + group + "_start")
        return plan, started, [s32 for _, s32 in sums]

    def reduce_finish(plan, started, sums32, tags, group, after):
        send, recv, srcs, lands, _ = started
        _, landed = _split_copy_wait(send, recv, srcs, lands, plan, after, "rs_chip_exchange_" + group + "_wait")
        mine = [_rs_chip_add(s32, qt, "rs_chip_add_" + tg) for s32, qt, tg in zip(sums32, landed, tags)]
        full = _rs_pair_share(mine, "rs_pair_share_" + group)
        return [f.reshape(2 * f.shape[1], f.shape[2]) for f in full]

    gw_out, gw_out16 = _matmul_tn(mix, dob, 1, "grad_w_out")
    dproj, dy, gsmall, gw_glu = _gate_bwd(proj, y, dmix, conv_w_full, conv_b2, wglu_b, b_glu2, dc)
    as_out = lambda a: a.reshape(N_CHIPS, 2, (dc + dssm) // (2 * N_CHIPS), d)
    gw_glu4 = gw_glu.reshape(N_CHIPS, 2, dssm // (2 * N_CHIPS), dssm)
    rs_a = reduce_start([as_out(gw_out), gw_glu4], [as_out(gw_out16), gw_glu4], ["w_out", "w_glu"], "a")
    dproj, gb_dense, gc_dense, glam, gd = _ssm_bwd(proj, u_block, dy, cin, dproj, bq, cq, lam, pw,
                                                   d_skip2 + rs_a[1][4][0:1, 0:1])
    gw_in, gw_in16 = _matmul_tn(hb, dproj, N_CHIPS, "grad_w_in")
    as_in = lambda a: a.reshape(N_CHIPS, 2, d // 2, n_shard)
    rs_b = reduce_start([as_in(gw_in)], [as_in(gw_in16)], ["w_in"], "b")
    gx, gg_pre = _dh_prenorm_bwd(dproj, win_b, xs, dout, g_pre2 + rs_b[1][4][0:1, 0:1])

    gb_blocks = _blockdiag_take(gb_dense, h, p)
    g_bbr2 = gb_blocks[:, :, :, 0, :].reshape(g, h * p)
    g_bbi2 = gb_blocks[:, :, :, 1, :].reshape(g, h * p)
    gc_blocks = _blockdiag_take(gc_dense, h, p)
    g_c_re = gc_blocks[:, :, :, 0, :].reshape(g, h, p)
    g_c_im = -gc_blocks[:, :, :, 1, :].reshape(g, h, p)
    g_a_re, g_a_im, g_ld, g_b_re2, g_b_im2 = _zoh_bwd(
        ssm_a_re, ssm_a_im, log_dt2, b_re2, b_im2, expand,
        glam[0].reshape(g, p), glam[1].reshape(g, p), g_bbr2, g_bbi2)

    small_g = {
        "norm_pre_g": gg_pre[0], "conv_w": gsmall[0:3], "conv_b": gsmall[3], "ssm_a_re": g_a_re, "ssm_a_im": g_a_im,
        "ssm_log_dt": g_ld.reshape(g), "ssm_b_re": g_b_re2.reshape(g, h, p), "ssm_b_im": g_b_im2.reshape(g, h, p),
        "ssm_c_re": g_c_re, "ssm_c_im": g_c_im, "ssm_d": gd[0], "b_glu": gsmall[4], "norm_post_g": gg_post[0],
        "loss": loss_blk[0, 0:1],
    }
    shapes = [small_w[nm].shape for nm in small_names]
    g_pack = _allreduce_small(_pack([small_g[nm] for nm in small_names]))
    packs = _adamw_small(g_pack, w_pack, m_pack, v_pack)
    sg, sd, sm, sv = [dict(zip(small_names, _unpack(pk, shapes))) for pk in (g_pack, *packs)]
    for group in (sg, sd, sm, sv):
        group.update(ssm_b_re=hp(group["ssm_b_re"]), ssm_b_im=hp(group["ssm_b_im"]))
    loss = sg["loss"][0]

    g_cw = lax.dynamic_slice_in_dim(sg["conv_w"], chip * cw_cols, cw_cols, axis=1)
    pad_cw = lambda a: jnp.zeros((SUBLANES, cw_pad), F32).at[:a.shape[0], :cw_cols].set(a)
    cut_cw = lambda a: a[:conv_w.shape[0], :cw_cols]
    _, d_cw, m_cw, v_cw = [cut_cw(a) for a in _adamw(pad_cw(conv_w), pad_cw(g_cw), pad_cw(m_conv_w), pad_cw(v_conv_w),
                                                     "adamw_conv_w")]

    g_wout, g_wglu = reduce_finish(*rs_a, ["w_out", "w_glu"], "a", [d_cw, packs[0]])
    (g_win,) = reduce_finish(*rs_b, ["w_in"], "b", [g_wout])
    g_win, d_win, m_win, v_win = _adamw(w_in, g_win, m_w_in, v_w_in, "adamw_w_in")
    g_wout, d_wout, m_wout, v_wout = _adamw(w_out, g_wout, m_w_out, v_w_out, "adamw_w_out")
    g_wglu, d_wglu, m_wglu, v_wglu = _adamw(w_glu, g_wglu, m_w_glu, v_w_glu, "adamw_w_glu")

    order = ["norm_pre_g", "w_in", "conv_w", "conv_b", "ssm_a_re", "ssm_a_im", "ssm_log_dt", "ssm_b_re", "ssm_b_im",
             "ssm_c_re", "ssm_c_im", "ssm_d", "w_glu", "b_glu", "w_out", "norm_post_g"]
    grads, deltas, new_m, new_v = dict(sg), dict(sd), dict(sm), dict(sv)
    grads.update(w_in=g_win, w_out=g_wout, w_glu=g_wglu, conv_w=g_cw)
    deltas.update(w_in=d_win, w_out=d_wout, w_glu=d_wglu, conv_w=d_cw)
    new_m.update(w_in=m_win, w_out=m_wout, w_glu=m_wglu, conv_w=m_cw)
    new_v.update(w_in=v_win, w_out=v_wout, w_glu=v_wglu, conv_w=v_cw)
    return (loss, gx[None], *[grads[nm] for nm in order], *[deltas[nm] for nm in order],
            *[new_m[nm] for nm in order], *[new_v[nm] for nm in order])
```

```python
import math

import jax
import jax.numpy as jnp
from jax import lax
from jax.experimental import pallas as pl
from jax.experimental.pallas import tpu as pltpu

F32 = jnp.float32
BF16 = jnp.bfloat16
MESH = pl.DeviceIdType.MESH

EPS = 1e-6
SSM_H = 16
SSM_P = 64
GROUPS_PER_SLAB = 8
SLAB = 128
SLAB_STATES = GROUPS_PER_SLAB * SSM_P
N_CHIPS = 4
N_DEV = 8

ADAM_LR = 0.001
ADAM_B1 = 0.9
ADAM_B2 = 0.999
ADAM_EPS = 1e-08
ADAM_WD = 0.01
ADAM_STEP = 10

MIB = 1024 * 1024
VMEM_CAP = 48 * MIB
SUBLANES = 8

TM_NORM = 512
TM_PROJ = 512
TM_GATE = 256
TM_OUT = 256
TM_DH = 256
T_SCAN = 256
TK_TN = 1024
TM_TN = 1024
TR_ELT = 256


def _cp(vmem_bytes, **kw):
    return pltpu.CompilerParams(vmem_limit_bytes=int(min(VMEM_CAP, max(16 * MIB, vmem_bytes))), **kw)


def _pallas(body, **kw):
    if "grid" not in kw and "grid_spec" not in kw:
        return pl.pallas_call(body, **kw)
    pin = lambda s: pltpu.HBM(s.shape, s.dtype) if isinstance(s, jax.ShapeDtypeStruct) else s
    out_shape = kw.pop("out_shape")
    out_shape = [pin(s) for s in out_shape] if isinstance(out_shape, (list, tuple)) else pin(out_shape)
    call = pl.pallas_call(body, out_shape=out_shape, **kw)

    def run(*args):
        return call(*[pltpu.with_memory_space_constraint(a, pltpu.HBM) if jnp.issubdtype(a.dtype, jnp.floating) else a
                      for a in args])

    return run


def _my_place():
    return lax.axis_index("x"), lax.axis_index("y"), lax.axis_index("c")


def _other_chips(x, y):
    return [(1 - x, y), (x, 1 - y), (1 - x, 1 - y)]


def _silu(z):
    s = jax.nn.sigmoid(z)
    return z * s, s


def _dsilu(z, s):
    return s * (1.0 + z * (1.0 - s))


_GELU_K = math.sqrt(2.0 / math.pi)
_GELU_C = 0.044715


def _gelu(y):
    th = jnp.tanh(_GELU_K * (y + _GELU_C * y * y * y))
    return 0.5 * y * (1.0 + th), th


def _dgelu(y, th):
    return 0.5 * (1.0 + th) + 0.5 * y * (1.0 - th * th) * _GELU_K * (1.0 + 3.0 * _GELU_C * y * y)


def _cast_bf16(w, name, after=None):
    r, c = w.shape
    tr = min(TR_ELT, r)
    extra = [] if after is None else [after]

    def body(w_ref, *rest):
        rest[-1][...] = w_ref[...].astype(BF16)

    return _pallas(
        body, name=name, grid=(r // tr,),
        in_specs=[pl.BlockSpec((tr, c), lambda i: (i, 0))] + [pl.BlockSpec((SUBLANES, SLAB), lambda i: (0, 0))] * len(extra),
        out_specs=pl.BlockSpec((tr, c), lambda i: (i, 0)),
        out_shape=jax.ShapeDtypeStruct((r, c), BF16),
        compiler_params=_cp(12 * tr * c),
    )(w, *extra)


def _prenorm(x, g):
    t, d = x.shape
    tm = min(TM_NORM, t)

    def body(x_ref, g_ref, h_ref):
        xv = x_ref[...]
        r = lax.rsqrt(jnp.mean(xv * xv, axis=-1, keepdims=True) + EPS)
        h_ref[...] = (xv * r * g_ref[...]).astype(BF16)

    return _pallas(
        body, name="prenorm", grid=(t // tm,),
        in_specs=[pl.BlockSpec((tm, d), lambda i: (i, 0)), pl.BlockSpec((1, d), lambda i: (0, 0))],
        out_specs=pl.BlockSpec((tm, d), lambda i: (i, 0)),
        out_shape=jax.ShapeDtypeStruct((t, d), BF16),
        compiler_params=_cp(20 * tm * d),
    )(x, g)


def _inproj_own(place, a, w_own):
    t, k = a.shape
    n = w_own.shape[1]
    tm = min(TM_PROJ, t)

    def body(place_ref, a_ref, b_ref, o_ref):
        del place_ref
        o_ref[...] = jnp.dot(a_ref[...], b_ref[...], preferred_element_type=F32)

    return _pallas(
        body, name="inproj_own",
        grid_spec=pltpu.PrefetchScalarGridSpec(
            num_scalar_prefetch=1, grid=(t // tm,),
            in_specs=[pl.BlockSpec((tm, k), lambda i, p: (i, 0)), pl.BlockSpec((k, n), lambda i, p: (0, 0))],
            out_specs=pl.BlockSpec((tm, n), lambda i, p: (i, p[1]))),
        out_shape=jax.ShapeDtypeStruct((t, N_CHIPS * n), F32),
        compiler_params=_cp(2 * (2 * tm * k + 2 * k * n + 4 * tm * n) + 4 * MIB),
    )(place, a, w_own)


def _inproj_rest(place, a, b, partial, after):
    t, k = a.shape
    nb, _, n = b.shape
    tm = min(TM_PROJ, t)
    shard = lambda s, p: (p[1] + 1 + s) % nb

    def body(place_ref, a_ref, b_ref, after_ref, partial_ref, o_ref):
        del place_ref, after_ref, partial_ref
        o_ref[...] = jnp.dot(a_ref[...], b_ref[...], preferred_element_type=F32)

    return _pallas(
        body, name="inproj_rest",
        grid_spec=pltpu.PrefetchScalarGridSpec(
            num_scalar_prefetch=1, grid=(nb - 1, t // tm),
            in_specs=[pl.BlockSpec((tm, k), lambda s, i, p: (i, 0)),
                      pl.BlockSpec((None, k, n), lambda s, i, p: (shard(s, p), 0, 0)),
                      pl.BlockSpec((SUBLANES, SLAB), lambda s, i, p: (0, 0)),
                      pl.BlockSpec(memory_space=pl.ANY)],
            out_specs=pl.BlockSpec((tm, n), lambda s, i, p: (i, shard(s, p)))),
        out_shape=jax.ShapeDtypeStruct((t, nb * n), F32),
        input_output_aliases={4: 0},
        compiler_params=_cp(2 * (2 * tm * k + 2 * k * n + 4 * tm * n) + 4 * MIB),
    )(place, a, b, after, partial)


def _matmul_tn(a, b, nb, name):
    t, m = a.shape
    n = b.shape[1] // nb
    tk = min(TK_TN, t)
    tma = min(TM_TN, m)

    def body(a_ref, b_ref, o_ref, o16_ref):
        k = pl.program_id(2)

        @pl.when(k == 0)
        def _():
            o_ref[...] = jnp.zeros_like(o_ref)

        o_ref[...] += lax.dot_general(a_ref[...], b_ref[...], (((0,), (0,)), ((), ())), preferred_element_type=F32)

        @pl.when(k == t // tk - 1)
        def _():
            o16_ref[...] = o_ref[...].astype(BF16)

    blk = pl.BlockSpec((None, tma, n), lambda j, i, k: (j, i, 0))
    return _pallas(
        body, name=name, grid=(nb, m // tma, t // tk),
        in_specs=[pl.BlockSpec((tk, tma), lambda j, i, k: (k, i)), pl.BlockSpec((tk, n), lambda j, i, k: (k, j))],
        out_specs=[blk, blk],
        out_shape=[jax.ShapeDtypeStruct((nb, m, n), F32), jax.ShapeDtypeStruct((nb, m, n), BF16)],
        compiler_params=_cp(2 * (2 * tk * tma + 2 * tk * n + 6 * tma * n) + 8 * MIB),
    )(a, b)


def _outproj(mix, w_out, x, tgt, g_post):
    t, dm = mix.shape
    d = w_out.shape[1]
    tm = min(TM_OUT, t)

    rb = 2 * SUBLANES
    assert tm % rb == 0

    def body(mix_ref, w_ref, x_ref, t_ref, g_ref, loss_ref, dout_ref, do_ref, dmix_ref, gg_ref, o_buf):
        @pl.when(pl.program_id(0) == 0)
        def _():
            loss_ref[...] = jnp.zeros_like(loss_ref)
            gg_ref[...] = jnp.zeros_like(gg_ref)

        w = w_ref[...]
        o_buf[...] = jnp.dot(mix_ref[...], w, preferred_element_type=F32)
        g = g_ref[...]
        fold = lambda v: v[0:SUBLANES] + v[SUBLANES:rb]
        sq_err = jnp.zeros((SUBLANES, d), F32)
        gg = jnp.zeros((SUBLANES, d), F32)
        for c in range(tm // rb):
            rs = pl.ds(c * rb, rb)
            o = o_buf[rs, :]
            r = lax.rsqrt(jnp.mean(o * o, axis=-1, keepdims=True) + EPS)
            nh = o * r
            e = x_ref[rs, :] + nh * g - t_ref[rs, :]
            sq_err = sq_err + fold(e * e)
            dout = e * (1.0 / d)
            dout_ref[rs, :] = dout
            gg = gg + fold(dout * nh)
            dn = dout * g
            do = r * (dn - nh * jnp.mean(dn * nh, axis=-1, keepdims=True))
            do_ref[rs, :] = do.astype(BF16)
        loss_ref[...] += jnp.sum(sq_err) * (0.5 / d)
        gg_ref[0:1, :] += jnp.sum(gg, axis=0, keepdims=True)
        dmix_ref[...] = lax.dot_general(do_ref[...], w, (((1,), (1,)), ((), ())), preferred_element_type=F32)

    row = lambda i: (i, 0)
    fixed = lambda i: (0, 0)
    return _pallas(
        body, name="outproj", grid=(t // tm,),
        in_specs=[pl.BlockSpec((tm, dm), row), pl.BlockSpec((dm, d), fixed), pl.BlockSpec((tm, d), row),
                  pl.BlockSpec((tm, d), row), pl.BlockSpec((1, d), fixed)],
        out_specs=[pl.BlockSpec((SUBLANES, SLAB), fixed), pl.BlockSpec((tm, d), row), pl.BlockSpec((tm, d), row),
                   pl.BlockSpec((tm, dm), row), pl.BlockSpec((SUBLANES, d), fixed)],
        out_shape=[jax.ShapeDtypeStruct((SUBLANES, SLAB), F32), jax.ShapeDtypeStruct((t, d), F32),
                   jax.ShapeDtypeStruct((t, d), BF16), jax.ShapeDtypeStruct((t, dm), F32),
                   jax.ShapeDtypeStruct((SUBLANES, d), F32)],
        scratch_shapes=[pltpu.VMEM((tm, d), F32)],
        compiler_params=_cp(2 * (2 * dm * d + tm * (2 * dm + 4 * d * 3 + 2 * d + 4 * dm)) + 4 * tm * d + 12 * MIB),
    )(mix, w_out, x, tgt, g_post)


def _dh_prenorm_bwd(dproj, w_in, x, dout, g_pre):
    t, d = x.shape
    nb, _, n = w_in.shape
    tm = min(TM_DH, t)

    def body(dp_ref, w_ref, x_ref, dout_ref, g_ref, dx_ref, gg_ref):
        @pl.when(pl.program_id(0) == 0)
        def _():
            gg_ref[...] = jnp.zeros_like(gg_ref)

        dh = None
        for k in range(nb):
            part = lax.dot_general(dp_ref[:, k * n:(k + 1) * n], w_ref[k], (((1,), (1,)), ((), ())),
                                   preferred_element_type=F32)
            dh = part if dh is None else dh + part
        xv = x_ref[...]
        r = lax.rsqrt(jnp.mean(xv * xv, axis=-1, keepdims=True) + EPS)
        xh = xv * r
        gg_ref[0:1, :] += jnp.sum(dh * xh, axis=0, keepdims=True)
        dg = dh * g_ref[...]
        dx_ref[...] = dout_ref[...] + r * (dg - xh * jnp.mean(dg * xh, axis=-1, keepdims=True))

    row = lambda i: (i, 0)
    fixed = lambda i: (0, 0)
    w_spec = pl.BlockSpec(w_in.shape, lambda i: (0, 0, 0), pipeline_mode=pl.Buffered(1))
    return _pallas(
        body, name="dh_prenorm_bwd", grid=(t // tm,),
        in_specs=[pl.BlockSpec((tm, nb * n), row), w_spec,
                  pl.BlockSpec((tm, d), row), pl.BlockSpec((tm, d), row), pl.BlockSpec((1, d), fixed)],
        out_specs=[pl.BlockSpec((tm, d), row), pl.BlockSpec((SUBLANES, d), fixed)],
        out_shape=[jax.ShapeDtypeStruct((t, d), F32), jax.ShapeDtypeStruct((SUBLANES, d), F32)],
        compiler_params=_cp(2 * nb * d * n + 2 * (2 * tm * nb * n + 12 * tm * d) + 16 * tm * d + 4 * MIB),
    )(dproj, w_in, x, dout, g_pre)


def _adamw(w, g, m, v, name):
    r, c = w.shape
    tr = min(TR_ELT, r)
    c1 = 1.0 - ADAM_B1 ** ADAM_STEP
    c2 = 1.0 - ADAM_B2 ** ADAM_STEP

    def body(w_ref, g_ref, m_ref, v_ref, go_ref, d_ref, mo_ref, vo_ref):
        gv = g_ref[...]
        go_ref[...] = gv
        mn = ADAM_B1 * m_ref[...] + (1.0 - ADAM_B1) * gv
        vn = ADAM_B2 * v_ref[...] + (1.0 - ADAM_B2) * (gv * gv)
        d_ref[...] = -ADAM_LR * ((mn / c1) / (jnp.sqrt(vn / c2) + ADAM_EPS) + ADAM_WD * w_ref[...])
        mo_ref[...] = mn
        vo_ref[...] = vn

    spec = pl.BlockSpec((tr, c), lambda i: (i, 0))
    sds = jax.ShapeDtypeStruct((r, c), F32)
    return _pallas(
        body, name=name, grid=(r // tr,), in_specs=[spec] * 4, out_specs=[spec] * 4, out_shape=[sds] * 4,
        compiler_params=_cp(2 * 8 * 4 * tr * c + 8 * MIB),
    )(w, g, m, v)


def _adamw_small(g, w, m, v):
    r, c = w.shape
    c1 = 1.0 - ADAM_B1 ** ADAM_STEP
    c2 = 1.0 - ADAM_B2 ** ADAM_STEP

    def body(g_ref, w_ref, m_ref, v_ref, d_ref, mo_ref, vo_ref):
        gv = g_ref[...]
        mn = ADAM_B1 * m_ref[...] + (1.0 - ADAM_B1) * gv
        vn = ADAM_B2 * v_ref[...] + (1.0 - ADAM_B2) * (gv * gv)
        d_ref[...] = -ADAM_LR * ((mn / c1) / (jnp.sqrt(vn / c2) + ADAM_EPS) + ADAM_WD * w_ref[...])
        mo_ref[...] = mn
        vo_ref[...] = vn

    sds = jax.ShapeDtypeStruct((r, c), F32)
    return _pallas(
        body, name="adamw_small", out_shape=[sds] * 3,
        compiler_params=_cp(12 * 4 * r * c + 8 * MIB),
    )(g, w, m, v)


def _allreduce_small(v):
    r, c = v.shape
    assert r % (2 * SUBLANES) == 0
    h = r // 2

    def body(v_ref, out_ref, got_ref, chip_ref, slots_ref, send, recv):
        x, y, cc = _my_place()
        k = 2 * x + y
        sib = (x, y, 1 - cc)

        def rc(s, src, dst, to):
            return pltpu.make_async_remote_copy(src_ref=src, dst_ref=dst, send_sem=send.at[s], recv_sem=recv.at[s],
                                                device_id=to, device_id_type=MESH)

        pair = rc(0, v_ref, got_ref, sib)
        pair.start()
        pair.wait()
        chip_ref[...] = v_ref[...] + got_ref[...]
        mine = pl.ds(pl.multiple_of(cc * h, SUBLANES), h)
        theirs = pl.ds(pl.multiple_of((1 - cc) * h, SUBLANES), h)
        chips = _other_chips(x, y)
        sent = []
        for j, (cx, cy) in enumerate(chips):
            cp = rc(1 + j, chip_ref.at[mine], slots_ref.at[k], (cx, cy, cc))
            cp.start()
            sent.append(cp)
        slots_ref[k] = chip_ref[mine, :]
        for j, (cx, cy) in enumerate(chips):
            rc(1 + j, chip_ref.at[mine], slots_ref.at[2 * cx + cy], (cx, cy, cc)).wait_recv()
        total = slots_ref[0]
        for kk in range(1, N_CHIPS):
            total = total + slots_ref[kk]
        out_ref[mine, :] = total
        for cp in sent:
            cp.wait_send()
        share = rc(N_CHIPS, out_ref.at[mine], out_ref.at[mine], sib)
        share.start()
        rc(N_CHIPS, out_ref.at[theirs], out_ref.at[theirs], sib).wait_recv()
        share.wait_send()

    vm = pl.BlockSpec(memory_space=pltpu.VMEM)
    return _pallas(
        body, name="allreduce_small", out_shape=jax.ShapeDtypeStruct((r, c), F32), in_specs=[vm], out_specs=vm,
        scratch_shapes=[pltpu.VMEM((r, c), F32), pltpu.VMEM((r, c), F32), pltpu.VMEM((N_CHIPS, h, c), F32),
                        pltpu.SemaphoreType.DMA((N_CHIPS + 1,)), pltpu.SemaphoreType.DMA((N_CHIPS + 1,))],
        compiler_params=_cp(6 * 4 * r * c + 8 * MIB),
    )(v)


def _zoh(a_re, a_im, log_dt, b_re2, b_im2, expand):
    dt = jnp.exp(log_dt)
    mag = jnp.exp(a_re * dt)
    lbr, lbi = mag * jnp.cos(a_im * dt), mag * jnp.sin(a_im * dt)
    nr, ni = lbr - 1.0, lbi
    den = a_re * a_re + a_im * a_im
    qr = (nr * a_re + ni * a_im) / den
    qi = (ni * a_re - nr * a_im) / den
    qr2 = jnp.dot(qr, expand, precision=lax.Precision.HIGHEST, preferred_element_type=F32)
    qi2 = jnp.dot(qi, expand, precision=lax.Precision.HIGHEST, preferred_element_type=F32)
    return lbr, lbi, qr2 * b_re2 - qi2 * b_im2, qr2 * b_im2 + qi2 * b_re2


def _zoh_fwd(a_re, a_im, log_dt, b_re2, b_im2, expand, power):
    g, p = a_re.shape

    def body(ar_ref, ai_ref, ld_ref, br_ref, bi_ref, e_ref, lbr_ref, lbi_ref, bbr_ref, bbi_ref, pw_ref):
        ar, ai, ld = ar_ref[...], ai_ref[...], ld_ref[...]
        lbr, lbi, bbr, bbi = _zoh(ar, ai, ld, br_ref[...], bi_ref[...], e_ref[...])
        lbr_ref[...], lbi_ref[...], bbr_ref[...], bbi_ref[...] = lbr, lbi, bbr, bbi
        dt = jnp.exp(ld) * float(power)
        mag = jnp.exp(ar * dt)
        pw_ref[0] = mag * jnp.cos(ai * dt)
        pw_ref[1] = mag * jnp.sin(ai * dt)

    gp = jax.ShapeDtypeStruct((g, p), F32)
    gph = jax.ShapeDtypeStruct(b_re2.shape, F32)
    return _pallas(
        body, name="zoh_fwd", out_shape=[gp, gp, gph, gph, jax.ShapeDtypeStruct((2, g, p), F32)],
        compiler_params=_cp(16 * MIB),
    )(a_re, a_im, log_dt, b_re2, b_im2, expand)


def _zoh_bwd(a_re, a_im, log_dt, b_re2, b_im2, expand, g_lbr, g_lbi, g_bbr, g_bbi):
    def body(ar_ref, ai_ref, ld_ref, br_ref, bi_ref, e_ref, c0, c1, c2, c3, gar, gai, gld, gbr, gbi):
        e = e_ref[...]
        _, vjp = jax.vjp(lambda a, b, c, d, f: _zoh(a, b, c, d, f, e),
                         ar_ref[...], ai_ref[...], ld_ref[...], br_ref[...], bi_ref[...])
        gar[...], gai[...], gld[...], gbr[...], gbi[...] = vjp((c0[...], c1[...], c2[...], c3[...]))

    sds = lambda a: jax.ShapeDtypeStruct(a.shape, F32)
    return _pallas(
        body, name="zoh_bwd", out_shape=[sds(a_re), sds(a_im), sds(log_dt), sds(b_re2), sds(b_im2)],
        compiler_params=_cp(16 * MIB),
    )(a_re, a_im, log_dt, b_re2, b_im2, expand, g_lbr, g_lbi, g_bbr, g_bbi)


def _blockdiag(blocks):
    nq, _, r, c = blocks.shape
    eye = jnp.eye(GROUPS_PER_SLAB, dtype=blocks.dtype)
    out = blocks[:, :, :, None, :] * eye[None, :, None, :, None]
    return out.reshape(nq, GROUPS_PER_SLAB * r, GROUPS_PER_SLAB * c)


def _blockdiag_take(dense, r, c):
    half = GROUPS_PER_SLAB * c
    blocks = [jnp.stack([dense[:, i * r:(i + 1) * r, part * half + i * c:part * half + (i + 1) * c]
                         for part in (0, 1)], axis=2) for i in range(GROUPS_PER_SLAB)]
    return jnp.stack(blocks, axis=1)


def _scan_slab(s_ref, row0, q, lam_ref, pw_ref, car_ref, steps, reverse, prev_ref=None, prev_row0=0, glam_ref=None):
    sign = -1.0 if reverse else 1.0
    half = SLAB_STATES // SLAB
    cols = [(q * 2 * half + m, q * 2 * half + half + m, q * SLAB_STATES + m * SLAB) for m in range(half)]
    nm = len(cols)
    full = (SUBLANES, SLAB)
    lam = [(jnp.broadcast_to(lam_ref[0:1, pl.ds(cl, SLAB)], full),
            jnp.broadcast_to(sign * lam_ref[1:2, pl.ds(cl, SLAB)], full)) for (_, _, cl) in cols]

    def step_rows(jj, base):
        j = (steps - 1 - jj) if reverse else jj
        return j, pl.ds(base + j * SUBLANES, SUBLANES)

    def pass1(jj, car):
        _, rows = step_rows(jj, row0)
        out = []
        for m, (cr, ci, _) in enumerate(cols):
            sr, si = car[2 * m], car[2 * m + 1]
            lr, li = lam[m]
            nr = lr * sr - li * si + s_ref[cr, rows, :]
            ni = lr * si + li * sr + s_ref[ci, rows, :]
            s_ref[cr, rows, :] = nr
            s_ref[ci, rows, :] = ni
            out += [nr, ni]
        return tuple(out)

    def run_steps(step_fn, carry):
        for jj in range(steps):
            carry = step_fn(jj, carry)
        return carry

    ends = run_steps(pass1, tuple(jnp.zeros(full, F32) for _ in range(2 * nm)))

    entry = []
    for m, (cr, ci, cl) in enumerate(cols):
        ljr = pw_ref[0:1, pl.ds(cl, SLAB)]
        lji = sign * pw_ref[1:2, pl.ds(cl, SLAB)]
        c_r = car_ref[cr, 0:1, :]
        c_i = car_ref[ci, 0:1, :]
        rows_r, rows_i = [None] * SUBLANES, [None] * SUBLANES
        order = range(SUBLANES - 1, -1, -1) if reverse else range(SUBLANES)
        for b in order:
            rows_r[b], rows_i[b] = c_r, c_i
            e_r, e_i = ends[2 * m][b:b + 1], ends[2 * m + 1][b:b + 1]
            c_r, c_i = ljr * c_r - lji * c_i + e_r, ljr * c_i + lji * c_r + e_i
        car_ref[cr, 0:1, :] = c_r
        car_ref[ci, 0:1, :] = c_i
        entry.append((jnp.concatenate(rows_r, axis=0), jnp.concatenate(rows_i, axis=0)))

    def pass2(jj, carry):
        j, rows = step_rows(jj, row0)
        decayed, acc = carry[:2 * nm], carry[2 * nm:]
        out_d, out_a = [], []
        for m, (cr, ci, cl) in enumerate(cols):
            lr, li = lam[m]
            dr, di = decayed[2 * m], decayed[2 * m + 1]
            dr, di = lr * dr - li * di, lr * di + li * dr
            nr = s_ref[cr, rows, :] + dr
            ni = s_ref[ci, rows, :] + di
            s_ref[cr, rows, :] = nr
            s_ref[ci, rows, :] = ni
            out_d += [dr, di]
            if prev_ref is not None:
                prow = pl.ds(prev_row0 + (j - 1) * SUBLANES, SUBLANES)
                qr = prev_ref[cr, prow, :]
                qi = prev_ref[ci, prow, :]
                out_a += [acc[2 * m] + (nr * qr + ni * qi), acc[2 * m + 1] + (ni * qr - nr * qi)]
        return tuple(out_d) + tuple(out_a)

    n_acc = 2 * nm if prev_ref is not None else 0
    init = tuple(e for pair in entry for e in pair) + tuple(jnp.zeros(full, F32) for _ in range(n_acc))
    accs = run_steps(pass2, init)[2 * nm:]
    if prev_ref is not None:
        for m, (_, _, cl) in enumerate(cols):
            glam_ref[0:1, pl.ds(cl, SLAB)] += jnp.sum(accs[2 * m], axis=0, keepdims=True)
            glam_ref[1:2, pl.ds(cl, SLAB)] += jnp.sum(accs[2 * m + 1], axis=0, keepdims=True)


def _permute_rows_f32(perm_bf16, v):
    hi = v.astype(BF16)
    lo = (v - hi.astype(F32)).astype(BF16)
    return (jnp.dot(perm_bf16, hi, preferred_element_type=F32) + jnp.dot(perm_bf16, lo, preferred_element_type=F32))


def _put_slab(s_ref, rows, q, val):
    per = 2 * SLAB_STATES // SLAB
    for i in range(per):
        s_ref[q * per + i, rows, :] = val[:, i * SLAB:(i + 1) * SLAB]


def _get_slab(s_ref, rows, q):
    per = 2 * SLAB_STATES // SLAB
    return jnp.concatenate([s_ref[q * per + i, rows, :] for i in range(per)], axis=1)


def _step_major_perm(tt):
    r = jnp.arange(tt)
    held = (r % SUBLANES) * (tt // SUBLANES) + r // SUBLANES
    return held[:, None] == r[None, :]


def _ssm_fwd(proj, u_block, bq, cq, lam, pw, d_skip):
    t = proj.shape[0]
    nq, ds, w2 = bq.shape
    assert ds == SLAB and w2 == 2 * SLAB_STATES
    dssm = nq * SLAB
    width = nq * w2
    ntile = width // SLAB
    tt = min(T_SCAN, t)
    steps = tt // SUBLANES
    perm = _step_major_perm(tt)
    pm, pmt = perm.astype(BF16), perm.T.astype(BF16)

    def body(u_ref, pm_ref, pmt_ref, bq_ref, cq_ref, lam_ref, pw_ref, d_ref, y_ref, cin_ref, s_ref, car_ref):
        @pl.when(pl.program_id(0) == 0)
        def _():
            car_ref[...] = jnp.zeros_like(car_ref)

        cin_ref[...] = jnp.broadcast_to(car_ref[:, 0:1, :], cin_ref.shape)
        u = u_ref[...]
        ub = jnp.dot(pm_ref[...], u.astype(BF16), preferred_element_type=F32).astype(BF16)
        everything = slice(None)
        for q in range(nq):
            _put_slab(s_ref, everything, q,
                      jnp.dot(ub[:, q * SLAB:(q + 1) * SLAB], bq_ref[q], preferred_element_type=F32))
        for q in range(nq):
            _scan_slab(s_ref, 0, q, lam_ref, pw_ref, car_ref, steps, reverse=False)
        y_sm = jnp.concatenate(
            [jnp.dot(_get_slab(s_ref, everything, q).astype(BF16), cq_ref[q], preferred_element_type=F32)
             for q in range(nq)], axis=1)
        y_ref[...] = _permute_rows_f32(pmt_ref[...], y_sm) + d_ref[...] * u

    c3 = lambda i: (0, 0, 0)
    c2 = lambda i: (0, 0)
    return _pallas(
        body, name="ssm_fwd", grid=(t // tt,),
        in_specs=[pl.BlockSpec((tt, dssm), lambda i: (i, u_block)), pl.BlockSpec((tt, tt), c2),
                  pl.BlockSpec((tt, tt), c2), pl.BlockSpec(bq.shape, c3),
                  pl.BlockSpec(cq.shape, c3), pl.BlockSpec(lam.shape, c2), pl.BlockSpec(pw.shape, c2),
                  pl.BlockSpec((1, dssm), c2)],
        out_specs=[pl.BlockSpec((tt, dssm), lambda i: (i, 0)),
                   pl.BlockSpec((None, ntile, SUBLANES, SLAB), lambda i: (i, 0, 0, 0))],
        out_shape=[jax.ShapeDtypeStruct((t, dssm), F32), jax.ShapeDtypeStruct((t // tt, ntile, SUBLANES, SLAB), F32)],
        scratch_shapes=[pltpu.VMEM((ntile, tt, SLAB), F32), pltpu.VMEM((ntile, SUBLANES, SLAB), F32)],
        compiler_params=_cp(2 * (8 * tt * dssm + 4 * nq * ds * w2) + 4 * tt * width + 16 * MIB,
                            dimension_semantics=("arbitrary",)),
    )(proj, pm, pmt, bq, cq, lam, pw, d_skip)


def _ssm_bwd(proj, u_block, dy, cin, dproj, bq, cq, lam, pw, d_skip):
    t = proj.shape[0]
    nq, ds, w2 = bq.shape
    dssm = nq * SLAB
    width = nq * w2
    ntile = width // SLAB
    tt = min(T_SCAN, t)
    nt = t // tt
    steps = tt // SUBLANES
    halo = SUBLANES
    perm = _step_major_perm(tt)
    pm, pmt = perm.astype(BF16), perm.T.astype(BF16)

    def body(u_ref, dy_ref, cin_ref, dp_any, pm_ref, pmt_ref, bq_ref, cq_ref, lam_ref, pw_ref, d_ref,
             du_ref, gb_ref, gc_ref, glam_ref, gd_ref, s_ref, gs_ref, car_f, car_b):
        del dp_any

        @pl.when(pl.program_id(0) == 0)
        def _():
            car_b[...] = jnp.zeros_like(car_b)
            gb_ref[...] = jnp.zeros_like(gb_ref)
            gc_ref[...] = jnp.zeros_like(gc_ref)
            glam_ref[...] = jnp.zeros_like(glam_ref)
            gd_ref[...] = jnp.zeros_like(gd_ref)

        u = u_ref[...]
        dyv = dy_ref[...]
        gd_ref[0:1, :] += jnp.sum(dyv * u, axis=0, keepdims=True)
        pmv = pm_ref[...]
        ub = jnp.dot(pmv, u.astype(BF16), preferred_element_type=F32).astype(BF16)
        dyb = jnp.dot(pmv, dyv.astype(BF16), preferred_element_type=F32).astype(BF16)
        car_f[...] = cin_ref[...]
        data = slice(halo, halo + tt)
        everything = slice(None)
        for q in range(nq):
            _put_slab(s_ref, data, q, jnp.dot(ub[:, q * SLAB:(q + 1) * SLAB], bq_ref[q], preferred_element_type=F32))
        for q in range(nq):
            _scan_slab(s_ref, halo, q, lam_ref, pw_ref, car_f, steps, reverse=False)
        last_step = s_ref[:, halo + tt - SUBLANES:halo + tt, :]
        s_ref[:, 0:halo, :] = jnp.concatenate([cin_ref[:, 0:1, :], last_step[:, 0:SUBLANES - 1, :]], axis=1)
        tn = (((0,), (0,)), ((), ()))
        nt_dims = (((1,), (1,)), ((), ()))
        for q in range(nq):
            sl = slice(q * SLAB, (q + 1) * SLAB)
            gc_ref[q] += lax.dot_general(dyb[:, sl], _get_slab(s_ref, data, q).astype(BF16), tn,
                                         preferred_element_type=F32)
            _put_slab(gs_ref, everything, q,
                      lax.dot_general(dyb[:, sl], cq_ref[q], nt_dims, preferred_element_type=F32))
        for q in range(nq):
            _scan_slab(gs_ref, 0, q, lam_ref, pw_ref, car_b, steps, reverse=True,
                       prev_ref=s_ref, prev_row0=halo, glam_ref=glam_ref)
        du_parts = []
        for q in range(nq):
            sl = slice(q * SLAB, (q + 1) * SLAB)
            gsb = _get_slab(gs_ref, everything, q).astype(BF16)
            du_parts.append(lax.dot_general(gsb, bq_ref[q], nt_dims, preferred_element_type=F32))
            gb_ref[q] += lax.dot_general(ub[:, sl], gsb, tn, preferred_element_type=F32)
        du_sm = jnp.concatenate(du_parts, axis=1)
        du_ref[...] = (_permute_rows_f32(pmt_ref[...], du_sm) + dyv * d_ref[...]).astype(BF16)

    c3 = lambda i: (0, 0, 0)
    c2 = lambda i: (0, 0)
    rev = lambda i: (nt - 1 - i, 0)
    dense = jax.ShapeDtypeStruct((nq, SLAB, w2), F32)
    gp = lam.shape[1]
    return _pallas(
        body, name="ssm_bwd", grid=(nt,),
        in_specs=[pl.BlockSpec((tt, dssm), lambda i: (nt - 1 - i, u_block)), pl.BlockSpec((tt, dssm), rev),
                  pl.BlockSpec((None, ntile, SUBLANES, SLAB), lambda i: (nt - 1 - i, 0, 0, 0)),
                  pl.BlockSpec(memory_space=pl.ANY), pl.BlockSpec((tt, tt), c2), pl.BlockSpec((tt, tt), c2),
                  pl.BlockSpec(bq.shape, c3), pl.BlockSpec(cq.shape, c3),
                  pl.BlockSpec(lam.shape, c2), pl.BlockSpec(pw.shape, c2), pl.BlockSpec((1, dssm), c2)],
        out_specs=[pl.BlockSpec((tt, dssm), lambda i: (nt - 1 - i, u_block)), pl.BlockSpec(dense.shape, c3),
                   pl.BlockSpec(dense.shape, c3), pl.BlockSpec((SUBLANES, gp), c2), pl.BlockSpec((SUBLANES, dssm), c2)],
        out_shape=[jax.ShapeDtypeStruct(dproj.shape, dproj.dtype), dense, dense,
                   jax.ShapeDtypeStruct((SUBLANES, gp), F32), jax.ShapeDtypeStruct((SUBLANES, dssm), F32)],
        scratch_shapes=[pltpu.VMEM((ntile, tt + halo, SLAB), F32), pltpu.VMEM((ntile, tt, SLAB), F32),
                        pltpu.VMEM((ntile, SUBLANES, SLAB), F32), pltpu.VMEM((ntile, SUBLANES, SLAB), F32)],
        input_output_aliases={3: 0},
        compiler_params=_cp(2 * (10 * tt * dssm + 4 * nq * ds * w2 + 8 * nq * SLAB * w2)
                            + 8 * tt * width + 12 * MIB, dimension_semantics=("arbitrary",)),
    )(proj, dy, cin, dproj, pm, pmt, bq, cq, lam, pw, d_skip)


def _gate_fwd(proj, y, conv_w, conv_b, w_glu, b_glu, dc):
    t = proj.shape[0]
    dssm = y.shape[1]
    assert dc == dssm
    tm = min(TM_GATE, t)
    halo = SUBLANES

    rb = 2 * SUBLANES
    assert tm % rb == 0

    def body(b_ref, c_ref, v_ref, zc_ref, zs_ref, y_ref, cw_ref, cb_ref, wg_ref, bg_ref, mix_ref,
             cv_buf, sh1_buf, sh2_buf, ge_buf, geb_buf, gl_buf):
        @pl.when(pl.program_id(0) == 0)
        def _():
            cv_buf[0:halo, :] = jnp.zeros((halo, dc), F32)

        cv_buf[halo:, :] = c_ref[...] * v_ref[...]
        sh1_buf[...] = cv_buf[halo - 1:halo - 1 + tm, :]
        sh2_buf[...] = cv_buf[halo - 2:halo - 2 + tm, :]
        w0, w1, w2, cb = cw_ref[0:1, :], cw_ref[1:2, :], cw_ref[2:3, :], cb_ref[...]
        chunks = [pl.ds(r * rb, rb) for r in range(tm // rb)]
        for r, rs in enumerate(chunks):
            ge, _ = _gelu(y_ref[rs, :])
            ge_buf[rs, :] = ge
            geb_buf[rs, :] = ge.astype(BF16)
        gl_buf[...] = jnp.dot(geb_buf[...], wg_ref[...], preferred_element_type=F32) + bg_ref[...]
        for r, rs in enumerate(chunks):
            conv = cb + w2 * cv_buf[pl.ds(halo + r * rb, rb), :] + w1 * sh1_buf[rs, :] + w0 * sh2_buf[rs, :]
            sz, _ = _silu(zc_ref[rs, :])
            mix_ref[rs, 0:dc] = (b_ref[rs, :] * conv * sz).astype(BF16)
        cv_buf[0:halo, :] = cv_buf[tm:tm + halo, :]
        for r, rs in enumerate(chunks):
            szs, _ = _silu(zs_ref[rs, :])
            mix_ref[rs, dc:] = (ge_buf[rs, :] * jax.nn.sigmoid(gl_buf[rs, :]) * szs).astype(BF16)

    col = lambda j: pl.BlockSpec((tm, dc), lambda i, j=j: (i, j))
    fixed = lambda i: (0, 0)
    return _pallas(
        body, name="gate_fwd", grid=(t // tm,),
        in_specs=[col(0), col(1), col(2), col(3), col(5), pl.BlockSpec((tm, dssm), lambda i: (i, 0)),
                  pl.BlockSpec(conv_w.shape, fixed), pl.BlockSpec((1, dc), fixed),
                  pl.BlockSpec(w_glu.shape, fixed), pl.BlockSpec((1, dssm), fixed)],
        out_specs=pl.BlockSpec((tm, dc + dssm), lambda i: (i, 0)),
        out_shape=jax.ShapeDtypeStruct((t, dc + dssm), BF16),
        scratch_shapes=[pltpu.VMEM((tm + halo, dc), F32), pltpu.VMEM((tm, dc), F32), pltpu.VMEM((tm, dc), F32),
                        pltpu.VMEM((tm, dssm), F32), pltpu.VMEM((tm, dssm), BF16), pltpu.VMEM((tm, dssm), F32)],
        compiler_params=_cp(2 * (6 * 4 * tm * dc + 2 * tm * (dc + dssm) + 2 * dssm * dssm) + 24 * tm * dc + 8 * MIB,
                            dimension_semantics=("arbitrary",)),
    )(proj, proj, proj, proj, proj, y, conv_w, conv_b, w_glu, b_glu)


def _gate_bwd(proj, y, dmix, conv_w, conv_b, w_glu, b_glu, dc):
    t = proj.shape[0]
    dssm = y.shape[1]
    tm = min(TM_GATE, t)
    nt = t // tm
    halo = SUBLANES
    blocks_per_tile = tm // halo

    rb = 2 * SUBLANES
    assert tm % rb == 0
    n_chunk = tm // rb

    def body(b_ref, c_ref, v_ref, zc_ref, zs_ref, cp_ref, vp_ref, y_ref, dm_ref, cw_ref, cb_ref, wg_ref, bg_ref,
             dp_ref, dy_ref, gs_ref, gwg_ref, cv_buf, dc_buf, sh1_buf, sh2_buf, geb_buf, dglb_buf, ge_buf, th_buf):
        i = pl.program_id(0)

        @pl.when(i == 0)
        def _():
            dc_buf[tm:, :] = jnp.zeros((halo, dc), F32)
            gs_ref[...] = jnp.zeros_like(gs_ref)
            gwg_ref[...] = jnp.zeros_like(gwg_ref)

        def rows(r):
            return pl.ds(r * rb, rb)

        def rows_after_halo(r):
            return pl.ds(halo + r * rb, rb)

        def chunks(step, carry):
            for r in range(n_chunk):
                carry = step(r, carry)
            return carry

        def fold(v):
            return v[0:SUBLANES] + v[SUBLANES:rb]

        first_tile = (i == nt - 1)
        cv_buf[0:halo, :] = jnp.where(first_tile, 0.0, cp_ref[...] * vp_ref[...])
        cv_buf[halo:, :] = c_ref[...] * v_ref[...]
        sh1_buf[...] = cv_buf[halo - 1:halo - 1 + tm, :]
        sh2_buf[...] = cv_buf[halo - 2:halo - 2 + tm, :]
        w0, w1, w2, cb = cw_ref[0:1, :], cw_ref[1:2, :], cw_ref[2:3, :], cb_ref[...]

        def conv_pass_a(r, carry):
            rs = rows(r)
            bv, zc, dyc = b_ref[rs, :], zc_ref[rs, :], dm_ref[rs, 0:dc]
            conv = cb + w2 * cv_buf[rows_after_halo(r), :] + w1 * sh1_buf[rs, :] + w0 * sh2_buf[rs, :]
            sz, sgc = _silu(zc)
            dp_ref[rs, 0:dc] = (dyc * conv * sz).astype(BF16)
            dp_ref[rs, 3 * dc:4 * dc] = (dyc * bv * conv * _dsilu(zc, sgc)).astype(BF16)
            dc_buf[rs, :] = dyc * bv * sz
            return carry

        chunks(conv_pass_a, 0)
        sh1_buf[...] = dc_buf[1:1 + tm, :]
        sh2_buf[...] = dc_buf[2:2 + tm, :]

        def conv_pass_b(r, acc):
            rs = rows(r)
            dconv, d1, d2 = dc_buf[rs, :], sh1_buf[rs, :], sh2_buf[rs, :]
            cv = cv_buf[rows_after_halo(r), :]
            dcv = w2 * dconv + w1 * d1 + w0 * d2
            dp_ref[rs, dc:2 * dc] = (dcv * v_ref[rs, :]).astype(BF16)
            dp_ref[rs, 2 * dc:3 * dc] = (dcv * c_ref[rs, :]).astype(BF16)
            return (acc[0] + fold(cv * d2), acc[1] + fold(cv * d1), acc[2] + fold(cv * dconv), acc[3] + fold(dconv))

        zero = jnp.zeros((SUBLANES, dc), F32)
        sums = chunks(conv_pass_b, (zero, zero, zero, zero))
        for k in range(4):
            gs_ref[k:k + 1, :] += jnp.sum(sums[k], axis=0, keepdims=True)
        dc_buf[tm:, :] = dc_buf[0:halo, :]

        def glu_pass_a(r, carry):
            rs = rows(r)
            ge, th = _gelu(y_ref[rs, :])
            geb_buf[rs, :] = ge.astype(BF16)
            ge_buf[rs, :] = ge
            th_buf[rs, :] = th
            return carry

        chunks(glu_pass_a, 0)
        wg = wg_ref[...]
        gl_buf = cv_buf.at[halo:halo + tm]
        keep_buf = dc_buf.at[0:tm]
        gl_buf[...] = jnp.dot(geb_buf[...], wg, preferred_element_type=F32) + bg_ref[...]
        dp_ref[:, 4 * dc:5 * dc] = jnp.zeros((tm, dc), BF16)

        def glu_pass_b(r, acc):
            rs = rows(r)
            ge = ge_buf[rs, :]
            sg = jax.nn.sigmoid(gl_buf[rs, :])
            zs, dys = zs_ref[rs, :], dm_ref[rs, dc:]
            szs, sgs = _silu(zs)
            dp_ref[rs, 5 * dc:] = (dys * ge * sg * _dsilu(zs, sgs)).astype(BF16)
            d_ys = dys * szs
            dgl = d_ys * ge * sg * (1.0 - sg)
            dglb_buf[rs, :] = dgl.astype(BF16)
            keep_buf[rs, :] = d_ys * sg
            return acc + fold(dgl)

        gs_ref[4:5, :] += jnp.sum(chunks(glu_pass_b, zero), axis=0, keepdims=True)
        dglb = dglb_buf[...]
        gwg_ref[...] += lax.dot_general(geb_buf[...], dglb, (((0,), (0,)), ((), ())), preferred_element_type=F32)
        gl_buf[...] = lax.dot_general(dglb, wg, (((1,), (1,)), ((), ())), preferred_element_type=F32)

        def glu_pass_c(r, carry):
            rs = rows(r)
            dy_ref[rs, :] = (keep_buf[rs, :] + gl_buf[rs, :]) * _dgelu(y_ref[rs, :], th_buf[rs, :])
            return carry

        chunks(glu_pass_c, 0)

    col = lambda j: pl.BlockSpec((tm, dc), lambda i, j=j: (nt - 1 - i, j))
    prev = lambda j: pl.BlockSpec((halo, dc), lambda i, j=j: (jnp.maximum((nt - 1 - i) * blocks_per_tile - 1, 0), j))
    rev = lambda i: (nt - 1 - i, 0)
    fixed = lambda i: (0, 0)
    return _pallas(
        body, name="gate_bwd", grid=(nt,),
        in_specs=[col(0), col(1), col(2), col(3), col(5), prev(1), prev(2), pl.BlockSpec((tm, dssm), rev),
                  pl.BlockSpec((tm, dc + dssm), rev), pl.BlockSpec(conv_w.shape, fixed), pl.BlockSpec((1, dc), fixed),
                  pl.BlockSpec(w_glu.shape, fixed), pl.BlockSpec((1, dssm), fixed)],
        out_specs=[pl.BlockSpec((tm, 6 * dc), rev), pl.BlockSpec((tm, dssm), rev),
                   pl.BlockSpec((2 * SUBLANES, dc), fixed), pl.BlockSpec((dssm, dssm), fixed)],
        out_shape=[jax.ShapeDtypeStruct((t, 6 * dc), BF16), jax.ShapeDtypeStruct((t, dssm), F32),
                   jax.ShapeDtypeStruct((2 * SUBLANES, dc), F32), jax.ShapeDtypeStruct((dssm, dssm), F32)],
        scratch_shapes=[pltpu.VMEM((tm + halo, dc), F32), pltpu.VMEM((tm + halo, dc), F32), pltpu.VMEM((tm, dc), F32),
                        pltpu.VMEM((tm, dc), F32), pltpu.VMEM((tm, dssm), BF16), pltpu.VMEM((tm, dssm), BF16),
                        pltpu.VMEM((tm, dssm), F32), pltpu.VMEM((tm, dssm), F32)],
        compiler_params=_cp(2 * (6 * 4 * tm * dc + 8 * tm * dc + 12 * tm * dc + 4 * tm * dc + 6 * dssm * dssm)
                            + 28 * tm * dc + 8 * MIB, dimension_semantics=("arbitrary",)),
    )(proj, proj, proj, proj, proj, proj, proj, y, dmix, conv_w, conv_b, w_glu, b_glu)


def _allgather_flat(v, name):
    r, c = v.shape
    rels = [(dx, dy, dc) for dx in (0, 1) for dy in (0, 1) for dc in (0, 1)][1:]

    def body(v_ref, out_ref, send, recv):
        x, y, cc = _my_place()
        me = 4 * x + 2 * y + cc

        def peer(rel):
            dx, dy, dc = rel
            return (1 - x if dx else x, 1 - y if dy else y, 1 - cc if dc else cc)

        def rc(s, slot, to):
            return pltpu.make_async_remote_copy(src_ref=v_ref, dst_ref=out_ref.at[slot], send_sem=send.at[s],
                                                recv_sem=recv.at[s], device_id=to, device_id_type=MESH)

        sent = []
        for s, rel in enumerate(rels):
            cp = rc(s, me, peer(rel))
            cp.start()
            sent.append(cp)
        out_ref[me] = v_ref[...]
        for s, rel in enumerate(rels):
            px, py, pc = peer(rel)
            rc(s, 4 * px + 2 * py + pc, (px, py, pc)).wait_recv()
        for cp in sent:
            cp.wait_send()

    return _pallas(
        body, name=name, out_shape=jax.ShapeDtypeStruct((N_DEV, r, c), F32),
        in_specs=[pl.BlockSpec(memory_space=pltpu.VMEM)], out_specs=pl.BlockSpec(memory_space=pltpu.VMEM),
        scratch_shapes=[pltpu.SemaphoreType.DMA((N_DEV - 1,)), pltpu.SemaphoreType.DMA((N_DEV - 1,))],
        compiler_params=_cp((N_DEV + 2) * 4 * r * c + 8 * MIB),
    )(v)


def _rs_pair_exchange(grads, name):
    n = len(grads)

    def body(*refs):
        ins, outs = refs[:n], refs[n:2 * n]
        send, recv = refs[2 * n:]
        x, y, c = _my_place()
        cps = []
        for t in range(n):
            cp = pltpu.make_async_remote_copy(src_ref=ins[t].at[:, 1 - c], dst_ref=outs[t], send_sem=send.at[t],
                                              recv_sem=recv.at[t], device_id=(x, y, 1 - c), device_id_type=MESH)
            cp.start()
            cps.append(cp)
        for cp in cps:
            cp.wait()

    any_spec = pl.BlockSpec(memory_space=pl.ANY)
    return _pallas(
        body, name=name, in_specs=[any_spec] * n, out_specs=[any_spec] * n,
        out_shape=[jax.ShapeDtypeStruct((g.shape[0],) + g.shape[2:], g.dtype) for g in grads],
        scratch_shapes=[pltpu.SemaphoreType.DMA((n,)), pltpu.SemaphoreType.DMA((n,))],
        compiler_params=_cp(16 * MIB),
    )(*grads)


def _rs_pair_add(place, grad, got, name):
    nk, _, r2, c = grad.shape
    tr = min(TR_ELT, r2)
    other = lambda s, p: (p[1] + 1 + s) % nk

    def send_body(place_ref, g_ref, r_ref, o_ref):
        del place_ref
        o_ref[...] = (g_ref[...] + r_ref[...].astype(F32)).astype(BF16)

    to_send = _pallas(
        send_body, name=name + "_send",
        grid_spec=pltpu.PrefetchScalarGridSpec(
            num_scalar_prefetch=1, grid=(nk - 1, r2 // tr),
            in_specs=[pl.BlockSpec((None, None, tr, c), lambda s, i, p: (other(s, p), p[0], i, 0)),
                      pl.BlockSpec((None, tr, c), lambda s, i, p: (other(s, p), i, 0))],
            out_specs=pl.BlockSpec((None, tr, c), lambda s, i, p: (other(s, p), i, 0))),
        out_shape=jax.ShapeDtypeStruct((nk, r2, c), BF16),
        compiler_params=_cp(2 * 8 * tr * c + 8 * MIB),
    )(place, grad, got)

    def keep_body(place_ref, g_ref, r_ref, o_ref):
        del place_ref
        o_ref[...] = g_ref[...] + r_ref[...].astype(F32)

    to_keep = _pallas(
        keep_body, name=name + "_keep",
        grid_spec=pltpu.PrefetchScalarGridSpec(
            num_scalar_prefetch=1, grid=(r2 // tr,),
            in_specs=[pl.BlockSpec((None, None, tr, c), lambda i, p: (p[1], p[0], i, 0)),
                      pl.BlockSpec((None, tr, c), lambda i, p: (p[1], i, 0))],
            out_specs=pl.BlockSpec((tr, c), lambda i, p: (i, 0))),
        out_shape=jax.ShapeDtypeStruct((r2, c), F32),
        compiler_params=_cp(2 * 10 * tr * c + 8 * MIB),
    )(place, grad, got)
    return to_send, to_keep


_HBM_SPEC = pl.BlockSpec(memory_space=pltpu.HBM)
_SEM_SPEC = pl.BlockSpec(memory_space=pltpu.SEMAPHORE)
_DATAFLOW = pltpu.SideEffectType.DATAFLOW_SIDE_EFFECTING


def _split_copy_start(srcs, land_shapes, plan, n_sems, name, after=()):
    ns, nl, na = len(srcs), len(land_shapes), len(after)

    def body(*refs):
        src_refs, land_refs = refs[:ns], refs[ns:ns + nl]
        send, recv = refs[ns + nl + na], refs[ns + nl + na + 1]
        token = refs[-1]
        sends, _ = plan(src_refs, land_refs)
        for src, dst, to, si, ri in sends:
            pltpu.make_async_remote_copy(src_ref=src, dst_ref=dst, send_sem=send.at[si], recv_sem=recv.at[ri],
                                         device_id=to, device_id_type=MESH).start()
        token[...] = jnp.zeros_like(token)

    lands = [lax.empty(shp, dt) for shp, dt in land_shapes]
    through = [pltpu.HBM(a.shape, a.dtype) for a in srcs] + [pltpu.HBM(shp, dt) for shp, dt in land_shapes]
    out = pl.pallas_call(
        body, name=name,
        out_shape=(pltpu.SemaphoreType.DMA((n_sems,)), pltpu.SemaphoreType.DMA((n_sems,)), *through,
                   jax.ShapeDtypeStruct((SUBLANES, SLAB), F32)),
        in_specs=[_HBM_SPEC] * (ns + nl) + [pl.BlockSpec(memory_space=pl.ANY)] * na,
        out_specs=(_SEM_SPEC, _SEM_SPEC, *([_HBM_SPEC] * (ns + nl)), pl.BlockSpec(memory_space=pltpu.VMEM)),
        input_output_aliases={i: 2 + i for i in range(ns + nl)},
        compiler_params=pltpu.CompilerParams(has_side_effects=_DATAFLOW),
    )(*[pltpu.with_memory_space_constraint(a, pltpu.HBM) for a in (*srcs, *lands)], *after)
    return out[0], out[1], list(out[2:2 + ns]), list(out[2 + ns:2 + ns + nl]), out[-1]


def _split_copy_wait(send, recv, srcs, lands, plan, after, name):
    ns, nl, na = len(srcs), len(lands), len(after)

    def body(*refs):
        src_refs, land_refs = refs[:ns], refs[ns:ns + nl]
        send_ref, recv_ref = refs[ns + nl], refs[ns + nl + 1]
        sends, arrivals = plan(src_refs, land_refs)
        for src, dst, to, si, ri in sends:
            pltpu.make_async_remote_copy(src_ref=src, dst_ref=dst, send_sem=send_ref.at[si], recv_sem=recv_ref.at[ri],
                                         device_id=to, device_id_type=MESH).wait_send()
        for (src, _, to, si, _), (view, ri) in zip(sends, arrivals):
            pltpu.make_async_remote_copy(src_ref=view, dst_ref=view, send_sem=send_ref.at[si], recv_sem=recv_ref.at[ri],
                                         device_id=to, device_id_type=MESH).wait_recv()

    out = pl.pallas_call(
        body, name=name,
        out_shape=[pltpu.HBM(a.shape, a.dtype) for a in (*srcs, *lands)],
        in_specs=[_HBM_SPEC] * (ns + nl) + [_SEM_SPEC, _SEM_SPEC] + [pl.BlockSpec(memory_space=pl.ANY)] * na,
        out_specs=[_HBM_SPEC] * (ns + nl),
        input_output_aliases={i: i for i in range(ns + nl)},
        compiler_params=pltpu.CompilerParams(has_side_effects=_DATAFLOW),
    )(*srcs, *lands, send, recv, *after)
    return list(out[:ns]), list(out[ns:])


def _chip_exchange_plan(n):
    per = N_CHIPS - 1

    def plan(srcs, lands):
        x, y, c = _my_place()
        sends, arrivals = [], []
        for t in range(n):
            for j, (cx, cy) in enumerate(_other_chips(x, y)):
                sends.append((srcs[t].at[2 * cx + cy], lands[t].at[j], (cx, cy, c), t * per + j, t * per + j))
                arrivals.append((lands[t].at[j], t * per + j))
        return sends, arrivals

    return plan


def _gather_half_plan(n):
    per = N_CHIPS - 1

    def plan(srcs, lands):
        x, y, c = _my_place()
        k = 2 * x + y
        sends, arrivals = [], []
        for t in range(n):
            for j, (cx, cy) in enumerate(_other_chips(x, y)):
                sends.append((srcs[t].at[c], lands[t].at[k, c], (cx, cy, c), t * per + j, t * per + j))
                arrivals.append((lands[t].at[2 * cx + cy, c], t * per + j))
        return sends, arrivals

    return plan


def _gather_complete(own, landed, name):
    def body(own_ref, in_ref, out_ref, send, recv, local):
        x, y, c = _my_place()
        sib = (x, y, 1 - c)
        mine = pltpu.make_async_copy(own_ref, out_ref.at[2 * x + y], local)
        mine.start()
        sent = []
        for j, (cx, cy) in enumerate(_other_chips(x, y)):
            half = (2 * cx + cy, c)
            cp = pltpu.make_async_remote_copy(src_ref=in_ref.at[half], dst_ref=out_ref.at[half], send_sem=send.at[j],
                                              recv_sem=recv.at[j], device_id=sib, device_id_type=MESH)
            cp.start()
            sent.append(cp)
        for j, (cx, cy) in enumerate(_other_chips(x, y)):
            other = out_ref.at[2 * cx + cy, 1 - c]
            pltpu.make_async_remote_copy(src_ref=other, dst_ref=other, send_sem=send.at[j], recv_sem=recv.at[j],
                                         device_id=sib, device_id_type=MESH).wait_recv()
        for cp in sent:
            cp.wait_send()
        mine.wait()

    any_spec = pl.BlockSpec(memory_space=pl.ANY)
    per = N_CHIPS - 1
    return _pallas(
        body, name=name, in_specs=[any_spec, any_spec], out_specs=any_spec,
        out_shape=jax.ShapeDtypeStruct(landed.shape, landed.dtype), input_output_aliases={1: 0},
        scratch_shapes=[pltpu.SemaphoreType.DMA((per,)), pltpu.SemaphoreType.DMA((per,)), pltpu.SemaphoreType.DMA],
        compiler_params=_cp(16 * MIB),
    )(own, landed)


def _gather_direct_plan(n):
    per = 2 * (N_CHIPS - 1)

    def plan(srcs, lands):
        x, y, c = _my_place()
        k = 2 * x + y
        sends, arrivals = [], []
        for t in range(n):
            for j, (cx, cy) in enumerate(_other_chips(x, y)):
                for core in (0, 1):
                    sends.append((srcs[t].at[c], lands[t].at[k, c], (cx, cy, core),
                                  t * per + 2 * j + core, t * per + 2 * j + c))
                    arrivals.append((lands[t].at[2 * cx + cy, core], t * per + 2 * j + core))
        return sends, arrivals

    return plan


def _rs_chip_add(kept, got, name):
    r2, c = kept.shape
    tr = min(TR_ELT, r2)

    def body(s_ref, q_ref, o_ref):
        o_ref[...] = ((s_ref[...] + q_ref[0].astype(F32)) + q_ref[1].astype(F32)) + q_ref[2].astype(F32)

    blk = pl.BlockSpec((tr, c), lambda i: (i, 0))
    return _pallas(
        body, name=name, grid=(r2 // tr,),
        in_specs=[blk, pl.BlockSpec((N_CHIPS - 1, tr, c), lambda i: (0, i, 0))], out_specs=blk,
        out_shape=jax.ShapeDtypeStruct((r2, c), F32),
        compiler_params=_cp(2 * 14 * tr * c + 8 * MIB),
    )(kept, got)


def _rs_pair_share(halves, name):
    n = len(halves)

    def body(*refs):
        ins, outs = refs[:n], refs[n:2 * n]
        send, recv, local = refs[2 * n:]
        x, y, c = _my_place()
        cps, mine = [], []
        for t in range(n):
            lc = pltpu.make_async_copy(ins[t], outs[t].at[c], local.at[t])
            lc.start()
            mine.append(lc)
            cp = pltpu.make_async_remote_copy(src_ref=ins[t], dst_ref=outs[t].at[c], send_sem=send.at[t],
                                              recv_sem=recv.at[t], device_id=(x, y, 1 - c), device_id_type=MESH)
            cp.start()
            cps.append(cp)
        for t in range(n):
            other = outs[t].at[1 - c]
            pltpu.make_async_remote_copy(src_ref=other, dst_ref=other, send_sem=send.at[t], recv_sem=recv.at[t],
                                         device_id=(x, y, 1 - c), device_id_type=MESH).wait_recv()
        for cp in cps:
            cp.wait_send()
        for lc in mine:
            lc.wait()

    any_spec = pl.BlockSpec(memory_space=pl.ANY)
    return _pallas(
        body, name=name, in_specs=[any_spec] * n, out_specs=[any_spec] * n,
        out_shape=[jax.ShapeDtypeStruct((2,) + h.shape, F32) for h in halves],
        scratch_shapes=[pltpu.SemaphoreType.DMA((n,)), pltpu.SemaphoreType.DMA((n,)), pltpu.SemaphoreType.DMA((n,))],
        compiler_params=_cp(16 * MIB),
    )(*halves)


_PACK_TILE = SUBLANES * SLAB


def _pack(arrays):
    rows = []
    for a in arrays:
        flat = a.reshape(-1).astype(F32)
        padded = -(-flat.shape[0] // _PACK_TILE) * _PACK_TILE
        rows.append(jnp.pad(flat, (0, padded - flat.shape[0])).reshape(-1, SLAB))
    n_rows = sum(r.shape[0] for r in rows)
    if n_rows % (2 * SUBLANES):
        rows.append(jnp.zeros((SUBLANES, SLAB), F32))
    return jnp.concatenate(rows, axis=0)


def _unpack(packed, shapes):
    out, row = [], 0
    for shp in shapes:
        size = math.prod(shp)
        nrow = -(-size // _PACK_TILE) * SUBLANES
        out.append(packed[row:row + nrow].reshape(-1)[:size].reshape(shp))
        row += nrow
    return out


def kernel(x, norm_pre_g, w_in, conv_w, conv_b, ssm_a_re, ssm_a_im, ssm_log_dt, ssm_b_re, ssm_b_im, ssm_c_re, ssm_c_im, ssm_d, w_glu, b_glu, w_out, norm_post_g, loss_target, m_norm_pre_g, m_w_in, m_conv_w, m_conv_b, m_ssm_a_re, m_ssm_a_im, m_ssm_log_dt, m_ssm_b_re, m_ssm_b_im, m_ssm_c_re, m_ssm_c_im, m_ssm_d, m_w_glu, m_b_glu, m_w_out, m_norm_post_g, v_norm_pre_g, v_w_in, v_conv_w, v_conv_b, v_ssm_a_re, v_ssm_a_im, v_ssm_log_dt, v_ssm_b_re, v_ssm_b_im, v_ssm_c_re, v_ssm_c_im, v_ssm_d, v_w_glu, v_b_glu, v_w_out, v_norm_post_g):
    xs, tgt = x[0], loss_target[0]
    t, d = xs.shape
    dc = conv_b.shape[0]
    dssm = ssm_d.shape[0]
    g, p = ssm_a_re.shape
    h = SSM_H
    nq = dssm // SLAB
    n_shard = w_in.shape[1]
    steps = min(T_SCAN, t) // SUBLANES
    mx, my, mc = _my_place()
    chip = 2 * mx + my
    place = jnp.stack([mc, chip]).astype(jnp.int32)

    cw_cols = conv_w.shape[1]
    cw_pad = -(-cw_cols // SLAB) * SLAB
    cw_blk = jnp.zeros((SUBLANES, cw_pad), F32).at[:conv_w.shape[0], :cw_cols].set(conv_w)
    cw_all = _allgather_flat(cw_blk, "allgather_conv_w")
    conv_w_full = jnp.concatenate([cw_all[2 * k, :, :cw_cols] for k in range(N_CHIPS)], axis=1)

    halves = lambda a: a.reshape(2, a.shape[0] // 2, a.shape[1])
    win_half = halves(_cast_bf16(w_in, "cast_w_in", cw_all[0, :, :SLAB]))
    win_plan = _gather_half_plan(1)
    win_send, win_recv, win_srcs, win_lands, win_token = _split_copy_start(
        [win_half], [((N_CHIPS,) + win_half.shape, BF16)], win_plan, N_CHIPS - 1, "gather_w_in_start")
    behind_w_in = win_token[0:1, 0:1]

    small_names = ["norm_pre_g", "conv_w", "conv_b", "ssm_a_re", "ssm_a_im", "ssm_log_dt", "ssm_b_re", "ssm_b_im",
                   "ssm_c_re", "ssm_c_im", "ssm_d", "b_glu", "norm_post_g", "loss"]
    zeros_cw = jnp.zeros((conv_w.shape[0], dc), F32)
    one0 = jnp.zeros((1,), F32)
    hp = lambda b: b.transpose(0, 2, 1)
    small_w = dict(norm_pre_g=norm_pre_g, conv_w=zeros_cw, conv_b=conv_b, ssm_a_re=ssm_a_re, ssm_a_im=ssm_a_im,
                   ssm_log_dt=ssm_log_dt, ssm_b_re=hp(ssm_b_re), ssm_b_im=hp(ssm_b_im), ssm_c_re=ssm_c_re, ssm_c_im=ssm_c_im,
                   ssm_d=ssm_d, b_glu=b_glu, norm_post_g=norm_post_g, loss=one0)
    small_m = dict(norm_pre_g=m_norm_pre_g, conv_w=zeros_cw, conv_b=m_conv_b, ssm_a_re=m_ssm_a_re, ssm_a_im=m_ssm_a_im,
                   ssm_log_dt=m_ssm_log_dt, ssm_b_re=hp(m_ssm_b_re), ssm_b_im=hp(m_ssm_b_im), ssm_c_re=m_ssm_c_re,
                   ssm_c_im=m_ssm_c_im, ssm_d=m_ssm_d, b_glu=m_b_glu, norm_post_g=m_norm_post_g, loss=one0)
    small_v = dict(norm_pre_g=v_norm_pre_g, conv_w=zeros_cw, conv_b=v_conv_b, ssm_a_re=v_ssm_a_re, ssm_a_im=v_ssm_a_im,
                   ssm_log_dt=v_ssm_log_dt, ssm_b_re=hp(v_ssm_b_re), ssm_b_im=hp(v_ssm_b_im), ssm_c_re=v_ssm_c_re,
                   ssm_c_im=v_ssm_c_im, ssm_d=v_ssm_d, b_glu=v_b_glu, norm_post_g=v_norm_post_g, loss=one0)
    w_pack, m_pack, v_pack = [_pack([group[nm] for nm in small_names]) + behind_w_in
                              for group in (small_w, small_m, small_v)]
    side = [halves(_cast_bf16(w_out, "cast_w_out", win_token)), halves(_cast_bf16(w_glu, "cast_w_glu", win_token))]

    expand = jnp.tile(jnp.eye(p, dtype=F32), (1, h))
    b_re2, b_im2 = hp(ssm_b_re).reshape(g, h * p), hp(ssm_b_im).reshape(g, h * p)
    log_dt2 = ssm_log_dt.reshape(g, 1) + behind_w_in
    lbr, lbi, bbr2, bbi2, pw3 = _zoh_fwd(ssm_a_re, ssm_a_im, log_dt2, b_re2, b_im2, expand, steps)
    lam = jnp.stack([lbr.reshape(g * p), lbi.reshape(g * p)])
    pw = pw3.reshape(2, g * p)
    to_slab_b = lambda b2: _blockdiag(b2.reshape(nq, GROUPS_PER_SLAB, h, p))
    bq = jnp.concatenate([to_slab_b(bbr2), to_slab_b(bbi2)], axis=2).astype(BF16)
    to_slab_c = lambda c3: _blockdiag(c3.reshape(nq, GROUPS_PER_SLAB, h, p).transpose(0, 1, 3, 2))
    cq = jnp.concatenate([to_slab_c(ssm_c_re), to_slab_c(-ssm_c_im)], axis=1).astype(BF16)

    g_pre2, g_post2 = norm_pre_g.reshape(1, d), norm_post_g.reshape(1, d)
    conv_b2, b_glu2, d_skip2 = conv_b.reshape(1, dc), b_glu.reshape(1, dssm), ssm_d.reshape(1, dssm)
    u_block = 4 * dc // dssm

    hb = _prenorm(xs, g_pre2 + behind_w_in)
    proj_own = _inproj_own(place, hb, win_srcs[0].reshape(d, n_shard))
    win_own, win_landed = _split_copy_wait(win_send, win_recv, win_srcs, win_lands, win_plan,
                                           [proj_own, bq, cq, w_pack, m_pack, v_pack, *side], "gather_w_in_wait")
    win_g = _gather_complete(win_own[0], win_landed[0], "gather_w_in_complete")
    win_b = win_g.reshape(N_CHIPS, d, n_shard)
    side_plan = _gather_direct_plan(len(side))
    side_sems = 2 * (N_CHIPS - 1) * len(side)
    ag_send, ag_recv, ag_srcs, ag_lands, ag_token = _split_copy_start(
        side, [((N_CHIPS,) + a.shape, BF16) for a in side], side_plan, side_sems, "gather_side_weights_start",
        after=[win_g])
    proj = _inproj_rest(place, hb, win_b, proj_own, ag_token)
    y, cin = _ssm_fwd(proj, u_block, bq, cq, lam, pw, d_skip2)
    side_own, side_all = _split_copy_wait(ag_send, ag_recv, ag_srcs, ag_lands, side_plan, [cin],
                                          "gather_side_weights_wait")
    wout_g, wglu_g = [lax.dynamic_update_index_in_dim(all_, own, chip, 0) for own, all_ in zip(side_own, side_all)]
    wout_b = wout_g.reshape(dc + dssm, d)
    wglu_b = wglu_g.reshape(dssm, dssm)
    mix = _gate_fwd(proj, y, conv_w_full, conv_b2, wglu_b, b_glu2, dc)
    loss_blk, dout, dob, dmix, gg_post = _outproj(mix, wout_b, xs, tgt, g_post2)

    def reduce_start(grads, to_send, tags, group):
        got = _rs_pair_exchange(to_send, "rs_pair_exchange_" + group)
        sums = [_rs_pair_add(place, gt, rt, "rs_pair_add_" + tg) for gt, rt, tg in zip(grads, got, tags)]
        plan = _chip_exchange_plan(len(grads))
        started = _split_copy_start([s16 for s16, _ in sums],
                                    [((N_CHIPS - 1,) + s16.shape[1:], BF16) for s16, _ in sums], plan,
                                    (N_CHIPS - 1) * len(grads), "rs_chip_exchange_" + group + "_start")
        return plan, started, [s32 for _, s32 in sums]

    def reduce_finish(plan, started, sums32, tags, group, after):
        send, recv, srcs, lands, _ = started
        _, landed = _split_copy_wait(send, recv, srcs, lands, plan, after, "rs_chip_exchange_" + group + "_wait")
        mine = [_rs_chip_add(s32, qt, "rs_chip_add_" + tg) for s32, qt, tg in zip(sums32, landed, tags)]
        full = _rs_pair_share(mine, "rs_pair_share_" + group)
        return [f.reshape(2 * f.shape[1], f.shape[2]) for f in full]

    gw_out, gw_out16 = _matmul_tn(mix, dob, 1, "grad_w_out")
    dproj, dy, gsmall, gw_glu = _gate_bwd(proj, y, dmix, conv_w_full, conv_b2, wglu_b, b_glu2, dc)
    as_out = lambda a: a.reshape(N_CHIPS, 2, (dc + dssm) // (2 * N_CHIPS), d)
    gw_glu4 = gw_glu.reshape(N_CHIPS, 2, dssm // (2 * N_CHIPS), dssm)
    rs_a = reduce_start([as_out(gw_out), gw_glu4], [as_out(gw_out16), gw_glu4], ["w_out", "w_glu"], "a")
    dproj, gb_dense, gc_dense, glam, gd = _ssm_bwd(proj, u_block, dy, cin, dproj, bq, cq, lam, pw,
                                                   d_skip2 + rs_a[1][4][0:1, 0:1])
    gw_in, gw_in16 = _matmul_tn(hb, dproj, N_CHIPS, "grad_w_in")
    as_in = lambda a: a.reshape(N_CHIPS, 2, d // 2, n_shard)
    rs_b = reduce_start([as_in(gw_in)], [as_in(gw_in16)], ["w_in"], "b")
    gx, gg_pre = _dh_prenorm_bwd(dproj, win_b, xs, dout, g_pre2 + rs_b[1][4][0:1, 0:1])

    gb_blocks = _blockdiag_take(gb_dense, h, p)
    g_bbr2 = gb_blocks[:, :, :, 0, :].reshape(g, h * p)
    g_bbi2 = gb_blocks[:, :, :, 1, :].reshape(g, h * p)
    gc_blocks = _blockdiag_take(gc_dense, h, p)
    g_c_re = gc_blocks[:, :, :, 0, :].reshape(g, h, p)
    g_c_im = -gc_blocks[:, :, :, 1, :].reshape(g, h, p)
    g_a_re, g_a_im, g_ld, g_b_re2, g_b_im2 = _zoh_bwd(
        ssm_a_re, ssm_a_im, log_dt2, b_re2, b_im2, expand,
        glam[0].reshape(g, p), glam[1].reshape(g, p), g_bbr2, g_bbi2)

    small_g = {
        "norm_pre_g": gg_pre[0], "conv_w": gsmall[0:3], "conv_b": gsmall[3], "ssm_a_re": g_a_re, "ssm_a_im": g_a_im,
        "ssm_log_dt": g_ld.reshape(g), "ssm_b_re": g_b_re2.reshape(g, h, p), "ssm_b_im": g_b_im2.reshape(g, h, p),
        "ssm_c_re": g_c_re, "ssm_c_im": g_c_im, "ssm_d": gd[0], "b_glu": gsmall[4], "norm_post_g": gg_post[0],
        "loss": loss_blk[0, 0:1],
    }
    shapes = [small_w[nm].shape for nm in small_names]
    g_pack = _allreduce_small(_pack([small_g[nm] for nm in small_names]))
    packs = _adamw_small(g_pack, w_pack, m_pack, v_pack)
    sg, sd, sm, sv = [dict(zip(small_names, _unpack(pk, shapes))) for pk in (g_pack, *packs)]
    for group in (sg, sd, sm, sv):
        group.update(ssm_b_re=hp(group["ssm_b_re"]), ssm_b_im=hp(group["ssm_b_im"]))
    loss = sg["loss"][0]

    g_cw = lax.dynamic_slice_in_dim(sg["conv_w"], chip * cw_cols, cw_cols, axis=1)
    pad_cw = lambda a: jnp.zeros((SUBLANES, cw_pad), F32).at[:a.shape[0], :cw_cols].set(a)
    cut_cw = lambda a: a[:conv_w.shape[0], :cw_cols]
    _, d_cw, m_cw, v_cw = [cut_cw(a) for a in _adamw(pad_cw(conv_w), pad_cw(g_cw), pad_cw(m_conv_w), pad_cw(v_conv_w),
                                                     "adamw_conv_w")]

    g_wout, g_wglu = reduce_finish(*rs_a, ["w_out", "w_glu"], "a", [d_cw, packs[0]])
    (g_win,) = reduce_finish(*rs_b, ["w_in"], "b", [g_wout])
    g_win, d_win, m_win, v_win = _adamw(w_in, g_win, m_w_in, v_w_in, "adamw_w_in")
    g_wout, d_wout, m_wout, v_wout = _adamw(w_out, g_wout, m_w_out, v_w_out, "adamw_w_out")
    g_wglu, d_wglu, m_wglu, v_wglu = _adamw(w_glu, g_wglu, m_w_glu, v_w_glu, "adamw_w_glu")

    order = ["norm_pre_g", "w_in", "conv_w", "conv_b", "ssm_a_re", "ssm_a_im", "ssm_log_dt", "ssm_b_re", "ssm_b_im",
             "ssm_c_re", "ssm_c_im", "ssm_d", "w_glu", "b_glu", "w_out", "norm_post_g"]
    grads, deltas, new_m, new_v = dict(sg), dict(sd), dict(sm), dict(sv)
    grads.update(w_in=g_win, w_out=g_wout, w_glu=g_wglu, conv_w=g_cw)
    deltas.update(w_in=d_win, w_out=d_wout, w_glu=d_wglu, conv_w=d_cw)
    new_m.update(w_in=m_win, w_out=m_wout, w_glu=m_wglu, conv_w=m_cw)
    new_v.update(w_in=v_win, w_out=v_wout, w_glu=v_wglu, conv_w=v_cw)
    return (loss, gx[None], *[grads[nm] for nm in order], *[deltas[nm] for nm in order],
            *[new_m[nm] for nm in order], *[new_v[nm] for nm in order])
```

```python
import math

import jax
import jax.numpy as jnp
from jax import lax
from jax.experimental import pallas as pl
from jax.experimental.pallas import tpu as pltpu

F32 = jnp.float32
BF16 = jnp.bfloat16
MESH = pl.DeviceIdType.MESH

EPS = 1e-6
SSM_H = 16
SSM_P = 64
GROUPS_PER_SLAB = 8
SLAB = 128
SLAB_STATES = GROUPS_PER_SLAB * SSM_P
N_CHIPS = 4
N_DEV = 8

ADAM_LR = 0.001
ADAM_B1 = 0.9
ADAM_B2 = 0.999
ADAM_EPS = 1e-08
ADAM_WD = 0.01
ADAM_STEP = 10

MIB = 1024 * 1024
VMEM_CAP = 48 * MIB
SUBLANES = 8

TM_NORM = 512
TM_PROJ = 512
TM_GATE = 256
TM_OUT = 256
TM_DH = 256
T_SCAN = 256
TK_TN = 1024
TM_TN = 1024
TR_ELT = 256


def _cp(vmem_bytes, **kw):
    return pltpu.CompilerParams(vmem_limit_bytes=int(min(VMEM_CAP, max(16 * MIB, vmem_bytes))), **kw)


def _pallas(body, **kw):
    if "grid" not in kw and "grid_spec" not in kw:
        return pl.pallas_call(body, **kw)
    pin = lambda s: pltpu.HBM(s.shape, s.dtype) if isinstance(s, jax.ShapeDtypeStruct) else s
    out_shape = kw.pop("out_shape")
    out_shape = [pin(s) for s in out_shape] if isinstance(out_shape, (list, tuple)) else pin(out_shape)
    call = pl.pallas_call(body, out_shape=out_shape, **kw)

    def run(*args):
        return call(*[pltpu.with_memory_space_constraint(a, pltpu.HBM) if jnp.issubdtype(a.dtype, jnp.floating) else a
                      for a in args])

    return run


def _my_place():
    return lax.axis_index("x"), lax.axis_index("y"), lax.axis_index("c")


def _other_chips(x, y):
    return [(1 - x, y), (x, 1 - y), (1 - x, 1 - y)]


def _silu(z):
    s = jax.nn.sigmoid(z)
    return z * s, s


def _dsilu(z, s):
    return s * (1.0 + z * (1.0 - s))


_GELU_K = math.sqrt(2.0 / math.pi)
_GELU_C = 0.044715


def _gelu(y):
    th = jnp.tanh(_GELU_K * (y + _GELU_C * y * y * y))
    return 0.5 * y * (1.0 + th), th


def _dgelu(y, th):
    return 0.5 * (1.0 + th) + 0.5 * y * (1.0 - th * th) * _GELU_K * (1.0 + 3.0 * _GELU_C * y * y)


def _cast_bf16(w, name, after=None):
    r, c = w.shape
    tr = min(TR_ELT, r)
    extra = [] if after is None else [after]

    def body(w_ref, *rest):
        rest[-1][...] = w_ref[...].astype(BF16)

    return _pallas(
        body, name=name, grid=(r // tr,),
        in_specs=[pl.BlockSpec((tr, c), lambda i: (i, 0))] + [pl.BlockSpec((SUBLANES, SLAB), lambda i: (0, 0))] * len(extra),
        out_specs=pl.BlockSpec((tr, c), lambda i: (i, 0)),
        out_shape=jax.ShapeDtypeStruct((r, c), BF16),
        compiler_params=_cp(12 * tr * c),
    )(w, *extra)


def _prenorm(x, g):
    t, d = x.shape
    tm = min(TM_NORM, t)

    def body(x_ref, g_ref, h_ref):
        xv = x_ref[...]
        r = lax.rsqrt(jnp.mean(xv * xv, axis=-1, keepdims=True) + EPS)
        h_ref[...] = (xv * r * g_ref[...]).astype(BF16)

    return _pallas(
        body, name="prenorm", grid=(t // tm,),
        in_specs=[pl.BlockSpec((tm, d), lambda i: (i, 0)), pl.BlockSpec((1, d), lambda i: (0, 0))],
        out_specs=pl.BlockSpec((tm, d), lambda i: (i, 0)),
        out_shape=jax.ShapeDtypeStruct((t, d), BF16),
        compiler_params=_cp(20 * tm * d),
    )(x, g)


def _inproj_own(place, a, w_own):
    t, k = a.shape
    n = w_own.shape[1]
    tm = min(TM_PROJ, t)

    def body(place_ref, a_ref, b_ref, o_ref):
        del place_ref
        o_ref[...] = jnp.dot(a_ref[...], b_ref[...], preferred_element_type=F32)

    return _pallas(
        body, name="inproj_own",
        grid_spec=pltpu.PrefetchScalarGridSpec(
            num_scalar_prefetch=1, grid=(t // tm,),
            in_specs=[pl.BlockSpec((tm, k), lambda i, p: (i, 0)), pl.BlockSpec((k, n), lambda i, p: (0, 0))],
            out_specs=pl.BlockSpec((tm, n), lambda i, p: (i, p[1]))),
        out_shape=jax.ShapeDtypeStruct((t, N_CHIPS * n), F32),
        compiler_params=_cp(2 * (2 * tm * k + 2 * k * n + 4 * tm * n) + 4 * MIB),
    )(place, a, w_own)


def _inproj_rest(place, a, b, partial, after):
    t, k = a.shape
    nb, _, n = b.shape
    tm = min(TM_PROJ, t)
    shard = lambda s, p: (p[1] + 1 + s) % nb

    def body(place_ref, a_ref, b_ref, after_ref, partial_ref, o_ref):
        del place_ref, after_ref, partial_ref
        o_ref[...] = jnp.dot(a_ref[...], b_ref[...], preferred_element_type=F32)

    return _pallas(
        body, name="inproj_rest",
        grid_spec=pltpu.PrefetchScalarGridSpec(
            num_scalar_prefetch=1, grid=(nb - 1, t // tm),
            in_specs=[pl.BlockSpec((tm, k), lambda s, i, p: (i, 0)),
                      pl.BlockSpec((None, k, n), lambda s, i, p: (shard(s, p), 0, 0)),
                      pl.BlockSpec((SUBLANES, SLAB), lambda s, i, p: (0, 0)),
                      pl.BlockSpec(memory_space=pl.ANY)],
            out_specs=pl.BlockSpec((tm, n), lambda s, i, p: (i, shard(s, p)))),
        out_shape=jax.ShapeDtypeStruct((t, nb * n), F32),
        input_output_aliases={4: 0},
        compiler_params=_cp(2 * (2 * tm * k + 2 * k * n + 4 * tm * n) + 4 * MIB),
    )(place, a, b, after, partial)


def _matmul_tn(a, b, nb, name):
    t, m = a.shape
    n = b.shape[1] // nb
    tk = min(TK_TN, t)
    tma = min(TM_TN, m)

    def body(a_ref, b_ref, o_ref, o16_ref):
        k = pl.program_id(2)

        @pl.when(k == 0)
        def _():
            o_ref[...] = jnp.zeros_like(o_ref)

        o_ref[...] += lax.dot_general(a_ref[...], b_ref[...], (((0,), (0,)), ((), ())), preferred_element_type=F32)

        @pl.when(k == t // tk - 1)
        def _():
            o16_ref[...] = o_ref[...].astype(BF16)

    blk = pl.BlockSpec((None, tma, n), lambda j, i, k: (j, i, 0))
    return _pallas(
        body, name=name, grid=(nb, m // tma, t // tk),
        in_specs=[pl.BlockSpec((tk, tma), lambda j, i, k: (k, i)), pl.BlockSpec((tk, n), lambda j, i, k: (k, j))],
        out_specs=[blk, blk],
        out_shape=[jax.ShapeDtypeStruct((nb, m, n), F32), jax.ShapeDtypeStruct((nb, m, n), BF16)],
        compiler_params=_cp(2 * (2 * tk * tma + 2 * tk * n + 6 * tma * n) + 8 * MIB),
    )(a, b)


def _outproj(mix, w_out, x, tgt, g_post):
    t, dm = mix.shape
    d = w_out.shape[1]
    tm = min(TM_OUT, t)

    rb = 2 * SUBLANES
    assert tm % rb == 0

    def body(mix_ref, w_ref, x_ref, t_ref, g_ref, loss_ref, dout_ref, do_ref, dmix_ref, gg_ref, o_buf):
        @pl.when(pl.program_id(0) == 0)
        def _():
            loss_ref[...] = jnp.zeros_like(loss_ref)
            gg_ref[...] = jnp.zeros_like(gg_ref)

        w = w_ref[...]
        o_buf[...] = jnp.dot(mix_ref[...], w, preferred_element_type=F32)
        g = g_ref[...]
        fold = lambda v: v[0:SUBLANES] + v[SUBLANES:rb]
        sq_err = jnp.zeros((SUBLANES, d), F32)
        gg = jnp.zeros((SUBLANES, d), F32)
        for c in range(tm // rb):
            rs = pl.ds(c * rb, rb)
            o = o_buf[rs, :]
            r = lax.rsqrt(jnp.mean(o * o, axis=-1, keepdims=True) + EPS)
            nh = o * r
            e = x_ref[rs, :] + nh * g - t_ref[rs, :]
            sq_err = sq_err + fold(e * e)
            dout = e * (1.0 / d)
            dout_ref[rs, :] = dout
            gg = gg + fold(dout * nh)
            dn = dout * g
            do = r * (dn - nh * jnp.mean(dn * nh, axis=-1, keepdims=True))
            do_ref[rs, :] = do.astype(BF16)
        loss_ref[...] += jnp.sum(sq_err) * (0.5 / d)
        gg_ref[0:1, :] += jnp.sum(gg, axis=0, keepdims=True)
        dmix_ref[...] = lax.dot_general(do_ref[...], w, (((1,), (1,)), ((), ())), preferred_element_type=F32)

    row = lambda i: (i, 0)
    fixed = lambda i: (0, 0)
    return _pallas(
        body, name="outproj", grid=(t // tm,),
        in_specs=[pl.BlockSpec((tm, dm), row), pl.BlockSpec((dm, d), fixed), pl.BlockSpec((tm, d), row),
                  pl.BlockSpec((tm, d), row), pl.BlockSpec((1, d), fixed)],
        out_specs=[pl.BlockSpec((SUBLANES, SLAB), fixed), pl.BlockSpec((tm, d), row), pl.BlockSpec((tm, d), row),
                   pl.BlockSpec((tm, dm), row), pl.BlockSpec((SUBLANES, d), fixed)],
        out_shape=[jax.ShapeDtypeStruct((SUBLANES, SLAB), F32), jax.ShapeDtypeStruct((t, d), F32),
                   jax.ShapeDtypeStruct((t, d), BF16), jax.ShapeDtypeStruct((t, dm), F32),
                   jax.ShapeDtypeStruct((SUBLANES, d), F32)],
        scratch_shapes=[pltpu.VMEM((tm, d), F32)],
        compiler_params=_cp(2 * (2 * dm * d + tm * (2 * dm + 4 * d * 3 + 2 * d + 4 * dm)) + 4 * tm * d + 12 * MIB),
    )(mix, w_out, x, tgt, g_post)


def _dh_prenorm_bwd(dproj, w_in, x, dout, g_pre):
    t, d = x.shape
    nb, _, n = w_in.shape
    tm = min(TM_DH, t)

    def body(dp_ref, w_ref, x_ref, dout_ref, g_ref, dx_ref, gg_ref):
        @pl.when(pl.program_id(0) == 0)
        def _():
            gg_ref[...] = jnp.zeros_like(gg_ref)

        dh = None
        for k in range(nb):
            part = lax.dot_general(dp_ref[:, k * n:(k + 1) * n], w_ref[k], (((1,), (1,)), ((), ())),
                                   preferred_element_type=F32)
            dh = part if dh is None else dh + part
        xv = x_ref[...]
        r = lax.rsqrt(jnp.mean(xv * xv, axis=-1, keepdims=True) + EPS)
        xh = xv * r
        gg_ref[0:1, :] += jnp.sum(dh * xh, axis=0, keepdims=True)
        dg = dh * g_ref[...]
        dx_ref[...] = dout_ref[...] + r * (dg - xh * jnp.mean(dg * xh, axis=-1, keepdims=True))

    row = lambda i: (i, 0)
    fixed = lambda i: (0, 0)
    w_spec = pl.BlockSpec(w_in.shape, lambda i: (0, 0, 0), pipeline_mode=pl.Buffered(1))
    return _pallas(
        body, name="dh_prenorm_bwd", grid=(t // tm,),
        in_specs=[pl.BlockSpec((tm, nb * n), row), w_spec,
                  pl.BlockSpec((tm, d), row), pl.BlockSpec((tm, d), row), pl.BlockSpec((1, d), fixed)],
        out_specs=[pl.BlockSpec((tm, d), row), pl.BlockSpec((SUBLANES, d), fixed)],
        out_shape=[jax.ShapeDtypeStruct((t, d), F32), jax.ShapeDtypeStruct((SUBLANES, d), F32)],
        compiler_params=_cp(2 * nb * d * n + 2 * (2 * tm * nb * n + 12 * tm * d) + 16 * tm * d + 4 * MIB),
    )(dproj, w_in, x, dout, g_pre)


def _adamw(w, g, m, v, name):
    r, c = w.shape
    tr = min(TR_ELT, r)
    c1 = 1.0 - ADAM_B1 ** ADAM_STEP
    c2 = 1.0 - ADAM_B2 ** ADAM_STEP

    def body(w_ref, g_ref, m_ref, v_ref, go_ref, d_ref, mo_ref, vo_ref):
        gv = g_ref[...]
        go_ref[...] = gv
        mn = ADAM_B1 * m_ref[...] + (1.0 - ADAM_B1) * gv
        vn = ADAM_B2 * v_ref[...] + (1.0 - ADAM_B2) * (gv * gv)
        d_ref[...] = -ADAM_LR * ((mn / c1) / (jnp.sqrt(vn / c2) + ADAM_EPS) + ADAM_WD * w_ref[...])
        mo_ref[...] = mn
        vo_ref[...] = vn

    spec = pl.BlockSpec((tr, c), lambda i: (i, 0))
    sds = jax.ShapeDtypeStruct((r, c), F32)
    return _pallas(
        body, name=name, grid=(r // tr,), in_specs=[spec] * 4, out_specs=[spec] * 4, out_shape=[sds] * 4,
        compiler_params=_cp(2 * 8 * 4 * tr * c + 8 * MIB),
    )(w, g, m, v)


def _adamw_small(g, w, m, v):
    r, c = w.shape
    c1 = 1.0 - ADAM_B1 ** ADAM_STEP
    c2 = 1.0 - ADAM_B2 ** ADAM_STEP

    def body(g_ref, w_ref, m_ref, v_ref, d_ref, mo_ref, vo_ref):
        gv = g_ref[...]
        mn = ADAM_B1 * m_ref[...] + (1.0 - ADAM_B1) * gv
        vn = ADAM_B2 * v_ref[...] + (1.0 - ADAM_B2) * (gv * gv)
        d_ref[...] = -ADAM_LR * ((mn / c1) / (jnp.sqrt(vn / c2) + ADAM_EPS) + ADAM_WD * w_ref[...])
        mo_ref[...] = mn
        vo_ref[...] = vn

    sds = jax.ShapeDtypeStruct((r, c), F32)
    return _pallas(
        body, name="adamw_small", out_shape=[sds] * 3,
        compiler_params=_cp(12 * 4 * r * c + 8 * MIB),
    )(g, w, m, v)


def _allreduce_small(v):
    r, c = v.shape
    assert r % (2 * SUBLANES) == 0
    h = r // 2

    def body(v_ref, out_ref, got_ref, chip_ref, slots_ref, send, recv):
        x, y, cc = _my_place()
        k = 2 * x + y
        sib = (x, y, 1 - cc)

        def rc(s, src, dst, to):
            return pltpu.make_async_remote_copy(src_ref=src, dst_ref=dst, send_sem=send.at[s], recv_sem=recv.at[s],
                                                device_id=to, device_id_type=MESH)

        pair = rc(0, v_ref, got_ref, sib)
        pair.start()
        pair.wait()
        chip_ref[...] = v_ref[...] + got_ref[...]
        mine = pl.ds(pl.multiple_of(cc * h, SUBLANES), h)
        theirs = pl.ds(pl.multiple_of((1 - cc) * h, SUBLANES), h)
        chips = _other_chips(x, y)
        sent = []
        for j, (cx, cy) in enumerate(chips):
            cp = rc(1 + j, chip_ref.at[mine], slots_ref.at[k], (cx, cy, cc))
            cp.start()
            sent.append(cp)
        slots_ref[k] = chip_ref[mine, :]
        for j, (cx, cy) in enumerate(chips):
            rc(1 + j, chip_ref.at[mine], slots_ref.at[2 * cx + cy], (cx, cy, cc)).wait_recv()
        total = slots_ref[0]
        for kk in range(1, N_CHIPS):
            total = total + slots_ref[kk]
        out_ref[mine, :] = total
        for cp in sent:
            cp.wait_send()
        share = rc(N_CHIPS, out_ref.at[mine], out_ref.at[mine], sib)
        share.start()
        rc(N_CHIPS, out_ref.at[theirs], out_ref.at[theirs], sib).wait_recv()
        share.wait_send()

    vm = pl.BlockSpec(memory_space=pltpu.VMEM)
    return _pallas(
        body, name="allreduce_small", out_shape=jax.ShapeDtypeStruct((r, c), F32), in_specs=[vm], out_specs=vm,
        scratch_shapes=[pltpu.VMEM((r, c), F32), pltpu.VMEM((r, c), F32), pltpu.VMEM((N_CHIPS, h, c), F32),
                        pltpu.SemaphoreType.DMA((N_CHIPS + 1,)), pltpu.SemaphoreType.DMA((N_CHIPS + 1,))],
        compiler_params=_cp(6 * 4 * r * c + 8 * MIB),
    )(v)


def _zoh(a_re, a_im, log_dt, b_re2, b_im2, expand):
    dt = jnp.exp(log_dt)
    mag = jnp.exp(a_re * dt)
    lbr, lbi = mag * jnp.cos(a_im * dt), mag * jnp.sin(a_im * dt)
    nr, ni = lbr - 1.0, lbi
    den = a_re * a_re + a_im * a_im
    qr = (nr * a_re + ni * a_im) / den
    qi = (ni * a_re - nr * a_im) / den
    qr2 = jnp.dot(qr, expand, precision=lax.Precision.HIGHEST, preferred_element_type=F32)
    qi2 = jnp.dot(qi, expand, precision=lax.Precision.HIGHEST, preferred_element_type=F32)
    return lbr, lbi, qr2 * b_re2 - qi2 * b_im2, qr2 * b_im2 + qi2 * b_re2


def _zoh_fwd(a_re, a_im, log_dt, b_re2, b_im2, expand, power):
    g, p = a_re.shape

    def body(ar_ref, ai_ref, ld_ref, br_ref, bi_ref, e_ref, lbr_ref, lbi_ref, bbr_ref, bbi_ref, pw_ref):
        ar, ai, ld = ar_ref[...], ai_ref[...], ld_ref[...]
        lbr, lbi, bbr, bbi = _zoh(ar, ai, ld, br_ref[...], bi_ref[...], e_ref[...])
        lbr_ref[...], lbi_ref[...], bbr_ref[...], bbi_ref[...] = lbr, lbi, bbr, bbi
        dt = jnp.exp(ld) * float(power)
        mag = jnp.exp(ar * dt)
        pw_ref[0] = mag * jnp.cos(ai * dt)
        pw_ref[1] = mag * jnp.sin(ai * dt)

    gp = jax.ShapeDtypeStruct((g, p), F32)
    gph = jax.ShapeDtypeStruct(b_re2.shape, F32)
    return _pallas(
        body, name="zoh_fwd", out_shape=[gp, gp, gph, gph, jax.ShapeDtypeStruct((2, g, p), F32)],
        compiler_params=_cp(16 * MIB),
    )(a_re, a_im, log_dt, b_re2, b_im2, expand)


def _zoh_bwd(a_re, a_im, log_dt, b_re2, b_im2, expand, g_lbr, g_lbi, g_bbr, g_bbi):
    def body(ar_ref, ai_ref, ld_ref, br_ref, bi_ref, e_ref, c0, c1, c2, c3, gar, gai, gld, gbr, gbi):
        e = e_ref[...]
        _, vjp = jax.vjp(lambda a, b, c, d, f: _zoh(a, b, c, d, f, e),
                         ar_ref[...], ai_ref[...], ld_ref[...], br_ref[...], bi_ref[...])
        gar[...], gai[...], gld[...], gbr[...], gbi[...] = vjp((c0[...], c1[...], c2[...], c3[...]))

    sds = lambda a: jax.ShapeDtypeStruct(a.shape, F32)
    return _pallas(
        body, name="zoh_bwd", out_shape=[sds(a_re), sds(a_im), sds(log_dt), sds(b_re2), sds(b_im2)],
        compiler_params=_cp(16 * MIB),
    )(a_re, a_im, log_dt, b_re2, b_im2, expand, g_lbr, g_lbi, g_bbr, g_bbi)


def _blockdiag(blocks):
    n, c = blocks.shape[1], blocks.shape[3]
    rows = [jnp.pad(blocks[:, i], ((0, 0), (0, 0), (i * c, (n - 1 - i) * c))) for i in range(n)]
    return jnp.concatenate(rows, axis=1)


def _blockdiag_take(dense, r, c):
    half = GROUPS_PER_SLAB * c
    blocks = [jnp.stack([dense[:, i * r:(i + 1) * r, part * half + i * c:part * half + (i + 1) * c]
                         for part in (0, 1)], axis=2) for i in range(GROUPS_PER_SLAB)]
    return jnp.stack(blocks, axis=1)


def _scan_slab(s_ref, row0, q, lam_ref, pw_ref, car_ref, steps, reverse, prev_ref=None, prev_row0=0, glam_ref=None):
    sign = -1.0 if reverse else 1.0
    half = SLAB_STATES // SLAB
    cols = [(q * 2 * half + m, q * 2 * half + half + m, q * SLAB_STATES + m * SLAB) for m in range(half)]
    nm = len(cols)
    full = (SUBLANES, SLAB)
    lam = [(jnp.broadcast_to(lam_ref[0:1, pl.ds(cl, SLAB)], full),
            jnp.broadcast_to(sign * lam_ref[1:2, pl.ds(cl, SLAB)], full)) for (_, _, cl) in cols]

    def step_rows(jj, base):
        j = (steps - 1 - jj) if reverse else jj
        return j, pl.ds(base + j * SUBLANES, SUBLANES)

    def pass1(jj, car):
        _, rows = step_rows(jj, row0)
        out = []
        for m, (cr, ci, _) in enumerate(cols):
            sr, si = car[2 * m], car[2 * m + 1]
            lr, li = lam[m]
            nr = lr * sr - li * si + s_ref[cr, rows, :]
            ni = lr * si + li * sr + s_ref[ci, rows, :]
            s_ref[cr, rows, :] = nr
            s_ref[ci, rows, :] = ni
            out += [nr, ni]
        return tuple(out)

    def run_steps(step_fn, carry):
        for jj in range(steps):
            carry = step_fn(jj, carry)
        return carry

    ends = run_steps(pass1, tuple(jnp.zeros(full, F32) for _ in range(2 * nm)))

    entry = []
    for m, (cr, ci, cl) in enumerate(cols):
        ljr = pw_ref[0:1, pl.ds(cl, SLAB)]
        lji = sign * pw_ref[1:2, pl.ds(cl, SLAB)]
        c_r = car_ref[cr, 0:1, :]
        c_i = car_ref[ci, 0:1, :]
        rows_r, rows_i = [None] * SUBLANES, [None] * SUBLANES
        order = range(SUBLANES - 1, -1, -1) if reverse else range(SUBLANES)
        for b in order:
            rows_r[b], rows_i[b] = c_r, c_i
            e_r, e_i = ends[2 * m][b:b + 1], ends[2 * m + 1][b:b + 1]
            c_r, c_i = ljr * c_r - lji * c_i + e_r, ljr * c_i + lji * c_r + e_i
        car_ref[cr, 0:1, :] = c_r
        car_ref[ci, 0:1, :] = c_i
        entry.append((jnp.concatenate(rows_r, axis=0), jnp.concatenate(rows_i, axis=0)))

    def pass2(jj, carry):
        j, rows = step_rows(jj, row0)
        decayed, acc = carry[:2 * nm], carry[2 * nm:]
        out_d, out_a = [], []
        for m, (cr, ci, cl) in enumerate(cols):
            lr, li = lam[m]
            dr, di = decayed[2 * m], decayed[2 * m + 1]
            dr, di = lr * dr - li * di, lr * di + li * dr
            nr = s_ref[cr, rows, :] + dr
            ni = s_ref[ci, rows, :] + di
            s_ref[cr, rows, :] = nr
            s_ref[ci, rows, :] = ni
            out_d += [dr, di]
            if prev_ref is not None:
                prow = pl.ds(prev_row0 + (j - 1) * SUBLANES, SUBLANES)
                qr = prev_ref[cr, prow, :]
                qi = prev_ref[ci, prow, :]
                out_a += [acc[2 * m] + (nr * qr + ni * qi), acc[2 * m + 1] + (ni * qr - nr * qi)]
        return tuple(out_d) + tuple(out_a)

    n_acc = 2 * nm if prev_ref is not None else 0
    init = tuple(e for pair in entry for e in pair) + tuple(jnp.zeros(full, F32) for _ in range(n_acc))
    accs = run_steps(pass2, init)[2 * nm:]
    if prev_ref is not None:
        for m, (_, _, cl) in enumerate(cols):
            glam_ref[0:1, pl.ds(cl, SLAB)] += jnp.sum(accs[2 * m], axis=0, keepdims=True)
            glam_ref[1:2, pl.ds(cl, SLAB)] += jnp.sum(accs[2 * m + 1], axis=0, keepdims=True)


def _permute_rows_f32(perm_bf16, v):
    hi = v.astype(BF16)
    lo = (v - hi.astype(F32)).astype(BF16)
    return (jnp.dot(perm_bf16, hi, preferred_element_type=F32) + jnp.dot(perm_bf16, lo, preferred_element_type=F32))


def _put_slab(s_ref, rows, q, val):
    per = 2 * SLAB_STATES // SLAB
    for i in range(per):
        s_ref[q * per + i, rows, :] = val[:, i * SLAB:(i + 1) * SLAB]


def _get_slab(s_ref, rows, q):
    per = 2 * SLAB_STATES // SLAB
    return jnp.concatenate([s_ref[q * per + i, rows, :] for i in range(per)], axis=1)


def _step_major_perm(tt):
    r = jnp.arange(tt)
    held = (r % SUBLANES) * (tt // SUBLANES) + r // SUBLANES
    return held[:, None] == r[None, :]


def _ssm_fwd(proj, u_block, bq, cq, lam, pw, d_skip):
    t = proj.shape[0]
    nq, ds, w2 = bq.shape
    assert ds == SLAB and w2 == 2 * SLAB_STATES
    dssm = nq * SLAB
    width = nq * w2
    ntile = width // SLAB
    tt = min(T_SCAN, t)
    steps = tt // SUBLANES
    perm = _step_major_perm(tt)
    pm, pmt = perm.astype(BF16), perm.T.astype(BF16)

    def body(u_ref, pm_ref, pmt_ref, bq_ref, cq_ref, lam_ref, pw_ref, d_ref, y_ref, cin_ref, s_ref, car_ref):
        @pl.when(pl.program_id(0) == 0)
        def _():
            car_ref[...] = jnp.zeros_like(car_ref)

        cin_ref[...] = jnp.broadcast_to(car_ref[:, 0:1, :], cin_ref.shape)
        u = u_ref[...]
        ub = jnp.dot(pm_ref[...], u.astype(BF16), preferred_element_type=F32).astype(BF16)
        everything = slice(None)
        for q in range(nq):
            _put_slab(s_ref, everything, q,
                      jnp.dot(ub[:, q * SLAB:(q + 1) * SLAB], bq_ref[q], preferred_element_type=F32))
        for q in range(nq):
            _scan_slab(s_ref, 0, q, lam_ref, pw_ref, car_ref, steps, reverse=False)
        y_sm = jnp.concatenate(
            [jnp.dot(_get_slab(s_ref, everything, q).astype(BF16), cq_ref[q], preferred_element_type=F32)
             for q in range(nq)], axis=1)
        y_ref[...] = _permute_rows_f32(pmt_ref[...], y_sm) + d_ref[...] * u

    c3 = lambda i: (0, 0, 0)
    c2 = lambda i: (0, 0)
    return _pallas(
        body, name="ssm_fwd", grid=(t // tt,),
        in_specs=[pl.BlockSpec((tt, dssm), lambda i: (i, u_block)), pl.BlockSpec((tt, tt), c2),
                  pl.BlockSpec((tt, tt), c2), pl.BlockSpec(bq.shape, c3),
                  pl.BlockSpec(cq.shape, c3), pl.BlockSpec(lam.shape, c2), pl.BlockSpec(pw.shape, c2),
                  pl.BlockSpec((1, dssm), c2)],
        out_specs=[pl.BlockSpec((tt, dssm), lambda i: (i, 0)),
                   pl.BlockSpec((None, ntile, SUBLANES, SLAB), lambda i: (i, 0, 0, 0))],
        out_shape=[jax.ShapeDtypeStruct((t, dssm), F32), jax.ShapeDtypeStruct((t // tt, ntile, SUBLANES, SLAB), F32)],
        scratch_shapes=[pltpu.VMEM((ntile, tt, SLAB), F32), pltpu.VMEM((ntile, SUBLANES, SLAB), F32)],
        compiler_params=_cp(2 * (8 * tt * dssm + 4 * nq * ds * w2) + 4 * tt * width + 16 * MIB,
                            dimension_semantics=("arbitrary",)),
    )(proj, pm, pmt, bq, cq, lam, pw, d_skip)


def _ssm_bwd(proj, u_block, dy, cin, dproj, bq, cq, lam, pw, d_skip):
    t = proj.shape[0]
    nq, ds, w2 = bq.shape
    dssm = nq * SLAB
    width = nq * w2
    ntile = width // SLAB
    tt = min(T_SCAN, t)
    nt = t // tt
    steps = tt // SUBLANES
    halo = SUBLANES
    perm = _step_major_perm(tt)
    pm, pmt = perm.astype(BF16), perm.T.astype(BF16)

    def body(u_ref, dy_ref, cin_ref, dp_any, pm_ref, pmt_ref, bq_ref, cq_ref, lam_ref, pw_ref, d_ref,
             du_ref, gb_ref, gc_ref, glam_ref, gd_ref, s_ref, gs_ref, car_f, car_b):
        del dp_any

        @pl.when(pl.program_id(0) == 0)
        def _():
            car_b[...] = jnp.zeros_like(car_b)
            gb_ref[...] = jnp.zeros_like(gb_ref)
            gc_ref[...] = jnp.zeros_like(gc_ref)
            glam_ref[...] = jnp.zeros_like(glam_ref)
            gd_ref[...] = jnp.zeros_like(gd_ref)

        u = u_ref[...]
        dyv = dy_ref[...]
        gd_ref[0:1, :] += jnp.sum(dyv * u, axis=0, keepdims=True)
        pmv = pm_ref[...]
        ub = jnp.dot(pmv, u.astype(BF16), preferred_element_type=F32).astype(BF16)
        dyb = jnp.dot(pmv, dyv.astype(BF16), preferred_element_type=F32).astype(BF16)
        car_f[...] = cin_ref[...]
        data = slice(halo, halo + tt)
        everything = slice(None)
        for q in range(nq):
            _put_slab(s_ref, data, q, jnp.dot(ub[:, q * SLAB:(q + 1) * SLAB], bq_ref[q], preferred_element_type=F32))
        for q in range(nq):
            _scan_slab(s_ref, halo, q, lam_ref, pw_ref, car_f, steps, reverse=False)
        last_step = s_ref[:, halo + tt - SUBLANES:halo + tt, :]
        s_ref[:, 0:halo, :] = jnp.concatenate([cin_ref[:, 0:1, :], last_step[:, 0:SUBLANES - 1, :]], axis=1)
        tn = (((0,), (0,)), ((), ()))
        nt_dims = (((1,), (1,)), ((), ()))
        for q in range(nq):
            sl = slice(q * SLAB, (q + 1) * SLAB)
            gc_ref[q] += lax.dot_general(dyb[:, sl], _get_slab(s_ref, data, q).astype(BF16), tn,
                                         preferred_element_type=F32)
            _put_slab(gs_ref, everything, q,
                      lax.dot_general(dyb[:, sl], cq_ref[q], nt_dims, preferred_element_type=F32))
        for q in range(nq):
            _scan_slab(gs_ref, 0, q, lam_ref, pw_ref, car_b, steps, reverse=True,
                       prev_ref=s_ref, prev_row0=halo, glam_ref=glam_ref)
        du_parts = []
        for q in range(nq):
            sl = slice(q * SLAB, (q + 1) * SLAB)
            gsb = _get_slab(gs_ref, everything, q).astype(BF16)
            du_parts.append(lax.dot_general(gsb, bq_ref[q], nt_dims, preferred_element_type=F32))
            gb_ref[q] += lax.dot_general(ub[:, sl], gsb, tn, preferred_element_type=F32)
        du_sm = jnp.concatenate(du_parts, axis=1)
        du_ref[...] = (_permute_rows_f32(pmt_ref[...], du_sm) + dyv * d_ref[...]).astype(BF16)

    c3 = lambda i: (0, 0, 0)
    c2 = lambda i: (0, 0)
    rev = lambda i: (nt - 1 - i, 0)
    dense = jax.ShapeDtypeStruct((nq, SLAB, w2), F32)
    gp = lam.shape[1]
    return _pallas(
        body, name="ssm_bwd", grid=(nt,),
        in_specs=[pl.BlockSpec((tt, dssm), lambda i: (nt - 1 - i, u_block)), pl.BlockSpec((tt, dssm), rev),
                  pl.BlockSpec((None, ntile, SUBLANES, SLAB), lambda i: (nt - 1 - i, 0, 0, 0)),
                  pl.BlockSpec(memory_space=pl.ANY), pl.BlockSpec((tt, tt), c2), pl.BlockSpec((tt, tt), c2),
                  pl.BlockSpec(bq.shape, c3), pl.BlockSpec(cq.shape, c3),
                  pl.BlockSpec(lam.shape, c2), pl.BlockSpec(pw.shape, c2), pl.BlockSpec((1, dssm), c2)],
        out_specs=[pl.BlockSpec((tt, dssm), lambda i: (nt - 1 - i, u_block)), pl.BlockSpec(dense.shape, c3),
                   pl.BlockSpec(dense.shape, c3), pl.BlockSpec((SUBLANES, gp), c2), pl.BlockSpec((SUBLANES, dssm), c2)],
        out_shape=[jax.ShapeDtypeStruct(dproj.shape, dproj.dtype), dense, dense,
                   jax.ShapeDtypeStruct((SUBLANES, gp), F32), jax.ShapeDtypeStruct((SUBLANES, dssm), F32)],
        scratch_shapes=[pltpu.VMEM((ntile, tt + halo, SLAB), F32), pltpu.VMEM((ntile, tt, SLAB), F32),
                        pltpu.VMEM((ntile, SUBLANES, SLAB), F32), pltpu.VMEM((ntile, SUBLANES, SLAB), F32)],
        input_output_aliases={3: 0},
        compiler_params=_cp(2 * (10 * tt * dssm + 4 * nq * ds * w2 + 8 * nq * SLAB * w2)
                            + 8 * tt * width + 12 * MIB, dimension_semantics=("arbitrary",)),
    )(proj, dy, cin, dproj, pm, pmt, bq, cq, lam, pw, d_skip)


def _gate_fwd(proj, y, conv_w, conv_b, w_glu, b_glu, dc):
    t = proj.shape[0]
    dssm = y.shape[1]
    assert dc == dssm
    tm = min(TM_GATE, t)
    halo = SUBLANES

    rb = 2 * SUBLANES
    assert tm % rb == 0

    def body(b_ref, c_ref, v_ref, zc_ref, zs_ref, y_ref, cw_ref, cb_ref, wg_ref, bg_ref, mix_ref,
             cv_buf, sh1_buf, sh2_buf, ge_buf, geb_buf, gl_buf):
        @pl.when(pl.program_id(0) == 0)
        def _():
            cv_buf[0:halo, :] = jnp.zeros((halo, dc), F32)

        cv_buf[halo:, :] = c_ref[...] * v_ref[...]
        sh1_buf[...] = cv_buf[halo - 1:halo - 1 + tm, :]
        sh2_buf[...] = cv_buf[halo - 2:halo - 2 + tm, :]
        w0, w1, w2, cb = cw_ref[0:1, :], cw_ref[1:2, :], cw_ref[2:3, :], cb_ref[...]
        chunks = [pl.ds(r * rb, rb) for r in range(tm // rb)]
        for r, rs in enumerate(chunks):
            ge, _ = _gelu(y_ref[rs, :])
            ge_buf[rs, :] = ge
            geb_buf[rs, :] = ge.astype(BF16)
        gl_buf[...] = jnp.dot(geb_buf[...], wg_ref[...], preferred_element_type=F32) + bg_ref[...]
        for r, rs in enumerate(chunks):
            conv = cb + w2 * cv_buf[pl.ds(halo + r * rb, rb), :] + w1 * sh1_buf[rs, :] + w0 * sh2_buf[rs, :]
            sz, _ = _silu(zc_ref[rs, :])
            mix_ref[rs, 0:dc] = (b_ref[rs, :] * conv * sz).astype(BF16)
        cv_buf[0:halo, :] = cv_buf[tm:tm + halo, :]
        for r, rs in enumerate(chunks):
            szs, _ = _silu(zs_ref[rs, :])
            mix_ref[rs, dc:] = (ge_buf[rs, :] * jax.nn.sigmoid(gl_buf[rs, :]) * szs).astype(BF16)

    col = lambda j: pl.BlockSpec((tm, dc), lambda i, j=j: (i, j))
    fixed = lambda i: (0, 0)
    return _pallas(
        body, name="gate_fwd", grid=(t // tm,),
        in_specs=[col(0), col(1), col(2), col(3), col(5), pl.BlockSpec((tm, dssm), lambda i: (i, 0)),
                  pl.BlockSpec(conv_w.shape, fixed), pl.BlockSpec((1, dc), fixed),
                  pl.BlockSpec(w_glu.shape, fixed), pl.BlockSpec((1, dssm), fixed)],
        out_specs=pl.BlockSpec((tm, dc + dssm), lambda i: (i, 0)),
        out_shape=jax.ShapeDtypeStruct((t, dc + dssm), BF16),
        scratch_shapes=[pltpu.VMEM((tm + halo, dc), F32), pltpu.VMEM((tm, dc), F32), pltpu.VMEM((tm, dc), F32),
                        pltpu.VMEM((tm, dssm), F32), pltpu.VMEM((tm, dssm), BF16), pltpu.VMEM((tm, dssm), F32)],
        compiler_params=_cp(2 * (6 * 4 * tm * dc + 2 * tm * (dc + dssm) + 2 * dssm * dssm) + 24 * tm * dc + 8 * MIB,
                            dimension_semantics=("arbitrary",)),
    )(proj, proj, proj, proj, proj, y, conv_w, conv_b, w_glu, b_glu)


def _gate_bwd(proj, y, dmix, conv_w, conv_b, w_glu, b_glu, dc):
    t = proj.shape[0]
    dssm = y.shape[1]
    tm = min(TM_GATE, t)
    nt = t // tm
    halo = SUBLANES
    blocks_per_tile = tm // halo

    rb = 2 * SUBLANES
    assert tm % rb == 0
    n_chunk = tm // rb

    def body(b_ref, c_ref, v_ref, zc_ref, zs_ref, cp_ref, vp_ref, y_ref, dm_ref, cw_ref, cb_ref, wg_ref, bg_ref,
             dp_ref, dy_ref, gs_ref, gwg_ref, cv_buf, dc_buf, sh1_buf, sh2_buf, geb_buf, dglb_buf, ge_buf, th_buf):
        i = pl.program_id(0)

        @pl.when(i == 0)
        def _():
            dc_buf[tm:, :] = jnp.zeros((halo, dc), F32)
            gs_ref[...] = jnp.zeros_like(gs_ref)
            gwg_ref[...] = jnp.zeros_like(gwg_ref)

        def rows(r):
            return pl.ds(r * rb, rb)

        def rows_after_halo(r):
            return pl.ds(halo + r * rb, rb)

        def chunks(step, carry):
            for r in range(n_chunk):
                carry = step(r, carry)
            return carry

        def fold(v):
            return v[0:SUBLANES] + v[SUBLANES:rb]

        first_tile = (i == nt - 1)
        cv_buf[0:halo, :] = jnp.where(first_tile, 0.0, cp_ref[...] * vp_ref[...])
        cv_buf[halo:, :] = c_ref[...] * v_ref[...]
        sh1_buf[...] = cv_buf[halo - 1:halo - 1 + tm, :]
        sh2_buf[...] = cv_buf[halo - 2:halo - 2 + tm, :]
        w0, w1, w2, cb = cw_ref[0:1, :], cw_ref[1:2, :], cw_ref[2:3, :], cb_ref[...]

        def conv_pass_a(r, carry):
            rs = rows(r)
            bv, zc, dyc = b_ref[rs, :], zc_ref[rs, :], dm_ref[rs, 0:dc]
            conv = cb + w2 * cv_buf[rows_after_halo(r), :] + w1 * sh1_buf[rs, :] + w0 * sh2_buf[rs, :]
            sz, sgc = _silu(zc)
            dp_ref[rs, 0:dc] = (dyc * conv * sz).astype(BF16)
            dp_ref[rs, 3 * dc:4 * dc] = (dyc * bv * conv * _dsilu(zc, sgc)).astype(BF16)
            dc_buf[rs, :] = dyc * bv * sz
            return carry

        chunks(conv_pass_a, 0)
        sh1_buf[...] = dc_buf[1:1 + tm, :]
        sh2_buf[...] = dc_buf[2:2 + tm, :]

        def conv_pass_b(r, acc):
            rs = rows(r)
            dconv, d1, d2 = dc_buf[rs, :], sh1_buf[rs, :], sh2_buf[rs, :]
            cv = cv_buf[rows_after_halo(r), :]
            dcv = w2 * dconv + w1 * d1 + w0 * d2
            dp_ref[rs, dc:2 * dc] = (dcv * v_ref[rs, :]).astype(BF16)
            dp_ref[rs, 2 * dc:3 * dc] = (dcv * c_ref[rs, :]).astype(BF16)
            return (acc[0] + fold(cv * d2), acc[1] + fold(cv * d1), acc[2] + fold(cv * dconv), acc[3] + fold(dconv))

        zero = jnp.zeros((SUBLANES, dc), F32)
        sums = chunks(conv_pass_b, (zero, zero, zero, zero))
        for k in range(4):
            gs_ref[k:k + 1, :] += jnp.sum(sums[k], axis=0, keepdims=True)
        dc_buf[tm:, :] = dc_buf[0:halo, :]

        def glu_pass_a(r, carry):
            rs = rows(r)
            ge, th = _gelu(y_ref[rs, :])
            geb_buf[rs, :] = ge.astype(BF16)
            ge_buf[rs, :] = ge
            th_buf[rs, :] = th
            return carry

        chunks(glu_pass_a, 0)
        wg = wg_ref[...]
        gl_buf = cv_buf.at[halo:halo + tm]
        keep_buf = dc_buf.at[0:tm]
        gl_buf[...] = jnp.dot(geb_buf[...], wg, preferred_element_type=F32) + bg_ref[...]
        dp_ref[:, 4 * dc:5 * dc] = jnp.zeros((tm, dc), BF16)

        def glu_pass_b(r, acc):
            rs = rows(r)
            ge = ge_buf[rs, :]
            sg = jax.nn.sigmoid(gl_buf[rs, :])
            zs, dys = zs_ref[rs, :], dm_ref[rs, dc:]
            szs, sgs = _silu(zs)
            dp_ref[rs, 5 * dc:] = (dys * ge * sg * _dsilu(zs, sgs)).astype(BF16)
            d_ys = dys * szs
            dgl = d_ys * ge * sg * (1.0 - sg)
            dglb_buf[rs, :] = dgl.astype(BF16)
            keep_buf[rs, :] = d_ys * sg
            return acc + fold(dgl)

        gs_ref[4:5, :] += jnp.sum(chunks(glu_pass_b, zero), axis=0, keepdims=True)
        dglb = dglb_buf[...]
        gwg_ref[...] += lax.dot_general(geb_buf[...], dglb, (((0,), (0,)), ((), ())), preferred_element_type=F32)
        gl_buf[...] = lax.dot_general(dglb, wg, (((1,), (1,)), ((), ())), preferred_element_type=F32)

        def glu_pass_c(r, carry):
            rs = rows(r)
            dy_ref[rs, :] = (keep_buf[rs, :] + gl_buf[rs, :]) * _dgelu(y_ref[rs, :], th_buf[rs, :])
            return carry

        chunks(glu_pass_c, 0)

    col = lambda j: pl.BlockSpec((tm, dc), lambda i, j=j: (nt - 1 - i, j))
    prev = lambda j: pl.BlockSpec((halo, dc), lambda i, j=j: (jnp.maximum((nt - 1 - i) * blocks_per_tile - 1, 0), j))
    rev = lambda i: (nt - 1 - i, 0)
    fixed = lambda i: (0, 0)
    return _pallas(
        body, name="gate_bwd", grid=(nt,),
        in_specs=[col(0), col(1), col(2), col(3), col(5), prev(1), prev(2), pl.BlockSpec((tm, dssm), rev),
                  pl.BlockSpec((tm, dc + dssm), rev), pl.BlockSpec(conv_w.shape, fixed), pl.BlockSpec((1, dc), fixed),
                  pl.BlockSpec(w_glu.shape, fixed), pl.BlockSpec((1, dssm), fixed)],
        out_specs=[pl.BlockSpec((tm, 6 * dc), rev), pl.BlockSpec((tm, dssm), rev),
                   pl.BlockSpec((2 * SUBLANES, dc), fixed), pl.BlockSpec((dssm, dssm), fixed)],
        out_shape=[jax.ShapeDtypeStruct((t, 6 * dc), BF16), jax.ShapeDtypeStruct((t, dssm), F32),
                   jax.ShapeDtypeStruct((2 * SUBLANES, dc), F32), jax.ShapeDtypeStruct((dssm, dssm), F32)],
        scratch_shapes=[pltpu.VMEM((tm + halo, dc), F32), pltpu.VMEM((tm + halo, dc), F32), pltpu.VMEM((tm, dc), F32),
                        pltpu.VMEM((tm, dc), F32), pltpu.VMEM((tm, dssm), BF16), pltpu.VMEM((tm, dssm), BF16),
                        pltpu.VMEM((tm, dssm), F32), pltpu.VMEM((tm, dssm), F32)],
        compiler_params=_cp(2 * (6 * 4 * tm * dc + 8 * tm * dc + 12 * tm * dc + 4 * tm * dc + 6 * dssm * dssm)
                            + 28 * tm * dc + 8 * MIB, dimension_semantics=("arbitrary",)),
    )(proj, proj, proj, proj, proj, proj, proj, y, dmix, conv_w, conv_b, w_glu, b_glu)


def _allgather_flat(v, name):
    r, c = v.shape
    rels = [(dx, dy, dc) for dx in (0, 1) for dy in (0, 1) for dc in (0, 1)][1:]

    def body(v_ref, out_ref, send, recv):
        x, y, cc = _my_place()
        me = 4 * x + 2 * y + cc

        def peer(rel):
            dx, dy, dc = rel
            return (1 - x if dx else x, 1 - y if dy else y, 1 - cc if dc else cc)

        def rc(s, slot, to):
            return pltpu.make_async_remote_copy(src_ref=v_ref, dst_ref=out_ref.at[slot], send_sem=send.at[s],
                                                recv_sem=recv.at[s], device_id=to, device_id_type=MESH)

        sent = []
        for s, rel in enumerate(rels):
            cp = rc(s, me, peer(rel))
            cp.start()
            sent.append(cp)
        out_ref[me] = v_ref[...]
        for s, rel in enumerate(rels):
            px, py, pc = peer(rel)
            rc(s, 4 * px + 2 * py + pc, (px, py, pc)).wait_recv()
        for cp in sent:
            cp.wait_send()

    return _pallas(
        body, name=name, out_shape=jax.ShapeDtypeStruct((N_DEV, r, c), F32),
        in_specs=[pl.BlockSpec(memory_space=pltpu.VMEM)], out_specs=pl.BlockSpec(memory_space=pltpu.VMEM),
        scratch_shapes=[pltpu.SemaphoreType.DMA((N_DEV - 1,)), pltpu.SemaphoreType.DMA((N_DEV - 1,))],
        compiler_params=_cp((N_DEV + 2) * 4 * r * c + 8 * MIB),
    )(v)


def _rs_pair_exchange(grads, name):
    n = len(grads)

    def body(*refs):
        ins, outs = refs[:n], refs[n:2 * n]
        send, recv = refs[2 * n:]
        x, y, c = _my_place()
        cps = []
        for t in range(n):
            cp = pltpu.make_async_remote_copy(src_ref=ins[t].at[:, 1 - c], dst_ref=outs[t], send_sem=send.at[t],
                                              recv_sem=recv.at[t], device_id=(x, y, 1 - c), device_id_type=MESH)
            cp.start()
            cps.append(cp)
        for cp in cps:
            cp.wait()

    any_spec = pl.BlockSpec(memory_space=pl.ANY)
    return _pallas(
        body, name=name, in_specs=[any_spec] * n, out_specs=[any_spec] * n,
        out_shape=[jax.ShapeDtypeStruct((g.shape[0],) + g.shape[2:], g.dtype) for g in grads],
        scratch_shapes=[pltpu.SemaphoreType.DMA((n,)), pltpu.SemaphoreType.DMA((n,))],
        compiler_params=_cp(16 * MIB),
    )(*grads)


def _rs_pair_add(place, grad, got, name):
    nk, _, r2, c = grad.shape
    tr = min(TR_ELT, r2)
    other = lambda s, p: (p[1] + 1 + s) % nk

    def send_body(place_ref, g_ref, r_ref, o_ref):
        del place_ref
        o_ref[...] = (g_ref[...] + r_ref[...].astype(F32)).astype(BF16)

    to_send = _pallas(
        send_body, name=name + "_send",
        grid_spec=pltpu.PrefetchScalarGridSpec(
            num_scalar_prefetch=1, grid=(nk - 1, r2 // tr),
            in_specs=[pl.BlockSpec((None, None, tr, c), lambda s, i, p: (other(s, p), p[0], i, 0)),
                      pl.BlockSpec((None, tr, c), lambda s, i, p: (other(s, p), i, 0))],
            out_specs=pl.BlockSpec((None, tr, c), lambda s, i, p: (other(s, p), i, 0))),
        out_shape=jax.ShapeDtypeStruct((nk, r2, c), BF16),
        compiler_params=_cp(2 * 8 * tr * c + 8 * MIB),
    )(place, grad, got)

    def keep_body(place_ref, g_ref, r_ref, o_ref):
        del place_ref
        o_ref[...] = g_ref[...] + r_ref[...].astype(F32)

    to_keep = _pallas(
        keep_body, name=name + "_keep",
        grid_spec=pltpu.PrefetchScalarGridSpec(
            num_scalar_prefetch=1, grid=(r2 // tr,),
            in_specs=[pl.BlockSpec((None, None, tr, c), lambda i, p: (p[1], p[0], i, 0)),
                      pl.BlockSpec((None, tr, c), lambda i, p: (p[1], i, 0))],
            out_specs=pl.BlockSpec((tr, c), lambda i, p: (i, 0))),
        out_shape=jax.ShapeDtypeStruct((r2, c), F32),
        compiler_params=_cp(2 * 10 * tr * c + 8 * MIB),
    )(place, grad, got)
    return to_send, to_keep


_HBM_SPEC = pl.BlockSpec(memory_space=pltpu.HBM)
_SEM_SPEC = pl.BlockSpec(memory_space=pltpu.SEMAPHORE)
_DATAFLOW = pltpu.SideEffectType.DATAFLOW_SIDE_EFFECTING


def _split_copy_start(srcs, land_shapes, plan, n_sems, name, after=()):
    ns, nl, na = len(srcs), len(land_shapes), len(after)

    def body(*refs):
        src_refs, land_refs = refs[:ns], refs[ns:ns + nl]
        send, recv = refs[ns + nl + na], refs[ns + nl + na + 1]
        token = refs[-1]
        sends, _ = plan(src_refs, land_refs)
        for src, dst, to, si, ri in sends:
            pltpu.make_async_remote_copy(src_ref=src, dst_ref=dst, send_sem=send.at[si], recv_sem=recv.at[ri],
                                         device_id=to, device_id_type=MESH).start()
        token[...] = jnp.zeros_like(token)

    lands = [lax.empty(shp, dt) for shp, dt in land_shapes]
    through = [pltpu.HBM(a.shape, a.dtype) for a in srcs] + [pltpu.HBM(shp, dt) for shp, dt in land_shapes]
    out = pl.pallas_call(
        body, name=name,
        out_shape=(pltpu.SemaphoreType.DMA((n_sems,)), pltpu.SemaphoreType.DMA((n_sems,)), *through,
                   jax.ShapeDtypeStruct((SUBLANES, SLAB), F32)),
        in_specs=[_HBM_SPEC] * (ns + nl) + [pl.BlockSpec(memory_space=pl.ANY)] * na,
        out_specs=(_SEM_SPEC, _SEM_SPEC, *([_HBM_SPEC] * (ns + nl)), pl.BlockSpec(memory_space=pltpu.VMEM)),
        input_output_aliases={i: 2 + i for i in range(ns + nl)},
        compiler_params=pltpu.CompilerParams(has_side_effects=_DATAFLOW),
    )(*[pltpu.with_memory_space_constraint(a, pltpu.HBM) for a in (*srcs, *lands)], *after)
    return out[0], out[1], list(out[2:2 + ns]), list(out[2 + ns:2 + ns + nl]), out[-1]


def _split_copy_wait(send, recv, srcs, lands, plan, after, name):
    ns, nl, na = len(srcs), len(lands), len(after)

    def body(*refs):
        src_refs, land_refs = refs[:ns], refs[ns:ns + nl]
        send_ref, recv_ref = refs[ns + nl], refs[ns + nl + 1]
        sends, arrivals = plan(src_refs, land_refs)
        for src, dst, to, si, ri in sends:
            pltpu.make_async_remote_copy(src_ref=src, dst_ref=dst, send_sem=send_ref.at[si], recv_sem=recv_ref.at[ri],
                                         device_id=to, device_id_type=MESH).wait_send()
        for (src, _, to, si, _), (view, ri) in zip(sends, arrivals):
            pltpu.make_async_remote_copy(src_ref=view, dst_ref=view, send_sem=send_ref.at[si], recv_sem=recv_ref.at[ri],
                                         device_id=to, device_id_type=MESH).wait_recv()

    out = pl.pallas_call(
        body, name=name,
        out_shape=[pltpu.HBM(a.shape, a.dtype) for a in (*srcs, *lands)],
        in_specs=[_HBM_SPEC] * (ns + nl) + [_SEM_SPEC, _SEM_SPEC] + [pl.BlockSpec(memory_space=pl.ANY)] * na,
        out_specs=[_HBM_SPEC] * (ns + nl),
        input_output_aliases={i: i for i in range(ns + nl)},
        compiler_params=pltpu.CompilerParams(has_side_effects=_DATAFLOW),
    )(*srcs, *lands, send, recv, *after)
    return list(out[:ns]), list(out[ns:])


def _chip_exchange_plan(n):
    per = N_CHIPS - 1

    def plan(srcs, lands):
        x, y, c = _my_place()
        sends, arrivals = [], []
        for t in range(n):
            for j, (cx, cy) in enumerate(_other_chips(x, y)):
                sends.append((srcs[t].at[2 * cx + cy], lands[t].at[j], (cx, cy, c), t * per + j, t * per + j))
                arrivals.append((lands[t].at[j], t * per + j))
        return sends, arrivals

    return plan


def _gather_half_plan(n):
    per = N_CHIPS - 1

    def plan(srcs, lands):
        x, y, c = _my_place()
        k = 2 * x + y
        sends, arrivals = [], []
        for t in range(n):
            for j, (cx, cy) in enumerate(_other_chips(x, y)):
                sends.append((srcs[t].at[c], lands[t].at[k, c], (cx, cy, c), t * per + j, t * per + j))
                arrivals.append((lands[t].at[2 * cx + cy, c], t * per + j))
        return sends, arrivals

    return plan


def _gather_complete(own, landed, name):
    def body(own_ref, in_ref, out_ref, send, recv, local):
        x, y, c = _my_place()
        sib = (x, y, 1 - c)
        mine = pltpu.make_async_copy(own_ref, out_ref.at[2 * x + y], local)
        mine.start()
        sent = []
        for j, (cx, cy) in enumerate(_other_chips(x, y)):
            half = (2 * cx + cy, c)
            cp = pltpu.make_async_remote_copy(src_ref=in_ref.at[half], dst_ref=out_ref.at[half], send_sem=send.at[j],
                                              recv_sem=recv.at[j], device_id=sib, device_id_type=MESH)
            cp.start()
            sent.append(cp)
        for j, (cx, cy) in enumerate(_other_chips(x, y)):
            other = out_ref.at[2 * cx + cy, 1 - c]
            pltpu.make_async_remote_copy(src_ref=other, dst_ref=other, send_sem=send.at[j], recv_sem=recv.at[j],
                                         device_id=sib, device_id_type=MESH).wait_recv()
        for cp in sent:
            cp.wait_send()
        mine.wait()

    any_spec = pl.BlockSpec(memory_space=pl.ANY)
    per = N_CHIPS - 1
    return _pallas(
        body, name=name, in_specs=[any_spec, any_spec], out_specs=any_spec,
        out_shape=jax.ShapeDtypeStruct(landed.shape, landed.dtype), input_output_aliases={1: 0},
        scratch_shapes=[pltpu.SemaphoreType.DMA((per,)), pltpu.SemaphoreType.DMA((per,)), pltpu.SemaphoreType.DMA],
        compiler_params=_cp(16 * MIB),
    )(own, landed)


def _gather_direct_plan(n):
    per = 2 * (N_CHIPS - 1)

    def plan(srcs, lands):
        x, y, c = _my_place()
        k = 2 * x + y
        sends, arrivals = [], []
        for t in range(n):
            for j, (cx, cy) in enumerate(_other_chips(x, y)):
                for core in (0, 1):
                    sends.append((srcs[t].at[c], lands[t].at[k, c], (cx, cy, core),
                                  t * per + 2 * j + core, t * per + 2 * j + c))
                    arrivals.append((lands[t].at[2 * cx + cy, core], t * per + 2 * j + core))
        return sends, arrivals

    return plan


def _rs_chip_add(kept, got, name):
    r2, c = kept.shape
    tr = min(TR_ELT, r2)

    def body(s_ref, q_ref, o_ref):
        o_ref[...] = ((s_ref[...] + q_ref[0].astype(F32)) + q_ref[1].astype(F32)) + q_ref[2].astype(F32)

    blk = pl.BlockSpec((tr, c), lambda i: (i, 0))
    return _pallas(
        body, name=name, grid=(r2 // tr,),
        in_specs=[blk, pl.BlockSpec((N_CHIPS - 1, tr, c), lambda i: (0, i, 0))], out_specs=blk,
        out_shape=jax.ShapeDtypeStruct((r2, c), F32),
        compiler_params=_cp(2 * 14 * tr * c + 8 * MIB),
    )(kept, got)


def _rs_pair_share(halves, name):
    n = len(halves)

    def body(*refs):
        ins, outs = refs[:n], refs[n:2 * n]
        send, recv, local = refs[2 * n:]
        x, y, c = _my_place()
        cps, mine = [], []
        for t in range(n):
            lc = pltpu.make_async_copy(ins[t], outs[t].at[c], local.at[t])
            lc.start()
            mine.append(lc)
            cp = pltpu.make_async_remote_copy(src_ref=ins[t], dst_ref=outs[t].at[c], send_sem=send.at[t],
                                              recv_sem=recv.at[t], device_id=(x, y, 1 - c), device_id_type=MESH)
            cp.start()
            cps.append(cp)
        for t in range(n):
            other = outs[t].at[1 - c]
            pltpu.make_async_remote_copy(src_ref=other, dst_ref=other, send_sem=send.at[t], recv_sem=recv.at[t],
                                         device_id=(x, y, 1 - c), device_id_type=MESH).wait_recv()
        for cp in cps:
            cp.wait_send()
        for lc in mine:
            lc.wait()

    any_spec = pl.BlockSpec(memory_space=pl.ANY)
    return _pallas(
        body, name=name, in_specs=[any_spec] * n, out_specs=[any_spec] * n,
        out_shape=[jax.ShapeDtypeStruct((2,) + h.shape, F32) for h in halves],
        scratch_shapes=[pltpu.SemaphoreType.DMA((n,)), pltpu.SemaphoreType.DMA((n,)), pltpu.SemaphoreType.DMA((n,))],
        compiler_params=_cp(16 * MIB),
    )(*halves)


_PACK_TILE = SUBLANES * SLAB


def _pack(arrays):
    rows = []
    for a in arrays:
        flat = a.reshape(-1).astype(F32)
        padded = -(-flat.shape[0] // _PACK_TILE) * _PACK_TILE
        rows.append(jnp.pad(flat, (0, padded - flat.shape[0])).reshape(-1, SLAB))
    n_rows = sum(r.shape[0] for r in rows)
    if n_rows % (2 * SUBLANES):
        rows.append(jnp.zeros((SUBLANES, SLAB), F32))
    return jnp.concatenate(rows, axis=0)


def _unpack(packed, shapes):
    out, row = [], 0
    for shp in shapes:
        size = math.prod(shp)
        nrow = -(-size // _PACK_TILE) * SUBLANES
        out.append(packed[row:row + nrow].reshape(-1)[:size].reshape(shp))
        row += nrow
    return out


def kernel(x, norm_pre_g, w_in, conv_w, conv_b, ssm_a_re, ssm_a_im, ssm_log_dt, ssm_b_re, ssm_b_im, ssm_c_re, ssm_c_im, ssm_d, w_glu, b_glu, w_out, norm_post_g, loss_target, m_norm_pre_g, m_w_in, m_conv_w, m_conv_b, m_ssm_a_re, m_ssm_a_im, m_ssm_log_dt, m_ssm_b_re, m_ssm_b_im, m_ssm_c_re, m_ssm_c_im, m_ssm_d, m_w_glu, m_b_glu, m_w_out, m_norm_post_g, v_norm_pre_g, v_w_in, v_conv_w, v_conv_b, v_ssm_a_re, v_ssm_a_im, v_ssm_log_dt, v_ssm_b_re, v_ssm_b_im, v_ssm_c_re, v_ssm_c_im, v_ssm_d, v_w_glu, v_b_glu, v_w_out, v_norm_post_g):
    xs, tgt = x[0], loss_target[0]
    t, d = xs.shape
    dc = conv_b.shape[0]
    dssm = ssm_d.shape[0]
    g, p = ssm_a_re.shape
    h = SSM_H
    nq = dssm // SLAB
    n_shard = w_in.shape[1]
    steps = min(T_SCAN, t) // SUBLANES
    mx, my, mc = _my_place()
    chip = 2 * mx + my
    place = jnp.stack([mc, chip]).astype(jnp.int32)

    cw_cols = conv_w.shape[1]
    cw_pad = -(-cw_cols // SLAB) * SLAB
    cw_blk = jnp.zeros((SUBLANES, cw_pad), F32).at[:conv_w.shape[0], :cw_cols].set(conv_w)
    cw_all = _allgather_flat(cw_blk, "allgather_conv_w")
    conv_w_full = jnp.concatenate([cw_all[2 * k, :, :cw_cols] for k in range(N_CHIPS)], axis=1)

    halves = lambda a: a.reshape(2, a.shape[0] // 2, a.shape[1])
    win_half = halves(_cast_bf16(w_in, "cast_w_in", cw_all[0, :, :SLAB]))
    win_plan = _gather_half_plan(1)
    win_send, win_recv, win_srcs, win_lands, win_token = _split_copy_start(
        [win_half], [((N_CHIPS,) + win_half.shape, BF16)], win_plan, N_CHIPS - 1, "gather_w_in_start")
    behind_w_in = win_token[0:1, 0:1]

    small_names = ["norm_pre_g", "conv_w", "conv_b", "ssm_a_re", "ssm_a_im", "ssm_log_dt", "ssm_b_re", "ssm_b_im",
                   "ssm_c_re", "ssm_c_im", "ssm_d", "b_glu", "norm_post_g", "loss"]
    zeros_cw = jnp.zeros((conv_w.shape[0], dc), F32)
    one0 = jnp.zeros((1,), F32)
    hp = lambda b: b.transpose(0, 2, 1)
    small_w = dict(norm_pre_g=norm_pre_g, conv_w=zeros_cw, conv_b=conv_b, ssm_a_re=ssm_a_re, ssm_a_im=ssm_a_im,
                   ssm_log_dt=ssm_log_dt, ssm_b_re=hp(ssm_b_re), ssm_b_im=hp(ssm_b_im), ssm_c_re=ssm_c_re, ssm_c_im=ssm_c_im,
                   ssm_d=ssm_d, b_glu=b_glu, norm_post_g=norm_post_g, loss=one0)
    small_m = dict(norm_pre_g=m_norm_pre_g, conv_w=zeros_cw, conv_b=m_conv_b, ssm_a_re=m_ssm_a_re, ssm_a_im=m_ssm_a_im,
                   ssm_log_dt=m_ssm_log_dt, ssm_b_re=hp(m_ssm_b_re), ssm_b_im=hp(m_ssm_b_im), ssm_c_re=m_ssm_c_re,
                   ssm_c_im=m_ssm_c_im, ssm_d=m_ssm_d, b_glu=m_b_glu, norm_post_g=m_norm_post_g, loss=one0)
    small_v = dict(norm_pre_g=v_norm_pre_g, conv_w=zeros_cw, conv_b=v_conv_b, ssm_a_re=v_ssm_a_re, ssm_a_im=v_ssm_a_im,
                   ssm_log_dt=v_ssm_log_dt, ssm_b_re=hp(v_ssm_b_re), ssm_b_im=hp(v_ssm_b_im), ssm_c_re=v_ssm_c_re,
                   ssm_c_im=v_ssm_c_im, ssm_d=v_ssm_d, b_glu=v_b_glu, norm_post_g=v_norm_post_g, loss=one0)
    w_pack, m_pack, v_pack = [_pack([group[nm] for nm in small_names]) + behind_w_in
                              for group in (small_w, small_m, small_v)]
    side = [halves(_cast_bf16(w_out, "cast_w_out", win_token)), halves(_cast_bf16(w_glu, "cast_w_glu", win_token))]

    expand = jnp.tile(jnp.eye(p, dtype=F32), (1, h))
    b_re2, b_im2 = hp(ssm_b_re).reshape(g, h * p), hp(ssm_b_im).reshape(g, h * p)
    log_dt2 = ssm_log_dt.reshape(g, 1) + behind_w_in
    lbr, lbi, bbr2, bbi2, pw3 = _zoh_fwd(ssm_a_re, ssm_a_im, log_dt2, b_re2, b_im2, expand, steps)
    lam = jnp.stack([lbr.reshape(g * p), lbi.reshape(g * p)])
    pw = pw3.reshape(2, g * p)
    to_slab_b = lambda b2: _blockdiag(b2.reshape(nq, GROUPS_PER_SLAB, h, p))
    bq = jnp.concatenate([to_slab_b(bbr2), to_slab_b(bbi2)], axis=2).astype(BF16)
    to_slab_c = lambda c3: _blockdiag(c3.reshape(nq, GROUPS_PER_SLAB, h, p).transpose(0, 1, 3, 2))
    cq = jnp.concatenate([to_slab_c(ssm_c_re), to_slab_c(-ssm_c_im)], axis=1).astype(BF16)

    g_pre2, g_post2 = norm_pre_g.reshape(1, d), norm_post_g.reshape(1, d)
    conv_b2, b_glu2, d_skip2 = conv_b.reshape(1, dc), b_glu.reshape(1, dssm), ssm_d.reshape(1, dssm)
    u_block = 4 * dc // dssm

    hb = _prenorm(xs, g_pre2 + behind_w_in)
    proj_own = _inproj_own(place, hb, win_srcs[0].reshape(d, n_shard))
    win_own, win_landed = _split_copy_wait(win_send, win_recv, win_srcs, win_lands, win_plan,
                                           [proj_own, bq, cq, w_pack, m_pack, v_pack, *side], "gather_w_in_wait")
    win_g = _gather_complete(win_own[0], win_landed[0], "gather_w_in_complete")
    win_b = win_g.reshape(N_CHIPS, d, n_shard)
    side_plan = _gather_direct_plan(len(side))
    side_sems = 2 * (N_CHIPS - 1) * len(side)
    ag_send, ag_recv, ag_srcs, ag_lands, ag_token = _split_copy_start(
        side, [((N_CHIPS,) + a.shape, BF16) for a in side], side_plan, side_sems, "gather_side_weights_start",
        after=[win_g])
    proj = _inproj_rest(place, hb, win_b, proj_own, ag_token)
    y, cin = _ssm_fwd(proj, u_block, bq, cq, lam, pw, d_skip2)
    side_own, side_all = _split_copy_wait(ag_send, ag_recv, ag_srcs, ag_lands, side_plan, [cin],
                                          "gather_side_weights_wait")
    wout_g, wglu_g = [lax.dynamic_update_index_in_dim(all_, own, chip, 0) for own, all_ in zip(side_own, side_all)]
    wout_b = wout_g.reshape(dc + dssm, d)
    wglu_b = wglu_g.reshape(dssm, dssm)
    mix = _gate_fwd(proj, y, conv_w_full, conv_b2, wglu_b, b_glu2, dc)
    loss_blk, dout, dob, dmix, gg_post = _outproj(mix, wout_b, xs, tgt, g_post2)

    def reduce_start(grads, to_send, tags, group):
        got = _rs_pair_exchange(to_send, "rs_pair_exchange_" + group)
        sums = [_rs_pair_add(place, gt, rt, "rs_pair_add_" + tg) for gt, rt, tg in zip(grads, got, tags)]
        plan = _chip_exchange_plan(len(grads))
        started = _split_copy_start([s16 for s16, _ in sums],
                                    [((N_CHIPS - 1,) + s16.shape[1:], BF16) for s16, _ in sums], plan,
                                    (N_CHIPS - 1) * len(grads), "rs_chip_exchange_" + group + "_start")
        return plan, started, [s32 for _, s32 in sums]

    def reduce_finish(plan, started, sums32, tags, group, after):
        send, recv, srcs, lands, _ = started
        _, landed = _split_copy_wait(send, recv, srcs, lands, plan, after, "rs_chip_exchange_" + group + "_wait")
        mine = [_rs_chip_add(s32, qt, "rs_chip_add_" + tg) for s32, qt, tg in zip(sums32, landed, tags)]
        full = _rs_pair_share(mine, "rs_pair_share_" + group)
        return [f.reshape(2 * f.shape[1], f.shape[2]) for f in full]

    gw_out, gw_out16 = _matmul_tn(mix, dob, 1, "grad_w_out")
    dproj, dy, gsmall, gw_glu = _gate_bwd(proj, y, dmix, conv_w_full, conv_b2, wglu_b, b_glu2, dc)
    as_out = lambda a: a.reshape(N_CHIPS, 2, (dc + dssm) // (2 * N_CHIPS), d)
    gw_glu4 = gw_glu.reshape(N_CHIPS, 2, dssm // (2 * N_CHIPS), dssm)
    rs_a = reduce_start([as_out(gw_out), gw_glu4], [as_out(gw_out16), gw_glu4], ["w_out", "w_glu"], "a")
    dproj, gb_dense, gc_dense, glam, gd = _ssm_bwd(proj, u_block, dy, cin, dproj, bq, cq, lam, pw,
                                                   d_skip2 + rs_a[1][4][0:1, 0:1])
    gw_in, gw_in16 = _matmul_tn(hb, dproj, N_CHIPS, "grad_w_in")
    as_in = lambda a: a.reshape(N_CHIPS, 2, d // 2, n_shard)
    rs_b = reduce_start([as_in(gw_in)], [as_in(gw_in16)], ["w_in"], "b")
    gx, gg_pre = _dh_prenorm_bwd(dproj, win_b, xs, dout, g_pre2 + rs_b[1][4][0:1, 0:1])

    gb_blocks = _blockdiag_take(gb_dense, h, p)
    g_bbr2 = gb_blocks[:, :, :, 0, :].reshape(g, h * p)
    g_bbi2 = gb_blocks[:, :, :, 1, :].reshape(g, h * p)
    gc_blocks = _blockdiag_take(gc_dense, h, p)
    g_c_re = gc_blocks[:, :, :, 0, :].reshape(g, h, p)
    g_c_im = -gc_blocks[:, :, :, 1, :].reshape(g, h, p)
    g_a_re, g_a_im, g_ld, g_b_re2, g_b_im2 = _zoh_bwd(
        ssm_a_re, ssm_a_im, log_dt2, b_re2, b_im2, expand,
        glam[0].reshape(g, p), glam[1].reshape(g, p), g_bbr2, g_bbi2)

    small_g = {
        "norm_pre_g": gg_pre[0], "conv_w": gsmall[0:3], "conv_b": gsmall[3], "ssm_a_re": g_a_re, "ssm_a_im": g_a_im,
        "ssm_log_dt": g_ld.reshape(g), "ssm_b_re": g_b_re2.reshape(g, h, p), "ssm_b_im": g_b_im2.reshape(g, h, p),
        "ssm_c_re": g_c_re, "ssm_c_im": g_c_im, "ssm_d": gd[0], "b_glu": gsmall[4], "norm_post_g": gg_post[0],
        "loss": loss_blk[0, 0:1],
    }
    shapes = [small_w[nm].shape for nm in small_names]
    g_pack = _allreduce_small(_pack([small_g[nm] for nm in small_names]))
    packs = _adamw_small(g_pack, w_pack, m_pack, v_pack)
    sg, sd, sm, sv = [dict(zip(small_names, _unpack(pk, shapes))) for pk in (g_pack, *packs)]
    for group in (sg, sd, sm, sv):
        group.update(ssm_b_re=hp(group["ssm_b_re"]), ssm_b_im=hp(group["ssm_b_im"]))
    loss = sg["loss"][0]

    g_cw = lax.dynamic_slice_in_dim(sg["conv_w"], chip * cw_cols, cw_cols, axis=1)
    pad_cw = lambda a: jnp.zeros((SUBLANES, cw_pad), F32).at[:a.shape[0], :cw_cols].set(a)
    cut_cw = lambda a: a[:conv_w.shape[0], :cw_cols]
    _, d_cw, m_cw, v_cw = [cut_cw(a) for a in _adamw(pad_cw(conv_w), pad_cw(g_cw), pad_cw(m_conv_w), pad_cw(v_conv_w),
                                                     "adamw_conv_w")]

    g_wout, g_wglu = reduce_finish(*rs_a, ["w_out", "w_glu"], "a", [d_cw, packs[0]])
    (g_win,) = reduce_finish(*rs_b, ["w_in"], "b", [g_wout])
    g_win, d_win, m_win, v_win = _adamw(w_in, g_win, m_w_in, v_w_in, "adamw_w_in")
    g_wout, d_wout, m_wout, v_wout = _adamw(w_out, g_wout, m_w_out, v_w_out, "adamw_w_out")
    g_wglu, d_wglu, m_wglu, v_wglu = _adamw(w_glu, g_wglu, m_w_glu, v_w_glu, "adamw_w_glu")

    order = ["norm_pre_g", "w_in", "conv_w", "conv_b", "ssm_a_re", "ssm_a_im", "ssm_log_dt", "ssm_b_re", "ssm_b_im",
             "ssm_c_re", "ssm_c_im", "ssm_d", "w_glu", "b_glu", "w_out", "norm_post_g"]
    grads, deltas, new_m, new_v = dict(sg), dict(sd), dict(sm), dict(sv)
    grads.update(w_in=g_win, w_out=g_wout, w_glu=g_wglu, conv_w=g_cw)
    deltas.update(w_in=d_win, w_out=d_wout, w_glu=d_wglu, conv_w=d_cw)
    new_m.update(w_in=m_win, w_out=m_wout, w_glu=m_wglu, conv_w=m_cw)
    new_v.update(w_in=v_win, w_out=v_wout, w_glu=v_wglu, conv_w=v_cw)
    return (loss, gx[None], *[grads[nm] for nm in order], *[deltas[nm] for nm in order],
            *[new_m[nm] for nm in order], *[new_v[nm] for nm in order])
```

```python
import math

import jax
import jax.numpy as jnp
from jax import lax
from jax.experimental import pallas as pl
from jax.experimental.pallas import tpu as pltpu

F32 = jnp.float32
BF16 = jnp.bfloat16
MESH = pl.DeviceIdType.MESH

EPS = 1e-6
SSM_H = 16
SSM_P = 64
GROUPS_PER_SLAB = 8
SLAB = 128
SLAB_STATES = GROUPS_PER_SLAB * SSM_P
N_CHIPS = 4
N_DEV = 8

ADAM_LR = 0.001
ADAM_B1 = 0.9
ADAM_B2 = 0.999
ADAM_EPS = 1e-08
ADAM_WD = 0.01
ADAM_STEP = 10

MIB = 1024 * 1024
VMEM_CAP = 48 * MIB
SUBLANES = 8

TM_NORM = 512
TM_PROJ = 512
TM_GATE = 256
TM_OUT = 256
TM_DH = 256
T_SCAN = 256
TK_TN = 1024
TM_TN = 1024
TR_ELT = 256


def _cp(vmem_bytes, **kw):
    return pltpu.CompilerParams(vmem_limit_bytes=int(min(VMEM_CAP, max(16 * MIB, vmem_bytes))), **kw)


def _pallas(body, **kw):
    if "grid" not in kw and "grid_spec" not in kw:
        return pl.pallas_call(body, **kw)
    pin = lambda s: pltpu.HBM(s.shape, s.dtype) if isinstance(s, jax.ShapeDtypeStruct) else s
    out_shape = kw.pop("out_shape")
    out_shape = [pin(s) for s in out_shape] if isinstance(out_shape, (list, tuple)) else pin(out_shape)
    call = pl.pallas_call(body, out_shape=out_shape, **kw)

    def run(*args):
        return call(*[pltpu.with_memory_space_constraint(a, pltpu.HBM) if jnp.issubdtype(a.dtype, jnp.floating) else a
                      for a in args])

    return run


def _my_place():
    return lax.axis_index("x"), lax.axis_index("y"), lax.axis_index("c")


def _other_chips(x, y):
    return [(1 - x, y), (x, 1 - y), (1 - x, 1 - y)]


def _silu(z):
    s = jax.nn.sigmoid(z)
    return z * s, s


def _dsilu(z, s):
    return s * (1.0 + z * (1.0 - s))


_GELU_K = math.sqrt(2.0 / math.pi)
_GELU_C = 0.044715


def _gelu(y):
    th = jnp.tanh(_GELU_K * (y + _GELU_C * y * y * y))
    return 0.5 * y * (1.0 + th), th


def _dgelu(y, th):
    return 0.5 * (1.0 + th) + 0.5 * y * (1.0 - th * th) * _GELU_K * (1.0 + 3.0 * _GELU_C * y * y)


def _cast_bf16(w, name, after=None):
    r, c = w.shape
    tr = min(TR_ELT, r)
    extra = [] if after is None else [after]

    def body(w_ref, *rest):
        rest[-1][...] = w_ref[...].astype(BF16)

    return _pallas(
        body, name=name, grid=(r // tr,),
        in_specs=[pl.BlockSpec((tr, c), lambda i: (i, 0))] + [pl.BlockSpec((SUBLANES, SLAB), lambda i: (0, 0))] * len(extra),
        out_specs=pl.BlockSpec((tr, c), lambda i: (i, 0)),
        out_shape=jax.ShapeDtypeStruct((r, c), BF16),
        compiler_params=_cp(12 * tr * c),
    )(w, *extra)


def _prenorm(x, g):
    t, d = x.shape
    tm = min(TM_NORM, t)

    def body(x_ref, g_ref, h_ref):
        xv = x_ref[...]
        r = lax.rsqrt(jnp.mean(xv * xv, axis=-1, keepdims=True) + EPS)
        h_ref[...] = (xv * r * g_ref[...]).astype(BF16)

    return _pallas(
        body, name="prenorm", grid=(t // tm,),
        in_specs=[pl.BlockSpec((tm, d), lambda i: (i, 0)), pl.BlockSpec((1, d), lambda i: (0, 0))],
        out_specs=pl.BlockSpec((tm, d), lambda i: (i, 0)),
        out_shape=jax.ShapeDtypeStruct((t, d), BF16),
        compiler_params=_cp(20 * tm * d),
    )(x, g)


def _inproj_own(place, a, w_own):
    t, k = a.shape
    n = w_own.shape[1]
    tm = min(TM_PROJ, t)

    def body(place_ref, a_ref, b_ref, o_ref):
        del place_ref
        o_ref[...] = jnp.dot(a_ref[...], b_ref[...], preferred_element_type=F32)

    return _pallas(
        body, name="inproj_own",
        grid_spec=pltpu.PrefetchScalarGridSpec(
            num_scalar_prefetch=1, grid=(t // tm,),
            in_specs=[pl.BlockSpec((tm, k), lambda i, p: (i, 0)), pl.BlockSpec((k, n), lambda i, p: (0, 0))],
            out_specs=pl.BlockSpec((tm, n), lambda i, p: (i, p[1]))),
        out_shape=jax.ShapeDtypeStruct((t, N_CHIPS * n), F32),
        compiler_params=_cp(2 * (2 * tm * k + 2 * k * n + 4 * tm * n) + 4 * MIB),
    )(place, a, w_own)


def _inproj_rest(place, a, b, partial, after):
    t, k = a.shape
    nb, _, n = b.shape
    tm = min(TM_PROJ, t)
    shard = lambda s, p: (p[1] + 1 + s) % nb

    def body(place_ref, a_ref, b_ref, after_ref, partial_ref, o_ref):
        del place_ref, after_ref, partial_ref
        o_ref[...] = jnp.dot(a_ref[...], b_ref[...], preferred_element_type=F32)

    return _pallas(
        body, name="inproj_rest",
        grid_spec=pltpu.PrefetchScalarGridSpec(
            num_scalar_prefetch=1, grid=(nb - 1, t // tm),
            in_specs=[pl.BlockSpec((tm, k), lambda s, i, p: (i, 0)),
                      pl.BlockSpec((None, k, n), lambda s, i, p: (shard(s, p), 0, 0)),
                      pl.BlockSpec((SUBLANES, SLAB), lambda s, i, p: (0, 0)),
                      pl.BlockSpec(memory_space=pl.ANY)],
            out_specs=pl.BlockSpec((tm, n), lambda s, i, p: (i, shard(s, p)))),
        out_shape=jax.ShapeDtypeStruct((t, nb * n), F32),
        input_output_aliases={4: 0},
        compiler_params=_cp(2 * (2 * tm * k + 2 * k * n + 4 * tm * n) + 4 * MIB),
    )(place, a, b, after, partial)


def _matmul_tn(a, b, nb, name):
    t, m = a.shape
    n = b.shape[1] // nb
    tk = min(TK_TN, t)
    tma = min(TM_TN, m)

    def body(a_ref, b_ref, o_ref, o16_ref):
        k = pl.program_id(2)

        @pl.when(k == 0)
        def _():
            o_ref[...] = jnp.zeros_like(o_ref)

        o_ref[...] += lax.dot_general(a_ref[...], b_ref[...], (((0,), (0,)), ((), ())), preferred_element_type=F32)

        @pl.when(k == t // tk - 1)
        def _():
            o16_ref[...] = o_ref[...].astype(BF16)

    blk = pl.BlockSpec((None, tma, n), lambda j, i, k: (j, i, 0))
    return _pallas(
        body, name=name, grid=(nb, m // tma, t // tk),
        in_specs=[pl.BlockSpec((tk, tma), lambda j, i, k: (k, i)), pl.BlockSpec((tk, n), lambda j, i, k: (k, j))],
        out_specs=[blk, blk],
        out_shape=[jax.ShapeDtypeStruct((nb, m, n), F32), jax.ShapeDtypeStruct((nb, m, n), BF16)],
        compiler_params=_cp(2 * (2 * tk * tma + 2 * tk * n + 6 * tma * n) + 8 * MIB),
    )(a, b)


def _outproj(mix, w_out, x, tgt, g_post):
    t, dm = mix.shape
    d = w_out.shape[1]
    tm = min(TM_OUT, t)

    rb = 2 * SUBLANES
    assert tm % rb == 0

    def body(mix_ref, w_ref, x_ref, t_ref, g_ref, loss_ref, dout_ref, do_ref, dmix_ref, gg_ref, o_buf):
        @pl.when(pl.program_id(0) == 0)
        def _():
            loss_ref[...] = jnp.zeros_like(loss_ref)
            gg_ref[...] = jnp.zeros_like(gg_ref)

        w = w_ref[...]
        o_buf[...] = jnp.dot(mix_ref[...], w, preferred_element_type=F32)
        g = g_ref[...]
        fold = lambda v: v[0:SUBLANES] + v[SUBLANES:rb]
        sq_err = jnp.zeros((SUBLANES, d), F32)
        gg = jnp.zeros((SUBLANES, d), F32)
        for c in range(tm // rb):
            rs = pl.ds(c * rb, rb)
            o = o_buf[rs, :]
            r = lax.rsqrt(jnp.mean(o * o, axis=-1, keepdims=True) + EPS)
            nh = o * r
            e = x_ref[rs, :] + nh * g - t_ref[rs, :]
            sq_err = sq_err + fold(e * e)
            dout = e * (1.0 / d)
            dout_ref[rs, :] = dout
            gg = gg + fold(dout * nh)
            dn = dout * g
            do = r * (dn - nh * jnp.mean(dn * nh, axis=-1, keepdims=True))
            do_ref[rs, :] = do.astype(BF16)
        loss_ref[...] += jnp.sum(sq_err) * (0.5 / d)
        gg_ref[0:1, :] += jnp.sum(gg, axis=0, keepdims=True)
        dmix_ref[...] = lax.dot_general(do_ref[...], w, (((1,), (1,)), ((), ())), preferred_element_type=F32)

    row = lambda i: (i, 0)
    fixed = lambda i: (0, 0)
    return _pallas(
        body, name="outproj", grid=(t // tm,),
        in_specs=[pl.BlockSpec((tm, dm), row), pl.BlockSpec((dm, d), fixed), pl.BlockSpec((tm, d), row),
                  pl.BlockSpec((tm, d), row), pl.BlockSpec((1, d), fixed)],
        out_specs=[pl.BlockSpec((SUBLANES, SLAB), fixed), pl.BlockSpec((tm, d), row), pl.BlockSpec((tm, d), row),
                   pl.BlockSpec((tm, dm), row), pl.BlockSpec((SUBLANES, d), fixed)],
        out_shape=[jax.ShapeDtypeStruct((SUBLANES, SLAB), F32), jax.ShapeDtypeStruct((t, d), F32),
                   jax.ShapeDtypeStruct((t, d), BF16), jax.ShapeDtypeStruct((t, dm), F32),
                   jax.ShapeDtypeStruct((SUBLANES, d), F32)],
        scratch_shapes=[pltpu.VMEM((tm, d), F32)],
        compiler_params=_cp(2 * (2 * dm * d + tm * (2 * dm + 4 * d * 3 + 2 * d + 4 * dm)) + 4 * tm * d + 12 * MIB),
    )(mix, w_out, x, tgt, g_post)


def _dh_prenorm_bwd(dproj, w_in, x, dout, g_pre):
    t, d = x.shape
    nb, _, n = w_in.shape
    tm = min(TM_DH, t)

    def body(dp_ref, w_ref, x_ref, dout_ref, g_ref, dx_ref, gg_ref):
        @pl.when(pl.program_id(0) == 0)
        def _():
            gg_ref[...] = jnp.zeros_like(gg_ref)

        dh = None
        for k in range(nb):
            part = lax.dot_general(dp_ref[:, k * n:(k + 1) * n], w_ref[k], (((1,), (1,)), ((), ())),
                                   preferred_element_type=F32)
            dh = part if dh is None else dh + part
        xv = x_ref[...]
        r = lax.rsqrt(jnp.mean(xv * xv, axis=-1, keepdims=True) + EPS)
        xh = xv * r
        gg_ref[0:1, :] += jnp.sum(dh * xh, axis=0, keepdims=True)
        dg = dh * g_ref[...]
        dx_ref[...] = dout_ref[...] + r * (dg - xh * jnp.mean(dg * xh, axis=-1, keepdims=True))

    row = lambda i: (i, 0)
    fixed = lambda i: (0, 0)
    w_spec = pl.BlockSpec(w_in.shape, lambda i: (0, 0, 0), pipeline_mode=pl.Buffered(1))
    return _pallas(
        body, name="dh_prenorm_bwd", grid=(t // tm,),
        in_specs=[pl.BlockSpec((tm, nb * n), row), w_spec,
                  pl.BlockSpec((tm, d), row), pl.BlockSpec((tm, d), row), pl.BlockSpec((1, d), fixed)],
        out_specs=[pl.BlockSpec((tm, d), row), pl.BlockSpec((SUBLANES, d), fixed)],
        out_shape=[jax.ShapeDtypeStruct((t, d), F32), jax.ShapeDtypeStruct((SUBLANES, d), F32)],
        compiler_params=_cp(2 * nb * d * n + 2 * (2 * tm * nb * n + 12 * tm * d) + 16 * tm * d + 4 * MIB),
    )(dproj, w_in, x, dout, g_pre)


def _adamw(w, g, m, v, name):
    r, c = w.shape
    tr = min(TR_ELT, r)
    c1 = 1.0 - ADAM_B1 ** ADAM_STEP
    c2 = 1.0 - ADAM_B2 ** ADAM_STEP

    def body(w_ref, g_ref, m_ref, v_ref, go_ref, d_ref, mo_ref, vo_ref):
        gv = g_ref[...]
        go_ref[...] = gv
        mn = ADAM_B1 * m_ref[...] + (1.0 - ADAM_B1) * gv
        vn = ADAM_B2 * v_ref[...] + (1.0 - ADAM_B2) * (gv * gv)
        d_ref[...] = -ADAM_LR * ((mn / c1) / (jnp.sqrt(vn / c2) + ADAM_EPS) + ADAM_WD * w_ref[...])
        mo_ref[...] = mn
        vo_ref[...] = vn

    spec = pl.BlockSpec((tr, c), lambda i: (i, 0))
    sds = jax.ShapeDtypeStruct((r, c), F32)
    return _pallas(
        body, name=name, grid=(r // tr,), in_specs=[spec] * 4, out_specs=[spec] * 4, out_shape=[sds] * 4,
        compiler_params=_cp(2 * 8 * 4 * tr * c + 8 * MIB),
    )(w, g, m, v)


def _adamw_small(g, w, m, v):
    r, c = w.shape
    c1 = 1.0 - ADAM_B1 ** ADAM_STEP
    c2 = 1.0 - ADAM_B2 ** ADAM_STEP

    def body(g_ref, w_ref, m_ref, v_ref, d_ref, mo_ref, vo_ref):
        gv = g_ref[...]
        mn = ADAM_B1 * m_ref[...] + (1.0 - ADAM_B1) * gv
        vn = ADAM_B2 * v_ref[...] + (1.0 - ADAM_B2) * (gv * gv)
        d_ref[...] = -ADAM_LR * ((mn / c1) / (jnp.sqrt(vn / c2) + ADAM_EPS) + ADAM_WD * w_ref[...])
        mo_ref[...] = mn
        vo_ref[...] = vn

    sds = jax.ShapeDtypeStruct((r, c), F32)
    return _pallas(
        body, name="adamw_small", out_shape=[sds] * 3,
        compiler_params=_cp(12 * 4 * r * c + 8 * MIB),
    )(g, w, m, v)


def _allreduce_small(v):
    r, c = v.shape
    assert r % (2 * SUBLANES) == 0
    h = r // 2

    def body(v_ref, out_ref, got_ref, chip_ref, slots_ref, send, recv):
        x, y, cc = _my_place()
        k = 2 * x + y
        sib = (x, y, 1 - cc)

        def rc(s, src, dst, to):
            return pltpu.make_async_remote_copy(src_ref=src, dst_ref=dst, send_sem=send.at[s], recv_sem=recv.at[s],
                                                device_id=to, device_id_type=MESH)

        pair = rc(0, v_ref, got_ref, sib)
        pair.start()
        pair.wait()
        chip_ref[...] = v_ref[...] + got_ref[...]
        mine = pl.ds(pl.multiple_of(cc * h, SUBLANES), h)
        theirs = pl.ds(pl.multiple_of((1 - cc) * h, SUBLANES), h)
        chips = _other_chips(x, y)
        sent = []
        for j, (cx, cy) in enumerate(chips):
            cp = rc(1 + j, chip_ref.at[mine], slots_ref.at[k], (cx, cy, cc))
            cp.start()
            sent.append(cp)
        slots_ref[k] = chip_ref[mine, :]
        for j, (cx, cy) in enumerate(chips):
            rc(1 + j, chip_ref.at[mine], slots_ref.at[2 * cx + cy], (cx, cy, cc)).wait_recv()
        total = slots_ref[0]
        for kk in range(1, N_CHIPS):
            total = total + slots_ref[kk]
        out_ref[mine, :] = total
        for cp in sent:
            cp.wait_send()
        share = rc(N_CHIPS, out_ref.at[mine], out_ref.at[mine], sib)
        share.start()
        rc(N_CHIPS, out_ref.at[theirs], out_ref.at[theirs], sib).wait_recv()
        share.wait_send()

    vm = pl.BlockSpec(memory_space=pltpu.VMEM)
    return _pallas(
        body, name="allreduce_small", out_shape=jax.ShapeDtypeStruct((r, c), F32), in_specs=[vm], out_specs=vm,
        scratch_shapes=[pltpu.VMEM((r, c), F32), pltpu.VMEM((r, c), F32), pltpu.VMEM((N_CHIPS, h, c), F32),
                        pltpu.SemaphoreType.DMA((N_CHIPS + 1,)), pltpu.SemaphoreType.DMA((N_CHIPS + 1,))],
        compiler_params=_cp(6 * 4 * r * c + 8 * MIB),
    )(v)


def _zoh(a_re, a_im, log_dt, b_re2, b_im2, expand):
    dt = jnp.exp(log_dt)
    mag = jnp.exp(a_re * dt)
    lbr, lbi = mag * jnp.cos(a_im * dt), mag * jnp.sin(a_im * dt)
    nr, ni = lbr - 1.0, lbi
    den = a_re * a_re + a_im * a_im
    qr = (nr * a_re + ni * a_im) / den
    qi = (ni * a_re - nr * a_im) / den
    qr2 = jnp.dot(qr, expand, precision=lax.Precision.HIGHEST, preferred_element_type=F32)
    qi2 = jnp.dot(qi, expand, precision=lax.Precision.HIGHEST, preferred_element_type=F32)
    return lbr, lbi, qr2 * b_re2 - qi2 * b_im2, qr2 * b_im2 + qi2 * b_re2


def _zoh_fwd(a_re, a_im, log_dt, b_re2, b_im2, expand, power):
    g, p = a_re.shape

    def body(ar_ref, ai_ref, ld_ref, br_ref, bi_ref, e_ref, lbr_ref, lbi_ref, bbr_ref, bbi_ref, pw_ref):
        ar, ai, ld = ar_ref[...], ai_ref[...], ld_ref[...]
        lbr, lbi, bbr, bbi = _zoh(ar, ai, ld, br_ref[...], bi_ref[...], e_ref[...])
        lbr_ref[...], lbi_ref[...], bbr_ref[...], bbi_ref[...] = lbr, lbi, bbr, bbi
        dt = jnp.exp(ld) * float(power)
        mag = jnp.exp(ar * dt)
        pw_ref[0] = mag * jnp.cos(ai * dt)
        pw_ref[1] = mag * jnp.sin(ai * dt)

    gp = jax.ShapeDtypeStruct((g, p), F32)
    gph = jax.ShapeDtypeStruct(b_re2.shape, F32)
    return _pallas(
        body, name="zoh_fwd", out_shape=[gp, gp, gph, gph, jax.ShapeDtypeStruct((2, g, p), F32)],
        compiler_params=_cp(16 * MIB),
    )(a_re, a_im, log_dt, b_re2, b_im2, expand)


def _zoh_bwd(a_re, a_im, log_dt, b_re2, b_im2, expand, g_lbr, g_lbi, g_bbr, g_bbi):
    def body(ar_ref, ai_ref, ld_ref, br_ref, bi_ref, e_ref, c0, c1, c2, c3, gar, gai, gld, gbr, gbi):
        e = e_ref[...]
        _, vjp = jax.vjp(lambda a, b, c, d, f: _zoh(a, b, c, d, f, e),
                         ar_ref[...], ai_ref[...], ld_ref[...], br_ref[...], bi_ref[...])
        gar[...], gai[...], gld[...], gbr[...], gbi[...] = vjp((c0[...], c1[...], c2[...], c3[...]))

    sds = lambda a: jax.ShapeDtypeStruct(a.shape, F32)
    return _pallas(
        body, name="zoh_bwd", out_shape=[sds(a_re), sds(a_im), sds(log_dt), sds(b_re2), sds(b_im2)],
        compiler_params=_cp(16 * MIB),
    )(a_re, a_im, log_dt, b_re2, b_im2, expand, g_lbr, g_lbi, g_bbr, g_bbi)


def _blockdiag(blocks):
    n, c = blocks.shape[1], blocks.shape[3]
    rows = [jnp.pad(blocks[:, i], ((0, 0), (0, 0), (i * c, (n - 1 - i) * c))) for i in range(n)]
    return jnp.concatenate(rows, axis=1)


def _blockdiag_take(dense, r, c):
    half = GROUPS_PER_SLAB * c
    blocks = [jnp.stack([dense[:, i * r:(i + 1) * r, part * half + i * c:part * half + (i + 1) * c]
                         for part in (0, 1)], axis=2) for i in range(GROUPS_PER_SLAB)]
    return jnp.stack(blocks, axis=1)


def _scan_slab(s_ref, row0, q, lam_ref, pw_ref, car_ref, steps, reverse, prev_ref=None, prev_row0=0, glam_ref=None):
    half = SLAB_STATES // SLAB
    for pairs in (range(0, half // 2), range(half // 2, half)):
        _scan_pairs(s_ref, row0, q, pairs, lam_ref, pw_ref, car_ref, steps, reverse, prev_ref, prev_row0, glam_ref)


def _scan_pairs(s_ref, row0, q, pairs, lam_ref, pw_ref, car_ref, steps, reverse, prev_ref, prev_row0, glam_ref):
    sign = -1.0 if reverse else 1.0
    half = SLAB_STATES // SLAB
    cols = [(q * 2 * half + m, q * 2 * half + half + m, q * SLAB_STATES + m * SLAB) for m in pairs]
    nm = len(cols)
    full = (SUBLANES, SLAB)
    lam = [(jnp.broadcast_to(lam_ref[0:1, pl.ds(cl, SLAB)], full),
            jnp.broadcast_to(sign * lam_ref[1:2, pl.ds(cl, SLAB)], full)) for (_, _, cl) in cols]

    def step_rows(jj, base):
        j = (steps - 1 - jj) if reverse else jj
        return j, pl.ds(base + j * SUBLANES, SUBLANES)

    def pass1(jj, car):
        _, rows = step_rows(jj, row0)
        out = []
        for m, (cr, ci, _) in enumerate(cols):
            sr, si = car[2 * m], car[2 * m + 1]
            lr, li = lam[m]
            nr = lr * sr - li * si + s_ref[cr, rows, :]
            ni = lr * si + li * sr + s_ref[ci, rows, :]
            s_ref[cr, rows, :] = nr
            s_ref[ci, rows, :] = ni
            out += [nr, ni]
        return tuple(out)

    def run_steps(step_fn, carry):
        for jj in range(steps):
            carry = step_fn(jj, carry)
        return carry

    ends = run_steps(pass1, tuple(jnp.zeros(full, F32) for _ in range(2 * nm)))

    entry = []
    for m, (cr, ci, cl) in enumerate(cols):
        ljr = pw_ref[0:1, pl.ds(cl, SLAB)]
        lji = sign * pw_ref[1:2, pl.ds(cl, SLAB)]
        c_r = car_ref[cr, 0:1, :]
        c_i = car_ref[ci, 0:1, :]
        rows_r, rows_i = [None] * SUBLANES, [None] * SUBLANES
        order = range(SUBLANES - 1, -1, -1) if reverse else range(SUBLANES)
        for b in order:
            rows_r[b], rows_i[b] = c_r, c_i
            e_r, e_i = ends[2 * m][b:b + 1], ends[2 * m + 1][b:b + 1]
            c_r, c_i = ljr * c_r - lji * c_i + e_r, ljr * c_i + lji * c_r + e_i
        car_ref[cr, 0:1, :] = c_r
        car_ref[ci, 0:1, :] = c_i
        entry.append((jnp.concatenate(rows_r, axis=0), jnp.concatenate(rows_i, axis=0)))

    def pass2(jj, carry):
        j, rows = step_rows(jj, row0)
        decayed, acc = carry[:2 * nm], carry[2 * nm:]
        out_d, out_a = [], []
        for m, (cr, ci, cl) in enumerate(cols):
            lr, li = lam[m]
            dr, di = decayed[2 * m], decayed[2 * m + 1]
            dr, di = lr * dr - li * di, lr * di + li * dr
            nr = s_ref[cr, rows, :] + dr
            ni = s_ref[ci, rows, :] + di
            s_ref[cr, rows, :] = nr
            s_ref[ci, rows, :] = ni
            out_d += [dr, di]
            if prev_ref is not None:
                prow = pl.ds(prev_row0 + (j - 1) * SUBLANES, SUBLANES)
                qr = prev_ref[cr, prow, :]
                qi = prev_ref[ci, prow, :]
                out_a += [acc[2 * m] + (nr * qr + ni * qi), acc[2 * m + 1] + (ni * qr - nr * qi)]
        return tuple(out_d) + tuple(out_a)

    n_acc = 2 * nm if prev_ref is not None else 0
    init = tuple(e for pair in entry for e in pair) + tuple(jnp.zeros(full, F32) for _ in range(n_acc))
    accs = run_steps(pass2, init)[2 * nm:]
    if prev_ref is not None:
        for m, (_, _, cl) in enumerate(cols):
            glam_ref[0:1, pl.ds(cl, SLAB)] += jnp.sum(accs[2 * m], axis=0, keepdims=True)
            glam_ref[1:2, pl.ds(cl, SLAB)] += jnp.sum(accs[2 * m + 1], axis=0, keepdims=True)


def _permute_rows_f32(perm_bf16, v):
    hi = v.astype(BF16)
    lo = (v - hi.astype(F32)).astype(BF16)
    return (jnp.dot(perm_bf16, hi, preferred_element_type=F32) + jnp.dot(perm_bf16, lo, preferred_element_type=F32))


def _put_slab(s_ref, rows, q, val):
    per = 2 * SLAB_STATES // SLAB
    for i in range(per):
        s_ref[q * per + i, rows, :] = val[:, i * SLAB:(i + 1) * SLAB]


def _get_slab(s_ref, rows, q):
    per = 2 * SLAB_STATES // SLAB
    return jnp.concatenate([s_ref[q * per + i, rows, :] for i in range(per)], axis=1)


def _step_major_perm(tt):
    r = jnp.arange(tt)
    held = (r % SUBLANES) * (tt // SUBLANES) + r // SUBLANES
    return held[:, None] == r[None, :]


def _ssm_fwd(proj, u_block, bq, cq, lam, pw, d_skip):
    t = proj.shape[0]
    nq, ds, w2 = bq.shape
    assert ds == SLAB and w2 == 2 * SLAB_STATES
    dssm = nq * SLAB
    width = nq * w2
    ntile = width // SLAB
    tt = min(T_SCAN, t)
    steps = tt // SUBLANES
    perm = _step_major_perm(tt)
    pm, pmt = perm.astype(BF16), perm.T.astype(BF16)

    def body(u_ref, pm_ref, pmt_ref, bq_ref, cq_ref, lam_ref, pw_ref, d_ref, y_ref, cin_ref, s_ref, car_ref):
        @pl.when(pl.program_id(0) == 0)
        def _():
            car_ref[...] = jnp.zeros_like(car_ref)

        cin_ref[...] = jnp.broadcast_to(car_ref[:, 0:1, :], cin_ref.shape)
        u = u_ref[...]
        ub = jnp.dot(pm_ref[...], u.astype(BF16), preferred_element_type=F32).astype(BF16)
        everything = slice(None)
        for q in range(nq):
            _put_slab(s_ref, everything, q,
                      jnp.dot(ub[:, q * SLAB:(q + 1) * SLAB], bq_ref[q], preferred_element_type=F32))
        for q in range(nq):
            _scan_slab(s_ref, 0, q, lam_ref, pw_ref, car_ref, steps, reverse=False)
        y_sm = jnp.concatenate(
            [jnp.dot(_get_slab(s_ref, everything, q).astype(BF16), cq_ref[q], preferred_element_type=F32)
             for q in range(nq)], axis=1)
        y_ref[...] = _permute_rows_f32(pmt_ref[...], y_sm) + d_ref[...] * u

    c3 = lambda i: (0, 0, 0)
    c2 = lambda i: (0, 0)
    return _pallas(
        body, name="ssm_fwd", grid=(t // tt,),
        in_specs=[pl.BlockSpec((tt, dssm), lambda i: (i, u_block)), pl.BlockSpec((tt, tt), c2),
                  pl.BlockSpec((tt, tt), c2), pl.BlockSpec(bq.shape, c3),
                  pl.BlockSpec(cq.shape, c3), pl.BlockSpec(lam.shape, c2), pl.BlockSpec(pw.shape, c2),
                  pl.BlockSpec((1, dssm), c2)],
        out_specs=[pl.BlockSpec((tt, dssm), lambda i: (i, 0)),
                   pl.BlockSpec((None, ntile, SUBLANES, SLAB), lambda i: (i, 0, 0, 0))],
        out_shape=[jax.ShapeDtypeStruct((t, dssm), F32), jax.ShapeDtypeStruct((t // tt, ntile, SUBLANES, SLAB), F32)],
        scratch_shapes=[pltpu.VMEM((ntile, tt, SLAB), F32), pltpu.VMEM((ntile, SUBLANES, SLAB), F32)],
        compiler_params=_cp(2 * (8 * tt * dssm + 4 * nq * ds * w2) + 4 * tt * width + 16 * MIB,
                            dimension_semantics=("arbitrary",)),
    )(proj, pm, pmt, bq, cq, lam, pw, d_skip)


def _ssm_bwd(proj, u_block, dy, cin, dproj, bq, cq, lam, pw, d_skip):
    t = proj.shape[0]
    nq, ds, w2 = bq.shape
    dssm = nq * SLAB
    width = nq * w2
    ntile = width // SLAB
    tt = min(T_SCAN, t)
    nt = t // tt
    steps = tt // SUBLANES
    halo = SUBLANES
    perm = _step_major_perm(tt)
    pm, pmt = perm.astype(BF16), perm.T.astype(BF16)

    def body(u_ref, dy_ref, cin_ref, dp_any, pm_ref, pmt_ref, bq_ref, cq_ref, lam_ref, pw_ref, d_ref,
             du_ref, gb_ref, gc_ref, glam_ref, gd_ref, s_ref, gs_ref, car_f, car_b):
        del dp_any

        @pl.when(pl.program_id(0) == 0)
        def _():
            car_b[...] = jnp.zeros_like(car_b)
            gb_ref[...] = jnp.zeros_like(gb_ref)
            gc_ref[...] = jnp.zeros_like(gc_ref)
            glam_ref[...] = jnp.zeros_like(glam_ref)
            gd_ref[...] = jnp.zeros_like(gd_ref)

        u = u_ref[...]
        dyv = dy_ref[...]
        gd_ref[0:1, :] += jnp.sum(dyv * u, axis=0, keepdims=True)
        pmv = pm_ref[...]
        ub = jnp.dot(pmv, u.astype(BF16), preferred_element_type=F32).astype(BF16)
        dyb = jnp.dot(pmv, dyv.astype(BF16), preferred_element_type=F32).astype(BF16)
        car_f[...] = cin_ref[...]
        data = slice(halo, halo + tt)
        everything = slice(None)
        for q in range(nq):
            _put_slab(s_ref, data, q, jnp.dot(ub[:, q * SLAB:(q + 1) * SLAB], bq_ref[q], preferred_element_type=F32))
        for q in range(nq):
            _scan_slab(s_ref, halo, q, lam_ref, pw_ref, car_f, steps, reverse=False)
        last_step = s_ref[:, halo + tt - SUBLANES:halo + tt, :]
        s_ref[:, 0:halo, :] = jnp.concatenate([cin_ref[:, 0:1, :], last_step[:, 0:SUBLANES - 1, :]], axis=1)
        tn = (((0,), (0,)), ((), ()))
        nt_dims = (((1,), (1,)), ((), ()))
        for q in range(nq):
            sl = slice(q * SLAB, (q + 1) * SLAB)
            gc_ref[q] += lax.dot_general(dyb[:, sl], _get_slab(s_ref, data, q).astype(BF16), tn,
                                         preferred_element_type=F32)
            _put_slab(gs_ref, everything, q,
                      lax.dot_general(dyb[:, sl], cq_ref[q], nt_dims, preferred_element_type=F32))
        for q in range(nq):
            _scan_slab(gs_ref, 0, q, lam_ref, pw_ref, car_b, steps, reverse=True,
                       prev_ref=s_ref, prev_row0=halo, glam_ref=glam_ref)
        du_parts = []
        for q in range(nq):
            sl = slice(q * SLAB, (q + 1) * SLAB)
            gsb = _get_slab(gs_ref, everything, q).astype(BF16)
            du_parts.append(lax.dot_general(gsb, bq_ref[q], nt_dims, preferred_element_type=F32))
            gb_ref[q] += lax.dot_general(ub[:, sl], gsb, tn, preferred_element_type=F32)
        du_sm = jnp.concatenate(du_parts, axis=1)
        du_ref[...] = (_permute_rows_f32(pmt_ref[...], du_sm) + dyv * d_ref[...]).astype(BF16)

    c3 = lambda i: (0, 0, 0)
    c2 = lambda i: (0, 0)
    rev = lambda i: (nt - 1 - i, 0)
    dense = jax.ShapeDtypeStruct((nq, SLAB, w2), F32)
    gp = lam.shape[1]
    return _pallas(
        body, name="ssm_bwd", grid=(nt,),
        in_specs=[pl.BlockSpec((tt, dssm), lambda i: (nt - 1 - i, u_block)), pl.BlockSpec((tt, dssm), rev),
                  pl.BlockSpec((None, ntile, SUBLANES, SLAB), lambda i: (nt - 1 - i, 0, 0, 0)),
                  pl.BlockSpec(memory_space=pl.ANY), pl.BlockSpec((tt, tt), c2), pl.BlockSpec((tt, tt), c2),
                  pl.BlockSpec(bq.shape, c3), pl.BlockSpec(cq.shape, c3),
                  pl.BlockSpec(lam.shape, c2), pl.BlockSpec(pw.shape, c2), pl.BlockSpec((1, dssm), c2)],
        out_specs=[pl.BlockSpec((tt, dssm), lambda i: (nt - 1 - i, u_block)), pl.BlockSpec(dense.shape, c3),
                   pl.BlockSpec(dense.shape, c3), pl.BlockSpec((SUBLANES, gp), c2), pl.BlockSpec((SUBLANES, dssm), c2)],
        out_shape=[jax.ShapeDtypeStruct(dproj.shape, dproj.dtype), dense, dense,
                   jax.ShapeDtypeStruct((SUBLANES, gp), F32), jax.ShapeDtypeStruct((SUBLANES, dssm), F32)],
        scratch_shapes=[pltpu.VMEM((ntile, tt + halo, SLAB), F32), pltpu.VMEM((ntile, tt, SLAB), F32),
                        pltpu.VMEM((ntile, SUBLANES, SLAB), F32), pltpu.VMEM((ntile, SUBLANES, SLAB), F32)],
        input_output_aliases={3: 0},
        compiler_params=_cp(2 * (10 * tt * dssm + 4 * nq * ds * w2 + 8 * nq * SLAB * w2)
                            + 8 * tt * width + 12 * MIB, dimension_semantics=("arbitrary",)),
    )(proj, dy, cin, dproj, pm, pmt, bq, cq, lam, pw, d_skip)


def _gate_fwd(proj, y, conv_w, conv_b, w_glu, b_glu, dc):
    t = proj.shape[0]
    dssm = y.shape[1]
    assert dc == dssm
    tm = min(TM_GATE, t)
    halo = SUBLANES

    rb = 2 * SUBLANES
    assert tm % rb == 0

    def body(b_ref, c_ref, v_ref, zc_ref, zs_ref, y_ref, cw_ref, cb_ref, wg_ref, bg_ref, mix_ref,
             cv_buf, sh1_buf, sh2_buf, ge_buf, geb_buf, gl_buf):
        @pl.when(pl.program_id(0) == 0)
        def _():
            cv_buf[0:halo, :] = jnp.zeros((halo, dc), F32)

        cv_buf[halo:, :] = c_ref[...] * v_ref[...]
        sh1_buf[...] = cv_buf[halo - 1:halo - 1 + tm, :]
        sh2_buf[...] = cv_buf[halo - 2:halo - 2 + tm, :]
        w0, w1, w2, cb = cw_ref[0:1, :], cw_ref[1:2, :], cw_ref[2:3, :], cb_ref[...]
        chunks = [pl.ds(r * rb, rb) for r in range(tm // rb)]
        for r, rs in enumerate(chunks):
            ge, _ = _gelu(y_ref[rs, :])
            ge_buf[rs, :] = ge
            geb_buf[rs, :] = ge.astype(BF16)
        gl_buf[...] = jnp.dot(geb_buf[...], wg_ref[...], preferred_element_type=F32) + bg_ref[...]
        for r, rs in enumerate(chunks):
            conv = cb + w2 * cv_buf[pl.ds(halo + r * rb, rb), :] + w1 * sh1_buf[rs, :] + w0 * sh2_buf[rs, :]
            sz, _ = _silu(zc_ref[rs, :])
            mix_ref[rs, 0:dc] = (b_ref[rs, :] * conv * sz).astype(BF16)
        cv_buf[0:halo, :] = cv_buf[tm:tm + halo, :]
        for r, rs in enumerate(chunks):
            szs, _ = _silu(zs_ref[rs, :])
            mix_ref[rs, dc:] = (ge_buf[rs, :] * jax.nn.sigmoid(gl_buf[rs, :]) * szs).astype(BF16)

    col = lambda j: pl.BlockSpec((tm, dc), lambda i, j=j: (i, j))
    fixed = lambda i: (0, 0)
    return _pallas(
        body, name="gate_fwd", grid=(t // tm,),
        in_specs=[col(0), col(1), col(2), col(3), col(5), pl.BlockSpec((tm, dssm), lambda i: (i, 0)),
                  pl.BlockSpec(conv_w.shape, fixed), pl.BlockSpec((1, dc), fixed),
                  pl.BlockSpec(w_glu.shape, fixed), pl.BlockSpec((1, dssm), fixed)],
        out_specs=pl.BlockSpec((tm, dc + dssm), lambda i: (i, 0)),
        out_shape=jax.ShapeDtypeStruct((t, dc + dssm), BF16),
        scratch_shapes=[pltpu.VMEM((tm + halo, dc), F32), pltpu.VMEM((tm, dc), F32), pltpu.VMEM((tm, dc), F32),
                        pltpu.VMEM((tm, dssm), F32), pltpu.VMEM((tm, dssm), BF16), pltpu.VMEM((tm, dssm), F32)],
        compiler_params=_cp(2 * (6 * 4 * tm * dc + 2 * tm * (dc + dssm) + 2 * dssm * dssm) + 24 * tm * dc + 8 * MIB,
                            dimension_semantics=("arbitrary",)),
    )(proj, proj, proj, proj, proj, y, conv_w, conv_b, w_glu, b_glu)


def _gate_bwd(proj, y, dmix, conv_w, conv_b, w_glu, b_glu, dc):
    t = proj.shape[0]
    dssm = y.shape[1]
    tm = min(TM_GATE, t)
    nt = t // tm
    halo = SUBLANES
    blocks_per_tile = tm // halo

    rb = 2 * SUBLANES
    assert tm % rb == 0
    n_chunk = tm // rb

    def body(b_ref, c_ref, v_ref, zc_ref, zs_ref, cp_ref, vp_ref, y_ref, dm_ref, cw_ref, cb_ref, wg_ref, bg_ref,
             dp_ref, dy_ref, gs_ref, gwg_ref, cv_buf, dc_buf, sh1_buf, sh2_buf, geb_buf, dglb_buf, ge_buf, th_buf):
        i = pl.program_id(0)

        @pl.when(i == 0)
        def _():
            dc_buf[tm:, :] = jnp.zeros((halo, dc), F32)
            gs_ref[...] = jnp.zeros_like(gs_ref)
            gwg_ref[...] = jnp.zeros_like(gwg_ref)

        def rows(r):
            return pl.ds(r * rb, rb)

        def rows_after_halo(r):
            return pl.ds(halo + r * rb, rb)

        def chunks(step, carry):
            for r in range(n_chunk):
                carry = step(r, carry)
            return carry

        def fold(v):
            return v[0:SUBLANES] + v[SUBLANES:rb]

        first_tile = (i == nt - 1)
        cv_buf[0:halo, :] = jnp.where(first_tile, 0.0, cp_ref[...] * vp_ref[...])
        cv_buf[halo:, :] = c_ref[...] * v_ref[...]
        sh1_buf[...] = cv_buf[halo - 1:halo - 1 + tm, :]
        sh2_buf[...] = cv_buf[halo - 2:halo - 2 + tm, :]
        w0, w1, w2, cb = cw_ref[0:1, :], cw_ref[1:2, :], cw_ref[2:3, :], cb_ref[...]

        def conv_pass_a(r, carry):
            rs = rows(r)
            bv, zc, dyc = b_ref[rs, :], zc_ref[rs, :], dm_ref[rs, 0:dc]
            conv = cb + w2 * cv_buf[rows_after_halo(r), :] + w1 * sh1_buf[rs, :] + w0 * sh2_buf[rs, :]
            sz, sgc = _silu(zc)
            dp_ref[rs, 0:dc] = (dyc * conv * sz).astype(BF16)
            dp_ref[rs, 3 * dc:4 * dc] = (dyc * bv * conv * _dsilu(zc, sgc)).astype(BF16)
            dc_buf[rs, :] = dyc * bv * sz
            return carry

        chunks(conv_pass_a, 0)
        sh1_buf[...] = dc_buf[1:1 + tm, :]
        sh2_buf[...] = dc_buf[2:2 + tm, :]

        def conv_pass_b(r, acc):
            rs = rows(r)
            dconv, d1, d2 = dc_buf[rs, :], sh1_buf[rs, :], sh2_buf[rs, :]
            cv = cv_buf[rows_after_halo(r), :]
            dcv = w2 * dconv + w1 * d1 + w0 * d2
            dp_ref[rs, dc:2 * dc] = (dcv * v_ref[rs, :]).astype(BF16)
            dp_ref[rs, 2 * dc:3 * dc] = (dcv * c_ref[rs, :]).astype(BF16)
            return (acc[0] + fold(cv * d2), acc[1] + fold(cv * d1), acc[2] + fold(cv * dconv), acc[3] + fold(dconv))

        zero = jnp.zeros((SUBLANES, dc), F32)
        sums = chunks(conv_pass_b, (zero, zero, zero, zero))
        for k in range(4):
            gs_ref[k:k + 1, :] += jnp.sum(sums[k], axis=0, keepdims=True)
        dc_buf[tm:, :] = dc_buf[0:halo, :]

        def glu_pass_a(r, carry):
            rs = rows(r)
            ge, th = _gelu(y_ref[rs, :])
            geb_buf[rs, :] = ge.astype(BF16)
            ge_buf[rs, :] = ge
            th_buf[rs, :] = th
            return carry

        chunks(glu_pass_a, 0)
        wg = wg_ref[...]
        gl_buf = cv_buf.at[halo:halo + tm]
        keep_buf = dc_buf.at[0:tm]
        gl_buf[...] = jnp.dot(geb_buf[...], wg, preferred_element_type=F32) + bg_ref[...]
        dp_ref[:, 4 * dc:5 * dc] = jnp.zeros((tm, dc), BF16)

        def glu_pass_b(r, acc):
            rs = rows(r)
            ge = ge_buf[rs, :]
            sg = jax.nn.sigmoid(gl_buf[rs, :])
            zs, dys = zs_ref[rs, :], dm_ref[rs, dc:]
            szs, sgs = _silu(zs)
            dp_ref[rs, 5 * dc:] = (dys * ge * sg * _dsilu(zs, sgs)).astype(BF16)
            d_ys = dys * szs
            dgl = d_ys * ge * sg * (1.0 - sg)
            dglb_buf[rs, :] = dgl.astype(BF16)
            keep_buf[rs, :] = d_ys * sg
            return acc + fold(dgl)

        gs_ref[4:5, :] += jnp.sum(chunks(glu_pass_b, zero), axis=0, keepdims=True)
        dglb = dglb_buf[...]
        gwg_ref[...] += lax.dot_general(geb_buf[...], dglb, (((0,), (0,)), ((), ())), preferred_element_type=F32)
        gl_buf[...] = lax.dot_general(dglb, wg, (((1,), (1,)), ((), ())), preferred_element_type=F32)

        def glu_pass_c(r, carry):
            rs = rows(r)
            dy_ref[rs, :] = (keep_buf[rs, :] + gl_buf[rs, :]) * _dgelu(y_ref[rs, :], th_buf[rs, :])
            return carry

        chunks(glu_pass_c, 0)

    col = lambda j: pl.BlockSpec((tm, dc), lambda i, j=j: (nt - 1 - i, j))
    prev = lambda j: pl.BlockSpec((halo, dc), lambda i, j=j: (jnp.maximum((nt - 1 - i) * blocks_per_tile - 1, 0), j))
    rev = lambda i: (nt - 1 - i, 0)
    fixed = lambda i: (0, 0)
    return _pallas(
        body, name="gate_bwd", grid=(nt,),
        in_specs=[col(0), col(1), col(2), col(3), col(5), prev(1), prev(2), pl.BlockSpec((tm, dssm), rev),
                  pl.BlockSpec((tm, dc + dssm), rev), pl.BlockSpec(conv_w.shape, fixed), pl.BlockSpec((1, dc), fixed),
                  pl.BlockSpec(w_glu.shape, fixed), pl.BlockSpec((1, dssm), fixed)],
        out_specs=[pl.BlockSpec((tm, 6 * dc), rev), pl.BlockSpec((tm, dssm), rev),
                   pl.BlockSpec((2 * SUBLANES, dc), fixed), pl.BlockSpec((dssm, dssm), fixed)],
        out_shape=[jax.ShapeDtypeStruct((t, 6 * dc), BF16), jax.ShapeDtypeStruct((t, dssm), F32),
                   jax.ShapeDtypeStruct((2 * SUBLANES, dc), F32), jax.ShapeDtypeStruct((dssm, dssm), F32)],
        scratch_shapes=[pltpu.VMEM((tm + halo, dc), F32), pltpu.VMEM((tm + halo, dc), F32), pltpu.VMEM((tm, dc), F32),
                        pltpu.VMEM((tm, dc), F32), pltpu.VMEM((tm, dssm), BF16), pltpu.VMEM((tm, dssm), BF16),
                        pltpu.VMEM((tm, dssm), F32), pltpu.VMEM((tm, dssm), F32)],
        compiler_params=_cp(2 * (6 * 4 * tm * dc + 8 * tm * dc + 12 * tm * dc + 4 * tm * dc + 6 * dssm * dssm)
                            + 28 * tm * dc + 8 * MIB, dimension_semantics=("arbitrary",)),
    )(proj, proj, proj, proj, proj, proj, proj, y, dmix, conv_w, conv_b, w_glu, b_glu)


def _allgather_flat(v, name):
    r, c = v.shape
    rels = [(dx, dy, dc) for dx in (0, 1) for dy in (0, 1) for dc in (0, 1)][1:]

    def body(v_ref, out_ref, send, recv):
        x, y, cc = _my_place()
        me = 4 * x + 2 * y + cc

        def peer(rel):
            dx, dy, dc = rel
            return (1 - x if dx else x, 1 - y if dy else y, 1 - cc if dc else cc)

        def rc(s, slot, to):
            return pltpu.make_async_remote_copy(src_ref=v_ref, dst_ref=out_ref.at[slot], send_sem=send.at[s],
                                                recv_sem=recv.at[s], device_id=to, device_id_type=MESH)

        sent = []
        for s, rel in enumerate(rels):
            cp = rc(s, me, peer(rel))
            cp.start()
            sent.append(cp)
        out_ref[me] = v_ref[...]
        for s, rel in enumerate(rels):
            px, py, pc = peer(rel)
            rc(s, 4 * px + 2 * py + pc, (px, py, pc)).wait_recv()
        for cp in sent:
            cp.wait_send()

    return _pallas(
        body, name=name, out_shape=jax.ShapeDtypeStruct((N_DEV, r, c), F32),
        in_specs=[pl.BlockSpec(memory_space=pltpu.VMEM)], out_specs=pl.BlockSpec(memory_space=pltpu.VMEM),
        scratch_shapes=[pltpu.SemaphoreType.DMA((N_DEV - 1,)), pltpu.SemaphoreType.DMA((N_DEV - 1,))],
        compiler_params=_cp((N_DEV + 2) * 4 * r * c + 8 * MIB),
    )(v)


def _rs_pair_exchange(grads, name):
    n = len(grads)

    def body(*refs):
        ins, outs = refs[:n], refs[n:2 * n]
        send, recv = refs[2 * n:]
        x, y, c = _my_place()
        cps = []
        for t in range(n):
            cp = pltpu.make_async_remote_copy(src_ref=ins[t].at[:, 1 - c], dst_ref=outs[t], send_sem=send.at[t],
                                              recv_sem=recv.at[t], device_id=(x, y, 1 - c), device_id_type=MESH)
            cp.start()
            cps.append(cp)
        for cp in cps:
            cp.wait()

    any_spec = pl.BlockSpec(memory_space=pl.ANY)
    return _pallas(
        body, name=name, in_specs=[any_spec] * n, out_specs=[any_spec] * n,
        out_shape=[jax.ShapeDtypeStruct((g.shape[0],) + g.shape[2:], g.dtype) for g in grads],
        scratch_shapes=[pltpu.SemaphoreType.DMA((n,)), pltpu.SemaphoreType.DMA((n,))],
        compiler_params=_cp(16 * MIB),
    )(*grads)


def _rs_pair_add(place, grad, got, name):
    nk, _, r2, c = grad.shape
    tr = min(TR_ELT, r2)
    other = lambda s, p: (p[1] + 1 + s) % nk

    def send_body(place_ref, g_ref, r_ref, o_ref):
        del place_ref
        o_ref[...] = (g_ref[...] + r_ref[...].astype(F32)).astype(BF16)

    to_send = _pallas(
        send_body, name=name + "_send",
        grid_spec=pltpu.PrefetchScalarGridSpec(
            num_scalar_prefetch=1, grid=(nk - 1, r2 // tr),
            in_specs=[pl.BlockSpec((None, None, tr, c), lambda s, i, p: (other(s, p), p[0], i, 0)),
                      pl.BlockSpec((None, tr, c), lambda s, i, p: (other(s, p), i, 0))],
            out_specs=pl.BlockSpec((None, tr, c), lambda s, i, p: (other(s, p), i, 0))),
        out_shape=jax.ShapeDtypeStruct((nk, r2, c), BF16),
        compiler_params=_cp(2 * 8 * tr * c + 8 * MIB),
    )(place, grad, got)

    def keep_body(place_ref, g_ref, r_ref, o_ref):
        del place_ref
        o_ref[...] = g_ref[...] + r_ref[...].astype(F32)

    to_keep = _pallas(
        keep_body, name=name + "_keep",
        grid_spec=pltpu.PrefetchScalarGridSpec(
            num_scalar_prefetch=1, grid=(r2 // tr,),
            in_specs=[pl.BlockSpec((None, None, tr, c), lambda i, p: (p[1], p[0], i, 0)),
                      pl.BlockSpec((None, tr, c), lambda i, p: (p[1], i, 0))],
            out_specs=pl.BlockSpec((tr, c), lambda i, p: (i, 0))),
        out_shape=jax.ShapeDtypeStruct((r2, c), F32),
        compiler_params=_cp(2 * 10 * tr * c + 8 * MIB),
    )(place, grad, got)
    return to_send, to_keep


_HBM_SPEC = pl.BlockSpec(memory_space=pltpu.HBM)
_SEM_SPEC = pl.BlockSpec(memory_space=pltpu.SEMAPHORE)
_DATAFLOW = pltpu.SideEffectType.DATAFLOW_SIDE_EFFECTING


def _split_copy_start(srcs, land_shapes, plan, n_sems, name, after=()):
    ns, nl, na = len(srcs), len(land_shapes), len(after)

    def body(*refs):
        src_refs, land_refs = refs[:ns], refs[ns:ns + nl]
        send, recv = refs[ns + nl + na], refs[ns + nl + na + 1]
        token = refs[-1]
        sends, _ = plan(src_refs, land_refs)
        for src, dst, to, si, ri in sends:
            pltpu.make_async_remote_copy(src_ref=src, dst_ref=dst, send_sem=send.at[si], recv_sem=recv.at[ri],
                                         device_id=to, device_id_type=MESH).start()
        token[...] = jnp.zeros_like(token)

    lands = [lax.empty(shp, dt) for shp, dt in land_shapes]
    through = [pltpu.HBM(a.shape, a.dtype) for a in srcs] + [pltpu.HBM(shp, dt) for shp, dt in land_shapes]
    out = pl.pallas_call(
        body, name=name,
        out_shape=(pltpu.SemaphoreType.DMA((n_sems,)), pltpu.SemaphoreType.DMA((n_sems,)), *through,
                   jax.ShapeDtypeStruct((SUBLANES, SLAB), F32)),
        in_specs=[_HBM_SPEC] * (ns + nl) + [pl.BlockSpec(memory_space=pl.ANY)] * na,
        out_specs=(_SEM_SPEC, _SEM_SPEC, *([_HBM_SPEC] * (ns + nl)), pl.BlockSpec(memory_space=pltpu.VMEM)),
        input_output_aliases={i: 2 + i for i in range(ns + nl)},
        compiler_params=pltpu.CompilerParams(has_side_effects=_DATAFLOW),
    )(*[pltpu.with_memory_space_constraint(a, pltpu.HBM) for a in (*srcs, *lands)], *after)
    return out[0], out[1], list(out[2:2 + ns]), list(out[2 + ns:2 + ns + nl]), out[-1]


def _split_copy_wait(send, recv, srcs, lands, plan, after, name):
    ns, nl, na = len(srcs), len(lands), len(after)

    def body(*refs):
        src_refs, land_refs = refs[:ns], refs[ns:ns + nl]
        send_ref, recv_ref = refs[ns + nl], refs[ns + nl + 1]
        sends, arrivals = plan(src_refs, land_refs)
        for src, dst, to, si, ri in sends:
            pltpu.make_async_remote_copy(src_ref=src, dst_ref=dst, send_sem=send_ref.at[si], recv_sem=recv_ref.at[ri],
                                         device_id=to, device_id_type=MESH).wait_send()
        for (src, _, to, si, _), (view, ri) in zip(sends, arrivals):
            pltpu.make_async_remote_copy(src_ref=view, dst_ref=view, send_sem=send_ref.at[si], recv_sem=recv_ref.at[ri],
                                         device_id=to, device_id_type=MESH).wait_recv()

    out = pl.pallas_call(
        body, name=name,
        out_shape=[pltpu.HBM(a.shape, a.dtype) for a in (*srcs, *lands)],
        in_specs=[_HBM_SPEC] * (ns + nl) + [_SEM_SPEC, _SEM_SPEC] + [pl.BlockSpec(memory_space=pl.ANY)] * na,
        out_specs=[_HBM_SPEC] * (ns + nl),
        input_output_aliases={i: i for i in range(ns + nl)},
        compiler_params=pltpu.CompilerParams(has_side_effects=_DATAFLOW),
    )(*srcs, *lands, send, recv, *after)
    return list(out[:ns]), list(out[ns:])


def _chip_exchange_plan(n):
    per = N_CHIPS - 1

    def plan(srcs, lands):
        x, y, c = _my_place()
        sends, arrivals = [], []
        for t in range(n):
            for j, (cx, cy) in enumerate(_other_chips(x, y)):
                sends.append((srcs[t].at[2 * cx + cy], lands[t].at[j], (cx, cy, c), t * per + j, t * per + j))
                arrivals.append((lands[t].at[j], t * per + j))
        return sends, arrivals

    return plan


def _gather_half_plan(n):
    per = N_CHIPS - 1

    def plan(srcs, lands):
        x, y, c = _my_place()
        k = 2 * x + y
        sends, arrivals = [], []
        for t in range(n):
            for j, (cx, cy) in enumerate(_other_chips(x, y)):
                sends.append((srcs[t].at[c], lands[t].at[k, c], (cx, cy, c), t * per + j, t * per + j))
                arrivals.append((lands[t].at[2 * cx + cy, c], t * per + j))
        return sends, arrivals

    return plan


def _gather_complete(own, landed, name):
    def body(own_ref, in_ref, out_ref, send, recv, local):
        x, y, c = _my_place()
        sib = (x, y, 1 - c)
        mine = pltpu.make_async_copy(own_ref, out_ref.at[2 * x + y], local)
        mine.start()
        sent = []
        for j, (cx, cy) in enumerate(_other_chips(x, y)):
            half = (2 * cx + cy, c)
            cp = pltpu.make_async_remote_copy(src_ref=in_ref.at[half], dst_ref=out_ref.at[half], send_sem=send.at[j],
                                              recv_sem=recv.at[j], device_id=sib, device_id_type=MESH)
            cp.start()
            sent.append(cp)
        for j, (cx, cy) in enumerate(_other_chips(x, y)):
            other = out_ref.at[2 * cx + cy, 1 - c]
            pltpu.make_async_remote_copy(src_ref=other, dst_ref=other, send_sem=send.at[j], recv_sem=recv.at[j],
                                         device_id=sib, device_id_type=MESH).wait_recv()
        for cp in sent:
            cp.wait_send()
        mine.wait()

    any_spec = pl.BlockSpec(memory_space=pl.ANY)
    per = N_CHIPS - 1
    return _pallas(
        body, name=name, in_specs=[any_spec, any_spec], out_specs=any_spec,
        out_shape=jax.ShapeDtypeStruct(landed.shape, landed.dtype), input_output_aliases={1: 0},
        scratch_shapes=[pltpu.SemaphoreType.DMA((per,)), pltpu.SemaphoreType.DMA((per,)), pltpu.SemaphoreType.DMA],
        compiler_params=_cp(16 * MIB),
    )(own, landed)


def _gather_direct_plan(n):
    per = 2 * (N_CHIPS - 1)

    def plan(srcs, lands):
        x, y, c = _my_place()
        k = 2 * x + y
        sends, arrivals = [], []
        for t in range(n):
            for j, (cx, cy) in enumerate(_other_chips(x, y)):
                for core in (0, 1):
                    sends.append((srcs[t].at[c], lands[t].at[k, c], (cx, cy, core),
                                  t * per + 2 * j + core, t * per + 2 * j + c))
                    arrivals.append((lands[t].at[2 * cx + cy, core], t * per + 2 * j + core))
        return sends, arrivals

    return plan


def _rs_chip_add(kept, got, name):
    r2, c = kept.shape
    tr = min(TR_ELT, r2)

    def body(s_ref, q_ref, o_ref):
        o_ref[...] = ((s_ref[...] + q_ref[0].astype(F32)) + q_ref[1].astype(F32)) + q_ref[2].astype(F32)

    blk = pl.BlockSpec((tr, c), lambda i: (i, 0))
    return _pallas(
        body, name=name, grid=(r2 // tr,),
        in_specs=[blk, pl.BlockSpec((N_CHIPS - 1, tr, c), lambda i: (0, i, 0))], out_specs=blk,
        out_shape=jax.ShapeDtypeStruct((r2, c), F32),
        compiler_params=_cp(2 * 14 * tr * c + 8 * MIB),
    )(kept, got)


def _rs_pair_share(halves, name):
    n = len(halves)

    def body(*refs):
        ins, outs = refs[:n], refs[n:2 * n]
        send, recv, local = refs[2 * n:]
        x, y, c = _my_place()
        cps, mine = [], []
        for t in range(n):
            lc = pltpu.make_async_copy(ins[t], outs[t].at[c], local.at[t])
            lc.start()
            mine.append(lc)
            cp = pltpu.make_async_remote_copy(src_ref=ins[t], dst_ref=outs[t].at[c], send_sem=send.at[t],
                                              recv_sem=recv.at[t], device_id=(x, y, 1 - c), device_id_type=MESH)
            cp.start()
            cps.append(cp)
        for t in range(n):
            other = outs[t].at[1 - c]
            pltpu.make_async_remote_copy(src_ref=other, dst_ref=other, send_sem=send.at[t], recv_sem=recv.at[t],
                                         device_id=(x, y, 1 - c), device_id_type=MESH).wait_recv()
        for cp in cps:
            cp.wait_send()
        for lc in mine:
            lc.wait()

    any_spec = pl.BlockSpec(memory_space=pl.ANY)
    return _pallas(
        body, name=name, in_specs=[any_spec] * n, out_specs=[any_spec] * n,
        out_shape=[jax.ShapeDtypeStruct((2,) + h.shape, F32) for h in halves],
        scratch_shapes=[pltpu.SemaphoreType.DMA((n,)), pltpu.SemaphoreType.DMA((n,)), pltpu.SemaphoreType.DMA((n,))],
        compiler_params=_cp(16 * MIB),
    )(*halves)


_PACK_TILE = SUBLANES * SLAB


def _pack(arrays):
    rows = []
    for a in arrays:
        flat = a.reshape(-1).astype(F32)
        padded = -(-flat.shape[0] // _PACK_TILE) * _PACK_TILE
        rows.append(jnp.pad(flat, (0, padded - flat.shape[0])).reshape(-1, SLAB))
    n_rows = sum(r.shape[0] for r in rows)
    if n_rows % (2 * SUBLANES):
        rows.append(jnp.zeros((SUBLANES, SLAB), F32))
    return jnp.concatenate(rows, axis=0)


def _unpack(packed, shapes):
    out, row = [], 0
    for shp in shapes:
        size = math.prod(shp)
        nrow = -(-size // _PACK_TILE) * SUBLANES
        out.append(packed[row:row + nrow].reshape(-1)[:size].reshape(shp))
        row += nrow
    return out


def kernel(x, norm_pre_g, w_in, conv_w, conv_b, ssm_a_re, ssm_a_im, ssm_log_dt, ssm_b_re, ssm_b_im, ssm_c_re, ssm_c_im, ssm_d, w_glu, b_glu, w_out, norm_post_g, loss_target, m_norm_pre_g, m_w_in, m_conv_w, m_conv_b, m_ssm_a_re, m_ssm_a_im, m_ssm_log_dt, m_ssm_b_re, m_ssm_b_im, m_ssm_c_re, m_ssm_c_im, m_ssm_d, m_w_glu, m_b_glu, m_w_out, m_norm_post_g, v_norm_pre_g, v_w_in, v_conv_w, v_conv_b, v_ssm_a_re, v_ssm_a_im, v_ssm_log_dt, v_ssm_b_re, v_ssm_b_im, v_ssm_c_re, v_ssm_c_im, v_ssm_d, v_w_glu, v_b_glu, v_w_out, v_norm_post_g):
    xs, tgt = x[0], loss_target[0]
    t, d = xs.shape
    dc = conv_b.shape[0]
    dssm = ssm_d.shape[0]
    g, p = ssm_a_re.shape
    h = SSM_H
    nq = dssm // SLAB
    n_shard = w_in.shape[1]
    steps = min(T_SCAN, t) // SUBLANES
    mx, my, mc = _my_place()
    chip = 2 * mx + my
    place = jnp.stack([mc, chip]).astype(jnp.int32)

    cw_cols = conv_w.shape[1]
    cw_pad = -(-cw_cols // SLAB) * SLAB
    cw_blk = jnp.zeros((SUBLANES, cw_pad), F32).at[:conv_w.shape[0], :cw_cols].set(conv_w)
    cw_all = _allgather_flat(cw_blk, "allgather_conv_w")
    conv_w_full = jnp.concatenate([cw_all[2 * k, :, :cw_cols] for k in range(N_CHIPS)], axis=1)

    halves = lambda a: a.reshape(2, a.shape[0] // 2, a.shape[1])
    win_half = halves(_cast_bf16(w_in, "cast_w_in", cw_all[0, :, :SLAB]))
    win_plan = _gather_half_plan(1)
    win_send, win_recv, win_srcs, win_lands, win_token = _split_copy_start(
        [win_half], [((N_CHIPS,) + win_half.shape, BF16)], win_plan, N_CHIPS - 1, "gather_w_in_start")
    behind_w_in = win_token[0:1, 0:1]

    small_names = ["norm_pre_g", "conv_w", "conv_b", "ssm_a_re", "ssm_a_im", "ssm_log_dt", "ssm_b_re", "ssm_b_im",
                   "ssm_c_re", "ssm_c_im", "ssm_d", "b_glu", "norm_post_g", "loss"]
    zeros_cw = jnp.zeros((conv_w.shape[0], dc), F32)
    one0 = jnp.zeros((1,), F32)
    hp = lambda b: b.transpose(0, 2, 1)
    small_w = dict(norm_pre_g=norm_pre_g, conv_w=zeros_cw, conv_b=conv_b, ssm_a_re=ssm_a_re, ssm_a_im=ssm_a_im,
                   ssm_log_dt=ssm_log_dt, ssm_b_re=hp(ssm_b_re), ssm_b_im=hp(ssm_b_im), ssm_c_re=ssm_c_re, ssm_c_im=ssm_c_im,
                   ssm_d=ssm_d, b_glu=b_glu, norm_post_g=norm_post_g, loss=one0)
    small_m = dict(norm_pre_g=m_norm_pre_g, conv_w=zeros_cw, conv_b=m_conv_b, ssm_a_re=m_ssm_a_re, ssm_a_im=m_ssm_a_im,
                   ssm_log_dt=m_ssm_log_dt, ssm_b_re=hp(m_ssm_b_re), ssm_b_im=hp(m_ssm_b_im), ssm_c_re=m_ssm_c_re,
                   ssm_c_im=m_ssm_c_im, ssm_d=m_ssm_d, b_glu=m_b_glu, norm_post_g=m_norm_post_g, loss=one0)
    small_v = dict(norm_pre_g=v_norm_pre_g, conv_w=zeros_cw, conv_b=v_conv_b, ssm_a_re=v_ssm_a_re, ssm_a_im=v_ssm_a_im,
                   ssm_log_dt=v_ssm_log_dt, ssm_b_re=hp(v_ssm_b_re), ssm_b_im=hp(v_ssm_b_im), ssm_c_re=v_ssm_c_re,
                   ssm_c_im=v_ssm_c_im, ssm_d=v_ssm_d, b_glu=v_b_glu, norm_post_g=v_norm_post_g, loss=one0)
    w_pack, m_pack, v_pack = [_pack([group[nm] for nm in small_names]) + behind_w_in
                              for group in (small_w, small_m, small_v)]
    side = [halves(_cast_bf16(w_out, "cast_w_out", win_token)), halves(_cast_bf16(w_glu, "cast_w_glu", win_token))]

    expand = jnp.tile(jnp.eye(p, dtype=F32), (1, h))
    b_re2, b_im2 = hp(ssm_b_re).reshape(g, h * p), hp(ssm_b_im).reshape(g, h * p)
    log_dt2 = ssm_log_dt.reshape(g, 1) + behind_w_in
    lbr, lbi, bbr2, bbi2, pw3 = _zoh_fwd(ssm_a_re, ssm_a_im, log_dt2, b_re2, b_im2, expand, steps)
    lam = jnp.stack([lbr.reshape(g * p), lbi.reshape(g * p)])
    pw = pw3.reshape(2, g * p)
    to_slab_b = lambda b2: _blockdiag(b2.reshape(nq, GROUPS_PER_SLAB, h, p))
    bq = jnp.concatenate([to_slab_b(bbr2), to_slab_b(bbi2)], axis=2).astype(BF16)
    to_slab_c = lambda c3: _blockdiag(c3.reshape(nq, GROUPS_PER_SLAB, h, p).transpose(0, 1, 3, 2))
    cq = jnp.concatenate([to_slab_c(ssm_c_re), to_slab_c(-ssm_c_im)], axis=1).astype(BF16)

    g_pre2, g_post2 = norm_pre_g.reshape(1, d), norm_post_g.reshape(1, d)
    conv_b2, b_glu2, d_skip2 = conv_b.reshape(1, dc), b_glu.reshape(1, dssm), ssm_d.reshape(1, dssm)
    u_block = 4 * dc // dssm

    hb = _prenorm(xs, g_pre2 + behind_w_in)
    proj_own = _inproj_own(place, hb, win_srcs[0].reshape(d, n_shard))
    win_own, win_landed = _split_copy_wait(win_send, win_recv, win_srcs, win_lands, win_plan,
                                           [proj_own, bq, cq, w_pack, m_pack, v_pack, *side], "gather_w_in_wait")
    win_g = _gather_complete(win_own[0], win_landed[0], "gather_w_in_complete")
    win_b = win_g.reshape(N_CHIPS, d, n_shard)
    side_plan = _gather_direct_plan(len(side))
    side_sems = 2 * (N_CHIPS - 1) * len(side)
    ag_send, ag_recv, ag_srcs, ag_lands, ag_token = _split_copy_start(
        side, [((N_CHIPS,) + a.shape, BF16) for a in side], side_plan, side_sems, "gather_side_weights_start",
        after=[win_g])
    proj = _inproj_rest(place, hb, win_b, proj_own, ag_token)
    y, cin = _ssm_fwd(proj, u_block, bq, cq, lam, pw, d_skip2)
    side_own, side_all = _split_copy_wait(ag_send, ag_recv, ag_srcs, ag_lands, side_plan, [cin],
                                          "gather_side_weights_wait")
    wout_g, wglu_g = [lax.dynamic_update_index_in_dim(all_, own, chip, 0) for own, all_ in zip(side_own, side_all)]
    wout_b = wout_g.reshape(dc + dssm, d)
    wglu_b = wglu_g.reshape(dssm, dssm)
    mix = _gate_fwd(proj, y, conv_w_full, conv_b2, wglu_b, b_glu2, dc)
    loss_blk, dout, dob, dmix, gg_post = _outproj(mix, wout_b, xs, tgt, g_post2)

    def reduce_start(grads, to_send, tags, group):
        got = _rs_pair_exchange(to_send, "rs_pair_exchange_" + group)
        sums = [_rs_pair_add(place, gt, rt, "rs_pair_add_" + tg) for gt, rt, tg in zip(grads, got, tags)]
        plan = _chip_exchange_plan(len(grads))
        started = _split_copy_start([s16 for s16, _ in sums],
                                    [((N_CHIPS - 1,) + s16.shape[1:], BF16) for s16, _ in sums], plan,
                                    (N_CHIPS - 1) * len(grads), "rs_chip_exchange_" + group + "_start")
        return plan, started, [s32 for _, s32 in sums]

    def reduce_finish(plan, started, sums32, tags, group, after):
        send, recv, srcs, lands, _ = started
        _, landed = _split_copy_wait(send, recv, srcs, lands, plan, after, "rs_chip_exchange_" + group + "_wait")
        mine = [_rs_chip_add(s32, qt, "rs_chip_add_" + tg) for s32, qt, tg in zip(sums32, landed, tags)]
        full = _rs_pair_share(mine, "rs_pair_share_" + group)
        return [f.reshape(2 * f.shape[1], f.shape[2]) for f in full]

    gw_out, gw_out16 = _matmul_tn(mix, dob, 1, "grad_w_out")
    dproj, dy, gsmall, gw_glu = _gate_bwd(proj, y, dmix, conv_w_full, conv_b2, wglu_b, b_glu2, dc)
    as_out = lambda a: a.reshape(N_CHIPS, 2, (dc + dssm) // (2 * N_CHIPS), d)
    gw_glu4 = gw_glu.reshape(N_CHIPS, 2, dssm // (2 * N_CHIPS), dssm)
    rs_a = reduce_start([as_out(gw_out), gw_glu4], [as_out(gw_out16), gw_glu4], ["w_out", "w_glu"], "a")
    dproj, gb_dense, gc_dense, glam, gd = _ssm_bwd(proj, u_block, dy, cin, dproj, bq, cq, lam, pw,
                                                   d_skip2 + rs_a[1][4][0:1, 0:1])
    gw_in, gw_in16 = _matmul_tn(hb, dproj, N_CHIPS, "grad_w_in")
    as_in = lambda a: a.reshape(N_CHIPS, 2, d // 2, n_shard)
    rs_b = reduce_start([as_in(gw_in)], [as_in(gw_in16)], ["w_in"], "b")
    gx, gg_pre = _dh_prenorm_bwd(dproj, win_b, xs, dout, g_pre2 + rs_b[1][4][0:1, 0:1])

    gb_blocks = _blockdiag_take(gb_dense, h, p)
    g_bbr2 = gb_blocks[:, :, :, 0, :].reshape(g, h * p)
    g_bbi2 = gb_blocks[:, :, :, 1, :].reshape(g, h * p)
    gc_blocks = _blockdiag_take(gc_dense, h, p)
    g_c_re = gc_blocks[:, :, :, 0, :].reshape(g, h, p)
    g_c_im = -gc_blocks[:, :, :, 1, :].reshape(g, h, p)
    g_a_re, g_a_im, g_ld, g_b_re2, g_b_im2 = _zoh_bwd(
        ssm_a_re, ssm_a_im, log_dt2, b_re2, b_im2, expand,
        glam[0].reshape(g, p), glam[1].reshape(g, p), g_bbr2, g_bbi2)

    small_g = {
        "norm_pre_g": gg_pre[0], "conv_w": gsmall[0:3], "conv_b": gsmall[3], "ssm_a_re": g_a_re, "ssm_a_im": g_a_im,
        "ssm_log_dt": g_ld.reshape(g), "ssm_b_re": g_b_re2.reshape(g, h, p), "ssm_b_im": g_b_im2.reshape(g, h, p),
        "ssm_c_re": g_c_re, "ssm_c_im": g_c_im, "ssm_d": gd[0], "b_glu": gsmall[4], "norm_post_g": gg_post[0],
        "loss": loss_blk[0, 0:1],
    }
    shapes = [small_w[nm].shape for nm in small_names]
    g_pack = _allreduce_small(_pack([small_g[nm] for nm in small_names]))
    packs = _adamw_small(g_pack, w_pack, m_pack, v_pack)
    sg, sd, sm, sv = [dict(zip(small_names, _unpack(pk, shapes))) for pk in (g_pack, *packs)]
    for group in (sg, sd, sm, sv):
        group.update(ssm_b_re=hp(group["ssm_b_re"]), ssm_b_im=hp(group["ssm_b_im"]))
    loss = sg["loss"][0]

    g_cw = lax.dynamic_slice_in_dim(sg["conv_w"], chip * cw_cols, cw_cols, axis=1)
    pad_cw = lambda a: jnp.zeros((SUBLANES, cw_pad), F32).at[:a.shape[0], :cw_cols].set(a)
    cut_cw = lambda a: a[:conv_w.shape[0], :cw_cols]
    _, d_cw, m_cw, v_cw = [cut_cw(a) for a in _adamw(pad_cw(conv_w), pad_cw(g_cw), pad_cw(m_conv_w), pad_cw(v_conv_w),
                                                     "adamw_conv_w")]

    g_wout, g_wglu = reduce_finish(*rs_a, ["w_out", "w_glu"], "a", [d_cw, packs[0]])
    (g_win,) = reduce_finish(*rs_b, ["w_in"], "b", [g_wout])
    g_win, d_win, m_win, v_win = _adamw(w_in, g_win, m_w_in, v_w_in, "adamw_w_in")
    g_wout, d_wout, m_wout, v_wout = _adamw(w_out, g_wout, m_w_out, v_w_out, "adamw_w_out")
    g_wglu, d_wglu, m_wglu, v_wglu = _adamw(w_glu, g_wglu, m_w_glu, v_w_glu, "adamw_w_glu")

    order = ["norm_pre_g", "w_in", "conv_w", "conv_b", "ssm_a_re", "ssm_a_im", "ssm_log_dt", "ssm_b_re", "ssm_b_im",
             "ssm_c_re", "ssm_c_im", "ssm_d", "w_glu", "b_glu", "w_out", "norm_post_g"]
    grads, deltas, new_m, new_v = dict(sg), dict(sd), dict(sm), dict(sv)
    grads.update(w_in=g_win, w_out=g_wout, w_glu=g_wglu, conv_w=g_cw)
    deltas.update(w_in=d_win, w_out=d_wout, w_glu=d_wglu, conv_w=d_cw)
    new_m.update(w_in=m_win, w_out=m_wout, w_glu=m_wglu, conv_w=m_cw)
    new_v.update(w_in=v_win, w_out=v_wout, w_glu=v_wglu, conv_w=v_cw)
    return (loss, gx[None], *[grads[nm] for nm in order], *[deltas[nm] for nm in order],
            *[new_m[nm] for nm in order], *[new_v[nm] for nm in order])
```
